```python
import jax, jax.numpy as jnp
from jax import lax
import numpy as np

D_MODEL = 1024
BATCH = 8
SEQ = 4096
DEPTH = 4

HEAD_DIM = 64
MIX_WIDTH = D_MODEL
MLA_HEADS = MIX_WIDTH // 2 // HEAD_DIM
MLA_WIDTH = MLA_HEADS * HEAD_DIM
NOPE_DIM = HEAD_DIM
ROPE_DIM = HEAD_DIM // 2
V_DIM = HEAD_DIM
Q_RANK = 3 * D_MODEL // 8
KV_RANK = 4 * HEAD_DIM
ROPE_THETA = 10000.0
Q_BLOCK = 128
SG_GROUPS = MIX_WIDTH // 4 // HEAD_DIM
SG_WIDTH = SG_GROUPS * HEAD_DIM
CHUNK = 128
CV_GROUPS = MIX_WIDTH // 4 // HEAD_DIM
CV_WIDTH = CV_GROUPS * HEAD_DIM
CONV_WIDTH = 3
D_FF = ((8 * D_MODEL + 3 * 256 - 1) // (3 * 256)) * 256
EPS = 1e-6
OFF_CQ = 0
OFF_CKV = OFF_CQ + Q_RANK
OFF_KR = OFF_CKV + KV_RANK
OFF_SG = OFF_KR + ROPE_DIM
OFF_CV = OFF_SG + 2 * SG_WIDTH
IN_WIDTH = OFF_CV + 3 * CV_WIDTH

kernel_name = "hybrid_mla_sgu_shortconv_sandwich"


def rms_norm(x, g):
    xf = x.astype(jnp.float32)
    y = xf * lax.rsqrt(jnp.mean(xf * xf, axis=-1, keepdims=True) + EPS)
    return y.astype(x.dtype) * g


def group_layer_norm(x, g, b, groups):
    shp = x.shape
    xf = x.astype(jnp.float32).reshape(shp[:-1] + (groups, shp[-1] // groups))
    mu = jnp.mean(xf, axis=-1, keepdims=True)
    var = jnp.mean(jnp.square(xf - mu), axis=-1, keepdims=True)
    y = ((xf - mu) * lax.rsqrt(var + EPS)).reshape(shp)
    return y.astype(x.dtype) * g + b


def rope_tables(positions):
    inv_freq = 1.0 / (ROPE_THETA ** (jnp.arange(0, ROPE_DIM // 2, dtype=jnp.float32) / (ROPE_DIM // 2)))
    ang = positions.astype(jnp.float32)[..., None] * inv_freq
    return jnp.cos(ang), jnp.sin(ang)


def apply_rope(t, cos, sin):
    tf = t.astype(jnp.float32)
    t1, t2 = jnp.split(tf, 2, axis=-1)
    return jnp.concatenate([t1 * cos - t2 * sin, t2 * cos + t1 * sin], axis=-1).astype(t.dtype)


def causal_latent_attention(q_nope, q_rope, k_nope, k_rope, v):
    b, s, h, _ = q_nope.shape
    nb = s // Q_BLOCK
    scale = (NOPE_DIM + ROPE_DIM) ** -0.5
    kpos = jnp.arange(s)

    def to_blocks(t):
        return jnp.moveaxis(t.reshape((b, nb, Q_BLOCK) + t.shape[2:]), 1, 0)

    def one_block(args):
        qn, qr, i = args
        sc = (jnp.einsum('bqhd,bkhd->bhqk', qn, k_nope)
              + jnp.einsum('bqhd,bkd->bhqk', qr, k_rope)).astype(jnp.float32) * scale
        qpos = i * Q_BLOCK + jnp.arange(Q_BLOCK)
        mask = kpos[None, :] <= qpos[:, None]
        sc = jnp.where(mask, sc, jnp.finfo(jnp.float32).min)
        p = jax.nn.softmax(sc, axis=-1).astype(v.dtype)
        return jnp.einsum('bhqk,bkhd->bqhd', p, v)

    out = lax.map(one_block, (to_blocks(q_nope), to_blocks(q_rope), jnp.arange(nb)))
    return jnp.moveaxis(out, 0, 1).reshape(b, s, h * V_DIM)


def mla_branch(z, cos, sin, q_norm_g, w_uq, kv_norm_g, w_ukv):
    b, s, _ = z.shape
    c_q = rms_norm(z[..., OFF_CQ:OFF_CKV], q_norm_g)
    q = (c_q @ w_uq).reshape(b, s, MLA_HEADS, NOPE_DIM + ROPE_DIM)
    q_nope = q[..., :NOPE_DIM]
    q_rope = apply_rope(q[..., NOPE_DIM:], cos[:, :, None, :], sin[:, :, None, :])
    c_kv = rms_norm(z[..., OFF_CKV:OFF_KR], kv_norm_g)
    kv = (c_kv @ w_ukv).reshape(b, s, MLA_HEADS, NOPE_DIM + V_DIM)
    k_nope, v = kv[..., :NOPE_DIM], kv[..., NOPE_DIM:]
    k_rope = apply_rope(z[..., OFF_KR:OFF_SG], cos, sin)
    return causal_latent_attention(q_nope, q_rope, k_nope, k_rope, v)


def sgu_branch(z, sg_ln_g, sg_ln_b, w_sp, b_sp):
    b, s, _ = z.shape
    uv = jax.nn.gelu(z[..., OFF_SG:OFF_CV])
    u, v = uv[..., :SG_WIDTH], uv[..., SG_WIDTH:]
    v = group_layer_norm(v, sg_ln_g, sg_ln_b, SG_GROUPS)
    vc = v.reshape(b, s // CHUNK, CHUNK, SG_GROUPS, HEAD_DIM)
    w_causal = w_sp * jnp.tril(jnp.ones((CHUNK, CHUNK), w_sp.dtype))
    mixed = jnp.einsum('gts,bcsge->bctge', w_causal, vc) + jnp.swapaxes(b_sp, 0, 1)[:, :, None]
    return u * mixed.reshape(b, s, SG_WIDTH)


def conv_branch(z, conv_w):
    gate_b = z[..., OFF_CV:OFF_CV + CV_WIDTH]
    gate_c = z[..., OFF_CV + CV_WIDTH:OFF_CV + 2 * CV_WIDTH]
    h = z[..., OFF_CV + 2 * CV_WIDTH:IN_WIDTH]
    y = gate_c * h
    yp = jnp.pad(y, ((0, 0), (CONV_WIDTH - 1, 0), (0, 0)))
    s = y.shape[1]
    conv = yp[:, 0:s] * conv_w[0] + yp[:, 1:s + 1] * conv_w[1] + yp[:, 2:s + 2] * conv_w[2]
    return gate_b * conv


def _fwd_setup_inputs(seed: int = 0) -> dict:
    key = jax.random.key(seed)
    ks = jax.random.split(key, 24)
    L, D = DEPTH, D_MODEL

    def nrm(k, shape, fan_in):
        return jax.random.normal(k, shape, jnp.float32) * fan_in ** -0.5

    def gain(k, shape):
        return 1.0 + 0.05 * jax.random.normal(k, shape, jnp.float32)

    x = jax.random.normal(ks[0], (BATCH, SEQ, D), jnp.float32)
    offsets = jax.random.randint(ks[1], (BATCH, 1), 0, 1024, dtype=jnp.int32)
    positions = (offsets + jnp.arange(SEQ, dtype=jnp.int32)[None, :]).astype(jnp.int32)
    return {
        "x": x,
        "positions": positions,
        "mix_pre_g": gain(ks[2], (L, D)),
        "mix_post_g": gain(ks[3], (L, D)),
        "ffn_pre_g": gain(ks[4], (L, D)),
        "ffn_post_g": gain(ks[5], (L, D)),
        "w_in": nrm(ks[6], (L, D, IN_WIDTH), D),
        "q_norm_g": gain(ks[7], (L, Q_RANK)),
        "w_uq": nrm(ks[8], (L, Q_RANK, MLA_HEADS * (NOPE_DIM + ROPE_DIM)), Q_RANK),
        "kv_norm_g": gain(ks[9], (L, KV_RANK)),
        "w_ukv": nrm(ks[10], (L, KV_RANK, MLA_HEADS * (NOPE_DIM + V_DIM)), KV_RANK),
        "sg_ln_g": gain(ks[11], (L, SG_WIDTH)),
        "sg_ln_b": 0.02 * jax.random.normal(ks[12], (L, SG_WIDTH), jnp.float32),
        "w_sp": nrm(ks[13], (L, SG_GROUPS, CHUNK, CHUNK), CHUNK),
        "b_sp": gain(ks[14], (L, SG_GROUPS, CHUNK)),
        "conv_w": nrm(ks[15], (L, CONV_WIDTH, CV_WIDTH), CONV_WIDTH),
        "out_norm_g": gain(ks[16], (L, MIX_WIDTH)),
        "w_out": nrm(ks[17], (L, MIX_WIDTH, D), MIX_WIDTH),
        "w_gate": nrm(ks[18], (L, D, D_FF), D),
        "w_up": nrm(ks[19], (L, D, D_FF), D),
        "w_down": nrm(ks[20], (L, D_FF, D), D_FF),
    }


def _fwd_reference(x, positions, mix_pre_g, mix_post_g, ffn_pre_g, ffn_post_g, w_in, q_norm_g, w_uq,
              kv_norm_g, w_ukv, sg_ln_g, sg_ln_b, w_sp, b_sp, conv_w, out_norm_g, w_out,
              w_gate, w_up, w_down):
    cos, sin = rope_tables(positions)
    a_end = MLA_WIDTH
    s_end = MLA_WIDTH + SG_WIDTH
    for l in range(DEPTH):
        h = rms_norm(x, mix_pre_g[l])
        z = h @ w_in[l]
        y_a = mla_branch(z, cos, sin, q_norm_g[l], w_uq[l], kv_norm_g[l], w_ukv[l])
        y_b = sgu_branch(z, sg_ln_g[l], sg_ln_b[l], w_sp[l], b_sp[l])
        y_c = conv_branch(z, conv_w[l])
        g = out_norm_g[l]
        mix = jnp.concatenate([rms_norm(y_a, g[:a_end]),
                               rms_norm(y_b, g[a_end:s_end]),
                               rms_norm(y_c, g[s_end:])], axis=-1)
        x = x + rms_norm(mix @ w_out[l], mix_post_g[l])
        h = rms_norm(x, ffn_pre_g[l])
        f = (jax.nn.silu(h @ w_gate[l]) * (h @ w_up[l])) @ w_down[l]
        x = x + rms_norm(f, ffn_post_g[l])
    return x


import jax as _jax
import jax.numpy as _jnp

TWIN_FORMAT = 'train_step'
FWD_PARAMS = ['x', 'positions', 'mix_pre_g', 'mix_post_g', 'ffn_pre_g', 'ffn_post_g', 'w_in', 'q_norm_g', 'w_uq', 'kv_norm_g', 'w_ukv', 'sg_ln_g', 'sg_ln_b', 'w_sp', 'b_sp', 'conv_w', 'out_norm_g', 'w_out', 'w_gate', 'w_up', 'w_down']
TWIN_WEIGHTS = ['mix_pre_g', 'mix_post_g', 'ffn_pre_g', 'ffn_post_g', 'w_in', 'q_norm_g', 'w_uq', 'kv_norm_g', 'w_ukv', 'sg_ln_g', 'sg_ln_b', 'w_sp', 'b_sp', 'conv_w', 'out_norm_g', 'w_out', 'w_gate', 'w_up', 'w_down']
TWIN_DIFF_INPUT = 'x'
TWIN_INPUTS = ['x', 'positions', 'mix_pre_g', 'mix_post_g', 'ffn_pre_g', 'ffn_post_g', 'w_in', 'q_norm_g', 'w_uq', 'kv_norm_g', 'w_ukv', 'sg_ln_g', 'sg_ln_b', 'w_sp', 'b_sp', 'conv_w', 'out_norm_g', 'w_out', 'w_gate', 'w_up', 'w_down', 'loss_target', 'm_mix_pre_g', 'm_mix_post_g', 'm_ffn_pre_g', 'm_ffn_post_g', 'm_w_in', 'm_q_norm_g', 'm_w_uq', 'm_kv_norm_g', 'm_w_ukv', 'm_sg_ln_g', 'm_sg_ln_b', 'm_w_sp', 'm_b_sp', 'm_conv_w', 'm_out_norm_g', 'm_w_out', 'm_w_gate', 'm_w_up', 'm_w_down', 'v_mix_pre_g', 'v_mix_post_g', 'v_ffn_pre_g', 'v_ffn_post_g', 'v_w_in', 'v_q_norm_g', 'v_w_uq', 'v_kv_norm_g', 'v_w_ukv', 'v_sg_ln_g', 'v_sg_ln_b', 'v_w_sp', 'v_b_sp', 'v_conv_w', 'v_out_norm_g', 'v_w_out', 'v_w_gate', 'v_w_up', 'v_w_down']
TWIN_OUTPUTS = ['loss', 'grad_x', 'grad_mix_pre_g', 'grad_mix_post_g', 'grad_ffn_pre_g', 'grad_ffn_post_g', 'grad_w_in', 'grad_q_norm_g', 'grad_w_uq', 'grad_kv_norm_g', 'grad_w_ukv', 'grad_sg_ln_g', 'grad_sg_ln_b', 'grad_w_sp', 'grad_b_sp', 'grad_conv_w', 'grad_out_norm_g', 'grad_w_out', 'grad_w_gate', 'grad_w_up', 'grad_w_down', 'delta_mix_pre_g', 'delta_mix_post_g', 'delta_ffn_pre_g', 'delta_ffn_post_g', 'delta_w_in', 'delta_q_norm_g', 'delta_w_uq', 'delta_kv_norm_g', 'delta_w_ukv', 'delta_sg_ln_g', 'delta_sg_ln_b', 'delta_w_sp', 'delta_b_sp', 'delta_conv_w', 'delta_out_norm_g', 'delta_w_out', 'delta_w_gate', 'delta_w_up', 'delta_w_down', 'new_m_mix_pre_g', 'new_m_mix_post_g', 'new_m_ffn_pre_g', 'new_m_ffn_post_g', 'new_m_w_in', 'new_m_q_norm_g', 'new_m_w_uq', 'new_m_kv_norm_g', 'new_m_w_ukv', 'new_m_sg_ln_g', 'new_m_sg_ln_b', 'new_m_w_sp', 'new_m_b_sp', 'new_m_conv_w', 'new_m_out_norm_g', 'new_m_w_out', 'new_m_w_gate', 'new_m_w_up', 'new_m_w_down', 'new_v_mix_pre_g', 'new_v_mix_post_g', 'new_v_ffn_pre_g', 'new_v_ffn_post_g', 'new_v_w_in', 'new_v_q_norm_g', 'new_v_w_uq', 'new_v_kv_norm_g', 'new_v_w_ukv', 'new_v_sg_ln_g', 'new_v_sg_ln_b', 'new_v_w_sp', 'new_v_b_sp', 'new_v_conv_w', 'new_v_out_norm_g', 'new_v_w_out', 'new_v_w_gate', 'new_v_w_up', 'new_v_w_down']
TWIN_LEAF_KINDS = {'loss': 'loss', 'grad_x': 'grad_x', 'grad_mix_pre_g': 'grad_w', 'grad_mix_post_g': 'grad_w', 'grad_ffn_pre_g': 'grad_w', 'grad_ffn_post_g': 'grad_w', 'grad_w_in': 'grad_w', 'grad_q_norm_g': 'grad_w', 'grad_w_uq': 'grad_w', 'grad_kv_norm_g': 'grad_w', 'grad_w_ukv': 'grad_w', 'grad_sg_ln_g': 'grad_w', 'grad_sg_ln_b': 'grad_w', 'grad_w_sp': 'grad_w', 'grad_b_sp': 'grad_w', 'grad_conv_w': 'grad_w', 'grad_out_norm_g': 'grad_w', 'grad_w_out': 'grad_w', 'grad_w_gate': 'grad_w', 'grad_w_up': 'grad_w', 'grad_w_down': 'grad_w', 'delta_mix_pre_g': 'delta_w', 'delta_mix_post_g': 'delta_w', 'delta_ffn_pre_g': 'delta_w', 'delta_ffn_post_g': 'delta_w', 'delta_w_in': 'delta_w', 'delta_q_norm_g': 'delta_w', 'delta_w_uq': 'delta_w', 'delta_kv_norm_g': 'delta_w', 'delta_w_ukv': 'delta_w', 'delta_sg_ln_g': 'delta_w', 'delta_sg_ln_b': 'delta_w', 'delta_w_sp': 'delta_w', 'delta_b_sp': 'delta_w', 'delta_conv_w': 'delta_w', 'delta_out_norm_g': 'delta_w', 'delta_w_out': 'delta_w', 'delta_w_gate': 'delta_w', 'delta_w_up': 'delta_w', 'delta_w_down': 'delta_w', 'new_m_mix_pre_g': 'new_m', 'new_m_mix_post_g': 'new_m', 'new_m_ffn_pre_g': 'new_m', 'new_m_ffn_post_g': 'new_m', 'new_m_w_in': 'new_m', 'new_m_q_norm_g': 'new_m', 'new_m_w_uq': 'new_m', 'new_m_kv_norm_g': 'new_m', 'new_m_w_ukv': 'new_m', 'new_m_sg_ln_g': 'new_m', 'new_m_sg_ln_b': 'new_m', 'new_m_w_sp': 'new_m', 'new_m_b_sp': 'new_m', 'new_m_conv_w': 'new_m', 'new_m_out_norm_g': 'new_m', 'new_m_w_out': 'new_m', 'new_m_w_gate': 'new_m', 'new_m_w_up': 'new_m', 'new_m_w_down': 'new_m', 'new_v_mix_pre_g': 'new_v', 'new_v_mix_post_g': 'new_v', 'new_v_ffn_pre_g': 'new_v', 'new_v_ffn_post_g': 'new_v', 'new_v_w_in': 'new_v', 'new_v_q_norm_g': 'new_v', 'new_v_w_uq': 'new_v', 'new_v_kv_norm_g': 'new_v', 'new_v_w_ukv': 'new_v', 'new_v_sg_ln_g': 'new_v', 'new_v_sg_ln_b': 'new_v', 'new_v_w_sp': 'new_v', 'new_v_b_sp': 'new_v', 'new_v_conv_w': 'new_v', 'new_v_out_norm_g': 'new_v', 'new_v_w_out': 'new_v', 'new_v_w_gate': 'new_v', 'new_v_w_up': 'new_v', 'new_v_w_down': 'new_v'}


def _forward(args):
    return _fwd_reference(*[args[k] for k in FWD_PARAMS])


def _output_shape():
    out = _jax.eval_shape(lambda: _forward(_fwd_setup_inputs(0)))
    return out.shape, out.dtype

N_MICROBATCH = 1
ADAM_LR = 0.001
ADAM_B1 = 0.9
ADAM_B2 = 0.999
ADAM_EPS = 1e-08
ADAM_WD = 0.01
ADAM_STEP = 10
PER_EXAMPLE_BATCH_AXIS = {'x': 0, 'positions': 0, 'loss_target': 0}
SHARED_INPUTS = []
_WEIGHT_DTYPES = {'mix_pre_g': _jnp.float32, 'mix_post_g': _jnp.float32, 'ffn_pre_g': _jnp.float32, 'ffn_post_g': _jnp.float32, 'w_in': _jnp.float32, 'q_norm_g': _jnp.float32, 'w_uq': _jnp.float32, 'kv_norm_g': _jnp.float32, 'w_ukv': _jnp.float32, 'sg_ln_g': _jnp.float32, 'sg_ln_b': _jnp.float32, 'w_sp': _jnp.float32, 'b_sp': _jnp.float32, 'conv_w': _jnp.float32, 'out_norm_g': _jnp.float32, 'w_out': _jnp.float32, 'w_gate': _jnp.float32, 'w_up': _jnp.float32, 'w_down': _jnp.float32}
MOMENT_SCALE = {'mix_pre_g': 1.601476e+01, 'mix_post_g': 3.853404e+01, 'ffn_pre_g': 6.469540e+00, 'ffn_post_g': 3.248278e+01, 'w_in': 1.202285e+01, 'q_norm_g': 1.738917e+00, 'w_uq': 1.350343e+00, 'kv_norm_g': 3.562491e+01, 'w_ukv': 1.755990e+01, 'sg_ln_g': 9.130901e-01, 'sg_ln_b': 1.367159e+00, 'w_sp': 6.162504e-01, 'b_sp': 9.594576e-01, 'conv_w': 1.785157e+00, 'out_norm_g': 1.977058e+01, 'w_out': 2.002662e+01, 'w_gate': 1.953611e+00, 'w_up': 3.166750e+00, 'w_down': 5.253944e+00}


def _to_microbatches(a, axis):
    t = _jnp.moveaxis(a, axis, 0)
    t = t.reshape((N_MICROBATCH, t.shape[0] // N_MICROBATCH) + t.shape[1:])
    return _jnp.moveaxis(t, 1, axis + 1)


def setup_inputs(seed: int = 0) -> dict:
    inp = _fwd_setup_inputs(seed)
    key = _jax.random.fold_in(_jax.random.key(seed), 7919)
    shape, _ = _output_shape()
    out = dict(inp)
    out["loss_target"] = _jax.random.normal(_jax.random.fold_in(key, 0), shape, _jnp.float32)
    for i, name in enumerate(TWIN_WEIGHTS):
        w = inp[name].astype(_jnp.float32)
        if MOMENT_SCALE is None:
            s = _jnp.sqrt(_jnp.mean(_jnp.square(w)) + 1e-30)
        else:
            s = MOMENT_SCALE[name]
        km, kv = _jax.random.split(_jax.random.fold_in(key, i + 1))
        out[name] = w
        out["m_" + name] = s * _jax.random.normal(km, w.shape, _jnp.float32)
        out["v_" + name] = (s * s) * _jax.random.uniform(kv, w.shape, _jnp.float32, 0.5, 1.5)
    if N_MICROBATCH > 1:
        for name, axis in PER_EXAMPLE_BATCH_AXIS.items():
            out[name] = _to_microbatches(out[name], axis)
    return {'x': out['x'], 'positions': out['positions'], 'mix_pre_g': out['mix_pre_g'], 'mix_post_g': out['mix_post_g'], 'ffn_pre_g': out['ffn_pre_g'], 'ffn_post_g': out['ffn_post_g'], 'w_in': out['w_in'], 'q_norm_g': out['q_norm_g'], 'w_uq': out['w_uq'], 'kv_norm_g': out['kv_norm_g'], 'w_ukv': out['w_ukv'], 'sg_ln_g': out['sg_ln_g'], 'sg_ln_b': out['sg_ln_b'], 'w_sp': out['w_sp'], 'b_sp': out['b_sp'], 'conv_w': out['conv_w'], 'out_norm_g': out['out_norm_g'], 'w_out': out['w_out'], 'w_gate': out['w_gate'], 'w_up': out['w_up'], 'w_down': out['w_down'], 'loss_target': out['loss_target'], 'm_mix_pre_g': out['m_mix_pre_g'], 'm_mix_post_g': out['m_mix_post_g'], 'm_ffn_pre_g': out['m_ffn_pre_g'], 'm_ffn_post_g': out['m_ffn_post_g'], 'm_w_in': out['m_w_in'], 'm_q_norm_g': out['m_q_norm_g'], 'm_w_uq': out['m_w_uq'], 'm_kv_norm_g': out['m_kv_norm_g'], 'm_w_ukv': out['m_w_ukv'], 'm_sg_ln_g': out['m_sg_ln_g'], 'm_sg_ln_b': out['m_sg_ln_b'], 'm_w_sp': out['m_w_sp'], 'm_b_sp': out['m_b_sp'], 'm_conv_w': out['m_conv_w'], 'm_out_norm_g': out['m_out_norm_g'], 'm_w_out': out['m_w_out'], 'm_w_gate': out['m_w_gate'], 'm_w_up': out['m_w_up'], 'm_w_down': out['m_w_down'], 'v_mix_pre_g': out['v_mix_pre_g'], 'v_mix_post_g': out['v_mix_post_g'], 'v_ffn_pre_g': out['v_ffn_pre_g'], 'v_ffn_post_g': out['v_ffn_post_g'], 'v_w_in': out['v_w_in'], 'v_q_norm_g': out['v_q_norm_g'], 'v_w_uq': out['v_w_uq'], 'v_kv_norm_g': out['v_kv_norm_g'], 'v_w_ukv': out['v_w_ukv'], 'v_sg_ln_g': out['v_sg_ln_g'], 'v_sg_ln_b': out['v_sg_ln_b'], 'v_w_sp': out['v_w_sp'], 'v_b_sp': out['v_b_sp'], 'v_conv_w': out['v_conv_w'], 'v_out_norm_g': out['v_out_norm_g'], 'v_w_out': out['v_w_out'], 'v_w_gate': out['v_w_gate'], 'v_w_up': out['v_w_up'], 'v_w_down': out['v_w_down']}


def _loss(weights, diff, rest, loss_target):
    with _jax.named_scope("forward"):
        args = {**rest, TWIN_DIFF_INPUT: diff, **{k: w.astype(_WEIGHT_DTYPES[k]) for k, w in weights.items()}}
        y = _forward(args)
    with _jax.named_scope("loss_head"):
        err = _jnp.square(y.astype(_jnp.float32) - loss_target)
        return 0.5 * _jnp.sum(_jnp.mean(err, axis=-1)) if err.ndim else 0.5 * err


def _adamw(w, g, m, v):
    m = ADAM_B1 * m + (1.0 - ADAM_B1) * g
    v = ADAM_B2 * v + (1.0 - ADAM_B2) * _jnp.square(g)
    m_hat = m / (1.0 - ADAM_B1 ** ADAM_STEP)
    v_hat = v / (1.0 - ADAM_B2 ** ADAM_STEP)
    delta = -ADAM_LR * (m_hat / (_jnp.sqrt(v_hat) + ADAM_EPS) + ADAM_WD * w)
    return delta, m, v


def reference(x, positions, mix_pre_g, mix_post_g, ffn_pre_g, ffn_post_g, w_in, q_norm_g, w_uq, kv_norm_g, w_ukv, sg_ln_g, sg_ln_b, w_sp, b_sp, conv_w, out_norm_g, w_out, w_gate, w_up, w_down, loss_target, m_mix_pre_g, m_mix_post_g, m_ffn_pre_g, m_ffn_post_g, m_w_in, m_q_norm_g, m_w_uq, m_kv_norm_g, m_w_ukv, m_sg_ln_g, m_sg_ln_b, m_w_sp, m_b_sp, m_conv_w, m_out_norm_g, m_w_out, m_w_gate, m_w_up, m_w_down, v_mix_pre_g, v_mix_post_g, v_ffn_pre_g, v_ffn_post_g, v_w_in, v_q_norm_g, v_w_uq, v_kv_norm_g, v_w_ukv, v_sg_ln_g, v_sg_ln_b, v_w_sp, v_b_sp, v_conv_w, v_out_norm_g, v_w_out, v_w_gate, v_w_up, v_w_down):
    given = dict(x=x, positions=positions, mix_pre_g=mix_pre_g, mix_post_g=mix_post_g, ffn_pre_g=ffn_pre_g, ffn_post_g=ffn_post_g, w_in=w_in, q_norm_g=q_norm_g, w_uq=w_uq, kv_norm_g=kv_norm_g, w_ukv=w_ukv, sg_ln_g=sg_ln_g, sg_ln_b=sg_ln_b, w_sp=w_sp, b_sp=b_sp, conv_w=conv_w, out_norm_g=out_norm_g, w_out=w_out, w_gate=w_gate, w_up=w_up, w_down=w_down, loss_target=loss_target, m_mix_pre_g=m_mix_pre_g, m_mix_post_g=m_mix_post_g, m_ffn_pre_g=m_ffn_pre_g, m_ffn_post_g=m_ffn_post_g, m_w_in=m_w_in, m_q_norm_g=m_q_norm_g, m_w_uq=m_w_uq, m_kv_norm_g=m_kv_norm_g, m_w_ukv=m_w_ukv, m_sg_ln_g=m_sg_ln_g, m_sg_ln_b=m_sg_ln_b, m_w_sp=m_w_sp, m_b_sp=m_b_sp, m_conv_w=m_conv_w, m_out_norm_g=m_out_norm_g, m_w_out=m_w_out, m_w_gate=m_w_gate, m_w_up=m_w_up, m_w_down=m_w_down, v_mix_pre_g=v_mix_pre_g, v_mix_post_g=v_mix_post_g, v_ffn_pre_g=v_ffn_pre_g, v_ffn_post_g=v_ffn_post_g, v_w_in=v_w_in, v_q_norm_g=v_q_norm_g, v_w_uq=v_w_uq, v_kv_norm_g=v_kv_norm_g, v_w_ukv=v_w_ukv, v_sg_ln_g=v_sg_ln_g, v_sg_ln_b=v_sg_ln_b, v_w_sp=v_w_sp, v_b_sp=v_b_sp, v_conv_w=v_conv_w, v_out_norm_g=v_out_norm_g, v_w_out=v_w_out, v_w_gate=v_w_gate, v_w_up=v_w_up, v_w_down=v_w_down)
    weights = {n: given[n] for n in TWIN_WEIGHTS}
    shared = {n: given[n] for n in SHARED_INPUTS}
    per_example = {n: given[n] for n in ['x', 'positions']}
    grad_fn = _jax.value_and_grad(_loss, argnums=(0, 1))

    def one_microbatch(ex, loss_target):
        ex = dict(ex)
        diff = ex.pop(TWIN_DIFF_INPUT)
        return grad_fn(weights, diff, {**shared, **ex}, loss_target)

    if N_MICROBATCH == 1:
        loss, (grad_w, grad_x) = one_microbatch(per_example, given["loss_target"])
    else:
        def body(carry, xs):
            loss_sum, grad_sum = carry
            l_k, (gw_k, gx_k) = one_microbatch(xs[0], xs[1])
            with _jax.named_scope("update"):
                return (loss_sum + l_k, _jax.tree.map(_jnp.add, grad_sum, gw_k)), gx_k

        init = (_jnp.zeros((), _jnp.float32), _jax.tree.map(_jnp.zeros_like, weights))
        (loss, grad_w), grad_x = _jax.lax.scan(body, init, (per_example, given["loss_target"]))
    with _jax.named_scope("update"):
        delta_w, new_m, new_v = {}, {}, {}
        for n in TWIN_WEIGHTS:
            delta_w[n], new_m[n], new_v[n] = _adamw(weights[n], grad_w[n], given["m_" + n], given["v_" + n])
    return (loss, grad_x, *[grad_w[n] for n in TWIN_WEIGHTS], *[delta_w[n] for n in TWIN_WEIGHTS],
            *[new_m[n] for n in TWIN_WEIGHTS], *[new_v[n] for n in TWIN_WEIGHTS])
```

```python
import math

import jax
import jax.numpy as jnp
from jax import lax
from jax.experimental import pallas as pl
from jax.experimental.pallas import tpu as pltpu

F32 = jnp.float32
BF16 = jnp.bfloat16

D = 1024
HEADS = 8
NOPE = 64
ROPE = 32
VD = 64
QR = 384
KVR = 256
SGW = 256
CVW = 256
CHUNK = 128
DFF = 2816
EPS = 1e-6
ROPE_THETA = 10000.0
SCALE = (NOPE + ROPE) ** -0.5
ZA = 768
ZB = 1280
QW = HEADS * 128
KVW = HEADS * 128 + HEADS * VD
NEG = -1e30
GC0 = 0.7978845608028654
GC1 = 0.044715

ADAM_LR = 0.001
ADAM_B1 = 0.9
ADAM_B2 = 0.999
ADAM_EPS = 1e-08
ADAM_WD = 0.01
ADAM_STEP = 10

V7X_VMEM_LIMIT = 52 * 1024 * 1024
ROW_TILE = 512
ATT_TILE = 512

NT = (((1,), (1,)), ((), ()))
TN = (((0,), (0,)), ((), ()))

PACK_ROWS = (("w_in", 488), ("w_uq", 72), ("w_ukv", 64), ("w_out", 256), ("w_gate", 704), ("w_up", 704), ("w_down", 704))
PACK_TOTAL = sum(r for _, r in PACK_ROWS)
WPACK_ROWS = 3008
SMALL = (("mix_pre_g", (D,)), ("mix_post_g", (D,)), ("ffn_pre_g", (D,)), ("ffn_post_g", (D,)), ("q_norm_g", (QR,)),
         ("kv_norm_g", (KVR,)), ("sg_ln_g", (SGW,)), ("sg_ln_b", (SGW,)), ("w_sp", (4, CHUNK, CHUNK)), ("b_sp", (4, CHUNK)),
         ("conv_w", (3, CVW)), ("out_norm_g", (D,)))
WEIGHTS = ["mix_pre_g", "mix_post_g", "ffn_pre_g", "ffn_post_g", "w_in", "q_norm_g", "w_uq", "kv_norm_g", "w_ukv", "sg_ln_g",
           "sg_ln_b", "w_sp", "b_sp", "conv_w", "out_norm_g", "w_out", "w_gate", "w_up", "w_down"]


def _cp(*sem):
    return pltpu.CompilerParams(dimension_semantics=sem, vmem_limit_bytes=V7X_VMEM_LIMIT)


def _sds(shape, dtype):
    return jax.ShapeDtypeStruct(shape, dtype)


def _row(tm, n):
    return pl.BlockSpec((tm, n), lambda i: (i, 0))


def _full(*shape):
    return pl.BlockSpec(shape, lambda *_: (0,) * len(shape))


def _rms(x, g):
    r = lax.rsqrt(jnp.mean(x * x, axis=-1, keepdims=True) + EPS)
    return x * r * g


def _rms_bwd(x, g, dy):
    r = lax.rsqrt(jnp.mean(x * x, axis=-1, keepdims=True) + EPS)
    xh = x * r
    dg = jnp.sum(dy * xh, axis=0, keepdims=True)
    dxh = dy * g
    dx = r * (dxh - xh * jnp.mean(dxh * xh, axis=-1, keepdims=True))
    return dx, dg


def _gelu(x):
    return 0.5 * x * (1.0 + jnp.tanh(GC0 * (x + GC1 * x * x * x)))


def _gelu_grad(x):
    t = jnp.tanh(GC0 * (x + GC1 * x * x * x))
    return 0.5 * (1.0 + t) + 0.5 * x * (1.0 - t * t) * GC0 * (1.0 + 3.0 * GC1 * x * x)


def _rope(xb, c, s1, s2):
    return xb * c + pltpu.roll(xb, 112, 1) * s1 + pltpu.roll(xb, 16, 1) * s2


def _rope_bwd(dy, c, s1, s2):
    return dy * c + pltpu.roll(dy * s1, 16, 1) + pltpu.roll(dy * s2, 112, 1)


def _group_masks(shape):
    lane = lax.broadcasted_iota(jnp.int32, shape, 1)
    return [(lane >= 64 * g) & (lane < 64 * g + 64) for g in range(shape[1] // 64)]


def _group_mean(v, masks):
    out = jnp.zeros_like(v)
    for m in masks:
        s = jnp.sum(jnp.where(m, v, 0.0), axis=-1, keepdims=True) * (1.0 / 64.0)
        out = jnp.where(m, s, out)
    return out


def _pick_row(blk, idx):
    row = lax.broadcasted_iota(jnp.int32, blk.shape, 0)
    return jnp.sum(jnp.where(row == idx, blk, 0.0), axis=0, keepdims=True)


def _shift_down(y, k, first_rows):
    out = pltpu.roll(y, k, 0)
    row = lax.broadcasted_iota(jnp.int32, y.shape, 0)
    for idx in range(k):
        out = jnp.where(row == idx, first_rows[idx], out)
    return out


def _shift_up(y, k, last_rows):
    n = y.shape[0]
    out = pltpu.roll(y, n - k, 0)
    row = lax.broadcasted_iota(jnp.int32, y.shape, 0)
    for idx in range(k):
        out = jnp.where(row == n - k + idx, last_rows[idx], out)
    return out


def _tril_mask():
    r = lax.broadcasted_iota(jnp.int32, (CHUNK, CHUNK), 0)
    c = lax.broadcasted_iota(jnp.int32, (CHUNK, CHUNK), 1)
    return r >= c


def _sgu_forward(zu, zv, g_ln, b_ln, wc_bf, bsp, masks, cmasks):
    u = _gelu(zu)
    vv = _gelu(zv)
    mu = _group_mean(vv, masks)
    dv = vv - mu
    rs = lax.rsqrt(_group_mean(dv * dv, masks) + EPS)
    xh = dv * rs
    vn = xh * g_ln + b_ln
    chunks = []
    for ci in range(zu.shape[0] // CHUNK):
        vc = vn[ci * CHUNK:(ci + 1) * CHUNK, :]
        acc = bsp
        for g in range(4):
            acc = acc + jnp.dot(wc_bf[g], jnp.where(cmasks[g], vc, 0.0).astype(BF16), preferred_element_type=F32)
        chunks.append(acc)
    mixed = jnp.concatenate(chunks, axis=0) if len(chunks) > 1 else chunks[0]
    return u, vv, xh, rs, vn, mixed


def _conv_forward(gc, hh, prev_gc, prev_hh, first_tile, cw):
    yv = gc * hh
    prev = jnp.where(first_tile, 0.0, prev_gc * prev_hh)
    p6, p7 = _pick_row(prev, 6), _pick_row(prev, 7)
    sh1 = _shift_down(yv, 1, [p7])
    sh2 = _shift_down(yv, 2, [p6, p7])
    conv = sh2 * cw[0:1, :] + sh1 * cw[1:2, :] + yv * cw[2:3, :]
    return yv, sh1, sh2, conv


def _in_proj(x, g, wa, wb):
    t = x.shape[0]
    tm = min(ROW_TILE, t)

    def body(x_ref, g_ref, wa_ref, wb_ref, h_ref, za_ref, zb_ref):
        h = _rms(x_ref[...], g_ref[...]).astype(BF16)
        h_ref[...] = h
        za_ref[...] = jnp.dot(h, wa_ref[...], preferred_element_type=F32)
        zb_ref[...] = jnp.dot(h, wb_ref[...], preferred_element_type=F32)

    return pl.pallas_call(
        body, name="in_proj", grid=(t // tm,),
        in_specs=[_row(tm, D), _full(1, D), _full(D, ZA), _full(D, ZB)],
        out_specs=[_row(tm, D), _row(tm, ZA), _row(tm, ZB)],
        out_shape=[_sds((t, D), BF16), _sds((t, ZA), F32), _sds((t, ZB), F32)],
        compiler_params=_cp("parallel"))(x, g, wa, wb)


def _mla_prep(za, gq, gkv, wuq, wukv, rc, rs1, rs2):
    t = za.shape[0]
    tm = min(ROW_TILE, t)

    def body(z_ref, gq_ref, gkv_ref, wuq_ref, wukv_ref, c_ref, s1_ref, s2_ref, cq_ref, ckv_ref, q_ref, k_ref, v_ref):
        z = z_ref[...]
        cq = _rms(z[:, :QR], gq_ref[...]).astype(BF16)
        ckv = _rms(z[:, QR:QR + KVR], gkv_ref[...]).astype(BF16)
        cq_ref[...] = cq
        ckv_ref[...] = ckv
        c, s1, s2 = c_ref[...], s1_ref[...], s2_ref[...]
        kr = _rope(z[:, QR + KVR:], c, s1, s2)
        q = jnp.dot(cq, wuq_ref[...], preferred_element_type=F32)
        kv = jnp.dot(ckv, wukv_ref[...], preferred_element_type=F32)
        for h in range(HEADS):
            sl = slice(128 * h, 128 * h + 128)
            q_ref[:, sl] = (_rope(q[:, sl], c, s1, s2) * SCALE).astype(BF16)
            k_ref[:, sl] = (kv[:, sl] + kr).astype(BF16)
        v_ref[...] = kv[:, QW:].astype(BF16)

    return pl.pallas_call(
        body, name="mla_prep", grid=(t // tm,),
        in_specs=[_row(tm, ZA), _full(1, QR), _full(1, KVR), _full(QR, QW), _full(KVR, KVW),
                  _row(tm, 128), _row(tm, 128), _row(tm, 128)],
        out_specs=[_row(tm, QR), _row(tm, KVR), _row(tm, QW), _row(tm, QW), _row(tm, HEADS * VD)],
        out_shape=[_sds((t, QR), BF16), _sds((t, KVR), BF16), _sds((t, QW), BF16), _sds((t, QW), BF16),
                   _sds((t, HEADS * VD), BF16)],
        compiler_params=_cp("parallel"))(za, gq, gkv, wuq, wukv, rc, rs1, rs2)


def _att_tile(t):
    return min(ATT_TILE, max(t // 2, 128))


def _attn_fwd(qs, k, v):
    t = qs.shape[0]
    tq = _att_tile(t)
    nq = t // tq

    def body(q_ref, k_ref, v_ref, o_ref, lse_ref, m_s, l_s, acc_s):
        i, j = pl.program_id(1), pl.program_id(2)

        @pl.when(j == 0)
        def _():
            m_s[...] = jnp.full(m_s.shape, NEG, F32)
            l_s[...] = jnp.zeros(l_s.shape, F32)
            acc_s[...] = jnp.zeros(acc_s.shape, F32)

        @pl.when(j <= i)
        def _():
            row = lax.broadcasted_iota(jnp.int32, (tq, tq), 0) + i * tq
            col = lax.broadcasted_iota(jnp.int32, (tq, tq), 1) + j * tq
            keep = col <= row
            vv = v_ref[...]
            for hh in range(2):
                sl = slice(128 * hh, 128 * hh + 128)
                s = lax.dot_general(q_ref[:, sl], k_ref[:, sl], NT, preferred_element_type=F32)
                s = jnp.where(keep, s, NEG)
                m_old = m_s[hh]
                m_new = jnp.maximum(m_old, jnp.max(s, axis=-1, keepdims=True))
                alpha = jnp.exp(m_old - m_new)
                p = jnp.exp(s - m_new)
                l_s[hh] = alpha * l_s[hh] + jnp.sum(p, axis=-1, keepdims=True)
                acc_s[hh] = alpha * acc_s[hh] + jnp.dot(p.astype(BF16), vv, preferred_element_type=F32)
                m_s[hh] = m_new

        @pl.when(j == nq - 1)
        def _():
            lane = lax.broadcasted_iota(jnp.int32, (tq, 128), 1)
            o_ref[...] = jnp.where(lane < VD, acc_s[0] / l_s[0], acc_s[1] / l_s[1])
            lse_ref[0] = m_s[0] + jnp.log(l_s[0])
            lse_ref[1] = m_s[1] + jnp.log(l_s[1])

    return pl.pallas_call(
        body, name="attn_fwd", grid=(HEADS // 2, nq, nq),
        in_specs=[pl.BlockSpec((tq, 256), lambda p, i, j: (i, p)),
                  pl.BlockSpec((tq, 256), lambda p, i, j: (jnp.minimum(j, i), p)),
                  pl.BlockSpec((tq, 128), lambda p, i, j: (jnp.minimum(j, i), p))],
        out_specs=[pl.BlockSpec((tq, 128), lambda p, i, j: (i, p)),
                   pl.BlockSpec((2, tq, 1), lambda p, i, j: (p, i, 0))],
        out_shape=[_sds((t, HEADS * VD), F32), _sds((HEADS, t, 1), F32)],
        scratch_shapes=[pltpu.VMEM((2, tq, 1), F32), pltpu.VMEM((2, tq, 1), F32), pltpu.VMEM((2, tq, 128), F32)],
        compiler_params=_cp("parallel", "parallel", "arbitrary"))(qs, k, v)


def _mixer_fwd(zb, ya, g_ln, b_ln, wsp, bsp, cw, g_out):
    t = zb.shape[0]
    tm = min(ROW_TILE, t)
    hb = tm // 8

    def body(zb_ref, zprev_ref, ya_ref, gln_ref, bln_ref, wsp_ref, bsp_ref, cw_ref, go_ref, mix_ref, yb_ref, yc_ref):
        i = pl.program_id(0)
        masks = _group_masks((tm, SGW))
        cmasks = _group_masks((CHUNK, SGW))
        tril = _tril_mask()
        wc_bf = [jnp.where(tril, wsp_ref[g], 0.0).astype(BF16) for g in range(4)]
        u, _, _, _, _, mixed = _sgu_forward(zb_ref[:, 0:256], zb_ref[:, 256:512], gln_ref[...], bln_ref[...], wc_bf,
                                            bsp_ref[...], masks, cmasks)
        yb = u * mixed
        _, _, _, conv = _conv_forward(zb_ref[:, 768:1024], zb_ref[:, 1024:1280], zprev_ref[:, 768:1024],
                                      zprev_ref[:, 1024:1280], i == 0, cw_ref[...])
        yc = zb_ref[:, 512:768] * conv
        yb_ref[...] = yb
        yc_ref[...] = yc
        go = go_ref[...]
        mix_ref[:, 0:512] = _rms(ya_ref[...], go[:, 0:512]).astype(BF16)
        mix_ref[:, 512:768] = _rms(yb, go[:, 512:768]).astype(BF16)
        mix_ref[:, 768:1024] = _rms(yc, go[:, 768:1024]).astype(BF16)

    return pl.pallas_call(
        body, name="mixer_fwd", grid=(t // tm,),
        in_specs=[_row(tm, ZB), pl.BlockSpec((8, ZB), lambda i: (jnp.maximum(i * hb - 1, 0), 0)), _row(tm, 512),
                  _full(1, SGW), _full(1, SGW), _full(4, CHUNK, CHUNK), _full(CHUNK, SGW), _full(3, CVW), _full(1, D)],
        out_specs=[_row(tm, D), _row(tm, SGW), _row(tm, CVW)],
        out_shape=[_sds((t, D), BF16), _sds((t, SGW), F32), _sds((t, CVW), F32)],
        compiler_params=_cp("parallel"))(zb, zb, ya, g_ln, b_ln, wsp, bsp, cw, g_out)


def _out_proj(mix, w_out, x, g_post, g_fpre):
    t = x.shape[0]
    tm = min(ROW_TILE, t)

    def body(mix_ref, w_ref, x_ref, gp_ref, gf_ref, o_ref, x2_ref, h2_ref):
        o = jnp.dot(mix_ref[...], w_ref[...], preferred_element_type=F32)
        o_ref[...] = o
        x2 = x_ref[...] + _rms(o, gp_ref[...])
        x2_ref[...] = x2
        h2_ref[...] = _rms(x2, gf_ref[...]).astype(BF16)

    return pl.pallas_call(
        body, name="out_proj", grid=(t // tm,),
        in_specs=[_row(tm, D), _full(D, D), _row(tm, D), _full(1, D), _full(1, D)],
        out_specs=[_row(tm, D), _row(tm, D), _row(tm, D)],
        out_shape=[_sds((t, D), F32), _sds((t, D), F32), _sds((t, D), BF16)],
        compiler_params=_cp("parallel"))(mix, w_out, x, g_post, g_fpre)


def _ffn_up(h2, wg, wu):
    t = h2.shape[0]
    tm = min(ROW_TILE, t)
    tn = DFF // 2

    def body(h_ref, wg_ref, wu_ref, a_ref, b_ref, s_ref):
        h = h_ref[...]
        a = jnp.dot(h, wg_ref[...], preferred_element_type=F32)
        b = jnp.dot(h, wu_ref[...], preferred_element_type=F32)
        a_ref[...] = a.astype(BF16)
        b_ref[...] = b.astype(BF16)
        s_ref[...] = (a * (1.0 / (1.0 + jnp.exp(-a))) * b).astype(BF16)

    blk = pl.BlockSpec((tm, tn), lambda n, i: (i, n))
    wblk = pl.BlockSpec((D, tn), lambda n, i: (0, n))
    return pl.pallas_call(
        body, name="ffn_up", grid=(DFF // tn, t // tm),
        in_specs=[pl.BlockSpec((tm, D), lambda n, i: (i, 0)), wblk, wblk],
        out_specs=[blk, blk, blk],
        out_shape=[_sds((t, DFF), BF16)] * 3,
        compiler_params=_cp("parallel", "parallel"))(h2, wg, wu)


def _ffn_down(s, w_down, x2, g_fpost):
    t = x2.shape[0]
    tm = min(ROW_TILE, t)

    def body(s_ref, w_ref, x_ref, g_ref, f_ref, x3_ref):
        f = jnp.dot(s_ref[...], w_ref[...], preferred_element_type=F32)
        f_ref[...] = f
        x3_ref[...] = x_ref[...] + _rms(f, g_ref[...])

    return pl.pallas_call(
        body, name="ffn_down", grid=(t // tm,),
        in_specs=[_row(tm, DFF), _full(DFF, D), _row(tm, D), _full(1, D)],
        out_specs=[_row(tm, D), _row(tm, D)],
        out_shape=[_sds((t, D), F32), _sds((t, D), F32)],
        compiler_params=_cp("parallel"))(s, w_down, x2, g_fpost)


def _loss_head(y, target):
    t = y.shape[0]
    tm = min(ROW_TILE, t)

    def body(y_ref, t_ref, dy_ref, acc_ref):
        e = y_ref[...] - t_ref[...]
        dy_ref[...] = e * (1.0 / D)
        sq = jnp.sum(e * e, axis=0, keepdims=True)
        part = sq[:, 0:128]
        for b in range(1, D // 128):
            part = part + sq[:, 128 * b:128 * b + 128]

        @pl.when(pl.program_id(0) == 0)
        def _():
            acc_ref[...] = jnp.zeros(acc_ref.shape, F32)

        acc_ref[...] += part

    return pl.pallas_call(
        body, name="loss_head", grid=(t // tm,),
        in_specs=[_row(tm, D), _row(tm, D)],
        out_specs=[_row(tm, D), _full(1, 128)],
        out_shape=[_sds((t, D), F32), _sds((1, 128), F32)],
        compiler_params=_cp("arbitrary"))(y, target)


def _acc_init(step, *refs):
    @pl.when(step == 0)
    def _():
        for r in refs:
            r[...] = jnp.zeros(r.shape, r.dtype)


def _ffn_down_bwd(dx3, f, g_fpost, w_down, a, b):
    t = f.shape[0]
    tm = min(256, t)

    def body(dx_ref, f_ref, g_ref, w_ref, a_ref, b_ref, df_ref, da_ref, db_ref, dg_ref):
        _acc_init(pl.program_id(0), dg_ref)
        df, dg = _rms_bwd(f_ref[...], g_ref[...], dx_ref[...])
        dg_ref[...] += dg
        df = df.astype(BF16)
        df_ref[...] = df
        ds = lax.dot_general(df, w_ref[...], NT, preferred_element_type=F32)
        av = a_ref[...].astype(F32)
        sig = 1.0 / (1.0 + jnp.exp(-av))
        da_ref[...] = (ds * b_ref[...].astype(F32) * (sig * (1.0 + av * (1.0 - sig)))).astype(BF16)
        db_ref[...] = (ds * (av * sig)).astype(BF16)

    return pl.pallas_call(
        body, name="ffn_down_bwd", grid=(t // tm,),
        in_specs=[_row(tm, D), _row(tm, D), _full(1, D), _full(DFF, D), _row(tm, DFF), _row(tm, DFF)],
        out_specs=[_row(tm, D), _row(tm, DFF), _row(tm, DFF), _full(1, D)],
        out_shape=[_sds((t, D), BF16), _sds((t, DFF), BF16), _sds((t, DFF), BF16), _sds((1, D), F32)],
        compiler_params=_cp("arbitrary"))(dx3, f, g_fpost, w_down, a, b)


def _ffn_up_bwd(da, db, wg, wu, x2, dx3, g_fpre):
    t = x2.shape[0]
    tm = min(256, t)

    def body(da_ref, db_ref, wg_ref, wu_ref, x_ref, dx3_ref, g_ref, dx2_ref, dg_ref):
        _acc_init(pl.program_id(0), dg_ref)
        dh = (lax.dot_general(da_ref[...], wg_ref[...], NT, preferred_element_type=F32)
              + lax.dot_general(db_ref[...], wu_ref[...], NT, preferred_element_type=F32))
        dx, dg = _rms_bwd(x_ref[...], g_ref[...], dh)
        dg_ref[...] += dg
        dx2_ref[...] = dx3_ref[...] + dx

    return pl.pallas_call(
        body, name="ffn_up_bwd", grid=(t // tm,),
        in_specs=[_row(tm, DFF), _row(tm, DFF), _full(D, DFF), _full(D, DFF), _row(tm, D), _row(tm, D), _full(1, D)],
        out_specs=[_row(tm, D), _full(1, D)],
        out_shape=[_sds((t, D), F32), _sds((1, D), F32)],
        compiler_params=_cp("arbitrary"))(da, db, wg, wu, x2, dx3, g_fpre)


def _out_proj_bwd(dx2, o, g_post, w_out):
    t = o.shape[0]
    tm = min(ROW_TILE, t)

    def body(dx_ref, o_ref, g_ref, w_ref, do_ref, dmix_ref, dg_ref):
        _acc_init(pl.program_id(0), dg_ref)
        do, dg = _rms_bwd(o_ref[...], g_ref[...], dx_ref[...])
        dg_ref[...] += dg
        do = do.astype(BF16)
        do_ref[...] = do
        dmix_ref[...] = lax.dot_general(do, w_ref[...], NT, preferred_element_type=F32)

    return pl.pallas_call(
        body, name="out_proj_bwd", grid=(t // tm,),
        in_specs=[_row(tm, D), _row(tm, D), _full(1, D), _full(D, D)],
        out_specs=[_row(tm, D), _row(tm, D), _full(1, D)],
        out_shape=[_sds((t, D), BF16), _sds((t, D), F32), _sds((1, D), F32)],
        compiler_params=_cp("arbitrary"))(dx2, o, g_post, w_out)


def _mixer_bwd(dmix, ya, yb, yc, zb, g_ln, b_ln, wsp, bsp, cw, g_out):
    t = zb.shape[0]
    tm = min(ROW_TILE, t)
    hb = tm // 8
    last_blk = t // 8 - 1
    nsteps = t // tm

    def body(dmix_ref, ya_ref, yb_ref, yc_ref, zb_ref, zprev_ref, znext_ref, ycn_ref, dmn_ref,
             gln_ref, bln_ref, wsp_ref, bsp_ref, cw_ref, go_ref,
             dya_ref, dzb_ref, delta_ref, dgo_ref, dgln_ref, dbln_ref, dwsp_ref, dbsp_ref, dcw_ref):
        i = pl.program_id(0)
        _acc_init(i, dgo_ref, dgln_ref, dbln_ref, dwsp_ref, dbsp_ref, dcw_ref)
        go = go_ref[...]
        dmix = dmix_ref[...]

        ya = ya_ref[...]
        dya, dga = _rms_bwd(ya, go[:, 0:512], dmix[:, 0:512])
        dyb, dgb_ = _rms_bwd(yb_ref[...], go[:, 512:768], dmix[:, 512:768])
        dyc, dgc_ = _rms_bwd(yc_ref[...], go[:, 768:1024], dmix[:, 768:1024])
        dgo_ref[:, 0:512] += dga
        dgo_ref[:, 512:768] += dgb_
        dgo_ref[:, 768:1024] += dgc_
        dya_ref[...] = dya.astype(BF16)
        prod = dya * ya
        hmasks = _group_masks((tm, 512))
        for h in range(HEADS):
            delta_ref[h] = jnp.sum(jnp.where(hmasks[h], prod, 0.0), axis=-1, keepdims=True)

        masks = _group_masks((tm, SGW))
        cmasks = _group_masks((CHUNK, SGW))
        tril = _tril_mask()
        wc_bf = [jnp.where(tril, wsp_ref[g], 0.0).astype(BF16) for g in range(4)]
        zu, zv = zb_ref[:, 0:256], zb_ref[:, 256:512]
        g_ln = gln_ref[...]
        u, _, xh, rs, vn, mixed = _sgu_forward(zu, zv, g_ln, bln_ref[...], wc_bf, bsp_ref[...], masks, cmasks)
        du = dyb * mixed
        dmixed = dyb * u
        dvn_chunks = []
        dbsp = jnp.zeros((CHUNK, SGW), F32)
        for ci in range(tm // CHUNK):
            rows = slice(ci * CHUNK, (ci + 1) * CHUNK)
            dm_c = dmixed[rows, :]
            vn_c = vn[rows, :]
            dbsp = dbsp + dm_c
            dvn_c = jnp.zeros((CHUNK, SGW), F32)
            for g in range(4):
                dm_g = jnp.where(cmasks[g], dm_c, 0.0).astype(BF16)
                dw = lax.dot_general(dm_g, vn_c.astype(BF16), NT, preferred_element_type=F32)
                dwsp_ref[g] += jnp.where(tril, dw, 0.0)
                dvn_c = dvn_c + lax.dot_general(wc_bf[g], dm_g, TN, preferred_element_type=F32)
            dvn_chunks.append(dvn_c)
        dbsp_ref[...] += dbsp
        dvn = jnp.concatenate(dvn_chunks, axis=0) if len(dvn_chunks) > 1 else dvn_chunks[0]
        dgln_ref[...] += jnp.sum(dvn * xh, axis=0, keepdims=True)
        dbln_ref[...] += jnp.sum(dvn, axis=0, keepdims=True)
        dxh = dvn * g_ln
        dvv = rs * (dxh - _group_mean(dxh, masks) - xh * _group_mean(dxh * xh, masks))
        dzb_ref[:, 0:256] = (du * _gelu_grad(zu)).astype(BF16)
        dzb_ref[:, 256:512] = (dvv * _gelu_grad(zv)).astype(BF16)

        cwv = cw_ref[...]
        gb, gc, hh = zb_ref[:, 512:768], zb_ref[:, 768:1024], zb_ref[:, 1024:1280]
        yv, sh1, sh2, conv = _conv_forward(gc, hh, zprev_ref[:, 768:1024], zprev_ref[:, 1024:1280], i == 0, cwv)
        dconv = dyc * gb
        dzb_ref[:, 512:768] = (dyc * conv).astype(BF16)
        dcw_ref[0:1, :] += jnp.sum(dconv * sh2, axis=0, keepdims=True)
        dcw_ref[1:2, :] += jnp.sum(dconv * sh1, axis=0, keepdims=True)
        dcw_ref[2:3, :] += jnp.sum(dconv * yv, axis=0, keepdims=True)
        dycn, _ = _rms_bwd(ycn_ref[...], go[:, 768:1024], dmn_ref[...])
        dconv_next = jnp.where(i == nsteps - 1, 0.0, dycn * znext_ref[:, 512:768])
        n0, n1 = _pick_row(dconv_next, 0), _pick_row(dconv_next, 1)
        dyv = dconv * cwv[2:3, :] + _shift_up(dconv, 1, [n0]) * cwv[1:2, :] + _shift_up(dconv, 2, [n0, n1]) * cwv[0:1, :]
        dzb_ref[:, 768:1024] = (dyv * hh).astype(BF16)
        dzb_ref[:, 1024:1280] = (dyv * gc).astype(BF16)

    prev_map = lambda i: (jnp.maximum(i * hb - 1, 0), 0)
    next_map = lambda i: (jnp.minimum((i + 1) * hb, last_blk), 0)
    return pl.pallas_call(
        body, name="mixer_bwd", grid=(nsteps,),
        in_specs=[_row(tm, D), _row(tm, 512), _row(tm, SGW), _row(tm, CVW), _row(tm, ZB),
                  pl.BlockSpec((8, ZB), prev_map), pl.BlockSpec((8, ZB), next_map), pl.BlockSpec((8, CVW), next_map),
                  pl.BlockSpec((8, 256), lambda i: (jnp.minimum((i + 1) * hb, last_blk), 3)),
                  _full(1, SGW), _full(1, SGW), _full(4, CHUNK, CHUNK), _full(CHUNK, SGW), _full(3, CVW), _full(1, D)],
        out_specs=[_row(tm, 512), _row(tm, ZB), pl.BlockSpec((HEADS, tm, 1), lambda i: (0, i, 0)),
                   _full(1, D), _full(1, SGW), _full(1, SGW), _full(4, CHUNK, CHUNK), _full(CHUNK, SGW), _full(3, CVW)],
        out_shape=[_sds((t, 512), BF16), _sds((t, ZB), BF16), _sds((HEADS, t, 1), F32),
                   _sds((1, D), F32), _sds((1, SGW), F32), _sds((1, SGW), F32), _sds((4, CHUNK, CHUNK), F32),
                   _sds((CHUNK, SGW), F32), _sds((3, CVW), F32)],
        compiler_params=_cp("arbitrary"))(dmix, ya, yb, yc, zb, zb, zb, yc, dmix, g_ln, b_ln, wsp, bsp, cw, g_out)


def _attn_bwd_dq(qs, k, v, dya, lse, delta):
    t = qs.shape[0]
    tq = _att_tile(t)
    nq = t // tq

    def body(q_ref, k_ref, v_ref, do_ref, lse_ref, dl_ref, dq_ref, acc_s):
        i, j = pl.program_id(1), pl.program_id(2)

        @pl.when(j == 0)
        def _():
            acc_s[...] = jnp.zeros(acc_s.shape, F32)

        @pl.when(j <= i)
        def _():
            row = lax.broadcasted_iota(jnp.int32, (tq, tq), 0) + i * tq
            col = lax.broadcasted_iota(jnp.int32, (tq, tq), 1) + j * tq
            keep = col <= row
            lane = lax.broadcasted_iota(jnp.int32, (tq, 128), 1)
            vv = v_ref[...]
            do = do_ref[...]
            for hh in range(2):
                sl = slice(128 * hh, 128 * hh + 128)
                kk = k_ref[:, sl]
                s = lax.dot_general(q_ref[:, sl], kk, NT, preferred_element_type=F32)
                p = jnp.where(keep, jnp.exp(s - lse_ref[hh]), 0.0)
                do_h = jnp.where((lane < VD) if hh == 0 else (lane >= VD), do, jnp.zeros_like(do))
                dp = lax.dot_general(do_h, vv, NT, preferred_element_type=F32)
                ds = (p * (dp - dl_ref[hh])).astype(BF16)
                acc_s[:, sl] += jnp.dot(ds, kk, preferred_element_type=F32)

        @pl.when(j == nq - 1)
        def _():
            dq_ref[...] = acc_s[...].astype(BF16)

    col_spec = pl.BlockSpec((2, tq, 1), lambda p, i, j: (p, i, 0))
    return pl.pallas_call(
        body, name="attn_bwd_dq", grid=(HEADS // 2, nq, nq),
        in_specs=[pl.BlockSpec((tq, 256), lambda p, i, j: (i, p)),
                  pl.BlockSpec((tq, 256), lambda p, i, j: (jnp.minimum(j, i), p)),
                  pl.BlockSpec((tq, 128), lambda p, i, j: (jnp.minimum(j, i), p)),
                  pl.BlockSpec((tq, 128), lambda p, i, j: (i, p)), col_spec, col_spec],
        out_specs=pl.BlockSpec((tq, 256), lambda p, i, j: (i, p)),
        out_shape=_sds((t, QW), BF16),
        scratch_shapes=[pltpu.VMEM((tq, 256), F32)],
        compiler_params=_cp("parallel", "parallel", "arbitrary"))(qs, k, v, dya, lse, delta)


def _attn_bwd_dkv(qs, k, v, dya, lse_row, delta_row):
    t = qs.shape[0]
    tk = _att_tile(t)
    nk = t // tk

    def body(q_ref, k_ref, v_ref, do_ref, lse_ref, dl_ref, dk_ref, dv_ref, dk_s, dv_s):
        j, i = pl.program_id(1), pl.program_id(2)

        @pl.when(i == 0)
        def _():
            dk_s[...] = jnp.zeros(dk_s.shape, F32)
            dv_s[...] = jnp.zeros(dv_s.shape, F32)

        @pl.when(i >= j)
        def _():
            krow = lax.broadcasted_iota(jnp.int32, (tk, tk), 0) + j * tk
            qcol = lax.broadcasted_iota(jnp.int32, (tk, tk), 1) + i * tk
            keep = krow <= qcol
            lane = lax.broadcasted_iota(jnp.int32, (tk, 128), 1)
            vv = v_ref[...]
            do = do_ref[...]
            for hh in range(2):
                sl = slice(128 * hh, 128 * hh + 128)
                qq = q_ref[:, sl]
                st = lax.dot_general(k_ref[:, sl], qq, NT, preferred_element_type=F32)
                pt = jnp.where(keep, jnp.exp(st - lse_ref[hh]), 0.0)
                do_h = jnp.where((lane < VD) if hh == 0 else (lane >= VD), do, jnp.zeros_like(do))
                dv_s[...] += jnp.dot(pt.astype(BF16), do_h, preferred_element_type=F32)
                dpt = lax.dot_general(vv, do_h, NT, preferred_element_type=F32)
                dst = (pt * (dpt - dl_ref[hh])).astype(BF16)
                dk_s[:, sl] += jnp.dot(dst, qq, preferred_element_type=F32)

        @pl.when(i == nk - 1)
        def _():
            dk_ref[...] = dk_s[...].astype(BF16)
            dv_ref[...] = dv_s[...].astype(BF16)

    row_spec = pl.BlockSpec((2, 1, tk), lambda p, j, i: (p, 0, jnp.maximum(i, j)))
    return pl.pallas_call(
        body, name="attn_bwd_dkv", grid=(HEADS // 2, nk, nk),
        in_specs=[pl.BlockSpec((tk, 256), lambda p, j, i: (jnp.maximum(i, j), p)),
                  pl.BlockSpec((tk, 256), lambda p, j, i: (j, p)),
                  pl.BlockSpec((tk, 128), lambda p, j, i: (j, p)),
                  pl.BlockSpec((tk, 128), lambda p, j, i: (jnp.maximum(i, j), p)), row_spec, row_spec],
        out_specs=[pl.BlockSpec((tk, 256), lambda p, j, i: (j, p)), pl.BlockSpec((tk, 128), lambda p, j, i: (j, p))],
        out_shape=[_sds((t, QW), BF16), _sds((t, HEADS * VD), BF16)],
        scratch_shapes=[pltpu.VMEM((tk, 256), F32), pltpu.VMEM((tk, 128), F32)],
        compiler_params=_cp("parallel", "parallel", "arbitrary"))(qs, k, v, dya, lse_row, delta_row)


def _mla_prep_bwd(dqs, dk, dv, za, gq, gkv, wuq, wukv, rc, rs1, rs2):
    t = za.shape[0]
    tm = min(ROW_TILE, t)

    def body(dq_ref, dk_ref, dv_ref, z_ref, gq_ref, gkv_ref, wuq_ref, wukv_ref, c_ref, s1_ref, s2_ref,
             dza_ref, dqp_ref, dkv_ref, dgq_ref, dgkv_ref):
        _acc_init(pl.program_id(0), dgq_ref, dgkv_ref)
        c, s1, s2 = c_ref[...], s1_ref[...], s2_ref[...]
        lane = lax.broadcasted_iota(jnp.int32, (tm, 128), 1)
        rope_lanes = (lane >= NOPE) & (lane < NOPE + ROPE)
        dkr = jnp.zeros((tm, 128), F32)
        for h in range(HEADS):
            sl = slice(128 * h, 128 * h + 128)
            dqp_ref[:, sl] = _rope_bwd(dq_ref[:, sl].astype(F32) * SCALE, c, s1, s2).astype(BF16)
            dkh = dk_ref[:, sl]
            dkv_ref[:, sl] = dkh
            dkr = dkr + jnp.where(rope_lanes, dkh.astype(F32), 0.0)
        dkv_ref[:, QW:] = dv_ref[...]
        z = z_ref[...]
        dcq = lax.dot_general(dqp_ref[...], wuq_ref[...], NT, preferred_element_type=F32)
        dzq, dgq = _rms_bwd(z[:, :QR], gq_ref[...], dcq)
        dckv = lax.dot_general(dkv_ref[...], wukv_ref[...], NT, preferred_element_type=F32)
        dzkv, dgkv = _rms_bwd(z[:, QR:QR + KVR], gkv_ref[...], dckv)
        dgq_ref[...] += dgq
        dgkv_ref[...] += dgkv
        dza_ref[:, :QR] = dzq.astype(BF16)
        dza_ref[:, QR:QR + KVR] = dzkv.astype(BF16)
        dza_ref[:, QR + KVR:] = _rope_bwd(dkr, c, s1, s2).astype(BF16)

    return pl.pallas_call(
        body, name="mla_prep_bwd", grid=(t // tm,),
        in_specs=[_row(tm, QW), _row(tm, QW), _row(tm, HEADS * VD), _row(tm, ZA), _full(1, QR), _full(1, KVR),
                  _full(QR, QW), _full(KVR, KVW), _row(tm, 128), _row(tm, 128), _row(tm, 128)],
        out_specs=[_row(tm, ZA), _row(tm, QW), _row(tm, KVW), _full(1, QR), _full(1, KVR)],
        out_shape=[_sds((t, ZA), BF16), _sds((t, QW), BF16), _sds((t, KVW), BF16), _sds((1, QR), F32),
                   _sds((1, KVR), F32)],
        compiler_params=_cp("arbitrary"))(dqs, dk, dv, za, gq, gkv, wuq, wukv, rc, rs1, rs2)


def _in_proj_bwd(dza, dzb, wa, wb, x, dx2, g_pre):
    t = x.shape[0]
    tm = min(ROW_TILE, t)

    def body(dza_ref, dzb_ref, wa_ref, wb_ref, x_ref, dx2_ref, g_ref, dx_ref, dg_ref):
        _acc_init(pl.program_id(0), dg_ref)
        dh = (lax.dot_general(dza_ref[...], wa_ref[...], NT, preferred_element_type=F32)
              + lax.dot_general(dzb_ref[...], wb_ref[...], NT, preferred_element_type=F32))
        dx, dg = _rms_bwd(x_ref[...], g_ref[...], dh)
        dg_ref[...] += dg
        dx_ref[...] = dx2_ref[...] + dx

    return pl.pallas_call(
        body, name="in_proj_bwd", grid=(t // tm,),
        in_specs=[_row(tm, ZA), _row(tm, ZB), _full(D, ZA), _full(D, ZB), _row(tm, D), _row(tm, D), _full(1, D)],
        out_specs=[_row(tm, D), _full(1, D)],
        out_shape=[_sds((t, D), F32), _sds((1, D), F32)],
        compiler_params=_cp("arbitrary"))(dza, dzb, wa, wb, x, dx2, g_pre)


def _mm_tn(a, b, tn, name):
    t, k = a.shape
    n = b.shape[1]
    tt = min(ROW_TILE, t)

    def body(a_ref, b_ref, o_ref):
        _acc_init(pl.program_id(1), o_ref)
        o_ref[...] += lax.dot_general(a_ref[...], b_ref[...], TN, preferred_element_type=F32)

    return pl.pallas_call(
        body, name=name, grid=(n // tn, t // tt),
        in_specs=[pl.BlockSpec((tt, k), lambda j, s: (s, 0)), pl.BlockSpec((tt, tn), lambda j, s: (s, j))],
        out_specs=pl.BlockSpec((k, tn), lambda j, s: (0, j)),
        out_shape=_sds((k, n), F32),
        compiler_params=_cp("parallel", "arbitrary"))(a, b)


def _layer_fwd(x, p, rope_tabs):
    h1, za, zb = _in_proj(x, p["mix_pre_g"], p["w_in_a"], p["w_in_b"])
    cqn, ckvn, qs, k, v = _mla_prep(za, p["q_norm_g"], p["kv_norm_g"], p["w_uq"], p["w_ukv"], *rope_tabs)
    ya, lse = _attn_fwd(qs, k, v)
    mix, yb, yc = _mixer_fwd(zb, ya, p["sg_ln_g"], p["sg_ln_b"], p["w_sp"], p["b_sp"], p["conv_w"], p["out_norm_g"])
    o, x2, h2 = _out_proj(mix, p["w_out"], x, p["mix_post_g"], p["ffn_pre_g"])
    a, b, s = _ffn_up(h2, p["w_gate"], p["w_up"])
    f, x3 = _ffn_down(s, p["w_down"], x2, p["ffn_post_g"])
    saved = dict(x=x, h1=h1, za=za, zb=zb, cqn=cqn, ckvn=ckvn, qs=qs, k=k, v=v, ya=ya, lse=lse, mix=mix, yb=yb, yc=yc,
                 o=o, x2=x2, h2=h2, a=a, b=b, s=s, f=f)
    return x3, saved


def _layer_bwd(dx3, p, sv, rope_tabs):
    t = dx3.shape[0]
    g = {}
    df, da, db, g["ffn_post_g"] = _ffn_down_bwd(dx3, sv["f"], p["ffn_post_g"], p["w_down"], sv["a"], sv["b"])
    g["w_down"] = _mm_tn(sv["s"], df, 512, "dw_down")
    dx2, g["ffn_pre_g"] = _ffn_up_bwd(da, db, p["w_gate"], p["w_up"], sv["x2"], dx3, p["ffn_pre_g"])
    g["w_gate"] = _mm_tn(sv["h2"], da, DFF // 2, "dw_gate")
    g["w_up"] = _mm_tn(sv["h2"], db, DFF // 2, "dw_up")
    do, dmix, g["mix_post_g"] = _out_proj_bwd(dx2, sv["o"], p["mix_post_g"], p["w_out"])
    g["w_out"] = _mm_tn(sv["mix"], do, D, "dw_out")
    (dya, dzb, delta, g["out_norm_g"], g["sg_ln_g"], g["sg_ln_b"], g["w_sp"], g["b_sp_t"], g["conv_w"]) = _mixer_bwd(
        dmix, sv["ya"], sv["yb"], sv["yc"], sv["zb"], p["sg_ln_g"], p["sg_ln_b"], p["w_sp"], p["b_sp"], p["conv_w"],
        p["out_norm_g"])
    dqs = _attn_bwd_dq(sv["qs"], sv["k"], sv["v"], dya, sv["lse"], delta)
    dk, dv = _attn_bwd_dkv(sv["qs"], sv["k"], sv["v"], dya, sv["lse"].reshape(HEADS, 1, t), delta.reshape(HEADS, 1, t))
    dza, dqp, dkv, g["q_norm_g"], g["kv_norm_g"] = _mla_prep_bwd(
        dqs, dk, dv, sv["za"], p["q_norm_g"], p["kv_norm_g"], p["w_uq"], p["w_ukv"], *rope_tabs)
    g["w_uq"] = _mm_tn(sv["cqn"], dqp, QW, "dw_uq")
    g["w_ukv"] = _mm_tn(sv["ckvn"], dkv, KVW, "dw_ukv")
    g["w_in_a"] = _mm_tn(sv["h1"], dza, ZA, "dw_in_a")
    g["w_in_b"] = _mm_tn(sv["h1"], dzb, ZB, "dw_in_b")
    dx, g["mix_pre_g"] = _in_proj_bwd(dza, dzb, p["w_in_a"], p["w_in_b"], sv["x"], dx2, p["mix_pre_g"])
    return dx, g


def _rope_tables(positions):
    inv_freq = 1.0 / (ROPE_THETA ** (jnp.arange(0, ROPE // 2, dtype=F32) / (ROPE // 2)))
    ang = positions.astype(F32)[:, None] * inv_freq
    cos, sin = jnp.cos(ang), jnp.sin(ang)
    t = positions.shape[0]
    one, zero = jnp.ones((t, 64), F32), jnp.zeros((t, 16), F32)
    c = jnp.concatenate([one, cos, cos, one[:, :32]], axis=1)
    s1 = jnp.concatenate([zero, zero, zero, zero, -sin, zero, zero, zero], axis=1)
    s2 = jnp.concatenate([zero, zero, zero, zero, zero, sin, zero, zero], axis=1)
    return c, s1, s2


def _layer_params(full, l):
    w_in = full["w_in"][l]
    zpad = lambda n: jnp.zeros((D, n), w_in.dtype)
    p = {
        "w_in_a": jnp.concatenate([w_in[:, :640], zpad(64), w_in[:, 640:672], zpad(32)], axis=1),
        "w_in_b": w_in[:, 672:],
        "w_uq": jnp.pad(full["w_uq"][l].reshape(QR, HEADS, NOPE + ROPE), ((0, 0), (0, 0), (0, 32))).reshape(QR, QW),
        "w_out": full["w_out"][l], "w_gate": full["w_gate"][l], "w_up": full["w_up"][l], "w_down": full["w_down"][l],
        "w_sp": full["w_sp"][l], "conv_w": full["conv_w"][l],
        "b_sp": jnp.repeat(full["b_sp"][l].T, 64, axis=1),
    }
    kv = full["w_ukv"][l].reshape(KVR, HEADS, NOPE + VD)
    p["w_ukv"] = jnp.concatenate([jnp.pad(kv[:, :, :NOPE], ((0, 0), (0, 0), (0, 64))).reshape(KVR, QW),
                                  kv[:, :, NOPE:].reshape(KVR, HEADS * VD)], axis=1)
    for n in ("mix_pre_g", "mix_post_g", "ffn_pre_g", "ffn_post_g", "q_norm_g", "kv_norm_g", "sg_ln_g", "sg_ln_b",
              "out_norm_g"):
        p[n] = full[n][l][None, :]
    return p


def _natural_grads(g):
    ga, gb = g["w_in_a"], g["w_in_b"]
    kv = g["w_ukv"]
    out = {
        "w_in": jnp.concatenate([ga[:, :640], ga[:, 704:736], gb], axis=1),
        "w_uq": g["w_uq"].reshape(QR, HEADS, 128)[:, :, :NOPE + ROPE].reshape(QR, HEADS * (NOPE + ROPE)),
        "w_ukv": jnp.concatenate([kv[:, :QW].reshape(KVR, HEADS, 128)[:, :, :NOPE],
                                  kv[:, QW:].reshape(KVR, HEADS, VD)], axis=2).reshape(KVR, HEADS * (NOPE + VD)),
        "b_sp": g["b_sp_t"].reshape(CHUNK, 4, 64).sum(axis=-1).T,
    }
    for n in ("w_out", "w_gate", "w_up", "w_down", "w_sp", "conv_w"):
        out[n] = g[n]
    for n in ("mix_pre_g", "mix_post_g", "ffn_pre_g", "ffn_post_g", "q_norm_g", "kv_norm_g", "sg_ln_g", "sg_ln_b",
              "out_norm_g"):
        out[n] = g[n][0]
    return out


def _local_step(x, positions, target, full):
    depth = full["w_in"].shape[0]
    tabs = _rope_tables(positions)
    params = [_layer_params(full, l) for l in range(depth)]
    saved = []
    for l in range(depth):
        x, sv = _layer_fwd(x, params[l], tabs)
        saved.append(sv)
    dx, acc = _loss_head(x, target)
    loss = (0.5 / D) * jnp.sum(acc)
    grads = [None] * depth
    for l in reversed(range(depth)):
        dx, g = _layer_bwd(dx, params[l], saved[l], tabs)
        grads[l] = _natural_grads(g)
    return loss, dx, grads


MESH_ID = pl.DeviceIdType.MESH
ANY = pl.BlockSpec(memory_space=pl.ANY)


def _place():
    x, y, c = lax.axis_index("x"), lax.axis_index("y"), lax.axis_index("c")
    chips = [(1 - x, y), (x, 1 - y), (1 - x, 1 - y)]
    return x, y, c, 2 * x + y, chips


def _gather_shards(mine):
    _, r, w = mine.shape

    def body(mine_ref, out_ref, send_sems, recv_sems, fsend_sems, frecv_sems, local_sem):
        x, y, c, k, chips = _place()

        def copy(src, dst, sems, n, to):
            return pltpu.make_async_remote_copy(src_ref=src, dst_ref=dst, send_sem=sems[0].at[n], recv_sem=sems[1].at[n],
                                                device_id=to, device_id_type=MESH_ID)

        ici, d2d = (send_sems, recv_sems), (fsend_sems, frecv_sems)
        local = pltpu.make_async_copy(mine_ref, out_ref.at[k], local_sem)
        local.start()
        sends = [copy(mine_ref.at[c], out_ref.at[k, c], ici, n, (cx, cy, c)) for n, (cx, cy) in enumerate(chips)]
        for cp in sends:
            cp.start()
        passed = [copy(out_ref.at[2 * cx + cy, c], out_ref.at[2 * cx + cy, c], d2d, n, (x, y, 1 - c))
                  for n, (cx, cy) in enumerate(chips)]
        for n, (cx, cy) in enumerate(chips):
            copy(mine_ref.at[c], out_ref.at[2 * cx + cy, c], ici, n, (cx, cy, c)).wait_recv()
            passed[n].start()
        for n, (cx, cy) in enumerate(chips):
            copy(mine_ref.at[1 - c], out_ref.at[2 * cx + cy, 1 - c], d2d, n, (x, y, 1 - c)).wait_recv()
        for cp in sends + passed:
            cp.wait_send()
        local.wait()

    return pl.pallas_call(
        body, name="gather_shards", in_specs=[ANY], out_specs=ANY, out_shape=_sds((4, 2, r, w), mine.dtype),
        scratch_shapes=[pltpu.SemaphoreType.DMA((3,))] * 4 + [pltpu.SemaphoreType.DMA(())],
        )(mine)


def _swap_halves(big, small):
    _, _, r, w = big.shape

    def body(big_ref, small_ref, rbig_ref, rsmall_ref, send_sems, recv_sems):
        x, y, c, _, _ = _place()
        sib = (x, y, 1 - c)
        cps = [pltpu.make_async_remote_copy(src_ref=big_ref.at[:, 1 - c], dst_ref=rbig_ref, send_sem=send_sems.at[0],
                                            recv_sem=recv_sems.at[0], device_id=sib, device_id_type=MESH_ID),
               pltpu.make_async_remote_copy(src_ref=small_ref, dst_ref=rsmall_ref, send_sem=send_sems.at[1],
                                            recv_sem=recv_sems.at[1], device_id=sib, device_id_type=MESH_ID)]
        for cp in cps:
            cp.start()
        for cp in cps:
            cp.wait()

    return pl.pallas_call(
        body, name="swap_halves", in_specs=[ANY, ANY], out_specs=[ANY, ANY],
        out_shape=[_sds((4, r, w), big.dtype), _sds(small.shape, small.dtype)],
        scratch_shapes=[pltpu.SemaphoreType.DMA((2,))] * 2,
        )(big, small)


def _sum_tile(r):
    return max(cand for cand in range(16, 641, 16) if r % cand == 0)


def _pair_sum(big, rbig, small, rsmall, c):
    _, _, r, w = big.shape
    tr = _sum_tile(r)
    ns = small.shape[0]

    def body(c_ref, big_ref, rbig_ref, small_ref, rsmall_ref, p_ref, ps_ref):
        p_ref[...] = (big_ref[...].astype(F32) + rbig_ref[...].astype(F32)).astype(BF16)

        @pl.when((pl.program_id(0) == 0) & (pl.program_id(1) == 0))
        def _():
            ps_ref[...] = small_ref[...] + rsmall_ref[...]

    grid_spec = pltpu.PrefetchScalarGridSpec(
        num_scalar_prefetch=1, grid=(4, r // tr),
        in_specs=[pl.BlockSpec((None, None, tr, w), lambda j, i, cr: (j, cr[0], i, 0)),
                  pl.BlockSpec((None, tr, w), lambda j, i, cr: (j, i, 0)),
                  pl.BlockSpec((ns, 128), lambda j, i, cr: (0, 0)), pl.BlockSpec((ns, 128), lambda j, i, cr: (0, 0))],
        out_specs=[pl.BlockSpec((None, tr, w), lambda j, i, cr: (j, i, 0)), pl.BlockSpec((ns, 128), lambda j, i, cr: (0, 0))])
    return pl.pallas_call(
        body, name="pair_sum", grid_spec=grid_spec,
        out_shape=[_sds((4, r, w), BF16), _sds(small.shape, F32)],
        compiler_params=_cp("arbitrary", "arbitrary"))(c, big, rbig, small, rsmall)


def _chip_exchange(p, ps):
    _, r, w = p.shape
    ns = ps.shape[0]

    def body(p_ref, ps_ref, rb_ref, rs_ref, send_sems, recv_sems, local_sems):
        x, y, c, k, chips = _place()
        loc = [pltpu.make_async_copy(p_ref.at[k], rb_ref.at[k], local_sems.at[0]),
               pltpu.make_async_copy(ps_ref, rs_ref.at[k], local_sems.at[1])]
        for cp in loc:
            cp.start()
        sends = []
        for n, (cx, cy) in enumerate(chips):
            to = (cx, cy, c)
            sends.append(pltpu.make_async_remote_copy(src_ref=p_ref.at[2 * cx + cy], dst_ref=rb_ref.at[k],
                                                      send_sem=send_sems.at[0, n], recv_sem=recv_sems.at[0, n],
                                                      device_id=to, device_id_type=MESH_ID))
            sends.append(pltpu.make_async_remote_copy(src_ref=ps_ref, dst_ref=rs_ref.at[k],
                                                      send_sem=send_sems.at[1, n], recv_sem=recv_sems.at[1, n],
                                                      device_id=to, device_id_type=MESH_ID))
        for cp in sends:
            cp.start()
        for n, (cx, cy) in enumerate(chips):
            to = (cx, cy, c)
            pltpu.make_async_remote_copy(src_ref=p_ref.at[k], dst_ref=rb_ref.at[2 * cx + cy], send_sem=send_sems.at[0, n],
                                         recv_sem=recv_sems.at[0, n], device_id=to, device_id_type=MESH_ID).wait_recv()
            pltpu.make_async_remote_copy(src_ref=ps_ref, dst_ref=rs_ref.at[2 * cx + cy], send_sem=send_sems.at[1, n],
                                         recv_sem=recv_sems.at[1, n], device_id=to, device_id_type=MESH_ID).wait_recv()
        for cp in sends:
            cp.wait_send()
        for cp in loc:
            cp.wait()

    return pl.pallas_call(
        body, name="chip_exchange", in_specs=[ANY, ANY], out_specs=[ANY, ANY],
        out_shape=[_sds((4, r, w), p.dtype), _sds((4, ns, 128), ps.dtype)],
        scratch_shapes=[pltpu.SemaphoreType.DMA((2, 3))] * 2 + [pltpu.SemaphoreType.DMA((2,))],
        )(p, ps)


def _chip_sum(rb, rs):
    _, r, w = rb.shape
    tr = _sum_tile(r)
    ns = rs.shape[1]

    def body(rb_ref, rs_ref, o_ref, os_ref):
        acc = rb_ref[0].astype(F32)
        for j in range(1, 4):
            acc = acc + rb_ref[j].astype(F32)
        o_ref[...] = acc

        @pl.when(pl.program_id(0) == 0)
        def _():
            s = rs_ref[0]
            for j in range(1, 4):
                s = s + rs_ref[j]
            os_ref[...] = s

    return pl.pallas_call(
        body, name="chip_sum", grid=(r // tr,),
        in_specs=[pl.BlockSpec((4, tr, w), lambda i: (0, i, 0)), _full(4, ns, 128)],
        out_specs=[pl.BlockSpec((tr, w), lambda i: (i, 0)), _full(ns, 128)],
        out_shape=[_sds((r, w), F32), _sds((ns, 128), F32)],
        compiler_params=_cp("arbitrary"))(rb, rs)


def _share_with_sibling(red):
    r, w = red.shape

    def body(red_ref, out_ref, send_sem, recv_sem, local_sem):
        x, y, c, _, _ = _place()
        loc = pltpu.make_async_copy(red_ref, out_ref.at[c], local_sem)
        loc.start()
        pltpu.make_async_remote_copy(src_ref=red_ref, dst_ref=out_ref.at[c], send_sem=send_sem, recv_sem=recv_sem,
                                     device_id=(x, y, 1 - c), device_id_type=MESH_ID).start()
        pltpu.make_async_remote_copy(src_ref=red_ref, dst_ref=out_ref.at[1 - c], send_sem=send_sem, recv_sem=recv_sem,
                                     device_id=(x, y, 1 - c), device_id_type=MESH_ID).wait()
        loc.wait()

    return pl.pallas_call(
        body, name="share_with_sibling", in_specs=[ANY], out_specs=ANY, out_shape=_sds((2, r, w), red.dtype),
        scratch_shapes=[pltpu.SemaphoreType.DMA(())] * 3,
        )(red)


def _adamw(w, g, m, v, name):
    r, n = w.shape
    tr = r
    for cand in (512, 256, 128, 64, 32, 16, 8):
        if r % cand == 0:
            tr = cand
            break

    def body(w_ref, g_ref, m_ref, v_ref, d_ref, nm_ref, nv_ref):
        gv = g_ref[...]
        nm = ADAM_B1 * m_ref[...] + (1.0 - ADAM_B1) * gv
        nv = ADAM_B2 * v_ref[...] + (1.0 - ADAM_B2) * (gv * gv)
        m_hat = nm / (1.0 - ADAM_B1 ** ADAM_STEP)
        v_hat = nv / (1.0 - ADAM_B2 ** ADAM_STEP)
        d_ref[...] = -ADAM_LR * (m_hat / (jnp.sqrt(v_hat) + ADAM_EPS) + ADAM_WD * w_ref[...])
        nm_ref[...] = nm
        nv_ref[...] = nv

    blk = pl.BlockSpec((tr, n), lambda i: (i, 0))
    return pl.pallas_call(
        body, name=name, grid=(r // tr,), in_specs=[blk] * 4, out_specs=[blk] * 3,
        out_shape=[_sds((r, n), F32)] * 3, compiler_params=_cp("parallel"))(w, g, m, v)


def _pack_weight_shards(sh):
    depth = sh["w_in"].shape[0]
    parts = [sh[n].astype(BF16).reshape(depth, rows, D) for n, rows in PACK_ROWS]
    conv = lax.bitcast_convert_type(sh["conv_w"].reshape(depth, 3 * 64), BF16).reshape(depth, 1, 384)
    parts.append(jnp.pad(conv, ((0, 0), (0, 0), (0, D - 384))))
    flat = jnp.concatenate(parts, axis=1)
    flat = jnp.pad(flat, ((0, 0), (0, WPACK_ROWS - flat.shape[1]), (0, 0)))
    return flat.reshape(2, depth // 2 * WPACK_ROWS, D)


def _unpack_weights(gathered, depth, shard_shapes):
    flat = gathered.reshape(4, depth, WPACK_ROWS, D)
    full = {}
    off = 0
    for n, rows in PACK_ROWS:
        shp = shard_shapes[n][1:]
        piece = flat[:, :, off:off + rows, :].reshape((4, depth) + shp)
        if n in ("w_out", "w_down"):
            full[n] = jnp.transpose(piece, (1, 0, 2, 3)).reshape(depth, 4 * shp[0], shp[1])
        else:
            full[n] = jnp.transpose(piece, (1, 2, 0, 3)).reshape(depth, shp[0], 4 * shp[1])
        off += rows
    conv = lax.bitcast_convert_type(flat[:, :, off, :384].reshape(4, depth, 192, 2), F32)
    full["conv_w"] = jnp.transpose(conv.reshape(4, depth, 3, 64), (1, 2, 0, 3)).reshape(depth, 3, CVW)
    return full


def _pack_grad_shards(grads):
    depth = len(grads)
    layers = []
    for g in grads:
        parts = []
        for n, rows in PACK_ROWS:
            a = g[n]
            if n in ("w_out", "w_down"):
                parts.append(a.reshape(4, rows, D))
            else:
                parts.append(jnp.transpose(a.reshape(a.shape[0], 4, a.shape[1] // 4), (1, 0, 2)).reshape(4, rows, D))
        layers.append(jnp.concatenate(parts, axis=1).astype(BF16))
    return jnp.stack(layers, axis=1).reshape(4, 2, depth // 2 * PACK_TOTAL, D)


def _pack_small(arrs, names_shapes, depth):
    flat = jnp.concatenate([arrs[n].reshape(depth, -1) for n, _ in names_shapes], axis=1).reshape(-1)
    rows = -(-flat.shape[0] // 1024) * 8
    return jnp.pad(flat, (0, rows * 128 - flat.shape[0])).reshape(rows, 128)


def _unpack_small(packed, names_shapes, depth):
    per_layer = sum(math.prod(s) for _, s in names_shapes)
    flat = packed.reshape(-1)[:depth * per_layer].reshape(depth, per_layer)
    out, off = {}, 0
    for n, s in names_shapes:
        size = math.prod(s)
        out[n] = flat[:, off:off + size].reshape((depth,) + s)
        off += size
    return out


def kernel(x, positions, mix_pre_g, mix_post_g, ffn_pre_g, ffn_post_g, w_in, q_norm_g, w_uq, kv_norm_g, w_ukv, sg_ln_g, sg_ln_b, w_sp, b_sp, conv_w, out_norm_g, w_out, w_gate, w_up, w_down, loss_target, m_mix_pre_g, m_mix_post_g, m_ffn_pre_g, m_ffn_post_g, m_w_in, m_q_norm_g, m_w_uq, m_kv_norm_g, m_w_ukv, m_sg_ln_g, m_sg_ln_b, m_w_sp, m_b_sp, m_conv_w, m_out_norm_g, m_w_out, m_w_gate, m_w_up, m_w_down, v_mix_pre_g, v_mix_post_g, v_ffn_pre_g, v_ffn_post_g, v_w_in, v_q_norm_g, v_w_uq, v_kv_norm_g, v_w_ukv, v_sg_ln_g, v_sg_ln_b, v_w_sp, v_b_sp, v_conv_w, v_out_norm_g, v_w_out, v_w_gate, v_w_up, v_w_down):
    w = dict(mix_pre_g=mix_pre_g, mix_post_g=mix_post_g, ffn_pre_g=ffn_pre_g, ffn_post_g=ffn_post_g, w_in=w_in,
             q_norm_g=q_norm_g, w_uq=w_uq, kv_norm_g=kv_norm_g, w_ukv=w_ukv, sg_ln_g=sg_ln_g, sg_ln_b=sg_ln_b, w_sp=w_sp,
             b_sp=b_sp, conv_w=conv_w, out_norm_g=out_norm_g, w_out=w_out, w_gate=w_gate, w_up=w_up, w_down=w_down)
    m = dict(mix_pre_g=m_mix_pre_g, mix_post_g=m_mix_post_g, ffn_pre_g=m_ffn_pre_g, ffn_post_g=m_ffn_post_g, w_in=m_w_in,
             q_norm_g=m_q_norm_g, w_uq=m_w_uq, kv_norm_g=m_kv_norm_g, w_ukv=m_w_ukv, sg_ln_g=m_sg_ln_g, sg_ln_b=m_sg_ln_b,
             w_sp=m_w_sp, b_sp=m_b_sp, conv_w=m_conv_w, out_norm_g=m_out_norm_g, w_out=m_w_out, w_gate=m_w_gate,
             w_up=m_w_up, w_down=m_w_down)
    v = dict(mix_pre_g=v_mix_pre_g, mix_post_g=v_mix_post_g, ffn_pre_g=v_ffn_pre_g, ffn_post_g=v_ffn_post_g, w_in=v_w_in,
             q_norm_g=v_q_norm_g, w_uq=v_w_uq, kv_norm_g=v_kv_norm_g, w_ukv=v_w_ukv, sg_ln_g=v_sg_ln_g, sg_ln_b=v_sg_ln_b,
             w_sp=v_w_sp, b_sp=v_b_sp, conv_w=v_conv_w, out_norm_g=v_out_norm_g, w_out=v_w_out, w_gate=v_w_gate,
             w_up=v_w_up, w_down=v_w_down)
    depth = w_in.shape[0]
    c = lax.axis_index("c")
    chip = 2 * lax.axis_index("x") + lax.axis_index("y")

    gathered = _gather_shards(_pack_weight_shards(w))
    full = _unpack_weights(gathered, depth, {n: w[n].shape for n, _ in PACK_ROWS})
    for n, _ in SMALL:
        if n != "conv_w":
            full[n] = w[n]

    loss, dx, grads = _local_step(x[0], positions[0], loss_target[0], full)
    loss = lax.psum(loss, ("x", "y", "c"))

    small = _pack_small({n: jnp.stack([g[n] for g in grads]) for n, _ in SMALL}, SMALL, depth)
    big = _pack_grad_shards(grads)
    rbig, rsmall = _swap_halves(big, small)
    p, ps = _pair_sum(big, rbig, small, rsmall, c.reshape(1).astype(jnp.int32))
    rb, rs = _chip_exchange(p, ps)
    red_half, red_small = _chip_sum(rb, rs)
    red = _share_with_sibling(red_half).reshape(depth, PACK_TOTAL, D)
    g_small = _unpack_small(red_small, SMALL, depth)
    g_small["conv_w"] = lax.dynamic_slice_in_dim(g_small["conv_w"], chip * 64, 64, axis=2)
    gw = dict(g_small)
    off = 0
    for n, rows in PACK_ROWS:
        gw[n] = red[:, off:off + rows, :].reshape(w[n].shape)
        off += rows

    delta, new_m, new_v = {}, {}, {}
    for n, _ in PACK_ROWS:
        shp = w[n].shape
        two_d = lambda a: a.reshape(shp[0] * shp[1], shp[2])
        d_, m_, v_ = _adamw(two_d(w[n]), two_d(gw[n]), two_d(m[n]), two_d(v[n]), "adamw_" + n)
        delta[n], new_m[n], new_v[n] = d_.reshape(shp), m_.reshape(shp), v_.reshape(shp)
    small_local = tuple((n, w[n].shape[1:]) for n, _ in SMALL)
    d_, m_, v_ = _adamw(_pack_small(w, small_local, depth), _pack_small(gw, small_local, depth),
                        _pack_small(m, small_local, depth), _pack_small(v, small_local, depth), "adamw_small")
    delta.update(_unpack_small(d_, small_local, depth))
    new_m.update(_unpack_small(m_, small_local, depth))
    new_v.update(_unpack_small(v_, small_local, depth))

    return (loss, dx[None], *[gw[n] for n in WEIGHTS], *[delta[n] for n in WEIGHTS], *[new_m[n] for n in WEIGHTS],
            *[new_v[n] for n in WEIGHTS])
```

```python
import math

import jax
import jax.numpy as jnp
from jax import lax
from jax.experimental import pallas as pl
from jax.experimental.pallas import tpu as pltpu

F32 = jnp.float32
BF16 = jnp.bfloat16

D = 1024
HEADS = 8
NOPE = 64
ROPE = 32
VD = 64
QR = 384
KVR = 256
SGW = 256
CVW = 256
CHUNK = 128
DFF = 2816
EPS = 1e-6
ROPE_THETA = 10000.0
LOG2E = 1.4426950408889634
LN2 = 0.6931471805599453
QSCALE = (NOPE + ROPE) ** -0.5 * LOG2E
ZA = 768
ZB = 1280
QW = HEADS * 128
KVW = HEADS * 128 + HEADS * VD
NEG = -1e30
GC0 = 0.7978845608028654
GC1 = 0.044715

ADAM_LR = 0.001
ADAM_B1 = 0.9
ADAM_B2 = 0.999
ADAM_EPS = 1e-08
ADAM_WD = 0.01
ADAM_STEP = 10

V7X_VMEM_LIMIT = 52 * 1024 * 1024
ROW_TILE = 512
ATT_TILE = 512

NT = (((1,), (1,)), ((), ()))
TN = (((0,), (0,)), ((), ()))

PACK = (("w_in", 0, 488), ("w_gate", 512, 704), ("w_up", 1216, 704), ("w_down", 1920, 704), ("w_out", 2624, 256),
        ("w_ukv", 2880, 64), ("w_uq", 2944, 72))
CONV_ROW = 3016
PACK_ROWS = 3024
ROW_SHARDED = ("w_out", "w_down")
SMALL = (("mix_pre_g", (D,)), ("mix_post_g", (D,)), ("ffn_pre_g", (D,)), ("ffn_post_g", (D,)), ("q_norm_g", (QR,)),
         ("kv_norm_g", (KVR,)), ("sg_ln_g", (SGW,)), ("sg_ln_b", (SGW,)), ("w_sp", (4, CHUNK, CHUNK)), ("b_sp", (4, CHUNK)),
         ("conv_w", (3, CVW)), ("out_norm_g", (D,)))
WEIGHTS = ["mix_pre_g", "mix_post_g", "ffn_pre_g", "ffn_post_g", "w_in", "q_norm_g", "w_uq", "kv_norm_g", "w_ukv", "sg_ln_g",
           "sg_ln_b", "w_sp", "b_sp", "conv_w", "out_norm_g", "w_out", "w_gate", "w_up", "w_down"]

MESH_ID = pl.DeviceIdType.MESH
ANY = pl.BlockSpec(memory_space=pl.ANY)


def _cp(*sem):
    return pltpu.CompilerParams(dimension_semantics=sem, vmem_limit_bytes=V7X_VMEM_LIMIT)


def _sds(shape, dtype):
    return jax.ShapeDtypeStruct(shape, dtype)


def _row(tm, n):
    return pl.BlockSpec((tm, n), lambda i: (i, 0))


def _lyr(l, *shape):
    return pl.BlockSpec((None,) + shape, lambda *_: (l,) + (0,) * len(shape))


def _pcall(body, name, grid, ins, in_specs, out_specs, out_shape, sem, scratch=(), prevs=None):
    prevs = {k: v for k, v in (prevs or {}).items() if v is not None}
    order = sorted(prevs)
    n_in = len(ins)

    def wrapped(*refs):
        return body(*refs[:n_in], *refs[n_in + len(order):])

    return pl.pallas_call(
        wrapped, name=name, grid=grid, in_specs=list(in_specs) + [ANY] * len(order), out_specs=out_specs,
        out_shape=out_shape, scratch_shapes=list(scratch),
        input_output_aliases={n_in + i: k for i, k in enumerate(order)},
        compiler_params=_cp(*sem))(*ins, *[prevs[k] for k in order])


def _rms(x, g):
    r = lax.rsqrt(jnp.mean(x * x, axis=-1, keepdims=True) + EPS)
    return x * r * g


def _rms_bwd(x, g, dy):
    r = lax.rsqrt(jnp.mean(x * x, axis=-1, keepdims=True) + EPS)
    xh = x * r
    dg = jnp.sum(dy * xh, axis=0, keepdims=True)
    dxh = dy * g
    dx = r * (dxh - xh * jnp.mean(dxh * xh, axis=-1, keepdims=True))
    return dx, dg


def _gelu(x):
    return 0.5 * x * (1.0 + jnp.tanh(GC0 * (x + GC1 * x * x * x)))


def _gelu_grad(x):
    t = jnp.tanh(GC0 * (x + GC1 * x * x * x))
    return 0.5 * (1.0 + t) + 0.5 * x * (1.0 - t * t) * GC0 * (1.0 + 3.0 * GC1 * x * x)


def _rope(xb, c, s1, s2):
    return xb * c + pltpu.roll(xb, 112, 1) * s1 + pltpu.roll(xb, 16, 1) * s2


def _rope_bwd(dy, c, s1, s2):
    return dy * c + pltpu.roll(dy * s1, 16, 1) + pltpu.roll(dy * s2, 112, 1)


def _group_masks(shape):
    lane = lax.broadcasted_iota(jnp.int32, shape, 1)
    return [(lane >= 64 * g) & (lane < 64 * g + 64) for g in range(shape[1] // 64)]


def _group_mean(v, masks):
    out = jnp.zeros_like(v)
    for m in masks:
        s = jnp.sum(jnp.where(m, v, 0.0), axis=-1, keepdims=True) * (1.0 / 64.0)
        out = jnp.where(m, s, out)
    return out


def _pick_row(blk, idx):
    row = lax.broadcasted_iota(jnp.int32, blk.shape, 0)
    return jnp.sum(jnp.where(row == idx, blk, 0.0), axis=0, keepdims=True)


def _shift_down(y, k, first_rows):
    out = pltpu.roll(y, k, 0)
    row = lax.broadcasted_iota(jnp.int32, y.shape, 0)
    for idx in range(k):
        out = jnp.where(row == idx, first_rows[idx], out)
    return out


def _shift_up(y, k, last_rows):
    n = y.shape[0]
    out = pltpu.roll(y, n - k, 0)
    row = lax.broadcasted_iota(jnp.int32, y.shape, 0)
    for idx in range(k):
        out = jnp.where(row == n - k + idx, last_rows[idx], out)
    return out


def _tril_mask():
    r = lax.broadcasted_iota(jnp.int32, (CHUNK, CHUNK), 0)
    c = lax.broadcasted_iota(jnp.int32, (CHUNK, CHUNK), 1)
    return r >= c


def _sgu_forward(zu, zv, g_ln, b_ln, wc_bf, bsp, masks, cmasks):
    u = _gelu(zu)
    vv = _gelu(zv)
    mu = _group_mean(vv, masks)
    dv = vv - mu
    rs = lax.rsqrt(_group_mean(dv * dv, masks) + EPS)
    xh = dv * rs
    vn = xh * g_ln + b_ln
    chunks = []
    for ci in range(zu.shape[0] // CHUNK):
        vc = vn[ci * CHUNK:(ci + 1) * CHUNK, :]
        acc = bsp
        for g in range(4):
            acc = acc + jnp.dot(wc_bf[g], jnp.where(cmasks[g], vc, 0.0).astype(BF16), preferred_element_type=F32)
        chunks.append(acc)
    mixed = jnp.concatenate(chunks, axis=0) if len(chunks) > 1 else chunks[0]
    return u, vv, xh, rs, vn, mixed


def _conv_forward(gc, hh, prev_gc, prev_hh, first_tile, cw):
    yv = gc * hh
    prev = jnp.where(first_tile, 0.0, prev_gc * prev_hh)
    p6, p7 = _pick_row(prev, 6), _pick_row(prev, 7)
    sh1 = _shift_down(yv, 1, [p7])
    sh2 = _shift_down(yv, 2, [p6, p7])
    conv = sh2 * cw[0:1, :] + sh1 * cw[1:2, :] + yv * cw[2:3, :]
    return yv, sh1, sh2, conv


def _acc_init(step, *refs):
    @pl.when(step == 0)
    def _():
        for r in refs:
            r[...] = jnp.zeros(r.shape, r.dtype)


def _in_proj(x, p, l):
    t = x.shape[0]
    tm = min(ROW_TILE, t)

    def body(x_ref, g_ref, wa_ref, wb_ref, h_ref, za_ref, zb_ref):
        h = _rms(x_ref[...], g_ref[...]).astype(BF16)
        h_ref[...] = h
        za_ref[...] = jnp.dot(h, wa_ref[...], preferred_element_type=F32)
        zb_ref[...] = jnp.dot(h, wb_ref[...], preferred_element_type=F32)

    return _pcall(
        body, "in_proj", (t // tm,), [x, p["mix_pre_g"], p["w_in_a"], p["w_in_b"]],
        [_row(tm, D), _lyr(l, 1, D), _lyr(l, D, ZA), _lyr(l, D, ZB)],
        [_row(tm, D), _row(tm, ZA), _row(tm, ZB)],
        [_sds((t, D), BF16), _sds((t, ZA), F32), _sds((t, ZB), F32)], ("parallel",))


def _mla_prep(za, p, l, tabs):
    t = za.shape[0]
    tm = min(ROW_TILE, t)

    def body(z_ref, gq_ref, gkv_ref, wuq_ref, wukv_ref, c_ref, s1_ref, s2_ref, cq_ref, ckv_ref, q_ref, k_ref, v_ref):
        z = z_ref[...]
        cq = _rms(z[:, :QR], gq_ref[...]).astype(BF16)
        ckv = _rms(z[:, QR:QR + KVR], gkv_ref[...]).astype(BF16)
        cq_ref[...] = cq
        ckv_ref[...] = ckv
        c, s1, s2 = c_ref[...], s1_ref[...], s2_ref[...]
        kr = _rope(z[:, QR + KVR:], c, s1, s2)
        q = jnp.dot(cq, wuq_ref[...], preferred_element_type=F32)
        kv = jnp.dot(ckv, wukv_ref[...], preferred_element_type=F32)
        for h in range(HEADS):
            sl = slice(128 * h, 128 * h + 128)
            q_ref[:, sl] = (_rope(q[:, sl], c, s1, s2) * QSCALE).astype(BF16)
            k_ref[:, sl] = (kv[:, sl] + kr).astype(BF16)
        v_ref[...] = kv[:, QW:].astype(BF16)

    return _pcall(
        body, "mla_prep", (t // tm,), [za, p["q_norm_g"], p["kv_norm_g"], p["w_uq"], p["w_ukv"], *tabs],
        [_row(tm, ZA), _lyr(l, 1, QR), _lyr(l, 1, KVR), _lyr(l, QR, QW), _lyr(l, KVR, KVW),
         _row(tm, 128), _row(tm, 128), _row(tm, 128)],
        [_row(tm, QR), _row(tm, KVR), _row(tm, QW), _row(tm, QW), _row(tm, HEADS * VD)],
        [_sds((t, QR), BF16), _sds((t, KVR), BF16), _sds((t, QW), BF16), _sds((t, QW), BF16),
         _sds((t, HEADS * VD), BF16)], ("parallel",))


def _att_tile(t):
    return min(ATT_TILE, max(t // 2, 128))


def _causal_keep(tq, i, j):
    row = lax.broadcasted_iota(jnp.int32, (tq, tq), 0) + i * tq
    col = lax.broadcasted_iota(jnp.int32, (tq, tq), 1) + j * tq
    return col <= row


def _attn_fwd(qs, k, v):
    t = qs.shape[0]
    tq = _att_tile(t)
    nq = t // tq
    rep = tq // 128

    def body(q_ref, k_ref, v_ref, o_ref, lse_ref, m_s, l_s, acc_s):
        i, j = pl.program_id(1), pl.program_id(2)

        @pl.when(j == 0)
        def _():
            m_s[...] = jnp.full(m_s.shape, NEG, F32)
            l_s[...] = jnp.zeros(l_s.shape, F32)
            acc_s[...] = jnp.zeros(acc_s.shape, F32)

        def step(masked):
            vv = v_ref[...]
            keep = _causal_keep(tq, i, j) if masked else None
            for hh in range(2):
                sl = slice(128 * hh, 128 * hh + 128)
                s = lax.dot_general(q_ref[:, sl], k_ref[:, sl], NT, preferred_element_type=F32)
                if masked:
                    s = jnp.where(keep, s, NEG)
                m_old = m_s[hh]
                m_new = jnp.maximum(m_old, jnp.max(s, axis=-1, keepdims=True))
                alpha = jnp.exp2(m_old - m_new)
                p = jnp.exp2(s - jnp.tile(m_new, (1, rep)))
                l_s[hh] = alpha * l_s[hh] + jnp.sum(p, axis=-1, keepdims=True)
                acc_s[hh] = alpha * acc_s[hh] + jnp.dot(p.astype(BF16), vv, preferred_element_type=F32)
                m_s[hh] = m_new

        @pl.when(j < i)
        def _():
            step(False)

        @pl.when(j == i)
        def _():
            step(True)
            lane = lax.broadcasted_iota(jnp.int32, (tq, 128), 1)
            o_ref[...] = jnp.where(lane < VD, acc_s[0] / l_s[0], acc_s[1] / l_s[1])
            for hh in range(2):
                lse_ref[hh] = (m_s[hh] + jnp.log2(l_s[hh]))[:, 0:1]

    return pl.pallas_call(
        body, name="attn_fwd", grid=(HEADS // 2, nq, nq),
        in_specs=[pl.BlockSpec((tq, 256), lambda p, i, j: (i, p)),
                  pl.BlockSpec((tq, 256), lambda p, i, j: (jnp.minimum(j, i), p)),
                  pl.BlockSpec((tq, 128), lambda p, i, j: (jnp.minimum(j, i), p))],
        out_specs=[pl.BlockSpec((tq, 128), lambda p, i, j: (i, p)),
                   pl.BlockSpec((2, tq, 1), lambda p, i, j: (p, i, 0))],
        out_shape=[_sds((t, HEADS * VD), F32), _sds((HEADS, t, 1), F32)],
        scratch_shapes=[pltpu.VMEM((2, tq, 128), F32), pltpu.VMEM((2, tq, 128), F32), pltpu.VMEM((2, tq, 128), F32)],
        compiler_params=_cp("parallel", "parallel", "arbitrary"))(qs, k, v)


def _mixer_fwd(zb, ya, p, l):
    t = zb.shape[0]
    tm = min(ROW_TILE, t)
    hb = tm // 8

    def body(zb_ref, zprev_ref, ya_ref, gln_ref, bln_ref, wsp_ref, bsp_ref, cw_ref, go_ref, mix_ref, yb_ref, yc_ref):
        i = pl.program_id(0)
        masks = _group_masks((tm, SGW))
        cmasks = _group_masks((CHUNK, SGW))
        tril = _tril_mask()
        wc_bf = [jnp.where(tril, wsp_ref[g], 0.0).astype(BF16) for g in range(4)]
        u, _, _, _, _, mixed = _sgu_forward(zb_ref[:, 0:256], zb_ref[:, 256:512], gln_ref[...], bln_ref[...], wc_bf,
                                            bsp_ref[...], masks, cmasks)
        yb = u * mixed
        _, _, _, conv = _conv_forward(zb_ref[:, 768:1024], zb_ref[:, 1024:1280], zprev_ref[:, 768:1024],
                                      zprev_ref[:, 1024:1280], i == 0, cw_ref[...])
        yc = zb_ref[:, 512:768] * conv
        yb_ref[...] = yb
        yc_ref[...] = yc
        go = go_ref[...]
        mix_ref[:, 0:512] = _rms(ya_ref[...], go[:, 0:512]).astype(BF16)
        mix_ref[:, 512:768] = _rms(yb, go[:, 512:768]).astype(BF16)
        mix_ref[:, 768:1024] = _rms(yc, go[:, 768:1024]).astype(BF16)

    return _pcall(
        body, "mixer_fwd", (t // tm,),
        [zb, zb, ya, p["sg_ln_g"], p["sg_ln_b"], p["w_sp"], p["b_sp"], p["conv_w"], p["out_norm_g"]],
        [_row(tm, ZB), pl.BlockSpec((8, ZB), lambda i: (jnp.maximum(i * hb - 1, 0), 0)), _row(tm, 512),
         _lyr(l, 1, SGW), _lyr(l, 1, SGW), _lyr(l, 4, CHUNK, CHUNK), _lyr(l, CHUNK, SGW), _lyr(l, 3, CVW), _lyr(l, 1, D)],
        [_row(tm, D), _row(tm, SGW), _row(tm, CVW)],
        [_sds((t, D), BF16), _sds((t, SGW), F32), _sds((t, CVW), F32)], ("parallel",))


def _out_proj(mix, x, p, l):
    t = x.shape[0]
    tm = min(ROW_TILE, t)

    def body(mix_ref, w_ref, x_ref, gp_ref, gf_ref, o_ref, x2_ref, h2_ref):
        o = jnp.dot(mix_ref[...], w_ref[...], preferred_element_type=F32)
        o_ref[...] = o
        x2 = x_ref[...] + _rms(o, gp_ref[...])
        x2_ref[...] = x2
        h2_ref[...] = _rms(x2, gf_ref[...]).astype(BF16)

    return _pcall(
        body, "out_proj", (t // tm,), [mix, p["w_out"], x, p["mix_post_g"], p["ffn_pre_g"]],
        [_row(tm, D), _lyr(l, D, D), _row(tm, D), _lyr(l, 1, D), _lyr(l, 1, D)],
        [_row(tm, D), _row(tm, D), _row(tm, D)],
        [_sds((t, D), F32), _sds((t, D), F32), _sds((t, D), BF16)], ("parallel",))


def _ffn_up(h2, p, l):
    t = h2.shape[0]
    tm = min(ROW_TILE, t)
    tn = DFF // 2

    def body(h_ref, wg_ref, wu_ref, a_ref, b_ref, s_ref):
        h = h_ref[...]
        a = jnp.dot(h, wg_ref[...], preferred_element_type=F32)
        b = jnp.dot(h, wu_ref[...], preferred_element_type=F32)
        a_ref[...] = a.astype(BF16)
        b_ref[...] = b.astype(BF16)
        s_ref[...] = (a * (1.0 / (1.0 + jnp.exp(-a))) * b).astype(BF16)

    blk = pl.BlockSpec((tm, tn), lambda n, i: (i, n))
    wblk = pl.BlockSpec((None, D, tn), lambda n, i: (l, 0, n))
    return _pcall(
        body, "ffn_up", (DFF // tn, t // tm), [h2, p["w_gate"], p["w_up"]],
        [pl.BlockSpec((tm, D), lambda n, i: (i, 0)), wblk, wblk], [blk, blk, blk],
        [_sds((t, DFF), BF16)] * 3, ("parallel", "parallel"))


def _ffn_down(s, x2, p, l):
    t = x2.shape[0]
    tm = min(ROW_TILE, t)

    def body(s_ref, w_ref, x_ref, g_ref, f_ref, x3_ref):
        f = jnp.dot(s_ref[...], w_ref[...], preferred_element_type=F32)
        f_ref[...] = f
        x3_ref[...] = x_ref[...] + _rms(f, g_ref[...])

    return _pcall(
        body, "ffn_down", (t // tm,), [s, p["w_down"], x2, p["ffn_post_g"]],
        [_row(tm, DFF), _lyr(l, DFF, D), _row(tm, D), _lyr(l, 1, D)], [_row(tm, D), _row(tm, D)],
        [_sds((t, D), F32), _sds((t, D), F32)], ("parallel",))


def _loss_head(y, target):
    t = y.shape[0]
    tm = min(ROW_TILE, t)

    def body(y_ref, t_ref, dy_ref, acc_ref):
        e = y_ref[...] - t_ref[...]
        dy_ref[...] = e * (1.0 / D)
        sq = jnp.sum(e * e, axis=0, keepdims=True)
        part = sq[:, 0:128]
        for b in range(1, D // 128):
            part = part + sq[:, 128 * b:128 * b + 128]
        _acc_init(pl.program_id(0), acc_ref)
        acc_ref[...] += part

    return _pcall(body, "loss_head", (t // tm,), [y, target], [_row(tm, D), _row(tm, D)],
                  [_row(tm, D), pl.BlockSpec((1, 128), lambda i: (0, 0))],
                  [_sds((t, D), F32), _sds((1, 128), F32)], ("arbitrary",))


def _ffn_down_bwd(dx3, sv, p, l, depth, gb):
    t = dx3.shape[0]
    tm = min(256, t)

    def body(dx_ref, f_ref, g_ref, w_ref, a_ref, b_ref, df_ref, da_ref, db_ref, dg_ref):
        _acc_init(pl.program_id(0), dg_ref)
        df, dg = _rms_bwd(f_ref[...], g_ref[...], dx_ref[...])
        dg_ref[...] += dg
        df = df.astype(BF16)
        df_ref[...] = df
        ds = lax.dot_general(df, w_ref[...], NT, preferred_element_type=F32)
        av = a_ref[...].astype(F32)
        sig = 1.0 / (1.0 + jnp.exp(-av))
        da_ref[...] = (ds * b_ref[...].astype(F32) * (sig * (1.0 + av * (1.0 - sig)))).astype(BF16)
        db_ref[...] = (ds * (av * sig)).astype(BF16)

    df, da, db, gb["ffn_post_g"] = _pcall(
        body, "ffn_down_bwd", (t // tm,), [dx3, sv["f"], p["ffn_post_g"], p["w_down"], sv["a"], sv["b"]],
        [_row(tm, D), _row(tm, D), _lyr(l, 1, D), _lyr(l, DFF, D), _row(tm, DFF), _row(tm, DFF)],
        [_row(tm, D), _row(tm, DFF), _row(tm, DFF), _lyr(l, 1, D)],
        [_sds((t, D), BF16), _sds((t, DFF), BF16), _sds((t, DFF), BF16), _sds((depth, 1, D), F32)], ("arbitrary",),
        prevs={3: gb.get("ffn_post_g")})
    return df, da, db


def _ffn_up_bwd(da, db, dx3, sv, p, l, depth, gb):
    t = dx3.shape[0]
    tm = min(256, t)

    def body(da_ref, db_ref, wg_ref, wu_ref, x_ref, dx3_ref, g_ref, dx2_ref, dg_ref):
        _acc_init(pl.program_id(0), dg_ref)
        dh = (lax.dot_general(da_ref[...], wg_ref[...], NT, preferred_element_type=F32)
              + lax.dot_general(db_ref[...], wu_ref[...], NT, preferred_element_type=F32))
        dx, dg = _rms_bwd(x_ref[...], g_ref[...], dh)
        dg_ref[...] += dg
        dx2_ref[...] = dx3_ref[...] + dx

    dx2, gb["ffn_pre_g"] = _pcall(
        body, "ffn_up_bwd", (t // tm,), [da, db, p["w_gate"], p["w_up"], sv["x2"], dx3, p["ffn_pre_g"]],
        [_row(tm, DFF), _row(tm, DFF), _lyr(l, D, DFF), _lyr(l, D, DFF), _row(tm, D), _row(tm, D), _lyr(l, 1, D)],
        [_row(tm, D), _lyr(l, 1, D)], [_sds((t, D), F32), _sds((depth, 1, D), F32)], ("arbitrary",),
        prevs={1: gb.get("ffn_pre_g")})
    return dx2


def _out_proj_bwd(dx2, sv, p, l, depth, gb):
    t = dx2.shape[0]
    tm = min(ROW_TILE, t)

    def body(dx_ref, o_ref, g_ref, w_ref, do_ref, dmix_ref, dg_ref):
        _acc_init(pl.program_id(0), dg_ref)
        do, dg = _rms_bwd(o_ref[...], g_ref[...], dx_ref[...])
        dg_ref[...] += dg
        do = do.astype(BF16)
        do_ref[...] = do
        dmix_ref[...] = lax.dot_general(do, w_ref[...], NT, preferred_element_type=F32)

    do, dmix, gb["mix_post_g"] = _pcall(
        body, "out_proj_bwd", (t // tm,), [dx2, sv["o"], p["mix_post_g"], p["w_out"]],
        [_row(tm, D), _row(tm, D), _lyr(l, 1, D), _lyr(l, D, D)], [_row(tm, D), _row(tm, D), _lyr(l, 1, D)],
        [_sds((t, D), BF16), _sds((t, D), F32), _sds((depth, 1, D), F32)], ("arbitrary",),
        prevs={2: gb.get("mix_post_g")})
    return do, dmix


def _mixer_bwd(dmix, sv, p, l, depth, gb):
    zb = sv["zb"]
    t = zb.shape[0]
    tm = min(ROW_TILE, t)
    hb = tm // 8
    last_blk = t // 8 - 1
    nsteps = t // tm

    def body(dmix_ref, ya_ref, yb_ref, yc_ref, zb_ref, zprev_ref, znext_ref, ycn_ref, dmn_ref,
             gln_ref, bln_ref, wsp_ref, bsp_ref, cw_ref, go_ref,
             dya_ref, dzb_ref, delta_ref, dgo_ref, dgln_ref, dbln_ref, dwsp_ref, dbsp_ref, dcw_ref):
        i = pl.program_id(0)
        _acc_init(i, dgo_ref, dgln_ref, dbln_ref, dwsp_ref, dbsp_ref, dcw_ref)
        go = go_ref[...]
        dmix = dmix_ref[...]

        ya = ya_ref[...]
        dya, dga = _rms_bwd(ya, go[:, 0:512], dmix[:, 0:512])
        dyb, dgb_ = _rms_bwd(yb_ref[...], go[:, 512:768], dmix[:, 512:768])
        dyc, dgc_ = _rms_bwd(yc_ref[...], go[:, 768:1024], dmix[:, 768:1024])
        dgo_ref[:, 0:512] += dga
        dgo_ref[:, 512:768] += dgb_
        dgo_ref[:, 768:1024] += dgc_
        dya = dya * LN2
        dya_ref[...] = dya.astype(BF16)
        prod = dya * ya
        hmasks = _group_masks((tm, 512))
        for h in range(HEADS):
            delta_ref[h] = jnp.sum(jnp.where(hmasks[h], prod, 0.0), axis=-1, keepdims=True)

        masks = _group_masks((tm, SGW))
        cmasks = _group_masks((CHUNK, SGW))
        tril = _tril_mask()
        wc_bf = [jnp.where(tril, wsp_ref[g], 0.0).astype(BF16) for g in range(4)]
        zu, zv = zb_ref[:, 0:256], zb_ref[:, 256:512]
        g_ln = gln_ref[...]
        u, _, xh, rs, vn, mixed = _sgu_forward(zu, zv, g_ln, bln_ref[...], wc_bf, bsp_ref[...], masks, cmasks)
        du = dyb * mixed
        dmixed = dyb * u
        dvn_chunks = []
        dbsp = jnp.zeros((CHUNK, SGW), F32)
        for ci in range(tm // CHUNK):
            rows = slice(ci * CHUNK, (ci + 1) * CHUNK)
            dm_c = dmixed[rows, :]
            vn_c = vn[rows, :].astype(BF16)
            dbsp = dbsp + dm_c
            dvn_c = jnp.zeros((CHUNK, SGW), F32)
            for g in range(4):
                dm_g = jnp.where(cmasks[g], dm_c, 0.0).astype(BF16)
                dw = lax.dot_general(dm_g, vn_c, NT, preferred_element_type=F32)
                dwsp_ref[g] += jnp.where(tril, dw, 0.0)
                dvn_c = dvn_c + lax.dot_general(wc_bf[g], dm_g, TN, preferred_element_type=F32)
            dvn_chunks.append(dvn_c)
        dbsp_ref[...] += dbsp
        dvn = jnp.concatenate(dvn_chunks, axis=0) if len(dvn_chunks) > 1 else dvn_chunks[0]
        dgln_ref[...] += jnp.sum(dvn * xh, axis=0, keepdims=True)
        dbln_ref[...] += jnp.sum(dvn, axis=0, keepdims=True)
        dxh = dvn * g_ln
        dvv = rs * (dxh - _group_mean(dxh, masks) - xh * _group_mean(dxh * xh, masks))
        dzb_ref[:, 0:256] = (du * _gelu_grad(zu)).astype(BF16)
        dzb_ref[:, 256:512] = (dvv * _gelu_grad(zv)).astype(BF16)

        cwv = cw_ref[...]
        gb_, gc, hh = zb_ref[:, 512:768], zb_ref[:, 768:1024], zb_ref[:, 1024:1280]
        yv, sh1, sh2, conv = _conv_forward(gc, hh, zprev_ref[:, 768:1024], zprev_ref[:, 1024:1280], i == 0, cwv)
        dconv = dyc * gb_
        dzb_ref[:, 512:768] = (dyc * conv).astype(BF16)
        dcw_ref[0:1, :] += jnp.sum(dconv * sh2, axis=0, keepdims=True)
        dcw_ref[1:2, :] += jnp.sum(dconv * sh1, axis=0, keepdims=True)
        dcw_ref[2:3, :] += jnp.sum(dconv * yv, axis=0, keepdims=True)
        dycn, _ = _rms_bwd(ycn_ref[...], go[:, 768:1024], dmn_ref[...])
        dconv_next = jnp.where(i == nsteps - 1, 0.0, dycn * znext_ref[:, 512:768])
        n0, n1 = _pick_row(dconv_next, 0), _pick_row(dconv_next, 1)
        dyv = dconv * cwv[2:3, :] + _shift_up(dconv, 1, [n0]) * cwv[1:2, :] + _shift_up(dconv, 2, [n0, n1]) * cwv[0:1, :]
        dzb_ref[:, 768:1024] = (dyv * hh).astype(BF16)
        dzb_ref[:, 1024:1280] = (dyv * gc).astype(BF16)

    prev_map = lambda i: (jnp.maximum(i * hb - 1, 0), 0)
    next_map = lambda i: (jnp.minimum((i + 1) * hb, last_blk), 0)
    names = ("out_norm_g", "sg_ln_g", "sg_ln_b", "w_sp", "b_sp_t", "conv_w")
    shapes = ((1, D), (1, SGW), (1, SGW), (4, CHUNK, CHUNK), (CHUNK, SGW), (3, CVW))
    outs = _pcall(
        body, "mixer_bwd", (nsteps,),
        [dmix, sv["ya"], sv["yb"], sv["yc"], zb, zb, zb, sv["yc"], dmix, p["sg_ln_g"], p["sg_ln_b"], p["w_sp"], p["b_sp"],
         p["conv_w"], p["out_norm_g"]],
        [_row(tm, D), _row(tm, 512), _row(tm, SGW), _row(tm, CVW), _row(tm, ZB),
         pl.BlockSpec((8, ZB), prev_map), pl.BlockSpec((8, ZB), next_map), pl.BlockSpec((8, CVW), next_map),
         pl.BlockSpec((8, 256), lambda i: (jnp.minimum((i + 1) * hb, last_blk), 3)),
         _lyr(l, 1, SGW), _lyr(l, 1, SGW), _lyr(l, 4, CHUNK, CHUNK), _lyr(l, CHUNK, SGW), _lyr(l, 3, CVW), _lyr(l, 1, D)],
        [_row(tm, 512), _row(tm, ZB), pl.BlockSpec((HEADS, tm, 1), lambda i: (0, i, 0))] + [_lyr(l, *s) for s in shapes],
        [_sds((t, 512), BF16), _sds((t, ZB), BF16), _sds((HEADS, t, 1), F32)] + [_sds((depth,) + s, F32) for s in shapes],
        ("arbitrary",), prevs={3 + n: gb.get(name) for n, name in enumerate(names)})
    for n, name in enumerate(names):
        gb[name] = outs[3 + n]
    return outs[0], outs[1], outs[2]


def _attn_bwd_dq(qs, k, v, dya, lse, delta):
    t = qs.shape[0]
    tq = _att_tile(t)
    nq = t // tq

    def body(q_ref, k_ref, v_ref, do_ref, lse_ref, dl_ref, dq_ref, acc_s):
        i, j = pl.program_id(1), pl.program_id(2)

        @pl.when(j == 0)
        def _():
            acc_s[...] = jnp.zeros(acc_s.shape, F32)

        def step(masked):
            keep = _causal_keep(tq, i, j) if masked else None
            lane = lax.broadcasted_iota(jnp.int32, (tq, 128), 1)
            vv = v_ref[...]
            do = do_ref[...]
            for hh in range(2):
                sl = slice(128 * hh, 128 * hh + 128)
                kk = k_ref[:, sl]
                s = lax.dot_general(q_ref[:, sl], kk, NT, preferred_element_type=F32)
                p = jnp.exp2(s - lse_ref[hh])
                if masked:
                    p = jnp.where(keep, p, 0.0)
                do_h = jnp.where((lane < VD) if hh == 0 else (lane >= VD), do, jnp.zeros_like(do))
                dp = lax.dot_general(do_h, vv, NT, preferred_element_type=F32)
                ds = (p * (dp - dl_ref[hh])).astype(BF16)
                acc_s[:, sl] += jnp.dot(ds, kk, preferred_element_type=F32)

        @pl.when(j < i)
        def _():
            step(False)

        @pl.when(j == i)
        def _():
            step(True)
            dq_ref[...] = acc_s[...].astype(BF16)

    col_spec = pl.BlockSpec((2, tq, 1), lambda p, i, j: (p, i, 0))
    return pl.pallas_call(
        body, name="attn_bwd_dq", grid=(HEADS // 2, nq, nq),
        in_specs=[pl.BlockSpec((tq, 256), lambda p, i, j: (i, p)),
                  pl.BlockSpec((tq, 256), lambda p, i, j: (jnp.minimum(j, i), p)),
                  pl.BlockSpec((tq, 128), lambda p, i, j: (jnp.minimum(j, i), p)),
                  pl.BlockSpec((tq, 128), lambda p, i, j: (i, p)), col_spec, col_spec],
        out_specs=pl.BlockSpec((tq, 256), lambda p, i, j: (i, p)),
        out_shape=_sds((t, QW), BF16),
        scratch_shapes=[pltpu.VMEM((tq, 256), F32)],
        compiler_params=_cp("parallel", "parallel", "arbitrary"))(qs, k, v, dya, lse, delta)


def _attn_bwd_dkv(qs, k, v, dya, lse_row, delta_row):
    t = qs.shape[0]
    tk = _att_tile(t)
    nk = t // tk

    def body(q_ref, k_ref, v_ref, do_ref, lse_ref, dl_ref, dk_ref, dv_ref, dk_s, dv_s):
        j, i = pl.program_id(1), pl.program_id(2)

        def step(masked):
            lane = lax.broadcasted_iota(jnp.int32, (tk, 128), 1)
            vv = v_ref[...]
            do = do_ref[...]
            if masked:
                krow = lax.broadcasted_iota(jnp.int32, (tk, tk), 0)
                qcol = lax.broadcasted_iota(jnp.int32, (tk, tk), 1)
                keep = krow <= qcol
            for hh in range(2):
                sl = slice(128 * hh, 128 * hh + 128)
                qq = q_ref[:, sl]
                st = lax.dot_general(k_ref[:, sl], qq, NT, preferred_element_type=F32)
                pt = jnp.exp2(st - lse_ref[hh])
                if masked:
                    pt = jnp.where(keep, pt, 0.0)
                do_h = jnp.where((lane < VD) if hh == 0 else (lane >= VD), do, jnp.zeros_like(do))
                dv_s[...] += jnp.dot(pt.astype(BF16), do_h, preferred_element_type=F32)
                dpt = lax.dot_general(vv, do_h, NT, preferred_element_type=F32)
                dst = (pt * (dpt - dl_ref[hh])).astype(BF16)
                dk_s[:, sl] += jnp.dot(dst, qq, preferred_element_type=F32)

        @pl.when(i == j)
        def _():
            dk_s[...] = jnp.zeros(dk_s.shape, F32)
            dv_s[...] = jnp.zeros(dv_s.shape, F32)
            step(True)

        @pl.when(i > j)
        def _():
            step(False)

        @pl.when(i == nk - 1)
        def _():
            dk_ref[...] = dk_s[...].astype(BF16)
            dv_ref[...] = (dv_s[...] * LOG2E).astype(BF16)

    row_spec = pl.BlockSpec((2, 1, tk), lambda p, j, i: (p, 0, jnp.maximum(i, j)))
    return pl.pallas_call(
        body, name="attn_bwd_dkv", grid=(HEADS // 2, nk, nk),
        in_specs=[pl.BlockSpec((tk, 256), lambda p, j, i: (jnp.maximum(i, j), p)),
                  pl.BlockSpec((tk, 256), lambda p, j, i: (j, p)),
                  pl.BlockSpec((tk, 128), lambda p, j, i: (j, p)),
                  pl.BlockSpec((tk, 128), lambda p, j, i: (jnp.maximum(i, j), p)), row_spec, row_spec],
        out_specs=[pl.BlockSpec((tk, 256), lambda p, j, i: (j, p)), pl.BlockSpec((tk, 128), lambda p, j, i: (j, p))],
        out_shape=[_sds((t, QW), BF16), _sds((t, HEADS * VD), BF16)],
        scratch_shapes=[pltpu.VMEM((tk, 256), F32), pltpu.VMEM((tk, 128), F32)],
        compiler_params=_cp("parallel", "parallel", "arbitrary"))(qs, k, v, dya, lse_row, delta_row)


def _mla_prep_bwd(dqs, dk, dv, sv, p, l, depth, gb, tabs):
    za = sv["za"]
    t = za.shape[0]
    tm = min(ROW_TILE, t)

    def body(dq_ref, dk_ref, dv_ref, z_ref, gq_ref, gkv_ref, wuq_ref, wukv_ref, c_ref, s1_ref, s2_ref,
             dza_ref, dqp_ref, dkv_ref, dgq_ref, dgkv_ref):
        _acc_init(pl.program_id(0), dgq_ref, dgkv_ref)
        c, s1, s2 = c_ref[...], s1_ref[...], s2_ref[...]
        lane = lax.broadcasted_iota(jnp.int32, (tm, 128), 1)
        rope_lanes = (lane >= NOPE) & (lane < NOPE + ROPE)
        dkr = jnp.zeros((tm, 128), F32)
        for h in range(HEADS):
            sl = slice(128 * h, 128 * h + 128)
            dqp_ref[:, sl] = _rope_bwd(dq_ref[:, sl].astype(F32) * QSCALE, c, s1, s2).astype(BF16)
            dkh = dk_ref[:, sl]
            dkv_ref[:, sl] = dkh
            dkr = dkr + jnp.where(rope_lanes, dkh.astype(F32), 0.0)
        dkv_ref[:, QW:] = dv_ref[...]
        z = z_ref[...]
        dcq = lax.dot_general(dqp_ref[...], wuq_ref[...], NT, preferred_element_type=F32)
        dzq, dgq = _rms_bwd(z[:, :QR], gq_ref[...], dcq)
        dckv = lax.dot_general(dkv_ref[...], wukv_ref[...], NT, preferred_element_type=F32)
        dzkv, dgkv = _rms_bwd(z[:, QR:QR + KVR], gkv_ref[...], dckv)
        dgq_ref[...] += dgq
        dgkv_ref[...] += dgkv
        dza_ref[:, :QR] = dzq.astype(BF16)
        dza_ref[:, QR:QR + KVR] = dzkv.astype(BF16)
        dza_ref[:, QR + KVR:] = _rope_bwd(dkr, c, s1, s2).astype(BF16)

    dza, dqp, dkv, gb["q_norm_g"], gb["kv_norm_g"] = _pcall(
        body, "mla_prep_bwd", (t // tm,),
        [dqs, dk, dv, za, p["q_norm_g"], p["kv_norm_g"], p["w_uq"], p["w_ukv"], *tabs],
        [_row(tm, QW), _row(tm, QW), _row(tm, HEADS * VD), _row(tm, ZA), _lyr(l, 1, QR), _lyr(l, 1, KVR),
         _lyr(l, QR, QW), _lyr(l, KVR, KVW), _row(tm, 128), _row(tm, 128), _row(tm, 128)],
        [_row(tm, ZA), _row(tm, QW), _row(tm, KVW), _lyr(l, 1, QR), _lyr(l, 1, KVR)],
        [_sds((t, ZA), BF16), _sds((t, QW), BF16), _sds((t, KVW), BF16), _sds((depth, 1, QR), F32),
         _sds((depth, 1, KVR), F32)], ("arbitrary",), prevs={3: gb.get("q_norm_g"), 4: gb.get("kv_norm_g")})
    return dza, dqp, dkv


def _in_proj_bwd(dza, dzb, dx2, sv, p, l, depth, gb):
    t = dx2.shape[0]
    tm = min(ROW_TILE, t)

    def body(dza_ref, dzb_ref, wa_ref, wb_ref, x_ref, dx2_ref, g_ref, dx_ref, dg_ref):
        _acc_init(pl.program_id(0), dg_ref)
        dh = (lax.dot_general(dza_ref[...], wa_ref[...], NT, preferred_element_type=F32)
              + lax.dot_general(dzb_ref[...], wb_ref[...], NT, preferred_element_type=F32))
        dx, dg = _rms_bwd(x_ref[...], g_ref[...], dh)
        dg_ref[...] += dg
        dx_ref[...] = dx2_ref[...] + dx

    dx, gb["mix_pre_g"] = _pcall(
        body, "in_proj_bwd", (t // tm,), [dza, dzb, p["w_in_a"], p["w_in_b"], sv["x"], dx2, p["mix_pre_g"]],
        [_row(tm, ZA), _row(tm, ZB), _lyr(l, D, ZA), _lyr(l, D, ZB), _row(tm, D), _row(tm, D), _lyr(l, 1, D)],
        [_row(tm, D), _lyr(l, 1, D)], [_sds((t, D), F32), _sds((depth, 1, D), F32)], ("arbitrary",),
        prevs={1: gb.get("mix_pre_g")})
    return dx


def _mm_tn(a, b, tn, name, l, depth, gb):
    t, k = a.shape
    n = b.shape[1]
    tt = min(ROW_TILE, t)

    def body(a_ref, b_ref, o_ref):
        _acc_init(pl.program_id(1), o_ref)
        o_ref[...] += lax.dot_general(a_ref[...], b_ref[...], TN, preferred_element_type=F32)

    gb[name] = _pcall(
        body, "d" + name, (n // tn, t // tt), [a, b],
        [pl.BlockSpec((tt, k), lambda j, s: (s, 0)), pl.BlockSpec((tt, tn), lambda j, s: (s, j))],
        pl.BlockSpec((None, k, tn), lambda j, s: (l, 0, j)), _sds((depth, k, n), F32), ("parallel", "arbitrary"),
        prevs={0: gb.get(name)})


def _layer_fwd(x, p, l, tabs):
    h1, za, zb = _in_proj(x, p, l)
    cqn, ckvn, qs, k, v = _mla_prep(za, p, l, tabs)
    ya, lse = _attn_fwd(qs, k, v)
    mix, yb, yc = _mixer_fwd(zb, ya, p, l)
    o, x2, h2 = _out_proj(mix, x, p, l)
    a, b, s = _ffn_up(h2, p, l)
    f, x3 = _ffn_down(s, x2, p, l)
    saved = dict(x=x, h1=h1, za=za, zb=zb, cqn=cqn, ckvn=ckvn, qs=qs, k=k, v=v, ya=ya, lse=lse, mix=mix, yb=yb, yc=yc,
                 o=o, x2=x2, h2=h2, a=a, b=b, s=s, f=f)
    return x3, saved


def _layer_bwd(dx3, p, sv, l, depth, gb, tabs):
    t = dx3.shape[0]
    df, da, db = _ffn_down_bwd(dx3, sv, p, l, depth, gb)
    _mm_tn(sv["s"], df, 512, "w_down", l, depth, gb)
    dx2 = _ffn_up_bwd(da, db, dx3, sv, p, l, depth, gb)
    _mm_tn(sv["h2"], da, DFF // 2, "w_gate", l, depth, gb)
    _mm_tn(sv["h2"], db, DFF // 2, "w_up", l, depth, gb)
    do, dmix = _out_proj_bwd(dx2, sv, p, l, depth, gb)
    _mm_tn(sv["mix"], do, D, "w_out", l, depth, gb)
    dya, dzb, delta = _mixer_bwd(dmix, sv, p, l, depth, gb)
    dqs = _attn_bwd_dq(sv["qs"], sv["k"], sv["v"], dya, sv["lse"], delta)
    dk, dv = _attn_bwd_dkv(sv["qs"], sv["k"], sv["v"], dya, sv["lse"].reshape(HEADS, 1, t), delta.reshape(HEADS, 1, t))
    dza, dqp, dkv = _mla_prep_bwd(dqs, dk, dv, sv, p, l, depth, gb, tabs)
    _mm_tn(sv["cqn"], dqp, QW, "w_uq", l, depth, gb)
    _mm_tn(sv["ckvn"], dkv, KVW, "w_ukv", l, depth, gb)
    _mm_tn(sv["h1"], dza, ZA, "w_in_a", l, depth, gb)
    _mm_tn(sv["h1"], dzb, ZB, "w_in_b", l, depth, gb)
    return _in_proj_bwd(dza, dzb, dx2, sv, p, l, depth, gb)


def _rope_tables(positions):
    inv_freq = 1.0 / (ROPE_THETA ** (jnp.arange(0, ROPE // 2, dtype=F32) / (ROPE // 2)))
    ang = positions.astype(F32)[:, None] * inv_freq
    cos, sin = jnp.cos(ang), jnp.sin(ang)
    t = positions.shape[0]
    one, zero = jnp.ones((t, 64), F32), jnp.zeros((t, 16), F32)
    c = jnp.concatenate([one, cos, cos, one[:, :32]], axis=1)
    s1 = jnp.concatenate([zero, zero, zero, zero, -sin, zero, zero, zero], axis=1)
    s2 = jnp.concatenate([zero, zero, zero, zero, zero, sin, zero, zero], axis=1)
    return c, s1, s2


def _kernel_params(full):
    w_in = full["w_in"]
    depth = w_in.shape[0]
    zpad = lambda n: jnp.zeros((depth, D, n), w_in.dtype)
    kv = full["w_ukv"].reshape(depth, KVR, HEADS, NOPE + VD)
    p = {
        "w_in_a": jnp.concatenate([w_in[:, :, :640], zpad(64), w_in[:, :, 640:672], zpad(32)], axis=2),
        "w_in_b": w_in[:, :, 672:],
        "w_uq": jnp.pad(full["w_uq"].reshape(depth, QR, HEADS, NOPE + ROPE),
                        ((0, 0), (0, 0), (0, 0), (0, 32))).reshape(depth, QR, QW),
        "w_ukv": jnp.concatenate([jnp.pad(kv[..., :NOPE], ((0, 0), (0, 0), (0, 0), (0, 64))).reshape(depth, KVR, QW),
                                  kv[..., NOPE:].reshape(depth, KVR, HEADS * VD)], axis=2),
        "w_out": full["w_out"], "w_gate": full["w_gate"], "w_up": full["w_up"], "w_down": full["w_down"],
        "w_sp": full["w_sp"], "conv_w": full["conv_w"],
        "b_sp": jnp.repeat(jnp.swapaxes(full["b_sp"], 1, 2), 64, axis=2),
    }
    for n in ("mix_pre_g", "mix_post_g", "ffn_pre_g", "ffn_post_g", "q_norm_g", "kv_norm_g", "sg_ln_g", "sg_ln_b",
              "out_norm_g"):
        p[n] = full[n][:, None, :]
    return p


def _natural_grads(gb):
    depth = gb["w_in_a"].shape[0]
    ga, kv = gb["w_in_a"], gb["w_ukv"]
    out = {
        "w_in": jnp.concatenate([ga[:, :, :640], ga[:, :, 704:736], gb["w_in_b"]], axis=2),
        "w_uq": gb["w_uq"].reshape(depth, QR, HEADS, 128)[..., :NOPE + ROPE].reshape(depth, QR, HEADS * (NOPE + ROPE)),
        "w_ukv": jnp.concatenate([kv[:, :, :QW].reshape(depth, KVR, HEADS, 128)[..., :NOPE],
                                  kv[:, :, QW:].reshape(depth, KVR, HEADS, VD)], axis=3).reshape(depth, KVR, -1),
        "b_sp": jnp.swapaxes(gb["b_sp_t"].reshape(depth, CHUNK, 4, 64).sum(axis=-1), 1, 2),
    }
    for n in ("w_out", "w_gate", "w_up", "w_down", "w_sp", "conv_w"):
        out[n] = gb[n]
    for n in ("mix_pre_g", "mix_post_g", "ffn_pre_g", "ffn_post_g", "q_norm_g", "kv_norm_g", "sg_ln_g", "sg_ln_b",
              "out_norm_g"):
        out[n] = gb[n][:, 0, :]
    return out


def _local_step(x, positions, target, full):
    depth = full["w_in"].shape[0]
    tabs = _rope_tables(positions)
    p = _kernel_params(full)
    saved = []
    for l in range(depth):
        x, sv = _layer_fwd(x, p, l, tabs)
        saved.append(sv)
    dx, acc = _loss_head(x, target)
    loss = (0.5 / D) * jnp.sum(acc)
    gb = {}
    for l in reversed(range(depth)):
        dx = _layer_bwd(dx, p, saved[l], l, depth, gb, tabs)
    return loss, dx, _natural_grads(gb)


def _place():
    x, y, c = lax.axis_index("x"), lax.axis_index("y"), lax.axis_index("c")
    chips = [(1 - x, y), (x, 1 - y), (1 - x, 1 - y)]
    return x, y, c, 2 * x + y, chips


def _gather_shards(mine):
    _, r, w = mine.shape

    def body(mine_ref, out_ref, send_sems, recv_sems, fsend_sems, frecv_sems):
        x, y, c, k, chips = _place()

        def copy(src, dst, sems, n, to):
            return pltpu.make_async_remote_copy(src_ref=src, dst_ref=dst, send_sem=sems[0].at[n], recv_sem=sems[1].at[n],
                                                device_id=to, device_id_type=MESH_ID)

        ici, d2d = (send_sems, recv_sems), (fsend_sems, frecv_sems)
        sends = [copy(mine_ref.at[c], out_ref.at[k, c], ici, n, (cx, cy, c)) for n, (cx, cy) in enumerate(chips)]
        for cp in sends:
            cp.start()
        passed = [copy(out_ref.at[2 * cx + cy, c], out_ref.at[2 * cx + cy, c], d2d, n, (x, y, 1 - c))
                  for n, (cx, cy) in enumerate(chips)]
        for n, (cx, cy) in enumerate(chips):
            copy(mine_ref.at[c], out_ref.at[2 * cx + cy, c], ici, n, (cx, cy, c)).wait_recv()
            passed[n].start()
        for n, (cx, cy) in enumerate(chips):
            copy(mine_ref.at[1 - c], out_ref.at[2 * cx + cy, 1 - c], d2d, n, (x, y, 1 - c)).wait_recv()
        for cp in sends + passed:
            cp.wait_send()

    return pl.pallas_call(
        body, name="gather_shards", in_specs=[ANY], out_specs=ANY, out_shape=_sds((4, 2, r, w), mine.dtype),
        scratch_shapes=[pltpu.SemaphoreType.DMA((3,))] * 4)(mine)


def _swap_halves(big, small):
    _, _, r, w = big.shape

    def body(big_ref, small_ref, rbig_ref, rsmall_ref, send_sems, recv_sems):
        x, y, c, _, _ = _place()
        sib = (x, y, 1 - c)
        cps = [pltpu.make_async_remote_copy(src_ref=big_ref.at[:, 1 - c], dst_ref=rbig_ref, send_sem=send_sems.at[0],
                                            recv_sem=recv_sems.at[0], device_id=sib, device_id_type=MESH_ID),
               pltpu.make_async_remote_copy(src_ref=small_ref, dst_ref=rsmall_ref, send_sem=send_sems.at[1],
                                            recv_sem=recv_sems.at[1], device_id=sib, device_id_type=MESH_ID)]
        for cp in cps:
            cp.start()
        for cp in cps:
            cp.wait()

    return pl.pallas_call(
        body, name="swap_halves", in_specs=[ANY, ANY], out_specs=[ANY, ANY],
        out_shape=[_sds((4, r, w), big.dtype), _sds(small.shape, small.dtype)],
        scratch_shapes=[pltpu.SemaphoreType.DMA((2,))] * 2)(big, small)


def _sum_tile(r):
    return max(cand for cand in range(16, 641, 16) if r % cand == 0)


def _pair_sum(big, rbig, small, rsmall, c):
    _, _, r, w = big.shape
    tr = _sum_tile(r)
    ns = small.shape[0]

    def body(c_ref, big_ref, rbig_ref, small_ref, rsmall_ref, p_ref, ps_ref):
        p_ref[...] = (big_ref[...].astype(F32) + rbig_ref[...].astype(F32)).astype(BF16)

        @pl.when((pl.program_id(0) == 0) & (pl.program_id(1) == 0))
        def _():
            ps_ref[...] = small_ref[...] + rsmall_ref[...]

    grid_spec = pltpu.PrefetchScalarGridSpec(
        num_scalar_prefetch=1, grid=(4, r // tr),
        in_specs=[pl.BlockSpec((None, None, tr, w), lambda j, i, cr: (j, cr[0], i, 0)),
                  pl.BlockSpec((None, tr, w), lambda j, i, cr: (j, i, 0)),
                  pl.BlockSpec((ns, 128), lambda j, i, cr: (0, 0)), pl.BlockSpec((ns, 128), lambda j, i, cr: (0, 0))],
        out_specs=[pl.BlockSpec((None, tr, w), lambda j, i, cr: (j, i, 0)), pl.BlockSpec((ns, 128), lambda j, i, cr: (0, 0))])
    return pl.pallas_call(
        body, name="pair_sum", grid_spec=grid_spec,
        out_shape=[_sds((4, r, w), BF16), _sds(small.shape, F32)],
        compiler_params=_cp("arbitrary", "arbitrary"))(c, big, rbig, small, rsmall)


def _chip_exchange(p, ps):
    _, r, w = p.shape
    ns = ps.shape[0]

    def body(p_ref, ps_ref, rb_ref, rs_ref, send_sems, recv_sems, local_sem):
        x, y, c, k, chips = _place()
        loc = pltpu.make_async_copy(ps_ref, rs_ref.at[k], local_sem)
        loc.start()
        sends = []
        for n, (cx, cy) in enumerate(chips):
            to = (cx, cy, c)
            sends.append(pltpu.make_async_remote_copy(src_ref=p_ref.at[2 * cx + cy], dst_ref=rb_ref.at[n],
                                                      send_sem=send_sems.at[0, n], recv_sem=recv_sems.at[0, n],
                                                      device_id=to, device_id_type=MESH_ID))
            sends.append(pltpu.make_async_remote_copy(src_ref=ps_ref, dst_ref=rs_ref.at[k],
                                                      send_sem=send_sems.at[1, n], recv_sem=recv_sems.at[1, n],
                                                      device_id=to, device_id_type=MESH_ID))
        for cp in sends:
            cp.start()
        for n, (cx, cy) in enumerate(chips):
            to = (cx, cy, c)
            pltpu.make_async_remote_copy(src_ref=p_ref.at[k], dst_ref=rb_ref.at[n], send_sem=send_sems.at[0, n],
                                         recv_sem=recv_sems.at[0, n], device_id=to, device_id_type=MESH_ID).wait_recv()
            pltpu.make_async_remote_copy(src_ref=ps_ref, dst_ref=rs_ref.at[2 * cx + cy], send_sem=send_sems.at[1, n],
                                         recv_sem=recv_sems.at[1, n], device_id=to, device_id_type=MESH_ID).wait_recv()
        for cp in sends:
            cp.wait_send()
        loc.wait()

    return pl.pallas_call(
        body, name="chip_exchange", in_specs=[ANY, ANY], out_specs=[ANY, ANY],
        out_shape=[_sds((3, r, w), p.dtype), _sds((4, ns, 128), ps.dtype)],
        scratch_shapes=[pltpu.SemaphoreType.DMA((2, 3))] * 2 + [pltpu.SemaphoreType.DMA(())])(p, ps)


def _chip_sum(p, rb, rs, chip):
    _, r, w = p.shape
    tr = _sum_tile(r)
    ns = rs.shape[1]

    def body(k_ref, p_ref, rb_ref, rs_ref, o_ref, os_ref):
        acc = p_ref[...].astype(F32)
        for j in range(3):
            acc = acc + rb_ref[j].astype(F32)
        o_ref[...] = acc

        @pl.when(pl.program_id(0) == 0)
        def _():
            s = rs_ref[0]
            for j in range(1, 4):
                s = s + rs_ref[j]
            os_ref[...] = s

    grid_spec = pltpu.PrefetchScalarGridSpec(
        num_scalar_prefetch=1, grid=(r // tr,),
        in_specs=[pl.BlockSpec((None, tr, w), lambda i, kr: (kr[0], i, 0)), pl.BlockSpec((3, tr, w), lambda i, kr: (0, i, 0)),
                  pl.BlockSpec((4, ns, 128), lambda i, kr: (0, 0, 0))],
        out_specs=[pl.BlockSpec((tr, w), lambda i, kr: (i, 0)), pl.BlockSpec((ns, 128), lambda i, kr: (0, 0))])
    return pl.pallas_call(
        body, name="chip_sum", grid_spec=grid_spec, out_shape=[_sds((r, w), F32), _sds((ns, 128), F32)],
        compiler_params=_cp("arbitrary"))(chip, p, rb, rs)


def _send_to_sibling(red):
    def body(red_ref, out_ref, send_sem, recv_sem):
        x, y, c, _, _ = _place()
        cp = pltpu.make_async_remote_copy(src_ref=red_ref, dst_ref=out_ref, send_sem=send_sem, recv_sem=recv_sem,
                                          device_id=(x, y, 1 - c), device_id_type=MESH_ID)
        cp.start()
        cp.wait()

    return pl.pallas_call(
        body, name="send_to_sibling", in_specs=[ANY], out_specs=ANY, out_shape=_sds(red.shape, red.dtype),
        scratch_shapes=[pltpu.SemaphoreType.DMA(())] * 2)(red)


def _adam_math(w, g, m, v):
    nm = ADAM_B1 * m + (1.0 - ADAM_B1) * g
    nv = ADAM_B2 * v + (1.0 - ADAM_B2) * (g * g)
    m_hat = nm / (1.0 - ADAM_B1 ** ADAM_STEP)
    v_hat = nv / (1.0 - ADAM_B2 ** ADAM_STEP)
    return -ADAM_LR * (m_hat / (jnp.sqrt(v_hat) + ADAM_EPS) + ADAM_WD * w), nm, nv


def _adamw_packed(w, m, v, own, other, c, off, rows, name):
    depth = w.shape[0]
    half = depth // 2
    tr = math.gcd(off, rows) if off else rows
    tr = max(cand for cand in range(8, min(tr, 512) + 1, 8) if tr % cand == 0)
    nb, ob = rows // tr, off // tr
    own, other = own.reshape(half, PACK_ROWS, D), other.reshape(half, PACK_ROWS, D)

    def body(c_ref, w_ref, m_ref, v_ref, own_ref, oth_ref, g_ref, d_ref, nm_ref, nv_ref):
        mine = (pl.program_id(0) // half) == c_ref[0]
        g = jnp.where(mine, own_ref[...], oth_ref[...])
        g_ref[...] = g
        d_ref[...], nm_ref[...], nv_ref[...] = _adam_math(w_ref[...], g, m_ref[...], v_ref[...])

    blk = pl.BlockSpec((None, tr, D), lambda l, i, cr: (l, i, 0))
    gblk = pl.BlockSpec((None, tr, D), lambda l, i, cr: (l % half, ob + i, 0))
    grid_spec = pltpu.PrefetchScalarGridSpec(num_scalar_prefetch=1, grid=(depth, nb), in_specs=[blk] * 3 + [gblk] * 2,
                                             out_specs=[blk] * 4)
    return pl.pallas_call(body, name=name, grid_spec=grid_spec, out_shape=[_sds(w.shape, F32)] * 4,
                          compiler_params=_cp("parallel", "parallel"))(c, w, m, v, own, other)


def _adamw_small(w, g, m, v):
    r = w.shape[0]
    tr = max(cand for cand in range(8, 513, 8) if r % cand == 0)

    def body(w_ref, g_ref, m_ref, v_ref, d_ref, nm_ref, nv_ref):
        d_ref[...], nm_ref[...], nv_ref[...] = _adam_math(w_ref[...], g_ref[...], m_ref[...], v_ref[...])

    blk = pl.BlockSpec((tr, 128), lambda i: (i, 0))
    return pl.pallas_call(body, name="adamw_small", grid=(r // tr,), in_specs=[blk] * 4, out_specs=[blk] * 3,
                          out_shape=[_sds(w.shape, F32)] * 3, compiler_params=_cp("parallel"))(w, g, m, v)


def _to_pack(a, name):
    depth = a.shape[0]
    if name in ROW_SHARDED:
        return jnp.swapaxes(a.reshape(depth, 4, -1, D), 0, 1)
    return jnp.transpose(a.reshape(depth, a.shape[1], 4, a.shape[2] // 4), (2, 0, 1, 3)).reshape(4, depth, -1, D)


def _pack_rows(parts, lead, dtype, tail=None):
    pieces, at = [], 0
    for n, off, rows in PACK:
        if off > at:
            pieces.append(jnp.zeros(lead + (off - at, D), dtype))
        pieces.append(parts[n].astype(dtype))
        at = off + rows
    if tail is not None:
        pieces.append(tail)
        at += tail.shape[-2]
    pieces.append(jnp.zeros(lead + (PACK_ROWS - at, D), dtype))
    return jnp.concatenate(pieces, axis=len(lead))


def _pack_weight_shards(sh):
    depth = sh["w_in"].shape[0]
    parts = {n: sh[n].reshape(depth, rows, D) for n, _, rows in PACK}
    conv = lax.bitcast_convert_type(sh["conv_w"].reshape(depth, 3 * 64), BF16).reshape(depth, 1, 384)
    flat = _pack_rows(parts, (depth,), BF16, tail=jnp.pad(conv, ((0, 0), (0, 0), (0, D - 384))))
    return flat.reshape(2, depth // 2 * PACK_ROWS, D)


def _unpack_weights(gathered, depth, shard_shapes):
    flat = gathered.reshape(4, depth, PACK_ROWS, D)
    full = {}
    for n, off, rows in PACK:
        shp = shard_shapes[n][1:]
        piece = flat[:, :, off:off + rows, :].reshape((4, depth) + shp)
        if n in ROW_SHARDED:
            full[n] = jnp.transpose(piece, (1, 0, 2, 3)).reshape(depth, 4 * shp[0], shp[1])
        else:
            full[n] = jnp.transpose(piece, (1, 2, 0, 3)).reshape(depth, shp[0], 4 * shp[1])
    conv = lax.bitcast_convert_type(flat[:, :, CONV_ROW, :384].reshape(4, depth, 192, 2), F32)
    full["conv_w"] = jnp.transpose(conv.reshape(4, depth, 3, 64), (1, 2, 0, 3)).reshape(depth, 3, CVW)
    return full


def _pack_grad_shards(g):
    depth = g["w_in"].shape[0]
    flat = _pack_rows({n: _to_pack(g[n], n) for n, _, _ in PACK}, (4, depth), BF16)
    return flat.reshape(4, 2, depth // 2 * PACK_ROWS, D)


def _pack_small(arrs, names_shapes, depth):
    flat = jnp.concatenate([arrs[n].reshape(depth, -1) for n, _ in names_shapes], axis=1).reshape(-1)
    rows = -(-flat.shape[0] // 1024) * 8
    return jnp.pad(flat, (0, rows * 128 - flat.shape[0])).reshape(rows, 128)


def _unpack_small(packed, names_shapes, depth):
    per_layer = sum(math.prod(s) for _, s in names_shapes)
    flat = packed.reshape(-1)[:depth * per_layer].reshape(depth, per_layer)
    out, off = {}, 0
    for n, s in names_shapes:
        size = math.prod(s)
        out[n] = flat[:, off:off + size].reshape((depth,) + s)
        off += size
    return out


def kernel(x, positions, mix_pre_g, mix_post_g, ffn_pre_g, ffn_post_g, w_in, q_norm_g, w_uq, kv_norm_g, w_ukv, sg_ln_g, sg_ln_b, w_sp, b_sp, conv_w, out_norm_g, w_out, w_gate, w_up, w_down, loss_target, m_mix_pre_g, m_mix_post_g, m_ffn_pre_g, m_ffn_post_g, m_w_in, m_q_norm_g, m_w_uq, m_kv_norm_g, m_w_ukv, m_sg_ln_g, m_sg_ln_b, m_w_sp, m_b_sp, m_conv_w, m_out_norm_g, m_w_out, m_w_gate, m_w_up, m_w_down, v_mix_pre_g, v_mix_post_g, v_ffn_pre_g, v_ffn_post_g, v_w_in, v_q_norm_g, v_w_uq, v_kv_norm_g, v_w_ukv, v_sg_ln_g, v_sg_ln_b, v_w_sp, v_b_sp, v_conv_w, v_out_norm_g, v_w_out, v_w_gate, v_w_up, v_w_down):
    w = dict(mix_pre_g=mix_pre_g, mix_post_g=mix_post_g, ffn_pre_g=ffn_pre_g, ffn_post_g=ffn_post_g, w_in=w_in,
             q_norm_g=q_norm_g, w_uq=w_uq, kv_norm_g=kv_norm_g, w_ukv=w_ukv, sg_ln_g=sg_ln_g, sg_ln_b=sg_ln_b, w_sp=w_sp,
             b_sp=b_sp, conv_w=conv_w, out_norm_g=out_norm_g, w_out=w_out, w_gate=w_gate, w_up=w_up, w_down=w_down)
    m = dict(mix_pre_g=m_mix_pre_g, mix_post_g=m_mix_post_g, ffn_pre_g=m_ffn_pre_g, ffn_post_g=m_ffn_post_g, w_in=m_w_in,
             q_norm_g=m_q_norm_g, w_uq=m_w_uq, kv_norm_g=m_kv_norm_g, w_ukv=m_w_ukv, sg_ln_g=m_sg_ln_g, sg_ln_b=m_sg_ln_b,
             w_sp=m_w_sp, b_sp=m_b_sp, conv_w=m_conv_w, out_norm_g=m_out_norm_g, w_out=m_w_out, w_gate=m_w_gate,
             w_up=m_w_up, w_down=m_w_down)
    v = dict(mix_pre_g=v_mix_pre_g, mix_post_g=v_mix_post_g, ffn_pre_g=v_ffn_pre_g, ffn_post_g=v_ffn_post_g, w_in=v_w_in,
             q_norm_g=v_q_norm_g, w_uq=v_w_uq, kv_norm_g=v_kv_norm_g, w_ukv=v_w_ukv, sg_ln_g=v_sg_ln_g, sg_ln_b=v_sg_ln_b,
             w_sp=v_w_sp, b_sp=v_b_sp, conv_w=v_conv_w, out_norm_g=v_out_norm_g, w_out=v_w_out, w_gate=v_w_gate,
             w_up=v_w_up, w_down=v_w_down)
    depth = w_in.shape[0]
    c = lax.axis_index("c").astype(jnp.int32).reshape(1)
    chip = (2 * lax.axis_index("x") + lax.axis_index("y")).astype(jnp.int32)

    mine = _pack_weight_shards(w)
    gathered = lax.dynamic_update_slice(_gather_shards(mine), mine[None], (chip, 0, 0, 0))
    full = _unpack_weights(gathered, depth, {n: w[n].shape for n, _, _ in PACK})
    for n, _ in SMALL:
        if n != "conv_w":
            full[n] = w[n]

    loss, dx, grads = _local_step(x[0], positions[0], loss_target[0], full)
    loss = lax.psum(loss, ("x", "y", "c"))

    small = _pack_small(grads, SMALL, depth)
    big = _pack_grad_shards(grads)
    rbig, rsmall = _swap_halves(big, small)
    p, ps = _pair_sum(big, rbig, small, rsmall, c)
    rb, rs = _chip_exchange(p, ps)
    red_own, red_small = _chip_sum(p, rb, rs, chip.reshape(1))
    red_other = _send_to_sibling(red_own)
    g_small = _unpack_small(red_small, SMALL, depth)
    g_small["conv_w"] = lax.dynamic_slice_in_dim(g_small["conv_w"], chip * 64, 64, axis=2)

    gw, delta, new_m, new_v = dict(g_small), {}, {}, {}
    for n, off, rows in PACK:
        shp = w[n].shape
        flat = lambda a: a.reshape(depth, rows, D)
        outs = _adamw_packed(flat(w[n]), flat(m[n]), flat(v[n]), red_own, red_other, c, off, rows, "adamw_" + n)
        gw[n], delta[n], new_m[n], new_v[n] = [o.reshape(shp) for o in outs]
    small_local = tuple((n, w[n].shape[1:]) for n, _ in SMALL)
    d_, m_, v_ = _adamw_small(_pack_small(w, small_local, depth), _pack_small(gw, small_local, depth),
                              _pack_small(m, small_local, depth), _pack_small(v, small_local, depth))
    delta.update(_unpack_small(d_, small_local, depth))
    new_m.update(_unpack_small(m_, small_local, depth))
    new_v.update(_unpack_small(v_, small_local, depth))

    return (loss, dx[None], *[gw[n] for n in WEIGHTS], *[delta[n] for n in WEIGHTS], *[new_m[n] for n in WEIGHTS],
            *[new_v[n] for n in WEIGHTS])
```

```python
import math

import jax
import jax.numpy as jnp
from jax import lax
from jax.experimental import pallas as pl
from jax.experimental.pallas import tpu as pltpu

F32 = jnp.float32
BF16 = jnp.bfloat16

D = 1024
HEADS = 8
NOPE = 64
ROPE = 32
VD = 64
QR = 384
KVR = 256
SGW = 256
CVW = 256
CHUNK = 128
DFF = 2816
EPS = 1e-6
ROPE_THETA = 10000.0
LOG2E = 1.4426950408889634
LN2 = 0.6931471805599453
QSCALE = (NOPE + ROPE) ** -0.5 * LOG2E
ZA = 768
ZB = 1280
QW = HEADS * 128
KVW = HEADS * 128 + HEADS * VD
NEG = -1e30
GC0 = 0.7978845608028654
GC1 = 0.044715

ADAM_LR = 0.001
ADAM_B1 = 0.9
ADAM_B2 = 0.999
ADAM_EPS = 1e-08
ADAM_WD = 0.01
ADAM_STEP = 10

V7X_VMEM_LIMIT = 52 * 1024 * 1024
ROW_TILE = 512
ATT_TILE = 512

NT = (((1,), (1,)), ((), ()))
TN = (((0,), (0,)), ((), ()))

HS = DFF // 4
HP = 768
DFFP = 4 * HP

PACK = (("w_in", 0, 488), ("w_out", 512, 256), ("w_ukv", 768, 64), ("w_uq", 832, 72))
CONV_ROW = 904
PACK_ROWS = 912
ROW_SHARDED = ("w_out",)
SMALL = (("mix_pre_g", (D,)), ("mix_post_g", (D,)), ("ffn_pre_g", (D,)), ("ffn_post_g", (D,)), ("q_norm_g", (QR,)),
         ("kv_norm_g", (KVR,)), ("sg_ln_g", (SGW,)), ("sg_ln_b", (SGW,)), ("w_sp", (4, CHUNK, CHUNK)), ("b_sp", (4, CHUNK)),
         ("conv_w", (3, CVW)), ("out_norm_g", (D,)))
WEIGHTS = ["mix_pre_g", "mix_post_g", "ffn_pre_g", "ffn_post_g", "w_in", "q_norm_g", "w_uq", "kv_norm_g", "w_ukv", "sg_ln_g",
           "sg_ln_b", "w_sp", "b_sp", "conv_w", "out_norm_g", "w_out", "w_gate", "w_up", "w_down"]

MESH_ID = pl.DeviceIdType.MESH
ANY = pl.BlockSpec(memory_space=pl.ANY)


def _cp(*sem):
    return pltpu.CompilerParams(dimension_semantics=sem, vmem_limit_bytes=V7X_VMEM_LIMIT)


def _sds(shape, dtype):
    return jax.ShapeDtypeStruct(shape, dtype)


def _row(tm, n):
    return pl.BlockSpec((tm, n), lambda i: (i, 0))


def _lyr(l, *shape):
    return pl.BlockSpec((None,) + shape, lambda *_: (l,) + (0,) * len(shape))


def _pcall(body, name, grid, ins, in_specs, out_specs, out_shape, sem, scratch=(), prevs=None):
    prevs = {k: v for k, v in (prevs or {}).items() if v is not None}
    order = sorted(prevs)
    n_in = len(ins)

    def wrapped(*refs):
        return body(*refs[:n_in], *refs[n_in + len(order):])

    return pl.pallas_call(
        wrapped, name=name, grid=grid, in_specs=list(in_specs) + [ANY] * len(order), out_specs=out_specs,
        out_shape=out_shape, scratch_shapes=list(scratch),
        input_output_aliases={n_in + i: k for i, k in enumerate(order)},
        compiler_params=_cp(*sem))(*ins, *[prevs[k] for k in order])


def _rms(x, g):
    r = lax.rsqrt(jnp.mean(x * x, axis=-1, keepdims=True) + EPS)
    return x * r * g


def _rms_bwd(x, g, dy):
    r = lax.rsqrt(jnp.mean(x * x, axis=-1, keepdims=True) + EPS)
    xh = x * r
    dg = jnp.sum(dy * xh, axis=0, keepdims=True)
    dxh = dy * g
    dx = r * (dxh - xh * jnp.mean(dxh * xh, axis=-1, keepdims=True))
    return dx, dg


def _gelu(x):
    return 0.5 * x * (1.0 + jnp.tanh(GC0 * (x + GC1 * x * x * x)))


def _gelu_grad(x):
    t = jnp.tanh(GC0 * (x + GC1 * x * x * x))
    return 0.5 * (1.0 + t) + 0.5 * x * (1.0 - t * t) * GC0 * (1.0 + 3.0 * GC1 * x * x)


def _rope(xb, c, s1, s2):
    return xb * c + pltpu.roll(xb, 112, 1) * s1 + pltpu.roll(xb, 16, 1) * s2


def _rope_bwd(dy, c, s1, s2):
    return dy * c + pltpu.roll(dy * s1, 16, 1) + pltpu.roll(dy * s2, 112, 1)


def _group_masks(shape):
    lane = lax.broadcasted_iota(jnp.int32, shape, 1)
    return [(lane >= 64 * g) & (lane < 64 * g + 64) for g in range(shape[1] // 64)]


def _group_mean(v, masks):
    out = jnp.zeros_like(v)
    for m in masks:
        s = jnp.sum(jnp.where(m, v, 0.0), axis=-1, keepdims=True) * (1.0 / 64.0)
        out = jnp.where(m, s, out)
    return out


def _pick_row(blk, idx):
    row = lax.broadcasted_iota(jnp.int32, blk.shape, 0)
    return jnp.sum(jnp.where(row == idx, blk, 0.0), axis=0, keepdims=True)


def _shift_down(y, k, first_rows):
    out = pltpu.roll(y, k, 0)
    row = lax.broadcasted_iota(jnp.int32, y.shape, 0)
    for idx in range(k):
        out = jnp.where(row == idx, first_rows[idx], out)
    return out


def _shift_up(y, k, last_rows):
    n = y.shape[0]
    out = pltpu.roll(y, n - k, 0)
    row = lax.broadcasted_iota(jnp.int32, y.shape, 0)
    for idx in range(k):
        out = jnp.where(row == n - k + idx, last_rows[idx], out)
    return out


def _tril_mask():
    r = lax.broadcasted_iota(jnp.int32, (CHUNK, CHUNK), 0)
    c = lax.broadcasted_iota(jnp.int32, (CHUNK, CHUNK), 1)
    return r >= c


def _sgu_forward(zu, zv, g_ln, b_ln, wc_bf, bsp, masks, cmasks):
    u = _gelu(zu)
    vv = _gelu(zv)
    mu = _group_mean(vv, masks)
    dv = vv - mu
    rs = lax.rsqrt(_group_mean(dv * dv, masks) + EPS)
    xh = dv * rs
    vn = xh * g_ln + b_ln
    chunks = []
    for ci in range(zu.shape[0] // CHUNK):
        vc = vn[ci * CHUNK:(ci + 1) * CHUNK, :]
        acc = bsp
        for g in range(4):
            acc = acc + jnp.dot(wc_bf[g], jnp.where(cmasks[g], vc, 0.0).astype(BF16), preferred_element_type=F32)
        chunks.append(acc)
    mixed = jnp.concatenate(chunks, axis=0) if len(chunks) > 1 else chunks[0]
    return u, vv, xh, rs, vn, mixed


def _conv_forward(gc, hh, prev_gc, prev_hh, first_tile, cw):
    yv = gc * hh
    prev = jnp.where(first_tile, 0.0, prev_gc * prev_hh)
    p6, p7 = _pick_row(prev, 6), _pick_row(prev, 7)
    sh1 = _shift_down(yv, 1, [p7])
    sh2 = _shift_down(yv, 2, [p6, p7])
    conv = sh2 * cw[0:1, :] + sh1 * cw[1:2, :] + yv * cw[2:3, :]
    return yv, sh1, sh2, conv


def _acc_init(step, *refs):
    @pl.when(step == 0)
    def _():
        for r in refs:
            r[...] = jnp.zeros(r.shape, r.dtype)


def _in_proj(x, p, l):
    t = x.shape[0]
    tm = min(ROW_TILE, t)

    def body(x_ref, g_ref, wa_ref, wb_ref, h_ref, za_ref, zb_ref):
        h = _rms(x_ref[...], g_ref[...]).astype(BF16)
        h_ref[...] = h
        za_ref[...] = jnp.dot(h, wa_ref[...], preferred_element_type=F32)
        zb_ref[...] = jnp.dot(h, wb_ref[...], preferred_element_type=F32)

    return _pcall(
        body, "in_proj", (t // tm,), [x, p["mix_pre_g"], p["w_in_a"], p["w_in_b"]],
        [_row(tm, D), _lyr(l, 1, D), _lyr(l, D, ZA), _lyr(l, D, ZB)],
        [_row(tm, D), _row(tm, ZA), _row(tm, ZB)],
        [_sds((t, D), BF16), _sds((t, ZA), F32), _sds((t, ZB), F32)], ("parallel",))


def _mla_prep(za, p, l, tabs):
    t = za.shape[0]
    tm = min(ROW_TILE, t)

    def body(z_ref, gq_ref, gkv_ref, wuq_ref, wukv_ref, c_ref, s1_ref, s2_ref, cq_ref, ckv_ref, q_ref, k_ref, v_ref):
        z = z_ref[...]
        cq = _rms(z[:, :QR], gq_ref[...]).astype(BF16)
        ckv = _rms(z[:, QR:QR + KVR], gkv_ref[...]).astype(BF16)
        cq_ref[...] = cq
        ckv_ref[...] = ckv
        c, s1, s2 = c_ref[...], s1_ref[...], s2_ref[...]
        kr = _rope(z[:, QR + KVR:], c, s1, s2)
        q = jnp.dot(cq, wuq_ref[...], preferred_element_type=F32)
        kv = jnp.dot(ckv, wukv_ref[...], preferred_element_type=F32)
        for h in range(HEADS):
            sl = slice(128 * h, 128 * h + 128)
            q_ref[:, sl] = (_rope(q[:, sl], c, s1, s2) * QSCALE).astype(BF16)
            k_ref[:, sl] = (kv[:, sl] + kr).astype(BF16)
        v_ref[...] = kv[:, QW:].astype(BF16)

    return _pcall(
        body, "mla_prep", (t // tm,), [za, p["q_norm_g"], p["kv_norm_g"], p["w_uq"], p["w_ukv"], *tabs],
        [_row(tm, ZA), _lyr(l, 1, QR), _lyr(l, 1, KVR), _lyr(l, QR, QW), _lyr(l, KVR, KVW),
         _row(tm, 128), _row(tm, 128), _row(tm, 128)],
        [_row(tm, QR), _row(tm, KVR), _row(tm, QW), _row(tm, QW), _row(tm, HEADS * VD)],
        [_sds((t, QR), BF16), _sds((t, KVR), BF16), _sds((t, QW), BF16), _sds((t, QW), BF16),
         _sds((t, HEADS * VD), BF16)], ("parallel",))


def _att_tile(t):
    return min(ATT_TILE, max(t // 2, 128))


def _causal_keep(tq, i, j):
    row = lax.broadcasted_iota(jnp.int32, (tq, tq), 0) + i * tq
    col = lax.broadcasted_iota(jnp.int32, (tq, tq), 1) + j * tq
    return col <= row


def _attn_fwd(qs, k, v):
    t = qs.shape[0]
    tq = _att_tile(t)
    nq = t // tq
    rep = tq // 128

    def body(q_ref, k_ref, v_ref, o_ref, lse_ref, m_s, l_s, acc_s):
        i, j = pl.program_id(1), pl.program_id(2)

        @pl.when(j == 0)
        def _():
            m_s[...] = jnp.full(m_s.shape, NEG, F32)
            l_s[...] = jnp.zeros(l_s.shape, F32)
            acc_s[...] = jnp.zeros(acc_s.shape, F32)

        def step(masked):
            vv = v_ref[...]
            keep = _causal_keep(tq, i, j) if masked else None
            for hh in range(2):
                sl = slice(128 * hh, 128 * hh + 128)
                s = lax.dot_general(q_ref[:, sl], k_ref[:, sl], NT, preferred_element_type=F32)
                if masked:
                    s = jnp.where(keep, s, NEG)
                m_old = m_s[hh]
                m_new = jnp.maximum(m_old, jnp.max(s, axis=-1, keepdims=True))
                alpha = jnp.exp2(m_old - m_new)
                p = jnp.exp2(s - jnp.tile(m_new, (1, rep)))
                l_s[hh] = alpha * l_s[hh] + jnp.sum(p, axis=-1, keepdims=True)
                acc_s[hh] = alpha * acc_s[hh] + jnp.dot(p.astype(BF16), vv, preferred_element_type=F32)
                m_s[hh] = m_new

        @pl.when(j < i)
        def _():
            step(False)

        @pl.when(j == i)
        def _():
            step(True)
            lane = lax.broadcasted_iota(jnp.int32, (tq, 128), 1)
            o_ref[...] = jnp.where(lane < VD, acc_s[0] / l_s[0], acc_s[1] / l_s[1])
            for hh in range(2):
                lse_ref[hh] = (m_s[hh] + jnp.log2(l_s[hh]))[:, 0:1]

    return pl.pallas_call(
        body, name="attn_fwd", grid=(HEADS // 2, nq, nq),
        in_specs=[pl.BlockSpec((tq, 256), lambda p, i, j: (i, p)),
                  pl.BlockSpec((tq, 256), lambda p, i, j: (jnp.minimum(j, i), p)),
                  pl.BlockSpec((tq, 128), lambda p, i, j: (jnp.minimum(j, i), p))],
        out_specs=[pl.BlockSpec((tq, 128), lambda p, i, j: (i, p)),
                   pl.BlockSpec((2, tq, 1), lambda p, i, j: (p, i, 0))],
        out_shape=[_sds((t, HEADS * VD), F32), _sds((HEADS, t, 1), F32)],
        scratch_shapes=[pltpu.VMEM((2, tq, 128), F32), pltpu.VMEM((2, tq, 128), F32), pltpu.VMEM((2, tq, 128), F32)],
        compiler_params=_cp("parallel", "parallel", "arbitrary"))(qs, k, v)


def _mixer_fwd(zb, ya, p, l):
    t = zb.shape[0]
    tm = min(ROW_TILE, t)
    hb = tm // 8

    def body(zb_ref, zprev_ref, ya_ref, gln_ref, bln_ref, wsp_ref, bsp_ref, cw_ref, go_ref, mix_ref, yb_ref, yc_ref):
        i = pl.program_id(0)
        masks = _group_masks((tm, SGW))
        cmasks = _group_masks((CHUNK, SGW))
        tril = _tril_mask()
        wc_bf = [jnp.where(tril, wsp_ref[g], 0.0).astype(BF16) for g in range(4)]
        u, _, _, _, _, mixed = _sgu_forward(zb_ref[:, 0:256], zb_ref[:, 256:512], gln_ref[...], bln_ref[...], wc_bf,
                                            bsp_ref[...], masks, cmasks)
        yb = u * mixed
        _, _, _, conv = _conv_forward(zb_ref[:, 768:1024], zb_ref[:, 1024:1280], zprev_ref[:, 768:1024],
                                      zprev_ref[:, 1024:1280], i == 0, cw_ref[...])
        yc = zb_ref[:, 512:768] * conv
        yb_ref[...] = yb
        yc_ref[...] = yc
        go = go_ref[...]
        mix_ref[:, 0:512] = _rms(ya_ref[...], go[:, 0:512]).astype(BF16)
        mix_ref[:, 512:768] = _rms(yb, go[:, 512:768]).astype(BF16)
        mix_ref[:, 768:1024] = _rms(yc, go[:, 768:1024]).astype(BF16)

    return _pcall(
        body, "mixer_fwd", (t // tm,),
        [zb, zb, ya, p["sg_ln_g"], p["sg_ln_b"], p["w_sp"], p["b_sp"], p["conv_w"], p["out_norm_g"]],
        [_row(tm, ZB), pl.BlockSpec((8, ZB), lambda i: (jnp.maximum(i * hb - 1, 0), 0)), _row(tm, 512),
         _lyr(l, 1, SGW), _lyr(l, 1, SGW), _lyr(l, 4, CHUNK, CHUNK), _lyr(l, CHUNK, SGW), _lyr(l, 3, CVW), _lyr(l, 1, D)],
        [_row(tm, D), _row(tm, SGW), _row(tm, CVW)],
        [_sds((t, D), BF16), _sds((t, SGW), F32), _sds((t, CVW), F32)], ("parallel",))


def _out_proj(mix, x, p, l):
    t = x.shape[0]
    tm = min(ROW_TILE, t)

    def body(mix_ref, w_ref, x_ref, gp_ref, gf_ref, o_ref, x2_ref, h2_ref):
        o = jnp.dot(mix_ref[...], w_ref[...], preferred_element_type=F32)
        o_ref[...] = o
        x2 = x_ref[...] + _rms(o, gp_ref[...])
        x2_ref[...] = x2
        h2_ref[...] = _rms(x2, gf_ref[...]).astype(BF16)

    return _pcall(
        body, "out_proj", (t // tm,), [mix, p["w_out"], x, p["mix_post_g"], p["ffn_pre_g"]],
        [_row(tm, D), _lyr(l, D, D), _row(tm, D), _lyr(l, 1, D), _lyr(l, 1, D)],
        [_row(tm, D), _row(tm, D), _row(tm, D)],
        [_sds((t, D), F32), _sds((t, D), F32), _sds((t, D), BF16)], ("parallel",))


def _gu_all(l, which):
    return pl.BlockSpec((4, None, None, D, HP), lambda *_: (0, l, which, 0, 0))


def _down_all(l):
    return pl.BlockSpec((4, None, HP, D), lambda *_: (0, l, 0, 0))


def _ffn_up(h2, p, l):
    t = h2.shape[0]
    tm = min(ROW_TILE, t)

    def body(h_ref, wg_ref, wu_ref, a_ref, b_ref, s_ref):
        h = h_ref[...]
        a = jnp.dot(h, wg_ref[...], preferred_element_type=F32)
        b = jnp.dot(h, wu_ref[...], preferred_element_type=F32)
        a_ref[...] = a.astype(BF16)
        b_ref[...] = b.astype(BF16)
        s_ref[...] = (a * (1.0 / (1.0 + jnp.exp(-a))) * b).astype(BF16)

    blk = pl.BlockSpec((tm, HP), lambda k, i: (i, k))
    wblk = lambda which: pl.BlockSpec((None, None, None, D, HP), lambda k, i: (k, l, which, 0, 0))
    return _pcall(
        body, "ffn_up", (4, t // tm), [h2, p["w_gu"], p["w_gu"]],
        [pl.BlockSpec((tm, D), lambda k, i: (i, 0)), wblk(0), wblk(1)], [blk, blk, blk],
        [_sds((t, DFFP), BF16)] * 3, ("parallel", "parallel"))


def _ffn_down(s, x2, p, l):
    t = x2.shape[0]
    tm = min(ROW_TILE, t)

    def body(s_ref, w_ref, x_ref, g_ref, f_ref, x3_ref):
        f = jnp.dot(s_ref[:, 0:HP], w_ref[0], preferred_element_type=F32)
        for k in range(1, 4):
            f = f + jnp.dot(s_ref[:, k * HP:(k + 1) * HP], w_ref[k], preferred_element_type=F32)
        f_ref[...] = f
        x3_ref[...] = x_ref[...] + _rms(f, g_ref[...])

    return _pcall(
        body, "ffn_down", (t // tm,), [s, p["w_down"], x2, p["ffn_post_g"]],
        [_row(tm, DFFP), _down_all(l), _row(tm, D), _lyr(l, 1, D)], [_row(tm, D), _row(tm, D)],
        [_sds((t, D), F32), _sds((t, D), F32)], ("parallel",))


def _loss_head(y, target):
    t = y.shape[0]
    tm = min(ROW_TILE, t)

    def body(y_ref, t_ref, dy_ref, acc_ref):
        e = y_ref[...] - t_ref[...]
        dy_ref[...] = e * (1.0 / D)
        sq = jnp.sum(e * e, axis=0, keepdims=True)
        part = sq[:, 0:128]
        for b in range(1, D // 128):
            part = part + sq[:, 128 * b:128 * b + 128]
        _acc_init(pl.program_id(0), acc_ref)
        acc_ref[...] += part

    return _pcall(body, "loss_head", (t // tm,), [y, target], [_row(tm, D), _row(tm, D)],
                  [_row(tm, D), pl.BlockSpec((1, 128), lambda i: (0, 0))],
                  [_sds((t, D), F32), _sds((1, 128), F32)], ("arbitrary",))


def _ffn_down_bwd(dx3, sv, p, l, depth, gb):
    t = dx3.shape[0]
    tm = min(256, t)

    def body(dx_ref, f_ref, g_ref, w_ref, a_ref, b_ref, df_ref, da_ref, db_ref, dg_ref):
        _acc_init(pl.program_id(0), dg_ref)
        df, dg = _rms_bwd(f_ref[...], g_ref[...], dx_ref[...])
        dg_ref[...] += dg
        df = df.astype(BF16)
        df_ref[...] = df
        for k in range(4):
            sl = slice(k * HP, (k + 1) * HP)
            ds = lax.dot_general(df, w_ref[k], NT, preferred_element_type=F32)
            av = a_ref[:, sl].astype(F32)
            sig = 1.0 / (1.0 + jnp.exp(-av))
            da_ref[:, sl] = (ds * b_ref[:, sl].astype(F32) * (sig * (1.0 + av * (1.0 - sig)))).astype(BF16)
            db_ref[:, sl] = (ds * (av * sig)).astype(BF16)

    df, da, db, gb["ffn_post_g"] = _pcall(
        body, "ffn_down_bwd", (t // tm,), [dx3, sv["f"], p["ffn_post_g"], p["w_down"], sv["a"], sv["b"]],
        [_row(tm, D), _row(tm, D), _lyr(l, 1, D), _down_all(l), _row(tm, DFFP), _row(tm, DFFP)],
        [_row(tm, D), _row(tm, DFFP), _row(tm, DFFP), _lyr(l, 1, D)],
        [_sds((t, D), BF16), _sds((t, DFFP), BF16), _sds((t, DFFP), BF16), _sds((depth, 1, D), F32)], ("arbitrary",),
        prevs={3: gb.get("ffn_post_g")})
    return df, da, db


def _ffn_up_bwd(da, db, dx3, sv, p, l, depth, gb):
    t = dx3.shape[0]
    tm = min(256, t)

    def body(da_ref, db_ref, wg_ref, wu_ref, x_ref, dx3_ref, g_ref, dx2_ref, dg_ref):
        _acc_init(pl.program_id(0), dg_ref)
        dh = jnp.zeros((tm, D), F32)
        for k in range(4):
            sl = slice(k * HP, (k + 1) * HP)
            dh = dh + lax.dot_general(da_ref[:, sl], wg_ref[k], NT, preferred_element_type=F32)
            dh = dh + lax.dot_general(db_ref[:, sl], wu_ref[k], NT, preferred_element_type=F32)
        dx, dg = _rms_bwd(x_ref[...], g_ref[...], dh)
        dg_ref[...] += dg
        dx2_ref[...] = dx3_ref[...] + dx

    dx2, gb["ffn_pre_g"] = _pcall(
        body, "ffn_up_bwd", (t // tm,), [da, db, p["w_gu"], p["w_gu"], sv["x2"], dx3, p["ffn_pre_g"]],
        [_row(tm, DFFP), _row(tm, DFFP), _gu_all(l, 0), _gu_all(l, 1), _row(tm, D), _row(tm, D), _lyr(l, 1, D)],
        [_row(tm, D), _lyr(l, 1, D)], [_sds((t, D), F32), _sds((depth, 1, D), F32)], ("arbitrary",),
        prevs={1: gb.get("ffn_pre_g")})
    return dx2


def _out_proj_bwd(dx2, sv, p, l, depth, gb):
    t = dx2.shape[0]
    tm = min(ROW_TILE, t)

    def body(dx_ref, o_ref, g_ref, w_ref, do_ref, dmix_ref, dg_ref):
        _acc_init(pl.program_id(0), dg_ref)
        do, dg = _rms_bwd(o_ref[...], g_ref[...], dx_ref[...])
        dg_ref[...] += dg
        do = do.astype(BF16)
        do_ref[...] = do
        dmix_ref[...] = lax.dot_general(do, w_ref[...], NT, preferred_element_type=F32)

    do, dmix, gb["mix_post_g"] = _pcall(
        body, "out_proj_bwd", (t // tm,), [dx2, sv["o"], p["mix_post_g"], p["w_out"]],
        [_row(tm, D), _row(tm, D), _lyr(l, 1, D), _lyr(l, D, D)], [_row(tm, D), _row(tm, D), _lyr(l, 1, D)],
        [_sds((t, D), BF16), _sds((t, D), F32), _sds((depth, 1, D), F32)], ("arbitrary",),
        prevs={2: gb.get("mix_post_g")})
    return do, dmix


def _mixer_bwd(dmix, sv, p, l, depth, gb):
    zb = sv["zb"]
    t = zb.shape[0]
    tm = min(ROW_TILE, t)
    hb = tm // 8
    last_blk = t // 8 - 1
    nsteps = t // tm

    def body(dmix_ref, ya_ref, yb_ref, yc_ref, zb_ref, zprev_ref, znext_ref, ycn_ref, dmn_ref,
             gln_ref, bln_ref, wsp_ref, bsp_ref, cw_ref, go_ref,
             dya_ref, dzb_ref, delta_ref, dgo_ref, dgln_ref, dbln_ref, dwsp_ref, dbsp_ref, dcw_ref):
        i = pl.program_id(0)
        _acc_init(i, dgo_ref, dgln_ref, dbln_ref, dwsp_ref, dbsp_ref, dcw_ref)
        go = go_ref[...]
        dmix = dmix_ref[...]

        ya = ya_ref[...]
        dya, dga = _rms_bwd(ya, go[:, 0:512], dmix[:, 0:512])
        dyb, dgb_ = _rms_bwd(yb_ref[...], go[:, 512:768], dmix[:, 512:768])
        dyc, dgc_ = _rms_bwd(yc_ref[...], go[:, 768:1024], dmix[:, 768:1024])
        dgo_ref[:, 0:512] += dga
        dgo_ref[:, 512:768] += dgb_
        dgo_ref[:, 768:1024] += dgc_
        dya = dya * LN2
        dya_ref[...] = dya.astype(BF16)
        prod = dya * ya
        hmasks = _group_masks((tm, 512))
        for h in range(HEADS):
            delta_ref[h] = jnp.sum(jnp.where(hmasks[h], prod, 0.0), axis=-1, keepdims=True)

        masks = _group_masks((tm, SGW))
        cmasks = _group_masks((CHUNK, SGW))
        tril = _tril_mask()
        wc_bf = [jnp.where(tril, wsp_ref[g], 0.0).astype(BF16) for g in range(4)]
        zu, zv = zb_ref[:, 0:256], zb_ref[:, 256:512]
        g_ln = gln_ref[...]
        u, _, xh, rs, vn, mixed = _sgu_forward(zu, zv, g_ln, bln_ref[...], wc_bf, bsp_ref[...], masks, cmasks)
        du = dyb * mixed
        dmixed = dyb * u
        dvn_chunks = []
        dbsp = jnp.zeros((CHUNK, SGW), F32)
        for ci in range(tm // CHUNK):
            rows = slice(ci * CHUNK, (ci + 1) * CHUNK)
            dm_c = dmixed[rows, :]
            vn_c = vn[rows, :].astype(BF16)
            dbsp = dbsp + dm_c
            dvn_c = jnp.zeros((CHUNK, SGW), F32)
            for g in range(4):
                dm_g = jnp.where(cmasks[g], dm_c, 0.0).astype(BF16)
                dw = lax.dot_general(dm_g, vn_c, NT, preferred_element_type=F32)
                dwsp_ref[g] += jnp.where(tril, dw, 0.0)
                dvn_c = dvn_c + lax.dot_general(wc_bf[g], dm_g, TN, preferred_element_type=F32)
            dvn_chunks.append(dvn_c)
        dbsp_ref[...] += dbsp
        dvn = jnp.concatenate(dvn_chunks, axis=0) if len(dvn_chunks) > 1 else dvn_chunks[0]
        dgln_ref[...] += jnp.sum(dvn * xh, axis=0, keepdims=True)
        dbln_ref[...] += jnp.sum(dvn, axis=0, keepdims=True)
        dxh = dvn * g_ln
        dvv = rs * (dxh - _group_mean(dxh, masks) - xh * _group_mean(dxh * xh, masks))
        dzb_ref[:, 0:256] = (du * _gelu_grad(zu)).astype(BF16)
        dzb_ref[:, 256:512] = (dvv * _gelu_grad(zv)).astype(BF16)

        cwv = cw_ref[...]
        gb_, gc, hh = zb_ref[:, 512:768], zb_ref[:, 768:1024], zb_ref[:, 1024:1280]
        yv, sh1, sh2, conv = _conv_forward(gc, hh, zprev_ref[:, 768:1024], zprev_ref[:, 1024:1280], i == 0, cwv)
        dconv = dyc * gb_
        dzb_ref[:, 512:768] = (dyc * conv).astype(BF16)
        dcw_ref[0:1, :] += jnp.sum(dconv * sh2, axis=0, keepdims=True)
        dcw_ref[1:2, :] += jnp.sum(dconv * sh1, axis=0, keepdims=True)
        dcw_ref[2:3, :] += jnp.sum(dconv * yv, axis=0, keepdims=True)
        dycn, _ = _rms_bwd(ycn_ref[...], go[:, 768:1024], dmn_ref[...])
        dconv_next = jnp.where(i == nsteps - 1, 0.0, dycn * znext_ref[:, 512:768])
        n0, n1 = _pick_row(dconv_next, 0), _pick_row(dconv_next, 1)
        dyv = dconv * cwv[2:3, :] + _shift_up(dconv, 1, [n0]) * cwv[1:2, :] + _shift_up(dconv, 2, [n0, n1]) * cwv[0:1, :]
        dzb_ref[:, 768:1024] = (dyv * hh).astype(BF16)
        dzb_ref[:, 1024:1280] = (dyv * gc).astype(BF16)

    prev_map = lambda i: (jnp.maximum(i * hb - 1, 0), 0)
    next_map = lambda i: (jnp.minimum((i + 1) * hb, last_blk), 0)
    names = ("out_norm_g", "sg_ln_g", "sg_ln_b", "w_sp", "b_sp_t", "conv_w")
    shapes = ((1, D), (1, SGW), (1, SGW), (4, CHUNK, CHUNK), (CHUNK, SGW), (3, CVW))
    outs = _pcall(
        body, "mixer_bwd", (nsteps,),
        [dmix, sv["ya"], sv["yb"], sv["yc"], zb, zb, zb, sv["yc"], dmix, p["sg_ln_g"], p["sg_ln_b"], p["w_sp"], p["b_sp"],
         p["conv_w"], p["out_norm_g"]],
        [_row(tm, D), _row(tm, 512), _row(tm, SGW), _row(tm, CVW), _row(tm, ZB),
         pl.BlockSpec((8, ZB), prev_map), pl.BlockSpec((8, ZB), next_map), pl.BlockSpec((8, CVW), next_map),
         pl.BlockSpec((8, 256), lambda i: (jnp.minimum((i + 1) * hb, last_blk), 3)),
         _lyr(l, 1, SGW), _lyr(l, 1, SGW), _lyr(l, 4, CHUNK, CHUNK), _lyr(l, CHUNK, SGW), _lyr(l, 3, CVW), _lyr(l, 1, D)],
        [_row(tm, 512), _row(tm, ZB), pl.BlockSpec((HEADS, tm, 1), lambda i: (0, i, 0))] + [_lyr(l, *s) for s in shapes],
        [_sds((t, 512), BF16), _sds((t, ZB), BF16), _sds((HEADS, t, 1), F32)] + [_sds((depth,) + s, F32) for s in shapes],
        ("arbitrary",), prevs={3 + n: gb.get(name) for n, name in enumerate(names)})
    for n, name in enumerate(names):
        gb[name] = outs[3 + n]
    return outs[0], outs[1], outs[2]


def _attn_bwd_dq(qs, k, v, dya, lse, delta):
    t = qs.shape[0]
    tq = _att_tile(t)
    nq = t // tq

    def body(q_ref, k_ref, v_ref, do_ref, lse_ref, dl_ref, dq_ref, acc_s):
        i, j = pl.program_id(1), pl.program_id(2)

        @pl.when(j == 0)
        def _():
            acc_s[...] = jnp.zeros(acc_s.shape, F32)

        def step(masked):
            keep = _causal_keep(tq, i, j) if masked else None
            lane = lax.broadcasted_iota(jnp.int32, (tq, 128), 1)
            vv = v_ref[...]
            do = do_ref[...]
            for hh in range(2):
                sl = slice(128 * hh, 128 * hh + 128)
                kk = k_ref[:, sl]
                s = lax.dot_general(q_ref[:, sl], kk, NT, preferred_element_type=F32)
                p = jnp.exp2(s - lse_ref[hh])
                if masked:
                    p = jnp.where(keep, p, 0.0)
                do_h = jnp.where((lane < VD) if hh == 0 else (lane >= VD), do, jnp.zeros_like(do))
                dp = lax.dot_general(do_h, vv, NT, preferred_element_type=F32)
                ds = (p * (dp - dl_ref[hh])).astype(BF16)
                acc_s[:, sl] += jnp.dot(ds, kk, preferred_element_type=F32)

        @pl.when(j < i)
        def _():
            step(False)

        @pl.when(j == i)
        def _():
            step(True)
            dq_ref[...] = acc_s[...].astype(BF16)

    col_spec = pl.BlockSpec((2, tq, 1), lambda p, i, j: (p, i, 0))
    return pl.pallas_call(
        body, name="attn_bwd_dq", grid=(HEADS // 2, nq, nq),
        in_specs=[pl.BlockSpec((tq, 256), lambda p, i, j: (i, p)),
                  pl.BlockSpec((tq, 256), lambda p, i, j: (jnp.minimum(j, i), p)),
                  pl.BlockSpec((tq, 128), lambda p, i, j: (jnp.minimum(j, i), p)),
                  pl.BlockSpec((tq, 128), lambda p, i, j: (i, p)), col_spec, col_spec],
        out_specs=pl.BlockSpec((tq, 256), lambda p, i, j: (i, p)),
        out_shape=_sds((t, QW), BF16),
        scratch_shapes=[pltpu.VMEM((tq, 256), F32)],
        compiler_params=_cp("parallel", "parallel", "arbitrary"))(qs, k, v, dya, lse, delta)


def _attn_bwd_dkv(qs, k, v, dya, lse_row, delta_row):
    t = qs.shape[0]
    tk = _att_tile(t)
    nk = t // tk

    def body(q_ref, k_ref, v_ref, do_ref, lse_ref, dl_ref, dk_ref, dv_ref, dk_s, dv_s):
        j, i = pl.program_id(1), pl.program_id(2)

        def step(masked):
            lane = lax.broadcasted_iota(jnp.int32, (tk, 128), 1)
            vv = v_ref[...]
            do = do_ref[...]
            if masked:
                krow = lax.broadcasted_iota(jnp.int32, (tk, tk), 0)
                qcol = lax.broadcasted_iota(jnp.int32, (tk, tk), 1)
                keep = krow <= qcol
            for hh in range(2):
                sl = slice(128 * hh, 128 * hh + 128)
                qq = q_ref[:, sl]
                st = lax.dot_general(k_ref[:, sl], qq, NT, preferred_element_type=F32)
                pt = jnp.exp2(st - lse_ref[hh])
                if masked:
                    pt = jnp.where(keep, pt, 0.0)
                do_h = jnp.where((lane < VD) if hh == 0 else (lane >= VD), do, jnp.zeros_like(do))
                dv_s[...] += jnp.dot(pt.astype(BF16), do_h, preferred_element_type=F32)
                dpt = lax.dot_general(vv, do_h, NT, preferred_element_type=F32)
                dst = (pt * (dpt - dl_ref[hh])).astype(BF16)
                dk_s[:, sl] += jnp.dot(dst, qq, preferred_element_type=F32)

        @pl.when(i == j)
        def _():
            dk_s[...] = jnp.zeros(dk_s.shape, F32)
            dv_s[...] = jnp.zeros(dv_s.shape, F32)
            step(True)

        @pl.when(i > j)
        def _():
            step(False)

        @pl.when(i == nk - 1)
        def _():
            dk_ref[...] = dk_s[...].astype(BF16)
            dv_ref[...] = (dv_s[...] * LOG2E).astype(BF16)

    row_spec = pl.BlockSpec((2, 1, tk), lambda p, j, i: (p, 0, jnp.maximum(i, j)))
    return pl.pallas_call(
        body, name="attn_bwd_dkv", grid=(HEADS // 2, nk, nk),
        in_specs=[pl.BlockSpec((tk, 256), lambda p, j, i: (jnp.maximum(i, j), p)),
                  pl.BlockSpec((tk, 256), lambda p, j, i: (j, p)),
                  pl.BlockSpec((tk, 128), lambda p, j, i: (j, p)),
                  pl.BlockSpec((tk, 128), lambda p, j, i: (jnp.maximum(i, j), p)), row_spec, row_spec],
        out_specs=[pl.BlockSpec((tk, 256), lambda p, j, i: (j, p)), pl.BlockSpec((tk, 128), lambda p, j, i: (j, p))],
        out_shape=[_sds((t, QW), BF16), _sds((t, HEADS * VD), BF16)],
        scratch_shapes=[pltpu.VMEM((tk, 256), F32), pltpu.VMEM((tk, 128), F32)],
        compiler_params=_cp("parallel", "parallel", "arbitrary"))(qs, k, v, dya, lse_row, delta_row)


def _mla_prep_bwd(dqs, dk, dv, sv, p, l, depth, gb, tabs):
    za = sv["za"]
    t = za.shape[0]
    tm = min(ROW_TILE, t)

    def body(dq_ref, dk_ref, dv_ref, z_ref, gq_ref, gkv_ref, wuq_ref, wukv_ref, c_ref, s1_ref, s2_ref,
             dza_ref, dqp_ref, dkv_ref, dgq_ref, dgkv_ref):
        _acc_init(pl.program_id(0), dgq_ref, dgkv_ref)
        c, s1, s2 = c_ref[...], s1_ref[...], s2_ref[...]
        lane = lax.broadcasted_iota(jnp.int32, (tm, 128), 1)
        rope_lanes = (lane >= NOPE) & (lane < NOPE + ROPE)
        dkr = jnp.zeros((tm, 128), F32)
        for h in range(HEADS):
            sl = slice(128 * h, 128 * h + 128)
            dqp_ref[:, sl] = _rope_bwd(dq_ref[:, sl].astype(F32) * QSCALE, c, s1, s2).astype(BF16)
            dkh = dk_ref[:, sl]
            dkv_ref[:, sl] = dkh
            dkr = dkr + jnp.where(rope_lanes, dkh.astype(F32), 0.0)
        dkv_ref[:, QW:] = dv_ref[...]
        z = z_ref[...]
        dcq = lax.dot_general(dqp_ref[...], wuq_ref[...], NT, preferred_element_type=F32)
        dzq, dgq = _rms_bwd(z[:, :QR], gq_ref[...], dcq)
        dckv = lax.dot_general(dkv_ref[...], wukv_ref[...], NT, preferred_element_type=F32)
        dzkv, dgkv = _rms_bwd(z[:, QR:QR + KVR], gkv_ref[...], dckv)
        dgq_ref[...] += dgq
        dgkv_ref[...] += dgkv
        dza_ref[:, :QR] = dzq.astype(BF16)
        dza_ref[:, QR:QR + KVR] = dzkv.astype(BF16)
        dza_ref[:, QR + KVR:] = _rope_bwd(dkr, c, s1, s2).astype(BF16)

    dza, dqp, dkv, gb["q_norm_g"], gb["kv_norm_g"] = _pcall(
        body, "mla_prep_bwd", (t // tm,),
        [dqs, dk, dv, za, p["q_norm_g"], p["kv_norm_g"], p["w_uq"], p["w_ukv"], *tabs],
        [_row(tm, QW), _row(tm, QW), _row(tm, HEADS * VD), _row(tm, ZA), _lyr(l, 1, QR), _lyr(l, 1, KVR),
         _lyr(l, QR, QW), _lyr(l, KVR, KVW), _row(tm, 128), _row(tm, 128), _row(tm, 128)],
        [_row(tm, ZA), _row(tm, QW), _row(tm, KVW), _lyr(l, 1, QR), _lyr(l, 1, KVR)],
        [_sds((t, ZA), BF16), _sds((t, QW), BF16), _sds((t, KVW), BF16), _sds((depth, 1, QR), F32),
         _sds((depth, 1, KVR), F32)], ("arbitrary",), prevs={3: gb.get("q_norm_g"), 4: gb.get("kv_norm_g")})
    return dza, dqp, dkv


def _in_proj_bwd(dza, dzb, dx2, sv, p, l, depth, gb):
    t = dx2.shape[0]
    tm = min(ROW_TILE, t)

    def body(dza_ref, dzb_ref, wa_ref, wb_ref, x_ref, dx2_ref, g_ref, dx_ref, dg_ref):
        _acc_init(pl.program_id(0), dg_ref)
        dh = (lax.dot_general(dza_ref[...], wa_ref[...], NT, preferred_element_type=F32)
              + lax.dot_general(dzb_ref[...], wb_ref[...], NT, preferred_element_type=F32))
        dx, dg = _rms_bwd(x_ref[...], g_ref[...], dh)
        dg_ref[...] += dg
        dx_ref[...] = dx2_ref[...] + dx

    dx, gb["mix_pre_g"] = _pcall(
        body, "in_proj_bwd", (t // tm,), [dza, dzb, p["w_in_a"], p["w_in_b"], sv["x"], dx2, p["mix_pre_g"]],
        [_row(tm, ZA), _row(tm, ZB), _lyr(l, D, ZA), _lyr(l, D, ZB), _row(tm, D), _row(tm, D), _lyr(l, 1, D)],
        [_row(tm, D), _lyr(l, 1, D)], [_sds((t, D), F32), _sds((depth, 1, D), F32)], ("arbitrary",),
        prevs={1: gb.get("mix_pre_g")})
    return dx


def _mm_tn(a, b, tn, name, l, depth, gb):
    t, k = a.shape
    n = b.shape[1]
    tt = min(ROW_TILE, t)

    def body(a_ref, b_ref, o_ref):
        _acc_init(pl.program_id(1), o_ref)
        o_ref[...] += lax.dot_general(a_ref[...], b_ref[...], TN, preferred_element_type=F32)

    gb[name] = _pcall(
        body, "d" + name, (n // tn, t // tt), [a, b],
        [pl.BlockSpec((tt, k), lambda j, s: (s, 0)), pl.BlockSpec((tt, tn), lambda j, s: (s, j))],
        pl.BlockSpec((None, k, tn), lambda j, s: (l, 0, j)), _sds((depth, k, n), F32), ("parallel", "arbitrary"),
        prevs={0: gb.get(name)})


def _dw_ffn(a, b, kind, l, depth, gb):
    t = a.shape[0]
    tt = min(ROW_TILE, t)
    nsteps = t // tt
    half = depth // 2

    def body(a_ref, b_ref, o_ref, acc):
        s = pl.program_id(1)
        _acc_init(s, acc)
        acc[...] += lax.dot_general(a_ref[...], b_ref[...], TN, preferred_element_type=F32)

        @pl.when(s == nsteps - 1)
        def _():
            o_ref[...] = acc[...].astype(BF16)

    whole = lambda n: pl.BlockSpec((tt, n), lambda k, s: (s, 0))
    block = pl.BlockSpec((tt, HP), lambda k, s: (s, k))
    if kind == "down":
        in_specs, acc_shape, name = [block, whole(D)], (HP, D), "down"
        out_spec = pl.BlockSpec((None, None, None, HP, D), lambda k, s: (k, l // half, l % half, 0, 0))
        out_shape = _sds((4, 2, half, HP, D), BF16)
    else:
        which = 0 if kind == "gate" else 1
        in_specs, acc_shape, name = [whole(D), block], (D, HP), "gu"
        out_spec = pl.BlockSpec((None, None, None, None, D, HP), lambda k, s: (k, l // half, l % half, which, 0, 0))
        out_shape = _sds((4, 2, half, 2, D, HP), BF16)
    gb[name] = _pcall(body, "dw_" + kind, (4, nsteps), [a, b], in_specs, out_spec, out_shape, ("parallel", "arbitrary"),
                      scratch=[pltpu.VMEM(acc_shape, F32)], prevs={0: gb.get(name)})


def _layer_fwd(x, p, l, tabs):
    h1, za, zb = _in_proj(x, p, l)
    cqn, ckvn, qs, k, v = _mla_prep(za, p, l, tabs)
    ya, lse = _attn_fwd(qs, k, v)
    mix, yb, yc = _mixer_fwd(zb, ya, p, l)
    o, x2, h2 = _out_proj(mix, x, p, l)
    a, b, s = _ffn_up(h2, p, l)
    f, x3 = _ffn_down(s, x2, p, l)
    saved = dict(x=x, h1=h1, za=za, zb=zb, cqn=cqn, ckvn=ckvn, qs=qs, k=k, v=v, ya=ya, lse=lse, mix=mix, yb=yb, yc=yc,
                 o=o, x2=x2, h2=h2, a=a, b=b, s=s, f=f)
    return x3, saved


def _layer_bwd(dx3, p, sv, l, depth, gb, tabs):
    t = dx3.shape[0]
    df, da, db = _ffn_down_bwd(dx3, sv, p, l, depth, gb)
    _dw_ffn(sv["s"], df, "down", l, depth, gb)
    dx2 = _ffn_up_bwd(da, db, dx3, sv, p, l, depth, gb)
    _dw_ffn(sv["h2"], da, "gate", l, depth, gb)
    _dw_ffn(sv["h2"], db, "up", l, depth, gb)
    do, dmix = _out_proj_bwd(dx2, sv, p, l, depth, gb)
    _mm_tn(sv["mix"], do, D, "w_out", l, depth, gb)
    dya, dzb, delta = _mixer_bwd(dmix, sv, p, l, depth, gb)
    dqs = _attn_bwd_dq(sv["qs"], sv["k"], sv["v"], dya, sv["lse"], delta)
    dk, dv = _attn_bwd_dkv(sv["qs"], sv["k"], sv["v"], dya, sv["lse"].reshape(HEADS, 1, t), delta.reshape(HEADS, 1, t))
    dza, dqp, dkv = _mla_prep_bwd(dqs, dk, dv, sv, p, l, depth, gb, tabs)
    _mm_tn(sv["cqn"], dqp, QW, "w_uq", l, depth, gb)
    _mm_tn(sv["ckvn"], dkv, KVW, "w_ukv", l, depth, gb)
    _mm_tn(sv["h1"], dza, ZA, "w_in_a", l, depth, gb)
    _mm_tn(sv["h1"], dzb, ZB, "w_in_b", l, depth, gb)
    return _in_proj_bwd(dza, dzb, dx2, sv, p, l, depth, gb)


def _rope_tables(positions):
    inv_freq = 1.0 / (ROPE_THETA ** (jnp.arange(0, ROPE // 2, dtype=F32) / (ROPE // 2)))
    ang = positions.astype(F32)[:, None] * inv_freq
    cos, sin = jnp.cos(ang), jnp.sin(ang)
    t = positions.shape[0]
    one, zero = jnp.ones((t, 64), F32), jnp.zeros((t, 16), F32)
    c = jnp.concatenate([one, cos, cos, one[:, :32]], axis=1)
    s1 = jnp.concatenate([zero, zero, zero, zero, -sin, zero, zero, zero], axis=1)
    s2 = jnp.concatenate([zero, zero, zero, zero, zero, sin, zero, zero], axis=1)
    return c, s1, s2


def _kernel_params(full):
    w_in = full["w_in"]
    depth = w_in.shape[0]
    zpad = lambda n: jnp.zeros((depth, D, n), w_in.dtype)
    kv = full["w_ukv"].reshape(depth, KVR, HEADS, NOPE + VD)
    p = {
        "w_in_a": jnp.concatenate([w_in[:, :, :640], zpad(64), w_in[:, :, 640:672], zpad(32)], axis=2),
        "w_in_b": w_in[:, :, 672:],
        "w_uq": jnp.pad(full["w_uq"].reshape(depth, QR, HEADS, NOPE + ROPE),
                        ((0, 0), (0, 0), (0, 0), (0, 32))).reshape(depth, QR, QW),
        "w_ukv": jnp.concatenate([jnp.pad(kv[..., :NOPE], ((0, 0), (0, 0), (0, 0), (0, 64))).reshape(depth, KVR, QW),
                                  kv[..., NOPE:].reshape(depth, KVR, HEADS * VD)], axis=2),
        "w_out": full["w_out"], "w_sp": full["w_sp"], "conv_w": full["conv_w"],
        "b_sp": jnp.repeat(jnp.swapaxes(full["b_sp"], 1, 2), 64, axis=2),
    }
    for n in ("mix_pre_g", "mix_post_g", "ffn_pre_g", "ffn_post_g", "q_norm_g", "kv_norm_g", "sg_ln_g", "sg_ln_b",
              "out_norm_g"):
        p[n] = full[n][:, None, :]
    return p


def _natural_grads(gb):
    depth = gb["w_in_a"].shape[0]
    ga, kv = gb["w_in_a"], gb["w_ukv"]
    out = {
        "w_in": jnp.concatenate([ga[:, :, :640], ga[:, :, 704:736], gb["w_in_b"]], axis=2),
        "w_uq": gb["w_uq"].reshape(depth, QR, HEADS, 128)[..., :NOPE + ROPE].reshape(depth, QR, HEADS * (NOPE + ROPE)),
        "w_ukv": jnp.concatenate([kv[:, :, :QW].reshape(depth, KVR, HEADS, 128)[..., :NOPE],
                                  kv[:, :, QW:].reshape(depth, KVR, HEADS, VD)], axis=3).reshape(depth, KVR, -1),
        "b_sp": jnp.swapaxes(gb["b_sp_t"].reshape(depth, CHUNK, 4, 64).sum(axis=-1), 1, 2),
    }
    for n in ("w_out", "w_sp", "conv_w"):
        out[n] = gb[n]
    for n in ("mix_pre_g", "mix_post_g", "ffn_pre_g", "ffn_post_g", "q_norm_g", "kv_norm_g", "sg_ln_g", "sg_ln_b",
              "out_norm_g"):
        out[n] = gb[n][:, 0, :]
    return out


def _local_step(x, positions, target, full, w_gu, w_down):
    depth = full["w_in"].shape[0]
    tabs = _rope_tables(positions)
    p = _kernel_params(full)
    p["w_gu"], p["w_down"] = w_gu, w_down
    saved = []
    for l in range(depth):
        x, sv = _layer_fwd(x, p, l, tabs)
        saved.append(sv)
    dx, acc = _loss_head(x, target)
    loss = (0.5 / D) * jnp.sum(acc)
    gb = {}
    for l in reversed(range(depth)):
        dx = _layer_bwd(dx, p, saved[l], l, depth, gb, tabs)
    return loss, dx, _natural_grads(gb), gb["gu"], gb["down"]


def _place():
    x, y, c = lax.axis_index("x"), lax.axis_index("y"), lax.axis_index("c")
    chips = [(1 - x, y), (x, 1 - y), (1 - x, 1 - y)]
    return x, y, c, 2 * x + y, chips


def _remote(src, dst, send_sem, recv_sem, to):
    return pltpu.make_async_remote_copy(src_ref=src, dst_ref=dst, send_sem=send_sem, recv_sem=recv_sem, device_id=to,
                                        device_id_type=MESH_ID)


def _gather_shards(mine):
    nb = len(mine)

    def body(*refs):
        mine_refs, out_refs = refs[:nb], refs[nb:2 * nb]
        send_sems, recv_sems, fsend_sems, frecv_sems = refs[2 * nb:]
        x, y, c, k, chips = _place()
        sib = (x, y, 1 - c)
        sends, passed = [], []
        for b in range(nb):
            for n, (cx, cy) in enumerate(chips):
                sends.append(_remote(mine_refs[b].at[c], out_refs[b].at[k, c], send_sems.at[b, n], recv_sems.at[b, n],
                                     (cx, cy, c)))
        for cp in sends:
            cp.start()
        for n, (cx, cy) in enumerate(chips):
            kj = 2 * cx + cy
            for b in range(nb):
                _remote(mine_refs[b].at[c], out_refs[b].at[kj, c], send_sems.at[b, n], recv_sems.at[b, n],
                        (cx, cy, c)).wait_recv()
                fwd = _remote(out_refs[b].at[kj, c], out_refs[b].at[kj, c], fsend_sems.at[b, n], frecv_sems.at[b, n], sib)
                fwd.start()
                passed.append(fwd)
        for n, (cx, cy) in enumerate(chips):
            kj = 2 * cx + cy
            for b in range(nb):
                _remote(mine_refs[b].at[1 - c], out_refs[b].at[kj, 1 - c], fsend_sems.at[b, n], frecv_sems.at[b, n],
                        sib).wait_recv()
        for cp in sends + passed:
            cp.wait_send()

    return pl.pallas_call(
        body, name="gather_shards", in_specs=[ANY] * nb, out_specs=[ANY] * nb,
        out_shape=[_sds((4,) + a.shape, a.dtype) for a in mine],
        scratch_shapes=[pltpu.SemaphoreType.DMA((nb, 3))] * 4)(*mine)


def _swap_halves(bigs, small):
    nb = len(bigs)

    def body(*refs):
        big_refs, small_ref = refs[:nb], refs[nb]
        rbig_refs, rsmall_ref = refs[nb + 1:2 * nb + 1], refs[2 * nb + 1]
        send_sems, recv_sems = refs[2 * nb + 2:]
        x, y, c, _, _ = _place()
        sib = (x, y, 1 - c)
        cps = [_remote(big_refs[b].at[:, 1 - c], rbig_refs[b], send_sems.at[b], recv_sems.at[b], sib) for b in range(nb)]
        cps.append(_remote(small_ref, rsmall_ref, send_sems.at[nb], recv_sems.at[nb], sib))
        for cp in cps:
            cp.start()
        for cp in cps:
            cp.wait()

    return pl.pallas_call(
        body, name="swap_halves", in_specs=[ANY] * (nb + 1), out_specs=[ANY] * (nb + 1),
        out_shape=[_sds((4,) + a.shape[2:], a.dtype) for a in bigs] + [_sds(small.shape, small.dtype)],
        scratch_shapes=[pltpu.SemaphoreType.DMA((nb + 1,))] * 2)(*bigs, small)


def _sum_tile(r):
    return max(cand for cand in range(16, 641, 16) if r % cand == 0)


def _pair_sum(big, rbig, c):
    _, _, r, w = big.shape
    tr = _sum_tile(r)

    def body(c_ref, big_ref, rbig_ref, p_ref):
        p_ref[...] = (big_ref[...].astype(F32) + rbig_ref[...].astype(F32)).astype(BF16)

    grid_spec = pltpu.PrefetchScalarGridSpec(
        num_scalar_prefetch=1, grid=(4, r // tr),
        in_specs=[pl.BlockSpec((None, None, tr, w), lambda j, i, cr: (j, cr[0], i, 0)),
                  pl.BlockSpec((None, tr, w), lambda j, i, cr: (j, i, 0))],
        out_specs=pl.BlockSpec((None, tr, w), lambda j, i, cr: (j, i, 0)))
    return pl.pallas_call(body, name="pair_sum", grid_spec=grid_spec, out_shape=_sds((4, r, w), BF16),
                          compiler_params=_cp("parallel", "parallel"))(c, big, rbig)


def _small_sum(parts):
    n, ns, _ = parts.shape

    def body(p_ref, o_ref):
        s = p_ref[0]
        for j in range(1, n):
            s = s + p_ref[j]
        o_ref[...] = s

    return pl.pallas_call(body, name="small_sum", out_shape=_sds((ns, 128), F32))(parts)


def _chip_exchange(ps, small):
    nb = len(ps)
    ns = small.shape[0]

    def body(*refs):
        p_refs, small_ref = refs[:nb], refs[nb]
        rb_refs, rs_ref = refs[nb + 1:2 * nb + 1], refs[2 * nb + 1]
        send_sems, recv_sems, local_sem = refs[2 * nb + 2:]
        x, y, c, k, chips = _place()
        loc = pltpu.make_async_copy(small_ref, rs_ref.at[k], local_sem)
        loc.start()
        sends = []
        for n, (cx, cy) in enumerate(chips):
            to = (cx, cy, c)
            for b in range(nb):
                sends.append(_remote(p_refs[b].at[2 * cx + cy], rb_refs[b].at[n], send_sems.at[b, n], recv_sems.at[b, n], to))
            sends.append(_remote(small_ref, rs_ref.at[k], send_sems.at[nb, n], recv_sems.at[nb, n], to))
        for cp in sends:
            cp.start()
        for n, (cx, cy) in enumerate(chips):
            to = (cx, cy, c)
            for b in range(nb):
                _remote(p_refs[b].at[k], rb_refs[b].at[n], send_sems.at[b, n], recv_sems.at[b, n], to).wait_recv()
            _remote(small_ref, rs_ref.at[2 * cx + cy], send_sems.at[nb, n], recv_sems.at[nb, n], to).wait_recv()
        for cp in sends:
            cp.wait_send()
        loc.wait()

    return pl.pallas_call(
        body, name="chip_exchange", in_specs=[ANY] * (nb + 1), out_specs=[ANY] * (nb + 1),
        out_shape=[_sds((3,) + a.shape[1:], a.dtype) for a in ps] + [_sds((4, ns, 128), small.dtype)],
        scratch_shapes=[pltpu.SemaphoreType.DMA((nb + 1, 3))] * 2 + [pltpu.SemaphoreType.DMA(())])(*ps, small)


def _chip_sum(p, rb, chip):
    _, r, w = p.shape
    tr = _sum_tile(r)

    def body(k_ref, p_ref, rb_ref, o_ref):
        acc = p_ref[...].astype(F32)
        for j in range(3):
            acc = acc + rb_ref[j].astype(F32)
        o_ref[...] = acc

    grid_spec = pltpu.PrefetchScalarGridSpec(
        num_scalar_prefetch=1, grid=(r // tr,),
        in_specs=[pl.BlockSpec((None, tr, w), lambda i, kr: (kr[0], i, 0)), pl.BlockSpec((3, tr, w), lambda i, kr: (0, i, 0))],
        out_specs=pl.BlockSpec((tr, w), lambda i, kr: (i, 0)))
    return pl.pallas_call(body, name="chip_sum", grid_spec=grid_spec, out_shape=_sds((r, w), F32),
                          compiler_params=_cp("parallel"))(chip, p, rb)


def _send_to_sibling(reds):
    nb = len(reds)

    def body(*refs):
        red_refs, out_refs = refs[:nb], refs[nb:2 * nb]
        send_sems, recv_sems = refs[2 * nb:]
        x, y, c, _, _ = _place()
        cps = [_remote(red_refs[b], out_refs[b], send_sems.at[b], recv_sems.at[b], (x, y, 1 - c)) for b in range(nb)]
        for cp in cps:
            cp.start()
        for cp in cps:
            cp.wait()

    return pl.pallas_call(
        body, name="send_to_sibling", in_specs=[ANY] * nb, out_specs=[ANY] * nb,
        out_shape=[_sds(a.shape, a.dtype) for a in reds], scratch_shapes=[pltpu.SemaphoreType.DMA((nb,))] * 2)(*reds)


def _adam_math(w, g, m, v):
    nm = ADAM_B1 * m + (1.0 - ADAM_B1) * g
    nv = ADAM_B2 * v + (1.0 - ADAM_B2) * (g * g)
    m_hat = nm / (1.0 - ADAM_B1 ** ADAM_STEP)
    v_hat = nv / (1.0 - ADAM_B2 ** ADAM_STEP)
    return -ADAM_LR * (m_hat / (jnp.sqrt(v_hat) + ADAM_EPS) + ADAM_WD * w), nm, nv


def _adamw_shard(w, m, v, own, other, c, name, pick=None):
    depth, r, n = w.shape
    half = depth // 2
    tr = max(cand for cand in range(8, min(r, 256) + 1, 8) if r % cand == 0)
    npad = own.shape[-1]

    def body(c_ref, w_ref, m_ref, v_ref, own_ref, oth_ref, g_ref, d_ref, nm_ref, nv_ref):
        mine = (pl.program_id(0) // half) == c_ref[0]
        g = jnp.where(mine, own_ref[...], oth_ref[...])[:, :n]
        g_ref[...] = g
        d_ref[...], nm_ref[...], nv_ref[...] = _adam_math(w_ref[...], g, m_ref[...], v_ref[...])

    blk = pl.BlockSpec((None, tr, n), lambda l, i, cr: (l, i, 0))
    if pick is None:
        gblk = pl.BlockSpec((None, tr, npad), lambda l, i, cr: (l % half, i, 0))
    else:
        gblk = pl.BlockSpec((None, None, tr, npad), lambda l, i, cr: (l % half, pick, i, 0))
    grid_spec = pltpu.PrefetchScalarGridSpec(num_scalar_prefetch=1, grid=(depth, r // tr), in_specs=[blk] * 3 + [gblk] * 2,
                                             out_specs=[blk] * 4)
    return pl.pallas_call(body, name=name, grid_spec=grid_spec, out_shape=[_sds(w.shape, F32)] * 4,
                          compiler_params=_cp("parallel", "parallel"))(c, w, m, v, own, other)


def _pad_ffn_shards(w_gate, w_up, w_down):
    depth = w_gate.shape[0]
    half = depth // 2
    tm = 256

    def gu_body(g_ref, u_ref, o_ref):
        o_ref[...] = jnp.zeros(o_ref.shape, BF16)
        o_ref[0, :, 0:HS] = g_ref[...].astype(BF16)
        o_ref[1, :, 0:HS] = u_ref[...].astype(BF16)

    blk = pl.BlockSpec((None, tm, HS), lambda l, i: (l, i, 0))
    gu = pl.pallas_call(
        gu_body, name="pad_gate_up", grid=(depth, D // tm), in_specs=[blk, blk],
        out_specs=pl.BlockSpec((None, None, 2, tm, HP), lambda l, i: (l // half, l % half, 0, i, 0)),
        out_shape=_sds((2, half, 2, D, HP), BF16), compiler_params=_cp("parallel", "parallel"))(w_gate, w_up)

    def down_body(w_ref, o_ref):
        o_ref[0:HS, :] = w_ref[...].astype(BF16)
        o_ref[HS:HP, :] = jnp.zeros((HP - HS, D), BF16)

    down = pl.pallas_call(
        down_body, name="pad_down", grid=(depth,), in_specs=[pl.BlockSpec((None, HS, D), lambda l: (l, 0, 0))],
        out_specs=pl.BlockSpec((None, None, HP, D), lambda l: (l // half, l % half, 0, 0)),
        out_shape=_sds((2, half, HP, D), BF16), compiler_params=_cp("parallel"))(w_down)
    return gu, down


def _adamw_small(w, g, m, v):
    r = w.shape[0]
    tr = max(cand for cand in range(8, 513, 8) if r % cand == 0)

    def body(w_ref, g_ref, m_ref, v_ref, d_ref, nm_ref, nv_ref):
        d_ref[...], nm_ref[...], nv_ref[...] = _adam_math(w_ref[...], g_ref[...], m_ref[...], v_ref[...])

    blk = pl.BlockSpec((tr, 128), lambda i: (i, 0))
    return pl.pallas_call(body, name="adamw_small", grid=(r // tr,), in_specs=[blk] * 4, out_specs=[blk] * 3,
                          out_shape=[_sds(w.shape, F32)] * 3, compiler_params=_cp("parallel"))(w, g, m, v)


def _to_pack(a, name):
    depth = a.shape[0]
    if name in ROW_SHARDED:
        return jnp.swapaxes(a.reshape(depth, 4, -1, D), 0, 1)
    return jnp.transpose(a.reshape(depth, a.shape[1], 4, a.shape[2] // 4), (2, 0, 1, 3)).reshape(4, depth, -1, D)


def _pack_rows(parts, lead, dtype, tail=None):
    pieces, at = [], 0
    for n, off, rows in PACK:
        if off > at:
            pieces.append(jnp.zeros(lead + (off - at, D), dtype))
        pieces.append(parts[n].astype(dtype))
        at = off + rows
    if tail is not None:
        pieces.append(tail)
        at += tail.shape[-2]
    pieces.append(jnp.zeros(lead + (PACK_ROWS - at, D), dtype))
    return jnp.concatenate(pieces, axis=len(lead))


def _pack_weight_shards(sh):
    depth = sh["w_in"].shape[0]
    parts = {n: sh[n].reshape(depth, rows, D) for n, _, rows in PACK}
    conv = lax.bitcast_convert_type(sh["conv_w"].reshape(depth, 3 * 64), BF16).reshape(depth, 1, 384)
    flat = _pack_rows(parts, (depth,), BF16, tail=jnp.pad(conv, ((0, 0), (0, 0), (0, D - 384))))
    return flat.reshape(2, depth // 2 * PACK_ROWS, D)


def _unpack_weights(gathered, depth, shard_shapes):
    flat = gathered.reshape(4, depth, PACK_ROWS, D)
    full = {}
    for n, off, rows in PACK:
        shp = shard_shapes[n][1:]
        piece = flat[:, :, off:off + rows, :].reshape((4, depth) + shp)
        if n in ROW_SHARDED:
            full[n] = jnp.transpose(piece, (1, 0, 2, 3)).reshape(depth, 4 * shp[0], shp[1])
        else:
            full[n] = jnp.transpose(piece, (1, 2, 0, 3)).reshape(depth, shp[0], 4 * shp[1])
    conv = lax.bitcast_convert_type(flat[:, :, CONV_ROW, :384].reshape(4, depth, 192, 2), F32)
    full["conv_w"] = jnp.transpose(conv.reshape(4, depth, 3, 64), (1, 2, 0, 3)).reshape(depth, 3, CVW)
    return full


def _pack_grad_shards(g):
    depth = g["w_in"].shape[0]
    flat = _pack_rows({n: _to_pack(g[n], n) for n, _, _ in PACK}, (4, depth), BF16)
    return flat.reshape(4, 2, depth // 2 * PACK_ROWS, D)


def _pack_small(arrs, names_shapes, depth):
    flat = jnp.concatenate([arrs[n].reshape(depth, -1) for n, _ in names_shapes], axis=1).reshape(-1)
    rows = -(-flat.shape[0] // 1024) * 8
    return jnp.pad(flat, (0, rows * 128 - flat.shape[0])).reshape(rows, 128)


def _unpack_small(packed, names_shapes, depth):
    per_layer = sum(math.prod(s) for _, s in names_shapes)
    flat = packed.reshape(-1)[:depth * per_layer].reshape(depth, per_layer)
    out, off = {}, 0
    for n, s in names_shapes:
        size = math.prod(s)
        out[n] = flat[:, off:off + size].reshape((depth,) + s)
        off += size
    return out


def kernel(x, positions, mix_pre_g, mix_post_g, ffn_pre_g, ffn_post_g, w_in, q_norm_g, w_uq, kv_norm_g, w_ukv, sg_ln_g, sg_ln_b, w_sp, b_sp, conv_w, out_norm_g, w_out, w_gate, w_up, w_down, loss_target, m_mix_pre_g, m_mix_post_g, m_ffn_pre_g, m_ffn_post_g, m_w_in, m_q_norm_g, m_w_uq, m_kv_norm_g, m_w_ukv, m_sg_ln_g, m_sg_ln_b, m_w_sp, m_b_sp, m_conv_w, m_out_norm_g, m_w_out, m_w_gate, m_w_up, m_w_down, v_mix_pre_g, v_mix_post_g, v_ffn_pre_g, v_ffn_post_g, v_w_in, v_q_norm_g, v_w_uq, v_kv_norm_g, v_w_ukv, v_sg_ln_g, v_sg_ln_b, v_w_sp, v_b_sp, v_conv_w, v_out_norm_g, v_w_out, v_w_gate, v_w_up, v_w_down):
    w = dict(mix_pre_g=mix_pre_g, mix_post_g=mix_post_g, ffn_pre_g=ffn_pre_g, ffn_post_g=ffn_post_g, w_in=w_in,
             q_norm_g=q_norm_g, w_uq=w_uq, kv_norm_g=kv_norm_g, w_ukv=w_ukv, sg_ln_g=sg_ln_g, sg_ln_b=sg_ln_b, w_sp=w_sp,
             b_sp=b_sp, conv_w=conv_w, out_norm_g=out_norm_g, w_out=w_out, w_gate=w_gate, w_up=w_up, w_down=w_down)
    m = dict(mix_pre_g=m_mix_pre_g, mix_post_g=m_mix_post_g, ffn_pre_g=m_ffn_pre_g, ffn_post_g=m_ffn_post_g, w_in=m_w_in,
             q_norm_g=m_q_norm_g, w_uq=m_w_uq, kv_norm_g=m_kv_norm_g, w_ukv=m_w_ukv, sg_ln_g=m_sg_ln_g, sg_ln_b=m_sg_ln_b,
             w_sp=m_w_sp, b_sp=m_b_sp, conv_w=m_conv_w, out_norm_g=m_out_norm_g, w_out=m_w_out, w_gate=m_w_gate,
             w_up=m_w_up, w_down=m_w_down)
    v = dict(mix_pre_g=v_mix_pre_g, mix_post_g=v_mix_post_g, ffn_pre_g=v_ffn_pre_g, ffn_post_g=v_ffn_post_g, w_in=v_w_in,
             q_norm_g=v_q_norm_g, w_uq=v_w_uq, kv_norm_g=v_kv_norm_g, w_ukv=v_w_ukv, sg_ln_g=v_sg_ln_g, sg_ln_b=v_sg_ln_b,
             w_sp=v_w_sp, b_sp=v_b_sp, conv_w=v_conv_w, out_norm_g=v_out_norm_g, w_out=v_w_out, w_gate=v_w_gate,
             w_up=v_w_up, w_down=v_w_down)
    depth = w_in.shape[0]
    c = lax.axis_index("c").astype(jnp.int32).reshape(1)
    chip = (2 * lax.axis_index("x") + lax.axis_index("y")).astype(jnp.int32)

    half = depth // 2

    mine = [_pack_weight_shards(w), *_pad_ffn_shards(w_gate, w_up, w_down)]
    gathered = [lax.dynamic_update_slice(g, a[None], (chip,) + (0,) * a.ndim) for g, a in zip(_gather_shards(mine), mine)]
    full = _unpack_weights(gathered[0], depth, {n: w[n].shape for n, _, _ in PACK})
    for n, _ in SMALL:
        if n != "conv_w":
            full[n] = w[n]

    loss, dx, grads, g_gu, g_down = _local_step(x[0], positions[0], loss_target[0], full,
                                                gathered[1].reshape(4, depth, 2, D, HP), gathered[2].reshape(4, depth, HP, D))
    loss = lax.psum(loss, ("x", "y", "c"))

    small = _pack_small(grads, SMALL, depth)
    bigs = [_pack_grad_shards(grads), g_gu.reshape(4, 2, half * 2 * D, HP), g_down.reshape(4, 2, half * HP, D)]
    *rbigs, rsmall = _swap_halves(bigs, small)
    ps = [_pair_sum(a, r, c) for a, r in zip(bigs, rbigs)]
    *rbs, rs = _chip_exchange(ps, _small_sum(jnp.stack([small, rsmall])))
    own = [_chip_sum(p, rb, chip.reshape(1)) for p, rb in zip(ps, rbs)]
    other = _send_to_sibling(own)
    g_small = _unpack_small(_small_sum(rs), SMALL, depth)
    g_small["conv_w"] = lax.dynamic_slice_in_dim(g_small["conv_w"], chip * 64, 64, axis=2)

    gw, delta, new_m, new_v = dict(g_small), {}, {}, {}

    def adam(n, own_g, other_g, pick=None):
        shp = w[n].shape
        outs = _adamw_shard(w[n], m[n], v[n], own_g, other_g, c, "adamw_" + n, pick)
        gw[n], delta[n], new_m[n], new_v[n] = outs

    for n, off, rows in PACK:
        nat = lambda a: a.reshape(half, PACK_ROWS, D)[:, off:off + rows, :].reshape((half,) + w[n].shape[1:])
        adam(n, nat(own[0]), nat(other[0]))
    adam("w_gate", own[1].reshape(half, 2, D, HP), other[1].reshape(half, 2, D, HP), 0)
    adam("w_up", own[1].reshape(half, 2, D, HP), other[1].reshape(half, 2, D, HP), 1)
    adam("w_down", own[2].reshape(half, HP, D), other[2].reshape(half, HP, D))
    small_local = tuple((n, w[n].shape[1:]) for n, _ in SMALL)
    d_, m_, v_ = _adamw_small(_pack_small(w, small_local, depth), _pack_small(gw, small_local, depth),
                              _pack_small(m, small_local, depth), _pack_small(v, small_local, depth))
    delta.update(_unpack_small(d_, small_local, depth))
    new_m.update(_unpack_small(m_, small_local, depth))
    new_v.update(_unpack_small(v_, small_local, depth))

    return (loss, dx[None], *[gw[n] for n in WEIGHTS], *[delta[n] for n in WEIGHTS], *[new_m[n] for n in WEIGHTS],
            *[new_v[n] for n in WEIGHTS])
```

```python
import math

import jax
import jax.numpy as jnp
from jax import lax
from jax.experimental import pallas as pl
from jax.experimental.pallas import tpu as pltpu

F32 = jnp.float32
BF16 = jnp.bfloat16

D = 1024
HEADS = 8
NOPE = 64
ROPE = 32
VD = 64
QR = 384
KVR = 256
SGW = 256
CVW = 256
CHUNK = 128
DFF = 2816
EPS = 1e-6
ROPE_THETA = 10000.0
LOG2E = 1.4426950408889634
LN2 = 0.6931471805599453
QSCALE = (NOPE + ROPE) ** -0.5 * LOG2E
ZA = 768
ZB = 1280
QW = HEADS * 128
KVW = HEADS * 128 + HEADS * VD
NEG = -1e30
GC0 = 0.7978845608028654
GC1 = 0.044715

ADAM_LR = 0.001
ADAM_B1 = 0.9
ADAM_B2 = 0.999
ADAM_EPS = 1e-08
ADAM_WD = 0.01
ADAM_STEP = 10

V7X_VMEM_LIMIT = 52 * 1024 * 1024
ROW_TILE = 512
ATT_TILE = 512

NT = (((1,), (1,)), ((), ()))
TN = (((0,), (0,)), ((), ()))

HS = DFF // 4
HP = 768
DFFP = 4 * HP

PACK = (("w_in", 0, 488), ("w_out", 512, 256), ("w_ukv", 768, 64), ("w_uq", 832, 72))
CONV_ROW = 904
PACK_ROWS = 912
ROW_SHARDED = ("w_out",)
SMALL = (("mix_pre_g", (D,)), ("mix_post_g", (D,)), ("ffn_pre_g", (D,)), ("ffn_post_g", (D,)), ("q_norm_g", (QR,)),
         ("kv_norm_g", (KVR,)), ("sg_ln_g", (SGW,)), ("sg_ln_b", (SGW,)), ("w_sp", (4, CHUNK, CHUNK)), ("b_sp", (4, CHUNK)),
         ("conv_w", (3, CVW)), ("out_norm_g", (D,)))
WEIGHTS = ["mix_pre_g", "mix_post_g", "ffn_pre_g", "ffn_post_g", "w_in", "q_norm_g", "w_uq", "kv_norm_g", "w_ukv", "sg_ln_g",
           "sg_ln_b", "w_sp", "b_sp", "conv_w", "out_norm_g", "w_out", "w_gate", "w_up", "w_down"]

MESH_ID = pl.DeviceIdType.MESH
ANY = pl.BlockSpec(memory_space=pl.ANY)


def _cp(*sem):
    return pltpu.CompilerParams(dimension_semantics=sem, vmem_limit_bytes=V7X_VMEM_LIMIT)


def _sds(shape, dtype):
    return jax.ShapeDtypeStruct(shape, dtype)


def _row(tm, n):
    return pl.BlockSpec((tm, n), lambda i: (i, 0))


def _lyr(l, *shape):
    return pl.BlockSpec((None,) + shape, lambda *_: (l,) + (0,) * len(shape))


def _pcall(body, name, grid, ins, in_specs, out_specs, out_shape, sem, scratch=(), prevs=None):
    prevs = {k: v for k, v in (prevs or {}).items() if v is not None}
    order = sorted(prevs)
    n_in = len(ins)

    def wrapped(*refs):
        return body(*refs[:n_in], *refs[n_in + len(order):])

    return pl.pallas_call(
        wrapped, name=name, grid=grid, in_specs=list(in_specs) + [ANY] * len(order), out_specs=out_specs,
        out_shape=out_shape, scratch_shapes=list(scratch),
        input_output_aliases={n_in + i: k for i, k in enumerate(order)},
        compiler_params=_cp(*sem))(*ins, *[prevs[k] for k in order])


def _rms(x, g):
    r = lax.rsqrt(jnp.mean(x * x, axis=-1, keepdims=True) + EPS)
    return x * r * g


def _rms_bwd(x, g, dy):
    r = lax.rsqrt(jnp.mean(x * x, axis=-1, keepdims=True) + EPS)
    xh = x * r
    dg = jnp.sum(dy * xh, axis=0, keepdims=True)
    dxh = dy * g
    dx = r * (dxh - xh * jnp.mean(dxh * xh, axis=-1, keepdims=True))
    return dx, dg


def _gelu(x):
    return 0.5 * x * (1.0 + jnp.tanh(GC0 * (x + GC1 * x * x * x)))


def _gelu_grad(x):
    t = jnp.tanh(GC0 * (x + GC1 * x * x * x))
    return 0.5 * (1.0 + t) + 0.5 * x * (1.0 - t * t) * GC0 * (1.0 + 3.0 * GC1 * x * x)


def _rope(xb, c, s1, s2):
    return xb * c + pltpu.roll(xb, 112, 1) * s1 + pltpu.roll(xb, 16, 1) * s2


def _rope_bwd(dy, c, s1, s2):
    return dy * c + pltpu.roll(dy * s1, 16, 1) + pltpu.roll(dy * s2, 112, 1)


def _group_masks(shape):
    lane = lax.broadcasted_iota(jnp.int32, shape, 1)
    return [(lane >= 64 * g) & (lane < 64 * g + 64) for g in range(shape[1] // 64)]


def _group_mean(v, masks):
    out = jnp.zeros_like(v)
    for m in masks:
        s = jnp.sum(jnp.where(m, v, 0.0), axis=-1, keepdims=True) * (1.0 / 64.0)
        out = jnp.where(m, s, out)
    return out


def _pick_row(blk, idx):
    row = lax.broadcasted_iota(jnp.int32, blk.shape, 0)
    return jnp.sum(jnp.where(row == idx, blk, 0.0), axis=0, keepdims=True)


def _shift_down(y, k, first_rows):
    out = pltpu.roll(y, k, 0)
    row = lax.broadcasted_iota(jnp.int32, y.shape, 0)
    for idx in range(k):
        out = jnp.where(row == idx, first_rows[idx], out)
    return out


def _shift_up(y, k, last_rows):
    n = y.shape[0]
    out = pltpu.roll(y, n - k, 0)
    row = lax.broadcasted_iota(jnp.int32, y.shape, 0)
    for idx in range(k):
        out = jnp.where(row == n - k + idx, last_rows[idx], out)
    return out


def _tril_mask():
    r = lax.broadcasted_iota(jnp.int32, (CHUNK, CHUNK), 0)
    c = lax.broadcasted_iota(jnp.int32, (CHUNK, CHUNK), 1)
    return r >= c


def _sgu_forward(zu, zv, g_ln, b_ln, wc_bf, bsp, masks, cmasks):
    u = _gelu(zu)
    vv = _gelu(zv)
    mu = _group_mean(vv, masks)
    dv = vv - mu
    rs = lax.rsqrt(_group_mean(dv * dv, masks) + EPS)
    xh = dv * rs
    vn = xh * g_ln + b_ln
    chunks = []
    for ci in range(zu.shape[0] // CHUNK):
        vc = vn[ci * CHUNK:(ci + 1) * CHUNK, :]
        acc = bsp
        for g in range(4):
            acc = acc + jnp.dot(wc_bf[g], jnp.where(cmasks[g], vc, 0.0).astype(BF16), preferred_element_type=F32)
        chunks.append(acc)
    mixed = jnp.concatenate(chunks, axis=0) if len(chunks) > 1 else chunks[0]
    return u, vv, xh, rs, vn, mixed


def _conv_forward(gc, hh, prev_gc, prev_hh, first_tile, cw):
    yv = gc * hh
    prev = jnp.where(first_tile, 0.0, prev_gc * prev_hh)
    p6, p7 = _pick_row(prev, 6), _pick_row(prev, 7)
    sh1 = _shift_down(yv, 1, [p7])
    sh2 = _shift_down(yv, 2, [p6, p7])
    conv = sh2 * cw[0:1, :] + sh1 * cw[1:2, :] + yv * cw[2:3, :]
    return yv, sh1, sh2, conv


def _acc_init(step, *refs):
    @pl.when(step == 0)
    def _():
        for r in refs:
            r[...] = jnp.zeros(r.shape, r.dtype)


def _in_proj(x, p, l):
    t = x.shape[0]
    tm = min(ROW_TILE, t)

    def body(x_ref, g_ref, wa_ref, wb_ref, h_ref, za_ref, zb_ref):
        h = _rms(x_ref[...], g_ref[...]).astype(BF16)
        h_ref[...] = h
        za_ref[...] = jnp.dot(h, wa_ref[...], preferred_element_type=F32)
        zb_ref[...] = jnp.dot(h, wb_ref[...], preferred_element_type=F32)

    return _pcall(
        body, "in_proj", (t // tm,), [x, p["mix_pre_g"], p["w_in_a"], p["w_in_b"]],
        [_row(tm, D), _lyr(l, 1, D), _lyr(l, D, ZA), _lyr(l, D, ZB)],
        [_row(tm, D), _row(tm, ZA), _row(tm, ZB)],
        [_sds((t, D), BF16), _sds((t, ZA), F32), _sds((t, ZB), F32)], ("parallel",))


def _mla_prep(za, p, l, tabs):
    t = za.shape[0]
    tm = min(ROW_TILE, t)

    def body(z_ref, gq_ref, gkv_ref, wuq_ref, wukv_ref, c_ref, s1_ref, s2_ref, cq_ref, ckv_ref, q_ref, k_ref, v_ref):
        z = z_ref[...]
        cq = _rms(z[:, :QR], gq_ref[...]).astype(BF16)
        ckv = _rms(z[:, QR:QR + KVR], gkv_ref[...]).astype(BF16)
        cq_ref[...] = cq
        ckv_ref[...] = ckv
        c, s1, s2 = c_ref[...], s1_ref[...], s2_ref[...]
        kr = _rope(z[:, QR + KVR:], c, s1, s2)
        q = jnp.dot(cq, wuq_ref[...], preferred_element_type=F32)
        kv = jnp.dot(ckv, wukv_ref[...], preferred_element_type=F32)
        for h in range(HEADS):
            sl = slice(128 * h, 128 * h + 128)
            q_ref[:, sl] = (_rope(q[:, sl], c, s1, s2) * QSCALE).astype(BF16)
            k_ref[:, sl] = (kv[:, sl] + kr).astype(BF16)
        v_ref[...] = kv[:, QW:].astype(BF16)

    return _pcall(
        body, "mla_prep", (t // tm,), [za, p["q_norm_g"], p["kv_norm_g"], p["w_uq"], p["w_ukv"], *tabs],
        [_row(tm, ZA), _lyr(l, 1, QR), _lyr(l, 1, KVR), _lyr(l, QR, QW), _lyr(l, KVR, KVW),
         _row(tm, 128), _row(tm, 128), _row(tm, 128)],
        [_row(tm, QR), _row(tm, KVR), _row(tm, QW), _row(tm, QW), _row(tm, HEADS * VD)],
        [_sds((t, QR), BF16), _sds((t, KVR), BF16), _sds((t, QW), BF16), _sds((t, QW), BF16),
         _sds((t, HEADS * VD), BF16)], ("parallel",))


def _att_tile(t):
    return min(ATT_TILE, max(t // 2, 128))


def _causal_keep(tq, i, j):
    row = lax.broadcasted_iota(jnp.int32, (tq, tq), 0) + i * tq
    col = lax.broadcasted_iota(jnp.int32, (tq, tq), 1) + j * tq
    return col <= row


def _attn_fwd(qs, k, v):
    t = qs.shape[0]
    tq = _att_tile(t)
    nq = t // tq
    rep = tq // 128

    def body(q_ref, k_ref, v_ref, o_ref, lse_ref, m_s, l_s, acc_s):
        i, j = pl.program_id(1), pl.program_id(2)

        @pl.when(j == 0)
        def _():
            m_s[...] = jnp.full(m_s.shape, NEG, F32)
            l_s[...] = jnp.zeros(l_s.shape, F32)
            acc_s[...] = jnp.zeros(acc_s.shape, F32)

        def step(masked):
            vv = v_ref[...]
            keep = _causal_keep(tq, i, j) if masked else None
            for hh in range(2):
                sl = slice(128 * hh, 128 * hh + 128)
                s = lax.dot_general(q_ref[:, sl], k_ref[:, sl], NT, preferred_element_type=F32)
                if masked:
                    s = jnp.where(keep, s, NEG)
                m_old = m_s[hh]
                m_new = jnp.maximum(m_old, jnp.max(s, axis=-1, keepdims=True))
                alpha = jnp.exp2(m_old - m_new)
                p = jnp.exp2(s - jnp.tile(m_new, (1, rep)))
                l_s[hh] = alpha * l_s[hh] + jnp.sum(p, axis=-1, keepdims=True)
                acc_s[hh] = alpha * acc_s[hh] + jnp.dot(p.astype(BF16), vv, preferred_element_type=F32)
                m_s[hh] = m_new

        @pl.when(j < i)
        def _():
            step(False)

        @pl.when(j == i)
        def _():
            step(True)
            lane = lax.broadcasted_iota(jnp.int32, (tq, 128), 1)
            o_ref[...] = jnp.where(lane < VD, acc_s[0] / l_s[0], acc_s[1] / l_s[1])
            for hh in range(2):
                lse_ref[hh] = (m_s[hh] + jnp.log2(l_s[hh]))[:, 0:1]

    return pl.pallas_call(
        body, name="attn_fwd", grid=(HEADS // 2, nq, nq),
        in_specs=[pl.BlockSpec((tq, 256), lambda p, i, j: (i, p)),
                  pl.BlockSpec((tq, 256), lambda p, i, j: (jnp.minimum(j, i), p)),
                  pl.BlockSpec((tq, 128), lambda p, i, j: (jnp.minimum(j, i), p))],
        out_specs=[pl.BlockSpec((tq, 128), lambda p, i, j: (i, p)),
                   pl.BlockSpec((2, tq, 1), lambda p, i, j: (p, i, 0))],
        out_shape=[_sds((t, HEADS * VD), F32), _sds((HEADS, t, 1), F32)],
        scratch_shapes=[pltpu.VMEM((2, tq, 128), F32), pltpu.VMEM((2, tq, 128), F32), pltpu.VMEM((2, tq, 128), F32)],
        compiler_params=_cp("parallel", "parallel", "arbitrary"))(qs, k, v)


def _mixer_fwd(zb, ya, p, l):
    t = zb.shape[0]
    tm = min(ROW_TILE, t)
    hb = tm // 8

    def body(zb_ref, zprev_ref, ya_ref, gln_ref, bln_ref, wsp_ref, bsp_ref, cw_ref, go_ref, mix_ref, yb_ref, yc_ref):
        i = pl.program_id(0)
        masks = _group_masks((tm, SGW))
        cmasks = _group_masks((CHUNK, SGW))
        tril = _tril_mask()
        wc_bf = [jnp.where(tril, wsp_ref[g], 0.0).astype(BF16) for g in range(4)]
        u, _, _, _, _, mixed = _sgu_forward(zb_ref[:, 0:256], zb_ref[:, 256:512], gln_ref[...], bln_ref[...], wc_bf,
                                            bsp_ref[...], masks, cmasks)
        yb = u * mixed
        _, _, _, conv = _conv_forward(zb_ref[:, 768:1024], zb_ref[:, 1024:1280], zprev_ref[:, 768:1024],
                                      zprev_ref[:, 1024:1280], i == 0, cw_ref[...])
        yc = zb_ref[:, 512:768] * conv
        yb_ref[...] = yb
        yc_ref[...] = yc
        go = go_ref[...]
        mix_ref[:, 0:512] = _rms(ya_ref[...], go[:, 0:512]).astype(BF16)
        mix_ref[:, 512:768] = _rms(yb, go[:, 512:768]).astype(BF16)
        mix_ref[:, 768:1024] = _rms(yc, go[:, 768:1024]).astype(BF16)

    return _pcall(
        body, "mixer_fwd", (t // tm,),
        [zb, zb, ya, p["sg_ln_g"], p["sg_ln_b"], p["w_sp"], p["b_sp"], p["conv_w"], p["out_norm_g"]],
        [_row(tm, ZB), pl.BlockSpec((8, ZB), lambda i: (jnp.maximum(i * hb - 1, 0), 0)), _row(tm, 512),
         _lyr(l, 1, SGW), _lyr(l, 1, SGW), _lyr(l, 4, CHUNK, CHUNK), _lyr(l, CHUNK, SGW), _lyr(l, 3, CVW), _lyr(l, 1, D)],
        [_row(tm, D), _row(tm, SGW), _row(tm, CVW)],
        [_sds((t, D), BF16), _sds((t, SGW), F32), _sds((t, CVW), F32)], ("parallel",))


def _out_proj(mix, x, p, l):
    t = x.shape[0]
    tm = min(ROW_TILE, t)

    def body(mix_ref, w_ref, x_ref, gp_ref, gf_ref, o_ref, x2_ref, h2_ref):
        o = jnp.dot(mix_ref[...], w_ref[...], preferred_element_type=F32)
        o_ref[...] = o
        x2 = x_ref[...] + _rms(o, gp_ref[...])
        x2_ref[...] = x2
        h2_ref[...] = _rms(x2, gf_ref[...]).astype(BF16)

    return _pcall(
        body, "out_proj", (t // tm,), [mix, p["w_out"], x, p["mix_post_g"], p["ffn_pre_g"]],
        [_row(tm, D), _lyr(l, D, D), _row(tm, D), _lyr(l, 1, D), _lyr(l, 1, D)],
        [_row(tm, D), _row(tm, D), _row(tm, D)],
        [_sds((t, D), F32), _sds((t, D), F32), _sds((t, D), BF16)], ("parallel",))


def _gu_all(l, which):
    return pl.BlockSpec((4, None, None, D, HP), lambda *_: (0, l, which, 0, 0))


def _down_all(l):
    return pl.BlockSpec((4, None, HP, D), lambda *_: (0, l, 0, 0))


def _ffn_up(h2, p, l):
    t = h2.shape[0]
    tm = min(ROW_TILE, t)

    def body(h_ref, wg_ref, wu_ref, a_ref, b_ref, s_ref):
        h = h_ref[...]
        a = jnp.dot(h, wg_ref[...], preferred_element_type=F32)
        b = jnp.dot(h, wu_ref[...], preferred_element_type=F32)
        a_ref[...] = a.astype(BF16)
        b_ref[...] = b.astype(BF16)
        s_ref[...] = (a * (1.0 / (1.0 + jnp.exp(-a))) * b).astype(BF16)

    blk = pl.BlockSpec((tm, HP), lambda k, i: (i, k))
    wblk = lambda which: pl.BlockSpec((None, None, None, D, HP), lambda k, i: (k, l, which, 0, 0))
    return _pcall(
        body, "ffn_up", (4, t // tm), [h2, p["w_gu"], p["w_gu"]],
        [pl.BlockSpec((tm, D), lambda k, i: (i, 0)), wblk(0), wblk(1)], [blk, blk, blk],
        [_sds((t, DFFP), BF16)] * 3, ("parallel", "parallel"))


def _ffn_down(s, x2, p, l):
    t = x2.shape[0]
    tm = min(ROW_TILE, t)

    def body(s_ref, w_ref, x_ref, g_ref, f_ref, x3_ref):
        f = jnp.dot(s_ref[:, 0:HP], w_ref[0], preferred_element_type=F32)
        for k in range(1, 4):
            f = f + jnp.dot(s_ref[:, k * HP:(k + 1) * HP], w_ref[k], preferred_element_type=F32)
        f_ref[...] = f
        x3_ref[...] = x_ref[...] + _rms(f, g_ref[...])

    return _pcall(
        body, "ffn_down", (t // tm,), [s, p["w_down"], x2, p["ffn_post_g"]],
        [_row(tm, DFFP), _down_all(l), _row(tm, D), _lyr(l, 1, D)], [_row(tm, D), _row(tm, D)],
        [_sds((t, D), F32), _sds((t, D), F32)], ("parallel",))


def _loss_head(y, target):
    t = y.shape[0]
    tm = min(ROW_TILE, t)

    def body(y_ref, t_ref, dy_ref, acc_ref):
        e = y_ref[...] - t_ref[...]
        dy_ref[...] = e * (1.0 / D)
        sq = jnp.sum(e * e, axis=0, keepdims=True)
        part = sq[:, 0:128]
        for b in range(1, D // 128):
            part = part + sq[:, 128 * b:128 * b + 128]
        _acc_init(pl.program_id(0), acc_ref)
        acc_ref[...] += part

    return _pcall(body, "loss_head", (t // tm,), [y, target], [_row(tm, D), _row(tm, D)],
                  [_row(tm, D), pl.BlockSpec((1, 128), lambda i: (0, 0))],
                  [_sds((t, D), F32), _sds((1, 128), F32)], ("arbitrary",))


def _ffn_down_bwd(dx3, sv, p, l, depth, gb):
    t = dx3.shape[0]
    tm = min(256, t)

    def body(dx_ref, f_ref, g_ref, w_ref, a_ref, b_ref, df_ref, da_ref, db_ref, dg_ref):
        _acc_init(pl.program_id(0), dg_ref)
        df, dg = _rms_bwd(f_ref[...], g_ref[...], dx_ref[...])
        dg_ref[...] += dg
        df = df.astype(BF16)
        df_ref[...] = df
        for k in range(4):
            sl = slice(k * HP, (k + 1) * HP)
            ds = lax.dot_general(df, w_ref[k], NT, preferred_element_type=F32)
            av = a_ref[:, sl].astype(F32)
            sig = 1.0 / (1.0 + jnp.exp(-av))
            da_ref[:, sl] = (ds * b_ref[:, sl].astype(F32) * (sig * (1.0 + av * (1.0 - sig)))).astype(BF16)
            db_ref[:, sl] = (ds * (av * sig)).astype(BF16)

    df, da, db, gb["ffn_post_g"] = _pcall(
        body, "ffn_down_bwd", (t // tm,), [dx3, sv["f"], p["ffn_post_g"], p["w_down"], sv["a"], sv["b"]],
        [_row(tm, D), _row(tm, D), _lyr(l, 1, D), _down_all(l), _row(tm, DFFP), _row(tm, DFFP)],
        [_row(tm, D), _row(tm, DFFP), _row(tm, DFFP), _lyr(l, 1, D)],
        [_sds((t, D), BF16), _sds((t, DFFP), BF16), _sds((t, DFFP), BF16), _sds((depth, 1, D), F32)], ("arbitrary",),
        prevs={3: gb.get("ffn_post_g")})
    return df, da, db


def _ffn_up_bwd(da, db, dx3, sv, p, l, depth, gb):
    t = dx3.shape[0]
    tm = min(256, t)

    def body(da_ref, db_ref, wg_ref, wu_ref, x_ref, dx3_ref, g_ref, dx2_ref, dg_ref):
        _acc_init(pl.program_id(0), dg_ref)
        dh = jnp.zeros((tm, D), F32)
        for k in range(4):
            sl = slice(k * HP, (k + 1) * HP)
            dh = dh + lax.dot_general(da_ref[:, sl], wg_ref[k], NT, preferred_element_type=F32)
            dh = dh + lax.dot_general(db_ref[:, sl], wu_ref[k], NT, preferred_element_type=F32)
        dx, dg = _rms_bwd(x_ref[...], g_ref[...], dh)
        dg_ref[...] += dg
        dx2_ref[...] = dx3_ref[...] + dx

    dx2, gb["ffn_pre_g"] = _pcall(
        body, "ffn_up_bwd", (t // tm,), [da, db, p["w_gu"], p["w_gu"], sv["x2"], dx3, p["ffn_pre_g"]],
        [_row(tm, DFFP), _row(tm, DFFP), _gu_all(l, 0), _gu_all(l, 1), _row(tm, D), _row(tm, D), _lyr(l, 1, D)],
        [_row(tm, D), _lyr(l, 1, D)], [_sds((t, D), F32), _sds((depth, 1, D), F32)], ("arbitrary",),
        prevs={1: gb.get("ffn_pre_g")})
    return dx2


def _out_proj_bwd(dx2, sv, p, l, depth, gb):
    t = dx2.shape[0]
    tm = min(ROW_TILE, t)

    def body(dx_ref, o_ref, g_ref, w_ref, do_ref, dmix_ref, dg_ref):
        _acc_init(pl.program_id(0), dg_ref)
        do, dg = _rms_bwd(o_ref[...], g_ref[...], dx_ref[...])
        dg_ref[...] += dg
        do = do.astype(BF16)
        do_ref[...] = do
        dmix_ref[...] = lax.dot_general(do, w_ref[...], NT, preferred_element_type=F32)

    do, dmix, gb["mix_post_g"] = _pcall(
        body, "out_proj_bwd", (t // tm,), [dx2, sv["o"], p["mix_post_g"], p["w_out"]],
        [_row(tm, D), _row(tm, D), _lyr(l, 1, D), _lyr(l, D, D)], [_row(tm, D), _row(tm, D), _lyr(l, 1, D)],
        [_sds((t, D), BF16), _sds((t, D), F32), _sds((depth, 1, D), F32)], ("arbitrary",),
        prevs={2: gb.get("mix_post_g")})
    return do, dmix


def _mixer_bwd(dmix, sv, p, l, depth, gb):
    zb = sv["zb"]
    t = zb.shape[0]
    tm = min(ROW_TILE, t)
    hb = tm // 8
    last_blk = t // 8 - 1
    nsteps = t // tm

    def body(dmix_ref, ya_ref, yb_ref, yc_ref, zb_ref, zprev_ref, znext_ref, ycn_ref, dmn_ref,
             gln_ref, bln_ref, wsp_ref, bsp_ref, cw_ref, go_ref,
             dya_ref, dzb_ref, delta_ref, dgo_ref, dgln_ref, dbln_ref, dwsp_ref, dbsp_ref, dcw_ref):
        i = pl.program_id(0)
        _acc_init(i, dgo_ref, dgln_ref, dbln_ref, dwsp_ref, dbsp_ref, dcw_ref)
        go = go_ref[...]
        dmix = dmix_ref[...]

        ya = ya_ref[...]
        dya, dga = _rms_bwd(ya, go[:, 0:512], dmix[:, 0:512])
        dyb, dgb_ = _rms_bwd(yb_ref[...], go[:, 512:768], dmix[:, 512:768])
        dyc, dgc_ = _rms_bwd(yc_ref[...], go[:, 768:1024], dmix[:, 768:1024])
        dgo_ref[:, 0:512] += dga
        dgo_ref[:, 512:768] += dgb_
        dgo_ref[:, 768:1024] += dgc_
        dya = dya * LN2
        dya_ref[...] = dya.astype(BF16)
        prod = dya * ya
        hmasks = _group_masks((tm, 512))
        for h in range(HEADS):
            delta_ref[h] = jnp.sum(jnp.where(hmasks[h], prod, 0.0), axis=-1, keepdims=True)

        masks = _group_masks((tm, SGW))
        cmasks = _group_masks((CHUNK, SGW))
        tril = _tril_mask()
        wc_bf = [jnp.where(tril, wsp_ref[g], 0.0).astype(BF16) for g in range(4)]
        zu, zv = zb_ref[:, 0:256], zb_ref[:, 256:512]
        g_ln = gln_ref[...]
        u, _, xh, rs, vn, mixed = _sgu_forward(zu, zv, g_ln, bln_ref[...], wc_bf, bsp_ref[...], masks, cmasks)
        du = dyb * mixed
        dmixed = dyb * u
        dvn_chunks = []
        dbsp = jnp.zeros((CHUNK, SGW), F32)
        for ci in range(tm // CHUNK):
            rows = slice(ci * CHUNK, (ci + 1) * CHUNK)
            dm_c = dmixed[rows, :]
            vn_c = vn[rows, :].astype(BF16)
            dbsp = dbsp + dm_c
            dvn_c = jnp.zeros((CHUNK, SGW), F32)
            for g in range(4):
                dm_g = jnp.where(cmasks[g], dm_c, 0.0).astype(BF16)
                dw = lax.dot_general(dm_g, vn_c, NT, preferred_element_type=F32)
                dwsp_ref[g] += jnp.where(tril, dw, 0.0)
                dvn_c = dvn_c + lax.dot_general(wc_bf[g], dm_g, TN, preferred_element_type=F32)
            dvn_chunks.append(dvn_c)
        dbsp_ref[...] += dbsp
        dvn = jnp.concatenate(dvn_chunks, axis=0) if len(dvn_chunks) > 1 else dvn_chunks[0]
        dgln_ref[...] += jnp.sum(dvn * xh, axis=0, keepdims=True)
        dbln_ref[...] += jnp.sum(dvn, axis=0, keepdims=True)
        dxh = dvn * g_ln
        dvv = rs * (dxh - _group_mean(dxh, masks) - xh * _group_mean(dxh * xh, masks))
        dzb_ref[:, 0:256] = (du * _gelu_grad(zu)).astype(BF16)
        dzb_ref[:, 256:512] = (dvv * _gelu_grad(zv)).astype(BF16)

        cwv = cw_ref[...]
        gb_, gc, hh = zb_ref[:, 512:768], zb_ref[:, 768:1024], zb_ref[:, 1024:1280]
        yv, sh1, sh2, conv = _conv_forward(gc, hh, zprev_ref[:, 768:1024], zprev_ref[:, 1024:1280], i == 0, cwv)
        dconv = dyc * gb_
        dzb_ref[:, 512:768] = (dyc * conv).astype(BF16)
        dcw_ref[0:1, :] += jnp.sum(dconv * sh2, axis=0, keepdims=True)
        dcw_ref[1:2, :] += jnp.sum(dconv * sh1, axis=0, keepdims=True)
        dcw_ref[2:3, :] += jnp.sum(dconv * yv, axis=0, keepdims=True)
        dycn, _ = _rms_bwd(ycn_ref[...], go[:, 768:1024], dmn_ref[...])
        dconv_next = jnp.where(i == nsteps - 1, 0.0, dycn * znext_ref[:, 512:768])
        n0, n1 = _pick_row(dconv_next, 0), _pick_row(dconv_next, 1)
        dyv = dconv * cwv[2:3, :] + _shift_up(dconv, 1, [n0]) * cwv[1:2, :] + _shift_up(dconv, 2, [n0, n1]) * cwv[0:1, :]
        dzb_ref[:, 768:1024] = (dyv * hh).astype(BF16)
        dzb_ref[:, 1024:1280] = (dyv * gc).astype(BF16)

    prev_map = lambda i: (jnp.maximum(i * hb - 1, 0), 0)
    next_map = lambda i: (jnp.minimum((i + 1) * hb, last_blk), 0)
    names = ("out_norm_g", "sg_ln_g", "sg_ln_b", "w_sp", "b_sp_t", "conv_w")
    shapes = ((1, D), (1, SGW), (1, SGW), (4, CHUNK, CHUNK), (CHUNK, SGW), (3, CVW))
    outs = _pcall(
        body, "mixer_bwd", (nsteps,),
        [dmix, sv["ya"], sv["yb"], sv["yc"], zb, zb, zb, sv["yc"], dmix, p["sg_ln_g"], p["sg_ln_b"], p["w_sp"], p["b_sp"],
         p["conv_w"], p["out_norm_g"]],
        [_row(tm, D), _row(tm, 512), _row(tm, SGW), _row(tm, CVW), _row(tm, ZB),
         pl.BlockSpec((8, ZB), prev_map), pl.BlockSpec((8, ZB), next_map), pl.BlockSpec((8, CVW), next_map),
         pl.BlockSpec((8, 256), lambda i: (jnp.minimum((i + 1) * hb, last_blk), 3)),
         _lyr(l, 1, SGW), _lyr(l, 1, SGW), _lyr(l, 4, CHUNK, CHUNK), _lyr(l, CHUNK, SGW), _lyr(l, 3, CVW), _lyr(l, 1, D)],
        [_row(tm, 512), _row(tm, ZB), pl.BlockSpec((HEADS, tm, 1), lambda i: (0, i, 0))] + [_lyr(l, *s) for s in shapes],
        [_sds((t, 512), BF16), _sds((t, ZB), BF16), _sds((HEADS, t, 1), F32)] + [_sds((depth,) + s, F32) for s in shapes],
        ("arbitrary",), prevs={3 + n: gb.get(name) for n, name in enumerate(names)})
    for n, name in enumerate(names):
        gb[name] = outs[3 + n]
    return outs[0], outs[1], outs[2]


def _attn_bwd(qs, k, v, dya, lse, delta):
    t = qs.shape[0]
    tq = _att_tile(t)
    nq = t // tq

    def body(q_ref, k_ref, v_ref, do_ref, lse_ref, dl_ref, dq_ref, dk_ref, dv_ref, dq_s, dk_s, dv_s):
        j, i = pl.program_id(1), pl.program_id(2)

        @pl.when((j == 0) & (i == 0))
        def _():
            dq_s[...] = jnp.zeros(dq_s.shape, F32)

        def step(masked):
            keep = _causal_keep(tq, 0, 0) if masked else None
            lane = lax.broadcasted_iota(jnp.int32, (tq, 128), 1)
            vv = v_ref[...]
            do = do_ref[...]
            rows = pl.ds(pl.multiple_of(i * tq, tq), tq)
            for hh in range(2):
                sl = slice(128 * hh, 128 * hh + 128)
                qq, kk = q_ref[:, sl], k_ref[:, sl]
                s = lax.dot_general(qq, kk, NT, preferred_element_type=F32)
                p = jnp.exp2(s - lse_ref[hh])
                if masked:
                    p = jnp.where(keep, p, 0.0)
                do_h = jnp.where((lane < VD) if hh == 0 else (lane >= VD), do, jnp.zeros_like(do))
                dp = lax.dot_general(do_h, vv, NT, preferred_element_type=F32)
                ds = (p * (dp - dl_ref[hh])).astype(BF16)
                dv_s[...] += lax.dot_general(p.astype(BF16), do_h, TN, preferred_element_type=F32)
                dk_s[:, sl] += lax.dot_general(ds, qq, TN, preferred_element_type=F32)
                dq_s[rows, sl] += jnp.dot(ds, kk, preferred_element_type=F32)

        @pl.when(i == j)
        def _():
            dk_s[...] = jnp.zeros(dk_s.shape, F32)
            dv_s[...] = jnp.zeros(dv_s.shape, F32)
            step(True)

        @pl.when(i > j)
        def _():
            step(False)

        @pl.when(i == nq - 1)
        def _():
            dk_ref[...] = dk_s[...].astype(BF16)
            dv_ref[...] = (dv_s[...] * LOG2E).astype(BF16)

        @pl.when((i == nq - 1) & (j == nq - 1))
        def _():
            dq_ref[...] = dq_s[...].astype(BF16)

    qrow = lambda p, j, i: (jnp.maximum(i, j), p)
    col_spec = pl.BlockSpec((2, tq, 1), lambda p, j, i: (p, jnp.maximum(i, j), 0))
    return pl.pallas_call(
        body, name="attn_bwd", grid=(HEADS // 2, nq, nq),
        in_specs=[pl.BlockSpec((tq, 256), qrow), pl.BlockSpec((tq, 256), lambda p, j, i: (j, p)),
                  pl.BlockSpec((tq, 128), lambda p, j, i: (j, p)), pl.BlockSpec((tq, 128), qrow), col_spec, col_spec],
        out_specs=[pl.BlockSpec((t, 256), lambda p, j, i: (0, p)), pl.BlockSpec((tq, 256), lambda p, j, i: (j, p)),
                   pl.BlockSpec((tq, 128), lambda p, j, i: (j, p))],
        out_shape=[_sds((t, QW), BF16), _sds((t, QW), BF16), _sds((t, HEADS * VD), BF16)],
        scratch_shapes=[pltpu.VMEM((t, 256), F32), pltpu.VMEM((tq, 256), F32), pltpu.VMEM((tq, 128), F32)],
        compiler_params=_cp("arbitrary", "arbitrary", "arbitrary"))(qs, k, v, dya, lse, delta)


def _mla_prep_bwd(dqs, dk, dv, sv, p, l, depth, gb, tabs):
    za = sv["za"]
    t = za.shape[0]
    tm = min(ROW_TILE, t)

    def body(dq_ref, dk_ref, dv_ref, z_ref, gq_ref, gkv_ref, wuq_ref, wukv_ref, c_ref, s1_ref, s2_ref,
             dza_ref, dqp_ref, dkv_ref, dgq_ref, dgkv_ref):
        _acc_init(pl.program_id(0), dgq_ref, dgkv_ref)
        c, s1, s2 = c_ref[...], s1_ref[...], s2_ref[...]
        lane = lax.broadcasted_iota(jnp.int32, (tm, 128), 1)
        rope_lanes = (lane >= NOPE) & (lane < NOPE + ROPE)
        dkr = jnp.zeros((tm, 128), F32)
        for h in range(HEADS):
            sl = slice(128 * h, 128 * h + 128)
            dqp_ref[:, sl] = _rope_bwd(dq_ref[:, sl].astype(F32) * QSCALE, c, s1, s2).astype(BF16)
            dkh = dk_ref[:, sl]
            dkv_ref[:, sl] = dkh
            dkr = dkr + jnp.where(rope_lanes, dkh.astype(F32), 0.0)
        dkv_ref[:, QW:] = dv_ref[...]
        z = z_ref[...]
        dcq = lax.dot_general(dqp_ref[...], wuq_ref[...], NT, preferred_element_type=F32)
        dzq, dgq = _rms_bwd(z[:, :QR], gq_ref[...], dcq)
        dckv = lax.dot_general(dkv_ref[...], wukv_ref[...], NT, preferred_element_type=F32)
        dzkv, dgkv = _rms_bwd(z[:, QR:QR + KVR], gkv_ref[...], dckv)
        dgq_ref[...] += dgq
        dgkv_ref[...] += dgkv
        dza_ref[:, :QR] = dzq.astype(BF16)
        dza_ref[:, QR:QR + KVR] = dzkv.astype(BF16)
        dza_ref[:, QR + KVR:] = _rope_bwd(dkr, c, s1, s2).astype(BF16)

    dza, dqp, dkv, gb["q_norm_g"], gb["kv_norm_g"] = _pcall(
        body, "mla_prep_bwd", (t // tm,),
        [dqs, dk, dv, za, p["q_norm_g"], p["kv_norm_g"], p["w_uq"], p["w_ukv"], *tabs],
        [_row(tm, QW), _row(tm, QW), _row(tm, HEADS * VD), _row(tm, ZA), _lyr(l, 1, QR), _lyr(l, 1, KVR),
         _lyr(l, QR, QW), _lyr(l, KVR, KVW), _row(tm, 128), _row(tm, 128), _row(tm, 128)],
        [_row(tm, ZA), _row(tm, QW), _row(tm, KVW), _lyr(l, 1, QR), _lyr(l, 1, KVR)],
        [_sds((t, ZA), BF16), _sds((t, QW), BF16), _sds((t, KVW), BF16), _sds((depth, 1, QR), F32),
         _sds((depth, 1, KVR), F32)], ("arbitrary",), prevs={3: gb.get("q_norm_g"), 4: gb.get("kv_norm_g")})
    return dza, dqp, dkv


def _in_proj_bwd(dza, dzb, dx2, sv, p, l, depth, gb):
    t = dx2.shape[0]
    tm = min(ROW_TILE, t)

    def body(dza_ref, dzb_ref, wa_ref, wb_ref, x_ref, dx2_ref, g_ref, dx_ref, dg_ref):
        _acc_init(pl.program_id(0), dg_ref)
        dh = (lax.dot_general(dza_ref[...], wa_ref[...], NT, preferred_element_type=F32)
              + lax.dot_general(dzb_ref[...], wb_ref[...], NT, preferred_element_type=F32))
        dx, dg = _rms_bwd(x_ref[...], g_ref[...], dh)
        dg_ref[...] += dg
        dx_ref[...] = dx2_ref[...] + dx

    dx, gb["mix_pre_g"] = _pcall(
        body, "in_proj_bwd", (t // tm,), [dza, dzb, p["w_in_a"], p["w_in_b"], sv["x"], dx2, p["mix_pre_g"]],
        [_row(tm, ZA), _row(tm, ZB), _lyr(l, D, ZA), _lyr(l, D, ZB), _row(tm, D), _row(tm, D), _lyr(l, 1, D)],
        [_row(tm, D), _lyr(l, 1, D)], [_sds((t, D), F32), _sds((depth, 1, D), F32)], ("arbitrary",),
        prevs={1: gb.get("mix_pre_g")})
    return dx


def _mm_tn(a, b, tn, name, l, depth, gb):
    t, k = a.shape
    n = b.shape[1]
    tt = min(ROW_TILE, t)

    def body(a_ref, b_ref, o_ref):
        _acc_init(pl.program_id(1), o_ref)
        o_ref[...] += lax.dot_general(a_ref[...], b_ref[...], TN, preferred_element_type=F32)

    gb[name] = _pcall(
        body, "d" + name, (n // tn, t // tt), [a, b],
        [pl.BlockSpec((tt, k), lambda j, s: (s, 0)), pl.BlockSpec((tt, tn), lambda j, s: (s, j))],
        pl.BlockSpec((None, k, tn), lambda j, s: (l, 0, j)), _sds((depth, k, n), F32), ("parallel", "arbitrary"),
        prevs={0: gb.get(name)})


def _dw_ffn(a, b, kind, l, depth, gb):
    t = a.shape[0]
    tt = min(ROW_TILE, t)
    nsteps = t // tt
    half = depth // 2

    down = kind == "down"

    def body(a_ref, b_ref, o_ref, acc):
        s = pl.program_id(0)
        _acc_init(s, acc)
        acc[...] += lax.dot_general(a_ref[...], b_ref[...], TN, preferred_element_type=F32)

        @pl.when(s == nsteps - 1)
        def _():
            for k in range(4):
                blk = acc[k * HP:(k + 1) * HP, :] if down else acc[:, k * HP:(k + 1) * HP]
                o_ref[k] = blk.astype(BF16)

    rows = lambda n: pl.BlockSpec((tt, n), lambda s: (s, 0))
    if down:
        in_specs, acc_shape, name = [rows(DFFP), rows(D)], (DFFP, D), "down"
        out_spec = pl.BlockSpec((4, None, None, HP, D), lambda s: (0, l // half, l % half, 0, 0))
        out_shape = _sds((4, 2, half, HP, D), BF16)
    else:
        which = 0 if kind == "gate" else 1
        in_specs, acc_shape, name = [rows(D), rows(DFFP)], (D, DFFP), "gu"
        out_spec = pl.BlockSpec((4, None, None, None, D, HP), lambda s: (0, l // half, l % half, which, 0, 0))
        out_shape = _sds((4, 2, half, 2, D, HP), BF16)
    gb[name] = _pcall(body, "dw_" + kind, (nsteps,), [a, b], in_specs, out_spec, out_shape, ("arbitrary",),
                      scratch=[pltpu.VMEM(acc_shape, F32)], prevs={0: gb.get(name)})


def _layer_fwd(x, p, l, tabs):
    h1, za, zb = _in_proj(x, p, l)
    cqn, ckvn, qs, k, v = _mla_prep(za, p, l, tabs)
    ya, lse = _attn_fwd(qs, k, v)
    mix, yb, yc = _mixer_fwd(zb, ya, p, l)
    o, x2, h2 = _out_proj(mix, x, p, l)
    a, b, s = _ffn_up(h2, p, l)
    f, x3 = _ffn_down(s, x2, p, l)
    saved = dict(x=x, h1=h1, za=za, zb=zb, cqn=cqn, ckvn=ckvn, qs=qs, k=k, v=v, ya=ya, lse=lse, mix=mix, yb=yb, yc=yc,
                 o=o, x2=x2, h2=h2, a=a, b=b, s=s, f=f)
    return x3, saved


def _layer_bwd(dx3, p, sv, l, depth, gb, tabs):
    t = dx3.shape[0]
    df, da, db = _ffn_down_bwd(dx3, sv, p, l, depth, gb)
    _dw_ffn(sv["s"], df, "down", l, depth, gb)
    dx2 = _ffn_up_bwd(da, db, dx3, sv, p, l, depth, gb)
    _dw_ffn(sv["h2"], da, "gate", l, depth, gb)
    _dw_ffn(sv["h2"], db, "up", l, depth, gb)
    do, dmix = _out_proj_bwd(dx2, sv, p, l, depth, gb)
    _mm_tn(sv["mix"], do, D, "w_out", l, depth, gb)
    dya, dzb, delta = _mixer_bwd(dmix, sv, p, l, depth, gb)
    dqs, dk, dv = _attn_bwd(sv["qs"], sv["k"], sv["v"], dya, sv["lse"], delta)
    dza, dqp, dkv = _mla_prep_bwd(dqs, dk, dv, sv, p, l, depth, gb, tabs)
    _mm_tn(sv["cqn"], dqp, QW, "w_uq", l, depth, gb)
    _mm_tn(sv["ckvn"], dkv, KVW, "w_ukv", l, depth, gb)
    _mm_tn(sv["h1"], dza, ZA, "w_in_a", l, depth, gb)
    _mm_tn(sv["h1"], dzb, ZB, "w_in_b", l, depth, gb)
    return _in_proj_bwd(dza, dzb, dx2, sv, p, l, depth, gb)


def _rope_tables(positions):
    inv_freq = 1.0 / (ROPE_THETA ** (jnp.arange(0, ROPE // 2, dtype=F32) / (ROPE // 2)))
    ang = positions.astype(F32)[:, None] * inv_freq
    cos, sin = jnp.cos(ang), jnp.sin(ang)
    t = positions.shape[0]
    one, zero = jnp.ones((t, 64), F32), jnp.zeros((t, 16), F32)
    c = jnp.concatenate([one, cos, cos, one[:, :32]], axis=1)
    s1 = jnp.concatenate([zero, zero, zero, zero, -sin, zero, zero, zero], axis=1)
    s2 = jnp.concatenate([zero, zero, zero, zero, zero, sin, zero, zero], axis=1)
    return c, s1, s2


def _kernel_params(full):
    w_in = full["w_in"]
    depth = w_in.shape[0]
    zpad = lambda n: jnp.zeros((depth, D, n), w_in.dtype)
    kv = full["w_ukv"].reshape(depth, KVR, HEADS, NOPE + VD)
    p = {
        "w_in_a": jnp.concatenate([w_in[:, :, :640], zpad(64), w_in[:, :, 640:672], zpad(32)], axis=2),
        "w_in_b": w_in[:, :, 672:],
        "w_uq": jnp.pad(full["w_uq"].reshape(depth, QR, HEADS, NOPE + ROPE),
                        ((0, 0), (0, 0), (0, 0), (0, 32))).reshape(depth, QR, QW),
        "w_ukv": jnp.concatenate([jnp.pad(kv[..., :NOPE], ((0, 0), (0, 0), (0, 0), (0, 64))).reshape(depth, KVR, QW),
                                  kv[..., NOPE:].reshape(depth, KVR, HEADS * VD)], axis=2),
        "w_out": full["w_out"], "w_sp": full["w_sp"], "conv_w": full["conv_w"],
        "b_sp": jnp.repeat(jnp.swapaxes(full["b_sp"], 1, 2), 64, axis=2),
    }
    for n in ("mix_pre_g", "mix_post_g", "ffn_pre_g", "ffn_post_g", "q_norm_g", "kv_norm_g", "sg_ln_g", "sg_ln_b",
              "out_norm_g"):
        p[n] = full[n][:, None, :]
    return p


def _natural_grads(gb):
    depth = gb["w_in_a"].shape[0]
    ga, kv = gb["w_in_a"], gb["w_ukv"]
    out = {
        "w_in": jnp.concatenate([ga[:, :, :640], ga[:, :, 704:736], gb["w_in_b"]], axis=2),
        "w_uq": gb["w_uq"].reshape(depth, QR, HEADS, 128)[..., :NOPE + ROPE].reshape(depth, QR, HEADS * (NOPE + ROPE)),
        "w_ukv": jnp.concatenate([kv[:, :, :QW].reshape(depth, KVR, HEADS, 128)[..., :NOPE],
                                  kv[:, :, QW:].reshape(depth, KVR, HEADS, VD)], axis=3).reshape(depth, KVR, -1),
        "b_sp": jnp.swapaxes(gb["b_sp_t"].reshape(depth, CHUNK, 4, 64).sum(axis=-1), 1, 2),
    }
    for n in ("w_out", "w_sp", "conv_w"):
        out[n] = gb[n]
    for n in ("mix_pre_g", "mix_post_g", "ffn_pre_g", "ffn_post_g", "q_norm_g", "kv_norm_g", "sg_ln_g", "sg_ln_b",
              "out_norm_g"):
        out[n] = gb[n][:, 0, :]
    return out


def _local_step(x, positions, target, full, w_gu, w_down):
    depth = full["w_in"].shape[0]
    tabs = _rope_tables(positions)
    p = _kernel_params(full)
    p["w_gu"], p["w_down"] = w_gu, w_down
    saved = []
    for l in range(depth):
        x, sv = _layer_fwd(x, p, l, tabs)
        saved.append(sv)
    dx, acc = _loss_head(x, target)
    loss = (0.5 / D) * jnp.sum(acc)
    gb = {}
    for l in reversed(range(depth)):
        dx = _layer_bwd(dx, p, saved[l], l, depth, gb, tabs)
    return loss, dx, _natural_grads(gb), gb["gu"], gb["down"]


def _place():
    x, y, c = lax.axis_index("x"), lax.axis_index("y"), lax.axis_index("c")
    chips = [(1 - x, y), (x, 1 - y), (1 - x, 1 - y)]
    return x, y, c, 2 * x + y, chips


def _remote(src, dst, send_sem, recv_sem, to):
    return pltpu.make_async_remote_copy(src_ref=src, dst_ref=dst, send_sem=send_sem, recv_sem=recv_sem, device_id=to,
                                        device_id_type=MESH_ID)


def _gather_shards(mine):
    nb = len(mine)

    def body(*refs):
        mine_refs, out_refs = refs[:nb], refs[nb:2 * nb]
        send_sems, recv_sems, fsend_sems, frecv_sems = refs[2 * nb:]
        x, y, c, k, chips = _place()
        sib = (x, y, 1 - c)
        sends, passed = [], []
        for b in range(nb):
            for n, (cx, cy) in enumerate(chips):
                sends.append(_remote(mine_refs[b].at[c], out_refs[b].at[k, c], send_sems.at[b, n], recv_sems.at[b, n],
                                     (cx, cy, c)))
        for cp in sends:
            cp.start()
        for n, (cx, cy) in enumerate(chips):
            kj = 2 * cx + cy
            for b in range(nb):
                _remote(mine_refs[b].at[c], out_refs[b].at[kj, c], send_sems.at[b, n], recv_sems.at[b, n],
                        (cx, cy, c)).wait_recv()
                fwd = _remote(out_refs[b].at[kj, c], out_refs[b].at[kj, c], fsend_sems.at[b, n], frecv_sems.at[b, n], sib)
                fwd.start()
                passed.append(fwd)
        for n, (cx, cy) in enumerate(chips):
            kj = 2 * cx + cy
            for b in range(nb):
                _remote(mine_refs[b].at[1 - c], out_refs[b].at[kj, 1 - c], fsend_sems.at[b, n], frecv_sems.at[b, n],
                        sib).wait_recv()
        for cp in sends + passed:
            cp.wait_send()

    return pl.pallas_call(
        body, name="gather_shards", in_specs=[ANY] * nb, out_specs=[ANY] * nb,
        out_shape=[_sds((4,) + a.shape, a.dtype) for a in mine],
        scratch_shapes=[pltpu.SemaphoreType.DMA((nb, 3))] * 4)(*mine)


def _swap_halves(bigs, small):
    nb = len(bigs)

    def body(*refs):
        big_refs, small_ref = refs[:nb], refs[nb]
        rbig_refs, rsmall_ref = refs[nb + 1:2 * nb + 1], refs[2 * nb + 1]
        send_sems, recv_sems = refs[2 * nb + 2:]
        x, y, c, _, _ = _place()
        sib = (x, y, 1 - c)
        cps = [_remote(big_refs[b].at[:, 1 - c], rbig_refs[b], send_sems.at[b], recv_sems.at[b], sib) for b in range(nb)]
        cps.append(_remote(small_ref, rsmall_ref, send_sems.at[nb], recv_sems.at[nb], sib))
        for cp in cps:
            cp.start()
        for cp in cps:
            cp.wait()

    return pl.pallas_call(
        body, name="swap_halves", in_specs=[ANY] * (nb + 1), out_specs=[ANY] * (nb + 1),
        out_shape=[_sds((4,) + a.shape[2:], a.dtype) for a in bigs] + [_sds(small.shape, small.dtype)],
        scratch_shapes=[pltpu.SemaphoreType.DMA((nb + 1,))] * 2)(*bigs, small)


def _sum_tile(r):
    return max(cand for cand in range(16, 641, 16) if r % cand == 0)


def _pair_sum(big, rbig, c):
    _, _, r, w = big.shape
    tr = _sum_tile(r)

    def body(c_ref, big_ref, rbig_ref, p_ref):
        p_ref[...] = (big_ref[...].astype(F32) + rbig_ref[...].astype(F32)).astype(BF16)

    grid_spec = pltpu.PrefetchScalarGridSpec(
        num_scalar_prefetch=1, grid=(4, r // tr),
        in_specs=[pl.BlockSpec((None, None, tr, w), lambda j, i, cr: (j, cr[0], i, 0)),
                  pl.BlockSpec((None, tr, w), lambda j, i, cr: (j, i, 0))],
        out_specs=pl.BlockSpec((None, tr, w), lambda j, i, cr: (j, i, 0)))
    return pl.pallas_call(body, name="pair_sum", grid_spec=grid_spec, out_shape=_sds((4, r, w), BF16),
                          compiler_params=_cp("parallel", "parallel"))(c, big, rbig)


def _small_sum(parts):
    n, ns, _ = parts.shape

    def body(p_ref, o_ref):
        s = p_ref[0]
        for j in range(1, n):
            s = s + p_ref[j]
        o_ref[...] = s

    return pl.pallas_call(body, name="small_sum", out_shape=_sds((ns, 128), F32))(parts)


def _chip_exchange(ps, small):
    nb = len(ps)
    ns = small.shape[0]

    def body(*refs):
        p_refs, small_ref = refs[:nb], refs[nb]
        rb_refs, rs_ref = refs[nb + 1:2 * nb + 1], refs[2 * nb + 1]
        send_sems, recv_sems, local_sem = refs[2 * nb + 2:]
        x, y, c, k, chips = _place()
        loc = pltpu.make_async_copy(small_ref, rs_ref.at[k], local_sem)
        loc.start()
        sends = []
        for n, (cx, cy) in enumerate(chips):
            to = (cx, cy, c)
            for b in range(nb):
                sends.append(_remote(p_refs[b].at[2 * cx + cy], rb_refs[b].at[n], send_sems.at[b, n], recv_sems.at[b, n], to))
            sends.append(_remote(small_ref, rs_ref.at[k], send_sems.at[nb, n], recv_sems.at[nb, n], to))
        for cp in sends:
            cp.start()
        for n, (cx, cy) in enumerate(chips):
            to = (cx, cy, c)
            for b in range(nb):
                _remote(p_refs[b].at[k], rb_refs[b].at[n], send_sems.at[b, n], recv_sems.at[b, n], to).wait_recv()
            _remote(small_ref, rs_ref.at[2 * cx + cy], send_sems.at[nb, n], recv_sems.at[nb, n], to).wait_recv()
        for cp in sends:
            cp.wait_send()
        loc.wait()

    return pl.pallas_call(
        body, name="chip_exchange", in_specs=[ANY] * (nb + 1), out_specs=[ANY] * (nb + 1),
        out_shape=[_sds((3,) + a.shape[1:], a.dtype) for a in ps] + [_sds((4, ns, 128), small.dtype)],
        scratch_shapes=[pltpu.SemaphoreType.DMA((nb + 1, 3))] * 2 + [pltpu.SemaphoreType.DMA(())])(*ps, small)


def _chip_sum(p, rb, chip):
    _, r, w = p.shape
    tr = _sum_tile(r)

    def body(k_ref, p_ref, rb_ref, o_ref):
        acc = p_ref[...].astype(F32)
        for j in range(3):
            acc = acc + rb_ref[j].astype(F32)
        o_ref[...] = acc

    grid_spec = pltpu.PrefetchScalarGridSpec(
        num_scalar_prefetch=1, grid=(r // tr,),
        in_specs=[pl.BlockSpec((None, tr, w), lambda i, kr: (kr[0], i, 0)), pl.BlockSpec((3, tr, w), lambda i, kr: (0, i, 0))],
        out_specs=pl.BlockSpec((tr, w), lambda i, kr: (i, 0)))
    return pl.pallas_call(body, name="chip_sum", grid_spec=grid_spec, out_shape=_sds((r, w), F32),
                          compiler_params=_cp("parallel"))(chip, p, rb)


def _send_to_sibling(reds):
    nb = len(reds)

    def body(*refs):
        red_refs, out_refs = refs[:nb], refs[nb:2 * nb]
        send_sems, recv_sems = refs[2 * nb:]
        x, y, c, _, _ = _place()
        cps = [_remote(red_refs[b], out_refs[b], send_sems.at[b], recv_sems.at[b], (x, y, 1 - c)) for b in range(nb)]
        for cp in cps:
            cp.start()
        for cp in cps:
            cp.wait()

    return pl.pallas_call(
        body, name="send_to_sibling", in_specs=[ANY] * nb, out_specs=[ANY] * nb,
        out_shape=[_sds(a.shape, a.dtype) for a in reds], scratch_shapes=[pltpu.SemaphoreType.DMA((nb,))] * 2)(*reds)


def _adam_math(w, g, m, v):
    nm = ADAM_B1 * m + (1.0 - ADAM_B1) * g
    nv = ADAM_B2 * v + (1.0 - ADAM_B2) * (g * g)
    m_hat = nm / (1.0 - ADAM_B1 ** ADAM_STEP)
    v_hat = nv / (1.0 - ADAM_B2 ** ADAM_STEP)
    return -ADAM_LR * (m_hat / (jnp.sqrt(v_hat) + ADAM_EPS) + ADAM_WD * w), nm, nv


def _adamw_shard(w, m, v, own, other, c, name, pick=None):
    depth, r, n = w.shape
    half = depth // 2
    tr = max(cand for cand in range(8, min(r, 256) + 1, 8) if r % cand == 0)
    npad = own.shape[-1]

    def body(c_ref, w_ref, m_ref, v_ref, own_ref, oth_ref, g_ref, d_ref, nm_ref, nv_ref):
        mine = (pl.program_id(0) // half) == c_ref[0]
        g = jnp.where(mine, own_ref[...], oth_ref[...])[:, :n]
        g_ref[...] = g
        d_ref[...], nm_ref[...], nv_ref[...] = _adam_math(w_ref[...], g, m_ref[...], v_ref[...])

    blk = pl.BlockSpec((None, tr, n), lambda l, i, cr: (l, i, 0))
    if pick is None:
        gblk = pl.BlockSpec((None, tr, npad), lambda l, i, cr: (l % half, i, 0))
    else:
        gblk = pl.BlockSpec((None, None, tr, npad), lambda l, i, cr: (l % half, pick, i, 0))
    grid_spec = pltpu.PrefetchScalarGridSpec(num_scalar_prefetch=1, grid=(depth, r // tr), in_specs=[blk] * 3 + [gblk] * 2,
                                             out_specs=[blk] * 4)
    return pl.pallas_call(body, name=name, grid_spec=grid_spec, out_shape=[_sds(w.shape, F32)] * 4,
                          compiler_params=_cp("parallel", "parallel"))(c, w, m, v, own, other)


def _pad_ffn_shards(w_gate, w_up, w_down):
    depth = w_gate.shape[0]
    half = depth // 2
    tm = 256

    def gu_body(g_ref, u_ref, o_ref):
        o_ref[...] = jnp.zeros(o_ref.shape, BF16)
        o_ref[0, :, 0:HS] = g_ref[...].astype(BF16)
        o_ref[1, :, 0:HS] = u_ref[...].astype(BF16)

    blk = pl.BlockSpec((None, tm, HS), lambda l, i: (l, i, 0))
    gu = pl.pallas_call(
        gu_body, name="pad_gate_up", grid=(depth, D // tm), in_specs=[blk, blk],
        out_specs=pl.BlockSpec((None, None, 2, tm, HP), lambda l, i: (l // half, l % half, 0, i, 0)),
        out_shape=_sds((2, half, 2, D, HP), BF16), compiler_params=_cp("parallel", "parallel"))(w_gate, w_up)

    def down_body(w_ref, o_ref):
        o_ref[0:HS, :] = w_ref[...].astype(BF16)
        o_ref[HS:HP, :] = jnp.zeros((HP - HS, D), BF16)

    down = pl.pallas_call(
        down_body, name="pad_down", grid=(depth,), in_specs=[pl.BlockSpec((None, HS, D), lambda l: (l, 0, 0))],
        out_specs=pl.BlockSpec((None, None, HP, D), lambda l: (l // half, l % half, 0, 0)),
        out_shape=_sds((2, half, HP, D), BF16), compiler_params=_cp("parallel"))(w_down)
    return gu, down


def _adamw_small(w, g, m, v):
    r = w.shape[0]
    tr = max(cand for cand in range(8, 513, 8) if r % cand == 0)

    def body(w_ref, g_ref, m_ref, v_ref, d_ref, nm_ref, nv_ref):
        d_ref[...], nm_ref[...], nv_ref[...] = _adam_math(w_ref[...], g_ref[...], m_ref[...], v_ref[...])

    blk = pl.BlockSpec((tr, 128), lambda i: (i, 0))
    return pl.pallas_call(body, name="adamw_small", grid=(r // tr,), in_specs=[blk] * 4, out_specs=[blk] * 3,
                          out_shape=[_sds(w.shape, F32)] * 3, compiler_params=_cp("parallel"))(w, g, m, v)


def _to_pack(a, name):
    depth = a.shape[0]
    if name in ROW_SHARDED:
        return jnp.swapaxes(a.reshape(depth, 4, -1, D), 0, 1)
    return jnp.transpose(a.reshape(depth, a.shape[1], 4, a.shape[2] // 4), (2, 0, 1, 3)).reshape(4, depth, -1, D)


def _pack_rows(parts, lead, dtype, tail=None):
    pieces, at = [], 0
    for n, off, rows in PACK:
        if off > at:
            pieces.append(jnp.zeros(lead + (off - at, D), dtype))
        pieces.append(parts[n].astype(dtype))
        at = off + rows
    if tail is not None:
        pieces.append(tail)
        at += tail.shape[-2]
    pieces.append(jnp.zeros(lead + (PACK_ROWS - at, D), dtype))
    return jnp.concatenate(pieces, axis=len(lead))


def _pack_weight_shards(sh):
    depth = sh["w_in"].shape[0]
    parts = {n: sh[n].reshape(depth, rows, D) for n, _, rows in PACK}
    conv = lax.bitcast_convert_type(sh["conv_w"].reshape(depth, 3 * 64), BF16).reshape(depth, 1, 384)
    flat = _pack_rows(parts, (depth,), BF16, tail=jnp.pad(conv, ((0, 0), (0, 0), (0, D - 384))))
    return flat.reshape(2, depth // 2 * PACK_ROWS, D)


def _unpack_weights(gathered, depth, shard_shapes):
    flat = gathered.reshape(4, depth, PACK_ROWS, D)
    full = {}
    for n, off, rows in PACK:
        shp = shard_shapes[n][1:]
        piece = flat[:, :, off:off + rows, :].reshape((4, depth) + shp)
        if n in ROW_SHARDED:
            full[n] = jnp.transpose(piece, (1, 0, 2, 3)).reshape(depth, 4 * shp[0], shp[1])
        else:
            full[n] = jnp.transpose(piece, (1, 2, 0, 3)).reshape(depth, shp[0], 4 * shp[1])
    conv = lax.bitcast_convert_type(flat[:, :, CONV_ROW, :384].reshape(4, depth, 192, 2), F32)
    full["conv_w"] = jnp.transpose(conv.reshape(4, depth, 3, 64), (1, 2, 0, 3)).reshape(depth, 3, CVW)
    return full


def _pack_grad_shards(g):
    depth = g["w_in"].shape[0]
    flat = _pack_rows({n: _to_pack(g[n], n) for n, _, _ in PACK}, (4, depth), BF16)
    return flat.reshape(4, 2, depth // 2 * PACK_ROWS, D)


def _pack_small(arrs, names_shapes, depth):
    flat = jnp.concatenate([arrs[n].reshape(depth, -1) for n, _ in names_shapes], axis=1).reshape(-1)
    rows = -(-flat.shape[0] // 1024) * 8
    return jnp.pad(flat, (0, rows * 128 - flat.shape[0])).reshape(rows, 128)


def _unpack_small(packed, names_shapes, depth):
    per_layer = sum(math.prod(s) for _, s in names_shapes)
    flat = packed.reshape(-1)[:depth * per_layer].reshape(depth, per_layer)
    out, off = {}, 0
    for n, s in names_shapes:
        size = math.prod(s)
        out[n] = flat[:, off:off + size].reshape((depth,) + s)
        off += size
    return out


def kernel(x, positions, mix_pre_g, mix_post_g, ffn_pre_g, ffn_post_g, w_in, q_norm_g, w_uq, kv_norm_g, w_ukv, sg_ln_g, sg_ln_b, w_sp, b_sp, conv_w, out_norm_g, w_out, w_gate, w_up, w_down, loss_target, m_mix_pre_g, m_mix_post_g, m_ffn_pre_g, m_ffn_post_g, m_w_in, m_q_norm_g, m_w_uq, m_kv_norm_g, m_w_ukv, m_sg_ln_g, m_sg_ln_b, m_w_sp, m_b_sp, m_conv_w, m_out_norm_g, m_w_out, m_w_gate, m_w_up, m_w_down, v_mix_pre_g, v_mix_post_g, v_ffn_pre_g, v_ffn_post_g, v_w_in, v_q_norm_g, v_w_uq, v_kv_norm_g, v_w_ukv, v_sg_ln_g, v_sg_ln_b, v_w_sp, v_b_sp, v_conv_w, v_out_norm_g, v_w_out, v_w_gate, v_w_up, v_w_down):
    w = dict(mix_pre_g=mix_pre_g, mix_post_g=mix_post_g, ffn_pre_g=ffn_pre_g, ffn_post_g=ffn_post_g, w_in=w_in,
             q_norm_g=q_norm_g, w_uq=w_uq, kv_norm_g=kv_norm_g, w_ukv=w_ukv, sg_ln_g=sg_ln_g, sg_ln_b=sg_ln_b, w_sp=w_sp,
             b_sp=b_sp, conv_w=conv_w, out_norm_g=out_norm_g, w_out=w_out, w_gate=w_gate, w_up=w_up, w_down=w_down)
    m = dict(mix_pre_g=m_mix_pre_g, mix_post_g=m_mix_post_g, ffn_pre_g=m_ffn_pre_g, ffn_post_g=m_ffn_post_g, w_in=m_w_in,
             q_norm_g=m_q_norm_g, w_uq=m_w_uq, kv_norm_g=m_kv_norm_g, w_ukv=m_w_ukv, sg_ln_g=m_sg_ln_g, sg_ln_b=m_sg_ln_b,
             w_sp=m_w_sp, b_sp=m_b_sp, conv_w=m_conv_w, out_norm_g=m_out_norm_g, w_out=m_w_out, w_gate=m_w_gate,
             w_up=m_w_up, w_down=m_w_down)
    v = dict(mix_pre_g=v_mix_pre_g, mix_post_g=v_mix_post_g, ffn_pre_g=v_ffn_pre_g, ffn_post_g=v_ffn_post_g, w_in=v_w_in,
             q_norm_g=v_q_norm_g, w_uq=v_w_uq, kv_norm_g=v_kv_norm_g, w_ukv=v_w_ukv, sg_ln_g=v_sg_ln_g, sg_ln_b=v_sg_ln_b,
             w_sp=v_w_sp, b_sp=v_b_sp, conv_w=v_conv_w, out_norm_g=v_out_norm_g, w_out=v_w_out, w_gate=v_w_gate,
             w_up=v_w_up, w_down=v_w_down)
    depth = w_in.shape[0]
    c = lax.axis_index("c").astype(jnp.int32).reshape(1)
    chip = (2 * lax.axis_index("x") + lax.axis_index("y")).astype(jnp.int32)

    half = depth // 2

    mine = [_pack_weight_shards(w), *_pad_ffn_shards(w_gate, w_up, w_down)]
    gathered = [lax.dynamic_update_slice(g, a[None], (chip,) + (0,) * a.ndim) for g, a in zip(_gather_shards(mine), mine)]
    full = _unpack_weights(gathered[0], depth, {n: w[n].shape for n, _, _ in PACK})
    for n, _ in SMALL:
        if n != "conv_w":
            full[n] = w[n]

    loss, dx, grads, g_gu, g_down = _local_step(x[0], positions[0], loss_target[0], full,
                                                gathered[1].reshape(4, depth, 2, D, HP), gathered[2].reshape(4, depth, HP, D))
    loss = lax.psum(loss, ("x", "y", "c"))

    small = _pack_small(grads, SMALL, depth)
    bigs = [_pack_grad_shards(grads), g_gu.reshape(4, 2, half * 2 * D, HP), g_down.reshape(4, 2, half * HP, D)]
    *rbigs, rsmall = _swap_halves(bigs, small)
    ps = [_pair_sum(a, r, c) for a, r in zip(bigs, rbigs)]
    *rbs, rs = _chip_exchange(ps, _small_sum(jnp.stack([small, rsmall])))
    own = [_chip_sum(p, rb, chip.reshape(1)) for p, rb in zip(ps, rbs)]
    other = _send_to_sibling(own)
    g_small = _unpack_small(_small_sum(rs), SMALL, depth)
    g_small["conv_w"] = lax.dynamic_slice_in_dim(g_small["conv_w"], chip * 64, 64, axis=2)

    gw, delta, new_m, new_v = dict(g_small), {}, {}, {}

    def adam(n, own_g, other_g, pick=None):
        shp = w[n].shape
        outs = _adamw_shard(w[n], m[n], v[n], own_g, other_g, c, "adamw_" + n, pick)
        gw[n], delta[n], new_m[n], new_v[n] = outs

    for n, off, rows in PACK:
        nat = lambda a: a.reshape(half, PACK_ROWS, D)[:, off:off + rows, :].reshape((half,) + w[n].shape[1:])
        adam(n, nat(own[0]), nat(other[0]))
    adam("w_gate", own[1].reshape(half, 2, D, HP), other[1].reshape(half, 2, D, HP), 0)
    adam("w_up", own[1].reshape(half, 2, D, HP), other[1].reshape(half, 2, D, HP), 1)
    adam("w_down", own[2].reshape(half, HP, D), other[2].reshape(half, HP, D))
    small_local = tuple((n, w[n].shape[1:]) for n, _ in SMALL)
    d_, m_, v_ = _adamw_small(_pack_small(w, small_local, depth), _pack_small(gw, small_local, depth),
                              _pack_small(m, small_local, depth), _pack_small(v, small_local, depth))
    delta.update(_unpack_small(d_, small_local, depth))
    new_m.update(_unpack_small(m_, small_local, depth))
    new_v.update(_unpack_small(v_, small_local, depth))

    return (loss, dx[None], *[gw[n] for n in WEIGHTS], *[delta[n] for n in WEIGHTS], *[new_m[n] for n in WEIGHTS],
            *[new_v[n] for n in WEIGHTS])
```

```python
import math

import jax
import jax.numpy as jnp
from jax import lax
from jax.experimental import pallas as pl
from jax.experimental.pallas import tpu as pltpu

F32 = jnp.float32
BF16 = jnp.bfloat16

D = 1024
HEADS = 8
NOPE = 64
ROPE = 32
VD = 64
QR = 384
KVR = 256
SGW = 256
CVW = 256
CHUNK = 128
DFF = 2816
EPS = 1e-6
ROPE_THETA = 10000.0
LOG2E = 1.4426950408889634
LN2 = 0.6931471805599453
QSCALE = (NOPE + ROPE) ** -0.5 * LOG2E
ZA = 768
ZB = 1280
QW = HEADS * 128
KVW = HEADS * 128 + HEADS * VD
NEG = -1e30
GC0 = 0.7978845608028654
GC1 = 0.044715

ADAM_LR = 0.001
ADAM_B1 = 0.9
ADAM_B2 = 0.999
ADAM_EPS = 1e-08
ADAM_WD = 0.01
ADAM_STEP = 10

V7X_VMEM_LIMIT = 52 * 1024 * 1024
ROW_TILE = 512
ATT_TILE = 512

NT = (((1,), (1,)), ((), ()))
TN = (((0,), (0,)), ((), ()))

HS = DFF // 4
HP = 768
DFFP = 4 * HP

PACK = (("w_in", 0, 488), ("w_out", 512, 256), ("w_ukv", 768, 64), ("w_uq", 832, 72))
CONV_ROW = 904
PACK_ROWS = 928
ROW_SHARDED = ("w_out",)
SMALL = (("mix_pre_g", (D,)), ("mix_post_g", (D,)), ("ffn_pre_g", (D,)), ("ffn_post_g", (D,)), ("q_norm_g", (QR,)),
         ("kv_norm_g", (KVR,)), ("sg_ln_g", (SGW,)), ("sg_ln_b", (SGW,)), ("w_sp", (4, CHUNK, CHUNK)), ("b_sp", (4, CHUNK)),
         ("conv_w", (3, CVW)), ("out_norm_g", (D,)))
WEIGHTS = ["mix_pre_g", "mix_post_g", "ffn_pre_g", "ffn_post_g", "w_in", "q_norm_g", "w_uq", "kv_norm_g", "w_ukv", "sg_ln_g",
           "sg_ln_b", "w_sp", "b_sp", "conv_w", "out_norm_g", "w_out", "w_gate", "w_up", "w_down"]

MESH_ID = pl.DeviceIdType.MESH
ANY = pl.BlockSpec(memory_space=pl.ANY)


def _cp(*sem):
    return pltpu.CompilerParams(dimension_semantics=sem, vmem_limit_bytes=V7X_VMEM_LIMIT)


def _sds(shape, dtype):
    return jax.ShapeDtypeStruct(shape, dtype)


def _row(tm, n):
    return pl.BlockSpec((tm, n), lambda i: (i, 0))


def _lyr(l, *shape):
    return pl.BlockSpec((None,) + shape, lambda *_: (l,) + (0,) * len(shape))


def _wl(a, l):
    return 0 if a.shape[0] == 1 else l


def _pcall(body, name, grid, ins, in_specs, out_specs, out_shape, sem, scratch=(), prevs=None):
    prevs = {k: v for k, v in (prevs or {}).items() if v is not None}
    order = sorted(prevs)
    n_in = len(ins)

    def wrapped(*refs):
        return body(*refs[:n_in], *refs[n_in + len(order):])

    return pl.pallas_call(
        wrapped, name=name, grid=grid, in_specs=list(in_specs) + [ANY] * len(order), out_specs=out_specs,
        out_shape=out_shape, scratch_shapes=list(scratch),
        input_output_aliases={n_in + i: k for i, k in enumerate(order)},
        compiler_params=_cp(*sem))(*ins, *[prevs[k] for k in order])


def _rms(x, g):
    r = lax.rsqrt(jnp.mean(x * x, axis=-1, keepdims=True) + EPS)
    return x * r * g


def _rms_bwd(x, g, dy):
    r = lax.rsqrt(jnp.mean(x * x, axis=-1, keepdims=True) + EPS)
    xh = x * r
    dg = jnp.sum(dy * xh, axis=0, keepdims=True)
    dxh = dy * g
    dx = r * (dxh - xh * jnp.mean(dxh * xh, axis=-1, keepdims=True))
    return dx, dg


def _gelu(x):
    return 0.5 * x * (1.0 + jnp.tanh(GC0 * (x + GC1 * x * x * x)))


def _gelu_grad(x):
    t = jnp.tanh(GC0 * (x + GC1 * x * x * x))
    return 0.5 * (1.0 + t) + 0.5 * x * (1.0 - t * t) * GC0 * (1.0 + 3.0 * GC1 * x * x)


def _rope(xb, c, s1, s2):
    return xb * c + pltpu.roll(xb, 112, 1) * s1 + pltpu.roll(xb, 16, 1) * s2


def _rope_bwd(dy, c, s1, s2):
    return dy * c + pltpu.roll(dy * s1, 16, 1) + pltpu.roll(dy * s2, 112, 1)


def _group_masks(shape):
    lane = lax.broadcasted_iota(jnp.int32, shape, 1)
    return [(lane >= 64 * g) & (lane < 64 * g + 64) for g in range(shape[1] // 64)]


def _group_mean(v, masks):
    out = jnp.zeros_like(v)
    for m in masks:
        s = jnp.sum(jnp.where(m, v, 0.0), axis=-1, keepdims=True) * (1.0 / 64.0)
        out = jnp.where(m, s, out)
    return out


def _pick_row(blk, idx):
    row = lax.broadcasted_iota(jnp.int32, blk.shape, 0)
    return jnp.sum(jnp.where(row == idx, blk, 0.0), axis=0, keepdims=True)


def _shift_down(y, k, first_rows):
    out = pltpu.roll(y, k, 0)
    row = lax.broadcasted_iota(jnp.int32, y.shape, 0)
    for idx in range(k):
        out = jnp.where(row == idx, first_rows[idx], out)
    return out


def _shift_up(y, k, last_rows):
    n = y.shape[0]
    out = pltpu.roll(y, n - k, 0)
    row = lax.broadcasted_iota(jnp.int32, y.shape, 0)
    for idx in range(k):
        out = jnp.where(row == n - k + idx, last_rows[idx], out)
    return out


def _tril_mask():
    r = lax.broadcasted_iota(jnp.int32, (CHUNK, CHUNK), 0)
    c = lax.broadcasted_iota(jnp.int32, (CHUNK, CHUNK), 1)
    return r >= c


def _sgu_forward(zu, zv, g_ln, b_ln, wc_bf, bsp, masks, cmasks):
    u = _gelu(zu)
    vv = _gelu(zv)
    mu = _group_mean(vv, masks)
    dv = vv - mu
    rs = lax.rsqrt(_group_mean(dv * dv, masks) + EPS)
    xh = dv * rs
    vn = xh * g_ln + b_ln
    chunks = []
    for ci in range(zu.shape[0] // CHUNK):
        vc = vn[ci * CHUNK:(ci + 1) * CHUNK, :]
        acc = bsp
        for g in range(4):
            acc = acc + jnp.dot(wc_bf[g], jnp.where(cmasks[g], vc, 0.0).astype(BF16), preferred_element_type=F32)
        chunks.append(acc)
    mixed = jnp.concatenate(chunks, axis=0) if len(chunks) > 1 else chunks[0]
    return u, vv, xh, rs, vn, mixed


def _conv_forward(gc, hh, prev_gc, prev_hh, first_tile, cw):
    yv = gc * hh
    prev = jnp.where(first_tile, 0.0, prev_gc * prev_hh)
    p6, p7 = _pick_row(prev, 6), _pick_row(prev, 7)
    sh1 = _shift_down(yv, 1, [p7])
    sh2 = _shift_down(yv, 2, [p6, p7])
    conv = sh2 * cw[0:1, :] + sh1 * cw[1:2, :] + yv * cw[2:3, :]
    return yv, sh1, sh2, conv


def _acc_init(step, *refs):
    @pl.when(step == 0)
    def _():
        for r in refs:
            r[...] = jnp.zeros(r.shape, r.dtype)


def _in_proj(x, p, l):
    t = x.shape[0]
    tm = min(ROW_TILE, t)

    def body(x_ref, g_ref, wa_ref, wb_ref, h_ref, za_ref, zb_ref):
        h = _rms(x_ref[...], g_ref[...]).astype(BF16)
        h_ref[...] = h
        za_ref[...] = jnp.dot(h, wa_ref[...], preferred_element_type=F32)
        zb_ref[...] = jnp.dot(h, wb_ref[...], preferred_element_type=F32)

    return _pcall(
        body, "in_proj", (t // tm,), [x, p["mix_pre_g"], p["w_in_a"], p["w_in_b"]],
        [_row(tm, D), _lyr(l, 1, D), _lyr(_wl(p["w_in_a"], l), D, ZA), _lyr(_wl(p["w_in_b"], l), D, ZB)],
        [_row(tm, D), _row(tm, ZA), _row(tm, ZB)],
        [_sds((t, D), BF16), _sds((t, ZA), F32), _sds((t, ZB), F32)], ("parallel",))


def _mla_prep(za, p, l, tabs):
    t = za.shape[0]
    tm = min(ROW_TILE, t)

    def body(z_ref, gq_ref, gkv_ref, wuq_ref, wukv_ref, c_ref, s1_ref, s2_ref, cq_ref, ckv_ref, q_ref, k_ref, v_ref):
        z = z_ref[...]
        cq = _rms(z[:, :QR], gq_ref[...]).astype(BF16)
        ckv = _rms(z[:, QR:QR + KVR], gkv_ref[...]).astype(BF16)
        cq_ref[...] = cq
        ckv_ref[...] = ckv
        c, s1, s2 = c_ref[...], s1_ref[...], s2_ref[...]
        kr = _rope(z[:, QR + KVR:], c, s1, s2)
        q = jnp.dot(cq, wuq_ref[...], preferred_element_type=F32)
        kv = jnp.dot(ckv, wukv_ref[...], preferred_element_type=F32)
        for h in range(HEADS):
            sl = slice(128 * h, 128 * h + 128)
            q_ref[:, sl] = (_rope(q[:, sl], c, s1, s2) * QSCALE).astype(BF16)
            k_ref[:, sl] = (kv[:, sl] + kr).astype(BF16)
        v_ref[...] = kv[:, QW:].astype(BF16)

    return _pcall(
        body, "mla_prep", (t // tm,), [za, p["q_norm_g"], p["kv_norm_g"], p["w_uq"], p["w_ukv"], *tabs],
        [_row(tm, ZA), _lyr(l, 1, QR), _lyr(l, 1, KVR), _lyr(_wl(p["w_uq"], l), QR, QW), _lyr(_wl(p["w_ukv"], l), KVR, KVW),
         _row(tm, 128), _row(tm, 128), _row(tm, 128)],
        [_row(tm, QR), _row(tm, KVR), _row(tm, QW), _row(tm, QW), _row(tm, HEADS * VD)],
        [_sds((t, QR), BF16), _sds((t, KVR), BF16), _sds((t, QW), BF16), _sds((t, QW), BF16),
         _sds((t, HEADS * VD), BF16)], ("parallel",))


def _att_tile(t):
    return min(ATT_TILE, max(t // 2, 128))


def _causal_keep(tq, i, j):
    row = lax.broadcasted_iota(jnp.int32, (tq, tq), 0) + i * tq
    col = lax.broadcasted_iota(jnp.int32, (tq, tq), 1) + j * tq
    return col <= row


def _attn_fwd(qs, k, v, fetch=None):
    t = qs.shape[0]
    tq = _att_tile(t)
    nq = t // tq
    rep = tq // 128
    mine, bufs, fetch_layer = fetch if fetch else ((), (), None)
    nb = len(mine)

    def body(q_ref, k_ref, v_ref, *refs):
        o_ref, lse_ref = refs[2 * nb:2 * nb + 2]
        m_s, l_s, acc_s = refs[3 * nb + 2:3 * nb + 5]
        i, j = pl.program_id(1), pl.program_id(2)
        if fetch:
            start, hand_over, drain = _gather_ops(refs[:nb], refs[2 * nb + 2:3 * nb + 2], refs[3 * nb + 5:], fetch_layer)
            pr = pl.program_id(0)
            pl.when((pr == 0) & (i == 0) & (j == 0))(start)
            pl.when((pr == HEADS // 2 - 1) & (i == 0) & (j == 0))(hand_over)
            pl.when((pr == HEADS // 2 - 1) & (i == nq - 1) & (j == nq - 1))(drain)

        @pl.when(j == 0)
        def _():
            m_s[...] = jnp.full(m_s.shape, NEG, F32)
            l_s[...] = jnp.zeros(l_s.shape, F32)
            acc_s[...] = jnp.zeros(acc_s.shape, F32)

        def step(masked):
            vv = v_ref[...]
            keep = _causal_keep(tq, i, j) if masked else None
            for hh in range(2):
                sl = slice(128 * hh, 128 * hh + 128)
                s = lax.dot_general(q_ref[:, sl], k_ref[:, sl], NT, preferred_element_type=F32)
                if masked:
                    s = jnp.where(keep, s, NEG)
                m_old = m_s[hh]
                m_new = jnp.maximum(m_old, jnp.max(s, axis=-1, keepdims=True))
                alpha = jnp.exp2(m_old - m_new)
                p = jnp.exp2(s - jnp.tile(m_new, (1, rep)))
                l_s[hh] = alpha * l_s[hh] + jnp.sum(p, axis=-1, keepdims=True)
                acc_s[hh] = alpha * acc_s[hh] + jnp.dot(p.astype(BF16), vv, preferred_element_type=F32)
                m_s[hh] = m_new

        @pl.when(j < i)
        def _():
            step(False)

        @pl.when(j == i)
        def _():
            step(True)
            lane = lax.broadcasted_iota(jnp.int32, (tq, 128), 1)
            o_ref[...] = jnp.where(lane < VD, acc_s[0] / l_s[0], acc_s[1] / l_s[1])
            for hh in range(2):
                lse_ref[hh] = (m_s[hh] + jnp.log2(l_s[hh]))[:, 0:1]

    outs = pl.pallas_call(
        body, name="attn_fwd_fetch" if fetch else "attn_fwd", grid=(HEADS // 2, nq, nq),
        in_specs=[pl.BlockSpec((tq, 256), lambda p, i, j: (i, p)),
                  pl.BlockSpec((tq, 256), lambda p, i, j: (jnp.minimum(j, i), p)),
                  pl.BlockSpec((tq, 128), lambda p, i, j: (jnp.minimum(j, i), p))] + [ANY] * (2 * nb),
        out_specs=[pl.BlockSpec((tq, 128), lambda p, i, j: (i, p)),
                   pl.BlockSpec((2, tq, 1), lambda p, i, j: (p, i, 0))] + [ANY] * nb,
        out_shape=[_sds((t, HEADS * VD), F32), _sds((HEADS, t, 1), F32)] + [_sds(b.shape, b.dtype) for b in bufs],
        scratch_shapes=[pltpu.VMEM((2, tq, 128), F32), pltpu.VMEM((2, tq, 128), F32), pltpu.VMEM((2, tq, 128), F32)]
        + ([pltpu.SemaphoreType.DMA((nb, 3))] * 4 if fetch else []),
        input_output_aliases={3 + nb + b: 2 + b for b in range(nb)},
        compiler_params=_cp("arbitrary", "arbitrary", "arbitrary"))(qs, k, v, *mine, *bufs)
    return outs[0], outs[1], list(outs[2:])


def _mixer_fwd(zb, ya, p, l):
    t = zb.shape[0]
    tm = min(ROW_TILE, t)
    hb = tm // 8

    def body(zb_ref, zprev_ref, ya_ref, gln_ref, bln_ref, wsp_ref, bsp_ref, cw_ref, go_ref, mix_ref, yb_ref, yc_ref):
        i = pl.program_id(0)
        masks = _group_masks((tm, SGW))
        cmasks = _group_masks((CHUNK, SGW))
        tril = _tril_mask()
        wc_bf = [jnp.where(tril, wsp_ref[g], 0.0).astype(BF16) for g in range(4)]
        u, _, _, _, _, mixed = _sgu_forward(zb_ref[:, 0:256], zb_ref[:, 256:512], gln_ref[...], bln_ref[...], wc_bf,
                                            bsp_ref[...], masks, cmasks)
        yb = u * mixed
        _, _, _, conv = _conv_forward(zb_ref[:, 768:1024], zb_ref[:, 1024:1280], zprev_ref[:, 768:1024],
                                      zprev_ref[:, 1024:1280], i == 0, cw_ref[...])
        yc = zb_ref[:, 512:768] * conv
        yb_ref[...] = yb
        yc_ref[...] = yc
        go = go_ref[...]
        mix_ref[:, 0:512] = _rms(ya_ref[...], go[:, 0:512]).astype(BF16)
        mix_ref[:, 512:768] = _rms(yb, go[:, 512:768]).astype(BF16)
        mix_ref[:, 768:1024] = _rms(yc, go[:, 768:1024]).astype(BF16)

    return _pcall(
        body, "mixer_fwd", (t // tm,),
        [zb, zb, ya, p["sg_ln_g"], p["sg_ln_b"], p["w_sp"], p["b_sp"], p["conv_w"], p["out_norm_g"]],
        [_row(tm, ZB), pl.BlockSpec((8, ZB), lambda i: (jnp.maximum(i * hb - 1, 0), 0)), _row(tm, 512),
         _lyr(l, 1, SGW), _lyr(l, 1, SGW), _lyr(l, 4, CHUNK, CHUNK), _lyr(l, CHUNK, SGW), _lyr(_wl(p["conv_w"], l), 3, CVW), _lyr(l, 1, D)],
        [_row(tm, D), _row(tm, SGW), _row(tm, CVW)],
        [_sds((t, D), BF16), _sds((t, SGW), F32), _sds((t, CVW), F32)], ("parallel",))


def _out_proj(mix, x, p, l):
    t = x.shape[0]
    tm = min(ROW_TILE, t)

    def body(mix_ref, w_ref, x_ref, gp_ref, gf_ref, o_ref, x2_ref, h2_ref):
        o = jnp.dot(mix_ref[...], w_ref[...], preferred_element_type=F32)
        o_ref[...] = o
        x2 = x_ref[...] + _rms(o, gp_ref[...])
        x2_ref[...] = x2
        h2_ref[...] = _rms(x2, gf_ref[...]).astype(BF16)

    return _pcall(
        body, "out_proj", (t // tm,), [mix, p["w_out"], x, p["mix_post_g"], p["ffn_pre_g"]],
        [_row(tm, D), _lyr(_wl(p["w_out"], l), D, D), _row(tm, D), _lyr(l, 1, D), _lyr(l, 1, D)],
        [_row(tm, D), _row(tm, D), _row(tm, D)],
        [_sds((t, D), F32), _sds((t, D), F32), _sds((t, D), BF16)], ("parallel",))


def _gu_all(l, which):
    return pl.BlockSpec((4, None, None, D, HP), lambda *_: (0, l, which, 0, 0))


def _down_all(l):
    return pl.BlockSpec((4, None, HP, D), lambda *_: (0, l, 0, 0))


def _ffn_up(h2, p, l):
    t = h2.shape[0]
    tm = min(ROW_TILE, t)

    def body(h_ref, wg_ref, wu_ref, a_ref, b_ref, s_ref):
        h = h_ref[...]
        a = jnp.dot(h, wg_ref[...], preferred_element_type=F32)
        b = jnp.dot(h, wu_ref[...], preferred_element_type=F32)
        a_ref[...] = a.astype(BF16)
        b_ref[...] = b.astype(BF16)
        s_ref[...] = (a * (1.0 / (1.0 + jnp.exp(-a))) * b).astype(BF16)

    blk = pl.BlockSpec((tm, HP), lambda k, i: (i, k))
    wblk = lambda which: pl.BlockSpec((None, None, None, D, HP), lambda k, i: (k, l, which, 0, 0))
    return _pcall(
        body, "ffn_up", (4, t // tm), [h2, p["w_gu"], p["w_gu"]],
        [pl.BlockSpec((tm, D), lambda k, i: (i, 0)), wblk(0), wblk(1)], [blk, blk, blk],
        [_sds((t, DFFP), BF16)] * 3, ("parallel", "parallel"))


def _ffn_down(s, x2, p, l):
    t = x2.shape[0]
    tm = min(ROW_TILE, t)

    def body(s_ref, w_ref, x_ref, g_ref, f_ref, x3_ref):
        f = jnp.dot(s_ref[:, 0:HP], w_ref[0], preferred_element_type=F32)
        for k in range(1, 4):
            f = f + jnp.dot(s_ref[:, k * HP:(k + 1) * HP], w_ref[k], preferred_element_type=F32)
        f_ref[...] = f
        x3_ref[...] = x_ref[...] + _rms(f, g_ref[...])

    return _pcall(
        body, "ffn_down", (t // tm,), [s, p["w_down"], x2, p["ffn_post_g"]],
        [_row(tm, DFFP), _down_all(l), _row(tm, D), _lyr(l, 1, D)], [_row(tm, D), _row(tm, D)],
        [_sds((t, D), F32), _sds((t, D), F32)], ("parallel",))


def _loss_head(y, target):
    t = y.shape[0]
    tm = min(ROW_TILE, t)

    def body(y_ref, t_ref, dy_ref, acc_ref):
        e = y_ref[...] - t_ref[...]
        dy_ref[...] = e * (1.0 / D)
        sq = jnp.sum(e * e, axis=0, keepdims=True)
        part = sq[:, 0:128]
        for b in range(1, D // 128):
            part = part + sq[:, 128 * b:128 * b + 128]
        _acc_init(pl.program_id(0), acc_ref)
        acc_ref[...] += part

    return _pcall(body, "loss_head", (t // tm,), [y, target], [_row(tm, D), _row(tm, D)],
                  [_row(tm, D), pl.BlockSpec((1, 128), lambda i: (0, 0))],
                  [_sds((t, D), F32), _sds((1, 128), F32)], ("arbitrary",))


def _ffn_down_bwd(dx3, sv, p, l, depth, gb):
    t = dx3.shape[0]
    tm = min(256, t)

    def body(dx_ref, f_ref, g_ref, w_ref, a_ref, b_ref, df_ref, da_ref, db_ref, dg_ref):
        _acc_init(pl.program_id(0), dg_ref)
        df, dg = _rms_bwd(f_ref[...], g_ref[...], dx_ref[...])
        dg_ref[...] += dg
        df = df.astype(BF16)
        df_ref[...] = df
        for k in range(4):
            sl = slice(k * HP, (k + 1) * HP)
            ds = lax.dot_general(df, w_ref[k], NT, preferred_element_type=F32)
            av = a_ref[:, sl].astype(F32)
            sig = 1.0 / (1.0 + jnp.exp(-av))
            da_ref[:, sl] = (ds * b_ref[:, sl].astype(F32) * (sig * (1.0 + av * (1.0 - sig)))).astype(BF16)
            db_ref[:, sl] = (ds * (av * sig)).astype(BF16)

    df, da, db, gb["ffn_post_g"] = _pcall(
        body, "ffn_down_bwd", (t // tm,), [dx3, sv["f"], p["ffn_post_g"], p["w_down"], sv["a"], sv["b"]],
        [_row(tm, D), _row(tm, D), _lyr(l, 1, D), _down_all(l), _row(tm, DFFP), _row(tm, DFFP)],
        [_row(tm, D), _row(tm, DFFP), _row(tm, DFFP), _lyr(l, 1, D)],
        [_sds((t, D), BF16), _sds((t, DFFP), BF16), _sds((t, DFFP), BF16), _sds((depth, 1, D), F32)], ("arbitrary",),
        prevs={3: gb.get("ffn_post_g")})
    return df, da, db


def _ffn_up_bwd(da, db, dx3, sv, p, l, depth, gb):
    t = dx3.shape[0]
    tm = min(256, t)

    def body(da_ref, db_ref, wg_ref, wu_ref, x_ref, dx3_ref, g_ref, dx2_ref, dg_ref):
        _acc_init(pl.program_id(0), dg_ref)
        dh = jnp.zeros((tm, D), F32)
        for k in range(4):
            sl = slice(k * HP, (k + 1) * HP)
            dh = dh + lax.dot_general(da_ref[:, sl], wg_ref[k], NT, preferred_element_type=F32)
            dh = dh + lax.dot_general(db_ref[:, sl], wu_ref[k], NT, preferred_element_type=F32)
        dx, dg = _rms_bwd(x_ref[...], g_ref[...], dh)
        dg_ref[...] += dg
        dx2_ref[...] = dx3_ref[...] + dx

    dx2, gb["ffn_pre_g"] = _pcall(
        body, "ffn_up_bwd", (t // tm,), [da, db, p["w_gu"], p["w_gu"], sv["x2"], dx3, p["ffn_pre_g"]],
        [_row(tm, DFFP), _row(tm, DFFP), _gu_all(l, 0), _gu_all(l, 1), _row(tm, D), _row(tm, D), _lyr(l, 1, D)],
        [_row(tm, D), _lyr(l, 1, D)], [_sds((t, D), F32), _sds((depth, 1, D), F32)], ("arbitrary",),
        prevs={1: gb.get("ffn_pre_g")})
    return dx2


def _out_proj_bwd(dx2, sv, p, l, depth, gb):
    t = dx2.shape[0]
    tm = min(ROW_TILE, t)

    def body(dx_ref, o_ref, g_ref, w_ref, do_ref, dmix_ref, dg_ref):
        _acc_init(pl.program_id(0), dg_ref)
        do, dg = _rms_bwd(o_ref[...], g_ref[...], dx_ref[...])
        dg_ref[...] += dg
        do = do.astype(BF16)
        do_ref[...] = do
        dmix_ref[...] = lax.dot_general(do, w_ref[...], NT, preferred_element_type=F32)

    do, dmix, gb["mix_post_g"] = _pcall(
        body, "out_proj_bwd", (t // tm,), [dx2, sv["o"], p["mix_post_g"], p["w_out"]],
        [_row(tm, D), _row(tm, D), _lyr(l, 1, D), _lyr(_wl(p["w_out"], l), D, D)], [_row(tm, D), _row(tm, D), _lyr(l, 1, D)],
        [_sds((t, D), BF16), _sds((t, D), F32), _sds((depth, 1, D), F32)], ("arbitrary",),
        prevs={2: gb.get("mix_post_g")})
    return do, dmix


def _mixer_bwd(dmix, sv, p, l, depth, gb):
    zb = sv["zb"]
    t = zb.shape[0]
    tm = min(ROW_TILE, t)
    hb = tm // 8
    last_blk = t // 8 - 1
    nsteps = t // tm

    def body(dmix_ref, ya_ref, yb_ref, yc_ref, zb_ref, zprev_ref, znext_ref, ycn_ref, dmn_ref,
             gln_ref, bln_ref, wsp_ref, bsp_ref, cw_ref, go_ref,
             dya_ref, dzb_ref, delta_ref, dgo_ref, dgln_ref, dbln_ref, dwsp_ref, dbsp_ref, dcw_ref):
        i = pl.program_id(0)
        _acc_init(i, dgo_ref, dgln_ref, dbln_ref, dwsp_ref, dbsp_ref, dcw_ref)
        go = go_ref[...]
        dmix = dmix_ref[...]

        ya = ya_ref[...]
        dya, dga = _rms_bwd(ya, go[:, 0:512], dmix[:, 0:512])
        dyb, dgb_ = _rms_bwd(yb_ref[...], go[:, 512:768], dmix[:, 512:768])
        dyc, dgc_ = _rms_bwd(yc_ref[...], go[:, 768:1024], dmix[:, 768:1024])
        dgo_ref[:, 0:512] += dga
        dgo_ref[:, 512:768] += dgb_
        dgo_ref[:, 768:1024] += dgc_
        dya = dya * LN2
        dya_ref[...] = dya.astype(BF16)
        prod = dya * ya
        hmasks = _group_masks((tm, 512))
        for h in range(HEADS):
            delta_ref[h] = jnp.sum(jnp.where(hmasks[h], prod, 0.0), axis=-1, keepdims=True)

        masks = _group_masks((tm, SGW))
        cmasks = _group_masks((CHUNK, SGW))
        tril = _tril_mask()
        wc_bf = [jnp.where(tril, wsp_ref[g], 0.0).astype(BF16) for g in range(4)]
        zu, zv = zb_ref[:, 0:256], zb_ref[:, 256:512]
        g_ln = gln_ref[...]
        u, _, xh, rs, vn, mixed = _sgu_forward(zu, zv, g_ln, bln_ref[...], wc_bf, bsp_ref[...], masks, cmasks)
        du = dyb * mixed
        dmixed = dyb * u
        dvn_chunks = []
        dbsp = jnp.zeros((CHUNK, SGW), F32)
        for ci in range(tm // CHUNK):
            rows = slice(ci * CHUNK, (ci + 1) * CHUNK)
            dm_c = dmixed[rows, :]
            vn_c = vn[rows, :].astype(BF16)
            dbsp = dbsp + dm_c
            dvn_c = jnp.zeros((CHUNK, SGW), F32)
            for g in range(4):
                dm_g = jnp.where(cmasks[g], dm_c, 0.0).astype(BF16)
                dw = lax.dot_general(dm_g, vn_c, NT, preferred_element_type=F32)
                dwsp_ref[g] += jnp.where(tril, dw, 0.0)
                dvn_c = dvn_c + lax.dot_general(wc_bf[g], dm_g, TN, preferred_element_type=F32)
            dvn_chunks.append(dvn_c)
        dbsp_ref[...] += dbsp
        dvn = jnp.concatenate(dvn_chunks, axis=0) if len(dvn_chunks) > 1 else dvn_chunks[0]
        dgln_ref[...] += jnp.sum(dvn * xh, axis=0, keepdims=True)
        dbln_ref[...] += jnp.sum(dvn, axis=0, keepdims=True)
        dxh = dvn * g_ln
        dvv = rs * (dxh - _group_mean(dxh, masks) - xh * _group_mean(dxh * xh, masks))
        dzb_ref[:, 0:256] = (du * _gelu_grad(zu)).astype(BF16)
        dzb_ref[:, 256:512] = (dvv * _gelu_grad(zv)).astype(BF16)

        cwv = cw_ref[...]
        gb_, gc, hh = zb_ref[:, 512:768], zb_ref[:, 768:1024], zb_ref[:, 1024:1280]
        yv, sh1, sh2, conv = _conv_forward(gc, hh, zprev_ref[:, 768:1024], zprev_ref[:, 1024:1280], i == 0, cwv)
        dconv = dyc * gb_
        dzb_ref[:, 512:768] = (dyc * conv).astype(BF16)
        dcw_ref[0:1, :] += jnp.sum(dconv * sh2, axis=0, keepdims=True)
        dcw_ref[1:2, :] += jnp.sum(dconv * sh1, axis=0, keepdims=True)
        dcw_ref[2:3, :] += jnp.sum(dconv * yv, axis=0, keepdims=True)
        dycn, _ = _rms_bwd(ycn_ref[...], go[:, 768:1024], dmn_ref[...])
        dconv_next = jnp.where(i == nsteps - 1, 0.0, dycn * znext_ref[:, 512:768])
        n0, n1 = _pick_row(dconv_next, 0), _pick_row(dconv_next, 1)
        dyv = dconv * cwv[2:3, :] + _shift_up(dconv, 1, [n0]) * cwv[1:2, :] + _shift_up(dconv, 2, [n0, n1]) * cwv[0:1, :]
        dzb_ref[:, 768:1024] = (dyv * hh).astype(BF16)
        dzb_ref[:, 1024:1280] = (dyv * gc).astype(BF16)

    prev_map = lambda i: (jnp.maximum(i * hb - 1, 0), 0)
    next_map = lambda i: (jnp.minimum((i + 1) * hb, last_blk), 0)
    names = ("out_norm_g", "sg_ln_g", "sg_ln_b", "w_sp", "b_sp_t", "conv_w")
    shapes = ((1, D), (1, SGW), (1, SGW), (4, CHUNK, CHUNK), (CHUNK, SGW), (3, CVW))
    outs = _pcall(
        body, "mixer_bwd", (nsteps,),
        [dmix, sv["ya"], sv["yb"], sv["yc"], zb, zb, zb, sv["yc"], dmix, p["sg_ln_g"], p["sg_ln_b"], p["w_sp"], p["b_sp"],
         p["conv_w"], p["out_norm_g"]],
        [_row(tm, D), _row(tm, 512), _row(tm, SGW), _row(tm, CVW), _row(tm, ZB),
         pl.BlockSpec((8, ZB), prev_map), pl.BlockSpec((8, ZB), next_map), pl.BlockSpec((8, CVW), next_map),
         pl.BlockSpec((8, 256), lambda i: (jnp.minimum((i + 1) * hb, last_blk), 3)),
         _lyr(l, 1, SGW), _lyr(l, 1, SGW), _lyr(l, 4, CHUNK, CHUNK), _lyr(l, CHUNK, SGW), _lyr(_wl(p["conv_w"], l), 3, CVW), _lyr(l, 1, D)],
        [_row(tm, 512), _row(tm, ZB), pl.BlockSpec((HEADS, tm, 1), lambda i: (0, i, 0))] + [_lyr(l, *s) for s in shapes],
        [_sds((t, 512), BF16), _sds((t, ZB), BF16), _sds((HEADS, t, 1), F32)] + [_sds((depth,) + s, F32) for s in shapes],
        ("arbitrary",), prevs={3 + n: gb.get(name) for n, name in enumerate(names)})
    for n, name in enumerate(names):
        gb[name] = outs[3 + n]
    return outs[0], outs[1], outs[2]


def _attn_bwd(qs, k, v, dya, lse, delta):
    t = qs.shape[0]
    tq = _att_tile(t)
    nq = t // tq

    def body(q_ref, k_ref, v_ref, do_ref, lse_ref, dl_ref, dq_ref, dk_ref, dv_ref, dq_s, dk_s, dv_s):
        j, i = pl.program_id(1), pl.program_id(2)

        @pl.when((j == 0) & (i == 0))
        def _():
            dq_s[...] = jnp.zeros(dq_s.shape, F32)

        def step(masked):
            keep = _causal_keep(tq, 0, 0) if masked else None
            lane = lax.broadcasted_iota(jnp.int32, (tq, 128), 1)
            vv = v_ref[...]
            do = do_ref[...]
            rows = pl.ds(pl.multiple_of(i * tq, tq), tq)
            for hh in range(2):
                sl = slice(128 * hh, 128 * hh + 128)
                qq, kk = q_ref[:, sl], k_ref[:, sl]
                s = lax.dot_general(qq, kk, NT, preferred_element_type=F32)
                p = jnp.exp2(s - lse_ref[hh])
                if masked:
                    p = jnp.where(keep, p, 0.0)
                do_h = jnp.where((lane < VD) if hh == 0 else (lane >= VD), do, jnp.zeros_like(do))
                dp = lax.dot_general(do_h, vv, NT, preferred_element_type=F32)
                ds = (p * (dp - dl_ref[hh])).astype(BF16)
                dv_s[...] += lax.dot_general(p.astype(BF16), do_h, TN, preferred_element_type=F32)
                dk_s[:, sl] += lax.dot_general(ds, qq, TN, preferred_element_type=F32)
                dq_s[rows, sl] += jnp.dot(ds, kk, preferred_element_type=F32)

        @pl.when(i == j)
        def _():
            dk_s[...] = jnp.zeros(dk_s.shape, F32)
            dv_s[...] = jnp.zeros(dv_s.shape, F32)
            step(True)

        @pl.when(i > j)
        def _():
            step(False)

        @pl.when(i == nq - 1)
        def _():
            dk_ref[...] = dk_s[...].astype(BF16)
            dv_ref[...] = (dv_s[...] * LOG2E).astype(BF16)

        @pl.when((i == nq - 1) & (j == nq - 1))
        def _():
            dq_ref[...] = dq_s[...].astype(BF16)

    qrow = lambda p, j, i: (jnp.maximum(i, j), p)
    col_spec = pl.BlockSpec((2, tq, 1), lambda p, j, i: (p, jnp.maximum(i, j), 0))
    return pl.pallas_call(
        body, name="attn_bwd", grid=(HEADS // 2, nq, nq),
        in_specs=[pl.BlockSpec((tq, 256), qrow), pl.BlockSpec((tq, 256), lambda p, j, i: (j, p)),
                  pl.BlockSpec((tq, 128), lambda p, j, i: (j, p)), pl.BlockSpec((tq, 128), qrow), col_spec, col_spec],
        out_specs=[pl.BlockSpec((t, 256), lambda p, j, i: (0, p)), pl.BlockSpec((tq, 256), lambda p, j, i: (j, p)),
                   pl.BlockSpec((tq, 128), lambda p, j, i: (j, p))],
        out_shape=[_sds((t, QW), BF16), _sds((t, QW), BF16), _sds((t, HEADS * VD), BF16)],
        scratch_shapes=[pltpu.VMEM((t, 256), F32), pltpu.VMEM((tq, 256), F32), pltpu.VMEM((tq, 128), F32)],
        compiler_params=_cp("arbitrary", "arbitrary", "arbitrary"))(qs, k, v, dya, lse, delta)


def _mla_prep_bwd(dqs, dk, dv, sv, p, l, depth, gb, tabs):
    za = sv["za"]
    t = za.shape[0]
    tm = min(ROW_TILE, t)

    def body(dq_ref, dk_ref, dv_ref, z_ref, gq_ref, gkv_ref, wuq_ref, wukv_ref, c_ref, s1_ref, s2_ref,
             dza_ref, dqp_ref, dkv_ref, dgq_ref, dgkv_ref):
        _acc_init(pl.program_id(0), dgq_ref, dgkv_ref)
        c, s1, s2 = c_ref[...], s1_ref[...], s2_ref[...]
        lane = lax.broadcasted_iota(jnp.int32, (tm, 128), 1)
        rope_lanes = (lane >= NOPE) & (lane < NOPE + ROPE)
        dkr = jnp.zeros((tm, 128), F32)
        for h in range(HEADS):
            sl = slice(128 * h, 128 * h + 128)
            dqp_ref[:, sl] = _rope_bwd(dq_ref[:, sl].astype(F32) * QSCALE, c, s1, s2).astype(BF16)
            dkh = dk_ref[:, sl]
            dkv_ref[:, sl] = dkh
            dkr = dkr + jnp.where(rope_lanes, dkh.astype(F32), 0.0)
        dkv_ref[:, QW:] = dv_ref[...]
        z = z_ref[...]
        dcq = lax.dot_general(dqp_ref[...], wuq_ref[...], NT, preferred_element_type=F32)
        dzq, dgq = _rms_bwd(z[:, :QR], gq_ref[...], dcq)
        dckv = lax.dot_general(dkv_ref[...], wukv_ref[...], NT, preferred_element_type=F32)
        dzkv, dgkv = _rms_bwd(z[:, QR:QR + KVR], gkv_ref[...], dckv)
        dgq_ref[...] += dgq
        dgkv_ref[...] += dgkv
        dza_ref[:, :QR] = dzq.astype(BF16)
        dza_ref[:, QR:QR + KVR] = dzkv.astype(BF16)
        dza_ref[:, QR + KVR:] = _rope_bwd(dkr, c, s1, s2).astype(BF16)

    dza, dqp, dkv, gb["q_norm_g"], gb["kv_norm_g"] = _pcall(
        body, "mla_prep_bwd", (t // tm,),
        [dqs, dk, dv, za, p["q_norm_g"], p["kv_norm_g"], p["w_uq"], p["w_ukv"], *tabs],
        [_row(tm, QW), _row(tm, QW), _row(tm, HEADS * VD), _row(tm, ZA), _lyr(l, 1, QR), _lyr(l, 1, KVR),
         _lyr(_wl(p["w_uq"], l), QR, QW), _lyr(_wl(p["w_ukv"], l), KVR, KVW), _row(tm, 128), _row(tm, 128), _row(tm, 128)],
        [_row(tm, ZA), _row(tm, QW), _row(tm, KVW), _lyr(l, 1, QR), _lyr(l, 1, KVR)],
        [_sds((t, ZA), BF16), _sds((t, QW), BF16), _sds((t, KVW), BF16), _sds((depth, 1, QR), F32),
         _sds((depth, 1, KVR), F32)], ("arbitrary",), prevs={3: gb.get("q_norm_g"), 4: gb.get("kv_norm_g")})
    return dza, dqp, dkv


def _in_proj_bwd(dza, dzb, dx2, sv, p, l, depth, gb):
    t = dx2.shape[0]
    tm = min(ROW_TILE, t)

    def body(dza_ref, dzb_ref, wa_ref, wb_ref, x_ref, dx2_ref, g_ref, dx_ref, dg_ref):
        _acc_init(pl.program_id(0), dg_ref)
        dh = (lax.dot_general(dza_ref[...], wa_ref[...], NT, preferred_element_type=F32)
              + lax.dot_general(dzb_ref[...], wb_ref[...], NT, preferred_element_type=F32))
        dx, dg = _rms_bwd(x_ref[...], g_ref[...], dh)
        dg_ref[...] += dg
        dx_ref[...] = dx2_ref[...] + dx

    dx, gb["mix_pre_g"] = _pcall(
        body, "in_proj_bwd", (t // tm,), [dza, dzb, p["w_in_a"], p["w_in_b"], sv["x"], dx2, p["mix_pre_g"]],
        [_row(tm, ZA), _row(tm, ZB), _lyr(_wl(p["w_in_a"], l), D, ZA), _lyr(_wl(p["w_in_b"], l), D, ZB), _row(tm, D), _row(tm, D), _lyr(l, 1, D)],
        [_row(tm, D), _lyr(l, 1, D)], [_sds((t, D), F32), _sds((depth, 1, D), F32)], ("arbitrary",),
        prevs={1: gb.get("mix_pre_g")})
    return dx


def _mm_tn(a, b, tn, name, l, depth, gb):
    t, k = a.shape
    n = b.shape[1]
    tt = min(ROW_TILE, t)

    def body(a_ref, b_ref, o_ref):
        _acc_init(pl.program_id(1), o_ref)
        o_ref[...] += lax.dot_general(a_ref[...], b_ref[...], TN, preferred_element_type=F32)

    gb[name] = _pcall(
        body, "d" + name, (n // tn, t // tt), [a, b],
        [pl.BlockSpec((tt, k), lambda j, s: (s, 0)), pl.BlockSpec((tt, tn), lambda j, s: (s, j))],
        pl.BlockSpec((None, k, tn), lambda j, s: (l, 0, j)), _sds((depth, k, n), F32), ("parallel", "arbitrary"),
        prevs={0: gb.get(name)})


def _dw_ffn(a, b, kind, l, depth, gb):
    t = a.shape[0]
    tt = min(ROW_TILE, t)
    nsteps = t // tt
    half = depth // 2

    down = kind == "down"

    def body(a_ref, b_ref, o_ref, acc):
        s = pl.program_id(0)
        _acc_init(s, acc)
        acc[...] += lax.dot_general(a_ref[...], b_ref[...], TN, preferred_element_type=F32)

        @pl.when(s == nsteps - 1)
        def _():
            for k in range(4):
                blk = acc[k * HP:(k + 1) * HP, :] if down else acc[:, k * HP:(k + 1) * HP]
                o_ref[k] = blk.astype(BF16)

    rows = lambda n: pl.BlockSpec((tt, n), lambda s: (s, 0))
    if down:
        in_specs, acc_shape, name = [rows(DFFP), rows(D)], (DFFP, D), "down"
        out_spec = pl.BlockSpec((4, None, None, HP, D), lambda s: (0, l // half, l % half, 0, 0))
        out_shape = _sds((4, 2, half, HP, D), BF16)
    else:
        which = 0 if kind == "gate" else 1
        in_specs, acc_shape, name = [rows(D), rows(DFFP)], (D, DFFP), "gu"
        out_spec = pl.BlockSpec((4, None, None, None, D, HP), lambda s: (0, l // half, l % half, which, 0, 0))
        out_shape = _sds((4, 2, half, 2, D, HP), BF16)
    gb[name] = _pcall(body, "dw_" + kind, (nsteps,), [a, b], in_specs, out_spec, out_shape, ("arbitrary",),
                      scratch=[pltpu.VMEM(acc_shape, F32)], prevs={0: gb.get(name)})


def _ffn_views(bufs):
    return {"w_gu": bufs[1], "w_down": bufs[2].reshape(bufs[2].shape[:2] + (HP, D))}


def _layer_fwd(x, p, l, tabs, fetch):
    h1, za, zb = _in_proj(x, p, l)
    cqn, ckvn, qs, k, v = _mla_prep(za, p, l, tabs)
    ya, lse, bufs = _attn_fwd(qs, k, v, fetch)
    if fetch:
        p = {**p, **_ffn_views(bufs)}
    mix, yb, yc = _mixer_fwd(zb, ya, p, l)
    o, x2, h2 = _out_proj(mix, x, p, l)
    a, b, s = _ffn_up(h2, p, l)
    f, x3 = _ffn_down(s, x2, p, l)
    saved = dict(x=x, h1=h1, za=za, zb=zb, cqn=cqn, ckvn=ckvn, qs=qs, k=k, v=v, ya=ya, lse=lse, mix=mix, yb=yb, yc=yc,
                 o=o, x2=x2, h2=h2, a=a, b=b, s=s, f=f)
    return x3, saved, bufs if fetch else None


def _layer_bwd(dx3, p, sv, l, depth, gb, tabs):
    t = dx3.shape[0]
    df, da, db = _ffn_down_bwd(dx3, sv, p, l, depth, gb)
    _dw_ffn(sv["s"], df, "down", l, depth, gb)
    dx2 = _ffn_up_bwd(da, db, dx3, sv, p, l, depth, gb)
    _dw_ffn(sv["h2"], da, "gate", l, depth, gb)
    _dw_ffn(sv["h2"], db, "up", l, depth, gb)
    do, dmix = _out_proj_bwd(dx2, sv, p, l, depth, gb)
    _mm_tn(sv["mix"], do, D, "w_out", l, depth, gb)
    dya, dzb, delta = _mixer_bwd(dmix, sv, p, l, depth, gb)
    dqs, dk, dv = _attn_bwd(sv["qs"], sv["k"], sv["v"], dya, sv["lse"], delta)
    dza, dqp, dkv = _mla_prep_bwd(dqs, dk, dv, sv, p, l, depth, gb, tabs)
    _mm_tn(sv["cqn"], dqp, QW, "w_uq", l, depth, gb)
    _mm_tn(sv["ckvn"], dkv, KVW, "w_ukv", l, depth, gb)
    _mm_tn(sv["h1"], dza, ZA, "w_in_a", l, depth, gb)
    _mm_tn(sv["h1"], dzb, ZB, "w_in_b", l, depth, gb)
    return _in_proj_bwd(dza, dzb, dx2, sv, p, l, depth, gb)


def _rope_tables(positions):
    inv_freq = 1.0 / (ROPE_THETA ** (jnp.arange(0, ROPE // 2, dtype=F32) / (ROPE // 2)))
    ang = positions.astype(F32)[:, None] * inv_freq
    cos, sin = jnp.cos(ang), jnp.sin(ang)
    t = positions.shape[0]
    one, zero = jnp.ones((t, 64), F32), jnp.zeros((t, 16), F32)
    c = jnp.concatenate([one, cos, cos, one[:, :32]], axis=1)
    s1 = jnp.concatenate([zero, zero, zero, zero, -sin, zero, zero, zero], axis=1)
    s2 = jnp.concatenate([zero, zero, zero, zero, zero, sin, zero, zero], axis=1)
    return c, s1, s2


def _mixer_weight_params(full):
    w_in = full["w_in"]
    depth = w_in.shape[0]
    zpad = lambda n: jnp.zeros((depth, D, n), w_in.dtype)
    kv = full["w_ukv"].reshape(depth, KVR, HEADS, NOPE + VD)
    return {
        "w_in_a": jnp.concatenate([w_in[:, :, :640], zpad(64), w_in[:, :, 640:672], zpad(32)], axis=2),
        "w_in_b": w_in[:, :, 672:],
        "w_uq": jnp.pad(full["w_uq"].reshape(depth, QR, HEADS, NOPE + ROPE),
                        ((0, 0), (0, 0), (0, 0), (0, 32))).reshape(depth, QR, QW),
        "w_ukv": jnp.concatenate([jnp.pad(kv[..., :NOPE], ((0, 0), (0, 0), (0, 0), (0, 64))).reshape(depth, KVR, QW),
                                  kv[..., NOPE:].reshape(depth, KVR, HEADS * VD)], axis=2),
        "w_out": full["w_out"], "conv_w": full["conv_w"],
    }


def _small_params(w):
    p = {"w_sp": w["w_sp"], "b_sp": jnp.repeat(jnp.swapaxes(w["b_sp"], 1, 2), 64, axis=2)}
    for n in ("mix_pre_g", "mix_post_g", "ffn_pre_g", "ffn_post_g", "q_norm_g", "kv_norm_g", "sg_ln_g", "sg_ln_b",
              "out_norm_g"):
        p[n] = w[n][:, None, :]
    return p


def _natural_grads(gb):
    depth = gb["w_in_a"].shape[0]
    ga, kv = gb["w_in_a"], gb["w_ukv"]
    out = {
        "w_in": jnp.concatenate([ga[:, :, :640], ga[:, :, 704:736], gb["w_in_b"]], axis=2),
        "w_uq": gb["w_uq"].reshape(depth, QR, HEADS, 128)[..., :NOPE + ROPE].reshape(depth, QR, HEADS * (NOPE + ROPE)),
        "w_ukv": jnp.concatenate([kv[:, :, :QW].reshape(depth, KVR, HEADS, 128)[..., :NOPE],
                                  kv[:, :, QW:].reshape(depth, KVR, HEADS, VD)], axis=3).reshape(depth, KVR, -1),
        "b_sp": jnp.swapaxes(gb["b_sp_t"].reshape(depth, CHUNK, 4, 64).sum(axis=-1), 1, 2),
    }
    for n in ("w_out", "w_sp", "conv_w"):
        out[n] = gb[n]
    for n in ("mix_pre_g", "mix_post_g", "ffn_pre_g", "ffn_post_g", "q_norm_g", "kv_norm_g", "sg_ln_g", "sg_ln_b",
              "out_norm_g"):
        out[n] = gb[n][:, 0, :]
    return out


def _local_step(x, positions, target, small, mine, bufs, shard_shapes, fetch=True):
    depth = small["w_sp"].shape[0]
    tabs = _rope_tables(positions)
    ps = _small_params(small)
    saved, mixer_w = [], []
    for l in range(depth):
        mixer_w.append(_mixer_weight_params(_unpack_weights(bufs[0], l, shard_shapes)))
        p = {**ps, **mixer_w[l], **_ffn_views(bufs)}
        x, sv, fetched = _layer_fwd(x, p, l, tabs, (mine, bufs, l + 1) if fetch and l + 1 < depth else None)
        bufs = fetched or bufs
        saved.append(sv)
    dx, acc = _loss_head(x, target)
    loss = (0.5 / D) * jnp.sum(acc)
    gb = {}
    for l in reversed(range(depth)):
        dx = _layer_bwd(dx, {**ps, **mixer_w[l], **_ffn_views(bufs)}, saved[l], l, depth, gb, tabs)
    return loss, dx, _natural_grads(gb), gb["gu"], gb["down"]


def _place():
    x, y, c = lax.axis_index("x"), lax.axis_index("y"), lax.axis_index("c")
    chips = [(1 - x, y), (x, 1 - y), (1 - x, 1 - y)]
    return x, y, c, 2 * x + y, chips


def _remote(src, dst, send_sem, recv_sem, to):
    return pltpu.make_async_remote_copy(src_ref=src, dst_ref=dst, send_sem=send_sem, recv_sem=recv_sem, device_id=to,
                                        device_id_type=MESH_ID)


def _gather_ops(mine_refs, out_refs, sems, l):
    send_sems, recv_sems, fsend_sems, frecv_sems = sems
    x, y, c, k, chips = _place()
    sib = (x, y, 1 - c)
    pairs = [(b, n) for n in range(3) for b in range(len(mine_refs))]

    def slot(n):
        return 2 * chips[n][0] + chips[n][1]

    def ici(b, n, dst_chip):
        return _remote(mine_refs[b].at[l, c], out_refs[b].at[dst_chip, l, c], send_sems.at[b, n], recv_sems.at[b, n],
                       (*chips[n], c))

    def d2d(b, n, half):
        piece = out_refs[b].at[slot(n), l, half]
        return _remote(piece, piece, fsend_sems.at[b, n], frecv_sems.at[b, n], sib)

    def start():
        for b, n in pairs:
            ici(b, n, k).start()

    def hand_over():
        for b, n in pairs:
            ici(b, n, slot(n)).wait_recv()
            d2d(b, n, c).start()

    def drain():
        for b, n in pairs:
            d2d(b, n, 1 - c).wait_recv()
        for b, n in pairs:
            ici(b, n, k).wait_send()
            d2d(b, n, c).wait_send()

    return start, hand_over, drain


def _gather_first_layer(mine):
    nb = len(mine)

    def body(*refs):
        start, hand_over, drain = _gather_ops(refs[:nb], refs[nb:2 * nb], refs[2 * nb:], 0)
        start()
        hand_over()
        drain()

    return pl.pallas_call(
        body, name="gather_first_layer", in_specs=[ANY] * nb, out_specs=[ANY] * nb,
        out_shape=[_sds((4,) + a.shape, a.dtype) for a in mine],
        scratch_shapes=[pltpu.SemaphoreType.DMA((nb, 3))] * 4)(*mine)


def _swap_halves(bigs, small):
    nb = len(bigs)

    def body(*refs):
        big_refs, small_ref = refs[:nb], refs[nb]
        rbig_refs, rsmall_ref = refs[nb + 1:2 * nb + 1], refs[2 * nb + 1]
        send_sems, recv_sems = refs[2 * nb + 2:]
        x, y, c, _, _ = _place()
        sib = (x, y, 1 - c)
        cps = [_remote(big_refs[b].at[:, 1 - c], rbig_refs[b], send_sems.at[b], recv_sems.at[b], sib) for b in range(nb)]
        cps.append(_remote(small_ref, rsmall_ref, send_sems.at[nb], recv_sems.at[nb], sib))
        for cp in cps:
            cp.start()
        for cp in cps:
            cp.wait()

    return pl.pallas_call(
        body, name="swap_halves", in_specs=[ANY] * (nb + 1), out_specs=[ANY] * (nb + 1),
        out_shape=[_sds((4,) + a.shape[2:], a.dtype) for a in bigs] + [_sds(small.shape, small.dtype)],
        scratch_shapes=[pltpu.SemaphoreType.DMA((nb + 1,))] * 2)(*bigs, small)


def _sum_tile(r):
    return max(cand for cand in range(16, 641, 16) if r % cand == 0)


def _pair_sum(big, rbig, c):
    _, _, r, w = big.shape
    tr = _sum_tile(r)

    def body(c_ref, big_ref, rbig_ref, p_ref):
        p_ref[...] = (big_ref[...].astype(F32) + rbig_ref[...].astype(F32)).astype(BF16)

    grid_spec = pltpu.PrefetchScalarGridSpec(
        num_scalar_prefetch=1, grid=(4, r // tr),
        in_specs=[pl.BlockSpec((None, None, tr, w), lambda j, i, cr: (j, cr[0], i, 0)),
                  pl.BlockSpec((None, tr, w), lambda j, i, cr: (j, i, 0))],
        out_specs=pl.BlockSpec((None, tr, w), lambda j, i, cr: (j, i, 0)))
    return pl.pallas_call(body, name="pair_sum", grid_spec=grid_spec, out_shape=_sds((4, r, w), BF16),
                          compiler_params=_cp("parallel", "parallel"))(c, big, rbig)


def _small_sum(parts):
    n, ns, _ = parts.shape

    def body(p_ref, o_ref):
        s = p_ref[0]
        for j in range(1, n):
            s = s + p_ref[j]
        o_ref[...] = s

    return pl.pallas_call(body, name="small_sum", out_shape=_sds((ns, 128), F32))(parts)


def _chip_exchange(ps, small):
    nb = len(ps)
    ns = small.shape[0]

    def body(*refs):
        p_refs, small_ref = refs[:nb], refs[nb]
        rb_refs, rs_ref = refs[nb + 1:2 * nb + 1], refs[2 * nb + 1]
        send_sems, recv_sems, local_sem = refs[2 * nb + 2:]
        x, y, c, k, chips = _place()
        loc = pltpu.make_async_copy(small_ref, rs_ref.at[k], local_sem)
        loc.start()
        sends = []
        for n, (cx, cy) in enumerate(chips):
            to = (cx, cy, c)
            for b in range(nb):
                sends.append(_remote(p_refs[b].at[2 * cx + cy], rb_refs[b].at[n], send_sems.at[b, n], recv_sems.at[b, n], to))
            sends.append(_remote(small_ref, rs_ref.at[k], send_sems.at[nb, n], recv_sems.at[nb, n], to))
        for cp in sends:
            cp.start()
        for n, (cx, cy) in enumerate(chips):
            to = (cx, cy, c)
            for b in range(nb):
                _remote(p_refs[b].at[k], rb_refs[b].at[n], send_sems.at[b, n], recv_sems.at[b, n], to).wait_recv()
            _remote(small_ref, rs_ref.at[2 * cx + cy], send_sems.at[nb, n], recv_sems.at[nb, n], to).wait_recv()
        for cp in sends:
            cp.wait_send()
        loc.wait()

    return pl.pallas_call(
        body, name="chip_exchange", in_specs=[ANY] * (nb + 1), out_specs=[ANY] * (nb + 1),
        out_shape=[_sds((3,) + a.shape[1:], a.dtype) for a in ps] + [_sds((4, ns, 128), small.dtype)],
        scratch_shapes=[pltpu.SemaphoreType.DMA((nb + 1, 3))] * 2 + [pltpu.SemaphoreType.DMA(())])(*ps, small)


def _chip_sum(p, rb, chip):
    _, r, w = p.shape
    tr = _sum_tile(r)

    def body(k_ref, p_ref, rb_ref, o_ref):
        acc = p_ref[...].astype(F32)
        for j in range(3):
            acc = acc + rb_ref[j].astype(F32)
        o_ref[...] = acc

    grid_spec = pltpu.PrefetchScalarGridSpec(
        num_scalar_prefetch=1, grid=(r // tr,),
        in_specs=[pl.BlockSpec((None, tr, w), lambda i, kr: (kr[0], i, 0)), pl.BlockSpec((3, tr, w), lambda i, kr: (0, i, 0))],
        out_specs=pl.BlockSpec((tr, w), lambda i, kr: (i, 0)))
    return pl.pallas_call(body, name="chip_sum", grid_spec=grid_spec, out_shape=_sds((r, w), F32),
                          compiler_params=_cp("parallel"))(chip, p, rb)


def _send_to_sibling(reds):
    nb = len(reds)

    def body(*refs):
        red_refs, out_refs = refs[:nb], refs[nb:2 * nb]
        send_sems, recv_sems = refs[2 * nb:]
        x, y, c, _, _ = _place()
        cps = [_remote(red_refs[b], out_refs[b], send_sems.at[b], recv_sems.at[b], (x, y, 1 - c)) for b in range(nb)]
        for cp in cps:
            cp.start()
        for cp in cps:
            cp.wait()

    return pl.pallas_call(
        body, name="send_to_sibling", in_specs=[ANY] * nb, out_specs=[ANY] * nb,
        out_shape=[_sds(a.shape, a.dtype) for a in reds], scratch_shapes=[pltpu.SemaphoreType.DMA((nb,))] * 2)(*reds)


def _adam_math(w, g, m, v):
    nm = ADAM_B1 * m + (1.0 - ADAM_B1) * g
    nv = ADAM_B2 * v + (1.0 - ADAM_B2) * (g * g)
    m_hat = nm / (1.0 - ADAM_B1 ** ADAM_STEP)
    v_hat = nv / (1.0 - ADAM_B2 ** ADAM_STEP)
    return -ADAM_LR * (m_hat / (jnp.sqrt(v_hat) + ADAM_EPS) + ADAM_WD * w), nm, nv


def _adamw_shard(w, m, v, own, other, c, name, pick=None):
    depth, r, n = w.shape
    half = depth // 2
    tr = max(cand for cand in range(8, min(r, 256) + 1, 8) if r % cand == 0)
    npad = own.shape[-1]

    def body(c_ref, w_ref, m_ref, v_ref, own_ref, oth_ref, g_ref, d_ref, nm_ref, nv_ref):
        mine = (pl.program_id(0) // half) == c_ref[0]
        g = jnp.where(mine, own_ref[...], oth_ref[...])[:, :n]
        g_ref[...] = g
        d_ref[...], nm_ref[...], nv_ref[...] = _adam_math(w_ref[...], g, m_ref[...], v_ref[...])

    blk = pl.BlockSpec((None, tr, n), lambda l, i, cr: (l, i, 0))
    if pick is None:
        gblk = pl.BlockSpec((None, tr, npad), lambda l, i, cr: (l % half, i, 0))
    else:
        gblk = pl.BlockSpec((None, None, tr, npad), lambda l, i, cr: (l % half, pick, i, 0))
    grid_spec = pltpu.PrefetchScalarGridSpec(num_scalar_prefetch=1, grid=(depth, r // tr), in_specs=[blk] * 3 + [gblk] * 2,
                                             out_specs=[blk] * 4)
    return pl.pallas_call(body, name=name, grid_spec=grid_spec, out_shape=[_sds(w.shape, F32)] * 4,
                          compiler_params=_cp("parallel", "parallel"))(c, w, m, v, own, other)


def _pad_ffn_shards(w_gate, w_up, w_down):
    depth = w_gate.shape[0]
    tm = 256
    hr = HP // 2

    def gu_body(g_ref, u_ref, o_ref):
        o_ref[...] = jnp.zeros(o_ref.shape, BF16)
        o_ref[0, :, 0:HS] = g_ref[...].astype(BF16)
        o_ref[1, :, 0:HS] = u_ref[...].astype(BF16)

    blk = pl.BlockSpec((None, tm, HS), lambda l, i: (l, i, 0))
    gu = pl.pallas_call(
        gu_body, name="pad_gate_up", grid=(depth, D // tm), in_specs=[blk, blk],
        out_specs=pl.BlockSpec((None, 2, tm, HP), lambda l, i: (l, 0, i, 0)),
        out_shape=_sds((depth, 2, D, HP), BF16), compiler_params=_cp("parallel", "parallel"))(w_gate, w_up)

    def down_body(w_ref, o_ref):
        o_ref[0] = w_ref[0:hr, :].astype(BF16)
        o_ref[1, 0:HS - hr, :] = w_ref[hr:HS, :].astype(BF16)
        o_ref[1, HS - hr:hr, :] = jnp.zeros((HP - HS, D), BF16)

    down = pl.pallas_call(
        down_body, name="pad_down", grid=(depth,), in_specs=[pl.BlockSpec((None, HS, D), lambda l: (l, 0, 0))],
        out_specs=pl.BlockSpec((None, 2, hr, D), lambda l: (l, 0, 0, 0)),
        out_shape=_sds((depth, 2, hr, D), BF16), compiler_params=_cp("parallel"))(w_down)
    return gu, down


def _adamw_small(w, g, m, v):
    r = w.shape[0]
    tr = max(cand for cand in range(8, 513, 8) if r % cand == 0)

    def body(w_ref, g_ref, m_ref, v_ref, d_ref, nm_ref, nv_ref):
        d_ref[...], nm_ref[...], nv_ref[...] = _adam_math(w_ref[...], g_ref[...], m_ref[...], v_ref[...])

    blk = pl.BlockSpec((tr, 128), lambda i: (i, 0))
    return pl.pallas_call(body, name="adamw_small", grid=(r // tr,), in_specs=[blk] * 4, out_specs=[blk] * 3,
                          out_shape=[_sds(w.shape, F32)] * 3, compiler_params=_cp("parallel"))(w, g, m, v)


def _to_pack(a, name):
    depth = a.shape[0]
    if name in ROW_SHARDED:
        return jnp.swapaxes(a.reshape(depth, 4, -1, D), 0, 1)
    return jnp.transpose(a.reshape(depth, a.shape[1], 4, a.shape[2] // 4), (2, 0, 1, 3)).reshape(4, depth, -1, D)


def _pack_rows(parts, lead, dtype, tail=None):
    pieces, at = [], 0
    for n, off, rows in PACK:
        if off > at:
            pieces.append(jnp.zeros(lead + (off - at, D), dtype))
        pieces.append(parts[n].astype(dtype))
        at = off + rows
    if tail is not None:
        pieces.append(tail)
        at += tail.shape[-2]
    pieces.append(jnp.zeros(lead + (PACK_ROWS - at, D), dtype))
    return jnp.concatenate(pieces, axis=len(lead))


def _pack_weight_shards(sh):
    depth = sh["w_in"].shape[0]
    parts = {n: sh[n].reshape(depth, rows, D) for n, _, rows in PACK}
    conv = lax.bitcast_convert_type(sh["conv_w"].reshape(depth, 3 * 64), BF16).reshape(depth, 1, 384)
    flat = _pack_rows(parts, (depth,), BF16, tail=jnp.pad(conv, ((0, 0), (0, 0), (0, D - 384))))
    return flat.reshape(depth, 2, PACK_ROWS // 2, D)


def _unpack_weights(gathered, l, shard_shapes):
    depth = 1
    flat = gathered[:, l].reshape(4, 1, PACK_ROWS, D)
    full = {}
    for n, off, rows in PACK:
        shp = shard_shapes[n][1:]
        piece = flat[:, :, off:off + rows, :].reshape((4, depth) + shp)
        if n in ROW_SHARDED:
            full[n] = jnp.transpose(piece, (1, 0, 2, 3)).reshape(depth, 4 * shp[0], shp[1])
        else:
            full[n] = jnp.transpose(piece, (1, 2, 0, 3)).reshape(depth, shp[0], 4 * shp[1])
    conv = lax.bitcast_convert_type(flat[:, :, CONV_ROW, :384].reshape(4, depth, 192, 2), F32)
    full["conv_w"] = jnp.transpose(conv.reshape(4, depth, 3, 64), (1, 2, 0, 3)).reshape(depth, 3, CVW)
    return full


def _pack_grad_shards(g):
    depth = g["w_in"].shape[0]
    flat = _pack_rows({n: _to_pack(g[n], n) for n, _, _ in PACK}, (4, depth), BF16)
    return flat.reshape(4, 2, depth // 2 * PACK_ROWS, D)


def _pack_small(arrs, names_shapes, depth):
    flat = jnp.concatenate([arrs[n].reshape(depth, -1) for n, _ in names_shapes], axis=1).reshape(-1)
    rows = -(-flat.shape[0] // 1024) * 8
    return jnp.pad(flat, (0, rows * 128 - flat.shape[0])).reshape(rows, 128)


def _unpack_small(packed, names_shapes, depth):
    per_layer = sum(math.prod(s) for _, s in names_shapes)
    flat = packed.reshape(-1)[:depth * per_layer].reshape(depth, per_layer)
    out, off = {}, 0
    for n, s in names_shapes:
        size = math.prod(s)
        out[n] = flat[:, off:off + size].reshape((depth,) + s)
        off += size
    return out


def kernel(x, positions, mix_pre_g, mix_post_g, ffn_pre_g, ffn_post_g, w_in, q_norm_g, w_uq, kv_norm_g, w_ukv, sg_ln_g, sg_ln_b, w_sp, b_sp, conv_w, out_norm_g, w_out, w_gate, w_up, w_down, loss_target, m_mix_pre_g, m_mix_post_g, m_ffn_pre_g, m_ffn_post_g, m_w_in, m_q_norm_g, m_w_uq, m_kv_norm_g, m_w_ukv, m_sg_ln_g, m_sg_ln_b, m_w_sp, m_b_sp, m_conv_w, m_out_norm_g, m_w_out, m_w_gate, m_w_up, m_w_down, v_mix_pre_g, v_mix_post_g, v_ffn_pre_g, v_ffn_post_g, v_w_in, v_q_norm_g, v_w_uq, v_kv_norm_g, v_w_ukv, v_sg_ln_g, v_sg_ln_b, v_w_sp, v_b_sp, v_conv_w, v_out_norm_g, v_w_out, v_w_gate, v_w_up, v_w_down):
    w = dict(mix_pre_g=mix_pre_g, mix_post_g=mix_post_g, ffn_pre_g=ffn_pre_g, ffn_post_g=ffn_post_g, w_in=w_in,
             q_norm_g=q_norm_g, w_uq=w_uq, kv_norm_g=kv_norm_g, w_ukv=w_ukv, sg_ln_g=sg_ln_g, sg_ln_b=sg_ln_b, w_sp=w_sp,
             b_sp=b_sp, conv_w=conv_w, out_norm_g=out_norm_g, w_out=w_out, w_gate=w_gate, w_up=w_up, w_down=w_down)
    m = dict(mix_pre_g=m_mix_pre_g, mix_post_g=m_mix_post_g, ffn_pre_g=m_ffn_pre_g, ffn_post_g=m_ffn_post_g, w_in=m_w_in,
             q_norm_g=m_q_norm_g, w_uq=m_w_uq, kv_norm_g=m_kv_norm_g, w_ukv=m_w_ukv, sg_ln_g=m_sg_ln_g, sg_ln_b=m_sg_ln_b,
             w_sp=m_w_sp, b_sp=m_b_sp, conv_w=m_conv_w, out_norm_g=m_out_norm_g, w_out=m_w_out, w_gate=m_w_gate,
             w_up=m_w_up, w_down=m_w_down)
    v = dict(mix_pre_g=v_mix_pre_g, mix_post_g=v_mix_post_g, ffn_pre_g=v_ffn_pre_g, ffn_post_g=v_ffn_post_g, w_in=v_w_in,
             q_norm_g=v_q_norm_g, w_uq=v_w_uq, kv_norm_g=v_kv_norm_g, w_ukv=v_w_ukv, sg_ln_g=v_sg_ln_g, sg_ln_b=v_sg_ln_b,
             w_sp=v_w_sp, b_sp=v_b_sp, conv_w=v_conv_w, out_norm_g=v_out_norm_g, w_out=v_w_out, w_gate=v_w_gate,
             w_up=v_w_up, w_down=v_w_down)
    depth = w_in.shape[0]
    c = lax.axis_index("c").astype(jnp.int32).reshape(1)
    chip = (2 * lax.axis_index("x") + lax.axis_index("y")).astype(jnp.int32)

    half = depth // 2

    mine = [_pack_weight_shards(w), *_pad_ffn_shards(w_gate, w_up, w_down)]
    bufs = [lax.dynamic_update_slice(g, a[None], (chip,) + (0,) * a.ndim)
            for g, a in zip(_gather_first_layer(mine), mine)]

    loss, dx, grads, g_gu, g_down = _local_step(x[0], positions[0], loss_target[0], w, mine, bufs,
                                                {n: w[n].shape for n, _, _ in PACK})
    loss = lax.psum(loss, ("x", "y", "c"))

    small = _pack_small(grads, SMALL, depth)
    bigs = [_pack_grad_shards(grads), g_gu.reshape(4, 2, half * 2 * D, HP), g_down.reshape(4, 2, half * HP, D)]
    *rbigs, rsmall = _swap_halves(bigs, small)
    ps = [_pair_sum(a, r, c) for a, r in zip(bigs, rbigs)]
    *rbs, rs = _chip_exchange(ps, _small_sum(jnp.stack([small, rsmall])))
    own = [_chip_sum(p, rb, chip.reshape(1)) for p, rb in zip(ps, rbs)]
    other = _send_to_sibling(own)
    g_small = _unpack_small(_small_sum(rs), SMALL, depth)
    g_small["conv_w"] = lax.dynamic_slice_in_dim(g_small["conv_w"], chip * 64, 64, axis=2)

    gw, delta, new_m, new_v = dict(g_small), {}, {}, {}

    def adam(n, own_g, other_g, pick=None):
        shp = w[n].shape
        outs = _adamw_shard(w[n], m[n], v[n], own_g, other_g, c, "adamw_" + n, pick)
        gw[n], delta[n], new_m[n], new_v[n] = outs

    for n, off, rows in PACK:
        nat = lambda a: a.reshape(half, PACK_ROWS, D)[:, off:off + rows, :].reshape((half,) + w[n].shape[1:])
        adam(n, nat(own[0]), nat(other[0]))
    adam("w_gate", own[1].reshape(half, 2, D, HP), other[1].reshape(half, 2, D, HP), 0)
    adam("w_up", own[1].reshape(half, 2, D, HP), other[1].reshape(half, 2, D, HP), 1)
    adam("w_down", own[2].reshape(half, HP, D), other[2].reshape(half, HP, D))
    small_local = tuple((n, w[n].shape[1:]) for n, _ in SMALL)
    d_, m_, v_ = _adamw_small(_pack_small(w, small_local, depth), _pack_small(gw, small_local, depth),
                              _pack_small(m, small_local, depth), _pack_small(v, small_local, depth))
    delta.update(_unpack_small(d_, small_local, depth))
    new_m.update(_unpack_small(m_, small_local, depth))
    new_v.update(_unpack_small(v_, small_local, depth))

    return (loss, dx[None], *[gw[n] for n in WEIGHTS], *[delta[n] for n in WEIGHTS], *[new_m[n] for n in WEIGHTS],
            *[new_v[n] for n in WEIGHTS])
```

```python
import math

import jax
import jax.numpy as jnp
from jax import lax
from jax.experimental import pallas as pl
from jax.experimental.pallas import tpu as pltpu

F32 = jnp.float32
BF16 = jnp.bfloat16

D = 1024
HEADS = 8
NOPE = 64
ROPE = 32
VD = 64
QR = 384
KVR = 256
SGW = 256
CVW = 256
CHUNK = 128
DFF = 2816
EPS = 1e-6
ROPE_THETA = 10000.0
LOG2E = 1.4426950408889634
LN2 = 0.6931471805599453
QSCALE = (NOPE + ROPE) ** -0.5 * LOG2E
ZA = 768
ZB = 1280
QW = HEADS * 128
KVW = HEADS * 128 + HEADS * VD
NEG = -1e30
GC0 = 0.7978845608028654
GC1 = 0.044715

ADAM_LR = 0.001
ADAM_B1 = 0.9
ADAM_B2 = 0.999
ADAM_EPS = 1e-08
ADAM_WD = 0.01
ADAM_STEP = 10

V7X_VMEM_LIMIT = 52 * 1024 * 1024
ROW_TILE = 512
ATT_TILE = 512

NT = (((1,), (1,)), ((), ()))
TN = (((0,), (0,)), ((), ()))

HS = DFF // 4
HP = 768
DFFP = 4 * HP

PACK = (("w_in", 0, 488), ("w_out", 512, 256), ("w_ukv", 768, 64), ("w_uq", 832, 72))
CONV_ROW = 904
PACK_ROWS = 928
ROW_SHARDED = ("w_out",)
SMALL = (("mix_pre_g", (D,)), ("mix_post_g", (D,)), ("ffn_pre_g", (D,)), ("ffn_post_g", (D,)), ("q_norm_g", (QR,)),
         ("kv_norm_g", (KVR,)), ("sg_ln_g", (SGW,)), ("sg_ln_b", (SGW,)), ("w_sp", (4, CHUNK, CHUNK)), ("b_sp", (4, CHUNK)),
         ("conv_w", (3, CVW)), ("out_norm_g", (D,)))
WEIGHTS = ["mix_pre_g", "mix_post_g", "ffn_pre_g", "ffn_post_g", "w_in", "q_norm_g", "w_uq", "kv_norm_g", "w_ukv", "sg_ln_g",
           "sg_ln_b", "w_sp", "b_sp", "conv_w", "out_norm_g", "w_out", "w_gate", "w_up", "w_down"]

MESH_ID = pl.DeviceIdType.MESH
ANY = pl.BlockSpec(memory_space=pl.ANY)


def _cp(*sem):
    return pltpu.CompilerParams(dimension_semantics=sem, vmem_limit_bytes=V7X_VMEM_LIMIT)


def _sds(shape, dtype):
    return jax.ShapeDtypeStruct(shape, dtype)


def _row(tm, n):
    return pl.BlockSpec((tm, n), lambda i: (i, 0))


def _lyr(l, *shape):
    return pl.BlockSpec((None,) + shape, lambda *_: (l,) + (0,) * len(shape))


def _wl(a, l):
    return 0 if a.shape[0] == 1 else l


def _pcall(body, name, grid, ins, in_specs, out_specs, out_shape, sem, scratch=(), prevs=None):
    prevs = {k: v for k, v in (prevs or {}).items() if v is not None}
    order = sorted(prevs)
    n_in = len(ins)

    def wrapped(*refs):
        return body(*refs[:n_in], *refs[n_in + len(order):])

    return pl.pallas_call(
        wrapped, name=name, grid=grid, in_specs=list(in_specs) + [ANY] * len(order), out_specs=out_specs,
        out_shape=out_shape, scratch_shapes=list(scratch),
        input_output_aliases={n_in + i: k for i, k in enumerate(order)},
        compiler_params=_cp(*sem))(*ins, *[prevs[k] for k in order])


def _rms(x, g):
    r = lax.rsqrt(jnp.mean(x * x, axis=-1, keepdims=True) + EPS)
    return x * r * g


def _rms_bwd(x, g, dy):
    r = lax.rsqrt(jnp.mean(x * x, axis=-1, keepdims=True) + EPS)
    xh = x * r
    dg = jnp.sum(dy * xh, axis=0, keepdims=True)
    dxh = dy * g
    dx = r * (dxh - xh * jnp.mean(dxh * xh, axis=-1, keepdims=True))
    return dx, dg


def _gelu(x):
    return 0.5 * x * (1.0 + jnp.tanh(GC0 * (x + GC1 * x * x * x)))


def _gelu_grad(x):
    t = jnp.tanh(GC0 * (x + GC1 * x * x * x))
    return 0.5 * (1.0 + t) + 0.5 * x * (1.0 - t * t) * GC0 * (1.0 + 3.0 * GC1 * x * x)


def _rope(xb, c, s1, s2):
    return xb * c + pltpu.roll(xb, 112, 1) * s1 + pltpu.roll(xb, 16, 1) * s2


def _rope_bwd(dy, c, s1, s2):
    return dy * c + pltpu.roll(dy * s1, 16, 1) + pltpu.roll(dy * s2, 112, 1)


def _group_masks(shape):
    lane = lax.broadcasted_iota(jnp.int32, shape, 1)
    return [(lane >= 64 * g) & (lane < 64 * g + 64) for g in range(shape[1] // 64)]


def _group_mean(v, masks):
    out = jnp.zeros_like(v)
    for m in masks:
        s = jnp.sum(jnp.where(m, v, 0.0), axis=-1, keepdims=True) * (1.0 / 64.0)
        out = jnp.where(m, s, out)
    return out


def _pick_row(blk, idx):
    row = lax.broadcasted_iota(jnp.int32, blk.shape, 0)
    return jnp.sum(jnp.where(row == idx, blk, 0.0), axis=0, keepdims=True)


def _shift_down(y, k, first_rows):
    out = pltpu.roll(y, k, 0)
    row = lax.broadcasted_iota(jnp.int32, y.shape, 0)
    for idx in range(k):
        out = jnp.where(row == idx, first_rows[idx], out)
    return out


def _shift_up(y, k, last_rows):
    n = y.shape[0]
    out = pltpu.roll(y, n - k, 0)
    row = lax.broadcasted_iota(jnp.int32, y.shape, 0)
    for idx in range(k):
        out = jnp.where(row == n - k + idx, last_rows[idx], out)
    return out


def _tril_mask():
    r = lax.broadcasted_iota(jnp.int32, (CHUNK, CHUNK), 0)
    c = lax.broadcasted_iota(jnp.int32, (CHUNK, CHUNK), 1)
    return r >= c


def _sgu_forward(zu, zv, g_ln, b_ln, wc_bf, bsp, masks, cmasks):
    u = _gelu(zu)
    vv = _gelu(zv)
    mu = _group_mean(vv, masks)
    dv = vv - mu
    rs = lax.rsqrt(_group_mean(dv * dv, masks) + EPS)
    xh = dv * rs
    vn = xh * g_ln + b_ln
    chunks = []
    for ci in range(zu.shape[0] // CHUNK):
        vc = vn[ci * CHUNK:(ci + 1) * CHUNK, :]
        acc = bsp
        for g in range(4):
            acc = acc + jnp.dot(wc_bf[g], jnp.where(cmasks[g], vc, 0.0).astype(BF16), preferred_element_type=F32)
        chunks.append(acc)
    mixed = jnp.concatenate(chunks, axis=0) if len(chunks) > 1 else chunks[0]
    return u, vv, xh, rs, vn, mixed


def _conv_forward(gc, hh, prev_gc, prev_hh, first_tile, cw):
    yv = gc * hh
    prev = jnp.where(first_tile, 0.0, prev_gc * prev_hh)
    p6, p7 = _pick_row(prev, 6), _pick_row(prev, 7)
    sh1 = _shift_down(yv, 1, [p7])
    sh2 = _shift_down(yv, 2, [p6, p7])
    conv = sh2 * cw[0:1, :] + sh1 * cw[1:2, :] + yv * cw[2:3, :]
    return yv, sh1, sh2, conv


def _acc_init(step, *refs):
    @pl.when(step == 0)
    def _():
        for r in refs:
            r[...] = jnp.zeros(r.shape, r.dtype)


def _in_proj(x, p, l):
    t = x.shape[0]
    tm = min(ROW_TILE, t)

    def body(x_ref, g_ref, wa_ref, wb_ref, h_ref, za_ref, zb_ref):
        h = _rms(x_ref[...], g_ref[...]).astype(BF16)
        h_ref[...] = h
        za_ref[...] = jnp.dot(h, wa_ref[...], preferred_element_type=F32)
        zb_ref[...] = jnp.dot(h, wb_ref[...], preferred_element_type=F32)

    return _pcall(
        body, "in_proj", (t // tm,), [x, p["mix_pre_g"], p["w_in_a"], p["w_in_b"]],
        [_row(tm, D), _lyr(l, 1, D), _lyr(_wl(p["w_in_a"], l), D, ZA), _lyr(_wl(p["w_in_b"], l), D, ZB)],
        [_row(tm, D), _row(tm, ZA), _row(tm, ZB)],
        [_sds((t, D), BF16), _sds((t, ZA), F32), _sds((t, ZB), F32)], ("parallel",))


def _mla_prep(za, p, l, tabs):
    t = za.shape[0]
    tm = min(ROW_TILE, t)

    def body(z_ref, gq_ref, gkv_ref, wuq_ref, wukv_ref, c_ref, s1_ref, s2_ref, cq_ref, ckv_ref, q_ref, k_ref, v_ref):
        z = z_ref[...]
        cq = _rms(z[:, :QR], gq_ref[...]).astype(BF16)
        ckv = _rms(z[:, QR:QR + KVR], gkv_ref[...]).astype(BF16)
        cq_ref[...] = cq
        ckv_ref[...] = ckv
        c, s1, s2 = c_ref[...], s1_ref[...], s2_ref[...]
        kr = _rope(z[:, QR + KVR:], c, s1, s2)
        q = jnp.dot(cq, wuq_ref[...], preferred_element_type=F32)
        kv = jnp.dot(ckv, wukv_ref[...], preferred_element_type=F32)
        for h in range(HEADS):
            sl = slice(128 * h, 128 * h + 128)
            q_ref[:, sl] = (_rope(q[:, sl], c, s1, s2) * QSCALE).astype(BF16)
            k_ref[:, sl] = (kv[:, sl] + kr).astype(BF16)
        v_ref[...] = kv[:, QW:].astype(BF16)

    return _pcall(
        body, "mla_prep", (t // tm,), [za, p["q_norm_g"], p["kv_norm_g"], p["w_uq"], p["w_ukv"], *tabs],
        [_row(tm, ZA), _lyr(l, 1, QR), _lyr(l, 1, KVR), _lyr(_wl(p["w_uq"], l), QR, QW), _lyr(_wl(p["w_ukv"], l), KVR, KVW),
         _row(tm, 128), _row(tm, 128), _row(tm, 128)],
        [_row(tm, QR), _row(tm, KVR), _row(tm, QW), _row(tm, QW), _row(tm, HEADS * VD)],
        [_sds((t, QR), BF16), _sds((t, KVR), BF16), _sds((t, QW), BF16), _sds((t, QW), BF16),
         _sds((t, HEADS * VD), BF16)], ("parallel",))


def _att_tile(t):
    return min(ATT_TILE, max(t // 2, 128))


def _causal_keep(tq, i, j):
    row = lax.broadcasted_iota(jnp.int32, (tq, tq), 0) + i * tq
    col = lax.broadcasted_iota(jnp.int32, (tq, tq), 1) + j * tq
    return col <= row


def _attn_fwd(qs, k, v, fetch=None):
    t = qs.shape[0]
    tq = _att_tile(t)
    nq = t // tq
    rep = tq // 128
    mine, bufs, fetch_layer = fetch if fetch else ((), (), None)
    nb = len(mine)

    steps = [(i, j) for i in range(nq) for j in range(i + 1)]
    i_of = jnp.asarray([s[0] for s in steps], jnp.int32)
    j_of = jnp.asarray([s[1] for s in steps], jnp.int32)

    def body(i_ref, j_ref, q_ref, k_ref, v_ref, *refs):
        o_ref, lse_ref = refs[2 * nb:2 * nb + 2]
        m_s, l_s, acc_s = refs[3 * nb + 2:3 * nb + 5]
        step_no = pl.program_id(1)
        i, j = i_ref[step_no], j_ref[step_no]
        if fetch:
            start, hand_over, drain = _gather_ops(refs[:nb], refs[2 * nb + 2:3 * nb + 2], refs[3 * nb + 5:], fetch_layer)
            pr = pl.program_id(0)
            pl.when((pr == 0) & (step_no == 0))(start)
            pl.when((pr == HEADS // 2 - 1) & (step_no == 0))(hand_over)
            pl.when((pr == HEADS // 2 - 1) & (step_no == len(steps) - 1))(drain)

        @pl.when(j == 0)
        def _():
            m_s[...] = jnp.full(m_s.shape, NEG, F32)
            l_s[...] = jnp.zeros(l_s.shape, F32)
            acc_s[...] = jnp.zeros(acc_s.shape, F32)

        def step(masked):
            vv = v_ref[...]
            keep = _causal_keep(tq, i, j) if masked else None
            for hh in range(2):
                sl = slice(128 * hh, 128 * hh + 128)
                s = lax.dot_general(q_ref[:, sl], k_ref[:, sl], NT, preferred_element_type=F32)
                if masked:
                    s = jnp.where(keep, s, NEG)
                m_old = m_s[hh]
                m_new = jnp.maximum(m_old, jnp.max(s, axis=-1, keepdims=True))
                alpha = jnp.exp2(m_old - m_new)
                p = jnp.exp2(s - jnp.tile(m_new, (1, rep)))
                l_s[hh] = alpha * l_s[hh] + jnp.sum(p, axis=-1, keepdims=True)
                acc_s[hh] = alpha * acc_s[hh] + jnp.dot(p.astype(BF16), vv, preferred_element_type=F32)
                m_s[hh] = m_new

        @pl.when(j < i)
        def _():
            step(False)

        @pl.when(j == i)
        def _():
            step(True)
            lane = lax.broadcasted_iota(jnp.int32, (tq, 128), 1)
            o_ref[...] = jnp.where(lane < VD, acc_s[0] / l_s[0], acc_s[1] / l_s[1])
            for hh in range(2):
                lse_ref[hh] = (m_s[hh] + jnp.log2(l_s[hh]))[:, 0:1]

    grid_spec = pltpu.PrefetchScalarGridSpec(
        num_scalar_prefetch=2, grid=(HEADS // 2, len(steps)),
        in_specs=[pl.BlockSpec((tq, 256), lambda p, s, it, jt: (it[s], p)),
                  pl.BlockSpec((tq, 256), lambda p, s, it, jt: (jt[s], p)),
                  pl.BlockSpec((tq, 128), lambda p, s, it, jt: (jt[s], p))] + [ANY] * (2 * nb),
        out_specs=[pl.BlockSpec((tq, 128), lambda p, s, it, jt: (it[s], p)),
                   pl.BlockSpec((2, tq, 1), lambda p, s, it, jt: (p, it[s], 0))] + [ANY] * nb,
        scratch_shapes=[pltpu.VMEM((2, tq, 128), F32), pltpu.VMEM((2, tq, 128), F32), pltpu.VMEM((2, tq, 128), F32)]
        + ([pltpu.SemaphoreType.DMA((nb, 3))] * 4 if fetch else []))
    outs = pl.pallas_call(
        body, name="attn_fwd_fetch" if fetch else "attn_fwd", grid_spec=grid_spec,
        out_shape=[_sds((t, HEADS * VD), F32), _sds((HEADS, t, 1), F32)] + [_sds(b.shape, b.dtype) for b in bufs],
        input_output_aliases={5 + nb + b: 2 + b for b in range(nb)},
        compiler_params=_cp("arbitrary", "arbitrary"))(i_of, j_of, qs, k, v, *mine, *bufs)
    return outs[0], outs[1], list(outs[2:])


def _mixer_fwd(zb, ya, p, l):
    t = zb.shape[0]
    tm = min(ROW_TILE, t)
    hb = tm // 8

    def body(zb_ref, zprev_ref, ya_ref, gln_ref, bln_ref, wsp_ref, bsp_ref, cw_ref, go_ref, mix_ref, yb_ref, yc_ref):
        i = pl.program_id(0)
        masks = _group_masks((tm, SGW))
        cmasks = _group_masks((CHUNK, SGW))
        tril = _tril_mask()
        wc_bf = [jnp.where(tril, wsp_ref[g], 0.0).astype(BF16) for g in range(4)]
        u, _, _, _, _, mixed = _sgu_forward(zb_ref[:, 0:256], zb_ref[:, 256:512], gln_ref[...], bln_ref[...], wc_bf,
                                            bsp_ref[...], masks, cmasks)
        yb = u * mixed
        _, _, _, conv = _conv_forward(zb_ref[:, 768:1024], zb_ref[:, 1024:1280], zprev_ref[:, 768:1024],
                                      zprev_ref[:, 1024:1280], i == 0, cw_ref[...])
        yc = zb_ref[:, 512:768] * conv
        yb_ref[...] = yb
        yc_ref[...] = yc
        go = go_ref[...]
        mix_ref[:, 0:512] = _rms(ya_ref[...], go[:, 0:512]).astype(BF16)
        mix_ref[:, 512:768] = _rms(yb, go[:, 512:768]).astype(BF16)
        mix_ref[:, 768:1024] = _rms(yc, go[:, 768:1024]).astype(BF16)

    return _pcall(
        body, "mixer_fwd", (t // tm,),
        [zb, zb, ya, p["sg_ln_g"], p["sg_ln_b"], p["w_sp"], p["b_sp"], p["conv_w"], p["out_norm_g"]],
        [_row(tm, ZB), pl.BlockSpec((8, ZB), lambda i: (jnp.maximum(i * hb - 1, 0), 0)), _row(tm, 512),
         _lyr(l, 1, SGW), _lyr(l, 1, SGW), _lyr(l, 4, CHUNK, CHUNK), _lyr(l, CHUNK, SGW), _lyr(_wl(p["conv_w"], l), 3, CVW), _lyr(l, 1, D)],
        [_row(tm, D), _row(tm, SGW), _row(tm, CVW)],
        [_sds((t, D), BF16), _sds((t, SGW), F32), _sds((t, CVW), F32)], ("parallel",))


def _out_proj(mix, x, p, l):
    t = x.shape[0]
    tm = min(ROW_TILE, t)

    def body(mix_ref, w_ref, x_ref, gp_ref, gf_ref, o_ref, x2_ref, h2_ref):
        o = jnp.dot(mix_ref[...], w_ref[...], preferred_element_type=F32)
        o_ref[...] = o
        x2 = x_ref[...] + _rms(o, gp_ref[...])
        x2_ref[...] = x2
        h2_ref[...] = _rms(x2, gf_ref[...]).astype(BF16)

    return _pcall(
        body, "out_proj", (t // tm,), [mix, p["w_out"], x, p["mix_post_g"], p["ffn_pre_g"]],
        [_row(tm, D), _lyr(_wl(p["w_out"], l), D, D), _row(tm, D), _lyr(l, 1, D), _lyr(l, 1, D)],
        [_row(tm, D), _row(tm, D), _row(tm, D)],
        [_sds((t, D), F32), _sds((t, D), F32), _sds((t, D), BF16)], ("parallel",))


def _gu_all(l, which):
    return pl.BlockSpec((4, None, None, D, HP), lambda *_: (0, l, which, 0, 0))


def _down_all(l):
    return pl.BlockSpec((4, None, HP, D), lambda *_: (0, l, 0, 0))


def _ffn_up(h2, p, l):
    t = h2.shape[0]
    tm = min(ROW_TILE, t)

    def body(h_ref, wg_ref, wu_ref, a_ref, b_ref, s_ref):
        h = h_ref[...]
        a = jnp.dot(h, wg_ref[...], preferred_element_type=F32)
        b = jnp.dot(h, wu_ref[...], preferred_element_type=F32)
        a_ref[...] = a.astype(BF16)
        b_ref[...] = b.astype(BF16)
        s_ref[...] = (a * (1.0 / (1.0 + jnp.exp(-a))) * b).astype(BF16)

    blk = pl.BlockSpec((tm, HP), lambda k, i: (i, k))
    wblk = lambda which: pl.BlockSpec((None, None, None, D, HP), lambda k, i: (k, l, which, 0, 0))
    return _pcall(
        body, "ffn_up", (4, t // tm), [h2, p["w_gu"], p["w_gu"]],
        [pl.BlockSpec((tm, D), lambda k, i: (i, 0)), wblk(0), wblk(1)], [blk, blk, blk],
        [_sds((t, DFFP), BF16)] * 3, ("parallel", "parallel"))


def _ffn_down(s, x2, p, l):
    t = x2.shape[0]
    tm = min(ROW_TILE, t)

    def body(s_ref, w_ref, x_ref, g_ref, f_ref, x3_ref):
        f = jnp.dot(s_ref[:, 0:HP], w_ref[0], preferred_element_type=F32)
        for k in range(1, 4):
            f = f + jnp.dot(s_ref[:, k * HP:(k + 1) * HP], w_ref[k], preferred_element_type=F32)
        f_ref[...] = f
        x3_ref[...] = x_ref[...] + _rms(f, g_ref[...])

    return _pcall(
        body, "ffn_down", (t // tm,), [s, p["w_down"], x2, p["ffn_post_g"]],
        [_row(tm, DFFP), _down_all(l), _row(tm, D), _lyr(l, 1, D)], [_row(tm, D), _row(tm, D)],
        [_sds((t, D), F32), _sds((t, D), F32)], ("parallel",))


def _loss_head(y, target):
    t = y.shape[0]
    tm = min(ROW_TILE, t)

    def body(y_ref, t_ref, dy_ref, acc_ref):
        e = y_ref[...] - t_ref[...]
        dy_ref[...] = e * (1.0 / D)
        sq = jnp.sum(e * e, axis=0, keepdims=True)
        part = sq[:, 0:128]
        for b in range(1, D // 128):
            part = part + sq[:, 128 * b:128 * b + 128]
        _acc_init(pl.program_id(0), acc_ref)
        acc_ref[...] += part

    return _pcall(body, "loss_head", (t // tm,), [y, target], [_row(tm, D), _row(tm, D)],
                  [_row(tm, D), pl.BlockSpec((1, 128), lambda i: (0, 0))],
                  [_sds((t, D), F32), _sds((1, 128), F32)], ("arbitrary",))


def _ffn_down_bwd(dx3, sv, p, l, depth, gb):
    t = dx3.shape[0]
    tm = min(256, t)

    def body(dx_ref, f_ref, g_ref, w_ref, a_ref, b_ref, df_ref, da_ref, db_ref, dg_ref):
        _acc_init(pl.program_id(0), dg_ref)
        df, dg = _rms_bwd(f_ref[...], g_ref[...], dx_ref[...])
        dg_ref[...] += dg
        df = df.astype(BF16)
        df_ref[...] = df
        for k in range(4):
            sl = slice(k * HP, (k + 1) * HP)
            ds = lax.dot_general(df, w_ref[k], NT, preferred_element_type=F32)
            av = a_ref[:, sl].astype(F32)
            sig = 1.0 / (1.0 + jnp.exp(-av))
            da_ref[:, sl] = (ds * b_ref[:, sl].astype(F32) * (sig * (1.0 + av * (1.0 - sig)))).astype(BF16)
            db_ref[:, sl] = (ds * (av * sig)).astype(BF16)

    df, da, db, gb["ffn_post_g"] = _pcall(
        body, "ffn_down_bwd", (t // tm,), [dx3, sv["f"], p["ffn_post_g"], p["w_down"], sv["a"], sv["b"]],
        [_row(tm, D), _row(tm, D), _lyr(l, 1, D), _down_all(l), _row(tm, DFFP), _row(tm, DFFP)],
        [_row(tm, D), _row(tm, DFFP), _row(tm, DFFP), _lyr(l, 1, D)],
        [_sds((t, D), BF16), _sds((t, DFFP), BF16), _sds((t, DFFP), BF16), _sds((depth, 1, D), F32)], ("arbitrary",),
        prevs={3: gb.get("ffn_post_g")})
    return df, da, db


def _ffn_up_bwd(da, db, dx3, sv, p, l, depth, gb):
    t = dx3.shape[0]
    tm = min(256, t)

    def body(da_ref, db_ref, wg_ref, wu_ref, x_ref, dx3_ref, g_ref, dx2_ref, dg_ref):
        _acc_init(pl.program_id(0), dg_ref)
        dh = jnp.zeros((tm, D), F32)
        for k in range(4):
            sl = slice(k * HP, (k + 1) * HP)
            dh = dh + lax.dot_general(da_ref[:, sl], wg_ref[k], NT, preferred_element_type=F32)
            dh = dh + lax.dot_general(db_ref[:, sl], wu_ref[k], NT, preferred_element_type=F32)
        dx, dg = _rms_bwd(x_ref[...], g_ref[...], dh)
        dg_ref[...] += dg
        dx2_ref[...] = dx3_ref[...] + dx

    dx2, gb["ffn_pre_g"] = _pcall(
        body, "ffn_up_bwd", (t // tm,), [da, db, p["w_gu"], p["w_gu"], sv["x2"], dx3, p["ffn_pre_g"]],
        [_row(tm, DFFP), _row(tm, DFFP), _gu_all(l, 0), _gu_all(l, 1), _row(tm, D), _row(tm, D), _lyr(l, 1, D)],
        [_row(tm, D), _lyr(l, 1, D)], [_sds((t, D), F32), _sds((depth, 1, D), F32)], ("arbitrary",),
        prevs={1: gb.get("ffn_pre_g")})
    return dx2


def _out_proj_bwd(dx2, sv, p, l, depth, gb):
    t = dx2.shape[0]
    tm = min(ROW_TILE, t)

    def body(dx_ref, o_ref, g_ref, w_ref, do_ref, dmix_ref, dg_ref):
        _acc_init(pl.program_id(0), dg_ref)
        do, dg = _rms_bwd(o_ref[...], g_ref[...], dx_ref[...])
        dg_ref[...] += dg
        do = do.astype(BF16)
        do_ref[...] = do
        dmix_ref[...] = lax.dot_general(do, w_ref[...], NT, preferred_element_type=F32)

    do, dmix, gb["mix_post_g"] = _pcall(
        body, "out_proj_bwd", (t // tm,), [dx2, sv["o"], p["mix_post_g"], p["w_out"]],
        [_row(tm, D), _row(tm, D), _lyr(l, 1, D), _lyr(_wl(p["w_out"], l), D, D)], [_row(tm, D), _row(tm, D), _lyr(l, 1, D)],
        [_sds((t, D), BF16), _sds((t, D), F32), _sds((depth, 1, D), F32)], ("arbitrary",),
        prevs={2: gb.get("mix_post_g")})
    return do, dmix


def _mixer_bwd(dmix, sv, p, l, depth, gb):
    zb = sv["zb"]
    t = zb.shape[0]
    tm = min(ROW_TILE, t)
    hb = tm // 8
    last_blk = t // 8 - 1
    nsteps = t // tm

    def body(dmix_ref, ya_ref, yb_ref, yc_ref, zb_ref, zprev_ref, znext_ref, ycn_ref, dmn_ref,
             gln_ref, bln_ref, wsp_ref, bsp_ref, cw_ref, go_ref,
             dya_ref, dzb_ref, delta_ref, dgo_ref, dgln_ref, dbln_ref, dwsp_ref, dbsp_ref, dcw_ref):
        i = pl.program_id(0)
        _acc_init(i, dgo_ref, dgln_ref, dbln_ref, dwsp_ref, dbsp_ref, dcw_ref)
        go = go_ref[...]
        dmix = dmix_ref[...]

        ya = ya_ref[...]
        dya, dga = _rms_bwd(ya, go[:, 0:512], dmix[:, 0:512])
        dyb, dgb_ = _rms_bwd(yb_ref[...], go[:, 512:768], dmix[:, 512:768])
        dyc, dgc_ = _rms_bwd(yc_ref[...], go[:, 768:1024], dmix[:, 768:1024])
        dgo_ref[:, 0:512] += dga
        dgo_ref[:, 512:768] += dgb_
        dgo_ref[:, 768:1024] += dgc_
        dya = dya * LN2
        dya_ref[...] = dya.astype(BF16)
        prod = dya * ya
        hmasks = _group_masks((tm, 512))
        for h in range(HEADS):
            delta_ref[h] = jnp.sum(jnp.where(hmasks[h], prod, 0.0), axis=-1, keepdims=True)

        masks = _group_masks((tm, SGW))
        cmasks = _group_masks((CHUNK, SGW))
        tril = _tril_mask()
        wc_bf = [jnp.where(tril, wsp_ref[g], 0.0).astype(BF16) for g in range(4)]
        zu, zv = zb_ref[:, 0:256], zb_ref[:, 256:512]
        g_ln = gln_ref[...]
        u, _, xh, rs, vn, mixed = _sgu_forward(zu, zv, g_ln, bln_ref[...], wc_bf, bsp_ref[...], masks, cmasks)
        du = dyb * mixed
        dmixed = dyb * u
        dvn_chunks = []
        dbsp = jnp.zeros((CHUNK, SGW), F32)
        for ci in range(tm // CHUNK):
            rows = slice(ci * CHUNK, (ci + 1) * CHUNK)
            dm_c = dmixed[rows, :]
            vn_c = vn[rows, :].astype(BF16)
            dbsp = dbsp + dm_c
            dvn_c = jnp.zeros((CHUNK, SGW), F32)
            for g in range(4):
                dm_g = jnp.where(cmasks[g], dm_c, 0.0).astype(BF16)
                dw = lax.dot_general(dm_g, vn_c, NT, preferred_element_type=F32)
                dwsp_ref[g] += jnp.where(tril, dw, 0.0)
                dvn_c = dvn_c + lax.dot_general(wc_bf[g], dm_g, TN, preferred_element_type=F32)
            dvn_chunks.append(dvn_c)
        dbsp_ref[...] += dbsp
        dvn = jnp.concatenate(dvn_chunks, axis=0) if len(dvn_chunks) > 1 else dvn_chunks[0]
        dgln_ref[...] += jnp.sum(dvn * xh, axis=0, keepdims=True)
        dbln_ref[...] += jnp.sum(dvn, axis=0, keepdims=True)
        dxh = dvn * g_ln
        dvv = rs * (dxh - _group_mean(dxh, masks) - xh * _group_mean(dxh * xh, masks))
        dzb_ref[:, 0:256] = (du * _gelu_grad(zu)).astype(BF16)
        dzb_ref[:, 256:512] = (dvv * _gelu_grad(zv)).astype(BF16)

        cwv = cw_ref[...]
        gb_, gc, hh = zb_ref[:, 512:768], zb_ref[:, 768:1024], zb_ref[:, 1024:1280]
        yv, sh1, sh2, conv = _conv_forward(gc, hh, zprev_ref[:, 768:1024], zprev_ref[:, 1024:1280], i == 0, cwv)
        dconv = dyc * gb_
        dzb_ref[:, 512:768] = (dyc * conv).astype(BF16)
        dcw_ref[0:1, :] += jnp.sum(dconv * sh2, axis=0, keepdims=True)
        dcw_ref[1:2, :] += jnp.sum(dconv * sh1, axis=0, keepdims=True)
        dcw_ref[2:3, :] += jnp.sum(dconv * yv, axis=0, keepdims=True)
        dycn, _ = _rms_bwd(ycn_ref[...], go[:, 768:1024], dmn_ref[...])
        dconv_next = jnp.where(i == nsteps - 1, 0.0, dycn * znext_ref[:, 512:768])
        n0, n1 = _pick_row(dconv_next, 0), _pick_row(dconv_next, 1)
        dyv = dconv * cwv[2:3, :] + _shift_up(dconv, 1, [n0]) * cwv[1:2, :] + _shift_up(dconv, 2, [n0, n1]) * cwv[0:1, :]
        dzb_ref[:, 768:1024] = (dyv * hh).astype(BF16)
        dzb_ref[:, 1024:1280] = (dyv * gc).astype(BF16)

    prev_map = lambda i: (jnp.maximum(i * hb - 1, 0), 0)
    next_map = lambda i: (jnp.minimum((i + 1) * hb, last_blk), 0)
    names = ("out_norm_g", "sg_ln_g", "sg_ln_b", "w_sp", "b_sp_t", "conv_w")
    shapes = ((1, D), (1, SGW), (1, SGW), (4, CHUNK, CHUNK), (CHUNK, SGW), (3, CVW))
    outs = _pcall(
        body, "mixer_bwd", (nsteps,),
        [dmix, sv["ya"], sv["yb"], sv["yc"], zb, zb, zb, sv["yc"], dmix, p["sg_ln_g"], p["sg_ln_b"], p["w_sp"], p["b_sp"],
         p["conv_w"], p["out_norm_g"]],
        [_row(tm, D), _row(tm, 512), _row(tm, SGW), _row(tm, CVW), _row(tm, ZB),
         pl.BlockSpec((8, ZB), prev_map), pl.BlockSpec((8, ZB), next_map), pl.BlockSpec((8, CVW), next_map),
         pl.BlockSpec((8, 256), lambda i: (jnp.minimum((i + 1) * hb, last_blk), 3)),
         _lyr(l, 1, SGW), _lyr(l, 1, SGW), _lyr(l, 4, CHUNK, CHUNK), _lyr(l, CHUNK, SGW), _lyr(_wl(p["conv_w"], l), 3, CVW), _lyr(l, 1, D)],
        [_row(tm, 512), _row(tm, ZB), pl.BlockSpec((HEADS, tm, 1), lambda i: (0, i, 0))] + [_lyr(l, *s) for s in shapes],
        [_sds((t, 512), BF16), _sds((t, ZB), BF16), _sds((HEADS, t, 1), F32)] + [_sds((depth,) + s, F32) for s in shapes],
        ("arbitrary",), prevs={3 + n: gb.get(name) for n, name in enumerate(names)})
    for n, name in enumerate(names):
        gb[name] = outs[3 + n]
    return outs[0], outs[1], outs[2]


def _attn_bwd(qs, k, v, dya, lse, delta):
    t = qs.shape[0]
    tq = _att_tile(t)
    nq = t // tq

    steps = [(j, i) for j in range(nq) for i in range(j, nq)]
    j_of = jnp.asarray([s[0] for s in steps], jnp.int32)
    i_of = jnp.asarray([s[1] for s in steps], jnp.int32)

    def body(j_ref, i_ref, q_ref, k_ref, v_ref, do_ref, lse_ref, dl_ref, dq_ref, dk_ref, dv_ref, dq_s, dk_s, dv_s):
        step_no = pl.program_id(1)
        j, i = j_ref[step_no], i_ref[step_no]

        @pl.when(step_no == 0)
        def _():
            dq_s[...] = jnp.zeros(dq_s.shape, F32)

        def step(masked):
            keep = _causal_keep(tq, 0, 0) if masked else None
            lane = lax.broadcasted_iota(jnp.int32, (tq, 128), 1)
            vv = v_ref[...]
            do = do_ref[...]
            rows = pl.ds(pl.multiple_of(i * tq, tq), tq)
            for hh in range(2):
                sl = slice(128 * hh, 128 * hh + 128)
                qq, kk = q_ref[:, sl], k_ref[:, sl]
                s = lax.dot_general(qq, kk, NT, preferred_element_type=F32)
                p = jnp.exp2(s - lse_ref[hh])
                if masked:
                    p = jnp.where(keep, p, 0.0)
                do_h = jnp.where((lane < VD) if hh == 0 else (lane >= VD), do, jnp.zeros_like(do))
                dp = lax.dot_general(do_h, vv, NT, preferred_element_type=F32)
                ds = (p * (dp - dl_ref[hh])).astype(BF16)
                dv_s[...] += lax.dot_general(p.astype(BF16), do_h, TN, preferred_element_type=F32)
                dk_s[:, sl] += lax.dot_general(ds, qq, TN, preferred_element_type=F32)
                dq_s[rows, sl] += jnp.dot(ds, kk, preferred_element_type=F32)

        @pl.when(i == j)
        def _():
            dk_s[...] = jnp.zeros(dk_s.shape, F32)
            dv_s[...] = jnp.zeros(dv_s.shape, F32)
            step(True)

        @pl.when(i > j)
        def _():
            step(False)

        @pl.when(i == nq - 1)
        def _():
            dk_ref[...] = dk_s[...].astype(BF16)
            dv_ref[...] = (dv_s[...] * LOG2E).astype(BF16)

        @pl.when(step_no == len(steps) - 1)
        def _():
            dq_ref[...] = dq_s[...].astype(BF16)

    qrow = lambda p, s, jt, it: (it[s], p)
    krow = lambda p, s, jt, it: (jt[s], p)
    col_spec = pl.BlockSpec((2, tq, 1), lambda p, s, jt, it: (p, it[s], 0))
    grid_spec = pltpu.PrefetchScalarGridSpec(
        num_scalar_prefetch=2, grid=(HEADS // 2, len(steps)),
        in_specs=[pl.BlockSpec((tq, 256), qrow), pl.BlockSpec((tq, 256), krow), pl.BlockSpec((tq, 128), krow),
                  pl.BlockSpec((tq, 128), qrow), col_spec, col_spec],
        out_specs=[pl.BlockSpec((t, 256), lambda p, s, jt, it: (0, p)), pl.BlockSpec((tq, 256), krow),
                   pl.BlockSpec((tq, 128), krow)],
        scratch_shapes=[pltpu.VMEM((t, 256), F32), pltpu.VMEM((tq, 256), F32), pltpu.VMEM((tq, 128), F32)])
    return pl.pallas_call(
        body, name="attn_bwd", grid_spec=grid_spec,
        out_shape=[_sds((t, QW), BF16), _sds((t, QW), BF16), _sds((t, HEADS * VD), BF16)],
        compiler_params=_cp("arbitrary", "arbitrary"))(j_of, i_of, qs, k, v, dya, lse, delta)


def _mla_prep_bwd(dqs, dk, dv, sv, p, l, depth, gb, tabs):
    za = sv["za"]
    t = za.shape[0]
    tm = min(ROW_TILE, t)

    def body(dq_ref, dk_ref, dv_ref, z_ref, gq_ref, gkv_ref, wuq_ref, wukv_ref, c_ref, s1_ref, s2_ref,
             dza_ref, dqp_ref, dkv_ref, dgq_ref, dgkv_ref):
        _acc_init(pl.program_id(0), dgq_ref, dgkv_ref)
        c, s1, s2 = c_ref[...], s1_ref[...], s2_ref[...]
        lane = lax.broadcasted_iota(jnp.int32, (tm, 128), 1)
        rope_lanes = (lane >= NOPE) & (lane < NOPE + ROPE)
        dkr = jnp.zeros((tm, 128), F32)
        for h in range(HEADS):
            sl = slice(128 * h, 128 * h + 128)
            dqp_ref[:, sl] = _rope_bwd(dq_ref[:, sl].astype(F32) * QSCALE, c, s1, s2).astype(BF16)
            dkh = dk_ref[:, sl]
            dkv_ref[:, sl] = dkh
            dkr = dkr + jnp.where(rope_lanes, dkh.astype(F32), 0.0)
        dkv_ref[:, QW:] = dv_ref[...]
        z = z_ref[...]
        dcq = lax.dot_general(dqp_ref[...], wuq_ref[...], NT, preferred_element_type=F32)
        dzq, dgq = _rms_bwd(z[:, :QR], gq_ref[...], dcq)
        dckv = lax.dot_general(dkv_ref[...], wukv_ref[...], NT, preferred_element_type=F32)
        dzkv, dgkv = _rms_bwd(z[:, QR:QR + KVR], gkv_ref[...], dckv)
        dgq_ref[...] += dgq
        dgkv_ref[...] += dgkv
        dza_ref[:, :QR] = dzq.astype(BF16)
        dza_ref[:, QR:QR + KVR] = dzkv.astype(BF16)
        dza_ref[:, QR + KVR:] = _rope_bwd(dkr, c, s1, s2).astype(BF16)

    dza, dqp, dkv, gb["q_norm_g"], gb["kv_norm_g"] = _pcall(
        body, "mla_prep_bwd", (t // tm,),
        [dqs, dk, dv, za, p["q_norm_g"], p["kv_norm_g"], p["w_uq"], p["w_ukv"], *tabs],
        [_row(tm, QW), _row(tm, QW), _row(tm, HEADS * VD), _row(tm, ZA), _lyr(l, 1, QR), _lyr(l, 1, KVR),
         _lyr(_wl(p["w_uq"], l), QR, QW), _lyr(_wl(p["w_ukv"], l), KVR, KVW), _row(tm, 128), _row(tm, 128), _row(tm, 128)],
        [_row(tm, ZA), _row(tm, QW), _row(tm, KVW), _lyr(l, 1, QR), _lyr(l, 1, KVR)],
        [_sds((t, ZA), BF16), _sds((t, QW), BF16), _sds((t, KVW), BF16), _sds((depth, 1, QR), F32),
         _sds((depth, 1, KVR), F32)], ("arbitrary",), prevs={3: gb.get("q_norm_g"), 4: gb.get("kv_norm_g")})
    return dza, dqp, dkv


def _in_proj_bwd(dza, dzb, dx2, sv, p, l, depth, gb):
    t = dx2.shape[0]
    tm = min(ROW_TILE, t)

    def body(dza_ref, dzb_ref, wa_ref, wb_ref, x_ref, dx2_ref, g_ref, dx_ref, dg_ref):
        _acc_init(pl.program_id(0), dg_ref)
        dh = (lax.dot_general(dza_ref[...], wa_ref[...], NT, preferred_element_type=F32)
              + lax.dot_general(dzb_ref[...], wb_ref[...], NT, preferred_element_type=F32))
        dx, dg = _rms_bwd(x_ref[...], g_ref[...], dh)
        dg_ref[...] += dg
        dx_ref[...] = dx2_ref[...] + dx

    dx, gb["mix_pre_g"] = _pcall(
        body, "in_proj_bwd", (t // tm,), [dza, dzb, p["w_in_a"], p["w_in_b"], sv["x"], dx2, p["mix_pre_g"]],
        [_row(tm, ZA), _row(tm, ZB), _lyr(_wl(p["w_in_a"], l), D, ZA), _lyr(_wl(p["w_in_b"], l), D, ZB), _row(tm, D), _row(tm, D), _lyr(l, 1, D)],
        [_row(tm, D), _lyr(l, 1, D)], [_sds((t, D), F32), _sds((depth, 1, D), F32)], ("arbitrary",),
        prevs={1: gb.get("mix_pre_g")})
    return dx


def _mm_tn(a, b, tn, name, l, depth, gb):
    t, k = a.shape
    n = b.shape[1]
    tt = min(ROW_TILE, t)

    def body(a_ref, b_ref, o_ref):
        _acc_init(pl.program_id(1), o_ref)
        o_ref[...] += lax.dot_general(a_ref[...], b_ref[...], TN, preferred_element_type=F32)

    gb[name] = _pcall(
        body, "d" + name, (n // tn, t // tt), [a, b],
        [pl.BlockSpec((tt, k), lambda j, s: (s, 0)), pl.BlockSpec((tt, tn), lambda j, s: (s, j))],
        pl.BlockSpec((None, k, tn), lambda j, s: (l, 0, j)), _sds((depth, k, n), F32), ("parallel", "arbitrary"),
        prevs={0: gb.get(name)})


def _dw_ffn(a, b, kind, l, depth, gb):
    t = a.shape[0]
    tt = min(ROW_TILE, t)
    nsteps = t // tt
    half = depth // 2

    down = kind == "down"

    def body(a_ref, b_ref, o_ref, acc):
        s = pl.program_id(0)
        _acc_init(s, acc)
        acc[...] += lax.dot_general(a_ref[...], b_ref[...], TN, preferred_element_type=F32)

        @pl.when(s == nsteps - 1)
        def _():
            for k in range(4):
                blk = acc[k * HP:(k + 1) * HP, :] if down else acc[:, k * HP:(k + 1) * HP]
                o_ref[k] = blk.astype(BF16)

    rows = lambda n: pl.BlockSpec((tt, n), lambda s: (s, 0))
    if down:
        in_specs, acc_shape, name = [rows(DFFP), rows(D)], (DFFP, D), "down"
        out_spec = pl.BlockSpec((4, None, None, HP, D), lambda s: (0, l // half, l % half, 0, 0))
        out_shape = _sds((4, 2, half, HP, D), BF16)
    else:
        which = 0 if kind == "gate" else 1
        in_specs, acc_shape, name = [rows(D), rows(DFFP)], (D, DFFP), "gu"
        out_spec = pl.BlockSpec((4, None, None, None, D, HP), lambda s: (0, l // half, l % half, which, 0, 0))
        out_shape = _sds((4, 2, half, 2, D, HP), BF16)
    gb[name] = _pcall(body, "dw_" + kind, (nsteps,), [a, b], in_specs, out_spec, out_shape, ("arbitrary",),
                      scratch=[pltpu.VMEM(acc_shape, F32)], prevs={0: gb.get(name)})


def _ffn_views(bufs):
    return {"w_gu": bufs[1], "w_down": bufs[2].reshape(bufs[2].shape[:2] + (HP, D))}


def _layer_fwd(x, p, l, tabs, fetch):
    h1, za, zb = _in_proj(x, p, l)
    cqn, ckvn, qs, k, v = _mla_prep(za, p, l, tabs)
    ya, lse, bufs = _attn_fwd(qs, k, v, fetch)
    if fetch:
        p = {**p, **_ffn_views(bufs)}
    mix, yb, yc = _mixer_fwd(zb, ya, p, l)
    o, x2, h2 = _out_proj(mix, x, p, l)
    a, b, s = _ffn_up(h2, p, l)
    f, x3 = _ffn_down(s, x2, p, l)
    saved = dict(x=x, h1=h1, za=za, zb=zb, cqn=cqn, ckvn=ckvn, qs=qs, k=k, v=v, ya=ya, lse=lse, mix=mix, yb=yb, yc=yc,
                 o=o, x2=x2, h2=h2, a=a, b=b, s=s, f=f)
    return x3, saved, bufs if fetch else None


def _layer_bwd(dx3, p, sv, l, depth, gb, tabs):
    t = dx3.shape[0]
    df, da, db = _ffn_down_bwd(dx3, sv, p, l, depth, gb)
    _dw_ffn(sv["s"], df, "down", l, depth, gb)
    dx2 = _ffn_up_bwd(da, db, dx3, sv, p, l, depth, gb)
    _dw_ffn(sv["h2"], da, "gate", l, depth, gb)
    _dw_ffn(sv["h2"], db, "up", l, depth, gb)
    do, dmix = _out_proj_bwd(dx2, sv, p, l, depth, gb)
    _mm_tn(sv["mix"], do, D, "w_out", l, depth, gb)
    dya, dzb, delta = _mixer_bwd(dmix, sv, p, l, depth, gb)
    dqs, dk, dv = _attn_bwd(sv["qs"], sv["k"], sv["v"], dya, sv["lse"], delta)
    dza, dqp, dkv = _mla_prep_bwd(dqs, dk, dv, sv, p, l, depth, gb, tabs)
    _mm_tn(sv["cqn"], dqp, QW, "w_uq", l, depth, gb)
    _mm_tn(sv["ckvn"], dkv, KVW, "w_ukv", l, depth, gb)
    _mm_tn(sv["h1"], dza, ZA, "w_in_a", l, depth, gb)
    _mm_tn(sv["h1"], dzb, ZB, "w_in_b", l, depth, gb)
    return _in_proj_bwd(dza, dzb, dx2, sv, p, l, depth, gb)


def _rope_tables(positions):
    inv_freq = 1.0 / (ROPE_THETA ** (jnp.arange(0, ROPE // 2, dtype=F32) / (ROPE // 2)))
    ang = positions.astype(F32)[:, None] * inv_freq
    cos, sin = jnp.cos(ang), jnp.sin(ang)
    t = positions.shape[0]
    one, zero = jnp.ones((t, 64), F32), jnp.zeros((t, 16), F32)
    c = jnp.concatenate([one, cos, cos, one[:, :32]], axis=1)
    s1 = jnp.concatenate([zero, zero, zero, zero, -sin, zero, zero, zero], axis=1)
    s2 = jnp.concatenate([zero, zero, zero, zero, zero, sin, zero, zero], axis=1)
    return c, s1, s2


def _mixer_weight_params(full):
    w_in = full["w_in"]
    depth = w_in.shape[0]
    zpad = lambda n: jnp.zeros((depth, D, n), w_in.dtype)
    kv = full["w_ukv"].reshape(depth, KVR, HEADS, NOPE + VD)
    return {
        "w_in_a": jnp.concatenate([w_in[:, :, :640], zpad(64), w_in[:, :, 640:672], zpad(32)], axis=2),
        "w_in_b": w_in[:, :, 672:],
        "w_uq": jnp.pad(full["w_uq"].reshape(depth, QR, HEADS, NOPE + ROPE),
                        ((0, 0), (0, 0), (0, 0), (0, 32))).reshape(depth, QR, QW),
        "w_ukv": jnp.concatenate([jnp.pad(kv[..., :NOPE], ((0, 0), (0, 0), (0, 0), (0, 64))).reshape(depth, KVR, QW),
                                  kv[..., NOPE:].reshape(depth, KVR, HEADS * VD)], axis=2),
        "w_out": full["w_out"], "conv_w": full["conv_w"],
    }


def _small_params(w):
    p = {"w_sp": w["w_sp"], "b_sp": jnp.repeat(jnp.swapaxes(w["b_sp"], 1, 2), 64, axis=2)}
    for n in ("mix_pre_g", "mix_post_g", "ffn_pre_g", "ffn_post_g", "q_norm_g", "kv_norm_g", "sg_ln_g", "sg_ln_b",
              "out_norm_g"):
        p[n] = w[n][:, None, :]
    return p


def _natural_grads(gb):
    depth = gb["w_in_a"].shape[0]
    ga, kv = gb["w_in_a"], gb["w_ukv"]
    out = {
        "w_in": jnp.concatenate([ga[:, :, :640], ga[:, :, 704:736], gb["w_in_b"]], axis=2),
        "w_uq": gb["w_uq"].reshape(depth, QR, HEADS, 128)[..., :NOPE + ROPE].reshape(depth, QR, HEADS * (NOPE + ROPE)),
        "w_ukv": jnp.concatenate([kv[:, :, :QW].reshape(depth, KVR, HEADS, 128)[..., :NOPE],
                                  kv[:, :, QW:].reshape(depth, KVR, HEADS, VD)], axis=3).reshape(depth, KVR, -1),
        "b_sp": jnp.swapaxes(gb["b_sp_t"].reshape(depth, CHUNK, 4, 64).sum(axis=-1), 1, 2),
    }
    for n in ("w_out", "w_sp", "conv_w"):
        out[n] = gb[n]
    for n in ("mix_pre_g", "mix_post_g", "ffn_pre_g", "ffn_post_g", "q_norm_g", "kv_norm_g", "sg_ln_g", "sg_ln_b",
              "out_norm_g"):
        out[n] = gb[n][:, 0, :]
    return out


def _local_step(x, positions, target, small, mine, bufs, shard_shapes, fetch=True):
    depth = small["w_sp"].shape[0]
    tabs = _rope_tables(positions)
    ps = _small_params(small)
    saved, mixer_w = [], []
    for l in range(depth):
        mixer_w.append(_mixer_weight_params(_unpack_weights(bufs[0], l, shard_shapes)))
        p = {**ps, **mixer_w[l], **_ffn_views(bufs)}
        x, sv, fetched = _layer_fwd(x, p, l, tabs, (mine, bufs, l + 1) if fetch and l + 1 < depth else None)
        bufs = fetched or bufs
        saved.append(sv)
    dx, acc = _loss_head(x, target)
    loss = (0.5 / D) * jnp.sum(acc)
    gb = {}
    for l in reversed(range(depth)):
        dx = _layer_bwd(dx, {**ps, **mixer_w[l], **_ffn_views(bufs)}, saved[l], l, depth, gb, tabs)
    return loss, dx, _natural_grads(gb), gb["gu"], gb["down"]


def _place():
    x, y, c = lax.axis_index("x"), lax.axis_index("y"), lax.axis_index("c")
    chips = [(1 - x, y), (x, 1 - y), (1 - x, 1 - y)]
    return x, y, c, 2 * x + y, chips


def _remote(src, dst, send_sem, recv_sem, to):
    return pltpu.make_async_remote_copy(src_ref=src, dst_ref=dst, send_sem=send_sem, recv_sem=recv_sem, device_id=to,
                                        device_id_type=MESH_ID)


def _gather_ops(mine_refs, out_refs, sems, l):
    send_sems, recv_sems, fsend_sems, frecv_sems = sems
    x, y, c, k, chips = _place()
    sib = (x, y, 1 - c)
    pairs = [(b, n) for n in range(3) for b in range(len(mine_refs))]

    def slot(n):
        return 2 * chips[n][0] + chips[n][1]

    def ici(b, n, dst_chip):
        return _remote(mine_refs[b].at[l, c], out_refs[b].at[dst_chip, l, c], send_sems.at[b, n], recv_sems.at[b, n],
                       (*chips[n], c))

    def d2d(b, n, half):
        piece = out_refs[b].at[slot(n), l, half]
        return _remote(piece, piece, fsend_sems.at[b, n], frecv_sems.at[b, n], sib)

    def start():
        for b, n in pairs:
            ici(b, n, k).start()

    def hand_over():
        for b, n in pairs:
            ici(b, n, slot(n)).wait_recv()
            d2d(b, n, c).start()

    def drain():
        for b, n in pairs:
            d2d(b, n, 1 - c).wait_recv()
        for b, n in pairs:
            ici(b, n, k).wait_send()
            d2d(b, n, c).wait_send()

    return start, hand_over, drain


def _gather_first_layer(mine):
    nb = len(mine)

    def body(*refs):
        start, hand_over, drain = _gather_ops(refs[:nb], refs[nb:2 * nb], refs[2 * nb:], 0)
        start()
        hand_over()
        drain()

    return pl.pallas_call(
        body, name="gather_first_layer", in_specs=[ANY] * nb, out_specs=[ANY] * nb,
        out_shape=[_sds((4,) + a.shape, a.dtype) for a in mine],
        scratch_shapes=[pltpu.SemaphoreType.DMA((nb, 3))] * 4)(*mine)


def _swap_halves(bigs, small):
    nb = len(bigs)

    def body(*refs):
        big_refs, small_ref = refs[:nb], refs[nb]
        rbig_refs, rsmall_ref = refs[nb + 1:2 * nb + 1], refs[2 * nb + 1]
        send_sems, recv_sems = refs[2 * nb + 2:]
        x, y, c, _, _ = _place()
        sib = (x, y, 1 - c)
        cps = [_remote(big_refs[b].at[:, 1 - c], rbig_refs[b], send_sems.at[b], recv_sems.at[b], sib) for b in range(nb)]
        cps.append(_remote(small_ref, rsmall_ref, send_sems.at[nb], recv_sems.at[nb], sib))
        for cp in cps:
            cp.start()
        for cp in cps:
            cp.wait()

    return pl.pallas_call(
        body, name="swap_halves", in_specs=[ANY] * (nb + 1), out_specs=[ANY] * (nb + 1),
        out_shape=[_sds((4,) + a.shape[2:], a.dtype) for a in bigs] + [_sds(small.shape, small.dtype)],
        scratch_shapes=[pltpu.SemaphoreType.DMA((nb + 1,))] * 2)(*bigs, small)


def _sum_tile(r):
    return max(cand for cand in range(16, 641, 16) if r % cand == 0)


def _pair_sum(big, rbig, c):
    _, _, r, w = big.shape
    tr = _sum_tile(r)

    def body(c_ref, big_ref, rbig_ref, p_ref):
        p_ref[...] = (big_ref[...].astype(F32) + rbig_ref[...].astype(F32)).astype(BF16)

    grid_spec = pltpu.PrefetchScalarGridSpec(
        num_scalar_prefetch=1, grid=(4, r // tr),
        in_specs=[pl.BlockSpec((None, None, tr, w), lambda j, i, cr: (j, cr[0], i, 0)),
                  pl.BlockSpec((None, tr, w), lambda j, i, cr: (j, i, 0))],
        out_specs=pl.BlockSpec((None, tr, w), lambda j, i, cr: (j, i, 0)))
    return pl.pallas_call(body, name="pair_sum", grid_spec=grid_spec, out_shape=_sds((4, r, w), BF16),
                          compiler_params=_cp("parallel", "parallel"))(c, big, rbig)


def _small_sum(parts):
    n, ns, _ = parts.shape

    def body(p_ref, o_ref):
        s = p_ref[0]
        for j in range(1, n):
            s = s + p_ref[j]
        o_ref[...] = s

    return pl.pallas_call(body, name="small_sum", out_shape=_sds((ns, 128), F32))(parts)


def _chip_exchange(ps, small):
    nb = len(ps)
    ns = small.shape[0]

    def body(*refs):
        p_refs, small_ref = refs[:nb], refs[nb]
        rb_refs, rs_ref = refs[nb + 1:2 * nb + 1], refs[2 * nb + 1]
        send_sems, recv_sems, local_sem = refs[2 * nb + 2:]
        x, y, c, k, chips = _place()
        loc = pltpu.make_async_copy(small_ref, rs_ref.at[k], local_sem)
        loc.start()
        sends = []
        for n, (cx, cy) in enumerate(chips):
            to = (cx, cy, c)
            for b in range(nb):
                sends.append(_remote(p_refs[b].at[2 * cx + cy], rb_refs[b].at[n], send_sems.at[b, n], recv_sems.at[b, n], to))
            sends.append(_remote(small_ref, rs_ref.at[k], send_sems.at[nb, n], recv_sems.at[nb, n], to))
        for cp in sends:
            cp.start()
        for n, (cx, cy) in enumerate(chips):
            to = (cx, cy, c)
            for b in range(nb):
                _remote(p_refs[b].at[k], rb_refs[b].at[n], send_sems.at[b, n], recv_sems.at[b, n], to).wait_recv()
            _remote(small_ref, rs_ref.at[2 * cx + cy], send_sems.at[nb, n], recv_sems.at[nb, n], to).wait_recv()
        for cp in sends:
            cp.wait_send()
        loc.wait()

    return pl.pallas_call(
        body, name="chip_exchange", in_specs=[ANY] * (nb + 1), out_specs=[ANY] * (nb + 1),
        out_shape=[_sds((3,) + a.shape[1:], a.dtype) for a in ps] + [_sds((4, ns, 128), small.dtype)],
        scratch_shapes=[pltpu.SemaphoreType.DMA((nb + 1, 3))] * 2 + [pltpu.SemaphoreType.DMA(())])(*ps, small)


def _chip_sum(p, rb, chip):
    _, r, w = p.shape
    tr = _sum_tile(r)

    def body(k_ref, p_ref, rb_ref, o_ref):
        acc = p_ref[...].astype(F32)
        for j in range(3):
            acc = acc + rb_ref[j].astype(F32)
        o_ref[...] = acc

    grid_spec = pltpu.PrefetchScalarGridSpec(
        num_scalar_prefetch=1, grid=(r // tr,),
        in_specs=[pl.BlockSpec((None, tr, w), lambda i, kr: (kr[0], i, 0)), pl.BlockSpec((3, tr, w), lambda i, kr: (0, i, 0))],
        out_specs=pl.BlockSpec((tr, w), lambda i, kr: (i, 0)))
    return pl.pallas_call(body, name="chip_sum", grid_spec=grid_spec, out_shape=_sds((r, w), F32),
                          compiler_params=_cp("parallel"))(chip, p, rb)


def _send_to_sibling(reds):
    nb = len(reds)

    def body(*refs):
        red_refs, out_refs = refs[:nb], refs[nb:2 * nb]
        send_sems, recv_sems = refs[2 * nb:]
        x, y, c, _, _ = _place()
        cps = [_remote(red_refs[b], out_refs[b], send_sems.at[b], recv_sems.at[b], (x, y, 1 - c)) for b in range(nb)]
        for cp in cps:
            cp.start()
        for cp in cps:
            cp.wait()

    return pl.pallas_call(
        body, name="send_to_sibling", in_specs=[ANY] * nb, out_specs=[ANY] * nb,
        out_shape=[_sds(a.shape, a.dtype) for a in reds], scratch_shapes=[pltpu.SemaphoreType.DMA((nb,))] * 2)(*reds)


def _adam_math(w, g, m, v):
    nm = ADAM_B1 * m + (1.0 - ADAM_B1) * g
    nv = ADAM_B2 * v + (1.0 - ADAM_B2) * (g * g)
    m_hat = nm / (1.0 - ADAM_B1 ** ADAM_STEP)
    v_hat = nv / (1.0 - ADAM_B2 ** ADAM_STEP)
    return -ADAM_LR * (m_hat / (jnp.sqrt(v_hat) + ADAM_EPS) + ADAM_WD * w), nm, nv


def _adamw_shard(w, m, v, own, other, c, name, pick=None):
    depth, r, n = w.shape
    half = depth // 2
    tr = max(cand for cand in range(8, min(r, 256) + 1, 8) if r % cand == 0)
    npad = own.shape[-1]

    def body(c_ref, w_ref, m_ref, v_ref, own_ref, oth_ref, g_ref, d_ref, nm_ref, nv_ref):
        mine = (pl.program_id(0) // half) == c_ref[0]
        g = jnp.where(mine, own_ref[...], oth_ref[...])[:, :n]
        g_ref[...] = g
        d_ref[...], nm_ref[...], nv_ref[...] = _adam_math(w_ref[...], g, m_ref[...], v_ref[...])

    blk = pl.BlockSpec((None, tr, n), lambda l, i, cr: (l, i, 0))
    if pick is None:
        gblk = pl.BlockSpec((None, tr, npad), lambda l, i, cr: (l % half, i, 0))
    else:
        gblk = pl.BlockSpec((None, None, tr, npad), lambda l, i, cr: (l % half, pick, i, 0))
    grid_spec = pltpu.PrefetchScalarGridSpec(num_scalar_prefetch=1, grid=(depth, r // tr), in_specs=[blk] * 3 + [gblk] * 2,
                                             out_specs=[blk] * 4)
    return pl.pallas_call(body, name=name, grid_spec=grid_spec, out_shape=[_sds(w.shape, F32)] * 4,
                          compiler_params=_cp("parallel", "parallel"))(c, w, m, v, own, other)


def _pad_ffn_shards(w_gate, w_up, w_down):
    depth = w_gate.shape[0]
    tm = 256
    hr = HP // 2

    def gu_body(g_ref, u_ref, o_ref):
        o_ref[...] = jnp.zeros(o_ref.shape, BF16)
        o_ref[0, :, 0:HS] = g_ref[...].astype(BF16)
        o_ref[1, :, 0:HS] = u_ref[...].astype(BF16)

    blk = pl.BlockSpec((None, tm, HS), lambda l, i: (l, i, 0))
    gu = pl.pallas_call(
        gu_body, name="pad_gate_up", grid=(depth, D // tm), in_specs=[blk, blk],
        out_specs=pl.BlockSpec((None, 2, tm, HP), lambda l, i: (l, 0, i, 0)),
        out_shape=_sds((depth, 2, D, HP), BF16), compiler_params=_cp("parallel", "parallel"))(w_gate, w_up)

    def down_body(w_ref, o_ref):
        o_ref[0] = w_ref[0:hr, :].astype(BF16)
        o_ref[1, 0:HS - hr, :] = w_ref[hr:HS, :].astype(BF16)
        o_ref[1, HS - hr:hr, :] = jnp.zeros((HP - HS, D), BF16)

    down = pl.pallas_call(
        down_body, name="pad_down", grid=(depth,), in_specs=[pl.BlockSpec((None, HS, D), lambda l: (l, 0, 0))],
        out_specs=pl.BlockSpec((None, 2, hr, D), lambda l: (l, 0, 0, 0)),
        out_shape=_sds((depth, 2, hr, D), BF16), compiler_params=_cp("parallel"))(w_down)
    return gu, down


def _adamw_small(w, g, m, v):
    r = w.shape[0]
    tr = max(cand for cand in range(8, 513, 8) if r % cand == 0)

    def body(w_ref, g_ref, m_ref, v_ref, d_ref, nm_ref, nv_ref):
        d_ref[...], nm_ref[...], nv_ref[...] = _adam_math(w_ref[...], g_ref[...], m_ref[...], v_ref[...])

    blk = pl.BlockSpec((tr, 128), lambda i: (i, 0))
    return pl.pallas_call(body, name="adamw_small", grid=(r // tr,), in_specs=[blk] * 4, out_specs=[blk] * 3,
                          out_shape=[_sds(w.shape, F32)] * 3, compiler_params=_cp("parallel"))(w, g, m, v)


def _to_pack(a, name):
    depth = a.shape[0]
    if name in ROW_SHARDED:
        return jnp.swapaxes(a.reshape(depth, 4, -1, D), 0, 1)
    return jnp.transpose(a.reshape(depth, a.shape[1], 4, a.shape[2] // 4), (2, 0, 1, 3)).reshape(4, depth, -1, D)


def _pack_rows(parts, lead, dtype, tail=None):
    pieces, at = [], 0
    for n, off, rows in PACK:
        if off > at:
            pieces.append(jnp.zeros(lead + (off - at, D), dtype))
        pieces.append(parts[n].astype(dtype))
        at = off + rows
    if tail is not None:
        pieces.append(tail)
        at += tail.shape[-2]
    pieces.append(jnp.zeros(lead + (PACK_ROWS - at, D), dtype))
    return jnp.concatenate(pieces, axis=len(lead))


def _pack_weight_shards(sh):
    depth = sh["w_in"].shape[0]
    parts = {n: sh[n].reshape(depth, rows, D) for n, _, rows in PACK}
    conv = lax.bitcast_convert_type(sh["conv_w"].reshape(depth, 3 * 64), BF16).reshape(depth, 1, 384)
    flat = _pack_rows(parts, (depth,), BF16, tail=jnp.pad(conv, ((0, 0), (0, 0), (0, D - 384))))
    return flat.reshape(depth, 2, PACK_ROWS // 2, D)


def _unpack_weights(gathered, l, shard_shapes):
    depth = 1
    flat = gathered[:, l].reshape(4, 1, PACK_ROWS, D)
    full = {}
    for n, off, rows in PACK:
        shp = shard_shapes[n][1:]
        piece = flat[:, :, off:off + rows, :].reshape((4, depth) + shp)
        if n in ROW_SHARDED:
            full[n] = jnp.transpose(piece, (1, 0, 2, 3)).reshape(depth, 4 * shp[0], shp[1])
        else:
            full[n] = jnp.transpose(piece, (1, 2, 0, 3)).reshape(depth, shp[0], 4 * shp[1])
    conv = lax.bitcast_convert_type(flat[:, :, CONV_ROW, :384].reshape(4, depth, 192, 2), F32)
    full["conv_w"] = jnp.transpose(conv.reshape(4, depth, 3, 64), (1, 2, 0, 3)).reshape(depth, 3, CVW)
    return full


def _pack_grad_shards(g):
    depth = g["w_in"].shape[0]
    flat = _pack_rows({n: _to_pack(g[n], n) for n, _, _ in PACK}, (4, depth), BF16)
    return flat.reshape(4, 2, depth // 2 * PACK_ROWS, D)


def _pack_small(arrs, names_shapes, depth):
    flat = jnp.concatenate([arrs[n].reshape(depth, -1) for n, _ in names_shapes], axis=1).reshape(-1)
    rows = -(-flat.shape[0] // 1024) * 8
    return jnp.pad(flat, (0, rows * 128 - flat.shape[0])).reshape(rows, 128)


def _unpack_small(packed, names_shapes, depth):
    per_layer = sum(math.prod(s) for _, s in names_shapes)
    flat = packed.reshape(-1)[:depth * per_layer].reshape(depth, per_layer)
    out, off = {}, 0
    for n, s in names_shapes:
        size = math.prod(s)
        out[n] = flat[:, off:off + size].reshape((depth,) + s)
        off += size
    return out


def kernel(x, positions, mix_pre_g, mix_post_g, ffn_pre_g, ffn_post_g, w_in, q_norm_g, w_uq, kv_norm_g, w_ukv, sg_ln_g, sg_ln_b, w_sp, b_sp, conv_w, out_norm_g, w_out, w_gate, w_up, w_down, loss_target, m_mix_pre_g, m_mix_post_g, m_ffn_pre_g, m_ffn_post_g, m_w_in, m_q_norm_g, m_w_uq, m_kv_norm_g, m_w_ukv, m_sg_ln_g, m_sg_ln_b, m_w_sp, m_b_sp, m_conv_w, m_out_norm_g, m_w_out, m_w_gate, m_w_up, m_w_down, v_mix_pre_g, v_mix_post_g, v_ffn_pre_g, v_ffn_post_g, v_w_in, v_q_norm_g, v_w_uq, v_kv_norm_g, v_w_ukv, v_sg_ln_g, v_sg_ln_b, v_w_sp, v_b_sp, v_conv_w, v_out_norm_g, v_w_out, v_w_gate, v_w_up, v_w_down):
    w = dict(mix_pre_g=mix_pre_g, mix_post_g=mix_post_g, ffn_pre_g=ffn_pre_g, ffn_post_g=ffn_post_g, w_in=w_in,
             q_norm_g=q_norm_g, w_uq=w_uq, kv_norm_g=kv_norm_g, w_ukv=w_ukv, sg_ln_g=sg_ln_g, sg_ln_b=sg_ln_b, w_sp=w_sp,
             b_sp=b_sp, conv_w=conv_w, out_norm_g=out_norm_g, w_out=w_out, w_gate=w_gate, w_up=w_up, w_down=w_down)
    m = dict(mix_pre_g=m_mix_pre_g, mix_post_g=m_mix_post_g, ffn_pre_g=m_ffn_pre_g, ffn_post_g=m_ffn_post_g, w_in=m_w_in,
             q_norm_g=m_q_norm_g, w_uq=m_w_uq, kv_norm_g=m_kv_norm_g, w_ukv=m_w_ukv, sg_ln_g=m_sg_ln_g, sg_ln_b=m_sg_ln_b,
             w_sp=m_w_sp, b_sp=m_b_sp, conv_w=m_conv_w, out_norm_g=m_out_norm_g, w_out=m_w_out, w_gate=m_w_gate,
             w_up=m_w_up, w_down=m_w_down)
    v = dict(mix_pre_g=v_mix_pre_g, mix_post_g=v_mix_post_g, ffn_pre_g=v_ffn_pre_g, ffn_post_g=v_ffn_post_g, w_in=v_w_in,
             q_norm_g=v_q_norm_g, w_uq=v_w_uq, kv_norm_g=v_kv_norm_g, w_ukv=v_w_ukv, sg_ln_g=v_sg_ln_g, sg_ln_b=v_sg_ln_b,
             w_sp=v_w_sp, b_sp=v_b_sp, conv_w=v_conv_w, out_norm_g=v_out_norm_g, w_out=v_w_out, w_gate=v_w_gate,
             w_up=v_w_up, w_down=v_w_down)
    depth = w_in.shape[0]
    c = lax.axis_index("c").astype(jnp.int32).reshape(1)
    chip = (2 * lax.axis_index("x") + lax.axis_index("y")).astype(jnp.int32)

    half = depth // 2

    mine = [_pack_weight_shards(w), *_pad_ffn_shards(w_gate, w_up, w_down)]
    bufs = [lax.dynamic_update_slice(g, a[None], (chip,) + (0,) * a.ndim)
            for g, a in zip(_gather_first_layer(mine), mine)]

    loss, dx, grads, g_gu, g_down = _local_step(x[0], positions[0], loss_target[0], w, mine, bufs,
                                                {n: w[n].shape for n, _, _ in PACK})
    loss = lax.psum(loss, ("x", "y", "c"))

    small = _pack_small(grads, SMALL, depth)
    bigs = [_pack_grad_shards(grads), g_gu.reshape(4, 2, half * 2 * D, HP), g_down.reshape(4, 2, half * HP, D)]
    *rbigs, rsmall = _swap_halves(bigs, small)
    ps = [_pair_sum(a, r, c) for a, r in zip(bigs, rbigs)]
    *rbs, rs = _chip_exchange(ps, _small_sum(jnp.stack([small, rsmall])))
    own = [_chip_sum(p, rb, chip.reshape(1)) for p, rb in zip(ps, rbs)]
    other = _send_to_sibling(own)
    g_small = _unpack_small(_small_sum(rs), SMALL, depth)
    g_small["conv_w"] = lax.dynamic_slice_in_dim(g_small["conv_w"], chip * 64, 64, axis=2)

    gw, delta, new_m, new_v = dict(g_small), {}, {}, {}

    def adam(n, own_g, other_g, pick=None):
        shp = w[n].shape
        outs = _adamw_shard(w[n], m[n], v[n], own_g, other_g, c, "adamw_" + n, pick)
        gw[n], delta[n], new_m[n], new_v[n] = outs

    for n, off, rows in PACK:
        nat = lambda a: a.reshape(half, PACK_ROWS, D)[:, off:off + rows, :].reshape((half,) + w[n].shape[1:])
        adam(n, nat(own[0]), nat(other[0]))
    adam("w_gate", own[1].reshape(half, 2, D, HP), other[1].reshape(half, 2, D, HP), 0)
    adam("w_up", own[1].reshape(half, 2, D, HP), other[1].reshape(half, 2, D, HP), 1)
    adam("w_down", own[2].reshape(half, HP, D), other[2].reshape(half, HP, D))
    small_local = tuple((n, w[n].shape[1:]) for n, _ in SMALL)
    d_, m_, v_ = _adamw_small(_pack_small(w, small_local, depth), _pack_small(gw, small_local, depth),
                              _pack_small(m, small_local, depth), _pack_small(v, small_local, depth))
    delta.update(_unpack_small(d_, small_local, depth))
    new_m.update(_unpack_small(m_, small_local, depth))
    new_v.update(_unpack_small(v_, small_local, depth))

    return (loss, dx[None], *[gw[n] for n in WEIGHTS], *[delta[n] for n in WEIGHTS], *[new_m[n] for n in WEIGHTS],
            *[new_v[n] for n in WEIGHTS])
```

```python
import math

import jax
import jax.numpy as jnp
from jax import lax
from jax.experimental import pallas as pl
from jax.experimental.pallas import tpu as pltpu

F32 = jnp.float32
BF16 = jnp.bfloat16

D = 1024
HEADS = 8
NOPE = 64
ROPE = 32
VD = 64
QR = 384
KVR = 256
SGW = 256
CVW = 256
CHUNK = 128
DFF = 2816
EPS = 1e-6
ROPE_THETA = 10000.0
LOG2E = 1.4426950408889634
LN2 = 0.6931471805599453
QSCALE = (NOPE + ROPE) ** -0.5 * LOG2E
ZA = 768
ZB = 1280
QW = HEADS * 128
KVW = HEADS * 128 + HEADS * VD
NEG = -1e30
GC0 = 0.7978845608028654
GC1 = 0.044715

ADAM_LR = 0.001
ADAM_B1 = 0.9
ADAM_B2 = 0.999
ADAM_EPS = 1e-08
ADAM_WD = 0.01
ADAM_STEP = 10

V7X_VMEM_LIMIT = 52 * 1024 * 1024
ROW_TILE = 512
ATT_TILE = 512

NT = (((1,), (1,)), ((), ()))
TN = (((0,), (0,)), ((), ()))

HS = DFF // 4
HP = 768
DFFP = 4 * HP

PACK = (("w_in", 0, 488), ("w_out", 512, 256), ("w_ukv", 768, 64), ("w_uq", 832, 72))
CONV_ROW = 904
PACK_ROWS = 928
ROW_SHARDED = ("w_out",)
SMALL = (("mix_pre_g", (D,)), ("mix_post_g", (D,)), ("ffn_pre_g", (D,)), ("ffn_post_g", (D,)), ("q_norm_g", (QR,)),
         ("kv_norm_g", (KVR,)), ("sg_ln_g", (SGW,)), ("sg_ln_b", (SGW,)), ("w_sp", (4, CHUNK, CHUNK)), ("b_sp", (4, CHUNK)),
         ("conv_w", (3, CVW)), ("out_norm_g", (D,)))
WEIGHTS = ["mix_pre_g", "mix_post_g", "ffn_pre_g", "ffn_post_g", "w_in", "q_norm_g", "w_uq", "kv_norm_g", "w_ukv", "sg_ln_g",
           "sg_ln_b", "w_sp", "b_sp", "conv_w", "out_norm_g", "w_out", "w_gate", "w_up", "w_down"]

MESH_ID = pl.DeviceIdType.MESH
ANY = pl.BlockSpec(memory_space=pl.ANY)


def _cp(*sem):
    return pltpu.CompilerParams(dimension_semantics=sem, vmem_limit_bytes=V7X_VMEM_LIMIT)


def _sds(shape, dtype):
    return jax.ShapeDtypeStruct(shape, dtype)


def _row(tm, n):
    return pl.BlockSpec((tm, n), lambda i: (i, 0))


def _lyr(l, *shape):
    return pl.BlockSpec((None,) + shape, lambda *_: (l,) + (0,) * len(shape))


def _wl(a, l):
    return 0 if a.shape[0] == 1 else l


def _pcall(body, name, grid, ins, in_specs, out_specs, out_shape, sem, scratch=(), prevs=None):
    prevs = {k: v for k, v in (prevs or {}).items() if v is not None}
    order = sorted(prevs)
    n_in = len(ins)

    def wrapped(*refs):
        return body(*refs[:n_in], *refs[n_in + len(order):])

    return pl.pallas_call(
        wrapped, name=name, grid=grid, in_specs=list(in_specs) + [ANY] * len(order), out_specs=out_specs,
        out_shape=out_shape, scratch_shapes=list(scratch),
        input_output_aliases={n_in + i: k for i, k in enumerate(order)},
        compiler_params=_cp(*sem))(*ins, *[prevs[k] for k in order])


def _rms(x, g):
    r = lax.rsqrt(jnp.mean(x * x, axis=-1, keepdims=True) + EPS)
    return x * r * g


def _rms_bwd(x, g, dy):
    r = lax.rsqrt(jnp.mean(x * x, axis=-1, keepdims=True) + EPS)
    xh = x * r
    dg = jnp.sum(dy * xh, axis=0, keepdims=True)
    dxh = dy * g
    dx = r * (dxh - xh * jnp.mean(dxh * xh, axis=-1, keepdims=True))
    return dx, dg


def _gelu(x):
    return 0.5 * x * (1.0 + jnp.tanh(GC0 * (x + GC1 * x * x * x)))


def _gelu_grad(x):
    t = jnp.tanh(GC0 * (x + GC1 * x * x * x))
    return 0.5 * (1.0 + t) + 0.5 * x * (1.0 - t * t) * GC0 * (1.0 + 3.0 * GC1 * x * x)


def _rope(xb, c, s1, s2):
    return xb * c + pltpu.roll(xb, 112, 1) * s1 + pltpu.roll(xb, 16, 1) * s2


def _rope_bwd(dy, c, s1, s2):
    return dy * c + pltpu.roll(dy * s1, 16, 1) + pltpu.roll(dy * s2, 112, 1)


def _group_masks(shape):
    lane = lax.broadcasted_iota(jnp.int32, shape, 1)
    return [(lane >= 64 * g) & (lane < 64 * g + 64) for g in range(shape[1] // 64)]


def _group_mean(v, masks):
    out = jnp.zeros_like(v)
    for m in masks:
        s = jnp.sum(jnp.where(m, v, 0.0), axis=-1, keepdims=True) * (1.0 / 64.0)
        out = jnp.where(m, s, out)
    return out


def _pick_row(blk, idx):
    row = lax.broadcasted_iota(jnp.int32, blk.shape, 0)
    return jnp.sum(jnp.where(row == idx, blk, 0.0), axis=0, keepdims=True)


def _shift_down(y, k, first_rows):
    out = pltpu.roll(y, k, 0)
    row = lax.broadcasted_iota(jnp.int32, y.shape, 0)
    for idx in range(k):
        out = jnp.where(row == idx, first_rows[idx], out)
    return out


def _shift_up(y, k, last_rows):
    n = y.shape[0]
    out = pltpu.roll(y, n - k, 0)
    row = lax.broadcasted_iota(jnp.int32, y.shape, 0)
    for idx in range(k):
        out = jnp.where(row == n - k + idx, last_rows[idx], out)
    return out


def _tril_mask():
    r = lax.broadcasted_iota(jnp.int32, (CHUNK, CHUNK), 0)
    c = lax.broadcasted_iota(jnp.int32, (CHUNK, CHUNK), 1)
    return r >= c


def _sgu_forward(zu, zv, g_ln, b_ln, wc_bf, bsp, masks, cmasks):
    u = _gelu(zu)
    vv = _gelu(zv)
    mu = _group_mean(vv, masks)
    dv = vv - mu
    rs = lax.rsqrt(_group_mean(dv * dv, masks) + EPS)
    xh = dv * rs
    vn = xh * g_ln + b_ln
    chunks = []
    for ci in range(zu.shape[0] // CHUNK):
        vc = vn[ci * CHUNK:(ci + 1) * CHUNK, :]
        acc = bsp
        for g in range(4):
            acc = acc + jnp.dot(wc_bf[g], jnp.where(cmasks[g], vc, 0.0).astype(BF16), preferred_element_type=F32)
        chunks.append(acc)
    mixed = jnp.concatenate(chunks, axis=0) if len(chunks) > 1 else chunks[0]
    return u, vv, xh, rs, vn, mixed


def _conv_forward(gc, hh, prev_gc, prev_hh, first_tile, cw):
    yv = gc * hh
    prev = jnp.where(first_tile, 0.0, prev_gc * prev_hh)
    p6, p7 = _pick_row(prev, 6), _pick_row(prev, 7)
    sh1 = _shift_down(yv, 1, [p7])
    sh2 = _shift_down(yv, 2, [p6, p7])
    conv = sh2 * cw[0:1, :] + sh1 * cw[1:2, :] + yv * cw[2:3, :]
    return yv, sh1, sh2, conv


def _acc_init(step, *refs):
    @pl.when(step == 0)
    def _():
        for r in refs:
            r[...] = jnp.zeros(r.shape, r.dtype)


def _in_proj(x, p, l):
    t = x.shape[0]
    tm = min(ROW_TILE, t)

    def body(x_ref, g_ref, wa_ref, wb_ref, h_ref, za_ref, zb_ref):
        h = _rms(x_ref[...], g_ref[...]).astype(BF16)
        h_ref[...] = h
        za_ref[...] = jnp.dot(h, wa_ref[...], preferred_element_type=F32)
        zb_ref[...] = jnp.dot(h, wb_ref[...], preferred_element_type=F32)

    return _pcall(
        body, "in_proj", (t // tm,), [x, p["mix_pre_g"], p["w_in_a"], p["w_in_b"]],
        [_row(tm, D), _lyr(l, 1, D), _lyr(_wl(p["w_in_a"], l), D, ZA), _lyr(_wl(p["w_in_b"], l), D, ZB)],
        [_row(tm, D), _row(tm, ZA), _row(tm, ZB)],
        [_sds((t, D), BF16), _sds((t, ZA), F32), _sds((t, ZB), F32)], ("parallel",))


def _mla_prep(za, p, l, tabs):
    t = za.shape[0]
    tm = min(ROW_TILE, t)

    def body(z_ref, gq_ref, gkv_ref, wuq_ref, wukv_ref, c_ref, s1_ref, s2_ref, cq_ref, ckv_ref, q_ref, k_ref, v_ref):
        z = z_ref[...]
        cq = _rms(z[:, :QR], gq_ref[...]).astype(BF16)
        ckv = _rms(z[:, QR:QR + KVR], gkv_ref[...]).astype(BF16)
        cq_ref[...] = cq
        ckv_ref[...] = ckv
        c, s1, s2 = c_ref[...], s1_ref[...], s2_ref[...]
        kr = _rope(z[:, QR + KVR:], c, s1, s2)
        q = jnp.dot(cq, wuq_ref[...], preferred_element_type=F32)
        kv = jnp.dot(ckv, wukv_ref[...], preferred_element_type=F32)
        for h in range(HEADS):
            sl = slice(128 * h, 128 * h + 128)
            q_ref[:, sl] = (_rope(q[:, sl], c, s1, s2) * QSCALE).astype(BF16)
            k_ref[:, sl] = (kv[:, sl] + kr).astype(BF16)
        v_ref[...] = kv[:, QW:].astype(BF16)

    return _pcall(
        body, "mla_prep", (t // tm,), [za, p["q_norm_g"], p["kv_norm_g"], p["w_uq"], p["w_ukv"], *tabs],
        [_row(tm, ZA), _lyr(l, 1, QR), _lyr(l, 1, KVR), _lyr(_wl(p["w_uq"], l), QR, QW), _lyr(_wl(p["w_ukv"], l), KVR, KVW),
         _row(tm, 128), _row(tm, 128), _row(tm, 128)],
        [_row(tm, QR), _row(tm, KVR), _row(tm, QW), _row(tm, QW), _row(tm, HEADS * VD)],
        [_sds((t, QR), BF16), _sds((t, KVR), BF16), _sds((t, QW), BF16), _sds((t, QW), BF16),
         _sds((t, HEADS * VD), BF16)], ("parallel",))


def _att_tile(t):
    return min(ATT_TILE, max(t // 2, 128))


def _causal_keep(tq, i, j):
    row = lax.broadcasted_iota(jnp.int32, (tq, tq), 0) + i * tq
    col = lax.broadcasted_iota(jnp.int32, (tq, tq), 1) + j * tq
    return col <= row


def _attn_fwd(qs, k, v, fetch=None):
    t = qs.shape[0]
    tq = _att_tile(t)
    nq = t // tq
    rep = tq // 128
    mine, bufs, fetch_layer = fetch if fetch else ((), (), None)
    nb = len(mine)

    steps = [(i, j) for i in range(nq) for j in range(i + 1)]
    i_of = jnp.asarray([s[0] for s in steps], jnp.int32)
    j_of = jnp.asarray([s[1] for s in steps], jnp.int32)

    def body(i_ref, j_ref, q_ref, k_ref, v_ref, *refs):
        o_ref, lse_ref = refs[2 * nb:2 * nb + 2]
        m_s, l_s, acc_s = refs[3 * nb + 2:3 * nb + 5]
        step_no = pl.program_id(1)
        i, j = i_ref[step_no], j_ref[step_no]
        if fetch:
            start, hand_over, drain = _gather_ops(refs[:nb], refs[2 * nb + 2:3 * nb + 2], refs[3 * nb + 5:], fetch_layer)
            pr = pl.program_id(0)
            pl.when((pr == 0) & (step_no == 0))(start)
            pl.when((pr == HEADS // 2 - 1) & (step_no == 0))(hand_over)
            pl.when((pr == HEADS // 2 - 1) & (step_no == len(steps) - 1))(drain)

        @pl.when(j == 0)
        def _():
            m_s[...] = jnp.full(m_s.shape, NEG, F32)
            l_s[...] = jnp.zeros(l_s.shape, F32)
            acc_s[...] = jnp.zeros(acc_s.shape, F32)

        def step(masked):
            vv = v_ref[...]
            keep = _causal_keep(tq, i, j) if masked else None
            for hh in range(2):
                sl = slice(128 * hh, 128 * hh + 128)
                s = lax.dot_general(q_ref[:, sl], k_ref[:, sl], NT, preferred_element_type=F32)
                if masked:
                    s = jnp.where(keep, s, NEG)
                m_old = m_s[hh]
                m_new = jnp.maximum(m_old, jnp.max(s, axis=-1, keepdims=True))
                alpha = jnp.exp2(m_old - m_new)
                p = jnp.exp2(s - jnp.tile(m_new, (1, rep)))
                l_s[hh] = alpha * l_s[hh] + jnp.sum(p, axis=-1, keepdims=True)
                acc_s[hh] = alpha * acc_s[hh] + jnp.dot(p.astype(BF16), vv, preferred_element_type=F32)
                m_s[hh] = m_new

        @pl.when(j < i)
        def _():
            step(False)

        @pl.when(j == i)
        def _():
            step(True)
            lane = lax.broadcasted_iota(jnp.int32, (tq, 128), 1)
            o_ref[...] = jnp.where(lane < VD, acc_s[0] / l_s[0], acc_s[1] / l_s[1])
            for hh in range(2):
                lse_ref[hh] = (m_s[hh] + jnp.log2(l_s[hh]))[:, 0:1]

    grid_spec = pltpu.PrefetchScalarGridSpec(
        num_scalar_prefetch=2, grid=(HEADS // 2, len(steps)),
        in_specs=[pl.BlockSpec((tq, 256), lambda p, s, it, jt: (it[s], p)),
                  pl.BlockSpec((tq, 256), lambda p, s, it, jt: (jt[s], p)),
                  pl.BlockSpec((tq, 128), lambda p, s, it, jt: (jt[s], p))] + [ANY] * (2 * nb),
        out_specs=[pl.BlockSpec((tq, 128), lambda p, s, it, jt: (it[s], p)),
                   pl.BlockSpec((2, tq, 1), lambda p, s, it, jt: (p, it[s], 0))] + [ANY] * nb,
        scratch_shapes=[pltpu.VMEM((2, tq, 128), F32), pltpu.VMEM((2, tq, 128), F32), pltpu.VMEM((2, tq, 128), F32)]
        + ([pltpu.SemaphoreType.DMA((nb, 3))] * 4 if fetch else []))
    outs = pl.pallas_call(
        body, name="attn_fwd_fetch" if fetch else "attn_fwd", grid_spec=grid_spec,
        out_shape=[_sds((t, HEADS * VD), F32), _sds((HEADS, t, 1), F32)] + [_sds(b.shape, b.dtype) for b in bufs],
        input_output_aliases={5 + nb + b: 2 + b for b in range(nb)},
        compiler_params=_cp("arbitrary", "arbitrary"))(i_of, j_of, qs, k, v, *mine, *bufs)
    return outs[0], outs[1], list(outs[2:])


def _mixer_fwd(zb, ya, p, l):
    t = zb.shape[0]
    tm = min(ROW_TILE, t)
    hb = tm // 8

    def body(zb_ref, zprev_ref, ya_ref, gln_ref, bln_ref, wsp_ref, bsp_ref, cw_ref, go_ref, mix_ref, yb_ref, yc_ref):
        i = pl.program_id(0)
        masks = _group_masks((tm, SGW))
        cmasks = _group_masks((CHUNK, SGW))
        tril = _tril_mask()
        wc_bf = [jnp.where(tril, wsp_ref[g], 0.0).astype(BF16) for g in range(4)]
        u, _, _, _, _, mixed = _sgu_forward(zb_ref[:, 0:256], zb_ref[:, 256:512], gln_ref[...], bln_ref[...], wc_bf,
                                            bsp_ref[...], masks, cmasks)
        yb = u * mixed
        _, _, _, conv = _conv_forward(zb_ref[:, 768:1024], zb_ref[:, 1024:1280], zprev_ref[:, 768:1024],
                                      zprev_ref[:, 1024:1280], i == 0, cw_ref[...])
        yc = zb_ref[:, 512:768] * conv
        yb_ref[...] = yb
        yc_ref[...] = yc
        go = go_ref[...]
        mix_ref[:, 0:512] = _rms(ya_ref[...], go[:, 0:512]).astype(BF16)
        mix_ref[:, 512:768] = _rms(yb, go[:, 512:768]).astype(BF16)
        mix_ref[:, 768:1024] = _rms(yc, go[:, 768:1024]).astype(BF16)

    return _pcall(
        body, "mixer_fwd", (t // tm,),
        [zb, zb, ya, p["sg_ln_g"], p["sg_ln_b"], p["w_sp"], p["b_sp"], p["conv_w"], p["out_norm_g"]],
        [_row(tm, ZB), pl.BlockSpec((8, ZB), lambda i: (jnp.maximum(i * hb - 1, 0), 0)), _row(tm, 512),
         _lyr(l, 1, SGW), _lyr(l, 1, SGW), _lyr(l, 4, CHUNK, CHUNK), _lyr(l, CHUNK, SGW), _lyr(_wl(p["conv_w"], l), 3, CVW), _lyr(l, 1, D)],
        [_row(tm, D), _row(tm, SGW), _row(tm, CVW)],
        [_sds((t, D), BF16), _sds((t, SGW), F32), _sds((t, CVW), F32)], ("parallel",))


def _out_proj(mix, x, p, l):
    t = x.shape[0]
    tm = min(ROW_TILE, t)

    def body(mix_ref, w_ref, x_ref, gp_ref, gf_ref, o_ref, x2_ref, h2_ref):
        o = jnp.dot(mix_ref[...], w_ref[...], preferred_element_type=F32)
        o_ref[...] = o
        x2 = x_ref[...] + _rms(o, gp_ref[...])
        x2_ref[...] = x2
        h2_ref[...] = _rms(x2, gf_ref[...]).astype(BF16)

    return _pcall(
        body, "out_proj", (t // tm,), [mix, p["w_out"], x, p["mix_post_g"], p["ffn_pre_g"]],
        [_row(tm, D), _lyr(_wl(p["w_out"], l), D, D), _row(tm, D), _lyr(l, 1, D), _lyr(l, 1, D)],
        [_row(tm, D), _row(tm, D), _row(tm, D)],
        [_sds((t, D), F32), _sds((t, D), F32), _sds((t, D), BF16)], ("parallel",))


def _gu_all(l, which):
    return pl.BlockSpec((4, None, None, D, HP), lambda *_: (0, l, which, 0, 0))


def _down_all(l):
    return pl.BlockSpec((4, None, HP, D), lambda *_: (0, l, 0, 0))


def _ffn_up(h2, p, l):
    t = h2.shape[0]
    tm = min(ROW_TILE, t)

    def body(h_ref, wg_ref, wu_ref, a_ref, b_ref, s_ref):
        h = h_ref[...]
        a = jnp.dot(h, wg_ref[...], preferred_element_type=F32)
        b = jnp.dot(h, wu_ref[...], preferred_element_type=F32)
        a_ref[...] = a.astype(BF16)
        b_ref[...] = b.astype(BF16)
        s_ref[...] = (a * (1.0 / (1.0 + jnp.exp(-a))) * b).astype(BF16)

    blk = pl.BlockSpec((tm, HP), lambda k, i: (i, k))
    wblk = lambda which: pl.BlockSpec((None, None, None, D, HP), lambda k, i: (k, l, which, 0, 0))
    return _pcall(
        body, "ffn_up", (4, t // tm), [h2, p["w_gu"], p["w_gu"]],
        [pl.BlockSpec((tm, D), lambda k, i: (i, 0)), wblk(0), wblk(1)], [blk, blk, blk],
        [_sds((t, DFFP), BF16)] * 3, ("parallel", "parallel"))


def _ffn_down(s, x2, p, l):
    t = x2.shape[0]
    tm = min(ROW_TILE, t)

    def body(s_ref, w_ref, x_ref, g_ref, f_ref, x3_ref):
        f = jnp.dot(s_ref[:, 0:HP], w_ref[0], preferred_element_type=F32)
        for k in range(1, 4):
            f = f + jnp.dot(s_ref[:, k * HP:(k + 1) * HP], w_ref[k], preferred_element_type=F32)
        f_ref[...] = f
        x3_ref[...] = x_ref[...] + _rms(f, g_ref[...])

    return _pcall(
        body, "ffn_down", (t // tm,), [s, p["w_down"], x2, p["ffn_post_g"]],
        [_row(tm, DFFP), _down_all(l), _row(tm, D), _lyr(l, 1, D)], [_row(tm, D), _row(tm, D)],
        [_sds((t, D), F32), _sds((t, D), F32)], ("parallel",))


def _loss_head(y, target):
    t = y.shape[0]
    tm = min(ROW_TILE, t)

    def body(y_ref, t_ref, dy_ref, acc_ref):
        e = y_ref[...] - t_ref[...]
        dy_ref[...] = e * (1.0 / D)
        sq = jnp.sum(e * e, axis=0, keepdims=True)
        part = sq[:, 0:128]
        for b in range(1, D // 128):
            part = part + sq[:, 128 * b:128 * b + 128]
        _acc_init(pl.program_id(0), acc_ref)
        acc_ref[...] += part

    return _pcall(body, "loss_head", (t // tm,), [y, target], [_row(tm, D), _row(tm, D)],
                  [_row(tm, D), pl.BlockSpec((1, 128), lambda i: (0, 0))],
                  [_sds((t, D), F32), _sds((1, 128), F32)], ("arbitrary",))


def _ffn_down_bwd(dx3, sv, p, l, depth, gb):
    t = dx3.shape[0]
    tm = min(256, t)

    def body(dx_ref, f_ref, g_ref, w_ref, a_ref, b_ref, df_ref, da_ref, db_ref, dg_ref):
        _acc_init(pl.program_id(0), dg_ref)
        df, dg = _rms_bwd(f_ref[...], g_ref[...], dx_ref[...])
        dg_ref[...] += dg
        df = df.astype(BF16)
        df_ref[...] = df
        for k in range(4):
            sl = slice(k * HP, (k + 1) * HP)
            ds = lax.dot_general(df, w_ref[k], NT, preferred_element_type=F32)
            av = a_ref[:, sl].astype(F32)
            sig = 1.0 / (1.0 + jnp.exp(-av))
            da_ref[:, sl] = (ds * b_ref[:, sl].astype(F32) * (sig * (1.0 + av * (1.0 - sig)))).astype(BF16)
            db_ref[:, sl] = (ds * (av * sig)).astype(BF16)

    df, da, db, gb["ffn_post_g"] = _pcall(
        body, "ffn_down_bwd", (t // tm,), [dx3, sv["f"], p["ffn_post_g"], p["w_down"], sv["a"], sv["b"]],
        [_row(tm, D), _row(tm, D), _lyr(l, 1, D), _down_all(l), _row(tm, DFFP), _row(tm, DFFP)],
        [_row(tm, D), _row(tm, DFFP), _row(tm, DFFP), _lyr(l.g, 1, D)],
        [_sds((t, D), BF16), _sds((t, DFFP), BF16), _sds((t, DFFP), BF16), _sds((depth, 1, D), F32)], ("arbitrary",),
        prevs={3: gb.get("ffn_post_g")})
    return df, da, db


def _ffn_up_bwd(da, db, dx3, sv, p, l, depth, gb):
    t = dx3.shape[0]
    tm = min(256, t)

    def body(da_ref, db_ref, wg_ref, wu_ref, x_ref, dx3_ref, g_ref, dx2_ref, dg_ref):
        _acc_init(pl.program_id(0), dg_ref)
        dh = jnp.zeros((tm, D), F32)
        for k in range(4):
            sl = slice(k * HP, (k + 1) * HP)
            dh = dh + lax.dot_general(da_ref[:, sl], wg_ref[k], NT, preferred_element_type=F32)
            dh = dh + lax.dot_general(db_ref[:, sl], wu_ref[k], NT, preferred_element_type=F32)
        dx, dg = _rms_bwd(x_ref[...], g_ref[...], dh)
        dg_ref[...] += dg
        dx2_ref[...] = dx3_ref[...] + dx

    dx2, gb["ffn_pre_g"] = _pcall(
        body, "ffn_up_bwd", (t // tm,), [da, db, p["w_gu"], p["w_gu"], sv["x2"], dx3, p["ffn_pre_g"]],
        [_row(tm, DFFP), _row(tm, DFFP), _gu_all(l, 0), _gu_all(l, 1), _row(tm, D), _row(tm, D), _lyr(l, 1, D)],
        [_row(tm, D), _lyr(l.g, 1, D)], [_sds((t, D), F32), _sds((depth, 1, D), F32)], ("arbitrary",),
        prevs={1: gb.get("ffn_pre_g")})
    return dx2


def _out_proj_bwd(dx2, sv, p, l, depth, gb):
    t = dx2.shape[0]
    tm = min(ROW_TILE, t)

    def body(dx_ref, o_ref, g_ref, w_ref, do_ref, dmix_ref, dg_ref):
        _acc_init(pl.program_id(0), dg_ref)
        do, dg = _rms_bwd(o_ref[...], g_ref[...], dx_ref[...])
        dg_ref[...] += dg
        do = do.astype(BF16)
        do_ref[...] = do
        dmix_ref[...] = lax.dot_general(do, w_ref[...], NT, preferred_element_type=F32)

    do, dmix, gb["mix_post_g"] = _pcall(
        body, "out_proj_bwd", (t // tm,), [dx2, sv["o"], p["mix_post_g"], p["w_out"]],
        [_row(tm, D), _row(tm, D), _lyr(l, 1, D), _lyr(_wl(p["w_out"], l), D, D)], [_row(tm, D), _row(tm, D), _lyr(l.g, 1, D)],
        [_sds((t, D), BF16), _sds((t, D), F32), _sds((depth, 1, D), F32)], ("arbitrary",),
        prevs={2: gb.get("mix_post_g")})
    return do, dmix


def _mixer_bwd(dmix, sv, p, l, depth, gb):
    zb = sv["zb"]
    t = zb.shape[0]
    tm = min(ROW_TILE, t)
    hb = tm // 8
    last_blk = t // 8 - 1
    nsteps = t // tm

    def body(dmix_ref, ya_ref, yb_ref, yc_ref, zb_ref, zprev_ref, znext_ref, ycn_ref, dmn_ref,
             gln_ref, bln_ref, wsp_ref, bsp_ref, cw_ref, go_ref,
             dya_ref, dzb_ref, delta_ref, dgo_ref, dgln_ref, dbln_ref, dwsp_ref, dbsp_ref, dcw_ref):
        i = pl.program_id(0)
        _acc_init(i, dgo_ref, dgln_ref, dbln_ref, dwsp_ref, dbsp_ref, dcw_ref)
        go = go_ref[...]
        dmix = dmix_ref[...]

        ya = ya_ref[...]
        dya, dga = _rms_bwd(ya, go[:, 0:512], dmix[:, 0:512])
        dyb, dgb_ = _rms_bwd(yb_ref[...], go[:, 512:768], dmix[:, 512:768])
        dyc, dgc_ = _rms_bwd(yc_ref[...], go[:, 768:1024], dmix[:, 768:1024])
        dgo_ref[:, 0:512] += dga
        dgo_ref[:, 512:768] += dgb_
        dgo_ref[:, 768:1024] += dgc_
        dya = dya * LN2
        dya_ref[...] = dya.astype(BF16)
        prod = dya * ya
        hmasks = _group_masks((tm, 512))
        for h in range(HEADS):
            delta_ref[h] = jnp.sum(jnp.where(hmasks[h], prod, 0.0), axis=-1, keepdims=True)

        masks = _group_masks((tm, SGW))
        cmasks = _group_masks((CHUNK, SGW))
        tril = _tril_mask()
        wc_bf = [jnp.where(tril, wsp_ref[g], 0.0).astype(BF16) for g in range(4)]
        zu, zv = zb_ref[:, 0:256], zb_ref[:, 256:512]
        g_ln = gln_ref[...]
        u, _, xh, rs, vn, mixed = _sgu_forward(zu, zv, g_ln, bln_ref[...], wc_bf, bsp_ref[...], masks, cmasks)
        du = dyb * mixed
        dmixed = dyb * u
        dvn_chunks = []
        dbsp = jnp.zeros((CHUNK, SGW), F32)
        for ci in range(tm // CHUNK):
            rows = slice(ci * CHUNK, (ci + 1) * CHUNK)
            dm_c = dmixed[rows, :]
            vn_c = vn[rows, :].astype(BF16)
            dbsp = dbsp + dm_c
            dvn_c = jnp.zeros((CHUNK, SGW), F32)
            for g in range(4):
                dm_g = jnp.where(cmasks[g], dm_c, 0.0).astype(BF16)
                dw = lax.dot_general(dm_g, vn_c, NT, preferred_element_type=F32)
                dwsp_ref[g] += jnp.where(tril, dw, 0.0)
                dvn_c = dvn_c + lax.dot_general(wc_bf[g], dm_g, TN, preferred_element_type=F32)
            dvn_chunks.append(dvn_c)
        dbsp_ref[...] += dbsp
        dvn = jnp.concatenate(dvn_chunks, axis=0) if len(dvn_chunks) > 1 else dvn_chunks[0]
        dgln_ref[...] += jnp.sum(dvn * xh, axis=0, keepdims=True)
        dbln_ref[...] += jnp.sum(dvn, axis=0, keepdims=True)
        dxh = dvn * g_ln
        dvv = rs * (dxh - _group_mean(dxh, masks) - xh * _group_mean(dxh * xh, masks))
        dzb_ref[:, 0:256] = (du * _gelu_grad(zu)).astype(BF16)
        dzb_ref[:, 256:512] = (dvv * _gelu_grad(zv)).astype(BF16)

        cwv = cw_ref[...]
        gb_, gc, hh = zb_ref[:, 512:768], zb_ref[:, 768:1024], zb_ref[:, 1024:1280]
        yv, sh1, sh2, conv = _conv_forward(gc, hh, zprev_ref[:, 768:1024], zprev_ref[:, 1024:1280], i == 0, cwv)
        dconv = dyc * gb_
        dzb_ref[:, 512:768] = (dyc * conv).astype(BF16)
        dcw_ref[0:1, :] += jnp.sum(dconv * sh2, axis=0, keepdims=True)
        dcw_ref[1:2, :] += jnp.sum(dconv * sh1, axis=0, keepdims=True)
        dcw_ref[2:3, :] += jnp.sum(dconv * yv, axis=0, keepdims=True)
        dycn, _ = _rms_bwd(ycn_ref[...], go[:, 768:1024], dmn_ref[...])
        dconv_next = jnp.where(i == nsteps - 1, 0.0, dycn * znext_ref[:, 512:768])
        n0, n1 = _pick_row(dconv_next, 0), _pick_row(dconv_next, 1)
        dyv = dconv * cwv[2:3, :] + _shift_up(dconv, 1, [n0]) * cwv[1:2, :] + _shift_up(dconv, 2, [n0, n1]) * cwv[0:1, :]
        dzb_ref[:, 768:1024] = (dyv * hh).astype(BF16)
        dzb_ref[:, 1024:1280] = (dyv * gc).astype(BF16)

    prev_map = lambda i: (jnp.maximum(i * hb - 1, 0), 0)
    next_map = lambda i: (jnp.minimum((i + 1) * hb, last_blk), 0)
    names = ("out_norm_g", "sg_ln_g", "sg_ln_b", "w_sp", "b_sp_t", "conv_w")
    shapes = ((1, D), (1, SGW), (1, SGW), (4, CHUNK, CHUNK), (CHUNK, SGW), (3, CVW))
    outs = _pcall(
        body, "mixer_bwd", (nsteps,),
        [dmix, sv["ya"], sv["yb"], sv["yc"], zb, zb, zb, sv["yc"], dmix, p["sg_ln_g"], p["sg_ln_b"], p["w_sp"], p["b_sp"],
         p["conv_w"], p["out_norm_g"]],
        [_row(tm, D), _row(tm, 512), _row(tm, SGW), _row(tm, CVW), _row(tm, ZB),
         pl.BlockSpec((8, ZB), prev_map), pl.BlockSpec((8, ZB), next_map), pl.BlockSpec((8, CVW), next_map),
         pl.BlockSpec((8, 256), lambda i: (jnp.minimum((i + 1) * hb, last_blk), 3)),
         _lyr(l, 1, SGW), _lyr(l, 1, SGW), _lyr(l, 4, CHUNK, CHUNK), _lyr(l, CHUNK, SGW), _lyr(_wl(p["conv_w"], l), 3, CVW), _lyr(l, 1, D)],
        [_row(tm, 512), _row(tm, ZB), pl.BlockSpec((HEADS, tm, 1), lambda i: (0, i, 0))] + [_lyr(l.g, *s) for s in shapes],
        [_sds((t, 512), BF16), _sds((t, ZB), BF16), _sds((HEADS, t, 1), F32)] + [_sds((depth,) + s, F32) for s in shapes],
        ("arbitrary",), prevs={3 + n: gb.get(name) for n, name in enumerate(names)})
    for n, name in enumerate(names):
        gb[name] = outs[3 + n]
    return outs[0], outs[1], outs[2]


def _attn_bwd(qs, k, v, dya, lse, delta, ride=()):
    t = qs.shape[0]
    tq = _att_tile(t)
    nq = t // tq
    nb = len(ride)

    steps = [(j, i) for j in range(nq) for i in range(j, nq)]
    j_of = jnp.asarray([s[0] for s in steps], jnp.int32)
    i_of = jnp.asarray([s[1] for s in steps], jnp.int32)

    def body(j_ref, i_ref, q_ref, k_ref, v_ref, do_ref, lse_ref, dl_ref, *refs):
        dq_ref, dk_ref, dv_ref = refs[nb:nb + 3]
        dq_s, dk_s, dv_s = refs[2 * nb + 3:2 * nb + 6]
        step_no = pl.program_id(1)
        j, i = j_ref[step_no], i_ref[step_no]
        if ride:
            start, finish = _exchange_ops(refs[:nb], refs[nb + 3:2 * nb + 3], refs[2 * nb + 6:])
            pr = pl.program_id(0)
            pl.when((pr == 0) & (step_no == 0))(start)
            pl.when((pr == HEADS // 2 - 1) & (step_no == len(steps) - 1))(finish)

        @pl.when(step_no == 0)
        def _():
            dq_s[...] = jnp.zeros(dq_s.shape, F32)

        def step(masked):
            keep = _causal_keep(tq, 0, 0) if masked else None
            lane = lax.broadcasted_iota(jnp.int32, (tq, 128), 1)
            vv = v_ref[...]
            do = do_ref[...]
            rows = pl.ds(pl.multiple_of(i * tq, tq), tq)
            for hh in range(2):
                sl = slice(128 * hh, 128 * hh + 128)
                qq, kk = q_ref[:, sl], k_ref[:, sl]
                s = lax.dot_general(qq, kk, NT, preferred_element_type=F32)
                p = jnp.exp2(s - lse_ref[hh])
                if masked:
                    p = jnp.where(keep, p, 0.0)
                do_h = jnp.where((lane < VD) if hh == 0 else (lane >= VD), do, jnp.zeros_like(do))
                dp = lax.dot_general(do_h, vv, NT, preferred_element_type=F32)
                ds = (p * (dp - dl_ref[hh])).astype(BF16)
                dv_s[...] += lax.dot_general(p.astype(BF16), do_h, TN, preferred_element_type=F32)
                dk_s[:, sl] += lax.dot_general(ds, qq, TN, preferred_element_type=F32)
                dq_s[rows, sl] += jnp.dot(ds, kk, preferred_element_type=F32)

        @pl.when(i == j)
        def _():
            dk_s[...] = jnp.zeros(dk_s.shape, F32)
            dv_s[...] = jnp.zeros(dv_s.shape, F32)
            step(True)

        @pl.when(i > j)
        def _():
            step(False)

        @pl.when(i == nq - 1)
        def _():
            dk_ref[...] = dk_s[...].astype(BF16)
            dv_ref[...] = (dv_s[...] * LOG2E).astype(BF16)

        @pl.when(step_no == len(steps) - 1)
        def _():
            dq_ref[...] = dq_s[...].astype(BF16)

    qrow = lambda p, s, jt, it: (it[s], p)
    krow = lambda p, s, jt, it: (jt[s], p)
    col_spec = pl.BlockSpec((2, tq, 1), lambda p, s, jt, it: (p, it[s], 0))
    grid_spec = pltpu.PrefetchScalarGridSpec(
        num_scalar_prefetch=2, grid=(HEADS // 2, len(steps)),
        in_specs=[pl.BlockSpec((tq, 256), qrow), pl.BlockSpec((tq, 256), krow), pl.BlockSpec((tq, 128), krow),
                  pl.BlockSpec((tq, 128), qrow), col_spec, col_spec] + [ANY] * nb,
        out_specs=[pl.BlockSpec((t, 256), lambda p, s, jt, it: (0, p)), pl.BlockSpec((tq, 256), krow),
                   pl.BlockSpec((tq, 128), krow)] + [ANY] * nb,
        scratch_shapes=[pltpu.VMEM((t, 256), F32), pltpu.VMEM((tq, 256), F32), pltpu.VMEM((tq, 128), F32)]
        + ([pltpu.SemaphoreType.DMA((nb, 3))] * 2 if ride else []))
    outs = pl.pallas_call(
        body, name="attn_bwd_ride" if ride else "attn_bwd", grid_spec=grid_spec,
        out_shape=[_sds((t, QW), BF16), _sds((t, QW), BF16), _sds((t, HEADS * VD), BF16)]
        + [_sds((3,) + a.shape[1:], a.dtype) for a in ride],
        compiler_params=_cp("arbitrary", "arbitrary"))(j_of, i_of, qs, k, v, dya, lse, delta, *ride)
    return outs[0], outs[1], outs[2], list(outs[3:])


def _mla_prep_bwd(dqs, dk, dv, sv, p, l, depth, gb, tabs):
    za = sv["za"]
    t = za.shape[0]
    tm = min(ROW_TILE, t)

    def body(dq_ref, dk_ref, dv_ref, z_ref, gq_ref, gkv_ref, wuq_ref, wukv_ref, c_ref, s1_ref, s2_ref,
             dza_ref, dqp_ref, dkv_ref, dgq_ref, dgkv_ref):
        _acc_init(pl.program_id(0), dgq_ref, dgkv_ref)
        c, s1, s2 = c_ref[...], s1_ref[...], s2_ref[...]
        lane = lax.broadcasted_iota(jnp.int32, (tm, 128), 1)
        rope_lanes = (lane >= NOPE) & (lane < NOPE + ROPE)
        dkr = jnp.zeros((tm, 128), F32)
        for h in range(HEADS):
            sl = slice(128 * h, 128 * h + 128)
            dqp_ref[:, sl] = _rope_bwd(dq_ref[:, sl].astype(F32) * QSCALE, c, s1, s2).astype(BF16)
            dkh = dk_ref[:, sl]
            dkv_ref[:, sl] = dkh
            dkr = dkr + jnp.where(rope_lanes, dkh.astype(F32), 0.0)
        dkv_ref[:, QW:] = dv_ref[...]
        z = z_ref[...]
        dcq = lax.dot_general(dqp_ref[...], wuq_ref[...], NT, preferred_element_type=F32)
        dzq, dgq = _rms_bwd(z[:, :QR], gq_ref[...], dcq)
        dckv = lax.dot_general(dkv_ref[...], wukv_ref[...], NT, preferred_element_type=F32)
        dzkv, dgkv = _rms_bwd(z[:, QR:QR + KVR], gkv_ref[...], dckv)
        dgq_ref[...] += dgq
        dgkv_ref[...] += dgkv
        dza_ref[:, :QR] = dzq.astype(BF16)
        dza_ref[:, QR:QR + KVR] = dzkv.astype(BF16)
        dza_ref[:, QR + KVR:] = _rope_bwd(dkr, c, s1, s2).astype(BF16)

    dza, dqp, dkv, gb["q_norm_g"], gb["kv_norm_g"] = _pcall(
        body, "mla_prep_bwd", (t // tm,),
        [dqs, dk, dv, za, p["q_norm_g"], p["kv_norm_g"], p["w_uq"], p["w_ukv"], *tabs],
        [_row(tm, QW), _row(tm, QW), _row(tm, HEADS * VD), _row(tm, ZA), _lyr(l, 1, QR), _lyr(l, 1, KVR),
         _lyr(_wl(p["w_uq"], l), QR, QW), _lyr(_wl(p["w_ukv"], l), KVR, KVW), _row(tm, 128), _row(tm, 128), _row(tm, 128)],
        [_row(tm, ZA), _row(tm, QW), _row(tm, KVW), _lyr(l.g, 1, QR), _lyr(l.g, 1, KVR)],
        [_sds((t, ZA), BF16), _sds((t, QW), BF16), _sds((t, KVW), BF16), _sds((depth, 1, QR), F32),
         _sds((depth, 1, KVR), F32)], ("arbitrary",), prevs={3: gb.get("q_norm_g"), 4: gb.get("kv_norm_g")})
    return dza, dqp, dkv


def _in_proj_bwd(dza, dzb, dx2, sv, p, l, depth, gb):
    t = dx2.shape[0]
    tm = min(ROW_TILE, t)

    def body(dza_ref, dzb_ref, wa_ref, wb_ref, x_ref, dx2_ref, g_ref, dx_ref, dg_ref):
        _acc_init(pl.program_id(0), dg_ref)
        dh = (lax.dot_general(dza_ref[...], wa_ref[...], NT, preferred_element_type=F32)
              + lax.dot_general(dzb_ref[...], wb_ref[...], NT, preferred_element_type=F32))
        dx, dg = _rms_bwd(x_ref[...], g_ref[...], dh)
        dg_ref[...] += dg
        dx_ref[...] = dx2_ref[...] + dx

    dx, gb["mix_pre_g"] = _pcall(
        body, "in_proj_bwd", (t // tm,), [dza, dzb, p["w_in_a"], p["w_in_b"], sv["x"], dx2, p["mix_pre_g"]],
        [_row(tm, ZA), _row(tm, ZB), _lyr(_wl(p["w_in_a"], l), D, ZA), _lyr(_wl(p["w_in_b"], l), D, ZB), _row(tm, D), _row(tm, D), _lyr(l, 1, D)],
        [_row(tm, D), _lyr(l.g, 1, D)], [_sds((t, D), F32), _sds((depth, 1, D), F32)], ("arbitrary",),
        prevs={1: gb.get("mix_pre_g")})
    return dx


def _mm_tn(a, b, tn, name, l, depth, gb):
    t, k = a.shape
    n = b.shape[1]
    tt = min(ROW_TILE, t)

    def body(a_ref, b_ref, o_ref):
        _acc_init(pl.program_id(1), o_ref)
        o_ref[...] += lax.dot_general(a_ref[...], b_ref[...], TN, preferred_element_type=F32)

    gb[name] = _pcall(
        body, "d" + name, (n // tn, t // tt), [a, b],
        [pl.BlockSpec((tt, k), lambda j, s: (s, 0)), pl.BlockSpec((tt, tn), lambda j, s: (s, j))],
        pl.BlockSpec((None, k, tn), lambda j, s: (l.g, 0, j)), _sds((depth, k, n), F32), ("parallel", "arbitrary"),
        prevs={0: gb.get(name)})


def _dw_ffn(a, b, kind, l, depth, gb):
    t = a.shape[0]
    tt = min(ROW_TILE, t)
    nsteps = t // tt
    down = kind == "down"

    def body(a_ref, b_ref, o_ref, acc):
        s = pl.program_id(0)
        _acc_init(s, acc)
        acc[...] += lax.dot_general(a_ref[...], b_ref[...], TN, preferred_element_type=F32)

        @pl.when(s == nsteps - 1)
        def _():
            for k in range(4):
                blk = acc[k * HP:(k + 1) * HP, :] if down else acc[:, k * HP:(k + 1) * HP]
                o_ref[k] = blk.astype(BF16)

    rows = lambda n: pl.BlockSpec((tt, n), lambda s: (s, 0))
    if down:
        in_specs, acc_shape, name = [rows(DFFP), rows(D)], (DFFP, D), "down"
        out_spec = pl.BlockSpec((4, None, HP, D), lambda s: (0, l.g, 0, 0))
        out_shape = _sds((4, depth, HP, D), BF16)
    else:
        which = 0 if kind == "gate" else 1
        in_specs, acc_shape, name = [rows(D), rows(DFFP)], (D, DFFP), "gu"
        out_spec = pl.BlockSpec((4, None, None, D, HP), lambda s: (0, l.g, which, 0, 0))
        out_shape = _sds((4, depth, 2, D, HP), BF16)
    gb[name] = _pcall(body, "dw_" + kind, (nsteps,), [a, b], in_specs, out_spec, out_shape, ("arbitrary",),
                      scratch=[pltpu.VMEM(acc_shape, F32)], prevs={0: gb.get(name)})


def _ffn_views(bufs):
    return {"w_gu": bufs[1], "w_down": bufs[2].reshape(bufs[2].shape[:2] + (HP, D))}


def _layer_fwd(x, p, l, tabs, fetch):
    h1, za, zb = _in_proj(x, p, l)
    cqn, ckvn, qs, k, v = _mla_prep(za, p, l, tabs)
    ya, lse, bufs = _attn_fwd(qs, k, v, fetch)
    if fetch:
        p = {**p, **_ffn_views(bufs)}
    mix, yb, yc = _mixer_fwd(zb, ya, p, l)
    o, x2, h2 = _out_proj(mix, x, p, l)
    a, b, s = _ffn_up(h2, p, l)
    f, x3 = _ffn_down(s, x2, p, l)
    saved = dict(x=x, h1=h1, za=za, zb=zb, cqn=cqn, ckvn=ckvn, qs=qs, k=k, v=v, ya=ya, lse=lse, mix=mix, yb=yb, yc=yc,
                 o=o, x2=x2, h2=h2, a=a, b=b, s=s, f=f)
    return x3, saved, bufs if fetch else None


class _Layer(int):
    def __new__(cls, l, g):
        obj = int.__new__(cls, l)
        obj.g = g
        return obj


def _layer_bwd(dx3, p, sv, l, depth, gb, tabs, ride=()):
    df, da, db = _ffn_down_bwd(dx3, sv, p, l, depth, gb)
    _dw_ffn(sv["s"], df, "down", l, depth, gb)
    dx2 = _ffn_up_bwd(da, db, dx3, sv, p, l, depth, gb)
    _dw_ffn(sv["h2"], da, "gate", l, depth, gb)
    _dw_ffn(sv["h2"], db, "up", l, depth, gb)
    do, dmix = _out_proj_bwd(dx2, sv, p, l, depth, gb)
    _mm_tn(sv["mix"], do, D, "w_out", l, depth, gb)
    dya, dzb, delta = _mixer_bwd(dmix, sv, p, l, depth, gb)
    dqs, dk, dv, sent = _attn_bwd(sv["qs"], sv["k"], sv["v"], dya, sv["lse"], delta, ride)
    dza, dqp, dkv = _mla_prep_bwd(dqs, dk, dv, sv, p, l, depth, gb, tabs)
    _mm_tn(sv["cqn"], dqp, QW, "w_uq", l, depth, gb)
    _mm_tn(sv["ckvn"], dkv, KVW, "w_ukv", l, depth, gb)
    _mm_tn(sv["h1"], dza, ZA, "w_in_a", l, depth, gb)
    _mm_tn(sv["h1"], dzb, ZB, "w_in_b", l, depth, gb)
    return _in_proj_bwd(dza, dzb, dx2, sv, p, l, depth, gb), sent


def _rope_tables(positions):
    inv_freq = 1.0 / (ROPE_THETA ** (jnp.arange(0, ROPE // 2, dtype=F32) / (ROPE // 2)))
    ang = positions.astype(F32)[:, None] * inv_freq
    cos, sin = jnp.cos(ang), jnp.sin(ang)
    t = positions.shape[0]
    one, zero = jnp.ones((t, 64), F32), jnp.zeros((t, 16), F32)
    c = jnp.concatenate([one, cos, cos, one[:, :32]], axis=1)
    s1 = jnp.concatenate([zero, zero, zero, zero, -sin, zero, zero, zero], axis=1)
    s2 = jnp.concatenate([zero, zero, zero, zero, zero, sin, zero, zero], axis=1)
    return c, s1, s2


def _mixer_weight_params(full):
    w_in = full["w_in"]
    depth = w_in.shape[0]
    zpad = lambda n: jnp.zeros((depth, D, n), w_in.dtype)
    kv = full["w_ukv"].reshape(depth, KVR, HEADS, NOPE + VD)
    return {
        "w_in_a": jnp.concatenate([w_in[:, :, :640], zpad(64), w_in[:, :, 640:672], zpad(32)], axis=2),
        "w_in_b": w_in[:, :, 672:],
        "w_uq": jnp.pad(full["w_uq"].reshape(depth, QR, HEADS, NOPE + ROPE),
                        ((0, 0), (0, 0), (0, 0), (0, 32))).reshape(depth, QR, QW),
        "w_ukv": jnp.concatenate([jnp.pad(kv[..., :NOPE], ((0, 0), (0, 0), (0, 0), (0, 64))).reshape(depth, KVR, QW),
                                  kv[..., NOPE:].reshape(depth, KVR, HEADS * VD)], axis=2),
        "w_out": full["w_out"], "conv_w": full["conv_w"],
    }


def _small_params(w):
    p = {"w_sp": w["w_sp"], "b_sp": jnp.repeat(jnp.swapaxes(w["b_sp"], 1, 2), 64, axis=2)}
    for n in ("mix_pre_g", "mix_post_g", "ffn_pre_g", "ffn_post_g", "q_norm_g", "kv_norm_g", "sg_ln_g", "sg_ln_b",
              "out_norm_g"):
        p[n] = w[n][:, None, :]
    return p


def _natural_grads(gb):
    depth = gb["w_in_a"].shape[0]
    ga, kv = gb["w_in_a"], gb["w_ukv"]
    out = {
        "w_in": jnp.concatenate([ga[:, :, :640], ga[:, :, 704:736], gb["w_in_b"]], axis=2),
        "w_uq": gb["w_uq"].reshape(depth, QR, HEADS, 128)[..., :NOPE + ROPE].reshape(depth, QR, HEADS * (NOPE + ROPE)),
        "w_ukv": jnp.concatenate([kv[:, :, :QW].reshape(depth, KVR, HEADS, 128)[..., :NOPE],
                                  kv[:, :, QW:].reshape(depth, KVR, HEADS, VD)], axis=3).reshape(depth, KVR, -1),
        "b_sp": jnp.swapaxes(gb["b_sp_t"].reshape(depth, CHUNK, 4, 64).sum(axis=-1), 1, 2),
    }
    for n in ("w_out", "w_sp", "conv_w"):
        out[n] = gb[n]
    for n in ("mix_pre_g", "mix_post_g", "ffn_pre_g", "ffn_post_g", "q_norm_g", "kv_norm_g", "sg_ln_g", "sg_ln_b",
              "out_norm_g"):
        out[n] = gb[n][:, 0, :]
    return out


def _local_step(x, positions, target, small, mine, bufs, shard_shapes, fetch=True, front=None):
    depth = small["w_sp"].shape[0]
    tabs = _rope_tables(positions)
    ps = _small_params(small)
    saved, mixer_w = [], []
    for l in range(depth):
        mixer_w.append(_mixer_weight_params(_unpack_weights(bufs[0], l, shard_shapes)))
        p = {**ps, **mixer_w[l], **_ffn_views(bufs)}
        x, sv, fetched = _layer_fwd(x, p, l, tabs, (mine, bufs, l + 1) if fetch and l + 1 < depth else None)
        bufs = fetched or bufs
        saved.append(sv)
    dx, acc = _loss_head(x, target)
    loss = (0.5 / D) * jnp.sum(acc)
    nbatch = depth // 2
    gbs = [{} for _ in range(nbatch)]
    fronts, sent = [None] * nbatch, [None] * nbatch
    ride = ()
    for l in reversed(range(depth)):
        b = l // 2
        dx, got = _layer_bwd(dx, {**ps, **mixer_w[l], **_ffn_views(bufs)}, saved[l], _Layer(l, l % 2), 2, gbs[b], tabs, ride)
        if ride:
            sent[b + 1], ride = got, ()
        if l % 2 == 0:
            done = (_natural_grads(gbs[b]), gbs[b]["gu"], gbs[b]["down"])
            fronts[b] = front(b, *done) if front else done
            if front and b > 0:
                ride = tuple(fronts[b])
    return loss, dx, fronts, sent


def _place():
    x, y, c = lax.axis_index("x"), lax.axis_index("y"), lax.axis_index("c")
    chips = [(1 - x, y), (x, 1 - y), (1 - x, 1 - y)]
    return x, y, c, 2 * x + y, chips


def _remote(src, dst, send_sem, recv_sem, to):
    return pltpu.make_async_remote_copy(src_ref=src, dst_ref=dst, send_sem=send_sem, recv_sem=recv_sem, device_id=to,
                                        device_id_type=MESH_ID)


def _gather_ops(mine_refs, out_refs, sems, l):
    send_sems, recv_sems, fsend_sems, frecv_sems = sems
    x, y, c, k, chips = _place()
    sib = (x, y, 1 - c)
    pairs = [(b, n) for n in range(3) for b in range(len(mine_refs))]

    def slot(n):
        return 2 * chips[n][0] + chips[n][1]

    def ici(b, n, dst_chip):
        return _remote(mine_refs[b].at[l, c], out_refs[b].at[dst_chip, l, c], send_sems.at[b, n], recv_sems.at[b, n],
                       (*chips[n], c))

    def d2d(b, n, half):
        piece = out_refs[b].at[slot(n), l, half]
        return _remote(piece, piece, fsend_sems.at[b, n], frecv_sems.at[b, n], sib)

    def start():
        for b, n in pairs:
            ici(b, n, k).start()

    def hand_over():
        for b, n in pairs:
            ici(b, n, slot(n)).wait_recv()
            d2d(b, n, c).start()

    def drain():
        for b, n in pairs:
            d2d(b, n, 1 - c).wait_recv()
        for b, n in pairs:
            ici(b, n, k).wait_send()
            d2d(b, n, c).wait_send()

    return start, hand_over, drain


def _gather_first_layer(mine):
    nb = len(mine)

    def body(*refs):
        start, hand_over, drain = _gather_ops(refs[:nb], refs[nb:2 * nb], refs[2 * nb:], 0)
        start()
        hand_over()
        drain()

    return pl.pallas_call(
        body, name="gather_first_layer", in_specs=[ANY] * nb, out_specs=[ANY] * nb,
        out_shape=[_sds((4,) + a.shape, a.dtype) for a in mine],
        scratch_shapes=[pltpu.SemaphoreType.DMA((nb, 3))] * 4)(*mine)


def _swap_halves(bigs, wholes=()):
    nb, n = len(bigs), len(bigs) + len(wholes)

    def body(*refs):
        src, dst = refs[:n], refs[n:2 * n]
        send_sems, recv_sems = refs[2 * n:]
        x, y, c, _, _ = _place()
        sib = (x, y, 1 - c)
        cps = [_remote(src[b].at[:, 1 - c] if b < nb else src[b], dst[b], send_sems.at[b], recv_sems.at[b], sib)
               for b in range(n)]
        for cp in cps:
            cp.start()
        for cp in cps:
            cp.wait()

    return pl.pallas_call(
        body, name="swap_halves", in_specs=[ANY] * n, out_specs=[ANY] * n,
        out_shape=[_sds((4,) + a.shape[2:], a.dtype) for a in bigs] + [_sds(a.shape, a.dtype) for a in wholes],
        scratch_shapes=[pltpu.SemaphoreType.DMA((n,))] * 2)(*bigs, *wholes)


def _sum_tile(r):
    return max(cand for cand in range(16, 641, 16) if r % cand == 0)


def _pair_sum(big, rbig, c):
    _, _, r, w = big.shape
    tr = _sum_tile(r)

    def body(c_ref, big_ref, rbig_ref, p_ref):
        p_ref[...] = (big_ref[...].astype(F32) + rbig_ref[...].astype(F32)).astype(BF16)

    grid_spec = pltpu.PrefetchScalarGridSpec(
        num_scalar_prefetch=1, grid=(4, r // tr),
        in_specs=[pl.BlockSpec((None, None, tr, w), lambda j, i, cr: (j, cr[0], i, 0)),
                  pl.BlockSpec((None, tr, w), lambda j, i, cr: (j, i, 0))],
        out_specs=pl.BlockSpec((None, tr, w), lambda j, i, cr: (j, i, 0)))
    return pl.pallas_call(body, name="pair_sum", grid_spec=grid_spec, out_shape=_sds((4, r, w), BF16),
                          compiler_params=_cp("parallel", "parallel"))(c, big, rbig)


def _small_sum(parts):
    n, ns, _ = parts.shape

    def body(p_ref, o_ref):
        s = p_ref[0]
        for j in range(1, n):
            s = s + p_ref[j]
        o_ref[...] = s

    return pl.pallas_call(body, name="small_sum", out_shape=_sds((ns, 128), F32))(parts)


def _exchange_ops(p_refs, rb_refs, sems, small=None):
    send_sems, recv_sems = sems[0], sems[1]
    nb = len(p_refs)
    x, y, c, k, chips = _place()

    def copies(landing):
        out = []
        for n, (cx, cy) in enumerate(chips):
            to, kj = (cx, cy, c), 2 * cx + cy
            for b in range(nb):
                out.append(_remote(p_refs[b].at[k if landing else kj], rb_refs[b].at[n], send_sems.at[b, n],
                                   recv_sems.at[b, n], to))
            if small:
                out.append(_remote(small[0], small[1].at[kj if landing else k], send_sems.at[nb, n], recv_sems.at[nb, n], to))
        return out

    def local():
        return pltpu.make_async_copy(small[0], small[1].at[k], sems[2])

    def start():
        if small:
            local().start()
        for cp in copies(False):
            cp.start()

    def finish():
        for cp in copies(True):
            cp.wait_recv()
        for cp in copies(False):
            cp.wait_send()
        if small:
            local().wait()

    return start, finish


def _chip_exchange(ps, small):
    nb = len(ps)
    ns = small.shape[0]

    def body(*refs):
        start, finish = _exchange_ops(refs[:nb], refs[nb + 1:2 * nb + 1], refs[2 * nb + 2:], (refs[nb], refs[2 * nb + 1]))
        start()
        finish()

    return pl.pallas_call(
        body, name="chip_exchange", in_specs=[ANY] * (nb + 1), out_specs=[ANY] * (nb + 1),
        out_shape=[_sds((3,) + a.shape[1:], a.dtype) for a in ps] + [_sds((4, ns, 128), small.dtype)],
        scratch_shapes=[pltpu.SemaphoreType.DMA((nb + 1, 3))] * 2 + [pltpu.SemaphoreType.DMA(())])(*ps, small)


def _chip_sum(p, rb, chip):
    _, r, w = p.shape
    tr = _sum_tile(r)

    def body(k_ref, p_ref, rb_ref, o_ref):
        acc = p_ref[...].astype(F32)
        for j in range(3):
            acc = acc + rb_ref[j].astype(F32)
        o_ref[...] = acc

    grid_spec = pltpu.PrefetchScalarGridSpec(
        num_scalar_prefetch=1, grid=(r // tr,),
        in_specs=[pl.BlockSpec((None, tr, w), lambda i, kr: (kr[0], i, 0)), pl.BlockSpec((3, tr, w), lambda i, kr: (0, i, 0))],
        out_specs=pl.BlockSpec((tr, w), lambda i, kr: (i, 0)))
    return pl.pallas_call(body, name="chip_sum", grid_spec=grid_spec, out_shape=_sds((r, w), F32),
                          compiler_params=_cp("parallel"))(chip, p, rb)


def _send_to_sibling(reds):
    nb = len(reds)

    def body(*refs):
        red_refs, out_refs = refs[:nb], refs[nb:2 * nb]
        send_sems, recv_sems = refs[2 * nb:]
        x, y, c, _, _ = _place()
        cps = [_remote(red_refs[b], out_refs[b], send_sems.at[b], recv_sems.at[b], (x, y, 1 - c)) for b in range(nb)]
        for cp in cps:
            cp.start()
        for cp in cps:
            cp.wait()

    return pl.pallas_call(
        body, name="send_to_sibling", in_specs=[ANY] * nb, out_specs=[ANY] * nb,
        out_shape=[_sds(a.shape, a.dtype) for a in reds], scratch_shapes=[pltpu.SemaphoreType.DMA((nb,))] * 2)(*reds)


def _adam_math(w, g, m, v):
    nm = ADAM_B1 * m + (1.0 - ADAM_B1) * g
    nv = ADAM_B2 * v + (1.0 - ADAM_B2) * (g * g)
    m_hat = nm / (1.0 - ADAM_B1 ** ADAM_STEP)
    v_hat = nv / (1.0 - ADAM_B2 ** ADAM_STEP)
    return -ADAM_LR * (m_hat / (jnp.sqrt(v_hat) + ADAM_EPS) + ADAM_WD * w), nm, nv


def _adamw_shard(w, m, v, owns, others, c, name, pick=None):
    depth, r, n = w.shape
    nbatch = len(owns)
    tr = max(cand for cand in range(8, min(r, 256) + 1, 8) if r % cand == 0)
    npad = owns[0].shape[-1]

    def body(c_ref, w_ref, m_ref, v_ref, *refs):
        g_ref, d_ref, nm_ref, nv_ref = refs[2 * nbatch:]
        l = pl.program_id(0)
        mine = (l % 2) == c_ref[0]
        g = jnp.where(mine, refs[0][...], refs[nbatch][...])
        for b in range(1, nbatch):
            g = jnp.where(l // 2 == b, jnp.where(mine, refs[b][...], refs[nbatch + b][...]), g)
        g = g[:, :n]
        g_ref[...] = g
        d_ref[...], nm_ref[...], nv_ref[...] = _adam_math(w_ref[...], g, m_ref[...], v_ref[...])

    blk = pl.BlockSpec((None, tr, n), lambda l, i, cr: (l, i, 0))
    if pick is None:
        gblk = pl.BlockSpec((tr, npad), lambda l, i, cr: (i, 0))
    else:
        gblk = pl.BlockSpec((None, tr, npad), lambda l, i, cr: (pick, i, 0))
    grid_spec = pltpu.PrefetchScalarGridSpec(num_scalar_prefetch=1, grid=(depth, r // tr),
                                             in_specs=[blk] * 3 + [gblk] * (2 * nbatch), out_specs=[blk] * 4)
    return pl.pallas_call(body, name=name, grid_spec=grid_spec, out_shape=[_sds(w.shape, F32)] * 4,
                          compiler_params=_cp("parallel", "parallel"))(c, w, m, v, *owns, *others)


def _pad_ffn_shards(w_gate, w_up, w_down):
    depth = w_gate.shape[0]
    tm = 256
    hr = HP // 2

    def gu_body(g_ref, u_ref, o_ref):
        o_ref[...] = jnp.zeros(o_ref.shape, BF16)
        o_ref[0, :, 0:HS] = g_ref[...].astype(BF16)
        o_ref[1, :, 0:HS] = u_ref[...].astype(BF16)

    blk = pl.BlockSpec((None, tm, HS), lambda l, i: (l, i, 0))
    gu = pl.pallas_call(
        gu_body, name="pad_gate_up", grid=(depth, D // tm), in_specs=[blk, blk],
        out_specs=pl.BlockSpec((None, 2, tm, HP), lambda l, i: (l, 0, i, 0)),
        out_shape=_sds((depth, 2, D, HP), BF16), compiler_params=_cp("parallel", "parallel"))(w_gate, w_up)

    def down_body(w_ref, o_ref):
        o_ref[0] = w_ref[0:hr, :].astype(BF16)
        o_ref[1, 0:HS - hr, :] = w_ref[hr:HS, :].astype(BF16)
        o_ref[1, HS - hr:hr, :] = jnp.zeros((HP - HS, D), BF16)

    down = pl.pallas_call(
        down_body, name="pad_down", grid=(depth,), in_specs=[pl.BlockSpec((None, HS, D), lambda l: (l, 0, 0))],
        out_specs=pl.BlockSpec((None, 2, hr, D), lambda l: (l, 0, 0, 0)),
        out_shape=_sds((depth, 2, hr, D), BF16), compiler_params=_cp("parallel"))(w_down)
    return gu, down


def _adamw_small(w, g, m, v):
    r = w.shape[0]
    tr = max(cand for cand in range(8, 513, 8) if r % cand == 0)

    def body(w_ref, g_ref, m_ref, v_ref, d_ref, nm_ref, nv_ref):
        d_ref[...], nm_ref[...], nv_ref[...] = _adam_math(w_ref[...], g_ref[...], m_ref[...], v_ref[...])

    blk = pl.BlockSpec((tr, 128), lambda i: (i, 0))
    return pl.pallas_call(body, name="adamw_small", grid=(r // tr,), in_specs=[blk] * 4, out_specs=[blk] * 3,
                          out_shape=[_sds(w.shape, F32)] * 3, compiler_params=_cp("parallel"))(w, g, m, v)


def _to_pack(a, name):
    depth = a.shape[0]
    if name in ROW_SHARDED:
        return jnp.swapaxes(a.reshape(depth, 4, -1, D), 0, 1)
    return jnp.transpose(a.reshape(depth, a.shape[1], 4, a.shape[2] // 4), (2, 0, 1, 3)).reshape(4, depth, -1, D)


def _pack_rows(parts, lead, dtype, tail=None):
    pieces, at = [], 0
    for n, off, rows in PACK:
        if off > at:
            pieces.append(jnp.zeros(lead + (off - at, D), dtype))
        pieces.append(parts[n].astype(dtype))
        at = off + rows
    if tail is not None:
        pieces.append(tail)
        at += tail.shape[-2]
    pieces.append(jnp.zeros(lead + (PACK_ROWS - at, D), dtype))
    return jnp.concatenate(pieces, axis=len(lead))


def _pack_weight_shards(sh):
    depth = sh["w_in"].shape[0]
    parts = {n: sh[n].reshape(depth, rows, D) for n, _, rows in PACK}
    conv = lax.bitcast_convert_type(sh["conv_w"].reshape(depth, 3 * 64), BF16).reshape(depth, 1, 384)
    flat = _pack_rows(parts, (depth,), BF16, tail=jnp.pad(conv, ((0, 0), (0, 0), (0, D - 384))))
    return flat.reshape(depth, 2, PACK_ROWS // 2, D)


def _unpack_weights(gathered, l, shard_shapes):
    depth = 1
    flat = gathered[:, l].reshape(4, 1, PACK_ROWS, D)
    full = {}
    for n, off, rows in PACK:
        shp = shard_shapes[n][1:]
        piece = flat[:, :, off:off + rows, :].reshape((4, depth) + shp)
        if n in ROW_SHARDED:
            full[n] = jnp.transpose(piece, (1, 0, 2, 3)).reshape(depth, 4 * shp[0], shp[1])
        else:
            full[n] = jnp.transpose(piece, (1, 2, 0, 3)).reshape(depth, shp[0], 4 * shp[1])
    conv = lax.bitcast_convert_type(flat[:, :, CONV_ROW, :384].reshape(4, depth, 192, 2), F32)
    full["conv_w"] = jnp.transpose(conv.reshape(4, depth, 3, 64), (1, 2, 0, 3)).reshape(depth, 3, CVW)
    return full


def _pack_grad_shards(g):
    depth = g["w_in"].shape[0]
    return _pack_rows({n: _to_pack(g[n], n) for n, _, _ in PACK}, (4, depth), BF16)


def _pack_small(arrs, names_shapes, depth):
    flat = jnp.concatenate([arrs[n].reshape(depth, -1) for n, _ in names_shapes], axis=1).reshape(-1)
    rows = -(-flat.shape[0] // 1024) * 8
    return jnp.pad(flat, (0, rows * 128 - flat.shape[0])).reshape(rows, 128)


def _unpack_small(packed, names_shapes, depth):
    per_layer = sum(math.prod(s) for _, s in names_shapes)
    flat = packed.reshape(-1)[:depth * per_layer].reshape(depth, per_layer)
    out, off = {}, 0
    for n, s in names_shapes:
        size = math.prod(s)
        out[n] = flat[:, off:off + size].reshape((depth,) + s)
        off += size
    return out


def kernel(x, positions, mix_pre_g, mix_post_g, ffn_pre_g, ffn_post_g, w_in, q_norm_g, w_uq, kv_norm_g, w_ukv, sg_ln_g, sg_ln_b, w_sp, b_sp, conv_w, out_norm_g, w_out, w_gate, w_up, w_down, loss_target, m_mix_pre_g, m_mix_post_g, m_ffn_pre_g, m_ffn_post_g, m_w_in, m_q_norm_g, m_w_uq, m_kv_norm_g, m_w_ukv, m_sg_ln_g, m_sg_ln_b, m_w_sp, m_b_sp, m_conv_w, m_out_norm_g, m_w_out, m_w_gate, m_w_up, m_w_down, v_mix_pre_g, v_mix_post_g, v_ffn_pre_g, v_ffn_post_g, v_w_in, v_q_norm_g, v_w_uq, v_kv_norm_g, v_w_ukv, v_sg_ln_g, v_sg_ln_b, v_w_sp, v_b_sp, v_conv_w, v_out_norm_g, v_w_out, v_w_gate, v_w_up, v_w_down):
    w = dict(mix_pre_g=mix_pre_g, mix_post_g=mix_post_g, ffn_pre_g=ffn_pre_g, ffn_post_g=ffn_post_g, w_in=w_in,
             q_norm_g=q_norm_g, w_uq=w_uq, kv_norm_g=kv_norm_g, w_ukv=w_ukv, sg_ln_g=sg_ln_g, sg_ln_b=sg_ln_b, w_sp=w_sp,
             b_sp=b_sp, conv_w=conv_w, out_norm_g=out_norm_g, w_out=w_out, w_gate=w_gate, w_up=w_up, w_down=w_down)
    m = dict(mix_pre_g=m_mix_pre_g, mix_post_g=m_mix_post_g, ffn_pre_g=m_ffn_pre_g, ffn_post_g=m_ffn_post_g, w_in=m_w_in,
             q_norm_g=m_q_norm_g, w_uq=m_w_uq, kv_norm_g=m_kv_norm_g, w_ukv=m_w_ukv, sg_ln_g=m_sg_ln_g, sg_ln_b=m_sg_ln_b,
             w_sp=m_w_sp, b_sp=m_b_sp, conv_w=m_conv_w, out_norm_g=m_out_norm_g, w_out=m_w_out, w_gate=m_w_gate,
             w_up=m_w_up, w_down=m_w_down)
    v = dict(mix_pre_g=v_mix_pre_g, mix_post_g=v_mix_post_g, ffn_pre_g=v_ffn_pre_g, ffn_post_g=v_ffn_post_g, w_in=v_w_in,
             q_norm_g=v_q_norm_g, w_uq=v_w_uq, kv_norm_g=v_kv_norm_g, w_ukv=v_w_ukv, sg_ln_g=v_sg_ln_g, sg_ln_b=v_sg_ln_b,
             w_sp=v_w_sp, b_sp=v_b_sp, conv_w=v_conv_w, out_norm_g=v_out_norm_g, w_out=v_w_out, w_gate=v_w_gate,
             w_up=v_w_up, w_down=v_w_down)
    depth = w_in.shape[0]
    c = lax.axis_index("c").astype(jnp.int32).reshape(1)
    chip = (2 * lax.axis_index("x") + lax.axis_index("y")).astype(jnp.int32)

    nbatch = depth // 2

    mine = [_pack_weight_shards(w), *_pad_ffn_shards(w_gate, w_up, w_down)]
    bufs = [lax.dynamic_update_slice(g, a[None], (chip,) + (0,) * a.ndim)
            for g, a in zip(_gather_first_layer(mine), mine)]

    small_grads = [None] * nbatch
    small_pair = []

    def front(b, grads, g_gu, g_down):
        small_grads[b] = grads
        bigs = [_pack_grad_shards(grads), g_gu.reshape(4, 2, 2 * D, HP), g_down]
        if b > 0:
            rbigs = _swap_halves(bigs)
        else:
            small = _pack_small({n: jnp.concatenate([g[n] for g in small_grads]) for n, _ in SMALL}, SMALL, depth)
            *rbigs, rsmall = _swap_halves(bigs, [small])
            small_pair.append(_small_sum(jnp.stack([small, rsmall])))
        return [_pair_sum(a, r, c) for a, r in zip(bigs, rbigs)]

    loss, dx, ps, sent = _local_step(x[0], positions[0], loss_target[0], w, mine, bufs,
                                     {n: w[n].shape for n, _, _ in PACK}, front=front)
    loss = lax.psum(loss, ("x", "y", "c"))

    *sent[0], rs = _chip_exchange(ps[0], small_pair[0])
    own = [[_chip_sum(p, rb, chip.reshape(1)) for p, rb in zip(ps[b], sent[b])] for b in range(nbatch)]
    other = [_send_to_sibling(o) for o in own]
    g_small = _unpack_small(_small_sum(rs), SMALL, depth)
    g_small["conv_w"] = lax.dynamic_slice_in_dim(g_small["conv_w"], chip * 64, 64, axis=2)

    gw, delta, new_m, new_v = dict(g_small), {}, {}, {}

    def adam(n, pieces, pick=None):
        outs = _adamw_shard(w[n], m[n], v[n], [pieces(o) for o in own], [pieces(o) for o in other], c, "adamw_" + n, pick)
        gw[n], delta[n], new_m[n], new_v[n] = outs

    for n, off, rows in PACK:
        adam(n, lambda o: o[0][off:off + rows, :].reshape(w[n].shape[1:]))
    adam("w_gate", lambda o: o[1].reshape(2, D, HP), 0)
    adam("w_up", lambda o: o[1].reshape(2, D, HP), 1)
    adam("w_down", lambda o: o[2])
    small_local = tuple((n, w[n].shape[1:]) for n, _ in SMALL)
    d_, m_, v_ = _adamw_small(_pack_small(w, small_local, depth), _pack_small(gw, small_local, depth),
                              _pack_small(m, small_local, depth), _pack_small(v, small_local, depth))
    delta.update(_unpack_small(d_, small_local, depth))
    new_m.update(_unpack_small(m_, small_local, depth))
    new_v.update(_unpack_small(v_, small_local, depth))

    return (loss, dx[None], *[gw[n] for n in WEIGHTS], *[delta[n] for n in WEIGHTS], *[new_m[n] for n in WEIGHTS],
            *[new_v[n] for n in WEIGHTS])
```

```python
import math

import jax
import jax.numpy as jnp
from jax import lax
from jax.experimental import pallas as pl
from jax.experimental.pallas import tpu as pltpu

F32 = jnp.float32
BF16 = jnp.bfloat16

D = 1024
HEADS = 8
NOPE = 64
ROPE = 32
VD = 64
QR = 384
KVR = 256
SGW = 256
CVW = 256
CHUNK = 128
DFF = 2816
EPS = 1e-6
ROPE_THETA = 10000.0
LOG2E = 1.4426950408889634
LN2 = 0.6931471805599453
QSCALE = (NOPE + ROPE) ** -0.5 * LOG2E
ZA = 768
ZB = 1280
QW = HEADS * 128
KVW = HEADS * 128 + HEADS * VD
NEG = -1e30
GC0 = 0.7978845608028654
GC1 = 0.044715

ADAM_LR = 0.001
ADAM_B1 = 0.9
ADAM_B2 = 0.999
ADAM_EPS = 1e-08
ADAM_WD = 0.01
ADAM_STEP = 10

V7X_VMEM_LIMIT = 52 * 1024 * 1024
ROW_TILE = 512
ATT_TILE = 512

NT = (((1,), (1,)), ((), ()))
TN = (((0,), (0,)), ((), ()))

HS = DFF // 4
HP = 768
DFFP = 4 * HP

PACK = (("w_in", 0, 488), ("w_out", 512, 256), ("w_ukv", 768, 64), ("w_uq", 832, 72))
CONV_ROW = 904
PACK_ROWS = 928
ROW_SHARDED = ("w_out",)
SMALL = (("mix_pre_g", (D,)), ("mix_post_g", (D,)), ("ffn_pre_g", (D,)), ("ffn_post_g", (D,)), ("q_norm_g", (QR,)),
         ("kv_norm_g", (KVR,)), ("sg_ln_g", (SGW,)), ("sg_ln_b", (SGW,)), ("w_sp", (4, CHUNK, CHUNK)), ("b_sp", (4, CHUNK)),
         ("conv_w", (3, CVW)), ("out_norm_g", (D,)))
WEIGHTS = ["mix_pre_g", "mix_post_g", "ffn_pre_g", "ffn_post_g", "w_in", "q_norm_g", "w_uq", "kv_norm_g", "w_ukv", "sg_ln_g",
           "sg_ln_b", "w_sp", "b_sp", "conv_w", "out_norm_g", "w_out", "w_gate", "w_up", "w_down"]

MESH_ID = pl.DeviceIdType.MESH
ANY = pl.BlockSpec(memory_space=pl.ANY)


def _cp(*sem):
    return pltpu.CompilerParams(dimension_semantics=sem, vmem_limit_bytes=V7X_VMEM_LIMIT)


def _sds(shape, dtype):
    return jax.ShapeDtypeStruct(shape, dtype)


def _row(tm, n):
    return pl.BlockSpec((tm, n), lambda i: (i, 0))


def _lyr(l, *shape):
    return pl.BlockSpec((None,) + shape, lambda *_: (l,) + (0,) * len(shape))


def _wl(a, l):
    return 0 if a.shape[0] == 1 else l


def _pcall(body, name, grid, ins, in_specs, out_specs, out_shape, sem, scratch=(), prevs=None):
    prevs = {k: v for k, v in (prevs or {}).items() if v is not None}
    order = sorted(prevs)
    n_in = len(ins)

    def wrapped(*refs):
        return body(*refs[:n_in], *refs[n_in + len(order):])

    return pl.pallas_call(
        wrapped, name=name, grid=grid, in_specs=list(in_specs) + [ANY] * len(order), out_specs=out_specs,
        out_shape=out_shape, scratch_shapes=list(scratch),
        input_output_aliases={n_in + i: k for i, k in enumerate(order)},
        compiler_params=_cp(*sem))(*ins, *[prevs[k] for k in order])


def _rms(x, g):
    r = lax.rsqrt(jnp.mean(x * x, axis=-1, keepdims=True) + EPS)
    return x * r * g


def _rms_bwd(x, g, dy):
    r = lax.rsqrt(jnp.mean(x * x, axis=-1, keepdims=True) + EPS)
    xh = x * r
    dg = jnp.sum(dy * xh, axis=0, keepdims=True)
    dxh = dy * g
    dx = r * (dxh - xh * jnp.mean(dxh * xh, axis=-1, keepdims=True))
    return dx, dg


def _gelu(x):
    return 0.5 * x * (1.0 + jnp.tanh(GC0 * (x + GC1 * x * x * x)))


def _gelu_grad(x):
    t = jnp.tanh(GC0 * (x + GC1 * x * x * x))
    return 0.5 * (1.0 + t) + 0.5 * x * (1.0 - t * t) * GC0 * (1.0 + 3.0 * GC1 * x * x)


def _rope(xb, c, s1, s2):
    return xb * c + pltpu.roll(xb, 112, 1) * s1 + pltpu.roll(xb, 16, 1) * s2


def _rope_bwd(dy, c, s1, s2):
    return dy * c + pltpu.roll(dy * s1, 16, 1) + pltpu.roll(dy * s2, 112, 1)


def _group_masks(shape):
    lane = lax.broadcasted_iota(jnp.int32, shape, 1)
    return [(lane >= 64 * g) & (lane < 64 * g + 64) for g in range(shape[1] // 64)]


def _group_mean(v, masks):
    out = jnp.zeros_like(v)
    for m in masks:
        s = jnp.sum(jnp.where(m, v, 0.0), axis=-1, keepdims=True) * (1.0 / 64.0)
        out = jnp.where(m, s, out)
    return out


def _pick_row(blk, idx):
    row = lax.broadcasted_iota(jnp.int32, blk.shape, 0)
    return jnp.sum(jnp.where(row == idx, blk, 0.0), axis=0, keepdims=True)


def _shift_down(y, k, first_rows):
    out = pltpu.roll(y, k, 0)
    row = lax.broadcasted_iota(jnp.int32, y.shape, 0)
    for idx in range(k):
        out = jnp.where(row == idx, first_rows[idx], out)
    return out


def _shift_up(y, k, last_rows):
    n = y.shape[0]
    out = pltpu.roll(y, n - k, 0)
    row = lax.broadcasted_iota(jnp.int32, y.shape, 0)
    for idx in range(k):
        out = jnp.where(row == n - k + idx, last_rows[idx], out)
    return out


def _tril_mask():
    r = lax.broadcasted_iota(jnp.int32, (CHUNK, CHUNK), 0)
    c = lax.broadcasted_iota(jnp.int32, (CHUNK, CHUNK), 1)
    return r >= c


def _sgu_forward(zu, zv, g_ln, b_ln, wc_bf, bsp, masks, cmasks):
    u = _gelu(zu)
    vv = _gelu(zv)
    mu = _group_mean(vv, masks)
    dv = vv - mu
    rs = lax.rsqrt(_group_mean(dv * dv, masks) + EPS)
    xh = dv * rs
    vn = xh * g_ln + b_ln
    chunks = []
    for ci in range(zu.shape[0] // CHUNK):
        vc = vn[ci * CHUNK:(ci + 1) * CHUNK, :]
        acc = bsp
        for g in range(4):
            acc = acc + jnp.dot(wc_bf[g], jnp.where(cmasks[g], vc, 0.0).astype(BF16), preferred_element_type=F32)
        chunks.append(acc)
    mixed = jnp.concatenate(chunks, axis=0) if len(chunks) > 1 else chunks[0]
    return u, vv, xh, rs, vn, mixed


def _conv_forward(gc, hh, prev_gc, prev_hh, first_tile, cw):
    yv = gc * hh
    prev = jnp.where(first_tile, 0.0, prev_gc * prev_hh)
    p6, p7 = _pick_row(prev, 6), _pick_row(prev, 7)
    sh1 = _shift_down(yv, 1, [p7])
    sh2 = _shift_down(yv, 2, [p6, p7])
    conv = sh2 * cw[0:1, :] + sh1 * cw[1:2, :] + yv * cw[2:3, :]
    return yv, sh1, sh2, conv


def _acc_init(step, *refs):
    @pl.when(step == 0)
    def _():
        for r in refs:
            r[...] = jnp.zeros(r.shape, r.dtype)


def _in_proj(x, p, l):
    t = x.shape[0]
    tm = min(ROW_TILE, t)

    def body(x_ref, g_ref, wa_ref, wb_ref, h_ref, za_ref, zb_ref):
        h = _rms(x_ref[...], g_ref[...]).astype(BF16)
        h_ref[...] = h
        za_ref[...] = jnp.dot(h, wa_ref[...], preferred_element_type=F32)
        zb_ref[...] = jnp.dot(h, wb_ref[...], preferred_element_type=F32)

    return _pcall(
        body, "in_proj", (t // tm,), [x, p["mix_pre_g"], p["w_in_a"], p["w_in_b"]],
        [_row(tm, D), _lyr(l, 1, D), _lyr(_wl(p["w_in_a"], l), D, ZA), _lyr(_wl(p["w_in_b"], l), D, ZB)],
        [_row(tm, D), _row(tm, ZA), _row(tm, ZB)],
        [_sds((t, D), BF16), _sds((t, ZA), F32), _sds((t, ZB), F32)], ("parallel",))


def _mla_prep(za, p, l, tabs):
    t = za.shape[0]
    tm = min(ROW_TILE, t)

    def body(z_ref, gq_ref, gkv_ref, wuq_ref, wukv_ref, c_ref, s1_ref, s2_ref, cq_ref, ckv_ref, q_ref, k_ref, v_ref):
        z = z_ref[...]
        cq = _rms(z[:, :QR], gq_ref[...]).astype(BF16)
        ckv = _rms(z[:, QR:QR + KVR], gkv_ref[...]).astype(BF16)
        cq_ref[...] = cq
        ckv_ref[...] = ckv
        c, s1, s2 = c_ref[...], s1_ref[...], s2_ref[...]
        kr = _rope(z[:, QR + KVR:], c, s1, s2)
        q = jnp.dot(cq, wuq_ref[...], preferred_element_type=F32)
        kv = jnp.dot(ckv, wukv_ref[...], preferred_element_type=F32)
        for h in range(HEADS):
            sl = slice(128 * h, 128 * h + 128)
            q_ref[:, sl] = (_rope(q[:, sl], c, s1, s2) * QSCALE).astype(BF16)
            k_ref[:, sl] = (kv[:, sl] + kr).astype(BF16)
        v_ref[...] = kv[:, QW:].astype(BF16)

    return _pcall(
        body, "mla_prep", (t // tm,), [za, p["q_norm_g"], p["kv_norm_g"], p["w_uq"], p["w_ukv"], *tabs],
        [_row(tm, ZA), _lyr(l, 1, QR), _lyr(l, 1, KVR), _lyr(_wl(p["w_uq"], l), QR, QW), _lyr(_wl(p["w_ukv"], l), KVR, KVW),
         _row(tm, 128), _row(tm, 128), _row(tm, 128)],
        [_row(tm, QR), _row(tm, KVR), _row(tm, QW), _row(tm, QW), _row(tm, HEADS * VD)],
        [_sds((t, QR), BF16), _sds((t, KVR), BF16), _sds((t, QW), BF16), _sds((t, QW), BF16),
         _sds((t, HEADS * VD), BF16)], ("parallel",))


def _att_tile(t):
    return min(ATT_TILE, max(t // 2, 128))


def _causal_keep(tq, i, j):
    row = lax.broadcasted_iota(jnp.int32, (tq, tq), 0) + i * tq
    col = lax.broadcasted_iota(jnp.int32, (tq, tq), 1) + j * tq
    return col <= row


def _attn_fwd(qs, k, v, fetch=None):
    t = qs.shape[0]
    tq = _att_tile(t)
    nq = t // tq
    rep = tq // 128
    mine, bufs, fetch_layer = fetch if fetch else ((), (), None)
    nb = len(mine)

    steps = [(i, j) for i in range(nq) for j in range(i + 1)]
    i_of = jnp.asarray([s[0] for s in steps], jnp.int32)
    j_of = jnp.asarray([s[1] for s in steps], jnp.int32)

    def body(i_ref, j_ref, q_ref, k_ref, v_ref, *refs):
        o_ref, lse_ref = refs[2 * nb:2 * nb + 2]
        m_s, l_s, acc_s = refs[3 * nb + 2:3 * nb + 5]
        step_no = pl.program_id(1)
        i, j = i_ref[step_no], j_ref[step_no]
        if fetch:
            start, hand_over, drain = _gather_ops(refs[:nb], refs[2 * nb + 2:3 * nb + 2], refs[3 * nb + 5:], fetch_layer)
            pr = pl.program_id(0)
            pl.when((pr == 0) & (step_no == 0))(start)
            pl.when((pr == HEADS // 2 - 1) & (step_no == 0))(hand_over)
            pl.when((pr == HEADS // 2 - 1) & (step_no == len(steps) - 1))(drain)

        @pl.when(j == 0)
        def _():
            m_s[...] = jnp.full(m_s.shape, NEG, F32)
            l_s[...] = jnp.zeros(l_s.shape, F32)
            acc_s[...] = jnp.zeros(acc_s.shape, F32)

        def step(masked):
            vv = v_ref[...]
            keep = _causal_keep(tq, i, j) if masked else None
            for hh in range(2):
                sl = slice(128 * hh, 128 * hh + 128)
                s = lax.dot_general(q_ref[:, sl], k_ref[:, sl], NT, preferred_element_type=F32)
                if masked:
                    s = jnp.where(keep, s, NEG)
                m_old = m_s[hh]
                m_new = jnp.maximum(m_old, jnp.max(s, axis=-1, keepdims=True))
                alpha = jnp.exp2(m_old - m_new)
                p = jnp.exp2(s - jnp.tile(m_new, (1, rep)))
                l_s[hh] = alpha * l_s[hh] + jnp.sum(p, axis=-1, keepdims=True)
                acc_s[hh] = alpha * acc_s[hh] + jnp.dot(p.astype(BF16), vv, preferred_element_type=F32)
                m_s[hh] = m_new

        @pl.when(j < i)
        def _():
            step(False)

        @pl.when(j == i)
        def _():
            step(True)
            lane = lax.broadcasted_iota(jnp.int32, (tq, 128), 1)
            o_ref[...] = jnp.where(lane < VD, acc_s[0] / l_s[0], acc_s[1] / l_s[1])
            for hh in range(2):
                lse_ref[hh] = (m_s[hh] + jnp.log2(l_s[hh]))[:, 0:1]

    grid_spec = pltpu.PrefetchScalarGridSpec(
        num_scalar_prefetch=2, grid=(HEADS // 2, len(steps)),
        in_specs=[pl.BlockSpec((tq, 256), lambda p, s, it, jt: (it[s], p)),
                  pl.BlockSpec((tq, 256), lambda p, s, it, jt: (jt[s], p)),
                  pl.BlockSpec((tq, 128), lambda p, s, it, jt: (jt[s], p))] + [ANY] * (2 * nb),
        out_specs=[pl.BlockSpec((tq, 128), lambda p, s, it, jt: (it[s], p)),
                   pl.BlockSpec((2, tq, 1), lambda p, s, it, jt: (p, it[s], 0))] + [ANY] * nb,
        scratch_shapes=[pltpu.VMEM((2, tq, 128), F32), pltpu.VMEM((2, tq, 128), F32), pltpu.VMEM((2, tq, 128), F32)]
        + ([pltpu.SemaphoreType.DMA((nb, 3))] * 4 if fetch else []))
    outs = pl.pallas_call(
        body, name="attn_fwd_fetch" if fetch else "attn_fwd", grid_spec=grid_spec,
        out_shape=[_sds((t, HEADS * VD), F32), _sds((HEADS, t, 1), F32)] + [_sds(b.shape, b.dtype) for b in bufs],
        input_output_aliases={5 + nb + b: 2 + b for b in range(nb)},
        compiler_params=_cp("arbitrary", "arbitrary"))(i_of, j_of, qs, k, v, *mine, *bufs)
    return outs[0], outs[1], list(outs[2:])


def _mixer_fwd(zb, ya, p, l):
    t = zb.shape[0]
    tm = min(ROW_TILE, t)
    hb = tm // 8

    def body(zb_ref, zprev_ref, ya_ref, gln_ref, bln_ref, wsp_ref, bsp_ref, cw_ref, go_ref, mix_ref, yb_ref, yc_ref):
        i = pl.program_id(0)
        masks = _group_masks((tm, SGW))
        cmasks = _group_masks((CHUNK, SGW))
        tril = _tril_mask()
        wc_bf = [jnp.where(tril, wsp_ref[g], 0.0).astype(BF16) for g in range(4)]
        u, _, _, _, _, mixed = _sgu_forward(zb_ref[:, 0:256], zb_ref[:, 256:512], gln_ref[...], bln_ref[...], wc_bf,
                                            bsp_ref[...], masks, cmasks)
        yb = u * mixed
        _, _, _, conv = _conv_forward(zb_ref[:, 768:1024], zb_ref[:, 1024:1280], zprev_ref[:, 768:1024],
                                      zprev_ref[:, 1024:1280], i == 0, cw_ref[...])
        yc = zb_ref[:, 512:768] * conv
        yb_ref[...] = yb
        yc_ref[...] = yc
        go = go_ref[...]
        mix_ref[:, 0:512] = _rms(ya_ref[...], go[:, 0:512]).astype(BF16)
        mix_ref[:, 512:768] = _rms(yb, go[:, 512:768]).astype(BF16)
        mix_ref[:, 768:1024] = _rms(yc, go[:, 768:1024]).astype(BF16)

    return _pcall(
        body, "mixer_fwd", (t // tm,),
        [zb, zb, ya, p["sg_ln_g"], p["sg_ln_b"], p["w_sp"], p["b_sp"], p["conv_w"], p["out_norm_g"]],
        [_row(tm, ZB), pl.BlockSpec((8, ZB), lambda i: (jnp.maximum(i * hb - 1, 0), 0)), _row(tm, 512),
         _lyr(l, 1, SGW), _lyr(l, 1, SGW), _lyr(l, 4, CHUNK, CHUNK), _lyr(l, CHUNK, SGW), _lyr(_wl(p["conv_w"], l), 3, CVW), _lyr(l, 1, D)],
        [_row(tm, D), _row(tm, SGW), _row(tm, CVW)],
        [_sds((t, D), BF16), _sds((t, SGW), F32), _sds((t, CVW), F32)], ("parallel",))


def _out_proj(mix, x, p, l):
    t = x.shape[0]
    tm = min(ROW_TILE, t)

    def body(mix_ref, w_ref, x_ref, gp_ref, gf_ref, o_ref, x2_ref, h2_ref):
        o = jnp.dot(mix_ref[...], w_ref[...], preferred_element_type=F32)
        o_ref[...] = o
        x2 = x_ref[...] + _rms(o, gp_ref[...])
        x2_ref[...] = x2
        h2_ref[...] = _rms(x2, gf_ref[...]).astype(BF16)

    return _pcall(
        body, "out_proj", (t // tm,), [mix, p["w_out"], x, p["mix_post_g"], p["ffn_pre_g"]],
        [_row(tm, D), _lyr(_wl(p["w_out"], l), D, D), _row(tm, D), _lyr(l, 1, D), _lyr(l, 1, D)],
        [_row(tm, D), _row(tm, D), _row(tm, D)],
        [_sds((t, D), F32), _sds((t, D), F32), _sds((t, D), BF16)], ("parallel",))


def _gu_all(l, which):
    return pl.BlockSpec((4, None, None, HP, D), lambda *_: (0, l, which, 0, 0))


def _down_all(l):
    return pl.BlockSpec((4, None, HP, D), lambda *_: (0, l, 0, 0))


def _ffn_up(h2, p, l):
    t = h2.shape[0]
    tm = min(ROW_TILE, t)

    def body(h_ref, wg_ref, wu_ref, a_ref, b_ref, s_ref):
        h = h_ref[...]
        a = lax.dot_general(h, wg_ref[...], NT, preferred_element_type=F32)
        b = lax.dot_general(h, wu_ref[...], NT, preferred_element_type=F32)
        a_ref[...] = a.astype(BF16)
        b_ref[...] = b.astype(BF16)
        s_ref[...] = (a * (1.0 / (1.0 + jnp.exp(-a))) * b).astype(BF16)

    blk = pl.BlockSpec((tm, HP), lambda k, i: (i, k))
    wblk = lambda which: pl.BlockSpec((None, None, None, HP, D), lambda k, i: (k, l, which, 0, 0))
    return _pcall(
        body, "ffn_up", (4, t // tm), [h2, p["w_gu"], p["w_gu"]],
        [pl.BlockSpec((tm, D), lambda k, i: (i, 0)), wblk(0), wblk(1)], [blk, blk, blk],
        [_sds((t, DFFP), BF16)] * 3, ("parallel", "parallel"))


def _ffn_down(s, x2, p, l):
    t = x2.shape[0]
    tm = min(ROW_TILE, t)

    def body(s_ref, w_ref, x_ref, g_ref, f_ref, x3_ref):
        f = jnp.dot(s_ref[:, 0:HP], w_ref[0], preferred_element_type=F32)
        for k in range(1, 4):
            f = f + jnp.dot(s_ref[:, k * HP:(k + 1) * HP], w_ref[k], preferred_element_type=F32)
        f_ref[...] = f
        x3_ref[...] = x_ref[...] + _rms(f, g_ref[...])

    return _pcall(
        body, "ffn_down", (t // tm,), [s, p["w_down"], x2, p["ffn_post_g"]],
        [_row(tm, DFFP), _down_all(l), _row(tm, D), _lyr(l, 1, D)], [_row(tm, D), _row(tm, D)],
        [_sds((t, D), F32), _sds((t, D), F32)], ("parallel",))


def _loss_head(y, target):
    t = y.shape[0]
    tm = min(ROW_TILE, t)

    def body(y_ref, t_ref, dy_ref, acc_ref):
        e = y_ref[...] - t_ref[...]
        dy_ref[...] = e * (1.0 / D)
        sq = jnp.sum(e * e, axis=0, keepdims=True)
        part = sq[:, 0:128]
        for b in range(1, D // 128):
            part = part + sq[:, 128 * b:128 * b + 128]
        _acc_init(pl.program_id(0), acc_ref)
        acc_ref[...] += part

    return _pcall(body, "loss_head", (t // tm,), [y, target], [_row(tm, D), _row(tm, D)],
                  [_row(tm, D), pl.BlockSpec((1, 128), lambda i: (0, 0))],
                  [_sds((t, D), F32), _sds((1, 128), F32)], ("arbitrary",))


def _ffn_down_bwd(dx3, sv, p, l, depth, gb):
    t = dx3.shape[0]
    tm = min(256, t)

    def body(dx_ref, f_ref, g_ref, w_ref, a_ref, b_ref, df_ref, da_ref, db_ref, dg_ref):
        _acc_init(pl.program_id(0), dg_ref)
        df, dg = _rms_bwd(f_ref[...], g_ref[...], dx_ref[...])
        dg_ref[...] += dg
        df = df.astype(BF16)
        df_ref[...] = df
        for k in range(4):
            sl = slice(k * HP, (k + 1) * HP)
            ds = lax.dot_general(df, w_ref[k], NT, preferred_element_type=F32)
            av = a_ref[:, sl].astype(F32)
            sig = 1.0 / (1.0 + jnp.exp(-av))
            da_ref[:, sl] = (ds * b_ref[:, sl].astype(F32) * (sig * (1.0 + av * (1.0 - sig)))).astype(BF16)
            db_ref[:, sl] = (ds * (av * sig)).astype(BF16)

    df, da, db, gb["ffn_post_g"] = _pcall(
        body, "ffn_down_bwd", (t // tm,), [dx3, sv["f"], p["ffn_post_g"], p["w_down"], sv["a"], sv["b"]],
        [_row(tm, D), _row(tm, D), _lyr(l, 1, D), _down_all(l), _row(tm, DFFP), _row(tm, DFFP)],
        [_row(tm, D), _row(tm, DFFP), _row(tm, DFFP), _lyr(l.g, 1, D)],
        [_sds((t, D), BF16), _sds((t, DFFP), BF16), _sds((t, DFFP), BF16), _sds((depth, 1, D), F32)], ("arbitrary",),
        prevs={3: gb.get("ffn_post_g")})
    return df, da, db


def _ffn_up_bwd(da, db, dx3, sv, p, l, depth, gb):
    t = dx3.shape[0]
    tm = min(256, t)

    def body(da_ref, db_ref, wg_ref, wu_ref, x_ref, dx3_ref, g_ref, dx2_ref, dg_ref):
        _acc_init(pl.program_id(0), dg_ref)
        dh = jnp.zeros((tm, D), F32)
        for k in range(4):
            sl = slice(k * HP, (k + 1) * HP)
            dh = dh + jnp.dot(da_ref[:, sl], wg_ref[k], preferred_element_type=F32)
            dh = dh + jnp.dot(db_ref[:, sl], wu_ref[k], preferred_element_type=F32)
        dx, dg = _rms_bwd(x_ref[...], g_ref[...], dh)
        dg_ref[...] += dg
        dx2_ref[...] = dx3_ref[...] + dx

    dx2, gb["ffn_pre_g"] = _pcall(
        body, "ffn_up_bwd", (t // tm,), [da, db, p["w_gu"], p["w_gu"], sv["x2"], dx3, p["ffn_pre_g"]],
        [_row(tm, DFFP), _row(tm, DFFP), _gu_all(l, 0), _gu_all(l, 1), _row(tm, D), _row(tm, D), _lyr(l, 1, D)],
        [_row(tm, D), _lyr(l.g, 1, D)], [_sds((t, D), F32), _sds((depth, 1, D), F32)], ("arbitrary",),
        prevs={1: gb.get("ffn_pre_g")})
    return dx2


def _out_proj_bwd(dx2, sv, p, l, depth, gb):
    t = dx2.shape[0]
    tm = min(ROW_TILE, t)

    def body(dx_ref, o_ref, g_ref, w_ref, do_ref, dmix_ref, dg_ref):
        _acc_init(pl.program_id(0), dg_ref)
        do, dg = _rms_bwd(o_ref[...], g_ref[...], dx_ref[...])
        dg_ref[...] += dg
        do = do.astype(BF16)
        do_ref[...] = do
        dmix_ref[...] = lax.dot_general(do, w_ref[...], NT, preferred_element_type=F32)

    do, dmix, gb["mix_post_g"] = _pcall(
        body, "out_proj_bwd", (t // tm,), [dx2, sv["o"], p["mix_post_g"], p["w_out"]],
        [_row(tm, D), _row(tm, D), _lyr(l, 1, D), _lyr(_wl(p["w_out"], l), D, D)], [_row(tm, D), _row(tm, D), _lyr(l.g, 1, D)],
        [_sds((t, D), BF16), _sds((t, D), F32), _sds((depth, 1, D), F32)], ("arbitrary",),
        prevs={2: gb.get("mix_post_g")})
    return do, dmix


def _mixer_bwd(dmix, sv, p, l, depth, gb):
    zb = sv["zb"]
    t = zb.shape[0]
    tm = min(ROW_TILE, t)
    hb = tm // 8
    last_blk = t // 8 - 1
    nsteps = t // tm

    def body(dmix_ref, ya_ref, yb_ref, yc_ref, zb_ref, zprev_ref, znext_ref, ycn_ref, dmn_ref,
             gln_ref, bln_ref, wsp_ref, bsp_ref, cw_ref, go_ref,
             dya_ref, dzb_ref, delta_ref, dgo_ref, dgln_ref, dbln_ref, dwsp_ref, dbsp_ref, dcw_ref):
        i = pl.program_id(0)
        _acc_init(i, dgo_ref, dgln_ref, dbln_ref, dwsp_ref, dbsp_ref, dcw_ref)
        go = go_ref[...]
        dmix = dmix_ref[...]

        ya = ya_ref[...]
        dya, dga = _rms_bwd(ya, go[:, 0:512], dmix[:, 0:512])
        dyb, dgb_ = _rms_bwd(yb_ref[...], go[:, 512:768], dmix[:, 512:768])
        dyc, dgc_ = _rms_bwd(yc_ref[...], go[:, 768:1024], dmix[:, 768:1024])
        dgo_ref[:, 0:512] += dga
        dgo_ref[:, 512:768] += dgb_
        dgo_ref[:, 768:1024] += dgc_
        dya = dya * LN2
        dya_ref[...] = dya.astype(BF16)
        prod = dya * ya
        hmasks = _group_masks((tm, 512))
        for h in range(HEADS):
            delta_ref[h] = jnp.sum(jnp.where(hmasks[h], prod, 0.0), axis=-1, keepdims=True)

        masks = _group_masks((tm, SGW))
        cmasks = _group_masks((CHUNK, SGW))
        tril = _tril_mask()
        wc_bf = [jnp.where(tril, wsp_ref[g], 0.0).astype(BF16) for g in range(4)]
        zu, zv = zb_ref[:, 0:256], zb_ref[:, 256:512]
        g_ln = gln_ref[...]
        u, _, xh, rs, vn, mixed = _sgu_forward(zu, zv, g_ln, bln_ref[...], wc_bf, bsp_ref[...], masks, cmasks)
        du = dyb * mixed
        dmixed = dyb * u
        dvn_chunks = []
        dbsp = jnp.zeros((CHUNK, SGW), F32)
        for ci in range(tm // CHUNK):
            rows = slice(ci * CHUNK, (ci + 1) * CHUNK)
            dm_c = dmixed[rows, :]
            vn_c = vn[rows, :].astype(BF16)
            dbsp = dbsp + dm_c
            dvn_c = jnp.zeros((CHUNK, SGW), F32)
            for g in range(4):
                dm_g = jnp.where(cmasks[g], dm_c, 0.0).astype(BF16)
                dw = lax.dot_general(dm_g, vn_c, NT, preferred_element_type=F32)
                dwsp_ref[g] += jnp.where(tril, dw, 0.0)
                dvn_c = dvn_c + lax.dot_general(wc_bf[g], dm_g, TN, preferred_element_type=F32)
            dvn_chunks.append(dvn_c)
        dbsp_ref[...] += dbsp
        dvn = jnp.concatenate(dvn_chunks, axis=0) if len(dvn_chunks) > 1 else dvn_chunks[0]
        dgln_ref[...] += jnp.sum(dvn * xh, axis=0, keepdims=True)
        dbln_ref[...] += jnp.sum(dvn, axis=0, keepdims=True)
        dxh = dvn * g_ln
        dvv = rs * (dxh - _group_mean(dxh, masks) - xh * _group_mean(dxh * xh, masks))
        dzb_ref[:, 0:256] = (du * _gelu_grad(zu)).astype(BF16)
        dzb_ref[:, 256:512] = (dvv * _gelu_grad(zv)).astype(BF16)

        cwv = cw_ref[...]
        gb_, gc, hh = zb_ref[:, 512:768], zb_ref[:, 768:1024], zb_ref[:, 1024:1280]
        yv, sh1, sh2, conv = _conv_forward(gc, hh, zprev_ref[:, 768:1024], zprev_ref[:, 1024:1280], i == 0, cwv)
        dconv = dyc * gb_
        dzb_ref[:, 512:768] = (dyc * conv).astype(BF16)
        dcw_ref[0:1, :] += jnp.sum(dconv * sh2, axis=0, keepdims=True)
        dcw_ref[1:2, :] += jnp.sum(dconv * sh1, axis=0, keepdims=True)
        dcw_ref[2:3, :] += jnp.sum(dconv * yv, axis=0, keepdims=True)
        dycn, _ = _rms_bwd(ycn_ref[...], go[:, 768:1024], dmn_ref[...])
        dconv_next = jnp.where(i == nsteps - 1, 0.0, dycn * znext_ref[:, 512:768])
        n0, n1 = _pick_row(dconv_next, 0), _pick_row(dconv_next, 1)
        dyv = dconv * cwv[2:3, :] + _shift_up(dconv, 1, [n0]) * cwv[1:2, :] + _shift_up(dconv, 2, [n0, n1]) * cwv[0:1, :]
        dzb_ref[:, 768:1024] = (dyv * hh).astype(BF16)
        dzb_ref[:, 1024:1280] = (dyv * gc).astype(BF16)

    prev_map = lambda i: (jnp.maximum(i * hb - 1, 0), 0)
    next_map = lambda i: (jnp.minimum((i + 1) * hb, last_blk), 0)
    names = ("out_norm_g", "sg_ln_g", "sg_ln_b", "w_sp", "b_sp_t", "conv_w")
    shapes = ((1, D), (1, SGW), (1, SGW), (4, CHUNK, CHUNK), (CHUNK, SGW), (3, CVW))
    outs = _pcall(
        body, "mixer_bwd", (nsteps,),
        [dmix, sv["ya"], sv["yb"], sv["yc"], zb, zb, zb, sv["yc"], dmix, p["sg_ln_g"], p["sg_ln_b"], p["w_sp"], p["b_sp"],
         p["conv_w"], p["out_norm_g"]],
        [_row(tm, D), _row(tm, 512), _row(tm, SGW), _row(tm, CVW), _row(tm, ZB),
         pl.BlockSpec((8, ZB), prev_map), pl.BlockSpec((8, ZB), next_map), pl.BlockSpec((8, CVW), next_map),
         pl.BlockSpec((8, 256), lambda i: (jnp.minimum((i + 1) * hb, last_blk), 3)),
         _lyr(l, 1, SGW), _lyr(l, 1, SGW), _lyr(l, 4, CHUNK, CHUNK), _lyr(l, CHUNK, SGW), _lyr(_wl(p["conv_w"], l), 3, CVW), _lyr(l, 1, D)],
        [_row(tm, 512), _row(tm, ZB), pl.BlockSpec((HEADS, tm, 1), lambda i: (0, i, 0))] + [_lyr(l.g, *s) for s in shapes],
        [_sds((t, 512), BF16), _sds((t, ZB), BF16), _sds((HEADS, t, 1), F32)] + [_sds((depth,) + s, F32) for s in shapes],
        ("arbitrary",), prevs={3 + n: gb.get(name) for n, name in enumerate(names)})
    for n, name in enumerate(names):
        gb[name] = outs[3 + n]
    return outs[0], outs[1], outs[2]


def _attn_bwd(qs, k, v, dya, lse, delta, ride=()):
    t = qs.shape[0]
    tq = _att_tile(t)
    nq = t // tq
    nb = len(ride)

    steps = [(j, i) for j in range(nq) for i in range(j, nq)]
    j_of = jnp.asarray([s[0] for s in steps], jnp.int32)
    i_of = jnp.asarray([s[1] for s in steps], jnp.int32)

    def body(j_ref, i_ref, q_ref, k_ref, v_ref, do_ref, lse_ref, dl_ref, *refs):
        dq_ref, dk_ref, dv_ref = refs[nb:nb + 3]
        dq_s, dk_s, dv_s = refs[2 * nb + 3:2 * nb + 6]
        step_no = pl.program_id(1)
        j, i = j_ref[step_no], i_ref[step_no]
        if ride:
            start, finish = _exchange_ops(refs[:nb], refs[nb + 3:2 * nb + 3], refs[2 * nb + 6:])
            pr = pl.program_id(0)
            pl.when((pr == 0) & (step_no == 0))(start)
            pl.when((pr == HEADS // 2 - 1) & (step_no == len(steps) - 1))(finish)

        @pl.when(step_no == 0)
        def _():
            dq_s[...] = jnp.zeros(dq_s.shape, F32)

        def step(masked):
            keep = _causal_keep(tq, 0, 0) if masked else None
            lane = lax.broadcasted_iota(jnp.int32, (tq, 128), 1)
            vv = v_ref[...]
            do = do_ref[...]
            rows = pl.ds(pl.multiple_of(i * tq, tq), tq)
            for hh in range(2):
                sl = slice(128 * hh, 128 * hh + 128)
                qq, kk = q_ref[:, sl], k_ref[:, sl]
                s = lax.dot_general(qq, kk, NT, preferred_element_type=F32)
                p = jnp.exp2(s - lse_ref[hh])
                if masked:
                    p = jnp.where(keep, p, 0.0)
                do_h = jnp.where((lane < VD) if hh == 0 else (lane >= VD), do, jnp.zeros_like(do))
                dp = lax.dot_general(do_h, vv, NT, preferred_element_type=F32)
                ds = (p * (dp - dl_ref[hh])).astype(BF16)
                dv_s[...] += lax.dot_general(p.astype(BF16), do_h, TN, preferred_element_type=F32)
                dk_s[:, sl] += lax.dot_general(ds, qq, TN, preferred_element_type=F32)
                dq_s[rows, sl] += jnp.dot(ds, kk, preferred_element_type=F32)

        @pl.when(i == j)
        def _():
            dk_s[...] = jnp.zeros(dk_s.shape, F32)
            dv_s[...] = jnp.zeros(dv_s.shape, F32)
            step(True)

        @pl.when(i > j)
        def _():
            step(False)

        @pl.when(i == nq - 1)
        def _():
            dk_ref[...] = dk_s[...].astype(BF16)
            dv_ref[...] = (dv_s[...] * LOG2E).astype(BF16)

        @pl.when(step_no == len(steps) - 1)
        def _():
            dq_ref[...] = dq_s[...].astype(BF16)

    qrow = lambda p, s, jt, it: (it[s], p)
    krow = lambda p, s, jt, it: (jt[s], p)
    col_spec = pl.BlockSpec((2, tq, 1), lambda p, s, jt, it: (p, it[s], 0))
    grid_spec = pltpu.PrefetchScalarGridSpec(
        num_scalar_prefetch=2, grid=(HEADS // 2, len(steps)),
        in_specs=[pl.BlockSpec((tq, 256), qrow), pl.BlockSpec((tq, 256), krow), pl.BlockSpec((tq, 128), krow),
                  pl.BlockSpec((tq, 128), qrow), col_spec, col_spec] + [ANY] * nb,
        out_specs=[pl.BlockSpec((t, 256), lambda p, s, jt, it: (0, p)), pl.BlockSpec((tq, 256), krow),
                   pl.BlockSpec((tq, 128), krow)] + [ANY] * nb,
        scratch_shapes=[pltpu.VMEM((t, 256), F32), pltpu.VMEM((tq, 256), F32), pltpu.VMEM((tq, 128), F32)]
        + ([pltpu.SemaphoreType.DMA((nb, 3))] * 2 if ride else []))
    outs = pl.pallas_call(
        body, name="attn_bwd_ride" if ride else "attn_bwd", grid_spec=grid_spec,
        out_shape=[_sds((t, QW), BF16), _sds((t, QW), BF16), _sds((t, HEADS * VD), BF16)]
        + [_sds((3,) + a.shape[1:], a.dtype) for a in ride],
        compiler_params=_cp("arbitrary", "arbitrary"))(j_of, i_of, qs, k, v, dya, lse, delta, *ride)
    return outs[0], outs[1], outs[2], list(outs[3:])


def _mla_prep_bwd(dqs, dk, dv, sv, p, l, depth, gb, tabs):
    za = sv["za"]
    t = za.shape[0]
    tm = min(ROW_TILE, t)

    def body(dq_ref, dk_ref, dv_ref, z_ref, gq_ref, gkv_ref, wuq_ref, wukv_ref, c_ref, s1_ref, s2_ref,
             dza_ref, dqp_ref, dkv_ref, dgq_ref, dgkv_ref):
        _acc_init(pl.program_id(0), dgq_ref, dgkv_ref)
        c, s1, s2 = c_ref[...], s1_ref[...], s2_ref[...]
        lane = lax.broadcasted_iota(jnp.int32, (tm, 128), 1)
        rope_lanes = (lane >= NOPE) & (lane < NOPE + ROPE)
        dkr = jnp.zeros((tm, 128), F32)
        for h in range(HEADS):
            sl = slice(128 * h, 128 * h + 128)
            dqp_ref[:, sl] = _rope_bwd(dq_ref[:, sl].astype(F32) * QSCALE, c, s1, s2).astype(BF16)
            dkh = dk_ref[:, sl]
            dkv_ref[:, sl] = dkh
            dkr = dkr + jnp.where(rope_lanes, dkh.astype(F32), 0.0)
        dkv_ref[:, QW:] = dv_ref[...]
        z = z_ref[...]
        dcq = lax.dot_general(dqp_ref[...], wuq_ref[...], NT, preferred_element_type=F32)
        dzq, dgq = _rms_bwd(z[:, :QR], gq_ref[...], dcq)
        dckv = lax.dot_general(dkv_ref[...], wukv_ref[...], NT, preferred_element_type=F32)
        dzkv, dgkv = _rms_bwd(z[:, QR:QR + KVR], gkv_ref[...], dckv)
        dgq_ref[...] += dgq
        dgkv_ref[...] += dgkv
        dza_ref[:, :QR] = dzq.astype(BF16)
        dza_ref[:, QR:QR + KVR] = dzkv.astype(BF16)
        dza_ref[:, QR + KVR:] = _rope_bwd(dkr, c, s1, s2).astype(BF16)

    dza, dqp, dkv, gb["q_norm_g"], gb["kv_norm_g"] = _pcall(
        body, "mla_prep_bwd", (t // tm,),
        [dqs, dk, dv, za, p["q_norm_g"], p["kv_norm_g"], p["w_uq"], p["w_ukv"], *tabs],
        [_row(tm, QW), _row(tm, QW), _row(tm, HEADS * VD), _row(tm, ZA), _lyr(l, 1, QR), _lyr(l, 1, KVR),
         _lyr(_wl(p["w_uq"], l), QR, QW), _lyr(_wl(p["w_ukv"], l), KVR, KVW), _row(tm, 128), _row(tm, 128), _row(tm, 128)],
        [_row(tm, ZA), _row(tm, QW), _row(tm, KVW), _lyr(l.g, 1, QR), _lyr(l.g, 1, KVR)],
        [_sds((t, ZA), BF16), _sds((t, QW), BF16), _sds((t, KVW), BF16), _sds((depth, 1, QR), F32),
         _sds((depth, 1, KVR), F32)], ("arbitrary",), prevs={3: gb.get("q_norm_g"), 4: gb.get("kv_norm_g")})
    return dza, dqp, dkv


def _in_proj_bwd(dza, dzb, dx2, sv, p, l, depth, gb):
    t = dx2.shape[0]
    tm = min(ROW_TILE, t)

    def body(dza_ref, dzb_ref, wa_ref, wb_ref, x_ref, dx2_ref, g_ref, dx_ref, dg_ref):
        _acc_init(pl.program_id(0), dg_ref)
        dh = (lax.dot_general(dza_ref[...], wa_ref[...], NT, preferred_element_type=F32)
              + lax.dot_general(dzb_ref[...], wb_ref[...], NT, preferred_element_type=F32))
        dx, dg = _rms_bwd(x_ref[...], g_ref[...], dh)
        dg_ref[...] += dg
        dx_ref[...] = dx2_ref[...] + dx

    dx, gb["mix_pre_g"] = _pcall(
        body, "in_proj_bwd", (t // tm,), [dza, dzb, p["w_in_a"], p["w_in_b"], sv["x"], dx2, p["mix_pre_g"]],
        [_row(tm, ZA), _row(tm, ZB), _lyr(_wl(p["w_in_a"], l), D, ZA), _lyr(_wl(p["w_in_b"], l), D, ZB), _row(tm, D), _row(tm, D), _lyr(l, 1, D)],
        [_row(tm, D), _lyr(l.g, 1, D)], [_sds((t, D), F32), _sds((depth, 1, D), F32)], ("arbitrary",),
        prevs={1: gb.get("mix_pre_g")})
    return dx


def _mm_tn(a, b, tn, name, l, depth, gb):
    t, k = a.shape
    n = b.shape[1]
    tt = min(ROW_TILE, t)

    def body(a_ref, b_ref, o_ref):
        _acc_init(pl.program_id(1), o_ref)
        o_ref[...] += lax.dot_general(a_ref[...], b_ref[...], TN, preferred_element_type=F32)

    gb[name] = _pcall(
        body, "d" + name, (n // tn, t // tt), [a, b],
        [pl.BlockSpec((tt, k), lambda j, s: (s, 0)), pl.BlockSpec((tt, tn), lambda j, s: (s, j))],
        pl.BlockSpec((None, k, tn), lambda j, s: (l.g, 0, j)), _sds((depth, k, n), F32), ("parallel", "arbitrary"),
        prevs={0: gb.get(name)})


def _dw_ffn(a, b, kind, l, depth, gb):
    t = a.shape[0]
    tt = min(ROW_TILE, t)
    nsteps = t // tt

    def body(a_ref, b_ref, o_ref, acc):
        s = pl.program_id(0)
        _acc_init(s, acc)
        acc[...] += lax.dot_general(a_ref[...], b_ref[...], TN, preferred_element_type=F32)

        @pl.when(s == nsteps - 1)
        def _():
            for k in range(4):
                o_ref[k] = acc[k * HP:(k + 1) * HP, :].astype(BF16)

    rows = lambda n: pl.BlockSpec((tt, n), lambda s: (s, 0))
    if kind == "down":
        name = "down"
        out_spec = pl.BlockSpec((4, None, HP, D), lambda s: (0, l.g, 0, 0))
        out_shape = _sds((4, depth, HP, D), BF16)
    else:
        which = 0 if kind == "gate" else 1
        name = "gu"
        out_spec = pl.BlockSpec((4, None, None, HP, D), lambda s: (0, l.g, which, 0, 0))
        out_shape = _sds((4, depth, 2, HP, D), BF16)
    gb[name] = _pcall(body, "dw_" + kind, (nsteps,), [a, b], [rows(DFFP), rows(D)], out_spec, out_shape, ("arbitrary",),
                      scratch=[pltpu.VMEM((DFFP, D), F32)], prevs={0: gb.get(name)})


def _ffn_views(bufs):
    return {"w_gu": bufs[1], "w_down": bufs[2].reshape(bufs[2].shape[:2] + (HP, D))}


def _layer_fwd(x, p, l, tabs, fetch):
    h1, za, zb = _in_proj(x, p, l)
    cqn, ckvn, qs, k, v = _mla_prep(za, p, l, tabs)
    ya, lse, bufs = _attn_fwd(qs, k, v, fetch)
    if fetch:
        p = {**p, **_ffn_views(bufs)}
    mix, yb, yc = _mixer_fwd(zb, ya, p, l)
    o, x2, h2 = _out_proj(mix, x, p, l)
    a, b, s = _ffn_up(h2, p, l)
    f, x3 = _ffn_down(s, x2, p, l)
    saved = dict(x=x, h1=h1, za=za, zb=zb, cqn=cqn, ckvn=ckvn, qs=qs, k=k, v=v, ya=ya, lse=lse, mix=mix, yb=yb, yc=yc,
                 o=o, x2=x2, h2=h2, a=a, b=b, s=s, f=f)
    return x3, saved, bufs if fetch else None


class _Layer(int):
    def __new__(cls, l, g):
        obj = int.__new__(cls, l)
        obj.g = g
        return obj


def _layer_bwd(dx3, p, sv, l, depth, gb, tabs, ride=()):
    df, da, db = _ffn_down_bwd(dx3, sv, p, l, depth, gb)
    _dw_ffn(sv["s"], df, "down", l, depth, gb)
    dx2 = _ffn_up_bwd(da, db, dx3, sv, p, l, depth, gb)
    _dw_ffn(da, sv["h2"], "gate", l, depth, gb)
    _dw_ffn(db, sv["h2"], "up", l, depth, gb)
    do, dmix = _out_proj_bwd(dx2, sv, p, l, depth, gb)
    _mm_tn(sv["mix"], do, D, "w_out", l, depth, gb)
    dya, dzb, delta = _mixer_bwd(dmix, sv, p, l, depth, gb)
    dqs, dk, dv, sent = _attn_bwd(sv["qs"], sv["k"], sv["v"], dya, sv["lse"], delta, ride)
    dza, dqp, dkv = _mla_prep_bwd(dqs, dk, dv, sv, p, l, depth, gb, tabs)
    _mm_tn(sv["cqn"], dqp, QW, "w_uq", l, depth, gb)
    _mm_tn(sv["ckvn"], dkv, KVW, "w_ukv", l, depth, gb)
    _mm_tn(sv["h1"], dza, ZA, "w_in_a", l, depth, gb)
    _mm_tn(sv["h1"], dzb, ZB, "w_in_b", l, depth, gb)
    return _in_proj_bwd(dza, dzb, dx2, sv, p, l, depth, gb), sent


def _rope_tables(positions):
    inv_freq = 1.0 / (ROPE_THETA ** (jnp.arange(0, ROPE // 2, dtype=F32) / (ROPE // 2)))
    ang = positions.astype(F32)[:, None] * inv_freq
    cos, sin = jnp.cos(ang), jnp.sin(ang)
    t = positions.shape[0]
    one, zero = jnp.ones((t, 64), F32), jnp.zeros((t, 16), F32)
    c = jnp.concatenate([one, cos, cos, one[:, :32]], axis=1)
    s1 = jnp.concatenate([zero, zero, zero, zero, -sin, zero, zero, zero], axis=1)
    s2 = jnp.concatenate([zero, zero, zero, zero, zero, sin, zero, zero], axis=1)
    return c, s1, s2


def _mixer_weight_params(full):
    w_in = full["w_in"]
    depth = w_in.shape[0]
    zpad = lambda n: jnp.zeros((depth, D, n), w_in.dtype)
    kv = full["w_ukv"].reshape(depth, KVR, HEADS, NOPE + VD)
    return {
        "w_in_a": jnp.concatenate([w_in[:, :, :640], zpad(64), w_in[:, :, 640:672], zpad(32)], axis=2),
        "w_in_b": w_in[:, :, 672:],
        "w_uq": jnp.pad(full["w_uq"].reshape(depth, QR, HEADS, NOPE + ROPE),
                        ((0, 0), (0, 0), (0, 0), (0, 32))).reshape(depth, QR, QW),
        "w_ukv": jnp.concatenate([jnp.pad(kv[..., :NOPE], ((0, 0), (0, 0), (0, 0), (0, 64))).reshape(depth, KVR, QW),
                                  kv[..., NOPE:].reshape(depth, KVR, HEADS * VD)], axis=2),
        "w_out": full["w_out"], "conv_w": full["conv_w"],
    }


def _small_params(w):
    p = {"w_sp": w["w_sp"], "b_sp": jnp.repeat(jnp.swapaxes(w["b_sp"], 1, 2), 64, axis=2)}
    for n in ("mix_pre_g", "mix_post_g", "ffn_pre_g", "ffn_post_g", "q_norm_g", "kv_norm_g", "sg_ln_g", "sg_ln_b",
              "out_norm_g"):
        p[n] = w[n][:, None, :]
    return p


def _natural_grads(gb):
    depth = gb["w_in_a"].shape[0]
    ga, kv = gb["w_in_a"], gb["w_ukv"]
    out = {
        "w_in": jnp.concatenate([ga[:, :, :640], ga[:, :, 704:736], gb["w_in_b"]], axis=2),
        "w_uq": gb["w_uq"].reshape(depth, QR, HEADS, 128)[..., :NOPE + ROPE].reshape(depth, QR, HEADS * (NOPE + ROPE)),
        "w_ukv": jnp.concatenate([kv[:, :, :QW].reshape(depth, KVR, HEADS, 128)[..., :NOPE],
                                  kv[:, :, QW:].reshape(depth, KVR, HEADS, VD)], axis=3).reshape(depth, KVR, -1),
        "b_sp": jnp.swapaxes(gb["b_sp_t"].reshape(depth, CHUNK, 4, 64).sum(axis=-1), 1, 2),
    }
    for n in ("w_out", "w_sp", "conv_w"):
        out[n] = gb[n]
    for n in ("mix_pre_g", "mix_post_g", "ffn_pre_g", "ffn_post_g", "q_norm_g", "kv_norm_g", "sg_ln_g", "sg_ln_b",
              "out_norm_g"):
        out[n] = gb[n][:, 0, :]
    return out


def _local_step(x, positions, target, small, mine, bufs, shard_shapes, fetch=True, front=None):
    depth = small["w_sp"].shape[0]
    tabs = _rope_tables(positions)
    ps = _small_params(small)
    saved, mixer_w = [], []
    for l in range(depth):
        mixer_w.append(_mixer_weight_params(_unpack_weights(bufs[0], l, shard_shapes)))
        p = {**ps, **mixer_w[l], **_ffn_views(bufs)}
        x, sv, fetched = _layer_fwd(x, p, l, tabs, (mine, bufs, l + 1) if fetch and l + 1 < depth else None)
        bufs = fetched or bufs
        saved.append(sv)
    dx, acc = _loss_head(x, target)
    loss = (0.5 / D) * jnp.sum(acc)
    nbatch = depth // 2
    gbs = [{} for _ in range(nbatch)]
    fronts, sent = [None] * nbatch, [None] * nbatch
    ride = ()
    for l in reversed(range(depth)):
        b = l // 2
        dx, got = _layer_bwd(dx, {**ps, **mixer_w[l], **_ffn_views(bufs)}, saved[l], _Layer(l, l % 2), 2, gbs[b], tabs, ride)
        if ride:
            sent[b + 1], ride = got, ()
        if l % 2 == 0:
            done = (_natural_grads(gbs[b]), gbs[b]["gu"], gbs[b]["down"])
            fronts[b] = front(b, *done) if front else done
            if front and b > 0:
                ride = tuple(fronts[b])
    return loss, dx, fronts, sent


def _place():
    x, y, c = lax.axis_index("x"), lax.axis_index("y"), lax.axis_index("c")
    chips = [(1 - x, y), (x, 1 - y), (1 - x, 1 - y)]
    return x, y, c, 2 * x + y, chips


def _remote(src, dst, send_sem, recv_sem, to):
    return pltpu.make_async_remote_copy(src_ref=src, dst_ref=dst, send_sem=send_sem, recv_sem=recv_sem, device_id=to,
                                        device_id_type=MESH_ID)


def _gather_ops(mine_refs, out_refs, sems, l):
    send_sems, recv_sems, fsend_sems, frecv_sems = sems
    x, y, c, k, chips = _place()
    sib = (x, y, 1 - c)
    pairs = [(b, n) for n in range(3) for b in range(len(mine_refs))]

    def slot(n):
        return 2 * chips[n][0] + chips[n][1]

    def ici(b, n, dst_chip):
        return _remote(mine_refs[b].at[l, c], out_refs[b].at[dst_chip, l, c], send_sems.at[b, n], recv_sems.at[b, n],
                       (*chips[n], c))

    def d2d(b, n, half):
        piece = out_refs[b].at[slot(n), l, half]
        return _remote(piece, piece, fsend_sems.at[b, n], frecv_sems.at[b, n], sib)

    def start():
        for b, n in pairs:
            ici(b, n, k).start()

    def hand_over():
        for b, n in pairs:
            ici(b, n, slot(n)).wait_recv()
            d2d(b, n, c).start()

    def drain():
        for b, n in pairs:
            d2d(b, n, 1 - c).wait_recv()
        for b, n in pairs:
            ici(b, n, k).wait_send()
            d2d(b, n, c).wait_send()

    return start, hand_over, drain


def _gather_first_layer(mine):
    nb = len(mine)

    def body(*refs):
        start, hand_over, drain = _gather_ops(refs[:nb], refs[nb:2 * nb], refs[2 * nb:], 0)
        start()
        hand_over()
        drain()

    return pl.pallas_call(
        body, name="gather_first_layer", in_specs=[ANY] * nb, out_specs=[ANY] * nb,
        out_shape=[_sds((4,) + a.shape, a.dtype) for a in mine],
        scratch_shapes=[pltpu.SemaphoreType.DMA((nb, 3))] * 4)(*mine)


def _swap_halves(bigs, wholes=()):
    nb, n = len(bigs), len(bigs) + len(wholes)

    def body(*refs):
        src, dst = refs[:n], refs[n:2 * n]
        send_sems, recv_sems = refs[2 * n:]
        x, y, c, _, _ = _place()
        sib = (x, y, 1 - c)
        cps = [_remote(src[b].at[:, 1 - c] if b < nb else src[b], dst[b], send_sems.at[b], recv_sems.at[b], sib)
               for b in range(n)]
        for cp in cps:
            cp.start()
        for cp in cps:
            cp.wait()

    return pl.pallas_call(
        body, name="swap_halves", in_specs=[ANY] * n, out_specs=[ANY] * n,
        out_shape=[_sds((4,) + a.shape[2:], a.dtype) for a in bigs] + [_sds(a.shape, a.dtype) for a in wholes],
        scratch_shapes=[pltpu.SemaphoreType.DMA((n,))] * 2)(*bigs, *wholes)


def _sum_tile(r):
    return max(cand for cand in range(16, 641, 16) if r % cand == 0)


def _pair_sum(big, rbig, c):
    _, _, r, w = big.shape
    tr = _sum_tile(r)

    def body(c_ref, big_ref, rbig_ref, p_ref):
        p_ref[...] = (big_ref[...].astype(F32) + rbig_ref[...].astype(F32)).astype(BF16)

    grid_spec = pltpu.PrefetchScalarGridSpec(
        num_scalar_prefetch=1, grid=(4, r // tr),
        in_specs=[pl.BlockSpec((None, None, tr, w), lambda j, i, cr: (j, cr[0], i, 0)),
                  pl.BlockSpec((None, tr, w), lambda j, i, cr: (j, i, 0))],
        out_specs=pl.BlockSpec((None, tr, w), lambda j, i, cr: (j, i, 0)))
    return pl.pallas_call(body, name="pair_sum", grid_spec=grid_spec, out_shape=_sds((4, r, w), BF16),
                          compiler_params=_cp("parallel", "parallel"))(c, big, rbig)


def _small_sum(parts):
    n, ns, _ = parts.shape

    def body(p_ref, o_ref):
        s = p_ref[0]
        for j in range(1, n):
            s = s + p_ref[j]
        o_ref[...] = s

    return pl.pallas_call(body, name="small_sum", out_shape=_sds((ns, 128), F32))(parts)


def _exchange_ops(p_refs, rb_refs, sems, small=None):
    send_sems, recv_sems = sems[0], sems[1]
    nb = len(p_refs)
    x, y, c, k, chips = _place()

    def copies(landing):
        out = []
        for n, (cx, cy) in enumerate(chips):
            to, kj = (cx, cy, c), 2 * cx + cy
            for b in range(nb):
                out.append(_remote(p_refs[b].at[k if landing else kj], rb_refs[b].at[n], send_sems.at[b, n],
                                   recv_sems.at[b, n], to))
            if small:
                out.append(_remote(small[0], small[1].at[kj if landing else k], send_sems.at[nb, n], recv_sems.at[nb, n], to))
        return out

    def local():
        return pltpu.make_async_copy(small[0], small[1].at[k], sems[2])

    def start():
        if small:
            local().start()
        for cp in copies(False):
            cp.start()

    def finish():
        for cp in copies(True):
            cp.wait_recv()
        for cp in copies(False):
            cp.wait_send()
        if small:
            local().wait()

    return start, finish


def _chip_exchange(ps, small):
    nb = len(ps)
    ns = small.shape[0]

    def body(*refs):
        start, finish = _exchange_ops(refs[:nb], refs[nb + 1:2 * nb + 1], refs[2 * nb + 2:], (refs[nb], refs[2 * nb + 1]))
        start()
        finish()

    return pl.pallas_call(
        body, name="chip_exchange", in_specs=[ANY] * (nb + 1), out_specs=[ANY] * (nb + 1),
        out_shape=[_sds((3,) + a.shape[1:], a.dtype) for a in ps] + [_sds((4, ns, 128), small.dtype)],
        scratch_shapes=[pltpu.SemaphoreType.DMA((nb + 1, 3))] * 2 + [pltpu.SemaphoreType.DMA(())])(*ps, small)


def _chip_sum(p, rb, chip):
    _, r, w = p.shape
    tr = _sum_tile(r)

    def body(k_ref, p_ref, rb_ref, o_ref):
        acc = p_ref[...].astype(F32)
        for j in range(3):
            acc = acc + rb_ref[j].astype(F32)
        o_ref[...] = acc

    grid_spec = pltpu.PrefetchScalarGridSpec(
        num_scalar_prefetch=1, grid=(r // tr,),
        in_specs=[pl.BlockSpec((None, tr, w), lambda i, kr: (kr[0], i, 0)), pl.BlockSpec((3, tr, w), lambda i, kr: (0, i, 0))],
        out_specs=pl.BlockSpec((tr, w), lambda i, kr: (i, 0)))
    return pl.pallas_call(body, name="chip_sum", grid_spec=grid_spec, out_shape=_sds((r, w), F32),
                          compiler_params=_cp("parallel"))(chip, p, rb)


def _send_to_sibling(reds):
    nb = len(reds)

    def body(*refs):
        red_refs, out_refs = refs[:nb], refs[nb:2 * nb]
        send_sems, recv_sems = refs[2 * nb:]
        x, y, c, _, _ = _place()
        cps = [_remote(red_refs[b], out_refs[b], send_sems.at[b], recv_sems.at[b], (x, y, 1 - c)) for b in range(nb)]
        for cp in cps:
            cp.start()
        for cp in cps:
            cp.wait()

    return pl.pallas_call(
        body, name="send_to_sibling", in_specs=[ANY] * nb, out_specs=[ANY] * nb,
        out_shape=[_sds(a.shape, a.dtype) for a in reds], scratch_shapes=[pltpu.SemaphoreType.DMA((nb,))] * 2)(*reds)


def _adam_math(w, g, m, v):
    nm = ADAM_B1 * m + (1.0 - ADAM_B1) * g
    nv = ADAM_B2 * v + (1.0 - ADAM_B2) * (g * g)
    m_hat = nm / (1.0 - ADAM_B1 ** ADAM_STEP)
    v_hat = nv / (1.0 - ADAM_B2 ** ADAM_STEP)
    return -ADAM_LR * (m_hat / (jnp.sqrt(v_hat) + ADAM_EPS) + ADAM_WD * w), nm, nv


def _adamw_shard(w, m, v, owns, others, c, name, pick=None):
    depth, r, n = w.shape
    nbatch = len(owns)
    tr = max(cand for cand in range(8, min(r, 256) + 1, 8) if r % cand == 0)
    npad = owns[0].shape[-1]

    def body(c_ref, w_ref, m_ref, v_ref, *refs):
        g_ref, d_ref, nm_ref, nv_ref = refs[2 * nbatch:]
        l = pl.program_id(0)
        mine = (l % 2) == c_ref[0]
        g = jnp.where(mine, refs[0][...], refs[nbatch][...])
        for b in range(1, nbatch):
            g = jnp.where(l // 2 == b, jnp.where(mine, refs[b][...], refs[nbatch + b][...]), g)
        g = g[:, :n]
        g_ref[...] = g
        d_ref[...], nm_ref[...], nv_ref[...] = _adam_math(w_ref[...], g, m_ref[...], v_ref[...])

    blk = pl.BlockSpec((None, tr, n), lambda l, i, cr: (l, i, 0))
    if pick is None:
        gblk = pl.BlockSpec((tr, npad), lambda l, i, cr: (i, 0))
    else:
        gblk = pl.BlockSpec((None, tr, npad), lambda l, i, cr: (pick, i, 0))
    grid_spec = pltpu.PrefetchScalarGridSpec(num_scalar_prefetch=1, grid=(depth, r // tr),
                                             in_specs=[blk] * 3 + [gblk] * (2 * nbatch), out_specs=[blk] * 4)
    return pl.pallas_call(body, name=name, grid_spec=grid_spec, out_shape=[_sds(w.shape, F32)] * 4,
                          compiler_params=_cp("parallel", "parallel"))(c, w, m, v, *owns, *others)


def _pad_ffn_shards(w_gate, w_up, w_down):
    depth = w_gate.shape[0]
    hr = HP // 2

    def gu_body(g_ref, u_ref, o_ref):
        for which, ref in enumerate((g_ref, u_ref)):
            o_ref[which, 0:HS, :] = ref[...].astype(BF16)
            o_ref[which, HS:HP, :] = jnp.zeros((HP - HS, D), BF16)

    blk = pl.BlockSpec((None, HS, D), lambda l: (l, 0, 0))
    gu = pl.pallas_call(
        gu_body, name="pad_gate_up", grid=(depth,), in_specs=[blk, blk],
        out_specs=pl.BlockSpec((None, 2, HP, D), lambda l: (l, 0, 0, 0)),
        out_shape=_sds((depth, 2, HP, D), BF16), compiler_params=_cp("parallel"))(w_gate, w_up)

    def down_body(w_ref, o_ref):
        o_ref[0] = w_ref[0:hr, :].astype(BF16)
        o_ref[1, 0:HS - hr, :] = w_ref[hr:HS, :].astype(BF16)
        o_ref[1, HS - hr:hr, :] = jnp.zeros((HP - HS, D), BF16)

    down = pl.pallas_call(
        down_body, name="pad_down", grid=(depth,), in_specs=[pl.BlockSpec((None, HS, D), lambda l: (l, 0, 0))],
        out_specs=pl.BlockSpec((None, 2, hr, D), lambda l: (l, 0, 0, 0)),
        out_shape=_sds((depth, 2, hr, D), BF16), compiler_params=_cp("parallel"))(w_down)
    return gu, down


def _adamw_small(w, g, m, v):
    r = w.shape[0]
    tr = max(cand for cand in range(8, 513, 8) if r % cand == 0)

    def body(w_ref, g_ref, m_ref, v_ref, d_ref, nm_ref, nv_ref):
        d_ref[...], nm_ref[...], nv_ref[...] = _adam_math(w_ref[...], g_ref[...], m_ref[...], v_ref[...])

    blk = pl.BlockSpec((tr, 128), lambda i: (i, 0))
    return pl.pallas_call(body, name="adamw_small", grid=(r // tr,), in_specs=[blk] * 4, out_specs=[blk] * 3,
                          out_shape=[_sds(w.shape, F32)] * 3, compiler_params=_cp("parallel"))(w, g, m, v)


def _to_pack(a, name):
    depth = a.shape[0]
    if name in ROW_SHARDED:
        return jnp.swapaxes(a.reshape(depth, 4, -1, D), 0, 1)
    return jnp.transpose(a.reshape(depth, a.shape[1], 4, a.shape[2] // 4), (2, 0, 1, 3)).reshape(4, depth, -1, D)


def _pack_rows(parts, lead, dtype, tail=None):
    pieces, at = [], 0
    for n, off, rows in PACK:
        if off > at:
            pieces.append(jnp.zeros(lead + (off - at, D), dtype))
        pieces.append(parts[n].astype(dtype))
        at = off + rows
    if tail is not None:
        pieces.append(tail)
        at += tail.shape[-2]
    pieces.append(jnp.zeros(lead + (PACK_ROWS - at, D), dtype))
    return jnp.concatenate(pieces, axis=len(lead))


def _pack_weight_shards(sh):
    depth = sh["w_in"].shape[0]
    parts = {n: sh[n].reshape(depth, rows, D) for n, _, rows in PACK}
    conv = lax.bitcast_convert_type(sh["conv_w"].reshape(depth, 3 * 64), BF16).reshape(depth, 1, 384)
    flat = _pack_rows(parts, (depth,), BF16, tail=jnp.pad(conv, ((0, 0), (0, 0), (0, D - 384))))
    return flat.reshape(depth, 2, PACK_ROWS // 2, D)


def _unpack_weights(gathered, l, shard_shapes):
    depth = 1
    flat = gathered[:, l].reshape(4, 1, PACK_ROWS, D)
    full = {}
    for n, off, rows in PACK:
        shp = shard_shapes[n][1:]
        piece = flat[:, :, off:off + rows, :].reshape((4, depth) + shp)
        if n in ROW_SHARDED:
            full[n] = jnp.transpose(piece, (1, 0, 2, 3)).reshape(depth, 4 * shp[0], shp[1])
        else:
            full[n] = jnp.transpose(piece, (1, 2, 0, 3)).reshape(depth, shp[0], 4 * shp[1])
    conv = lax.bitcast_convert_type(flat[:, :, CONV_ROW, :384].reshape(4, depth, 192, 2), F32)
    full["conv_w"] = jnp.transpose(conv.reshape(4, depth, 3, 64), (1, 2, 0, 3)).reshape(depth, 3, CVW)
    return full


def _pack_grad_shards(g):
    depth = g["w_in"].shape[0]
    return _pack_rows({n: _to_pack(g[n], n) for n, _, _ in PACK}, (4, depth), BF16)


def _pack_small(arrs, names_shapes, depth):
    flat = jnp.concatenate([arrs[n].reshape(depth, -1) for n, _ in names_shapes], axis=1).reshape(-1)
    rows = -(-flat.shape[0] // 1024) * 8
    return jnp.pad(flat, (0, rows * 128 - flat.shape[0])).reshape(rows, 128)


def _unpack_small(packed, names_shapes, depth):
    per_layer = sum(math.prod(s) for _, s in names_shapes)
    flat = packed.reshape(-1)[:depth * per_layer].reshape(depth, per_layer)
    out, off = {}, 0
    for n, s in names_shapes:
        size = math.prod(s)
        out[n] = flat[:, off:off + size].reshape((depth,) + s)
        off += size
    return out


def kernel(x, positions, mix_pre_g, mix_post_g, ffn_pre_g, ffn_post_g, w_in, q_norm_g, w_uq, kv_norm_g, w_ukv, sg_ln_g, sg_ln_b, w_sp, b_sp, conv_w, out_norm_g, w_out, w_gate, w_up, w_down, loss_target, m_mix_pre_g, m_mix_post_g, m_ffn_pre_g, m_ffn_post_g, m_w_in, m_q_norm_g, m_w_uq, m_kv_norm_g, m_w_ukv, m_sg_ln_g, m_sg_ln_b, m_w_sp, m_b_sp, m_conv_w, m_out_norm_g, m_w_out, m_w_gate, m_w_up, m_w_down, v_mix_pre_g, v_mix_post_g, v_ffn_pre_g, v_ffn_post_g, v_w_in, v_q_norm_g, v_w_uq, v_kv_norm_g, v_w_ukv, v_sg_ln_g, v_sg_ln_b, v_w_sp, v_b_sp, v_conv_w, v_out_norm_g, v_w_out, v_w_gate, v_w_up, v_w_down):
    w = dict(mix_pre_g=mix_pre_g, mix_post_g=mix_post_g, ffn_pre_g=ffn_pre_g, ffn_post_g=ffn_post_g, w_in=w_in,
             q_norm_g=q_norm_g, w_uq=w_uq, kv_norm_g=kv_norm_g, w_ukv=w_ukv, sg_ln_g=sg_ln_g, sg_ln_b=sg_ln_b, w_sp=w_sp,
             b_sp=b_sp, conv_w=conv_w, out_norm_g=out_norm_g, w_out=w_out, w_gate=w_gate, w_up=w_up, w_down=w_down)
    m = dict(mix_pre_g=m_mix_pre_g, mix_post_g=m_mix_post_g, ffn_pre_g=m_ffn_pre_g, ffn_post_g=m_ffn_post_g, w_in=m_w_in,
             q_norm_g=m_q_norm_g, w_uq=m_w_uq, kv_norm_g=m_kv_norm_g, w_ukv=m_w_ukv, sg_ln_g=m_sg_ln_g, sg_ln_b=m_sg_ln_b,
             w_sp=m_w_sp, b_sp=m_b_sp, conv_w=m_conv_w, out_norm_g=m_out_norm_g, w_out=m_w_out, w_gate=m_w_gate,
             w_up=m_w_up, w_down=m_w_down)
    v = dict(mix_pre_g=v_mix_pre_g, mix_post_g=v_mix_post_g, ffn_pre_g=v_ffn_pre_g, ffn_post_g=v_ffn_post_g, w_in=v_w_in,
             q_norm_g=v_q_norm_g, w_uq=v_w_uq, kv_norm_g=v_kv_norm_g, w_ukv=v_w_ukv, sg_ln_g=v_sg_ln_g, sg_ln_b=v_sg_ln_b,
             w_sp=v_w_sp, b_sp=v_b_sp, conv_w=v_conv_w, out_norm_g=v_out_norm_g, w_out=v_w_out, w_gate=v_w_gate,
             w_up=v_w_up, w_down=v_w_down)
    depth = w_in.shape[0]
    c = lax.axis_index("c").astype(jnp.int32).reshape(1)
    chip = (2 * lax.axis_index("x") + lax.axis_index("y")).astype(jnp.int32)

    nbatch = depth // 2

    mine = [_pack_weight_shards(w), *_pad_ffn_shards(jnp.swapaxes(w_gate, 1, 2), jnp.swapaxes(w_up, 1, 2), w_down)]
    bufs = [lax.dynamic_update_slice(g, a[None], (chip,) + (0,) * a.ndim)
            for g, a in zip(_gather_first_layer(mine), mine)]

    small_grads = [None] * nbatch
    small_pair = []

    def front(b, grads, g_gu, g_down):
        small_grads[b] = grads
        bigs = [_pack_grad_shards(grads), g_gu.reshape(4, 2, 2 * HP, D), g_down]
        if b > 0:
            rbigs = _swap_halves(bigs)
        else:
            small = _pack_small({n: jnp.concatenate([g[n] for g in small_grads]) for n, _ in SMALL}, SMALL, depth)
            *rbigs, rsmall = _swap_halves(bigs, [small])
            small_pair.append(_small_sum(jnp.stack([small, rsmall])))
        return [_pair_sum(a, r, c) for a, r in zip(bigs, rbigs)]

    loss, dx, ps, sent = _local_step(x[0], positions[0], loss_target[0], w, mine, bufs,
                                     {n: w[n].shape for n, _, _ in PACK}, front=front)
    loss = lax.psum(loss, ("x", "y", "c"))

    *sent[0], rs = _chip_exchange(ps[0], small_pair[0])
    own = [[_chip_sum(p, rb, chip.reshape(1)) for p, rb in zip(ps[b], sent[b])] for b in range(nbatch)]
    other = [_send_to_sibling(o) for o in own]
    g_small = _unpack_small(_small_sum(rs), SMALL, depth)
    g_small["conv_w"] = lax.dynamic_slice_in_dim(g_small["conv_w"], chip * 64, 64, axis=2)

    gw, delta, new_m, new_v = dict(g_small), {}, {}, {}

    def adam(n, pieces, pick=None, turned=False):
        view = (lambda a: jnp.swapaxes(a, 1, 2)) if turned else (lambda a: a)
        outs = _adamw_shard(view(w[n]), view(m[n]), view(v[n]), [pieces(o) for o in own], [pieces(o) for o in other], c,
                            "adamw_" + n, pick)
        gw[n], delta[n], new_m[n], new_v[n] = [view(o) for o in outs]

    for n, off, rows in PACK:
        adam(n, lambda o: o[0][off:off + rows, :].reshape(w[n].shape[1:]))
    adam("w_gate", lambda o: o[1].reshape(2, HP, D), 0, turned=True)
    adam("w_up", lambda o: o[1].reshape(2, HP, D), 1, turned=True)
    adam("w_down", lambda o: o[2])
    small_local = tuple((n, w[n].shape[1:]) for n, _ in SMALL)
    d_, m_, v_ = _adamw_small(_pack_small(w, small_local, depth), _pack_small(gw, small_local, depth),
                              _pack_small(m, small_local, depth), _pack_small(v, small_local, depth))
    delta.update(_unpack_small(d_, small_local, depth))
    new_m.update(_unpack_small(m_, small_local, depth))
    new_v.update(_unpack_small(v_, small_local, depth))

    return (loss, dx[None], *[gw[n] for n in WEIGHTS], *[delta[n] for n in WEIGHTS], *[new_m[n] for n in WEIGHTS],
            *[new_v[n] for n in WEIGHTS])
```

```python
import math

import jax
import jax.numpy as jnp
from jax import lax
from jax.experimental import pallas as pl
from jax.experimental.pallas import tpu as pltpu

F32 = jnp.float32
BF16 = jnp.bfloat16

D = 1024
HEADS = 8
NOPE = 64
ROPE = 32
VD = 64
QR = 384
KVR = 256
SGW = 256
CVW = 256
CHUNK = 128
DFF = 2816
EPS = 1e-6
ROPE_THETA = 10000.0
LOG2E = 1.4426950408889634
LN2 = 0.6931471805599453
QSCALE = (NOPE + ROPE) ** -0.5 * LOG2E
ZA = 768
ZB = 1280
QW = HEADS * 128
KVW = HEADS * 128 + HEADS * VD
NEG = -1e30
GC0 = 0.7978845608028654
GC1 = 0.044715

ADAM_LR = 0.001
ADAM_B1 = 0.9
ADAM_B2 = 0.999
ADAM_EPS = 1e-08
ADAM_WD = 0.01
ADAM_STEP = 10

V7X_VMEM_LIMIT = 52 * 1024 * 1024
ROW_TILE = 512
ATT_TILE = 512

NT = (((1,), (1,)), ((), ()))
TN = (((0,), (0,)), ((), ()))

HS = DFF // 4
HP = 768
DFFP = 4 * HP

PACK = (("w_in", 0, 488), ("w_out", 512, 256), ("w_ukv", 768, 64), ("w_uq", 832, 72))
CONV_ROW = 904
PACK_ROWS = 928
ROW_SHARDED = ("w_out",)
SMALL = (("mix_pre_g", (D,)), ("mix_post_g", (D,)), ("ffn_pre_g", (D,)), ("ffn_post_g", (D,)), ("q_norm_g", (QR,)),
         ("kv_norm_g", (KVR,)), ("sg_ln_g", (SGW,)), ("sg_ln_b", (SGW,)), ("w_sp", (4, CHUNK, CHUNK)), ("b_sp", (4, CHUNK)),
         ("conv_w", (3, CVW)), ("out_norm_g", (D,)))
WEIGHTS = ["mix_pre_g", "mix_post_g", "ffn_pre_g", "ffn_post_g", "w_in", "q_norm_g", "w_uq", "kv_norm_g", "w_ukv", "sg_ln_g",
           "sg_ln_b", "w_sp", "b_sp", "conv_w", "out_norm_g", "w_out", "w_gate", "w_up", "w_down"]

MESH_ID = pl.DeviceIdType.MESH
ANY = pl.BlockSpec(memory_space=pl.ANY)


def _cp(*sem):
    return pltpu.CompilerParams(dimension_semantics=sem, vmem_limit_bytes=V7X_VMEM_LIMIT)


def _sds(shape, dtype):
    return jax.ShapeDtypeStruct(shape, dtype)


def _row(tm, n):
    return pl.BlockSpec((tm, n), lambda i: (i, 0))


def _lyr(l, *shape):
    return pl.BlockSpec((None,) + shape, lambda *_: (l,) + (0,) * len(shape))


def _wl(a, l):
    return 0 if a.shape[0] == 1 else l


def _pcall(body, name, grid, ins, in_specs, out_specs, out_shape, sem, scratch=(), prevs=None):
    prevs = {k: v for k, v in (prevs or {}).items() if v is not None}
    order = sorted(prevs)
    n_in = len(ins)

    def wrapped(*refs):
        return body(*refs[:n_in], *refs[n_in + len(order):])

    return pl.pallas_call(
        wrapped, name=name, grid=grid, in_specs=list(in_specs) + [ANY] * len(order), out_specs=out_specs,
        out_shape=out_shape, scratch_shapes=list(scratch),
        input_output_aliases={n_in + i: k for i, k in enumerate(order)},
        compiler_params=_cp(*sem))(*ins, *[prevs[k] for k in order])


def _rms(x, g):
    r = lax.rsqrt(jnp.mean(x * x, axis=-1, keepdims=True) + EPS)
    return x * r * g


def _rms_bwd(x, g, dy):
    r = lax.rsqrt(jnp.mean(x * x, axis=-1, keepdims=True) + EPS)
    xh = x * r
    dg = jnp.sum(dy * xh, axis=0, keepdims=True)
    dxh = dy * g
    dx = r * (dxh - xh * jnp.mean(dxh * xh, axis=-1, keepdims=True))
    return dx, dg


def _gelu(x):
    return 0.5 * x * (1.0 + jnp.tanh(GC0 * (x + GC1 * x * x * x)))


def _gelu_grad(x):
    t = jnp.tanh(GC0 * (x + GC1 * x * x * x))
    return 0.5 * (1.0 + t) + 0.5 * x * (1.0 - t * t) * GC0 * (1.0 + 3.0 * GC1 * x * x)


def _rope(xb, c, s1, s2):
    return xb * c + pltpu.roll(xb, 112, 1) * s1 + pltpu.roll(xb, 16, 1) * s2


def _rope_bwd(dy, c, s1, s2):
    return dy * c + pltpu.roll(dy * s1, 16, 1) + pltpu.roll(dy * s2, 112, 1)


def _group_masks(shape):
    lane = lax.broadcasted_iota(jnp.int32, shape, 1)
    return [(lane >= 64 * g) & (lane < 64 * g + 64) for g in range(shape[1] // 64)]


def _group_mean(v, masks):
    out = jnp.zeros_like(v)
    for m in masks:
        s = jnp.sum(jnp.where(m, v, 0.0), axis=-1, keepdims=True) * (1.0 / 64.0)
        out = jnp.where(m, s, out)
    return out


def _pick_row(blk, idx):
    row = lax.broadcasted_iota(jnp.int32, blk.shape, 0)
    return jnp.sum(jnp.where(row == idx, blk, 0.0), axis=0, keepdims=True)


def _shift_down(y, k, first_rows):
    out = pltpu.roll(y, k, 0)
    row = lax.broadcasted_iota(jnp.int32, y.shape, 0)
    for idx in range(k):
        out = jnp.where(row == idx, first_rows[idx], out)
    return out


def _shift_up(y, k, last_rows):
    n = y.shape[0]
    out = pltpu.roll(y, n - k, 0)
    row = lax.broadcasted_iota(jnp.int32, y.shape, 0)
    for idx in range(k):
        out = jnp.where(row == n - k + idx, last_rows[idx], out)
    return out


def _tril_mask():
    r = lax.broadcasted_iota(jnp.int32, (CHUNK, CHUNK), 0)
    c = lax.broadcasted_iota(jnp.int32, (CHUNK, CHUNK), 1)
    return r >= c


def _sgu_forward(zu, zv, g_ln, b_ln, wc_bf, bsp, masks, cmasks):
    u = _gelu(zu)
    vv = _gelu(zv)
    mu = _group_mean(vv, masks)
    dv = vv - mu
    rs = lax.rsqrt(_group_mean(dv * dv, masks) + EPS)
    xh = dv * rs
    vn = xh * g_ln + b_ln
    chunks = []
    for ci in range(zu.shape[0] // CHUNK):
        vc = vn[ci * CHUNK:(ci + 1) * CHUNK, :]
        acc = bsp
        for g in range(4):
            acc = acc + jnp.dot(wc_bf[g], jnp.where(cmasks[g], vc, 0.0).astype(BF16), preferred_element_type=F32)
        chunks.append(acc)
    mixed = jnp.concatenate(chunks, axis=0) if len(chunks) > 1 else chunks[0]
    return u, vv, xh, rs, vn, mixed


def _conv_forward(gc, hh, prev_gc, prev_hh, first_tile, cw):
    yv = gc * hh
    prev = jnp.where(first_tile, 0.0, prev_gc * prev_hh)
    p6, p7 = _pick_row(prev, 6), _pick_row(prev, 7)
    sh1 = _shift_down(yv, 1, [p7])
    sh2 = _shift_down(yv, 2, [p6, p7])
    conv = sh2 * cw[0:1, :] + sh1 * cw[1:2, :] + yv * cw[2:3, :]
    return yv, sh1, sh2, conv


def _acc_init(step, *refs):
    @pl.when(step == 0)
    def _():
        for r in refs:
            r[...] = jnp.zeros(r.shape, r.dtype)


def _in_proj(x, p, l):
    t = x.shape[0]
    tm = min(ROW_TILE, t)

    def body(x_ref, g_ref, wa_ref, wb_ref, h_ref, za_ref, zb_ref):
        h = _rms(x_ref[...], g_ref[...]).astype(BF16)
        h_ref[...] = h
        za_ref[...] = jnp.dot(h, wa_ref[...], preferred_element_type=F32)
        zb_ref[...] = jnp.dot(h, wb_ref[...], preferred_element_type=F32)

    return _pcall(
        body, "in_proj", (t // tm,), [x, p["mix_pre_g"], p["w_in_a"], p["w_in_b"]],
        [_row(tm, D), _lyr(l, 1, D), _lyr(_wl(p["w_in_a"], l), D, ZA), _lyr(_wl(p["w_in_b"], l), D, ZB)],
        [_row(tm, D), _row(tm, ZA), _row(tm, ZB)],
        [_sds((t, D), BF16), _sds((t, ZA), F32), _sds((t, ZB), F32)], ("parallel",))


def _mla_prep(za, p, l, tabs):
    t = za.shape[0]
    tm = min(ROW_TILE, t)

    def body(z_ref, gq_ref, gkv_ref, wuq_ref, wukv_ref, c_ref, s1_ref, s2_ref, cq_ref, ckv_ref, q_ref, k_ref, v_ref):
        z = z_ref[...]
        cq = _rms(z[:, :QR], gq_ref[...]).astype(BF16)
        ckv = _rms(z[:, QR:QR + KVR], gkv_ref[...]).astype(BF16)
        cq_ref[...] = cq
        ckv_ref[...] = ckv
        c, s1, s2 = c_ref[...], s1_ref[...], s2_ref[...]
        kr = _rope(z[:, QR + KVR:], c, s1, s2)
        q = jnp.dot(cq, wuq_ref[...], preferred_element_type=F32)
        kv = jnp.dot(ckv, wukv_ref[...], preferred_element_type=F32)
        for h in range(HEADS):
            sl = slice(128 * h, 128 * h + 128)
            q_ref[:, sl] = (_rope(q[:, sl], c, s1, s2) * QSCALE).astype(BF16)
            k_ref[:, sl] = (kv[:, sl] + kr).astype(BF16)
        v_ref[...] = kv[:, QW:].astype(BF16)

    return _pcall(
        body, "mla_prep", (t // tm,), [za, p["q_norm_g"], p["kv_norm_g"], p["w_uq"], p["w_ukv"], *tabs],
        [_row(tm, ZA), _lyr(l, 1, QR), _lyr(l, 1, KVR), _lyr(_wl(p["w_uq"], l), QR, QW), _lyr(_wl(p["w_ukv"], l), KVR, KVW),
         _row(tm, 128), _row(tm, 128), _row(tm, 128)],
        [_row(tm, QR), _row(tm, KVR), _row(tm, QW), _row(tm, QW), _row(tm, HEADS * VD)],
        [_sds((t, QR), BF16), _sds((t, KVR), BF16), _sds((t, QW), BF16), _sds((t, QW), BF16),
         _sds((t, HEADS * VD), BF16)], ("parallel",))


def _att_tile(t):
    return min(ATT_TILE, max(t // 2, 128))


def _causal_keep(tq, i, j):
    row = lax.broadcasted_iota(jnp.int32, (tq, tq), 0) + i * tq
    col = lax.broadcasted_iota(jnp.int32, (tq, tq), 1) + j * tq
    return col <= row


def _attn_fwd(qs, k, v, fetch=None):
    t = qs.shape[0]
    tq = _att_tile(t)
    nq = t // tq
    rep = tq // 128
    mine, bufs, fetch_layer = fetch if fetch else ((), (), None)
    nb = len(mine)

    steps = [(i, j) for i in range(nq) for j in range(i + 1)]
    i_of = jnp.asarray([s[0] for s in steps], jnp.int32)
    j_of = jnp.asarray([s[1] for s in steps], jnp.int32)

    def body(i_ref, j_ref, q_ref, k_ref, v_ref, *refs):
        o_ref, lse_ref = refs[2 * nb:2 * nb + 2]
        m_s, l_s, acc_s = refs[3 * nb + 2:3 * nb + 5]
        step_no = pl.program_id(1)
        i, j = i_ref[step_no], j_ref[step_no]
        if fetch:
            start, hand_over, drain = _gather_ops(refs[:nb], refs[2 * nb + 2:3 * nb + 2], refs[3 * nb + 5:], fetch_layer)
            pr = pl.program_id(0)
            pl.when((pr == 0) & (step_no == 0))(start)
            pl.when((pr == HEADS // 2 - 1) & (step_no == 0))(hand_over)
            pl.when((pr == HEADS // 2 - 1) & (step_no == len(steps) - 1))(drain)

        @pl.when(j == 0)
        def _():
            m_s[...] = jnp.full(m_s.shape, NEG, F32)
            l_s[...] = jnp.zeros(l_s.shape, F32)
            acc_s[...] = jnp.zeros(acc_s.shape, F32)

        def step(masked):
            vv = v_ref[...]
            keep = _causal_keep(tq, i, j) if masked else None
            for hh in range(2):
                sl = slice(128 * hh, 128 * hh + 128)
                s = lax.dot_general(q_ref[:, sl], k_ref[:, sl], NT, preferred_element_type=F32)
                if masked:
                    s = jnp.where(keep, s, NEG)
                m_old = m_s[hh]
                m_new = jnp.maximum(m_old, jnp.max(s, axis=-1, keepdims=True))
                alpha = jnp.exp2(m_old - m_new)
                p = jnp.exp2(s - jnp.tile(m_new, (1, rep)))
                l_s[hh] = alpha * l_s[hh] + jnp.sum(p, axis=-1, keepdims=True)
                acc_s[hh] = alpha * acc_s[hh] + jnp.dot(p.astype(BF16), vv, preferred_element_type=F32)
                m_s[hh] = m_new

        @pl.when(j < i)
        def _():
            step(False)

        @pl.when(j == i)
        def _():
            step(True)
            lane = lax.broadcasted_iota(jnp.int32, (tq, 128), 1)
            o_ref[...] = jnp.where(lane < VD, acc_s[0] / l_s[0], acc_s[1] / l_s[1])
            for hh in range(2):
                lse_ref[hh] = (m_s[hh] + jnp.log2(l_s[hh]))[:, 0:1]

    grid_spec = pltpu.PrefetchScalarGridSpec(
        num_scalar_prefetch=2, grid=(HEADS // 2, len(steps)),
        in_specs=[pl.BlockSpec((tq, 256), lambda p, s, it, jt: (it[s], p)),
                  pl.BlockSpec((tq, 256), lambda p, s, it, jt: (jt[s], p)),
                  pl.BlockSpec((tq, 128), lambda p, s, it, jt: (jt[s], p))] + [ANY] * (2 * nb),
        out_specs=[pl.BlockSpec((tq, 128), lambda p, s, it, jt: (it[s], p)),
                   pl.BlockSpec((2, tq, 1), lambda p, s, it, jt: (p, it[s], 0))] + [ANY] * nb,
        scratch_shapes=[pltpu.VMEM((2, tq, 128), F32), pltpu.VMEM((2, tq, 128), F32), pltpu.VMEM((2, tq, 128), F32)]
        + ([pltpu.SemaphoreType.DMA((nb, 3))] * 4 if fetch else []))
    outs = pl.pallas_call(
        body, name="attn_fwd_fetch" if fetch else "attn_fwd", grid_spec=grid_spec,
        out_shape=[_sds((t, HEADS * VD), F32), _sds((HEADS, t, 1), F32)] + [_sds(b.shape, b.dtype) for b in bufs],
        input_output_aliases={5 + nb + b: 2 + b for b in range(nb)},
        compiler_params=_cp("arbitrary", "arbitrary"))(i_of, j_of, qs, k, v, *mine, *bufs)
    return outs[0], outs[1], list(outs[2:])


def _mixer_fwd(zb, ya, p, l):
    t = zb.shape[0]
    tm = min(ROW_TILE, t)
    hb = tm // 8

    def body(zb_ref, zprev_ref, ya_ref, gln_ref, bln_ref, wsp_ref, bsp_ref, cw_ref, go_ref, mix_ref, yb_ref, yc_ref):
        i = pl.program_id(0)
        masks = _group_masks((tm, SGW))
        cmasks = _group_masks((CHUNK, SGW))
        tril = _tril_mask()
        wc_bf = [jnp.where(tril, wsp_ref[g], 0.0).astype(BF16) for g in range(4)]
        u, _, _, _, _, mixed = _sgu_forward(zb_ref[:, 0:256], zb_ref[:, 256:512], gln_ref[...], bln_ref[...], wc_bf,
                                            bsp_ref[...], masks, cmasks)
        yb = u * mixed
        _, _, _, conv = _conv_forward(zb_ref[:, 768:1024], zb_ref[:, 1024:1280], zprev_ref[:, 768:1024],
                                      zprev_ref[:, 1024:1280], i == 0, cw_ref[...])
        yc = zb_ref[:, 512:768] * conv
        yb_ref[...] = yb
        yc_ref[...] = yc
        go = go_ref[...]
        mix_ref[:, 0:512] = _rms(ya_ref[...], go[:, 0:512]).astype(BF16)
        mix_ref[:, 512:768] = _rms(yb, go[:, 512:768]).astype(BF16)
        mix_ref[:, 768:1024] = _rms(yc, go[:, 768:1024]).astype(BF16)

    return _pcall(
        body, "mixer_fwd", (t // tm,),
        [zb, zb, ya, p["sg_ln_g"], p["sg_ln_b"], p["w_sp"], p["b_sp"], p["conv_w"], p["out_norm_g"]],
        [_row(tm, ZB), pl.BlockSpec((8, ZB), lambda i: (jnp.maximum(i * hb - 1, 0), 0)), _row(tm, 512),
         _lyr(l, 1, SGW), _lyr(l, 1, SGW), _lyr(l, 4, CHUNK, CHUNK), _lyr(l, CHUNK, SGW), _lyr(_wl(p["conv_w"], l), 3, CVW), _lyr(l, 1, D)],
        [_row(tm, D), _row(tm, SGW), _row(tm, CVW)],
        [_sds((t, D), BF16), _sds((t, SGW), F32), _sds((t, CVW), F32)], ("parallel",))


def _out_proj(mix, x, p, l):
    t = x.shape[0]
    tm = min(ROW_TILE, t)

    def body(mix_ref, w_ref, x_ref, gp_ref, gf_ref, o_ref, x2_ref, h2_ref):
        o = jnp.dot(mix_ref[...], w_ref[...], preferred_element_type=F32)
        o_ref[...] = o
        x2 = x_ref[...] + _rms(o, gp_ref[...])
        x2_ref[...] = x2
        h2_ref[...] = _rms(x2, gf_ref[...]).astype(BF16)

    return _pcall(
        body, "out_proj", (t // tm,), [mix, p["w_out"], x, p["mix_post_g"], p["ffn_pre_g"]],
        [_row(tm, D), _lyr(_wl(p["w_out"], l), D, D), _row(tm, D), _lyr(l, 1, D), _lyr(l, 1, D)],
        [_row(tm, D), _row(tm, D), _row(tm, D)],
        [_sds((t, D), F32), _sds((t, D), F32), _sds((t, D), BF16)], ("parallel",))


def _gu_all(l, which):
    return pl.BlockSpec((4, None, None, HP, D), lambda *_: (0, l, which, 0, 0))


def _down_all(l):
    return pl.BlockSpec((4, None, HP, D), lambda *_: (0, l, 0, 0))


def _ffn_up(h2, p, l):
    t = h2.shape[0]
    tm = min(ROW_TILE, t)

    def body(h_ref, wg_ref, wu_ref, a_ref, b_ref, s_ref):
        h = h_ref[...]
        a = lax.dot_general(h, wg_ref[...], NT, preferred_element_type=F32)
        b = lax.dot_general(h, wu_ref[...], NT, preferred_element_type=F32)
        a_ref[...] = a.astype(BF16)
        b_ref[...] = b.astype(BF16)
        s_ref[...] = (a * (1.0 / (1.0 + jnp.exp(-a))) * b).astype(BF16)

    blk = pl.BlockSpec((tm, HP), lambda k, i: (i, k))
    wblk = lambda which: pl.BlockSpec((None, None, None, HP, D), lambda k, i: (k, l, which, 0, 0))
    return _pcall(
        body, "ffn_up", (4, t // tm), [h2, p["w_gu"], p["w_gu"]],
        [pl.BlockSpec((tm, D), lambda k, i: (i, 0)), wblk(0), wblk(1)], [blk, blk, blk],
        [_sds((t, DFFP), BF16)] * 3, ("parallel", "parallel"))


def _ffn_down(s, x2, p, l):
    t = x2.shape[0]
    tm = min(ROW_TILE, t)

    def body(s_ref, w_ref, x_ref, g_ref, f_ref, x3_ref):
        f = jnp.dot(s_ref[:, 0:HP], w_ref[0], preferred_element_type=F32)
        for k in range(1, 4):
            f = f + jnp.dot(s_ref[:, k * HP:(k + 1) * HP], w_ref[k], preferred_element_type=F32)
        f_ref[...] = f
        x3_ref[...] = x_ref[...] + _rms(f, g_ref[...])

    return _pcall(
        body, "ffn_down", (t // tm,), [s, p["w_down"], x2, p["ffn_post_g"]],
        [_row(tm, DFFP), _down_all(l), _row(tm, D), _lyr(l, 1, D)], [_row(tm, D), _row(tm, D)],
        [_sds((t, D), F32), _sds((t, D), F32)], ("parallel",))


def _loss_head(y, target):
    t = y.shape[0]
    tm = min(ROW_TILE, t)

    def body(y_ref, t_ref, dy_ref, acc_ref):
        e = y_ref[...] - t_ref[...]
        dy_ref[...] = e * (1.0 / D)
        sq = jnp.sum(e * e, axis=0, keepdims=True)
        part = sq[:, 0:128]
        for b in range(1, D // 128):
            part = part + sq[:, 128 * b:128 * b + 128]
        _acc_init(pl.program_id(0), acc_ref)
        acc_ref[...] += part

    return _pcall(body, "loss_head", (t // tm,), [y, target], [_row(tm, D), _row(tm, D)],
                  [_row(tm, D), pl.BlockSpec((1, 128), lambda i: (0, 0))],
                  [_sds((t, D), F32), _sds((1, 128), F32)], ("arbitrary",))


def _ffn_down_bwd(dx3, sv, p, l, depth, gb):
    t = dx3.shape[0]
    tm = min(256, t)

    def body(dx_ref, f_ref, g_ref, w_ref, a_ref, b_ref, df_ref, da_ref, db_ref, dg_ref):
        _acc_init(pl.program_id(0), dg_ref)
        df, dg = _rms_bwd(f_ref[...], g_ref[...], dx_ref[...])
        dg_ref[...] += dg
        df = df.astype(BF16)
        df_ref[...] = df
        for k in range(4):
            sl = slice(k * HP, (k + 1) * HP)
            ds = lax.dot_general(df, w_ref[k], NT, preferred_element_type=F32)
            av = a_ref[:, sl].astype(F32)
            sig = 1.0 / (1.0 + jnp.exp(-av))
            da_ref[:, sl] = (ds * b_ref[:, sl].astype(F32) * (sig * (1.0 + av * (1.0 - sig)))).astype(BF16)
            db_ref[:, sl] = (ds * (av * sig)).astype(BF16)

    df, da, db, gb["ffn_post_g"] = _pcall(
        body, "ffn_down_bwd", (t // tm,), [dx3, sv["f"], p["ffn_post_g"], p["w_down"], sv["a"], sv["b"]],
        [_row(tm, D), _row(tm, D), _lyr(l, 1, D), _down_all(l), _row(tm, DFFP), _row(tm, DFFP)],
        [_row(tm, D), _row(tm, DFFP), _row(tm, DFFP), _lyr(l.g, 1, D)],
        [_sds((t, D), BF16), _sds((t, DFFP), BF16), _sds((t, DFFP), BF16), _sds((depth, 1, D), F32)], ("arbitrary",),
        prevs={3: gb.get("ffn_post_g")})
    return df, da, db


def _ffn_up_bwd(da, db, dx3, sv, p, l, depth, gb):
    t = dx3.shape[0]
    tm = min(256, t)

    def body(da_ref, db_ref, wg_ref, wu_ref, x_ref, dx3_ref, g_ref, dx2_ref, dg_ref):
        _acc_init(pl.program_id(0), dg_ref)
        dh = jnp.zeros((tm, D), F32)
        for k in range(4):
            sl = slice(k * HP, (k + 1) * HP)
            dh = dh + jnp.dot(da_ref[:, sl], wg_ref[k], preferred_element_type=F32)
            dh = dh + jnp.dot(db_ref[:, sl], wu_ref[k], preferred_element_type=F32)
        dx, dg = _rms_bwd(x_ref[...], g_ref[...], dh)
        dg_ref[...] += dg
        dx2_ref[...] = dx3_ref[...] + dx

    dx2, gb["ffn_pre_g"] = _pcall(
        body, "ffn_up_bwd", (t // tm,), [da, db, p["w_gu"], p["w_gu"], sv["x2"], dx3, p["ffn_pre_g"]],
        [_row(tm, DFFP), _row(tm, DFFP), _gu_all(l, 0), _gu_all(l, 1), _row(tm, D), _row(tm, D), _lyr(l, 1, D)],
        [_row(tm, D), _lyr(l.g, 1, D)], [_sds((t, D), F32), _sds((depth, 1, D), F32)], ("arbitrary",),
        prevs={1: gb.get("ffn_pre_g")})
    return dx2


def _out_proj_bwd(dx2, sv, p, l, depth, gb):
    t = dx2.shape[0]
    tm = min(ROW_TILE, t)

    def body(dx_ref, o_ref, g_ref, w_ref, do_ref, dmix_ref, dg_ref):
        _acc_init(pl.program_id(0), dg_ref)
        do, dg = _rms_bwd(o_ref[...], g_ref[...], dx_ref[...])
        dg_ref[...] += dg
        do = do.astype(BF16)
        do_ref[...] = do
        dmix_ref[...] = lax.dot_general(do, w_ref[...], NT, preferred_element_type=F32)

    do, dmix, gb["mix_post_g"] = _pcall(
        body, "out_proj_bwd", (t // tm,), [dx2, sv["o"], p["mix_post_g"], p["w_out"]],
        [_row(tm, D), _row(tm, D), _lyr(l, 1, D), _lyr(_wl(p["w_out"], l), D, D)], [_row(tm, D), _row(tm, D), _lyr(l.g, 1, D)],
        [_sds((t, D), BF16), _sds((t, D), F32), _sds((depth, 1, D), F32)], ("arbitrary",),
        prevs={2: gb.get("mix_post_g")})
    return do, dmix


def _mixer_bwd(dmix, sv, p, l, depth, gb):
    zb = sv["zb"]
    t = zb.shape[0]
    tm = min(ROW_TILE, t)
    hb = tm // 8
    last_blk = t // 8 - 1
    nsteps = t // tm

    def body(dmix_ref, ya_ref, yb_ref, yc_ref, zb_ref, zprev_ref, znext_ref, ycn_ref, dmn_ref,
             gln_ref, bln_ref, wsp_ref, bsp_ref, cw_ref, go_ref,
             dya_ref, dzb_ref, delta_ref, dgo_ref, dgln_ref, dbln_ref, dwsp_ref, dbsp_ref, dcw_ref):
        i = pl.program_id(0)
        _acc_init(i, dgo_ref, dgln_ref, dbln_ref, dwsp_ref, dbsp_ref, dcw_ref)
        go = go_ref[...]
        dmix = dmix_ref[...]

        ya = ya_ref[...]
        dya, dga = _rms_bwd(ya, go[:, 0:512], dmix[:, 0:512])
        dyb, dgb_ = _rms_bwd(yb_ref[...], go[:, 512:768], dmix[:, 512:768])
        dyc, dgc_ = _rms_bwd(yc_ref[...], go[:, 768:1024], dmix[:, 768:1024])
        dgo_ref[:, 0:512] += dga
        dgo_ref[:, 512:768] += dgb_
        dgo_ref[:, 768:1024] += dgc_
        dya = dya * LN2
        dya_ref[...] = dya.astype(BF16)
        prod = dya * ya
        hmasks = _group_masks((tm, 512))
        for h in range(HEADS):
            delta_ref[h] = jnp.sum(jnp.where(hmasks[h], prod, 0.0), axis=-1, keepdims=True)

        masks = _group_masks((tm, SGW))
        cmasks = _group_masks((CHUNK, SGW))
        tril = _tril_mask()
        wc_bf = [jnp.where(tril, wsp_ref[g], 0.0).astype(BF16) for g in range(4)]
        zu, zv = zb_ref[:, 0:256], zb_ref[:, 256:512]
        g_ln = gln_ref[...]
        u, _, xh, rs, vn, mixed = _sgu_forward(zu, zv, g_ln, bln_ref[...], wc_bf, bsp_ref[...], masks, cmasks)
        du = dyb * mixed
        dmixed = dyb * u
        dvn_chunks = []
        dbsp = jnp.zeros((CHUNK, SGW), F32)
        for ci in range(tm // CHUNK):
            rows = slice(ci * CHUNK, (ci + 1) * CHUNK)
            dm_c = dmixed[rows, :]
            vn_c = vn[rows, :].astype(BF16)
            dbsp = dbsp + dm_c
            dvn_c = jnp.zeros((CHUNK, SGW), F32)
            for g in range(4):
                dm_g = jnp.where(cmasks[g], dm_c, 0.0).astype(BF16)
                dw = lax.dot_general(dm_g, vn_c, NT, preferred_element_type=F32)
                dwsp_ref[g] += jnp.where(tril, dw, 0.0)
                dvn_c = dvn_c + lax.dot_general(wc_bf[g], dm_g, TN, preferred_element_type=F32)
            dvn_chunks.append(dvn_c)
        dbsp_ref[...] += dbsp
        dvn = jnp.concatenate(dvn_chunks, axis=0) if len(dvn_chunks) > 1 else dvn_chunks[0]
        dgln_ref[...] += jnp.sum(dvn * xh, axis=0, keepdims=True)
        dbln_ref[...] += jnp.sum(dvn, axis=0, keepdims=True)
        dxh = dvn * g_ln
        dvv = rs * (dxh - _group_mean(dxh, masks) - xh * _group_mean(dxh * xh, masks))
        dzb_ref[:, 0:256] = (du * _gelu_grad(zu)).astype(BF16)
        dzb_ref[:, 256:512] = (dvv * _gelu_grad(zv)).astype(BF16)

        cwv = cw_ref[...]
        gb_, gc, hh = zb_ref[:, 512:768], zb_ref[:, 768:1024], zb_ref[:, 1024:1280]
        yv, sh1, sh2, conv = _conv_forward(gc, hh, zprev_ref[:, 768:1024], zprev_ref[:, 1024:1280], i == 0, cwv)
        dconv = dyc * gb_
        dzb_ref[:, 512:768] = (dyc * conv).astype(BF16)
        dcw_ref[0:1, :] += jnp.sum(dconv * sh2, axis=0, keepdims=True)
        dcw_ref[1:2, :] += jnp.sum(dconv * sh1, axis=0, keepdims=True)
        dcw_ref[2:3, :] += jnp.sum(dconv * yv, axis=0, keepdims=True)
        dycn, _ = _rms_bwd(ycn_ref[...], go[:, 768:1024], dmn_ref[...])
        dconv_next = jnp.where(i == nsteps - 1, 0.0, dycn * znext_ref[:, 512:768])
        n0, n1 = _pick_row(dconv_next, 0), _pick_row(dconv_next, 1)
        dyv = dconv * cwv[2:3, :] + _shift_up(dconv, 1, [n0]) * cwv[1:2, :] + _shift_up(dconv, 2, [n0, n1]) * cwv[0:1, :]
        dzb_ref[:, 768:1024] = (dyv * hh).astype(BF16)
        dzb_ref[:, 1024:1280] = (dyv * gc).astype(BF16)

    prev_map = lambda i: (jnp.maximum(i * hb - 1, 0), 0)
    next_map = lambda i: (jnp.minimum((i + 1) * hb, last_blk), 0)
    names = ("out_norm_g", "sg_ln_g", "sg_ln_b", "w_sp", "b_sp_t", "conv_w")
    shapes = ((1, D), (1, SGW), (1, SGW), (4, CHUNK, CHUNK), (CHUNK, SGW), (3, CVW))
    outs = _pcall(
        body, "mixer_bwd", (nsteps,),
        [dmix, sv["ya"], sv["yb"], sv["yc"], zb, zb, zb, sv["yc"], dmix, p["sg_ln_g"], p["sg_ln_b"], p["w_sp"], p["b_sp"],
         p["conv_w"], p["out_norm_g"]],
        [_row(tm, D), _row(tm, 512), _row(tm, SGW), _row(tm, CVW), _row(tm, ZB),
         pl.BlockSpec((8, ZB), prev_map), pl.BlockSpec((8, ZB), next_map), pl.BlockSpec((8, CVW), next_map),
         pl.BlockSpec((8, 256), lambda i: (jnp.minimum((i + 1) * hb, last_blk), 3)),
         _lyr(l, 1, SGW), _lyr(l, 1, SGW), _lyr(l, 4, CHUNK, CHUNK), _lyr(l, CHUNK, SGW), _lyr(_wl(p["conv_w"], l), 3, CVW), _lyr(l, 1, D)],
        [_row(tm, 512), _row(tm, ZB), pl.BlockSpec((HEADS, tm, 1), lambda i: (0, i, 0))] + [_lyr(l.g, *s) for s in shapes],
        [_sds((t, 512), BF16), _sds((t, ZB), BF16), _sds((HEADS, t, 1), F32)] + [_sds((depth,) + s, F32) for s in shapes],
        ("arbitrary",), prevs={3 + n: gb.get(name) for n, name in enumerate(names)})
    for n, name in enumerate(names):
        gb[name] = outs[3 + n]
    return outs[0], outs[1], outs[2]


def _attn_bwd(qs, k, v, dya, lse, delta, ride=()):
    t = qs.shape[0]
    tq = _att_tile(t)
    nq = t // tq
    nb = len(ride)

    steps = [(j, i) for j in range(nq) for i in range(j, nq)]
    j_of = jnp.asarray([s[0] for s in steps], jnp.int32)
    i_of = jnp.asarray([s[1] for s in steps], jnp.int32)

    def body(j_ref, i_ref, q_ref, k_ref, v_ref, do_ref, lse_ref, dl_ref, *refs):
        dq_ref, dk_ref, dv_ref = refs[nb:nb + 3]
        dq_s, dk_s, dv_s = refs[2 * nb + 3:2 * nb + 6]
        step_no = pl.program_id(1)
        j, i = j_ref[step_no], i_ref[step_no]
        if ride:
            start, finish = _exchange_ops(refs[:nb], refs[nb + 3:2 * nb + 3], refs[2 * nb + 6:])
            pr = pl.program_id(0)
            pl.when((pr == 0) & (step_no == 0))(start)
            pl.when((pr == HEADS // 2 - 1) & (step_no == len(steps) - 1))(finish)

        @pl.when(step_no == 0)
        def _():
            dq_s[...] = jnp.zeros(dq_s.shape, F32)

        def step(masked):
            keep = _causal_keep(tq, 0, 0) if masked else None
            lane = lax.broadcasted_iota(jnp.int32, (tq, 128), 1)
            vv = v_ref[...]
            do = do_ref[...]
            rows = pl.ds(pl.multiple_of(i * tq, tq), tq)
            for hh in range(2):
                sl = slice(128 * hh, 128 * hh + 128)
                qq, kk = q_ref[:, sl], k_ref[:, sl]
                s = lax.dot_general(qq, kk, NT, preferred_element_type=F32)
                p = jnp.exp2(s - lse_ref[hh])
                if masked:
                    p = jnp.where(keep, p, 0.0)
                do_h = jnp.where((lane < VD) if hh == 0 else (lane >= VD), do, jnp.zeros_like(do))
                dp = lax.dot_general(do_h, vv, NT, preferred_element_type=F32)
                ds = (p * (dp - dl_ref[hh])).astype(BF16)
                dv_s[...] += lax.dot_general(p.astype(BF16), do_h, TN, preferred_element_type=F32)
                dk_s[:, sl] += lax.dot_general(ds, qq, TN, preferred_element_type=F32)
                dq_s[rows, sl] += jnp.dot(ds, kk, preferred_element_type=F32)

        @pl.when(i == j)
        def _():
            dk_s[...] = jnp.zeros(dk_s.shape, F32)
            dv_s[...] = jnp.zeros(dv_s.shape, F32)
            step(True)

        @pl.when(i > j)
        def _():
            step(False)

        @pl.when(i == nq - 1)
        def _():
            dk_ref[...] = dk_s[...].astype(BF16)
            dv_ref[...] = (dv_s[...] * LOG2E).astype(BF16)

        @pl.when(step_no == len(steps) - 1)
        def _():
            dq_ref[...] = dq_s[...].astype(BF16)

    qrow = lambda p, s, jt, it: (it[s], p)
    krow = lambda p, s, jt, it: (jt[s], p)
    col_spec = pl.BlockSpec((2, tq, 1), lambda p, s, jt, it: (p, it[s], 0))
    grid_spec = pltpu.PrefetchScalarGridSpec(
        num_scalar_prefetch=2, grid=(HEADS // 2, len(steps)),
        in_specs=[pl.BlockSpec((tq, 256), qrow), pl.BlockSpec((tq, 256), krow), pl.BlockSpec((tq, 128), krow),
                  pl.BlockSpec((tq, 128), qrow), col_spec, col_spec] + [ANY] * nb,
        out_specs=[pl.BlockSpec((t, 256), lambda p, s, jt, it: (0, p)), pl.BlockSpec((tq, 256), krow),
                   pl.BlockSpec((tq, 128), krow)] + [ANY] * nb,
        scratch_shapes=[pltpu.VMEM((t, 256), F32), pltpu.VMEM((tq, 256), F32), pltpu.VMEM((tq, 128), F32)]
        + ([pltpu.SemaphoreType.DMA((nb, 3))] * 2 if ride else []))
    outs = pl.pallas_call(
        body, name="attn_bwd_ride" if ride else "attn_bwd", grid_spec=grid_spec,
        out_shape=[_sds((t, QW), BF16), _sds((t, QW), BF16), _sds((t, HEADS * VD), BF16)]
        + [_sds((3,) + a.shape[1:], a.dtype) for a in ride],
        compiler_params=_cp("arbitrary", "arbitrary"))(j_of, i_of, qs, k, v, dya, lse, delta, *ride)
    return outs[0], outs[1], outs[2], list(outs[3:])


def _mla_prep_bwd(dqs, dk, dv, sv, p, l, depth, gb, tabs):
    za = sv["za"]
    t = za.shape[0]
    tm = min(ROW_TILE, t)

    def body(dq_ref, dk_ref, dv_ref, z_ref, gq_ref, gkv_ref, wuq_ref, wukv_ref, c_ref, s1_ref, s2_ref,
             dza_ref, dqp_ref, dkv_ref, dgq_ref, dgkv_ref):
        _acc_init(pl.program_id(0), dgq_ref, dgkv_ref)
        c, s1, s2 = c_ref[...], s1_ref[...], s2_ref[...]
        lane = lax.broadcasted_iota(jnp.int32, (tm, 128), 1)
        rope_lanes = (lane >= NOPE) & (lane < NOPE + ROPE)
        dkr = jnp.zeros((tm, 128), F32)
        for h in range(HEADS):
            sl = slice(128 * h, 128 * h + 128)
            dqp_ref[:, sl] = _rope_bwd(dq_ref[:, sl].astype(F32) * QSCALE, c, s1, s2).astype(BF16)
            dkh = dk_ref[:, sl]
            dkv_ref[:, sl] = dkh
            dkr = dkr + jnp.where(rope_lanes, dkh.astype(F32), 0.0)
        dkv_ref[:, QW:] = dv_ref[...]
        z = z_ref[...]
        dcq = lax.dot_general(dqp_ref[...], wuq_ref[...], NT, preferred_element_type=F32)
        dzq, dgq = _rms_bwd(z[:, :QR], gq_ref[...], dcq)
        dckv = lax.dot_general(dkv_ref[...], wukv_ref[...], NT, preferred_element_type=F32)
        dzkv, dgkv = _rms_bwd(z[:, QR:QR + KVR], gkv_ref[...], dckv)
        dgq_ref[...] += dgq
        dgkv_ref[...] += dgkv
        dza_ref[:, :QR] = dzq.astype(BF16)
        dza_ref[:, QR:QR + KVR] = dzkv.astype(BF16)
        dza_ref[:, QR + KVR:] = _rope_bwd(dkr, c, s1, s2).astype(BF16)

    dza, dqp, dkv, gb["q_norm_g"], gb["kv_norm_g"] = _pcall(
        body, "mla_prep_bwd", (t // tm,),
        [dqs, dk, dv, za, p["q_norm_g"], p["kv_norm_g"], p["w_uq"], p["w_ukv"], *tabs],
        [_row(tm, QW), _row(tm, QW), _row(tm, HEADS * VD), _row(tm, ZA), _lyr(l, 1, QR), _lyr(l, 1, KVR),
         _lyr(_wl(p["w_uq"], l), QR, QW), _lyr(_wl(p["w_ukv"], l), KVR, KVW), _row(tm, 128), _row(tm, 128), _row(tm, 128)],
        [_row(tm, ZA), _row(tm, QW), _row(tm, KVW), _lyr(l.g, 1, QR), _lyr(l.g, 1, KVR)],
        [_sds((t, ZA), BF16), _sds((t, QW), BF16), _sds((t, KVW), BF16), _sds((depth, 1, QR), F32),
         _sds((depth, 1, KVR), F32)], ("arbitrary",), prevs={3: gb.get("q_norm_g"), 4: gb.get("kv_norm_g")})
    return dza, dqp, dkv


def _in_proj_bwd(dza, dzb, dx2, sv, p, l, depth, gb):
    t = dx2.shape[0]
    tm = min(ROW_TILE, t)

    def body(dza_ref, dzb_ref, wa_ref, wb_ref, x_ref, dx2_ref, g_ref, dx_ref, dg_ref):
        _acc_init(pl.program_id(0), dg_ref)
        dh = (lax.dot_general(dza_ref[...], wa_ref[...], NT, preferred_element_type=F32)
              + lax.dot_general(dzb_ref[...], wb_ref[...], NT, preferred_element_type=F32))
        dx, dg = _rms_bwd(x_ref[...], g_ref[...], dh)
        dg_ref[...] += dg
        dx_ref[...] = dx2_ref[...] + dx

    dx, gb["mix_pre_g"] = _pcall(
        body, "in_proj_bwd", (t // tm,), [dza, dzb, p["w_in_a"], p["w_in_b"], sv["x"], dx2, p["mix_pre_g"]],
        [_row(tm, ZA), _row(tm, ZB), _lyr(_wl(p["w_in_a"], l), D, ZA), _lyr(_wl(p["w_in_b"], l), D, ZB), _row(tm, D), _row(tm, D), _lyr(l, 1, D)],
        [_row(tm, D), _lyr(l.g, 1, D)], [_sds((t, D), F32), _sds((depth, 1, D), F32)], ("arbitrary",),
        prevs={1: gb.get("mix_pre_g")})
    return dx


def _mm_tn(a, b, tn, name, l, depth, gb):
    t, k = a.shape
    n = b.shape[1]
    tt = min(ROW_TILE, t)

    def body(a_ref, b_ref, o_ref):
        _acc_init(pl.program_id(1), o_ref)
        o_ref[...] += lax.dot_general(a_ref[...], b_ref[...], TN, preferred_element_type=F32)

    gb[name] = _pcall(
        body, "d" + name, (n // tn, t // tt), [a, b],
        [pl.BlockSpec((tt, k), lambda j, s: (s, 0)), pl.BlockSpec((tt, tn), lambda j, s: (s, j))],
        pl.BlockSpec((None, k, tn), lambda j, s: (l.g, 0, j)), _sds((depth, k, n), F32), ("parallel", "arbitrary"),
        prevs={0: gb.get(name)})


def _dw_ffn(a, b, kind, l, depth, gb):
    t = a.shape[0]
    tt = min(ROW_TILE, t)
    nsteps = t // tt

    def body(a_ref, b_ref, o_ref, acc):
        s = pl.program_id(0)
        _acc_init(s, acc)
        acc[...] += lax.dot_general(a_ref[...], b_ref[...], TN, preferred_element_type=F32)

        @pl.when(s == nsteps - 1)
        def _():
            for k in range(4):
                o_ref[k] = acc[k * HP:(k + 1) * HP, :].astype(BF16)

    rows = lambda n: pl.BlockSpec((tt, n), lambda s: (s, 0))
    if kind == "down":
        name = "down"
        out_spec = pl.BlockSpec((4, None, HP, D), lambda s: (0, l.g, 0, 0))
        out_shape = _sds((4, depth, HP, D), BF16)
    else:
        which = 0 if kind == "gate" else 1
        name = "gu"
        out_spec = pl.BlockSpec((4, None, None, HP, D), lambda s: (0, l.g, which, 0, 0))
        out_shape = _sds((4, depth, 2, HP, D), BF16)
    gb[name] = _pcall(body, "dw_" + kind, (nsteps,), [a, b], [rows(DFFP), rows(D)], out_spec, out_shape, ("arbitrary",),
                      scratch=[pltpu.VMEM((DFFP, D), F32)], prevs={0: gb.get(name)})


def _ffn_views(bufs):
    return {"w_gu": bufs[1], "w_down": bufs[2].reshape(bufs[2].shape[:2] + (HP, D))}


def _layer_fwd(x, p, l, tabs, fetch):
    h1, za, zb = _in_proj(x, p, l)
    cqn, ckvn, qs, k, v = _mla_prep(za, p, l, tabs)
    ya, lse, bufs = _attn_fwd(qs, k, v, fetch)
    if fetch:
        p = {**p, **_ffn_views(bufs)}
    mix, yb, yc = _mixer_fwd(zb, ya, p, l)
    o, x2, h2 = _out_proj(mix, x, p, l)
    a, b, s = _ffn_up(h2, p, l)
    f, x3 = _ffn_down(s, x2, p, l)
    saved = dict(x=x, h1=h1, za=za, zb=zb, cqn=cqn, ckvn=ckvn, qs=qs, k=k, v=v, ya=ya, lse=lse, mix=mix, yb=yb, yc=yc,
                 o=o, x2=x2, h2=h2, a=a, b=b, s=s, f=f)
    return x3, saved, bufs if fetch else None


class _Layer(int):
    def __new__(cls, l, g):
        obj = int.__new__(cls, l)
        obj.g = g
        return obj


def _layer_bwd(dx3, p, sv, l, depth, gb, tabs, ride=()):
    df, da, db = _ffn_down_bwd(dx3, sv, p, l, depth, gb)
    _dw_ffn(sv["s"], df, "down", l, depth, gb)
    dx2 = _ffn_up_bwd(da, db, dx3, sv, p, l, depth, gb)
    _dw_ffn(da, sv["h2"], "gate", l, depth, gb)
    _dw_ffn(db, sv["h2"], "up", l, depth, gb)
    do, dmix = _out_proj_bwd(dx2, sv, p, l, depth, gb)
    _mm_tn(sv["mix"], do, D, "w_out", l, depth, gb)
    dya, dzb, delta = _mixer_bwd(dmix, sv, p, l, depth, gb)
    dqs, dk, dv, sent = _attn_bwd(sv["qs"], sv["k"], sv["v"], dya, sv["lse"], delta, ride)
    dza, dqp, dkv = _mla_prep_bwd(dqs, dk, dv, sv, p, l, depth, gb, tabs)
    _mm_tn(sv["cqn"], dqp, QW, "w_uq", l, depth, gb)
    _mm_tn(sv["ckvn"], dkv, KVW, "w_ukv", l, depth, gb)
    _mm_tn(sv["h1"], dza, ZA, "w_in_a", l, depth, gb)
    _mm_tn(sv["h1"], dzb, ZB, "w_in_b", l, depth, gb)
    return _in_proj_bwd(dza, dzb, dx2, sv, p, l, depth, gb), sent


def _rope_tables(positions):
    inv_freq = 1.0 / (ROPE_THETA ** (jnp.arange(0, ROPE // 2, dtype=F32) / (ROPE // 2)))
    ang = positions.astype(F32)[:, None] * inv_freq
    cos, sin = jnp.cos(ang), jnp.sin(ang)
    t = positions.shape[0]
    one, zero = jnp.ones((t, 64), F32), jnp.zeros((t, 16), F32)
    c = jnp.concatenate([one, cos, cos, one[:, :32]], axis=1)
    s1 = jnp.concatenate([zero, zero, zero, zero, -sin, zero, zero, zero], axis=1)
    s2 = jnp.concatenate([zero, zero, zero, zero, zero, sin, zero, zero], axis=1)
    return c, s1, s2


def _mixer_weight_params(full):
    w_in = full["w_in"]
    depth = w_in.shape[0]
    zpad = lambda n: jnp.zeros((depth, D, n), w_in.dtype)
    kv = full["w_ukv"].reshape(depth, KVR, HEADS, NOPE + VD)
    return {
        "w_in_a": jnp.concatenate([w_in[:, :, :640], zpad(64), w_in[:, :, 640:672], zpad(32)], axis=2),
        "w_in_b": w_in[:, :, 672:],
        "w_uq": jnp.pad(full["w_uq"].reshape(depth, QR, HEADS, NOPE + ROPE),
                        ((0, 0), (0, 0), (0, 0), (0, 32))).reshape(depth, QR, QW),
        "w_ukv": jnp.concatenate([jnp.pad(kv[..., :NOPE], ((0, 0), (0, 0), (0, 0), (0, 64))).reshape(depth, KVR, QW),
                                  kv[..., NOPE:].reshape(depth, KVR, HEADS * VD)], axis=2),
        "w_out": full["w_out"], "conv_w": full["conv_w"],
    }


def _small_params(w):
    p = {"w_sp": w["w_sp"], "b_sp": jnp.repeat(jnp.swapaxes(w["b_sp"], 1, 2), 64, axis=2)}
    for n in ("mix_pre_g", "mix_post_g", "ffn_pre_g", "ffn_post_g", "q_norm_g", "kv_norm_g", "sg_ln_g", "sg_ln_b",
              "out_norm_g"):
        p[n] = w[n][:, None, :]
    return p


def _natural_grads(gb):
    depth = gb["w_in_a"].shape[0]
    ga, kv = gb["w_in_a"], gb["w_ukv"]
    out = {
        "w_in": jnp.concatenate([ga[:, :, :640], ga[:, :, 704:736], gb["w_in_b"]], axis=2),
        "w_uq": gb["w_uq"].reshape(depth, QR, HEADS, 128)[..., :NOPE + ROPE].reshape(depth, QR, HEADS * (NOPE + ROPE)),
        "w_ukv": jnp.concatenate([kv[:, :, :QW].reshape(depth, KVR, HEADS, 128)[..., :NOPE],
                                  kv[:, :, QW:].reshape(depth, KVR, HEADS, VD)], axis=3).reshape(depth, KVR, -1),
        "b_sp": jnp.swapaxes(gb["b_sp_t"].reshape(depth, CHUNK, 4, 64).sum(axis=-1), 1, 2),
    }
    for n in ("w_out", "w_sp", "conv_w"):
        out[n] = gb[n]
    for n in ("mix_pre_g", "mix_post_g", "ffn_pre_g", "ffn_post_g", "q_norm_g", "kv_norm_g", "sg_ln_g", "sg_ln_b",
              "out_norm_g"):
        out[n] = gb[n][:, 0, :]
    return out


def _local_step(x, positions, target, small, mine, bufs, shard_shapes, fetch=True, front=None):
    depth = small["w_sp"].shape[0]
    tabs = _rope_tables(positions)
    ps = _small_params(small)
    saved, mixer_w = [], []
    for l in range(depth):
        mixer_w.append(_mixer_weight_params(_unpack_weights(bufs[0], l, shard_shapes)))
        p = {**ps, **mixer_w[l], **_ffn_views(bufs)}
        layers = [l + 1 if l + 1 < depth else None, l, l]
        x, sv, fetched = _layer_fwd(x, p, l, tabs, (mine, bufs, layers) if fetch else None)
        bufs = fetched or bufs
        saved.append(sv)
    dx, acc = _loss_head(x, target)
    loss = (0.5 / D) * jnp.sum(acc)
    nbatch = depth // 2
    gbs = [{} for _ in range(nbatch)]
    fronts, sent = [None] * nbatch, [None] * nbatch
    ride = ()
    for l in reversed(range(depth)):
        b = l // 2
        dx, got = _layer_bwd(dx, {**ps, **mixer_w[l], **_ffn_views(bufs)}, saved[l], _Layer(l, l % 2), 2, gbs[b], tabs, ride)
        if ride:
            sent[b + 1], ride = got, ()
        if l % 2 == 0:
            done = (_natural_grads(gbs[b]), gbs[b]["gu"], gbs[b]["down"])
            fronts[b] = front(b, *done) if front else done
            if front and b > 0:
                ride = tuple(fronts[b])
    return loss, dx, fronts, sent


def _place():
    x, y, c = lax.axis_index("x"), lax.axis_index("y"), lax.axis_index("c")
    chips = [(1 - x, y), (x, 1 - y), (1 - x, 1 - y)]
    return x, y, c, 2 * x + y, chips


def _remote(src, dst, send_sem, recv_sem, to):
    return pltpu.make_async_remote_copy(src_ref=src, dst_ref=dst, send_sem=send_sem, recv_sem=recv_sem, device_id=to,
                                        device_id_type=MESH_ID)


def _gather_ops(mine_refs, out_refs, sems, layers):
    send_sems, recv_sems, fsend_sems, frecv_sems = sems
    x, y, c, k, chips = _place()
    sib = (x, y, 1 - c)
    pairs = [(b, n) for n in range(3) for b in range(len(mine_refs)) if layers[b] is not None]

    def slot(n):
        return 2 * chips[n][0] + chips[n][1]

    def ici(b, n, dst_chip):
        return _remote(mine_refs[b].at[layers[b], c], out_refs[b].at[dst_chip, layers[b], c], send_sems.at[b, n],
                       recv_sems.at[b, n], (*chips[n], c))

    def d2d(b, n, half):
        piece = out_refs[b].at[slot(n), layers[b], half]
        return _remote(piece, piece, fsend_sems.at[b, n], frecv_sems.at[b, n], sib)

    def start():
        for b, n in pairs:
            ici(b, n, k).start()

    def hand_over():
        for b, n in pairs:
            ici(b, n, slot(n)).wait_recv()
            d2d(b, n, c).start()

    def drain():
        for b, n in pairs:
            d2d(b, n, 1 - c).wait_recv()
        for b, n in pairs:
            ici(b, n, k).wait_send()
            d2d(b, n, c).wait_send()

    return start, hand_over, drain


def _gather_first_layer(mine):
    nb = len(mine)

    def body(*refs):
        start, hand_over, drain = _gather_ops(refs[:nb], refs[nb:2 * nb], refs[2 * nb:], [0] + [None] * (nb - 1))
        start()
        hand_over()
        drain()

    return pl.pallas_call(
        body, name="gather_first_layer", in_specs=[ANY] * nb, out_specs=[ANY] * nb,
        out_shape=[_sds((4,) + a.shape, a.dtype) for a in mine],
        scratch_shapes=[pltpu.SemaphoreType.DMA((nb, 3))] * 4)(*mine)


def _swap_halves(bigs, wholes=()):
    nb, n = len(bigs), len(bigs) + len(wholes)

    def body(*refs):
        src, dst = refs[:n], refs[n:2 * n]
        send_sems, recv_sems = refs[2 * n:]
        x, y, c, _, _ = _place()
        sib = (x, y, 1 - c)
        cps = [_remote(src[b].at[:, 1 - c] if b < nb else src[b], dst[b], send_sems.at[b], recv_sems.at[b], sib)
               for b in range(n)]
        for cp in cps:
            cp.start()
        for cp in cps:
            cp.wait()

    return pl.pallas_call(
        body, name="swap_halves", in_specs=[ANY] * n, out_specs=[ANY] * n,
        out_shape=[_sds((4,) + a.shape[2:], a.dtype) for a in bigs] + [_sds(a.shape, a.dtype) for a in wholes],
        scratch_shapes=[pltpu.SemaphoreType.DMA((n,))] * 2)(*bigs, *wholes)


def _sum_tile(r):
    return max(cand for cand in range(16, 641, 16) if r % cand == 0)


def _pair_sum(big, rbig, c):
    _, _, r, w = big.shape
    tr = _sum_tile(r)

    def body(c_ref, big_ref, rbig_ref, p_ref):
        p_ref[...] = (big_ref[...].astype(F32) + rbig_ref[...].astype(F32)).astype(BF16)

    grid_spec = pltpu.PrefetchScalarGridSpec(
        num_scalar_prefetch=1, grid=(4, r // tr),
        in_specs=[pl.BlockSpec((None, None, tr, w), lambda j, i, cr: (j, cr[0], i, 0)),
                  pl.BlockSpec((None, tr, w), lambda j, i, cr: (j, i, 0))],
        out_specs=pl.BlockSpec((None, tr, w), lambda j, i, cr: (j, i, 0)))
    return pl.pallas_call(body, name="pair_sum", grid_spec=grid_spec, out_shape=_sds((4, r, w), BF16),
                          compiler_params=_cp("parallel", "parallel"))(c, big, rbig)


def _small_sum(parts):
    n, ns, _ = parts.shape

    def body(p_ref, o_ref):
        s = p_ref[0]
        for j in range(1, n):
            s = s + p_ref[j]
        o_ref[...] = s

    return pl.pallas_call(body, name="small_sum", out_shape=_sds((ns, 128), F32))(parts)


def _exchange_ops(p_refs, rb_refs, sems, small=None):
    send_sems, recv_sems = sems[0], sems[1]
    nb = len(p_refs)
    x, y, c, k, chips = _place()

    def copies(landing):
        out = []
        for n, (cx, cy) in enumerate(chips):
            to, kj = (cx, cy, c), 2 * cx + cy
            for b in range(nb):
                out.append(_remote(p_refs[b].at[k if landing else kj], rb_refs[b].at[n], send_sems.at[b, n],
                                   recv_sems.at[b, n], to))
            if small:
                out.append(_remote(small[0], small[1].at[kj if landing else k], send_sems.at[nb, n], recv_sems.at[nb, n], to))
        return out

    def local():
        return pltpu.make_async_copy(small[0], small[1].at[k], sems[2])

    def start():
        if small:
            local().start()
        for cp in copies(False):
            cp.start()

    def finish():
        for cp in copies(True):
            cp.wait_recv()
        for cp in copies(False):
            cp.wait_send()
        if small:
            local().wait()

    return start, finish


def _chip_exchange(ps, small):
    nb = len(ps)
    ns = small.shape[0]

    def body(*refs):
        start, finish = _exchange_ops(refs[:nb], refs[nb + 1:2 * nb + 1], refs[2 * nb + 2:], (refs[nb], refs[2 * nb + 1]))
        start()
        finish()

    return pl.pallas_call(
        body, name="chip_exchange", in_specs=[ANY] * (nb + 1), out_specs=[ANY] * (nb + 1),
        out_shape=[_sds((3,) + a.shape[1:], a.dtype) for a in ps] + [_sds((4, ns, 128), small.dtype)],
        scratch_shapes=[pltpu.SemaphoreType.DMA((nb + 1, 3))] * 2 + [pltpu.SemaphoreType.DMA(())])(*ps, small)


def _chip_sum(p, rb, chip):
    _, r, w = p.shape
    tr = _sum_tile(r)

    def body(k_ref, p_ref, rb_ref, o_ref):
        acc = p_ref[...].astype(F32)
        for j in range(3):
            acc = acc + rb_ref[j].astype(F32)
        o_ref[...] = acc

    grid_spec = pltpu.PrefetchScalarGridSpec(
        num_scalar_prefetch=1, grid=(r // tr,),
        in_specs=[pl.BlockSpec((None, tr, w), lambda i, kr: (kr[0], i, 0)), pl.BlockSpec((3, tr, w), lambda i, kr: (0, i, 0))],
        out_specs=pl.BlockSpec((tr, w), lambda i, kr: (i, 0)))
    return pl.pallas_call(body, name="chip_sum", grid_spec=grid_spec, out_shape=_sds((r, w), F32),
                          compiler_params=_cp("parallel"))(chip, p, rb)


def _send_to_sibling(reds):
    nb = len(reds)

    def body(*refs):
        red_refs, out_refs = refs[:nb], refs[nb:2 * nb]
        send_sems, recv_sems = refs[2 * nb:]
        x, y, c, _, _ = _place()
        cps = [_remote(red_refs[b], out_refs[b], send_sems.at[b], recv_sems.at[b], (x, y, 1 - c)) for b in range(nb)]
        for cp in cps:
            cp.start()
        for cp in cps:
            cp.wait()

    return pl.pallas_call(
        body, name="send_to_sibling", in_specs=[ANY] * nb, out_specs=[ANY] * nb,
        out_shape=[_sds(a.shape, a.dtype) for a in reds], scratch_shapes=[pltpu.SemaphoreType.DMA((nb,))] * 2)(*reds)


def _adam_math(w, g, m, v):
    nm = ADAM_B1 * m + (1.0 - ADAM_B1) * g
    nv = ADAM_B2 * v + (1.0 - ADAM_B2) * (g * g)
    m_hat = nm / (1.0 - ADAM_B1 ** ADAM_STEP)
    v_hat = nv / (1.0 - ADAM_B2 ** ADAM_STEP)
    return -ADAM_LR * (m_hat / (jnp.sqrt(v_hat) + ADAM_EPS) + ADAM_WD * w), nm, nv


def _adamw_shard(w, m, v, owns, others, c, name, pick=None):
    depth, r, n = w.shape
    nbatch = len(owns)
    tr = max(cand for cand in range(8, min(r, 256) + 1, 8) if r % cand == 0)
    npad = owns[0].shape[-1]

    def body(c_ref, w_ref, m_ref, v_ref, *refs):
        g_ref, d_ref, nm_ref, nv_ref = refs[2 * nbatch:]
        l = pl.program_id(0)
        mine = (l % 2) == c_ref[0]
        g = jnp.where(mine, refs[0][...], refs[nbatch][...])
        for b in range(1, nbatch):
            g = jnp.where(l // 2 == b, jnp.where(mine, refs[b][...], refs[nbatch + b][...]), g)
        g = g[:, :n]
        g_ref[...] = g
        d_ref[...], nm_ref[...], nv_ref[...] = _adam_math(w_ref[...], g, m_ref[...], v_ref[...])

    blk = pl.BlockSpec((None, tr, n), lambda l, i, cr: (l, i, 0))
    if pick is None:
        gblk = pl.BlockSpec((tr, npad), lambda l, i, cr: (i, 0))
    else:
        gblk = pl.BlockSpec((None, tr, npad), lambda l, i, cr: (pick, i, 0))
    grid_spec = pltpu.PrefetchScalarGridSpec(num_scalar_prefetch=1, grid=(depth, r // tr),
                                             in_specs=[blk] * 3 + [gblk] * (2 * nbatch), out_specs=[blk] * 4)
    return pl.pallas_call(body, name=name, grid_spec=grid_spec, out_shape=[_sds(w.shape, F32)] * 4,
                          compiler_params=_cp("parallel", "parallel"))(c, w, m, v, *owns, *others)


def _pad_ffn_shards(w_gate, w_up, w_down):
    depth = w_gate.shape[0]
    hr = HP // 2

    def gu_body(g_ref, u_ref, o_ref):
        for which, ref in enumerate((g_ref, u_ref)):
            o_ref[which, 0:HS, :] = ref[...].astype(BF16)
            o_ref[which, HS:HP, :] = jnp.zeros((HP - HS, D), BF16)

    blk = pl.BlockSpec((None, HS, D), lambda l: (l, 0, 0))
    gu = pl.pallas_call(
        gu_body, name="pad_gate_up", grid=(depth,), in_specs=[blk, blk],
        out_specs=pl.BlockSpec((None, 2, HP, D), lambda l: (l, 0, 0, 0)),
        out_shape=_sds((depth, 2, HP, D), BF16), compiler_params=_cp("parallel"))(w_gate, w_up)

    def down_body(w_ref, o_ref):
        o_ref[0] = w_ref[0:hr, :].astype(BF16)
        o_ref[1, 0:HS - hr, :] = w_ref[hr:HS, :].astype(BF16)
        o_ref[1, HS - hr:hr, :] = jnp.zeros((HP - HS, D), BF16)

    down = pl.pallas_call(
        down_body, name="pad_down", grid=(depth,), in_specs=[pl.BlockSpec((None, HS, D), lambda l: (l, 0, 0))],
        out_specs=pl.BlockSpec((None, 2, hr, D), lambda l: (l, 0, 0, 0)),
        out_shape=_sds((depth, 2, hr, D), BF16), compiler_params=_cp("parallel"))(w_down)
    return gu, down


def _adamw_small(w, g, m, v):
    r = w.shape[0]
    tr = max(cand for cand in range(8, 513, 8) if r % cand == 0)

    def body(w_ref, g_ref, m_ref, v_ref, d_ref, nm_ref, nv_ref):
        d_ref[...], nm_ref[...], nv_ref[...] = _adam_math(w_ref[...], g_ref[...], m_ref[...], v_ref[...])

    blk = pl.BlockSpec((tr, 128), lambda i: (i, 0))
    return pl.pallas_call(body, name="adamw_small", grid=(r // tr,), in_specs=[blk] * 4, out_specs=[blk] * 3,
                          out_shape=[_sds(w.shape, F32)] * 3, compiler_params=_cp("parallel"))(w, g, m, v)


def _to_pack(a, name):
    depth = a.shape[0]
    if name in ROW_SHARDED:
        return jnp.swapaxes(a.reshape(depth, 4, -1, D), 0, 1)
    return jnp.transpose(a.reshape(depth, a.shape[1], 4, a.shape[2] // 4), (2, 0, 1, 3)).reshape(4, depth, -1, D)


def _pack_rows(parts, lead, dtype, tail=None):
    pieces, at = [], 0
    for n, off, rows in PACK:
        if off > at:
            pieces.append(jnp.zeros(lead + (off - at, D), dtype))
        pieces.append(parts[n].astype(dtype))
        at = off + rows
    if tail is not None:
        pieces.append(tail)
        at += tail.shape[-2]
    pieces.append(jnp.zeros(lead + (PACK_ROWS - at, D), dtype))
    return jnp.concatenate(pieces, axis=len(lead))


def _pack_weight_shards(sh):
    depth = sh["w_in"].shape[0]
    parts = {n: sh[n].reshape(depth, rows, D) for n, _, rows in PACK}
    conv = lax.bitcast_convert_type(sh["conv_w"].reshape(depth, 3 * 64), BF16).reshape(depth, 1, 384)
    flat = _pack_rows(parts, (depth,), BF16, tail=jnp.pad(conv, ((0, 0), (0, 0), (0, D - 384))))
    return flat.reshape(depth, 2, PACK_ROWS // 2, D)


def _unpack_weights(gathered, l, shard_shapes):
    depth = 1
    flat = gathered[:, l].reshape(4, 1, PACK_ROWS, D)
    full = {}
    for n, off, rows in PACK:
        shp = shard_shapes[n][1:]
        piece = flat[:, :, off:off + rows, :].reshape((4, depth) + shp)
        if n in ROW_SHARDED:
            full[n] = jnp.transpose(piece, (1, 0, 2, 3)).reshape(depth, 4 * shp[0], shp[1])
        else:
            full[n] = jnp.transpose(piece, (1, 2, 0, 3)).reshape(depth, shp[0], 4 * shp[1])
    conv = lax.bitcast_convert_type(flat[:, :, CONV_ROW, :384].reshape(4, depth, 192, 2), F32)
    full["conv_w"] = jnp.transpose(conv.reshape(4, depth, 3, 64), (1, 2, 0, 3)).reshape(depth, 3, CVW)
    return full


def _pack_grad_shards(g):
    depth = g["w_in"].shape[0]
    return _pack_rows({n: _to_pack(g[n], n) for n, _, _ in PACK}, (4, depth), BF16)


def _pack_small(arrs, names_shapes, depth):
    flat = jnp.concatenate([arrs[n].reshape(depth, -1) for n, _ in names_shapes], axis=1).reshape(-1)
    rows = -(-flat.shape[0] // 1024) * 8
    return jnp.pad(flat, (0, rows * 128 - flat.shape[0])).reshape(rows, 128)


def _unpack_small(packed, names_shapes, depth):
    per_layer = sum(math.prod(s) for _, s in names_shapes)
    flat = packed.reshape(-1)[:depth * per_layer].reshape(depth, per_layer)
    out, off = {}, 0
    for n, s in names_shapes:
        size = math.prod(s)
        out[n] = flat[:, off:off + size].reshape((depth,) + s)
        off += size
    return out


def kernel(x, positions, mix_pre_g, mix_post_g, ffn_pre_g, ffn_post_g, w_in, q_norm_g, w_uq, kv_norm_g, w_ukv, sg_ln_g, sg_ln_b, w_sp, b_sp, conv_w, out_norm_g, w_out, w_gate, w_up, w_down, loss_target, m_mix_pre_g, m_mix_post_g, m_ffn_pre_g, m_ffn_post_g, m_w_in, m_q_norm_g, m_w_uq, m_kv_norm_g, m_w_ukv, m_sg_ln_g, m_sg_ln_b, m_w_sp, m_b_sp, m_conv_w, m_out_norm_g, m_w_out, m_w_gate, m_w_up, m_w_down, v_mix_pre_g, v_mix_post_g, v_ffn_pre_g, v_ffn_post_g, v_w_in, v_q_norm_g, v_w_uq, v_kv_norm_g, v_w_ukv, v_sg_ln_g, v_sg_ln_b, v_w_sp, v_b_sp, v_conv_w, v_out_norm_g, v_w_out, v_w_gate, v_w_up, v_w_down):
    w = dict(mix_pre_g=mix_pre_g, mix_post_g=mix_post_g, ffn_pre_g=ffn_pre_g, ffn_post_g=ffn_post_g, w_in=w_in,
             q_norm_g=q_norm_g, w_uq=w_uq, kv_norm_g=kv_norm_g, w_ukv=w_ukv, sg_ln_g=sg_ln_g, sg_ln_b=sg_ln_b, w_sp=w_sp,
             b_sp=b_sp, conv_w=conv_w, out_norm_g=out_norm_g, w_out=w_out, w_gate=w_gate, w_up=w_up, w_down=w_down)
    m = dict(mix_pre_g=m_mix_pre_g, mix_post_g=m_mix_post_g, ffn_pre_g=m_ffn_pre_g, ffn_post_g=m_ffn_post_g, w_in=m_w_in,
             q_norm_g=m_q_norm_g, w_uq=m_w_uq, kv_norm_g=m_kv_norm_g, w_ukv=m_w_ukv, sg_ln_g=m_sg_ln_g, sg_ln_b=m_sg_ln_b,
             w_sp=m_w_sp, b_sp=m_b_sp, conv_w=m_conv_w, out_norm_g=m_out_norm_g, w_out=m_w_out, w_gate=m_w_gate,
             w_up=m_w_up, w_down=m_w_down)
    v = dict(mix_pre_g=v_mix_pre_g, mix_post_g=v_mix_post_g, ffn_pre_g=v_ffn_pre_g, ffn_post_g=v_ffn_post_g, w_in=v_w_in,
             q_norm_g=v_q_norm_g, w_uq=v_w_uq, kv_norm_g=v_kv_norm_g, w_ukv=v_w_ukv, sg_ln_g=v_sg_ln_g, sg_ln_b=v_sg_ln_b,
             w_sp=v_w_sp, b_sp=v_b_sp, conv_w=v_conv_w, out_norm_g=v_out_norm_g, w_out=v_w_out, w_gate=v_w_gate,
             w_up=v_w_up, w_down=v_w_down)
    depth = w_in.shape[0]
    c = lax.axis_index("c").astype(jnp.int32).reshape(1)
    chip = (2 * lax.axis_index("x") + lax.axis_index("y")).astype(jnp.int32)

    nbatch = depth // 2

    mine = [_pack_weight_shards(w), *_pad_ffn_shards(jnp.swapaxes(w_gate, 1, 2), jnp.swapaxes(w_up, 1, 2), w_down)]
    bufs = [lax.dynamic_update_slice(g, a[None], (chip,) + (0,) * a.ndim)
            for g, a in zip(_gather_first_layer(mine), mine)]

    small_grads = [None] * nbatch
    small_pair = []

    def front(b, grads, g_gu, g_down):
        small_grads[b] = grads
        bigs = [_pack_grad_shards(grads), g_gu.reshape(4, 2, 2 * HP, D), g_down]
        if b > 0:
            rbigs = _swap_halves(bigs)
        else:
            small = _pack_small({n: jnp.concatenate([g[n] for g in small_grads]) for n, _ in SMALL}, SMALL, depth)
            *rbigs, rsmall = _swap_halves(bigs, [small])
            small_pair.append(_small_sum(jnp.stack([small, rsmall])))
        return [_pair_sum(a, r, c) for a, r in zip(bigs, rbigs)]

    loss, dx, ps, sent = _local_step(x[0], positions[0], loss_target[0], w, mine, bufs,
                                     {n: w[n].shape for n, _, _ in PACK}, front=front)
    loss = lax.psum(loss, ("x", "y", "c"))

    *sent[0], rs = _chip_exchange(ps[0], small_pair[0])
    own = [[_chip_sum(p, rb, chip.reshape(1)) for p, rb in zip(ps[b], sent[b])] for b in range(nbatch)]
    other = [_send_to_sibling(o) for o in own]
    g_small = _unpack_small(_small_sum(rs), SMALL, depth)
    g_small["conv_w"] = lax.dynamic_slice_in_dim(g_small["conv_w"], chip * 64, 64, axis=2)

    gw, delta, new_m, new_v = dict(g_small), {}, {}, {}

    def adam(n, pieces, pick=None, turned=False):
        view = (lambda a: jnp.swapaxes(a, 1, 2)) if turned else (lambda a: a)
        outs = _adamw_shard(view(w[n]), view(m[n]), view(v[n]), [pieces(o) for o in own], [pieces(o) for o in other], c,
                            "adamw_" + n, pick)
        gw[n], delta[n], new_m[n], new_v[n] = [view(o) for o in outs]

    for n, off, rows in PACK:
        adam(n, lambda o: o[0][off:off + rows, :].reshape(w[n].shape[1:]))
    adam("w_gate", lambda o: o[1].reshape(2, HP, D), 0, turned=True)
    adam("w_up", lambda o: o[1].reshape(2, HP, D), 1, turned=True)
    adam("w_down", lambda o: o[2])
    small_local = tuple((n, w[n].shape[1:]) for n, _ in SMALL)
    d_, m_, v_ = _adamw_small(_pack_small(w, small_local, depth), _pack_small(gw, small_local, depth),
                              _pack_small(m, small_local, depth), _pack_small(v, small_local, depth))
    delta.update(_unpack_small(d_, small_local, depth))
    new_m.update(_unpack_small(m_, small_local, depth))
    new_v.update(_unpack_small(v_, small_local, depth))

    return (loss, dx[None], *[gw[n] for n in WEIGHTS], *[delta[n] for n in WEIGHTS], *[new_m[n] for n in WEIGHTS],
            *[new_v[n] for n in WEIGHTS])
```

```python
import math

import jax
import jax.numpy as jnp
from jax import lax
from jax.experimental import pallas as pl
from jax.experimental.pallas import tpu as pltpu

F32 = jnp.float32
BF16 = jnp.bfloat16

D = 1024
HEADS = 8
NOPE = 64
ROPE = 32
VD = 64
QR = 384
KVR = 256
SGW = 256
CVW = 256
CHUNK = 128
DFF = 2816
EPS = 1e-6
ROPE_THETA = 10000.0
LOG2E = 1.4426950408889634
LN2 = 0.6931471805599453
QSCALE = (NOPE + ROPE) ** -0.5 * LOG2E
ZA = 768
ZB = 1280
QW = HEADS * 128
KVW = HEADS * 128 + HEADS * VD
NEG = -1e30
GC0 = 0.7978845608028654
GC1 = 0.044715

ADAM_LR = 0.001
ADAM_B1 = 0.9
ADAM_B2 = 0.999
ADAM_EPS = 1e-08
ADAM_WD = 0.01
ADAM_STEP = 10

V7X_VMEM_LIMIT = 52 * 1024 * 1024
ROW_TILE = 512
ATT_TILE = 512
ATT_HEADS = 4

NT = (((1,), (1,)), ((), ()))
TN = (((0,), (0,)), ((), ()))

HS = DFF // 4
HP = 768
DFFP = 4 * HP

PACK = (("w_in", 0, 488), ("w_out", 512, 256), ("w_ukv", 768, 64), ("w_uq", 832, 72))
CONV_ROW = 904
PACK_ROWS = 928
ROW_SHARDED = ("w_out",)
SMALL = (("mix_pre_g", (D,)), ("mix_post_g", (D,)), ("ffn_pre_g", (D,)), ("ffn_post_g", (D,)), ("q_norm_g", (QR,)),
         ("kv_norm_g", (KVR,)), ("sg_ln_g", (SGW,)), ("sg_ln_b", (SGW,)), ("w_sp", (4, CHUNK, CHUNK)), ("b_sp", (4, CHUNK)),
         ("conv_w", (3, CVW)), ("out_norm_g", (D,)))
WEIGHTS = ["mix_pre_g", "mix_post_g", "ffn_pre_g", "ffn_post_g", "w_in", "q_norm_g", "w_uq", "kv_norm_g", "w_ukv", "sg_ln_g",
           "sg_ln_b", "w_sp", "b_sp", "conv_w", "out_norm_g", "w_out", "w_gate", "w_up", "w_down"]

MESH_ID = pl.DeviceIdType.MESH
ANY = pl.BlockSpec(memory_space=pl.ANY)


def _cp(*sem):
    return pltpu.CompilerParams(dimension_semantics=sem, vmem_limit_bytes=V7X_VMEM_LIMIT)


def _sds(shape, dtype):
    return jax.ShapeDtypeStruct(shape, dtype)


def _row(tm, n):
    return pl.BlockSpec((tm, n), lambda i: (i, 0))


def _lyr(l, *shape):
    return pl.BlockSpec((None,) + shape, lambda *_: (l,) + (0,) * len(shape))


def _wl(a, l):
    return 0 if a.shape[0] == 1 else l


def _pcall(body, name, grid, ins, in_specs, out_specs, out_shape, sem, scratch=(), prevs=None):
    prevs = {k: v for k, v in (prevs or {}).items() if v is not None}
    order = sorted(prevs)
    n_in = len(ins)

    def wrapped(*refs):
        return body(*refs[:n_in], *refs[n_in + len(order):])

    return pl.pallas_call(
        wrapped, name=name, grid=grid, in_specs=list(in_specs) + [ANY] * len(order), out_specs=out_specs,
        out_shape=out_shape, scratch_shapes=list(scratch),
        input_output_aliases={n_in + i: k for i, k in enumerate(order)},
        compiler_params=_cp(*sem))(*ins, *[prevs[k] for k in order])


def _rms(x, g):
    r = lax.rsqrt(jnp.mean(x * x, axis=-1, keepdims=True) + EPS)
    return x * r * g


def _rms_bwd(x, g, dy):
    r = lax.rsqrt(jnp.mean(x * x, axis=-1, keepdims=True) + EPS)
    xh = x * r
    dg = jnp.sum(dy * xh, axis=0, keepdims=True)
    dxh = dy * g
    dx = r * (dxh - xh * jnp.mean(dxh * xh, axis=-1, keepdims=True))
    return dx, dg


def _gelu(x):
    return 0.5 * x * (1.0 + jnp.tanh(GC0 * (x + GC1 * x * x * x)))


def _gelu_grad(x):
    t = jnp.tanh(GC0 * (x + GC1 * x * x * x))
    return 0.5 * (1.0 + t) + 0.5 * x * (1.0 - t * t) * GC0 * (1.0 + 3.0 * GC1 * x * x)


def _rope(xb, c, s1, s2):
    return xb * c + pltpu.roll(xb, 112, 1) * s1 + pltpu.roll(xb, 16, 1) * s2


def _rope_bwd(dy, c, s1, s2):
    return dy * c + pltpu.roll(dy * s1, 16, 1) + pltpu.roll(dy * s2, 112, 1)


def _group_masks(shape):
    lane = lax.broadcasted_iota(jnp.int32, shape, 1)
    return [(lane >= 64 * g) & (lane < 64 * g + 64) for g in range(shape[1] // 64)]


def _group_mean(v, masks):
    out = jnp.zeros_like(v)
    for m in masks:
        s = jnp.sum(jnp.where(m, v, 0.0), axis=-1, keepdims=True) * (1.0 / 64.0)
        out = jnp.where(m, s, out)
    return out


def _pick_row(blk, idx):
    row = lax.broadcasted_iota(jnp.int32, blk.shape, 0)
    return jnp.sum(jnp.where(row == idx, blk, 0.0), axis=0, keepdims=True)


def _shift_down(y, k, first_rows):
    out = pltpu.roll(y, k, 0)
    row = lax.broadcasted_iota(jnp.int32, y.shape, 0)
    for idx in range(k):
        out = jnp.where(row == idx, first_rows[idx], out)
    return out


def _shift_up(y, k, last_rows):
    n = y.shape[0]
    out = pltpu.roll(y, n - k, 0)
    row = lax.broadcasted_iota(jnp.int32, y.shape, 0)
    for idx in range(k):
        out = jnp.where(row == n - k + idx, last_rows[idx], out)
    return out


def _tril_mask():
    r = lax.broadcasted_iota(jnp.int32, (CHUNK, CHUNK), 0)
    c = lax.broadcasted_iota(jnp.int32, (CHUNK, CHUNK), 1)
    return r >= c


def _sgu_forward(zu, zv, g_ln, b_ln, wc_bf, bsp, masks, cmasks):
    u = _gelu(zu)
    vv = _gelu(zv)
    mu = _group_mean(vv, masks)
    dv = vv - mu
    rs = lax.rsqrt(_group_mean(dv * dv, masks) + EPS)
    xh = dv * rs
    vn = xh * g_ln + b_ln
    chunks = []
    for ci in range(zu.shape[0] // CHUNK):
        vc = vn[ci * CHUNK:(ci + 1) * CHUNK, :]
        acc = bsp
        for g in range(4):
            acc = acc + jnp.dot(wc_bf[g], jnp.where(cmasks[g], vc, 0.0).astype(BF16), preferred_element_type=F32)
        chunks.append(acc)
    mixed = jnp.concatenate(chunks, axis=0) if len(chunks) > 1 else chunks[0]
    return u, vv, xh, rs, vn, mixed


def _conv_forward(gc, hh, prev_gc, prev_hh, first_tile, cw):
    yv = gc * hh
    prev = jnp.where(first_tile, 0.0, prev_gc * prev_hh)
    p6, p7 = _pick_row(prev, 6), _pick_row(prev, 7)
    sh1 = _shift_down(yv, 1, [p7])
    sh2 = _shift_down(yv, 2, [p6, p7])
    conv = sh2 * cw[0:1, :] + sh1 * cw[1:2, :] + yv * cw[2:3, :]
    return yv, sh1, sh2, conv


def _acc_init(step, *refs):
    @pl.when(step == 0)
    def _():
        for r in refs:
            r[...] = jnp.zeros(r.shape, r.dtype)


def _in_proj(x, p, l):
    t = x.shape[0]
    tm = min(ROW_TILE, t)

    def body(x_ref, g_ref, wa_ref, wb_ref, h_ref, za_ref, zb_ref):
        h = _rms(x_ref[...], g_ref[...]).astype(BF16)
        h_ref[...] = h
        za_ref[...] = jnp.dot(h, wa_ref[...], preferred_element_type=F32)
        zb_ref[...] = jnp.dot(h, wb_ref[...], preferred_element_type=F32)

    return _pcall(
        body, "in_proj", (t // tm,), [x, p["mix_pre_g"], p["w_in_a"], p["w_in_b"]],
        [_row(tm, D), _lyr(l, 1, D), _lyr(_wl(p["w_in_a"], l), D, ZA), _lyr(_wl(p["w_in_b"], l), D, ZB)],
        [_row(tm, D), _row(tm, ZA), _row(tm, ZB)],
        [_sds((t, D), BF16), _sds((t, ZA), F32), _sds((t, ZB), F32)], ("parallel",))


def _mla_prep(za, p, l, tabs):
    t = za.shape[0]
    tm = min(ROW_TILE, t)

    def body(z_ref, gq_ref, gkv_ref, wuq_ref, wukv_ref, c_ref, s1_ref, s2_ref, cq_ref, ckv_ref, q_ref, k_ref, v_ref):
        z = z_ref[...]
        cq = _rms(z[:, :QR], gq_ref[...]).astype(BF16)
        ckv = _rms(z[:, QR:QR + KVR], gkv_ref[...]).astype(BF16)
        cq_ref[...] = cq
        ckv_ref[...] = ckv
        c, s1, s2 = c_ref[...], s1_ref[...], s2_ref[...]
        kr = _rope(z[:, QR + KVR:], c, s1, s2)
        q = jnp.dot(cq, wuq_ref[...], preferred_element_type=F32)
        kv = jnp.dot(ckv, wukv_ref[...], preferred_element_type=F32)
        for h in range(HEADS):
            sl = slice(128 * h, 128 * h + 128)
            q_ref[:, sl] = (_rope(q[:, sl], c, s1, s2) * QSCALE).astype(BF16)
            k_ref[:, sl] = (kv[:, sl] + kr).astype(BF16)
        v_ref[...] = kv[:, QW:].astype(BF16)

    return _pcall(
        body, "mla_prep", (t // tm,), [za, p["q_norm_g"], p["kv_norm_g"], p["w_uq"], p["w_ukv"], *tabs],
        [_row(tm, ZA), _lyr(l, 1, QR), _lyr(l, 1, KVR), _lyr(_wl(p["w_uq"], l), QR, QW), _lyr(_wl(p["w_ukv"], l), KVR, KVW),
         _row(tm, 128), _row(tm, 128), _row(tm, 128)],
        [_row(tm, QR), _row(tm, KVR), _row(tm, QW), _row(tm, QW), _row(tm, HEADS * VD)],
        [_sds((t, QR), BF16), _sds((t, KVR), BF16), _sds((t, QW), BF16), _sds((t, QW), BF16),
         _sds((t, HEADS * VD), BF16)], ("parallel",))


def _att_tile(t):
    return min(ATT_TILE, max(t // 2, 128))


def _causal_keep(tq, i, j):
    row = lax.broadcasted_iota(jnp.int32, (tq, tq), 0) + i * tq
    col = lax.broadcasted_iota(jnp.int32, (tq, tq), 1) + j * tq
    return col <= row


def _attn_fwd(qs, k, v, fetch=None):
    t = qs.shape[0]
    tq = _att_tile(t)
    nq = t // tq
    rep = tq // 128
    groups = HEADS // ATT_HEADS
    mine, bufs, fetch_layer = fetch if fetch else ((), (), None)
    nb = len(mine)

    steps = [(i, j) for i in range(nq) for j in range(i + 1)]
    i_of = jnp.asarray([s[0] for s in steps], jnp.int32)
    j_of = jnp.asarray([s[1] for s in steps], jnp.int32)

    def body(i_ref, j_ref, q_ref, k_ref, v_ref, *refs):
        o_ref, lse_ref = refs[2 * nb:2 * nb + 2]
        m_s, l_s, acc_s = refs[3 * nb + 2:3 * nb + 5]
        step_no = pl.program_id(1)
        i, j = i_ref[step_no], j_ref[step_no]
        if fetch:
            start, hand_over, drain = _gather_ops(refs[:nb], refs[2 * nb + 2:3 * nb + 2], refs[3 * nb + 5:], fetch_layer)
            pr = pl.program_id(0)
            pl.when((pr == 0) & (step_no == 0))(start)
            pl.when((pr == groups - 1) & (step_no == 0))(hand_over)
            pl.when((pr == groups - 1) & (step_no == len(steps) - 1))(drain)

        @pl.when(j == 0)
        def _():
            m_s[...] = jnp.full(m_s.shape, NEG, F32)
            l_s[...] = jnp.zeros(l_s.shape, F32)
            acc_s[...] = jnp.zeros(acc_s.shape, F32)

        def step(masked):
            keep = _causal_keep(tq, i, j) if masked else None
            for hh in range(ATT_HEADS):
                sl = slice(128 * hh, 128 * hh + 128)
                vv = v_ref[:, 128 * (hh // 2):128 * (hh // 2) + 128]
                s = lax.dot_general(q_ref[:, sl], k_ref[:, sl], NT, preferred_element_type=F32)
                if masked:
                    s = jnp.where(keep, s, NEG)
                m_old = m_s[hh]
                m_new = jnp.maximum(m_old, jnp.max(s, axis=-1, keepdims=True))
                alpha = jnp.exp2(m_old - m_new)
                p = jnp.exp2(s - jnp.tile(m_new, (1, rep)))
                l_s[hh] = alpha * l_s[hh] + jnp.sum(p, axis=-1, keepdims=True)
                acc_s[hh] = alpha * acc_s[hh] + jnp.dot(p.astype(BF16), vv, preferred_element_type=F32)
                m_s[hh] = m_new

        @pl.when(j < i)
        def _():
            step(False)

        @pl.when(j == i)
        def _():
            step(True)
            lane = lax.broadcasted_iota(jnp.int32, (tq, 128), 1)
            for pp in range(ATT_HEADS // 2):
                a, b = 2 * pp, 2 * pp + 1
                o_ref[:, 128 * pp:128 * pp + 128] = jnp.where(lane < VD, acc_s[a] / l_s[a], acc_s[b] / l_s[b])
            for hh in range(ATT_HEADS):
                lse_ref[hh] = (m_s[hh] + jnp.log2(l_s[hh]))[:, 0:1]

    qw, vw = 128 * ATT_HEADS, VD * ATT_HEADS
    stat = pltpu.VMEM((ATT_HEADS, tq, 128), F32)
    grid_spec = pltpu.PrefetchScalarGridSpec(
        num_scalar_prefetch=2, grid=(groups, len(steps)),
        in_specs=[pl.BlockSpec((tq, qw), lambda p, s, it, jt: (it[s], p)),
                  pl.BlockSpec((tq, qw), lambda p, s, it, jt: (jt[s], p)),
                  pl.BlockSpec((tq, vw), lambda p, s, it, jt: (jt[s], p))] + [ANY] * (2 * nb),
        out_specs=[pl.BlockSpec((tq, vw), lambda p, s, it, jt: (it[s], p)),
                   pl.BlockSpec((ATT_HEADS, tq, 1), lambda p, s, it, jt: (p, it[s], 0))] + [ANY] * nb,
        scratch_shapes=[stat, stat, stat] + ([pltpu.SemaphoreType.DMA((nb, 3))] * 4 if fetch else []))
    outs = pl.pallas_call(
        body, name="attn_fwd_fetch" if fetch else "attn_fwd", grid_spec=grid_spec,
        out_shape=[_sds((t, HEADS * VD), F32), _sds((HEADS, t, 1), F32)] + [_sds(b.shape, b.dtype) for b in bufs],
        input_output_aliases={5 + nb + b: 2 + b for b in range(nb)},
        compiler_params=_cp("arbitrary", "arbitrary"))(i_of, j_of, qs, k, v, *mine, *bufs)
    return outs[0], outs[1], list(outs[2:])


def _mixer_fwd(zb, ya, p, l):
    t = zb.shape[0]
    tm = min(ROW_TILE, t)
    hb = tm // 8

    def body(zb_ref, zprev_ref, ya_ref, gln_ref, bln_ref, wsp_ref, bsp_ref, cw_ref, go_ref, mix_ref, yb_ref, yc_ref):
        i = pl.program_id(0)
        masks = _group_masks((tm, SGW))
        cmasks = _group_masks((CHUNK, SGW))
        tril = _tril_mask()
        wc_bf = [jnp.where(tril, wsp_ref[g], 0.0).astype(BF16) for g in range(4)]
        u, _, _, _, _, mixed = _sgu_forward(zb_ref[:, 0:256], zb_ref[:, 256:512], gln_ref[...], bln_ref[...], wc_bf,
                                            bsp_ref[...], masks, cmasks)
        yb = u * mixed
        _, _, _, conv = _conv_forward(zb_ref[:, 768:1024], zb_ref[:, 1024:1280], zprev_ref[:, 768:1024],
                                      zprev_ref[:, 1024:1280], i == 0, cw_ref[...])
        yc = zb_ref[:, 512:768] * conv
        yb_ref[...] = yb
        yc_ref[...] = yc
        go = go_ref[...]
        mix_ref[:, 0:512] = _rms(ya_ref[...], go[:, 0:512]).astype(BF16)
        mix_ref[:, 512:768] = _rms(yb, go[:, 512:768]).astype(BF16)
        mix_ref[:, 768:1024] = _rms(yc, go[:, 768:1024]).astype(BF16)

    return _pcall(
        body, "mixer_fwd", (t // tm,),
        [zb, zb, ya, p["sg_ln_g"], p["sg_ln_b"], p["w_sp"], p["b_sp"], p["conv_w"], p["out_norm_g"]],
        [_row(tm, ZB), pl.BlockSpec((8, ZB), lambda i: (jnp.maximum(i * hb - 1, 0), 0)), _row(tm, 512),
         _lyr(l, 1, SGW), _lyr(l, 1, SGW), _lyr(l, 4, CHUNK, CHUNK), _lyr(l, CHUNK, SGW), _lyr(_wl(p["conv_w"], l), 3, CVW), _lyr(l, 1, D)],
        [_row(tm, D), _row(tm, SGW), _row(tm, CVW)],
        [_sds((t, D), BF16), _sds((t, SGW), F32), _sds((t, CVW), F32)], ("parallel",))


def _out_proj(mix, x, p, l):
    t = x.shape[0]
    tm = min(ROW_TILE, t)

    def body(mix_ref, w_ref, x_ref, gp_ref, gf_ref, o_ref, x2_ref, h2_ref):
        o = jnp.dot(mix_ref[...], w_ref[...], preferred_element_type=F32)
        o_ref[...] = o
        x2 = x_ref[...] + _rms(o, gp_ref[...])
        x2_ref[...] = x2
        h2_ref[...] = _rms(x2, gf_ref[...]).astype(BF16)

    return _pcall(
        body, "out_proj", (t // tm,), [mix, p["w_out"], x, p["mix_post_g"], p["ffn_pre_g"]],
        [_row(tm, D), _lyr(_wl(p["w_out"], l), D, D), _row(tm, D), _lyr(l, 1, D), _lyr(l, 1, D)],
        [_row(tm, D), _row(tm, D), _row(tm, D)],
        [_sds((t, D), F32), _sds((t, D), F32), _sds((t, D), BF16)], ("parallel",))


def _gu_all(l, which):
    return pl.BlockSpec((4, None, None, HP, D), lambda *_: (0, l, which, 0, 0))


def _down_all(l):
    return pl.BlockSpec((4, None, HP, D), lambda *_: (0, l, 0, 0))


def _ffn_up(h2, p, l):
    t = h2.shape[0]
    tm = min(ROW_TILE, t)

    def body(h_ref, wg_ref, wu_ref, a_ref, b_ref, s_ref):
        h = h_ref[...]
        a = lax.dot_general(h, wg_ref[...], NT, preferred_element_type=F32)
        b = lax.dot_general(h, wu_ref[...], NT, preferred_element_type=F32)
        a_ref[...] = a.astype(BF16)
        b_ref[...] = b.astype(BF16)
        s_ref[...] = (a * (1.0 / (1.0 + jnp.exp(-a))) * b).astype(BF16)

    blk = pl.BlockSpec((tm, HP), lambda k, i: (i, k))
    wblk = lambda which: pl.BlockSpec((None, None, None, HP, D), lambda k, i: (k, l, which, 0, 0))
    return _pcall(
        body, "ffn_up", (4, t // tm), [h2, p["w_gu"], p["w_gu"]],
        [pl.BlockSpec((tm, D), lambda k, i: (i, 0)), wblk(0), wblk(1)], [blk, blk, blk],
        [_sds((t, DFFP), BF16)] * 3, ("parallel", "parallel"))


def _ffn_down(s, x2, p, l):
    t = x2.shape[0]
    tm = min(ROW_TILE, t)

    def body(s_ref, w_ref, x_ref, g_ref, f_ref, x3_ref):
        f = jnp.dot(s_ref[:, 0:HP], w_ref[0], preferred_element_type=F32)
        for k in range(1, 4):
            f = f + jnp.dot(s_ref[:, k * HP:(k + 1) * HP], w_ref[k], preferred_element_type=F32)
        f_ref[...] = f
        x3_ref[...] = x_ref[...] + _rms(f, g_ref[...])

    return _pcall(
        body, "ffn_down", (t // tm,), [s, p["w_down"], x2, p["ffn_post_g"]],
        [_row(tm, DFFP), _down_all(l), _row(tm, D), _lyr(l, 1, D)], [_row(tm, D), _row(tm, D)],
        [_sds((t, D), F32), _sds((t, D), F32)], ("parallel",))


def _loss_head(y, target):
    t = y.shape[0]
    tm = min(ROW_TILE, t)

    def body(y_ref, t_ref, dy_ref, acc_ref):
        e = y_ref[...] - t_ref[...]
        dy_ref[...] = e * (1.0 / D)
        sq = jnp.sum(e * e, axis=0, keepdims=True)
        part = sq[:, 0:128]
        for b in range(1, D // 128):
            part = part + sq[:, 128 * b:128 * b + 128]
        _acc_init(pl.program_id(0), acc_ref)
        acc_ref[...] += part

    return _pcall(body, "loss_head", (t // tm,), [y, target], [_row(tm, D), _row(tm, D)],
                  [_row(tm, D), pl.BlockSpec((1, 128), lambda i: (0, 0))],
                  [_sds((t, D), F32), _sds((1, 128), F32)], ("arbitrary",))


def _ffn_down_bwd(dx3, sv, p, l, depth, gb):
    t = dx3.shape[0]
    tm = min(256, t)

    def body(dx_ref, f_ref, g_ref, w_ref, a_ref, b_ref, df_ref, da_ref, db_ref, dg_ref):
        _acc_init(pl.program_id(0), dg_ref)
        df, dg = _rms_bwd(f_ref[...], g_ref[...], dx_ref[...])
        dg_ref[...] += dg
        df = df.astype(BF16)
        df_ref[...] = df
        for k in range(4):
            sl = slice(k * HP, (k + 1) * HP)
            ds = lax.dot_general(df, w_ref[k], NT, preferred_element_type=F32)
            av = a_ref[:, sl].astype(F32)
            sig = 1.0 / (1.0 + jnp.exp(-av))
            da_ref[:, sl] = (ds * b_ref[:, sl].astype(F32) * (sig * (1.0 + av * (1.0 - sig)))).astype(BF16)
            db_ref[:, sl] = (ds * (av * sig)).astype(BF16)

    df, da, db, gb["ffn_post_g"] = _pcall(
        body, "ffn_down_bwd", (t // tm,), [dx3, sv["f"], p["ffn_post_g"], p["w_down"], sv["a"], sv["b"]],
        [_row(tm, D), _row(tm, D), _lyr(l, 1, D), _down_all(l), _row(tm, DFFP), _row(tm, DFFP)],
        [_row(tm, D), _row(tm, DFFP), _row(tm, DFFP), _lyr(l.g, 1, D)],
        [_sds((t, D), BF16), _sds((t, DFFP), BF16), _sds((t, DFFP), BF16), _sds((depth, 1, D), F32)], ("arbitrary",),
        prevs={3: gb.get("ffn_post_g")})
    return df, da, db


def _ffn_up_bwd(da, db, dx3, sv, p, l, depth, gb):
    t = dx3.shape[0]
    tm = min(256, t)

    def body(da_ref, db_ref, wg_ref, wu_ref, x_ref, dx3_ref, g_ref, dx2_ref, dg_ref):
        _acc_init(pl.program_id(0), dg_ref)
        dh = jnp.zeros((tm, D), F32)
        for k in range(4):
            sl = slice(k * HP, (k + 1) * HP)
            dh = dh + jnp.dot(da_ref[:, sl], wg_ref[k], preferred_element_type=F32)
            dh = dh + jnp.dot(db_ref[:, sl], wu_ref[k], preferred_element_type=F32)
        dx, dg = _rms_bwd(x_ref[...], g_ref[...], dh)
        dg_ref[...] += dg
        dx2_ref[...] = dx3_ref[...] + dx

    dx2, gb["ffn_pre_g"] = _pcall(
        body, "ffn_up_bwd", (t // tm,), [da, db, p["w_gu"], p["w_gu"], sv["x2"], dx3, p["ffn_pre_g"]],
        [_row(tm, DFFP), _row(tm, DFFP), _gu_all(l, 0), _gu_all(l, 1), _row(tm, D), _row(tm, D), _lyr(l, 1, D)],
        [_row(tm, D), _lyr(l.g, 1, D)], [_sds((t, D), F32), _sds((depth, 1, D), F32)], ("arbitrary",),
        prevs={1: gb.get("ffn_pre_g")})
    return dx2


def _out_proj_bwd(dx2, sv, p, l, depth, gb):
    t = dx2.shape[0]
    tm = min(ROW_TILE, t)

    def body(dx_ref, o_ref, g_ref, w_ref, do_ref, dmix_ref, dg_ref):
        _acc_init(pl.program_id(0), dg_ref)
        do, dg = _rms_bwd(o_ref[...], g_ref[...], dx_ref[...])
        dg_ref[...] += dg
        do = do.astype(BF16)
        do_ref[...] = do
        dmix_ref[...] = lax.dot_general(do, w_ref[...], NT, preferred_element_type=F32)

    do, dmix, gb["mix_post_g"] = _pcall(
        body, "out_proj_bwd", (t // tm,), [dx2, sv["o"], p["mix_post_g"], p["w_out"]],
        [_row(tm, D), _row(tm, D), _lyr(l, 1, D), _lyr(_wl(p["w_out"], l), D, D)], [_row(tm, D), _row(tm, D), _lyr(l.g, 1, D)],
        [_sds((t, D), BF16), _sds((t, D), F32), _sds((depth, 1, D), F32)], ("arbitrary",),
        prevs={2: gb.get("mix_post_g")})
    return do, dmix


def _mixer_bwd(dmix, sv, p, l, depth, gb):
    zb = sv["zb"]
    t = zb.shape[0]
    tm = min(ROW_TILE, t)
    hb = tm // 8
    last_blk = t // 8 - 1
    nsteps = t // tm

    def body(dmix_ref, ya_ref, yb_ref, yc_ref, zb_ref, zprev_ref, znext_ref, ycn_ref, dmn_ref,
             gln_ref, bln_ref, wsp_ref, bsp_ref, cw_ref, go_ref,
             dya_ref, dzb_ref, delta_ref, dgo_ref, dgln_ref, dbln_ref, dwsp_ref, dbsp_ref, dcw_ref):
        i = pl.program_id(0)
        _acc_init(i, dgo_ref, dgln_ref, dbln_ref, dwsp_ref, dbsp_ref, dcw_ref)
        go = go_ref[...]
        dmix = dmix_ref[...]

        ya = ya_ref[...]
        dya, dga = _rms_bwd(ya, go[:, 0:512], dmix[:, 0:512])
        dyb, dgb_ = _rms_bwd(yb_ref[...], go[:, 512:768], dmix[:, 512:768])
        dyc, dgc_ = _rms_bwd(yc_ref[...], go[:, 768:1024], dmix[:, 768:1024])
        dgo_ref[:, 0:512] += dga
        dgo_ref[:, 512:768] += dgb_
        dgo_ref[:, 768:1024] += dgc_
        dya = dya * LN2
        dya_ref[...] = dya.astype(BF16)
        prod = dya * ya
        hmasks = _group_masks((tm, 512))
        for h in range(HEADS):
            delta_ref[h] = jnp.sum(jnp.where(hmasks[h], prod, 0.0), axis=-1, keepdims=True)

        masks = _group_masks((tm, SGW))
        cmasks = _group_masks((CHUNK, SGW))
        tril = _tril_mask()
        wc_bf = [jnp.where(tril, wsp_ref[g], 0.0).astype(BF16) for g in range(4)]
        zu, zv = zb_ref[:, 0:256], zb_ref[:, 256:512]
        g_ln = gln_ref[...]
        u, _, xh, rs, vn, mixed = _sgu_forward(zu, zv, g_ln, bln_ref[...], wc_bf, bsp_ref[...], masks, cmasks)
        du = dyb * mixed
        dmixed = dyb * u
        dvn_chunks = []
        dbsp = jnp.zeros((CHUNK, SGW), F32)
        for ci in range(tm // CHUNK):
            rows = slice(ci * CHUNK, (ci + 1) * CHUNK)
            dm_c = dmixed[rows, :]
            vn_c = vn[rows, :].astype(BF16)
            dbsp = dbsp + dm_c
            dvn_c = jnp.zeros((CHUNK, SGW), F32)
            for g in range(4):
                dm_g = jnp.where(cmasks[g], dm_c, 0.0).astype(BF16)
                dw = lax.dot_general(dm_g, vn_c, NT, preferred_element_type=F32)
                dwsp_ref[g] += jnp.where(tril, dw, 0.0)
                dvn_c = dvn_c + lax.dot_general(wc_bf[g], dm_g, TN, preferred_element_type=F32)
            dvn_chunks.append(dvn_c)
        dbsp_ref[...] += dbsp
        dvn = jnp.concatenate(dvn_chunks, axis=0) if len(dvn_chunks) > 1 else dvn_chunks[0]
        dgln_ref[...] += jnp.sum(dvn * xh, axis=0, keepdims=True)
        dbln_ref[...] += jnp.sum(dvn, axis=0, keepdims=True)
        dxh = dvn * g_ln
        dvv = rs * (dxh - _group_mean(dxh, masks) - xh * _group_mean(dxh * xh, masks))
        dzb_ref[:, 0:256] = (du * _gelu_grad(zu)).astype(BF16)
        dzb_ref[:, 256:512] = (dvv * _gelu_grad(zv)).astype(BF16)

        cwv = cw_ref[...]
        gb_, gc, hh = zb_ref[:, 512:768], zb_ref[:, 768:1024], zb_ref[:, 1024:1280]
        yv, sh1, sh2, conv = _conv_forward(gc, hh, zprev_ref[:, 768:1024], zprev_ref[:, 1024:1280], i == 0, cwv)
        dconv = dyc * gb_
        dzb_ref[:, 512:768] = (dyc * conv).astype(BF16)
        dcw_ref[0:1, :] += jnp.sum(dconv * sh2, axis=0, keepdims=True)
        dcw_ref[1:2, :] += jnp.sum(dconv * sh1, axis=0, keepdims=True)
        dcw_ref[2:3, :] += jnp.sum(dconv * yv, axis=0, keepdims=True)
        dycn, _ = _rms_bwd(ycn_ref[...], go[:, 768:1024], dmn_ref[...])
        dconv_next = jnp.where(i == nsteps - 1, 0.0, dycn * znext_ref[:, 512:768])
        n0, n1 = _pick_row(dconv_next, 0), _pick_row(dconv_next, 1)
        dyv = dconv * cwv[2:3, :] + _shift_up(dconv, 1, [n0]) * cwv[1:2, :] + _shift_up(dconv, 2, [n0, n1]) * cwv[0:1, :]
        dzb_ref[:, 768:1024] = (dyv * hh).astype(BF16)
        dzb_ref[:, 1024:1280] = (dyv * gc).astype(BF16)

    prev_map = lambda i: (jnp.maximum(i * hb - 1, 0), 0)
    next_map = lambda i: (jnp.minimum((i + 1) * hb, last_blk), 0)
    names = ("out_norm_g", "sg_ln_g", "sg_ln_b", "w_sp", "b_sp_t", "conv_w")
    shapes = ((1, D), (1, SGW), (1, SGW), (4, CHUNK, CHUNK), (CHUNK, SGW), (3, CVW))
    outs = _pcall(
        body, "mixer_bwd", (nsteps,),
        [dmix, sv["ya"], sv["yb"], sv["yc"], zb, zb, zb, sv["yc"], dmix, p["sg_ln_g"], p["sg_ln_b"], p["w_sp"], p["b_sp"],
         p["conv_w"], p["out_norm_g"]],
        [_row(tm, D), _row(tm, 512), _row(tm, SGW), _row(tm, CVW), _row(tm, ZB),
         pl.BlockSpec((8, ZB), prev_map), pl.BlockSpec((8, ZB), next_map), pl.BlockSpec((8, CVW), next_map),
         pl.BlockSpec((8, 256), lambda i: (jnp.minimum((i + 1) * hb, last_blk), 3)),
         _lyr(l, 1, SGW), _lyr(l, 1, SGW), _lyr(l, 4, CHUNK, CHUNK), _lyr(l, CHUNK, SGW), _lyr(_wl(p["conv_w"], l), 3, CVW), _lyr(l, 1, D)],
        [_row(tm, 512), _row(tm, ZB), pl.BlockSpec((HEADS, tm, 1), lambda i: (0, i, 0))] + [_lyr(l.g, *s) for s in shapes],
        [_sds((t, 512), BF16), _sds((t, ZB), BF16), _sds((HEADS, t, 1), F32)] + [_sds((depth,) + s, F32) for s in shapes],
        ("arbitrary",), prevs={3 + n: gb.get(name) for n, name in enumerate(names)})
    for n, name in enumerate(names):
        gb[name] = outs[3 + n]
    return outs[0], outs[1], outs[2]


def _attn_bwd(qs, k, v, dya, lse, delta, ride=()):
    t = qs.shape[0]
    tq = _att_tile(t)
    nq = t // tq
    nb = len(ride)
    groups = HEADS // ATT_HEADS

    steps = [(j, i) for j in range(nq) for i in range(j, nq)]
    j_of = jnp.asarray([s[0] for s in steps], jnp.int32)
    i_of = jnp.asarray([s[1] for s in steps], jnp.int32)

    def body(j_ref, i_ref, q_ref, k_ref, v_ref, do_ref, lse_ref, dl_ref, *refs):
        dq_ref, dk_ref, dv_ref = refs[nb:nb + 3]
        dq_s, dk_s, dv_s = refs[2 * nb + 3:2 * nb + 6]
        step_no = pl.program_id(1)
        j, i = j_ref[step_no], i_ref[step_no]
        if ride:
            start, finish = _exchange_ops(refs[:nb], refs[nb + 3:2 * nb + 3], refs[2 * nb + 6:])
            pr = pl.program_id(0)
            pl.when((pr == 0) & (step_no == 0))(start)
            pl.when((pr == groups - 1) & (step_no == len(steps) - 1))(finish)

        @pl.when(step_no == 0)
        def _():
            dq_s[...] = jnp.zeros(dq_s.shape, F32)

        def step(masked):
            keep = _causal_keep(tq, 0, 0) if masked else None
            lane = lax.broadcasted_iota(jnp.int32, (tq, 128), 1)
            rows = pl.ds(pl.multiple_of(i * tq, tq), tq)
            for hh in range(ATT_HEADS):
                sl = slice(128 * hh, 128 * hh + 128)
                pair = slice(128 * (hh // 2), 128 * (hh // 2) + 128)
                vv, do = v_ref[:, pair], do_ref[:, pair]
                qq, kk = q_ref[:, sl], k_ref[:, sl]
                s = lax.dot_general(qq, kk, NT, preferred_element_type=F32)
                p = jnp.exp2(s - lse_ref[hh])
                if masked:
                    p = jnp.where(keep, p, 0.0)
                do_h = jnp.where((lane < VD) if hh % 2 == 0 else (lane >= VD), do, jnp.zeros_like(do))
                dp = lax.dot_general(do_h, vv, NT, preferred_element_type=F32)
                ds = (p * (dp - dl_ref[hh])).astype(BF16)
                dv_s[:, pair] += lax.dot_general(p.astype(BF16), do_h, TN, preferred_element_type=F32)
                dk_s[:, sl] += lax.dot_general(ds, qq, TN, preferred_element_type=F32)
                dq_s[rows, sl] += jnp.dot(ds, kk, preferred_element_type=F32)

        @pl.when(i == j)
        def _():
            dk_s[...] = jnp.zeros(dk_s.shape, F32)
            dv_s[...] = jnp.zeros(dv_s.shape, F32)
            step(True)

        @pl.when(i > j)
        def _():
            step(False)

        @pl.when(i == nq - 1)
        def _():
            dk_ref[...] = dk_s[...].astype(BF16)
            dv_ref[...] = (dv_s[...] * LOG2E).astype(BF16)

        @pl.when(step_no == len(steps) - 1)
        def _():
            dq_ref[...] = dq_s[...].astype(BF16)

    qw, vw = 128 * ATT_HEADS, VD * ATT_HEADS
    qrow = lambda p, s, jt, it: (it[s], p)
    krow = lambda p, s, jt, it: (jt[s], p)
    col_spec = pl.BlockSpec((ATT_HEADS, tq, 1), lambda p, s, jt, it: (p, it[s], 0))
    grid_spec = pltpu.PrefetchScalarGridSpec(
        num_scalar_prefetch=2, grid=(groups, len(steps)),
        in_specs=[pl.BlockSpec((tq, qw), qrow), pl.BlockSpec((tq, qw), krow), pl.BlockSpec((tq, vw), krow),
                  pl.BlockSpec((tq, vw), qrow), col_spec, col_spec] + [ANY] * nb,
        out_specs=[pl.BlockSpec((t, qw), lambda p, s, jt, it: (0, p)), pl.BlockSpec((tq, qw), krow),
                   pl.BlockSpec((tq, vw), krow)] + [ANY] * nb,
        scratch_shapes=[pltpu.VMEM((t, qw), F32), pltpu.VMEM((tq, qw), F32), pltpu.VMEM((tq, vw), F32)]
        + ([pltpu.SemaphoreType.DMA((nb, 3))] * 2 if ride else []))
    outs = pl.pallas_call(
        body, name="attn_bwd_ride" if ride else "attn_bwd", grid_spec=grid_spec,
        out_shape=[_sds((t, QW), BF16), _sds((t, QW), BF16), _sds((t, HEADS * VD), BF16)]
        + [_sds((3,) + a.shape[1:], a.dtype) for a in ride],
        compiler_params=_cp("arbitrary", "arbitrary"))(j_of, i_of, qs, k, v, dya, lse, delta, *ride)
    return outs[0], outs[1], outs[2], list(outs[3:])


def _mla_prep_bwd(dqs, dk, dv, sv, p, l, depth, gb, tabs):
    za = sv["za"]
    t = za.shape[0]
    tm = min(ROW_TILE, t)

    def body(dq_ref, dk_ref, dv_ref, z_ref, gq_ref, gkv_ref, wuq_ref, wukv_ref, c_ref, s1_ref, s2_ref,
             dza_ref, dqp_ref, dkv_ref, dgq_ref, dgkv_ref):
        _acc_init(pl.program_id(0), dgq_ref, dgkv_ref)
        c, s1, s2 = c_ref[...], s1_ref[...], s2_ref[...]
        lane = lax.broadcasted_iota(jnp.int32, (tm, 128), 1)
        rope_lanes = (lane >= NOPE) & (lane < NOPE + ROPE)
        dkr = jnp.zeros((tm, 128), F32)
        for h in range(HEADS):
            sl = slice(128 * h, 128 * h + 128)
            dqp_ref[:, sl] = _rope_bwd(dq_ref[:, sl].astype(F32) * QSCALE, c, s1, s2).astype(BF16)
            dkh = dk_ref[:, sl]
            dkv_ref[:, sl] = dkh
            dkr = dkr + jnp.where(rope_lanes, dkh.astype(F32), 0.0)
        dkv_ref[:, QW:] = dv_ref[...]
        z = z_ref[...]
        dcq = lax.dot_general(dqp_ref[...], wuq_ref[...], NT, preferred_element_type=F32)
        dzq, dgq = _rms_bwd(z[:, :QR], gq_ref[...], dcq)
        dckv = lax.dot_general(dkv_ref[...], wukv_ref[...], NT, preferred_element_type=F32)
        dzkv, dgkv = _rms_bwd(z[:, QR:QR + KVR], gkv_ref[...], dckv)
        dgq_ref[...] += dgq
        dgkv_ref[...] += dgkv
        dza_ref[:, :QR] = dzq.astype(BF16)
        dza_ref[:, QR:QR + KVR] = dzkv.astype(BF16)
        dza_ref[:, QR + KVR:] = _rope_bwd(dkr, c, s1, s2).astype(BF16)

    dza, dqp, dkv, gb["q_norm_g"], gb["kv_norm_g"] = _pcall(
        body, "mla_prep_bwd", (t // tm,),
        [dqs, dk, dv, za, p["q_norm_g"], p["kv_norm_g"], p["w_uq"], p["w_ukv"], *tabs],
        [_row(tm, QW), _row(tm, QW), _row(tm, HEADS * VD), _row(tm, ZA), _lyr(l, 1, QR), _lyr(l, 1, KVR),
         _lyr(_wl(p["w_uq"], l), QR, QW), _lyr(_wl(p["w_ukv"], l), KVR, KVW), _row(tm, 128), _row(tm, 128), _row(tm, 128)],
        [_row(tm, ZA), _row(tm, QW), _row(tm, KVW), _lyr(l.g, 1, QR), _lyr(l.g, 1, KVR)],
        [_sds((t, ZA), BF16), _sds((t, QW), BF16), _sds((t, KVW), BF16), _sds((depth, 1, QR), F32),
         _sds((depth, 1, KVR), F32)], ("arbitrary",), prevs={3: gb.get("q_norm_g"), 4: gb.get("kv_norm_g")})
    return dza, dqp, dkv


def _in_proj_bwd(dza, dzb, dx2, sv, p, l, depth, gb):
    t = dx2.shape[0]
    tm = min(ROW_TILE, t)

    def body(dza_ref, dzb_ref, wa_ref, wb_ref, x_ref, dx2_ref, g_ref, dx_ref, dg_ref):
        _acc_init(pl.program_id(0), dg_ref)
        dh = (lax.dot_general(dza_ref[...], wa_ref[...], NT, preferred_element_type=F32)
              + lax.dot_general(dzb_ref[...], wb_ref[...], NT, preferred_element_type=F32))
        dx, dg = _rms_bwd(x_ref[...], g_ref[...], dh)
        dg_ref[...] += dg
        dx_ref[...] = dx2_ref[...] + dx

    dx, gb["mix_pre_g"] = _pcall(
        body, "in_proj_bwd", (t // tm,), [dza, dzb, p["w_in_a"], p["w_in_b"], sv["x"], dx2, p["mix_pre_g"]],
        [_row(tm, ZA), _row(tm, ZB), _lyr(_wl(p["w_in_a"], l), D, ZA), _lyr(_wl(p["w_in_b"], l), D, ZB), _row(tm, D), _row(tm, D), _lyr(l, 1, D)],
        [_row(tm, D), _lyr(l.g, 1, D)], [_sds((t, D), F32), _sds((depth, 1, D), F32)], ("arbitrary",),
        prevs={1: gb.get("mix_pre_g")})
    return dx


def _mm_tn(a, b, tn, name, l, depth, gb):
    t, k = a.shape
    n = b.shape[1]
    tt = min(ROW_TILE, t)

    def body(a_ref, b_ref, o_ref):
        _acc_init(pl.program_id(1), o_ref)
        o_ref[...] += lax.dot_general(a_ref[...], b_ref[...], TN, preferred_element_type=F32)

    gb[name] = _pcall(
        body, "d" + name, (n // tn, t // tt), [a, b],
        [pl.BlockSpec((tt, k), lambda j, s: (s, 0)), pl.BlockSpec((tt, tn), lambda j, s: (s, j))],
        pl.BlockSpec((None, k, tn), lambda j, s: (l.g, 0, j)), _sds((depth, k, n), F32), ("parallel", "arbitrary"),
        prevs={0: gb.get(name)})


def _dw_ffn(a, b, kind, l, depth, gb):
    t = a.shape[0]
    tt = min(ROW_TILE, t)
    nsteps = t // tt

    def body(a_ref, b_ref, o_ref, acc):
        s = pl.program_id(0)
        _acc_init(s, acc)
        acc[...] += lax.dot_general(a_ref[...], b_ref[...], TN, preferred_element_type=F32)

        @pl.when(s == nsteps - 1)
        def _():
            for k in range(4):
                o_ref[k] = acc[k * HP:(k + 1) * HP, :].astype(BF16)

    rows = lambda n: pl.BlockSpec((tt, n), lambda s: (s, 0))
    if kind == "down":
        name = "down"
        out_spec = pl.BlockSpec((4, None, HP, D), lambda s: (0, l.g, 0, 0))
        out_shape = _sds((4, depth, HP, D), BF16)
    else:
        which = 0 if kind == "gate" else 1
        name = "gu"
        out_spec = pl.BlockSpec((4, None, None, HP, D), lambda s: (0, l.g, which, 0, 0))
        out_shape = _sds((4, depth, 2, HP, D), BF16)
    gb[name] = _pcall(body, "dw_" + kind, (nsteps,), [a, b], [rows(DFFP), rows(D)], out_spec, out_shape, ("arbitrary",),
                      scratch=[pltpu.VMEM((DFFP, D), F32)], prevs={0: gb.get(name)})


def _ffn_views(bufs):
    return {"w_gu": bufs[1], "w_down": bufs[2].reshape(bufs[2].shape[:2] + (HP, D))}


def _layer_fwd(x, p, l, tabs, fetch):
    h1, za, zb = _in_proj(x, p, l)
    cqn, ckvn, qs, k, v = _mla_prep(za, p, l, tabs)
    ya, lse, bufs = _attn_fwd(qs, k, v, fetch)
    if fetch:
        p = {**p, **_ffn_views(bufs)}
    mix, yb, yc = _mixer_fwd(zb, ya, p, l)
    o, x2, h2 = _out_proj(mix, x, p, l)
    a, b, s = _ffn_up(h2, p, l)
    f, x3 = _ffn_down(s, x2, p, l)
    saved = dict(x=x, h1=h1, za=za, zb=zb, cqn=cqn, ckvn=ckvn, qs=qs, k=k, v=v, ya=ya, lse=lse, mix=mix, yb=yb, yc=yc,
                 o=o, x2=x2, h2=h2, a=a, b=b, s=s, f=f)
    return x3, saved, bufs if fetch else None


class _Layer(int):
    def __new__(cls, l, g):
        obj = int.__new__(cls, l)
        obj.g = g
        return obj


def _layer_bwd(dx3, p, sv, l, depth, gb, tabs, ride=()):
    df, da, db = _ffn_down_bwd(dx3, sv, p, l, depth, gb)
    _dw_ffn(sv["s"], df, "down", l, depth, gb)
    dx2 = _ffn_up_bwd(da, db, dx3, sv, p, l, depth, gb)
    _dw_ffn(da, sv["h2"], "gate", l, depth, gb)
    _dw_ffn(db, sv["h2"], "up", l, depth, gb)
    do, dmix = _out_proj_bwd(dx2, sv, p, l, depth, gb)
    _mm_tn(sv["mix"], do, D, "w_out", l, depth, gb)
    dya, dzb, delta = _mixer_bwd(dmix, sv, p, l, depth, gb)
    dqs, dk, dv, sent = _attn_bwd(sv["qs"], sv["k"], sv["v"], dya, sv["lse"], delta, ride)
    dza, dqp, dkv = _mla_prep_bwd(dqs, dk, dv, sv, p, l, depth, gb, tabs)
    _mm_tn(sv["cqn"], dqp, QW, "w_uq", l, depth, gb)
    _mm_tn(sv["ckvn"], dkv, KVW, "w_ukv", l, depth, gb)
    _mm_tn(sv["h1"], dza, ZA, "w_in_a", l, depth, gb)
    _mm_tn(sv["h1"], dzb, ZB, "w_in_b", l, depth, gb)
    return _in_proj_bwd(dza, dzb, dx2, sv, p, l, depth, gb), sent


def _rope_tables(positions):
    inv_freq = 1.0 / (ROPE_THETA ** (jnp.arange(0, ROPE // 2, dtype=F32) / (ROPE // 2)))
    ang = positions.astype(F32)[:, None] * inv_freq
    cos, sin = jnp.cos(ang), jnp.sin(ang)
    t = positions.shape[0]
    one, zero = jnp.ones((t, 64), F32), jnp.zeros((t, 16), F32)
    c = jnp.concatenate([one, cos, cos, one[:, :32]], axis=1)
    s1 = jnp.concatenate([zero, zero, zero, zero, -sin, zero, zero, zero], axis=1)
    s2 = jnp.concatenate([zero, zero, zero, zero, zero, sin, zero, zero], axis=1)
    return c, s1, s2


def _mixer_weight_params(full):
    w_in = full["w_in"]
    depth = w_in.shape[0]
    zpad = lambda n: jnp.zeros((depth, D, n), w_in.dtype)
    kv = full["w_ukv"].reshape(depth, KVR, HEADS, NOPE + VD)
    return {
        "w_in_a": jnp.concatenate([w_in[:, :, :640], zpad(64), w_in[:, :, 640:672], zpad(32)], axis=2),
        "w_in_b": w_in[:, :, 672:],
        "w_uq": jnp.pad(full["w_uq"].reshape(depth, QR, HEADS, NOPE + ROPE),
                        ((0, 0), (0, 0), (0, 0), (0, 32))).reshape(depth, QR, QW),
        "w_ukv": jnp.concatenate([jnp.pad(kv[..., :NOPE], ((0, 0), (0, 0), (0, 0), (0, 64))).reshape(depth, KVR, QW),
                                  kv[..., NOPE:].reshape(depth, KVR, HEADS * VD)], axis=2),
        "w_out": full["w_out"], "conv_w": full["conv_w"],
    }


def _small_params(w):
    p = {"w_sp": w["w_sp"], "b_sp": jnp.repeat(jnp.swapaxes(w["b_sp"], 1, 2), 64, axis=2)}
    for n in ("mix_pre_g", "mix_post_g", "ffn_pre_g", "ffn_post_g", "q_norm_g", "kv_norm_g", "sg_ln_g", "sg_ln_b",
              "out_norm_g"):
        p[n] = w[n][:, None, :]
    return p


def _natural_grads(gb):
    depth = gb["w_in_a"].shape[0]
    ga, kv = gb["w_in_a"], gb["w_ukv"]
    out = {
        "w_in": jnp.concatenate([ga[:, :, :640], ga[:, :, 704:736], gb["w_in_b"]], axis=2),
        "w_uq": gb["w_uq"].reshape(depth, QR, HEADS, 128)[..., :NOPE + ROPE].reshape(depth, QR, HEADS * (NOPE + ROPE)),
        "w_ukv": jnp.concatenate([kv[:, :, :QW].reshape(depth, KVR, HEADS, 128)[..., :NOPE],
                                  kv[:, :, QW:].reshape(depth, KVR, HEADS, VD)], axis=3).reshape(depth, KVR, -1),
        "b_sp": jnp.swapaxes(gb["b_sp_t"].reshape(depth, CHUNK, 4, 64).sum(axis=-1), 1, 2),
    }
    for n in ("w_out", "w_sp", "conv_w"):
        out[n] = gb[n]
    for n in ("mix_pre_g", "mix_post_g", "ffn_pre_g", "ffn_post_g", "q_norm_g", "kv_norm_g", "sg_ln_g", "sg_ln_b",
              "out_norm_g"):
        out[n] = gb[n][:, 0, :]
    return out


def _local_step(x, positions, target, small, mine, bufs, shard_shapes, fetch=True, front=None):
    depth = small["w_sp"].shape[0]
    tabs = _rope_tables(positions)
    ps = _small_params(small)
    saved, mixer_w = [], []
    for l in range(depth):
        mixer_w.append(_mixer_weight_params(_unpack_weights(bufs[0], l, shard_shapes)))
        p = {**ps, **mixer_w[l], **_ffn_views(bufs)}
        layers = [l + 1 if l + 1 < depth else None, l, l]
        x, sv, fetched = _layer_fwd(x, p, l, tabs, (mine, bufs, layers) if fetch else None)
        bufs = fetched or bufs
        saved.append(sv)
    dx, acc = _loss_head(x, target)
    loss = (0.5 / D) * jnp.sum(acc)
    nbatch = depth // 2
    gbs = [{} for _ in range(nbatch)]
    fronts, sent = [None] * nbatch, [None] * nbatch
    ride = ()
    for l in reversed(range(depth)):
        b = l // 2
        dx, got = _layer_bwd(dx, {**ps, **mixer_w[l], **_ffn_views(bufs)}, saved[l], _Layer(l, l % 2), 2, gbs[b], tabs, ride)
        if ride:
            sent[b + 1], ride = got, ()
        if l % 2 == 0:
            done = (_natural_grads(gbs[b]), gbs[b]["gu"], gbs[b]["down"])
            fronts[b] = front(b, *done) if front else done
            if front and b > 0:
                ride = tuple(fronts[b])
    return loss, dx, fronts, sent


def _place():
    x, y, c = lax.axis_index("x"), lax.axis_index("y"), lax.axis_index("c")
    chips = [(1 - x, y), (x, 1 - y), (1 - x, 1 - y)]
    return x, y, c, 2 * x + y, chips


def _remote(src, dst, send_sem, recv_sem, to):
    return pltpu.make_async_remote_copy(src_ref=src, dst_ref=dst, send_sem=send_sem, recv_sem=recv_sem, device_id=to,
                                        device_id_type=MESH_ID)


def _gather_ops(mine_refs, out_refs, sems, layers):
    send_sems, recv_sems, fsend_sems, frecv_sems = sems
    x, y, c, k, chips = _place()
    sib = (x, y, 1 - c)
    pairs = [(b, n) for n in range(3) for b in range(len(mine_refs)) if layers[b] is not None]

    def slot(n):
        return 2 * chips[n][0] + chips[n][1]

    def ici(b, n, dst_chip):
        return _remote(mine_refs[b].at[layers[b], c], out_refs[b].at[dst_chip, layers[b], c], send_sems.at[b, n],
                       recv_sems.at[b, n], (*chips[n], c))

    def d2d(b, n, half):
        piece = out_refs[b].at[slot(n), layers[b], half]
        return _remote(piece, piece, fsend_sems.at[b, n], frecv_sems.at[b, n], sib)

    def start():
        for b, n in pairs:
            ici(b, n, k).start()

    def hand_over():
        for b, n in pairs:
            ici(b, n, slot(n)).wait_recv()
            d2d(b, n, c).start()

    def drain():
        for b, n in pairs:
            d2d(b, n, 1 - c).wait_recv()
        for b, n in pairs:
            ici(b, n, k).wait_send()
            d2d(b, n, c).wait_send()

    return start, hand_over, drain


def _gather_first_layer(mine):
    nb = len(mine)

    def body(*refs):
        start, hand_over, drain = _gather_ops(refs[:nb], refs[nb:2 * nb], refs[2 * nb:], [0] + [None] * (nb - 1))
        start()
        hand_over()
        drain()

    return pl.pallas_call(
        body, name="gather_first_layer", in_specs=[ANY] * nb, out_specs=[ANY] * nb,
        out_shape=[_sds((4,) + a.shape, a.dtype) for a in mine],
        scratch_shapes=[pltpu.SemaphoreType.DMA((nb, 3))] * 4)(*mine)


def _swap_halves(bigs, wholes=()):
    nb, n = len(bigs), len(bigs) + len(wholes)

    def body(*refs):
        src, dst = refs[:n], refs[n:2 * n]
        send_sems, recv_sems = refs[2 * n:]
        x, y, c, _, _ = _place()
        sib = (x, y, 1 - c)
        cps = [_remote(src[b].at[:, 1 - c] if b < nb else src[b], dst[b], send_sems.at[b], recv_sems.at[b], sib)
               for b in range(n)]
        for cp in cps:
            cp.start()
        for cp in cps:
            cp.wait()

    return pl.pallas_call(
        body, name="swap_halves", in_specs=[ANY] * n, out_specs=[ANY] * n,
        out_shape=[_sds((4,) + a.shape[2:], a.dtype) for a in bigs] + [_sds(a.shape, a.dtype) for a in wholes],
        scratch_shapes=[pltpu.SemaphoreType.DMA((n,))] * 2)(*bigs, *wholes)


def _sum_tile(r):
    return max(cand for cand in range(16, 641, 16) if r % cand == 0)


def _pair_sum(big, rbig, c):
    _, _, r, w = big.shape
    tr = _sum_tile(r)

    def body(c_ref, big_ref, rbig_ref, p_ref):
        p_ref[...] = (big_ref[...].astype(F32) + rbig_ref[...].astype(F32)).astype(BF16)

    grid_spec = pltpu.PrefetchScalarGridSpec(
        num_scalar_prefetch=1, grid=(4, r // tr),
        in_specs=[pl.BlockSpec((None, None, tr, w), lambda j, i, cr: (j, cr[0], i, 0)),
                  pl.BlockSpec((None, tr, w), lambda j, i, cr: (j, i, 0))],
        out_specs=pl.BlockSpec((None, tr, w), lambda j, i, cr: (j, i, 0)))
    return pl.pallas_call(body, name="pair_sum", grid_spec=grid_spec, out_shape=_sds((4, r, w), BF16),
                          compiler_params=_cp("parallel", "parallel"))(c, big, rbig)


def _small_sum(parts):
    n, ns, _ = parts.shape

    def body(p_ref, o_ref):
        s = p_ref[0]
        for j in range(1, n):
            s = s + p_ref[j]
        o_ref[...] = s

    return pl.pallas_call(body, name="small_sum", out_shape=_sds((ns, 128), F32))(parts)


def _exchange_ops(p_refs, rb_refs, sems, small=None):
    send_sems, recv_sems = sems[0], sems[1]
    nb = len(p_refs)
    x, y, c, k, chips = _place()

    def copies(landing):
        out = []
        for n, (cx, cy) in enumerate(chips):
            to, kj = (cx, cy, c), 2 * cx + cy
            for b in range(nb):
                out.append(_remote(p_refs[b].at[k if landing else kj], rb_refs[b].at[n], send_sems.at[b, n],
                                   recv_sems.at[b, n], to))
            if small:
                out.append(_remote(small[0], small[1].at[kj if landing else k], send_sems.at[nb, n], recv_sems.at[nb, n], to))
        return out

    def local():
        return pltpu.make_async_copy(small[0], small[1].at[k], sems[2])

    def start():
        if small:
            local().start()
        for cp in copies(False):
            cp.start()

    def finish():
        for cp in copies(True):
            cp.wait_recv()
        for cp in copies(False):
            cp.wait_send()
        if small:
            local().wait()

    return start, finish


def _chip_exchange(ps, small):
    nb = len(ps)
    ns = small.shape[0]

    def body(*refs):
        start, finish = _exchange_ops(refs[:nb], refs[nb + 1:2 * nb + 1], refs[2 * nb + 2:], (refs[nb], refs[2 * nb + 1]))
        start()
        finish()

    return pl.pallas_call(
        body, name="chip_exchange", in_specs=[ANY] * (nb + 1), out_specs=[ANY] * (nb + 1),
        out_shape=[_sds((3,) + a.shape[1:], a.dtype) for a in ps] + [_sds((4, ns, 128), small.dtype)],
        scratch_shapes=[pltpu.SemaphoreType.DMA((nb + 1, 3))] * 2 + [pltpu.SemaphoreType.DMA(())])(*ps, small)


def _chip_sum(p, rb, chip):
    _, r, w = p.shape
    tr = _sum_tile(r)

    def body(k_ref, p_ref, rb_ref, o_ref):
        acc = p_ref[...].astype(F32)
        for j in range(3):
            acc = acc + rb_ref[j].astype(F32)
        o_ref[...] = acc

    grid_spec = pltpu.PrefetchScalarGridSpec(
        num_scalar_prefetch=1, grid=(r // tr,),
        in_specs=[pl.BlockSpec((None, tr, w), lambda i, kr: (kr[0], i, 0)), pl.BlockSpec((3, tr, w), lambda i, kr: (0, i, 0))],
        out_specs=pl.BlockSpec((tr, w), lambda i, kr: (i, 0)))
    return pl.pallas_call(body, name="chip_sum", grid_spec=grid_spec, out_shape=_sds((r, w), F32),
                          compiler_params=_cp("parallel"))(chip, p, rb)


def _send_to_sibling(reds):
    nb = len(reds)

    def body(*refs):
        red_refs, out_refs = refs[:nb], refs[nb:2 * nb]
        send_sems, recv_sems = refs[2 * nb:]
        x, y, c, _, _ = _place()
        cps = [_remote(red_refs[b], out_refs[b], send_sems.at[b], recv_sems.at[b], (x, y, 1 - c)) for b in range(nb)]
        for cp in cps:
            cp.start()
        for cp in cps:
            cp.wait()

    return pl.pallas_call(
        body, name="send_to_sibling", in_specs=[ANY] * nb, out_specs=[ANY] * nb,
        out_shape=[_sds(a.shape, a.dtype) for a in reds], scratch_shapes=[pltpu.SemaphoreType.DMA((nb,))] * 2)(*reds)


def _adam_math(w, g, m, v):
    nm = ADAM_B1 * m + (1.0 - ADAM_B1) * g
    nv = ADAM_B2 * v + (1.0 - ADAM_B2) * (g * g)
    m_hat = nm / (1.0 - ADAM_B1 ** ADAM_STEP)
    v_hat = nv / (1.0 - ADAM_B2 ** ADAM_STEP)
    return -ADAM_LR * (m_hat / (jnp.sqrt(v_hat) + ADAM_EPS) + ADAM_WD * w), nm, nv


def _adamw_shard(w, m, v, owns, others, c, name, pick=None):
    depth, r, n = w.shape
    nbatch = len(owns)
    tr = max(cand for cand in range(8, min(r, 256) + 1, 8) if r % cand == 0)
    npad = owns[0].shape[-1]

    def body(c_ref, w_ref, m_ref, v_ref, *refs):
        g_ref, d_ref, nm_ref, nv_ref = refs[2 * nbatch:]
        l = pl.program_id(0)
        mine = (l % 2) == c_ref[0]
        g = jnp.where(mine, refs[0][...], refs[nbatch][...])
        for b in range(1, nbatch):
            g = jnp.where(l // 2 == b, jnp.where(mine, refs[b][...], refs[nbatch + b][...]), g)
        g = g[:, :n]
        g_ref[...] = g
        d_ref[...], nm_ref[...], nv_ref[...] = _adam_math(w_ref[...], g, m_ref[...], v_ref[...])

    blk = pl.BlockSpec((None, tr, n), lambda l, i, cr: (l, i, 0))
    if pick is None:
        gblk = pl.BlockSpec((tr, npad), lambda l, i, cr: (i, 0))
    else:
        gblk = pl.BlockSpec((None, tr, npad), lambda l, i, cr: (pick, i, 0))
    grid_spec = pltpu.PrefetchScalarGridSpec(num_scalar_prefetch=1, grid=(depth, r // tr),
                                             in_specs=[blk] * 3 + [gblk] * (2 * nbatch), out_specs=[blk] * 4)
    return pl.pallas_call(body, name=name, grid_spec=grid_spec, out_shape=[_sds(w.shape, F32)] * 4,
                          compiler_params=_cp("parallel", "parallel"))(c, w, m, v, *owns, *others)


def _pad_ffn_shards(w_gate, w_up, w_down):
    depth = w_gate.shape[0]
    hr = HP // 2

    def gu_body(g_ref, u_ref, o_ref):
        for which, ref in enumerate((g_ref, u_ref)):
            o_ref[which, 0:HS, :] = ref[...].astype(BF16)
            o_ref[which, HS:HP, :] = jnp.zeros((HP - HS, D), BF16)

    blk = pl.BlockSpec((None, HS, D), lambda l: (l, 0, 0))
    gu = pl.pallas_call(
        gu_body, name="pad_gate_up", grid=(depth,), in_specs=[blk, blk],
        out_specs=pl.BlockSpec((None, 2, HP, D), lambda l: (l, 0, 0, 0)),
        out_shape=_sds((depth, 2, HP, D), BF16), compiler_params=_cp("parallel"))(w_gate, w_up)

    def down_body(w_ref, o_ref):
        o_ref[0] = w_ref[0:hr, :].astype(BF16)
        o_ref[1, 0:HS - hr, :] = w_ref[hr:HS, :].astype(BF16)
        o_ref[1, HS - hr:hr, :] = jnp.zeros((HP - HS, D), BF16)

    down = pl.pallas_call(
        down_body, name="pad_down", grid=(depth,), in_specs=[pl.BlockSpec((None, HS, D), lambda l: (l, 0, 0))],
        out_specs=pl.BlockSpec((None, 2, hr, D), lambda l: (l, 0, 0, 0)),
        out_shape=_sds((depth, 2, hr, D), BF16), compiler_params=_cp("parallel"))(w_down)
    return gu, down


def _adamw_small(w, g, m, v):
    r = w.shape[0]
    tr = max(cand for cand in range(8, 513, 8) if r % cand == 0)

    def body(w_ref, g_ref, m_ref, v_ref, d_ref, nm_ref, nv_ref):
        d_ref[...], nm_ref[...], nv_ref[...] = _adam_math(w_ref[...], g_ref[...], m_ref[...], v_ref[...])

    blk = pl.BlockSpec((tr, 128), lambda i: (i, 0))
    return pl.pallas_call(body, name="adamw_small", grid=(r // tr,), in_specs=[blk] * 4, out_specs=[blk] * 3,
                          out_shape=[_sds(w.shape, F32)] * 3, compiler_params=_cp("parallel"))(w, g, m, v)


def _to_pack(a, name):
    depth = a.shape[0]
    if name in ROW_SHARDED:
        return jnp.swapaxes(a.reshape(depth, 4, -1, D), 0, 1)
    return jnp.transpose(a.reshape(depth, a.shape[1], 4, a.shape[2] // 4), (2, 0, 1, 3)).reshape(4, depth, -1, D)


def _pack_rows(parts, lead, dtype, tail=None):
    pieces, at = [], 0
    for n, off, rows in PACK:
        if off > at:
            pieces.append(jnp.zeros(lead + (off - at, D), dtype))
        pieces.append(parts[n].astype(dtype))
        at = off + rows
    if tail is not None:
        pieces.append(tail)
        at += tail.shape[-2]
    pieces.append(jnp.zeros(lead + (PACK_ROWS - at, D), dtype))
    return jnp.concatenate(pieces, axis=len(lead))


def _pack_weight_shards(sh):
    depth = sh["w_in"].shape[0]
    parts = {n: sh[n].reshape(depth, rows, D) for n, _, rows in PACK}
    conv = lax.bitcast_convert_type(sh["conv_w"].reshape(depth, 3 * 64), BF16).reshape(depth, 1, 384)
    flat = _pack_rows(parts, (depth,), BF16, tail=jnp.pad(conv, ((0, 0), (0, 0), (0, D - 384))))
    return flat.reshape(depth, 2, PACK_ROWS // 2, D)


def _unpack_weights(gathered, l, shard_shapes):
    depth = 1
    flat = gathered[:, l].reshape(4, 1, PACK_ROWS, D)
    full = {}
    for n, off, rows in PACK:
        shp = shard_shapes[n][1:]
        piece = flat[:, :, off:off + rows, :].reshape((4, depth) + shp)
        if n in ROW_SHARDED:
            full[n] = jnp.transpose(piece, (1, 0, 2, 3)).reshape(depth, 4 * shp[0], shp[1])
        else:
            full[n] = jnp.transpose(piece, (1, 2, 0, 3)).reshape(depth, shp[0], 4 * shp[1])
    conv = lax.bitcast_convert_type(flat[:, :, CONV_ROW, :384].reshape(4, depth, 192, 2), F32)
    full["conv_w"] = jnp.transpose(conv.reshape(4, depth, 3, 64), (1, 2, 0, 3)).reshape(depth, 3, CVW)
    return full


def _pack_grad_shards(g):
    depth = g["w_in"].shape[0]
    return _pack_rows({n: _to_pack(g[n], n) for n, _, _ in PACK}, (4, depth), BF16)


def _pack_small(arrs, names_shapes, depth):
    flat = jnp.concatenate([arrs[n].reshape(depth, -1) for n, _ in names_shapes], axis=1).reshape(-1)
    rows = -(-flat.shape[0] // 1024) * 8
    return jnp.pad(flat, (0, rows * 128 - flat.shape[0])).reshape(rows, 128)


def _unpack_small(packed, names_shapes, depth):
    per_layer = sum(math.prod(s) for _, s in names_shapes)
    flat = packed.reshape(-1)[:depth * per_layer].reshape(depth, per_layer)
    out, off = {}, 0
    for n, s in names_shapes:
        size = math.prod(s)
        out[n] = flat[:, off:off + size].reshape((depth,) + s)
        off += size
    return out


def kernel(x, positions, mix_pre_g, mix_post_g, ffn_pre_g, ffn_post_g, w_in, q_norm_g, w_uq, kv_norm_g, w_ukv, sg_ln_g, sg_ln_b, w_sp, b_sp, conv_w, out_norm_g, w_out, w_gate, w_up, w_down, loss_target, m_mix_pre_g, m_mix_post_g, m_ffn_pre_g, m_ffn_post_g, m_w_in, m_q_norm_g, m_w_uq, m_kv_norm_g, m_w_ukv, m_sg_ln_g, m_sg_ln_b, m_w_sp, m_b_sp, m_conv_w, m_out_norm_g, m_w_out, m_w_gate, m_w_up, m_w_down, v_mix_pre_g, v_mix_post_g, v_ffn_pre_g, v_ffn_post_g, v_w_in, v_q_norm_g, v_w_uq, v_kv_norm_g, v_w_ukv, v_sg_ln_g, v_sg_ln_b, v_w_sp, v_b_sp, v_conv_w, v_out_norm_g, v_w_out, v_w_gate, v_w_up, v_w_down):
    w = dict(mix_pre_g=mix_pre_g, mix_post_g=mix_post_g, ffn_pre_g=ffn_pre_g, ffn_post_g=ffn_post_g, w_in=w_in,
             q_norm_g=q_norm_g, w_uq=w_uq, kv_norm_g=kv_norm_g, w_ukv=w_ukv, sg_ln_g=sg_ln_g, sg_ln_b=sg_ln_b, w_sp=w_sp,
             b_sp=b_sp, conv_w=conv_w, out_norm_g=out_norm_g, w_out=w_out, w_gate=w_gate, w_up=w_up, w_down=w_down)
    m = dict(mix_pre_g=m_mix_pre_g, mix_post_g=m_mix_post_g, ffn_pre_g=m_ffn_pre_g, ffn_post_g=m_ffn_post_g, w_in=m_w_in,
             q_norm_g=m_q_norm_g, w_uq=m_w_uq, kv_norm_g=m_kv_norm_g, w_ukv=m_w_ukv, sg_ln_g=m_sg_ln_g, sg_ln_b=m_sg_ln_b,
             w_sp=m_w_sp, b_sp=m_b_sp, conv_w=m_conv_w, out_norm_g=m_out_norm_g, w_out=m_w_out, w_gate=m_w_gate,
             w_up=m_w_up, w_down=m_w_down)
    v = dict(mix_pre_g=v_mix_pre_g, mix_post_g=v_mix_post_g, ffn_pre_g=v_ffn_pre_g, ffn_post_g=v_ffn_post_g, w_in=v_w_in,
             q_norm_g=v_q_norm_g, w_uq=v_w_uq, kv_norm_g=v_kv_norm_g, w_ukv=v_w_ukv, sg_ln_g=v_sg_ln_g, sg_ln_b=v_sg_ln_b,
             w_sp=v_w_sp, b_sp=v_b_sp, conv_w=v_conv_w, out_norm_g=v_out_norm_g, w_out=v_w_out, w_gate=v_w_gate,
             w_up=v_w_up, w_down=v_w_down)
    depth = w_in.shape[0]
    c = lax.axis_index("c").astype(jnp.int32).reshape(1)
    chip = (2 * lax.axis_index("x") + lax.axis_index("y")).astype(jnp.int32)

    nbatch = depth // 2

    mine = [_pack_weight_shards(w), *_pad_ffn_shards(jnp.swapaxes(w_gate, 1, 2), jnp.swapaxes(w_up, 1, 2), w_down)]
    bufs = [lax.dynamic_update_slice(g, a[None], (chip,) + (0,) * a.ndim)
            for g, a in zip(_gather_first_layer(mine), mine)]

    small_grads = [None] * nbatch
    small_pair = []

    def front(b, grads, g_gu, g_down):
        small_grads[b] = grads
        bigs = [_pack_grad_shards(grads), g_gu.reshape(4, 2, 2 * HP, D), g_down]
        if b > 0:
            rbigs = _swap_halves(bigs)
        else:
            small = _pack_small({n: jnp.concatenate([g[n] for g in small_grads]) for n, _ in SMALL}, SMALL, depth)
            *rbigs, rsmall = _swap_halves(bigs, [small])
            small_pair.append(_small_sum(jnp.stack([small, rsmall])))
        return [_pair_sum(a, r, c) for a, r in zip(bigs, rbigs)]

    loss, dx, ps, sent = _local_step(x[0], positions[0], loss_target[0], w, mine, bufs,
                                     {n: w[n].shape for n, _, _ in PACK}, front=front)
    loss = lax.psum(loss, ("x", "y", "c"))

    *sent[0], rs = _chip_exchange(ps[0], small_pair[0])
    own = [[_chip_sum(p, rb, chip.reshape(1)) for p, rb in zip(ps[b], sent[b])] for b in range(nbatch)]
    other = [_send_to_sibling(o) for o in own]
    g_small = _unpack_small(_small_sum(rs), SMALL, depth)
    g_small["conv_w"] = lax.dynamic_slice_in_dim(g_small["conv_w"], chip * 64, 64, axis=2)

    gw, delta, new_m, new_v = dict(g_small), {}, {}, {}

    def adam(n, pieces, pick=None, turned=False):
        view = (lambda a: jnp.swapaxes(a, 1, 2)) if turned else (lambda a: a)
        outs = _adamw_shard(view(w[n]), view(m[n]), view(v[n]), [pieces(o) for o in own], [pieces(o) for o in other], c,
                            "adamw_" + n, pick)
        gw[n], delta[n], new_m[n], new_v[n] = [view(o) for o in outs]

    for n, off, rows in PACK:
        adam(n, lambda o: o[0][off:off + rows, :].reshape(w[n].shape[1:]))
    adam("w_gate", lambda o: o[1].reshape(2, HP, D), 0, turned=True)
    adam("w_up", lambda o: o[1].reshape(2, HP, D), 1, turned=True)
    adam("w_down", lambda o: o[2])
    small_local = tuple((n, w[n].shape[1:]) for n, _ in SMALL)
    d_, m_, v_ = _adamw_small(_pack_small(w, small_local, depth), _pack_small(gw, small_local, depth),
                              _pack_small(m, small_local, depth), _pack_small(v, small_local, depth))
    delta.update(_unpack_small(d_, small_local, depth))
    new_m.update(_unpack_small(m_, small_local, depth))
    new_v.update(_unpack_small(v_, small_local, depth))

    return (loss, dx[None], *[gw[n] for n in WEIGHTS], *[delta[n] for n in WEIGHTS], *[new_m[n] for n in WEIGHTS],
            *[new_v[n] for n in WEIGHTS])
```

```python
import math

import jax
import jax.numpy as jnp
from jax import lax
from jax.experimental import pallas as pl
from jax.experimental.pallas import tpu as pltpu

F32 = jnp.float32
BF16 = jnp.bfloat16

D = 1024
HEADS = 8
NOPE = 64
ROPE = 32
VD = 64
QR = 384
KVR = 256
SGW = 256
CVW = 256
CHUNK = 128
DFF = 2816
EPS = 1e-6
ROPE_THETA = 10000.0
LOG2E = 1.4426950408889634
LN2 = 0.6931471805599453
QSCALE = (NOPE + ROPE) ** -0.5 * LOG2E
ZA = 768
ZB = 1280
QW = HEADS * 128
KVW = HEADS * 128 + HEADS * VD
NEG = -1e30
GC0 = 0.7978845608028654
GC1 = 0.044715

ADAM_LR = 0.001
ADAM_B1 = 0.9
ADAM_B2 = 0.999
ADAM_EPS = 1e-08
ADAM_WD = 0.01
ADAM_STEP = 10

V7X_VMEM_LIMIT = 52 * 1024 * 1024
ROW_TILE = 512
ATT_TILE = 512
ATT_HEADS = 4

NT = (((1,), (1,)), ((), ()))
TN = (((0,), (0,)), ((), ()))

HS = DFF // 4
HP = 768
DFFP = 4 * HP

PACK = (("w_in", 0, 488), ("w_out", 512, 256), ("w_ukv", 768, 64), ("w_uq", 832, 72))
CONV_ROW = 904
PACK_ROWS = 928
ROW_SHARDED = ("w_out",)
SMALL = (("mix_pre_g", (D,)), ("mix_post_g", (D,)), ("ffn_pre_g", (D,)), ("ffn_post_g", (D,)), ("q_norm_g", (QR,)),
         ("kv_norm_g", (KVR,)), ("sg_ln_g", (SGW,)), ("sg_ln_b", (SGW,)), ("w_sp", (4, CHUNK, CHUNK)), ("b_sp", (4, CHUNK)),
         ("conv_w", (3, CVW)), ("out_norm_g", (D,)))
WEIGHTS = ["mix_pre_g", "mix_post_g", "ffn_pre_g", "ffn_post_g", "w_in", "q_norm_g", "w_uq", "kv_norm_g", "w_ukv", "sg_ln_g",
           "sg_ln_b", "w_sp", "b_sp", "conv_w", "out_norm_g", "w_out", "w_gate", "w_up", "w_down"]

MESH_ID = pl.DeviceIdType.MESH
ANY = pl.BlockSpec(memory_space=pl.ANY)


def _cp(*sem):
    return pltpu.CompilerParams(dimension_semantics=sem, vmem_limit_bytes=V7X_VMEM_LIMIT)


def _sds(shape, dtype):
    return jax.ShapeDtypeStruct(shape, dtype)


def _row(tm, n):
    return pl.BlockSpec((tm, n), lambda i: (i, 0))


def _lyr(l, *shape):
    return pl.BlockSpec((None,) + shape, lambda *_: (l,) + (0,) * len(shape))


def _wl(a, l):
    return 0 if a.shape[0] == 1 else l


def _pcall(body, name, grid, ins, in_specs, out_specs, out_shape, sem, scratch=(), prevs=None):
    prevs = {k: v for k, v in (prevs or {}).items() if v is not None}
    order = sorted(prevs)
    n_in = len(ins)

    def wrapped(*refs):
        return body(*refs[:n_in], *refs[n_in + len(order):])

    return pl.pallas_call(
        wrapped, name=name, grid=grid, in_specs=list(in_specs) + [ANY] * len(order), out_specs=out_specs,
        out_shape=out_shape, scratch_shapes=list(scratch),
        input_output_aliases={n_in + i: k for i, k in enumerate(order)},
        compiler_params=_cp(*sem))(*ins, *[prevs[k] for k in order])


def _rms(x, g):
    r = lax.rsqrt(jnp.mean(x * x, axis=-1, keepdims=True) + EPS)
    return x * r * g


def _rms_bwd(x, g, dy):
    r = lax.rsqrt(jnp.mean(x * x, axis=-1, keepdims=True) + EPS)
    xh = x * r
    dg = jnp.sum(dy * xh, axis=0, keepdims=True)
    dxh = dy * g
    dx = r * (dxh - xh * jnp.mean(dxh * xh, axis=-1, keepdims=True))
    return dx, dg


def _gelu(x):
    return 0.5 * x * (1.0 + jnp.tanh(GC0 * (x + GC1 * x * x * x)))


def _gelu_grad(x):
    t = jnp.tanh(GC0 * (x + GC1 * x * x * x))
    return 0.5 * (1.0 + t) + 0.5 * x * (1.0 - t * t) * GC0 * (1.0 + 3.0 * GC1 * x * x)


def _rope(xb, c, s1, s2):
    return xb * c + pltpu.roll(xb, 112, 1) * s1 + pltpu.roll(xb, 16, 1) * s2


def _rope_bwd(dy, c, s1, s2):
    return dy * c + pltpu.roll(dy * s1, 16, 1) + pltpu.roll(dy * s2, 112, 1)


def _group_masks(shape):
    lane = lax.broadcasted_iota(jnp.int32, shape, 1)
    return [(lane >= 64 * g) & (lane < 64 * g + 64) for g in range(shape[1] // 64)]


def _group_mean(v, masks):
    out = jnp.zeros_like(v)
    for m in masks:
        s = jnp.sum(jnp.where(m, v, 0.0), axis=-1, keepdims=True) * (1.0 / 64.0)
        out = jnp.where(m, s, out)
    return out


def _pick_row(blk, idx):
    row = lax.broadcasted_iota(jnp.int32, blk.shape, 0)
    return jnp.sum(jnp.where(row == idx, blk, 0.0), axis=0, keepdims=True)


def _shift_down(y, k, first_rows):
    out = pltpu.roll(y, k, 0)
    row = lax.broadcasted_iota(jnp.int32, y.shape, 0)
    for idx in range(k):
        out = jnp.where(row == idx, first_rows[idx], out)
    return out


def _shift_up(y, k, last_rows):
    n = y.shape[0]
    out = pltpu.roll(y, n - k, 0)
    row = lax.broadcasted_iota(jnp.int32, y.shape, 0)
    for idx in range(k):
        out = jnp.where(row == n - k + idx, last_rows[idx], out)
    return out


def _tril_mask():
    r = lax.broadcasted_iota(jnp.int32, (CHUNK, CHUNK), 0)
    c = lax.broadcasted_iota(jnp.int32, (CHUNK, CHUNK), 1)
    return r >= c


def _sgu_forward(zu, zv, g_ln, b_ln, wc_bf, bsp, masks, cmasks):
    u = _gelu(zu)
    vv = _gelu(zv)
    mu = _group_mean(vv, masks)
    dv = vv - mu
    rs = lax.rsqrt(_group_mean(dv * dv, masks) + EPS)
    xh = dv * rs
    vn = xh * g_ln + b_ln
    chunks = []
    for ci in range(zu.shape[0] // CHUNK):
        vc = vn[ci * CHUNK:(ci + 1) * CHUNK, :]
        acc = bsp
        for g in range(4):
            acc = acc + jnp.dot(wc_bf[g], jnp.where(cmasks[g], vc, 0.0).astype(BF16), preferred_element_type=F32)
        chunks.append(acc)
    mixed = jnp.concatenate(chunks, axis=0) if len(chunks) > 1 else chunks[0]
    return u, vv, xh, rs, vn, mixed


def _conv_forward(gc, hh, prev_gc, prev_hh, first_tile, cw):
    yv = gc * hh
    prev = jnp.where(first_tile, 0.0, prev_gc * prev_hh)
    p6, p7 = _pick_row(prev, 6), _pick_row(prev, 7)
    sh1 = _shift_down(yv, 1, [p7])
    sh2 = _shift_down(yv, 2, [p6, p7])
    conv = sh2 * cw[0:1, :] + sh1 * cw[1:2, :] + yv * cw[2:3, :]
    return yv, sh1, sh2, conv


def _acc_init(step, *refs):
    @pl.when(step == 0)
    def _():
        for r in refs:
            r[...] = jnp.zeros(r.shape, r.dtype)


def _in_proj(x, p, l):
    t = x.shape[0]
    tm = min(ROW_TILE, t)

    def body(x_ref, g_ref, wa_ref, wb_ref, h_ref, za_ref, zb_ref):
        h = _rms(x_ref[...], g_ref[...]).astype(BF16)
        h_ref[...] = h
        za_ref[...] = jnp.dot(h, wa_ref[...], preferred_element_type=F32)
        zb_ref[...] = jnp.dot(h, wb_ref[...], preferred_element_type=F32)

    return _pcall(
        body, "in_proj", (t // tm,), [x, p["mix_pre_g"], p["w_in_a"], p["w_in_b"]],
        [_row(tm, D), _lyr(l, 1, D), _lyr(_wl(p["w_in_a"], l), D, ZA), _lyr(_wl(p["w_in_b"], l), D, ZB)],
        [_row(tm, D), _row(tm, ZA), _row(tm, ZB)],
        [_sds((t, D), BF16), _sds((t, ZA), F32), _sds((t, ZB), F32)], ("parallel",))


def _mla_prep(za, p, l, tabs):
    t = za.shape[0]
    tm = min(ROW_TILE, t)

    def body(z_ref, gq_ref, gkv_ref, wuq_ref, wukv_ref, c_ref, s1_ref, s2_ref, cq_ref, ckv_ref, q_ref, k_ref, v_ref):
        z = z_ref[...]
        cq = _rms(z[:, :QR], gq_ref[...]).astype(BF16)
        ckv = _rms(z[:, QR:QR + KVR], gkv_ref[...]).astype(BF16)
        cq_ref[...] = cq
        ckv_ref[...] = ckv
        c, s1, s2 = c_ref[...], s1_ref[...], s2_ref[...]
        kr = _rope(z[:, QR + KVR:], c, s1, s2)
        q = jnp.dot(cq, wuq_ref[...], preferred_element_type=F32)
        kv = jnp.dot(ckv, wukv_ref[...], preferred_element_type=F32)
        for h in range(HEADS):
            sl = slice(128 * h, 128 * h + 128)
            q_ref[:, sl] = (_rope(q[:, sl], c, s1, s2) * QSCALE).astype(BF16)
            k_ref[:, sl] = (kv[:, sl] + kr).astype(BF16)
        v_ref[...] = kv[:, QW:].astype(BF16)

    return _pcall(
        body, "mla_prep", (t // tm,), [za, p["q_norm_g"], p["kv_norm_g"], p["w_uq"], p["w_ukv"], *tabs],
        [_row(tm, ZA), _lyr(l, 1, QR), _lyr(l, 1, KVR), _lyr(_wl(p["w_uq"], l), QR, QW), _lyr(_wl(p["w_ukv"], l), KVR, KVW),
         _row(tm, 128), _row(tm, 128), _row(tm, 128)],
        [_row(tm, QR), _row(tm, KVR), _row(tm, QW), _row(tm, QW), _row(tm, HEADS * VD)],
        [_sds((t, QR), BF16), _sds((t, KVR), BF16), _sds((t, QW), BF16), _sds((t, QW), BF16),
         _sds((t, HEADS * VD), BF16)], ("parallel",))


def _att_tile(t):
    return min(ATT_TILE, max(t // 2, 128))


def _causal_keep(tq, i, j):
    row = lax.broadcasted_iota(jnp.int32, (tq, tq), 0) + i * tq
    col = lax.broadcasted_iota(jnp.int32, (tq, tq), 1) + j * tq
    return col <= row


def _attn_fwd(qs, k, v, fetch=None):
    t = qs.shape[0]
    tq = _att_tile(t)
    nq = t // tq
    rep = tq // 128
    groups = HEADS // ATT_HEADS
    mine, bufs, fetch_layer = fetch if fetch else ((), (), None)
    nb = len(mine)

    steps = [(i, j) for i in range(nq) for j in range(i + 1)]
    i_of = jnp.asarray([s[0] for s in steps], jnp.int32)
    j_of = jnp.asarray([s[1] for s in steps], jnp.int32)

    def body(i_ref, j_ref, q_ref, k_ref, v_ref, *refs):
        o_ref, lse_ref = refs[2 * nb:2 * nb + 2]
        m_s, l_s, acc_s = refs[3 * nb + 2:3 * nb + 5]
        step_no = pl.program_id(1)
        i, j = i_ref[step_no], j_ref[step_no]
        if fetch:
            start, hand_over, drain = _gather_ops(refs[:nb], refs[2 * nb + 2:3 * nb + 2], refs[3 * nb + 5:], fetch_layer)
            pr = pl.program_id(0)
            pl.when((pr == 0) & (step_no == 0))(start)
            pl.when((pr == groups - 1) & (step_no == len(steps) // 2))(hand_over)
            pl.when((pr == groups - 1) & (step_no == len(steps) - 1))(drain)

        @pl.when(j == 0)
        def _():
            m_s[...] = jnp.full(m_s.shape, NEG, F32)
            l_s[...] = jnp.zeros(l_s.shape, F32)
            acc_s[...] = jnp.zeros(acc_s.shape, F32)

        def step(masked):
            keep = _causal_keep(tq, i, j) if masked else None
            for hh in range(ATT_HEADS):
                sl = slice(128 * hh, 128 * hh + 128)
                vv = v_ref[:, 128 * (hh // 2):128 * (hh // 2) + 128]
                s = lax.dot_general(q_ref[:, sl], k_ref[:, sl], NT, preferred_element_type=F32)
                if masked:
                    s = jnp.where(keep, s, NEG)
                m_old = m_s[hh]
                m_new = jnp.maximum(m_old, jnp.max(s, axis=-1, keepdims=True))
                alpha = jnp.exp2(m_old - m_new)
                p = jnp.exp2(s - jnp.tile(m_new, (1, rep)))
                l_s[hh] = alpha * l_s[hh] + jnp.sum(p, axis=-1, keepdims=True)
                acc_s[hh] = alpha * acc_s[hh] + jnp.dot(p.astype(BF16), vv, preferred_element_type=F32)
                m_s[hh] = m_new

        @pl.when(j < i)
        def _():
            step(False)

        @pl.when(j == i)
        def _():
            step(True)
            lane = lax.broadcasted_iota(jnp.int32, (tq, 128), 1)
            for pp in range(ATT_HEADS // 2):
                a, b = 2 * pp, 2 * pp + 1
                o_ref[:, 128 * pp:128 * pp + 128] = jnp.where(lane < VD, acc_s[a] / l_s[a], acc_s[b] / l_s[b])
            for hh in range(ATT_HEADS):
                lse_ref[hh] = (m_s[hh] + jnp.log2(l_s[hh]))[:, 0:1]

    qw, vw = 128 * ATT_HEADS, VD * ATT_HEADS
    stat = pltpu.VMEM((ATT_HEADS, tq, 128), F32)
    grid_spec = pltpu.PrefetchScalarGridSpec(
        num_scalar_prefetch=2, grid=(groups, len(steps)),
        in_specs=[pl.BlockSpec((tq, qw), lambda p, s, it, jt: (it[s], p)),
                  pl.BlockSpec((tq, qw), lambda p, s, it, jt: (jt[s], p)),
                  pl.BlockSpec((tq, vw), lambda p, s, it, jt: (jt[s], p))] + [ANY] * (2 * nb),
        out_specs=[pl.BlockSpec((tq, vw), lambda p, s, it, jt: (it[s], p)),
                   pl.BlockSpec((ATT_HEADS, tq, 1), lambda p, s, it, jt: (p, it[s], 0))] + [ANY] * nb,
        scratch_shapes=[stat, stat, stat] + ([pltpu.SemaphoreType.DMA((nb, 3))] * 4 if fetch else []))
    outs = pl.pallas_call(
        body, name="attn_fwd_fetch" if fetch else "attn_fwd", grid_spec=grid_spec,
        out_shape=[_sds((t, HEADS * VD), F32), _sds((HEADS, t, 1), F32)] + [_sds(b.shape, b.dtype) for b in bufs],
        input_output_aliases={5 + nb + b: 2 + b for b in range(nb)},
        compiler_params=_cp("arbitrary", "arbitrary"))(i_of, j_of, qs, k, v, *mine, *bufs)
    return outs[0], outs[1], list(outs[2:])


def _mixer_fwd(zb, ya, p, l):
    t = zb.shape[0]
    tm = min(ROW_TILE, t)
    hb = tm // 8

    def body(zb_ref, zprev_ref, ya_ref, gln_ref, bln_ref, wsp_ref, bsp_ref, cw_ref, go_ref, mix_ref, yb_ref, yc_ref):
        i = pl.program_id(0)
        masks = _group_masks((tm, SGW))
        cmasks = _group_masks((CHUNK, SGW))
        tril = _tril_mask()
        wc_bf = [jnp.where(tril, wsp_ref[g], 0.0).astype(BF16) for g in range(4)]
        u, _, _, _, _, mixed = _sgu_forward(zb_ref[:, 0:256], zb_ref[:, 256:512], gln_ref[...], bln_ref[...], wc_bf,
                                            bsp_ref[...], masks, cmasks)
        yb = u * mixed
        _, _, _, conv = _conv_forward(zb_ref[:, 768:1024], zb_ref[:, 1024:1280], zprev_ref[:, 768:1024],
                                      zprev_ref[:, 1024:1280], i == 0, cw_ref[...])
        yc = zb_ref[:, 512:768] * conv
        yb_ref[...] = yb
        yc_ref[...] = yc
        go = go_ref[...]
        mix_ref[:, 0:512] = _rms(ya_ref[...], go[:, 0:512]).astype(BF16)
        mix_ref[:, 512:768] = _rms(yb, go[:, 512:768]).astype(BF16)
        mix_ref[:, 768:1024] = _rms(yc, go[:, 768:1024]).astype(BF16)

    return _pcall(
        body, "mixer_fwd", (t // tm,),
        [zb, zb, ya, p["sg_ln_g"], p["sg_ln_b"], p["w_sp"], p["b_sp"], p["conv_w"], p["out_norm_g"]],
        [_row(tm, ZB), pl.BlockSpec((8, ZB), lambda i: (jnp.maximum(i * hb - 1, 0), 0)), _row(tm, 512),
         _lyr(l, 1, SGW), _lyr(l, 1, SGW), _lyr(l, 4, CHUNK, CHUNK), _lyr(l, CHUNK, SGW), _lyr(_wl(p["conv_w"], l), 3, CVW), _lyr(l, 1, D)],
        [_row(tm, D), _row(tm, SGW), _row(tm, CVW)],
        [_sds((t, D), BF16), _sds((t, SGW), F32), _sds((t, CVW), F32)], ("parallel",))


def _out_proj(mix, x, p, l):
    t = x.shape[0]
    tm = min(ROW_TILE, t)

    def body(mix_ref, w_ref, x_ref, gp_ref, gf_ref, o_ref, x2_ref, h2_ref):
        o = jnp.dot(mix_ref[...], w_ref[...], preferred_element_type=F32)
        o_ref[...] = o
        x2 = x_ref[...] + _rms(o, gp_ref[...])
        x2_ref[...] = x2
        h2_ref[...] = _rms(x2, gf_ref[...]).astype(BF16)

    return _pcall(
        body, "out_proj", (t // tm,), [mix, p["w_out"], x, p["mix_post_g"], p["ffn_pre_g"]],
        [_row(tm, D), _lyr(_wl(p["w_out"], l), D, D), _row(tm, D), _lyr(l, 1, D), _lyr(l, 1, D)],
        [_row(tm, D), _row(tm, D), _row(tm, D)],
        [_sds((t, D), F32), _sds((t, D), F32), _sds((t, D), BF16)], ("parallel",))


def _gu_all(l, which):
    return pl.BlockSpec((4, None, None, HP, D), lambda *_: (0, l, which, 0, 0))


def _down_all(l):
    return pl.BlockSpec((4, None, HP, D), lambda *_: (0, l, 0, 0))


def _ffn_up(h2, p, l):
    t = h2.shape[0]
    tm = min(ROW_TILE, t)

    def body(h_ref, wg_ref, wu_ref, a_ref, b_ref, s_ref):
        h = h_ref[...]
        a = lax.dot_general(h, wg_ref[...], NT, preferred_element_type=F32)
        b = lax.dot_general(h, wu_ref[...], NT, preferred_element_type=F32)
        a_ref[...] = a.astype(BF16)
        b_ref[...] = b.astype(BF16)
        s_ref[...] = (a * (1.0 / (1.0 + jnp.exp(-a))) * b).astype(BF16)

    blk = pl.BlockSpec((tm, HP), lambda k, i: (i, k))
    wblk = lambda which: pl.BlockSpec((None, None, None, HP, D), lambda k, i: (k, l, which, 0, 0))
    return _pcall(
        body, "ffn_up", (4, t // tm), [h2, p["w_gu"], p["w_gu"]],
        [pl.BlockSpec((tm, D), lambda k, i: (i, 0)), wblk(0), wblk(1)], [blk, blk, blk],
        [_sds((t, DFFP), BF16)] * 3, ("parallel", "parallel"))


def _ffn_down(s, x2, p, l):
    t = x2.shape[0]
    tm = min(ROW_TILE, t)

    def body(s_ref, w_ref, x_ref, g_ref, f_ref, x3_ref):
        f = jnp.dot(s_ref[:, 0:HP], w_ref[0], preferred_element_type=F32)
        for k in range(1, 4):
            f = f + jnp.dot(s_ref[:, k * HP:(k + 1) * HP], w_ref[k], preferred_element_type=F32)
        f_ref[...] = f
        x3_ref[...] = x_ref[...] + _rms(f, g_ref[...])

    return _pcall(
        body, "ffn_down", (t // tm,), [s, p["w_down"], x2, p["ffn_post_g"]],
        [_row(tm, DFFP), _down_all(l), _row(tm, D), _lyr(l, 1, D)], [_row(tm, D), _row(tm, D)],
        [_sds((t, D), F32), _sds((t, D), F32)], ("parallel",))


def _loss_head(y, target):
    t = y.shape[0]
    tm = min(ROW_TILE, t)

    def body(y_ref, t_ref, dy_ref, acc_ref):
        e = y_ref[...] - t_ref[...]
        dy_ref[...] = e * (1.0 / D)
        sq = jnp.sum(e * e, axis=0, keepdims=True)
        part = sq[:, 0:128]
        for b in range(1, D // 128):
            part = part + sq[:, 128 * b:128 * b + 128]
        _acc_init(pl.program_id(0), acc_ref)
        acc_ref[...] += part

    return _pcall(body, "loss_head", (t // tm,), [y, target], [_row(tm, D), _row(tm, D)],
                  [_row(tm, D), pl.BlockSpec((1, 128), lambda i: (0, 0))],
                  [_sds((t, D), F32), _sds((1, 128), F32)], ("arbitrary",))


def _ffn_down_bwd(dx3, sv, p, l, depth, gb):
    t = dx3.shape[0]
    tm = min(256, t)

    def body(dx_ref, f_ref, g_ref, w_ref, a_ref, b_ref, df_ref, da_ref, db_ref, dg_ref):
        _acc_init(pl.program_id(0), dg_ref)
        df, dg = _rms_bwd(f_ref[...], g_ref[...], dx_ref[...])
        dg_ref[...] += dg
        df = df.astype(BF16)
        df_ref[...] = df
        for k in range(4):
            sl = slice(k * HP, (k + 1) * HP)
            ds = lax.dot_general(df, w_ref[k], NT, preferred_element_type=F32)
            av = a_ref[:, sl].astype(F32)
            sig = 1.0 / (1.0 + jnp.exp(-av))
            da_ref[:, sl] = (ds * b_ref[:, sl].astype(F32) * (sig * (1.0 + av * (1.0 - sig)))).astype(BF16)
            db_ref[:, sl] = (ds * (av * sig)).astype(BF16)

    df, da, db, gb["ffn_post_g"] = _pcall(
        body, "ffn_down_bwd", (t // tm,), [dx3, sv["f"], p["ffn_post_g"], p["w_down"], sv["a"], sv["b"]],
        [_row(tm, D), _row(tm, D), _lyr(l, 1, D), _down_all(l), _row(tm, DFFP), _row(tm, DFFP)],
        [_row(tm, D), _row(tm, DFFP), _row(tm, DFFP), _lyr(l.g, 1, D)],
        [_sds((t, D), BF16), _sds((t, DFFP), BF16), _sds((t, DFFP), BF16), _sds((depth, 1, D), F32)], ("arbitrary",),
        prevs={3: gb.get("ffn_post_g")})
    return df, da, db


def _ffn_up_bwd(da, db, dx3, sv, p, l, depth, gb):
    t = dx3.shape[0]
    tm = min(256, t)

    def body(da_ref, db_ref, wg_ref, wu_ref, x_ref, dx3_ref, g_ref, dx2_ref, dg_ref):
        _acc_init(pl.program_id(0), dg_ref)
        dh = jnp.zeros((tm, D), F32)
        for k in range(4):
            sl = slice(k * HP, (k + 1) * HP)
            dh = dh + jnp.dot(da_ref[:, sl], wg_ref[k], preferred_element_type=F32)
            dh = dh + jnp.dot(db_ref[:, sl], wu_ref[k], preferred_element_type=F32)
        dx, dg = _rms_bwd(x_ref[...], g_ref[...], dh)
        dg_ref[...] += dg
        dx2_ref[...] = dx3_ref[...] + dx

    dx2, gb["ffn_pre_g"] = _pcall(
        body, "ffn_up_bwd", (t // tm,), [da, db, p["w_gu"], p["w_gu"], sv["x2"], dx3, p["ffn_pre_g"]],
        [_row(tm, DFFP), _row(tm, DFFP), _gu_all(l, 0), _gu_all(l, 1), _row(tm, D), _row(tm, D), _lyr(l, 1, D)],
        [_row(tm, D), _lyr(l.g, 1, D)], [_sds((t, D), F32), _sds((depth, 1, D), F32)], ("arbitrary",),
        prevs={1: gb.get("ffn_pre_g")})
    return dx2


def _out_proj_bwd(dx2, sv, p, l, depth, gb):
    t = dx2.shape[0]
    tm = min(ROW_TILE, t)

    def body(dx_ref, o_ref, g_ref, w_ref, do_ref, dmix_ref, dg_ref):
        _acc_init(pl.program_id(0), dg_ref)
        do, dg = _rms_bwd(o_ref[...], g_ref[...], dx_ref[...])
        dg_ref[...] += dg
        do = do.astype(BF16)
        do_ref[...] = do
        dmix_ref[...] = lax.dot_general(do, w_ref[...], NT, preferred_element_type=F32)

    do, dmix, gb["mix_post_g"] = _pcall(
        body, "out_proj_bwd", (t // tm,), [dx2, sv["o"], p["mix_post_g"], p["w_out"]],
        [_row(tm, D), _row(tm, D), _lyr(l, 1, D), _lyr(_wl(p["w_out"], l), D, D)], [_row(tm, D), _row(tm, D), _lyr(l.g, 1, D)],
        [_sds((t, D), BF16), _sds((t, D), F32), _sds((depth, 1, D), F32)], ("arbitrary",),
        prevs={2: gb.get("mix_post_g")})
    return do, dmix


def _mixer_bwd(dmix, sv, p, l, depth, gb):
    zb = sv["zb"]
    t = zb.shape[0]
    tm = min(ROW_TILE, t)
    hb = tm // 8
    last_blk = t // 8 - 1
    nsteps = t // tm

    def body(dmix_ref, ya_ref, yb_ref, yc_ref, zb_ref, zprev_ref, znext_ref, ycn_ref, dmn_ref,
             gln_ref, bln_ref, wsp_ref, bsp_ref, cw_ref, go_ref,
             dya_ref, dzb_ref, delta_ref, dgo_ref, dgln_ref, dbln_ref, dwsp_ref, dbsp_ref, dcw_ref):
        i = pl.program_id(0)
        _acc_init(i, dgo_ref, dgln_ref, dbln_ref, dwsp_ref, dbsp_ref, dcw_ref)
        go = go_ref[...]
        dmix = dmix_ref[...]

        ya = ya_ref[...]
        dya, dga = _rms_bwd(ya, go[:, 0:512], dmix[:, 0:512])
        dyb, dgb_ = _rms_bwd(yb_ref[...], go[:, 512:768], dmix[:, 512:768])
        dyc, dgc_ = _rms_bwd(yc_ref[...], go[:, 768:1024], dmix[:, 768:1024])
        dgo_ref[:, 0:512] += dga
        dgo_ref[:, 512:768] += dgb_
        dgo_ref[:, 768:1024] += dgc_
        dya = dya * LN2
        dya_ref[...] = dya.astype(BF16)
        prod = dya * ya
        hmasks = _group_masks((tm, 512))
        for h in range(HEADS):
            delta_ref[h] = jnp.sum(jnp.where(hmasks[h], prod, 0.0), axis=-1, keepdims=True)

        masks = _group_masks((tm, SGW))
        cmasks = _group_masks((CHUNK, SGW))
        tril = _tril_mask()
        wc_bf = [jnp.where(tril, wsp_ref[g], 0.0).astype(BF16) for g in range(4)]
        zu, zv = zb_ref[:, 0:256], zb_ref[:, 256:512]
        g_ln = gln_ref[...]
        u, _, xh, rs, vn, mixed = _sgu_forward(zu, zv, g_ln, bln_ref[...], wc_bf, bsp_ref[...], masks, cmasks)
        du = dyb * mixed
        dmixed = dyb * u
        dvn_chunks = []
        dbsp = jnp.zeros((CHUNK, SGW), F32)
        for ci in range(tm // CHUNK):
            rows = slice(ci * CHUNK, (ci + 1) * CHUNK)
            dm_c = dmixed[rows, :]
            vn_c = vn[rows, :].astype(BF16)
            dbsp = dbsp + dm_c
            dvn_c = jnp.zeros((CHUNK, SGW), F32)
            for g in range(4):
                dm_g = jnp.where(cmasks[g], dm_c, 0.0).astype(BF16)
                dw = lax.dot_general(dm_g, vn_c, NT, preferred_element_type=F32)
                dwsp_ref[g] += jnp.where(tril, dw, 0.0)
                dvn_c = dvn_c + lax.dot_general(wc_bf[g], dm_g, TN, preferred_element_type=F32)
            dvn_chunks.append(dvn_c)
        dbsp_ref[...] += dbsp
        dvn = jnp.concatenate(dvn_chunks, axis=0) if len(dvn_chunks) > 1 else dvn_chunks[0]
        dgln_ref[...] += jnp.sum(dvn * xh, axis=0, keepdims=True)
        dbln_ref[...] += jnp.sum(dvn, axis=0, keepdims=True)
        dxh = dvn * g_ln
        dvv = rs * (dxh - _group_mean(dxh, masks) - xh * _group_mean(dxh * xh, masks))
        dzb_ref[:, 0:256] = (du * _gelu_grad(zu)).astype(BF16)
        dzb_ref[:, 256:512] = (dvv * _gelu_grad(zv)).astype(BF16)

        cwv = cw_ref[...]
        gb_, gc, hh = zb_ref[:, 512:768], zb_ref[:, 768:1024], zb_ref[:, 1024:1280]
        yv, sh1, sh2, conv = _conv_forward(gc, hh, zprev_ref[:, 768:1024], zprev_ref[:, 1024:1280], i == 0, cwv)
        dconv = dyc * gb_
        dzb_ref[:, 512:768] = (dyc * conv).astype(BF16)
        dcw_ref[0:1, :] += jnp.sum(dconv * sh2, axis=0, keepdims=True)
        dcw_ref[1:2, :] += jnp.sum(dconv * sh1, axis=0, keepdims=True)
        dcw_ref[2:3, :] += jnp.sum(dconv * yv, axis=0, keepdims=True)
        dycn, _ = _rms_bwd(ycn_ref[...], go[:, 768:1024], dmn_ref[...])
        dconv_next = jnp.where(i == nsteps - 1, 0.0, dycn * znext_ref[:, 512:768])
        n0, n1 = _pick_row(dconv_next, 0), _pick_row(dconv_next, 1)
        dyv = dconv * cwv[2:3, :] + _shift_up(dconv, 1, [n0]) * cwv[1:2, :] + _shift_up(dconv, 2, [n0, n1]) * cwv[0:1, :]
        dzb_ref[:, 768:1024] = (dyv * hh).astype(BF16)
        dzb_ref[:, 1024:1280] = (dyv * gc).astype(BF16)

    prev_map = lambda i: (jnp.maximum(i * hb - 1, 0), 0)
    next_map = lambda i: (jnp.minimum((i + 1) * hb, last_blk), 0)
    names = ("out_norm_g", "sg_ln_g", "sg_ln_b", "w_sp", "b_sp_t", "conv_w")
    shapes = ((1, D), (1, SGW), (1, SGW), (4, CHUNK, CHUNK), (CHUNK, SGW), (3, CVW))
    outs = _pcall(
        body, "mixer_bwd", (nsteps,),
        [dmix, sv["ya"], sv["yb"], sv["yc"], zb, zb, zb, sv["yc"], dmix, p["sg_ln_g"], p["sg_ln_b"], p["w_sp"], p["b_sp"],
         p["conv_w"], p["out_norm_g"]],
        [_row(tm, D), _row(tm, 512), _row(tm, SGW), _row(tm, CVW), _row(tm, ZB),
         pl.BlockSpec((8, ZB), prev_map), pl.BlockSpec((8, ZB), next_map), pl.BlockSpec((8, CVW), next_map),
         pl.BlockSpec((8, 256), lambda i: (jnp.minimum((i + 1) * hb, last_blk), 3)),
         _lyr(l, 1, SGW), _lyr(l, 1, SGW), _lyr(l, 4, CHUNK, CHUNK), _lyr(l, CHUNK, SGW), _lyr(_wl(p["conv_w"], l), 3, CVW), _lyr(l, 1, D)],
        [_row(tm, 512), _row(tm, ZB), pl.BlockSpec((HEADS, tm, 1), lambda i: (0, i, 0))] + [_lyr(l.g, *s) for s in shapes],
        [_sds((t, 512), BF16), _sds((t, ZB), BF16), _sds((HEADS, t, 1), F32)] + [_sds((depth,) + s, F32) for s in shapes],
        ("arbitrary",), prevs={3 + n: gb.get(name) for n, name in enumerate(names)})
    for n, name in enumerate(names):
        gb[name] = outs[3 + n]
    return outs[0], outs[1], outs[2]


def _attn_bwd(qs, k, v, dya, lse, delta, ride=()):
    t = qs.shape[0]
    tq = _att_tile(t)
    nq = t // tq
    nb = len(ride)
    groups = HEADS // ATT_HEADS

    steps = [(j, i) for j in range(nq) for i in range(j, nq)]
    j_of = jnp.asarray([s[0] for s in steps], jnp.int32)
    i_of = jnp.asarray([s[1] for s in steps], jnp.int32)

    def body(j_ref, i_ref, q_ref, k_ref, v_ref, do_ref, lse_ref, dl_ref, *refs):
        dq_ref, dk_ref, dv_ref = refs[nb:nb + 3]
        dq_s, dk_s, dv_s = refs[2 * nb + 3:2 * nb + 6]
        step_no = pl.program_id(1)
        j, i = j_ref[step_no], i_ref[step_no]
        if ride:
            start, finish = _exchange_ops(refs[:nb], refs[nb + 3:2 * nb + 3], refs[2 * nb + 6:])
            pr = pl.program_id(0)
            pl.when((pr == 0) & (step_no == 0))(start)
            pl.when((pr == groups - 1) & (step_no == len(steps) - 1))(finish)

        @pl.when(step_no == 0)
        def _():
            dq_s[...] = jnp.zeros(dq_s.shape, F32)

        def step(masked):
            keep = _causal_keep(tq, 0, 0) if masked else None
            lane = lax.broadcasted_iota(jnp.int32, (tq, 128), 1)
            rows = pl.ds(pl.multiple_of(i * tq, tq), tq)
            for hh in range(ATT_HEADS):
                sl = slice(128 * hh, 128 * hh + 128)
                pair = slice(128 * (hh // 2), 128 * (hh // 2) + 128)
                vv, do = v_ref[:, pair], do_ref[:, pair]
                qq, kk = q_ref[:, sl], k_ref[:, sl]
                s = lax.dot_general(qq, kk, NT, preferred_element_type=F32)
                p = jnp.exp2(s - lse_ref[hh])
                if masked:
                    p = jnp.where(keep, p, 0.0)
                do_h = jnp.where((lane < VD) if hh % 2 == 0 else (lane >= VD), do, jnp.zeros_like(do))
                dp = lax.dot_general(do_h, vv, NT, preferred_element_type=F32)
                ds = (p * (dp - dl_ref[hh])).astype(BF16)
                dv_s[:, pair] += lax.dot_general(p.astype(BF16), do_h, TN, preferred_element_type=F32)
                dk_s[:, sl] += lax.dot_general(ds, qq, TN, preferred_element_type=F32)
                dq_s[rows, sl] += jnp.dot(ds, kk, preferred_element_type=F32)

        @pl.when(i == j)
        def _():
            dk_s[...] = jnp.zeros(dk_s.shape, F32)
            dv_s[...] = jnp.zeros(dv_s.shape, F32)
            step(True)

        @pl.when(i > j)
        def _():
            step(False)

        @pl.when(i == nq - 1)
        def _():
            dk_ref[...] = dk_s[...].astype(BF16)
            dv_ref[...] = (dv_s[...] * LOG2E).astype(BF16)

        @pl.when(step_no == len(steps) - 1)
        def _():
            dq_ref[...] = dq_s[...].astype(BF16)

    qw, vw = 128 * ATT_HEADS, VD * ATT_HEADS
    qrow = lambda p, s, jt, it: (it[s], p)
    krow = lambda p, s, jt, it: (jt[s], p)
    col_spec = pl.BlockSpec((ATT_HEADS, tq, 1), lambda p, s, jt, it: (p, it[s], 0))
    grid_spec = pltpu.PrefetchScalarGridSpec(
        num_scalar_prefetch=2, grid=(groups, len(steps)),
        in_specs=[pl.BlockSpec((tq, qw), qrow), pl.BlockSpec((tq, qw), krow), pl.BlockSpec((tq, vw), krow),
                  pl.BlockSpec((tq, vw), qrow), col_spec, col_spec] + [ANY] * nb,
        out_specs=[pl.BlockSpec((t, qw), lambda p, s, jt, it: (0, p)), pl.BlockSpec((tq, qw), krow),
                   pl.BlockSpec((tq, vw), krow)] + [ANY] * nb,
        scratch_shapes=[pltpu.VMEM((t, qw), F32), pltpu.VMEM((tq, qw), F32), pltpu.VMEM((tq, vw), F32)]
        + ([pltpu.SemaphoreType.DMA((nb, 3))] * 2 if ride else []))
    outs = pl.pallas_call(
        body, name="attn_bwd_ride" if ride else "attn_bwd", grid_spec=grid_spec,
        out_shape=[_sds((t, QW), BF16), _sds((t, QW), BF16), _sds((t, HEADS * VD), BF16)]
        + [_sds((3,) + a.shape[1:], a.dtype) for a in ride],
        compiler_params=_cp("arbitrary", "arbitrary"))(j_of, i_of, qs, k, v, dya, lse, delta, *ride)
    return outs[0], outs[1], outs[2], list(outs[3:])


def _mla_prep_bwd(dqs, dk, dv, sv, p, l, depth, gb, tabs):
    za = sv["za"]
    t = za.shape[0]
    tm = min(ROW_TILE, t)

    def body(dq_ref, dk_ref, dv_ref, z_ref, gq_ref, gkv_ref, wuq_ref, wukv_ref, c_ref, s1_ref, s2_ref,
             dza_ref, dqp_ref, dkv_ref, dgq_ref, dgkv_ref):
        _acc_init(pl.program_id(0), dgq_ref, dgkv_ref)
        c, s1, s2 = c_ref[...], s1_ref[...], s2_ref[...]
        lane = lax.broadcasted_iota(jnp.int32, (tm, 128), 1)
        rope_lanes = (lane >= NOPE) & (lane < NOPE + ROPE)
        dkr = jnp.zeros((tm, 128), F32)
        for h in range(HEADS):
            sl = slice(128 * h, 128 * h + 128)
            dqp_ref[:, sl] = _rope_bwd(dq_ref[:, sl].astype(F32) * QSCALE, c, s1, s2).astype(BF16)
            dkh = dk_ref[:, sl]
            dkv_ref[:, sl] = dkh
            dkr = dkr + jnp.where(rope_lanes, dkh.astype(F32), 0.0)
        dkv_ref[:, QW:] = dv_ref[...]
        z = z_ref[...]
        dcq = lax.dot_general(dqp_ref[...], wuq_ref[...], NT, preferred_element_type=F32)
        dzq, dgq = _rms_bwd(z[:, :QR], gq_ref[...], dcq)
        dckv = lax.dot_general(dkv_ref[...], wukv_ref[...], NT, preferred_element_type=F32)
        dzkv, dgkv = _rms_bwd(z[:, QR:QR + KVR], gkv_ref[...], dckv)
        dgq_ref[...] += dgq
        dgkv_ref[...] += dgkv
        dza_ref[:, :QR] = dzq.astype(BF16)
        dza_ref[:, QR:QR + KVR] = dzkv.astype(BF16)
        dza_ref[:, QR + KVR:] = _rope_bwd(dkr, c, s1, s2).astype(BF16)

    dza, dqp, dkv, gb["q_norm_g"], gb["kv_norm_g"] = _pcall(
        body, "mla_prep_bwd", (t // tm,),
        [dqs, dk, dv, za, p["q_norm_g"], p["kv_norm_g"], p["w_uq"], p["w_ukv"], *tabs],
        [_row(tm, QW), _row(tm, QW), _row(tm, HEADS * VD), _row(tm, ZA), _lyr(l, 1, QR), _lyr(l, 1, KVR),
         _lyr(_wl(p["w_uq"], l), QR, QW), _lyr(_wl(p["w_ukv"], l), KVR, KVW), _row(tm, 128), _row(tm, 128), _row(tm, 128)],
        [_row(tm, ZA), _row(tm, QW), _row(tm, KVW), _lyr(l.g, 1, QR), _lyr(l.g, 1, KVR)],
        [_sds((t, ZA), BF16), _sds((t, QW), BF16), _sds((t, KVW), BF16), _sds((depth, 1, QR), F32),
         _sds((depth, 1, KVR), F32)], ("arbitrary",), prevs={3: gb.get("q_norm_g"), 4: gb.get("kv_norm_g")})
    return dza, dqp, dkv


def _in_proj_bwd(dza, dzb, dx2, sv, p, l, depth, gb):
    t = dx2.shape[0]
    tm = min(ROW_TILE, t)

    def body(dza_ref, dzb_ref, wa_ref, wb_ref, x_ref, dx2_ref, g_ref, dx_ref, dg_ref):
        _acc_init(pl.program_id(0), dg_ref)
        dh = (lax.dot_general(dza_ref[...], wa_ref[...], NT, preferred_element_type=F32)
              + lax.dot_general(dzb_ref[...], wb_ref[...], NT, preferred_element_type=F32))
        dx, dg = _rms_bwd(x_ref[...], g_ref[...], dh)
        dg_ref[...] += dg
        dx_ref[...] = dx2_ref[...] + dx

    dx, gb["mix_pre_g"] = _pcall(
        body, "in_proj_bwd", (t // tm,), [dza, dzb, p["w_in_a"], p["w_in_b"], sv["x"], dx2, p["mix_pre_g"]],
        [_row(tm, ZA), _row(tm, ZB), _lyr(_wl(p["w_in_a"], l), D, ZA), _lyr(_wl(p["w_in_b"], l), D, ZB), _row(tm, D), _row(tm, D), _lyr(l, 1, D)],
        [_row(tm, D), _lyr(l.g, 1, D)], [_sds((t, D), F32), _sds((depth, 1, D), F32)], ("arbitrary",),
        prevs={1: gb.get("mix_pre_g")})
    return dx


def _mm_tn(a, b, tn, name, l, depth, gb):
    t, k = a.shape
    n = b.shape[1]
    tt = min(ROW_TILE, t)

    def body(a_ref, b_ref, o_ref):
        _acc_init(pl.program_id(1), o_ref)
        o_ref[...] += lax.dot_general(a_ref[...], b_ref[...], TN, preferred_element_type=F32)

    gb[name] = _pcall(
        body, "d" + name, (n // tn, t // tt), [a, b],
        [pl.BlockSpec((tt, k), lambda j, s: (s, 0)), pl.BlockSpec((tt, tn), lambda j, s: (s, j))],
        pl.BlockSpec((None, k, tn), lambda j, s: (l.g, 0, j)), _sds((depth, k, n), F32), ("parallel", "arbitrary"),
        prevs={0: gb.get(name)})


def _dw_ffn(a, b, kind, l, depth, gb):
    t = a.shape[0]
    tt = min(ROW_TILE, t)
    nsteps = t // tt

    def body(a_ref, b_ref, o_ref, acc):
        s = pl.program_id(0)
        _acc_init(s, acc)
        acc[...] += lax.dot_general(a_ref[...], b_ref[...], TN, preferred_element_type=F32)

        @pl.when(s == nsteps - 1)
        def _():
            for k in range(4):
                o_ref[k] = acc[k * HP:(k + 1) * HP, :].astype(BF16)

    rows = lambda n: pl.BlockSpec((tt, n), lambda s: (s, 0))
    if kind == "down":
        name = "down"
        out_spec = pl.BlockSpec((4, None, HP, D), lambda s: (0, l.g, 0, 0))
        out_shape = _sds((4, depth, HP, D), BF16)
    else:
        which = 0 if kind == "gate" else 1
        name = "gu"
        out_spec = pl.BlockSpec((4, None, None, HP, D), lambda s: (0, l.g, which, 0, 0))
        out_shape = _sds((4, depth, 2, HP, D), BF16)
    gb[name] = _pcall(body, "dw_" + kind, (nsteps,), [a, b], [rows(DFFP), rows(D)], out_spec, out_shape, ("arbitrary",),
                      scratch=[pltpu.VMEM((DFFP, D), F32)], prevs={0: gb.get(name)})


def _ffn_views(bufs):
    return {"w_gu": bufs[1], "w_down": bufs[2].reshape(bufs[2].shape[:2] + (HP, D))}


def _layer_fwd(x, p, l, tabs, fetch):
    h1, za, zb = _in_proj(x, p, l)
    cqn, ckvn, qs, k, v = _mla_prep(za, p, l, tabs)
    ya, lse, bufs = _attn_fwd(qs, k, v, fetch)
    if fetch:
        p = {**p, **_ffn_views(bufs)}
    mix, yb, yc = _mixer_fwd(zb, ya, p, l)
    o, x2, h2 = _out_proj(mix, x, p, l)
    a, b, s = _ffn_up(h2, p, l)
    f, x3 = _ffn_down(s, x2, p, l)
    saved = dict(x=x, h1=h1, za=za, zb=zb, cqn=cqn, ckvn=ckvn, qs=qs, k=k, v=v, ya=ya, lse=lse, mix=mix, yb=yb, yc=yc,
                 o=o, x2=x2, h2=h2, a=a, b=b, s=s, f=f)
    return x3, saved, bufs if fetch else None


class _Layer(int):
    def __new__(cls, l, g):
        obj = int.__new__(cls, l)
        obj.g = g
        return obj


def _layer_bwd(dx3, p, sv, l, depth, gb, tabs, ride=()):
    df, da, db = _ffn_down_bwd(dx3, sv, p, l, depth, gb)
    _dw_ffn(sv["s"], df, "down", l, depth, gb)
    dx2 = _ffn_up_bwd(da, db, dx3, sv, p, l, depth, gb)
    _dw_ffn(da, sv["h2"], "gate", l, depth, gb)
    _dw_ffn(db, sv["h2"], "up", l, depth, gb)
    do, dmix = _out_proj_bwd(dx2, sv, p, l, depth, gb)
    _mm_tn(sv["mix"], do, D, "w_out", l, depth, gb)
    dya, dzb, delta = _mixer_bwd(dmix, sv, p, l, depth, gb)
    dqs, dk, dv, sent = _attn_bwd(sv["qs"], sv["k"], sv["v"], dya, sv["lse"], delta, ride)
    dza, dqp, dkv = _mla_prep_bwd(dqs, dk, dv, sv, p, l, depth, gb, tabs)
    _mm_tn(sv["cqn"], dqp, QW, "w_uq", l, depth, gb)
    _mm_tn(sv["ckvn"], dkv, KVW, "w_ukv", l, depth, gb)
    _mm_tn(sv["h1"], dza, ZA, "w_in_a", l, depth, gb)
    _mm_tn(sv["h1"], dzb, ZB, "w_in_b", l, depth, gb)
    return _in_proj_bwd(dza, dzb, dx2, sv, p, l, depth, gb), sent


def _rope_tables(positions):
    inv_freq = 1.0 / (ROPE_THETA ** (jnp.arange(0, ROPE // 2, dtype=F32) / (ROPE // 2)))
    ang = positions.astype(F32)[:, None] * inv_freq
    cos, sin = jnp.cos(ang), jnp.sin(ang)
    t = positions.shape[0]
    one, zero = jnp.ones((t, 64), F32), jnp.zeros((t, 16), F32)
    c = jnp.concatenate([one, cos, cos, one[:, :32]], axis=1)
    s1 = jnp.concatenate([zero, zero, zero, zero, -sin, zero, zero, zero], axis=1)
    s2 = jnp.concatenate([zero, zero, zero, zero, zero, sin, zero, zero], axis=1)
    return c, s1, s2


def _mixer_weight_params(full):
    w_in = full["w_in"]
    depth = w_in.shape[0]
    zpad = lambda n: jnp.zeros((depth, D, n), w_in.dtype)
    kv = full["w_ukv"].reshape(depth, KVR, HEADS, NOPE + VD)
    return {
        "w_in_a": jnp.concatenate([w_in[:, :, :640], zpad(64), w_in[:, :, 640:672], zpad(32)], axis=2),
        "w_in_b": w_in[:, :, 672:],
        "w_uq": jnp.pad(full["w_uq"].reshape(depth, QR, HEADS, NOPE + ROPE),
                        ((0, 0), (0, 0), (0, 0), (0, 32))).reshape(depth, QR, QW),
        "w_ukv": jnp.concatenate([jnp.pad(kv[..., :NOPE], ((0, 0), (0, 0), (0, 0), (0, 64))).reshape(depth, KVR, QW),
                                  kv[..., NOPE:].reshape(depth, KVR, HEADS * VD)], axis=2),
        "w_out": full["w_out"], "conv_w": full["conv_w"],
    }


def _small_params(w):
    p = {"w_sp": w["w_sp"], "b_sp": jnp.repeat(jnp.swapaxes(w["b_sp"], 1, 2), 64, axis=2)}
    for n in ("mix_pre_g", "mix_post_g", "ffn_pre_g", "ffn_post_g", "q_norm_g", "kv_norm_g", "sg_ln_g", "sg_ln_b",
              "out_norm_g"):
        p[n] = w[n][:, None, :]
    return p


def _natural_grads(gb):
    depth = gb["w_in_a"].shape[0]
    ga, kv = gb["w_in_a"], gb["w_ukv"]
    out = {
        "w_in": jnp.concatenate([ga[:, :, :640], ga[:, :, 704:736], gb["w_in_b"]], axis=2),
        "w_uq": gb["w_uq"].reshape(depth, QR, HEADS, 128)[..., :NOPE + ROPE].reshape(depth, QR, HEADS * (NOPE + ROPE)),
        "w_ukv": jnp.concatenate([kv[:, :, :QW].reshape(depth, KVR, HEADS, 128)[..., :NOPE],
                                  kv[:, :, QW:].reshape(depth, KVR, HEADS, VD)], axis=3).reshape(depth, KVR, -1),
        "b_sp": jnp.swapaxes(gb["b_sp_t"].reshape(depth, CHUNK, 4, 64).sum(axis=-1), 1, 2),
    }
    for n in ("w_out", "w_sp", "conv_w"):
        out[n] = gb[n]
    for n in ("mix_pre_g", "mix_post_g", "ffn_pre_g", "ffn_post_g", "q_norm_g", "kv_norm_g", "sg_ln_g", "sg_ln_b",
              "out_norm_g"):
        out[n] = gb[n][:, 0, :]
    return out


def _local_step(x, positions, target, small, mine, bufs, shard_shapes, fetch=True, front=None):
    depth = small["w_sp"].shape[0]
    tabs = _rope_tables(positions)
    ps = _small_params(small)
    saved, mixer_w = [], []
    for l in range(depth):
        mixer_w.append(_mixer_weight_params(_unpack_weights(bufs[0], l, shard_shapes)))
        p = {**ps, **mixer_w[l], **_ffn_views(bufs)}
        layers = [l + 1 if l + 1 < depth else None, l, l]
        x, sv, fetched = _layer_fwd(x, p, l, tabs, (mine, bufs, layers) if fetch else None)
        bufs = fetched or bufs
        saved.append(sv)
    dx, acc = _loss_head(x, target)
    loss = (0.5 / D) * jnp.sum(acc)
    nbatch = depth // 2
    gbs = [{} for _ in range(nbatch)]
    fronts, sent = [None] * nbatch, [None] * nbatch
    ride = ()
    for l in reversed(range(depth)):
        b = l // 2
        dx, got = _layer_bwd(dx, {**ps, **mixer_w[l], **_ffn_views(bufs)}, saved[l], _Layer(l, l % 2), 2, gbs[b], tabs, ride)
        if ride:
            sent[b + 1], ride = got, ()
        if l % 2 == 0:
            done = (_natural_grads(gbs[b]), gbs[b]["gu"], gbs[b]["down"])
            fronts[b] = front(b, *done) if front else done
            if front and b > 0:
                ride = tuple(fronts[b])
    return loss, dx, fronts, sent


def _place():
    x, y, c = lax.axis_index("x"), lax.axis_index("y"), lax.axis_index("c")
    chips = [(1 - x, y), (x, 1 - y), (1 - x, 1 - y)]
    return x, y, c, 2 * x + y, chips


def _remote(src, dst, send_sem, recv_sem, to):
    return pltpu.make_async_remote_copy(src_ref=src, dst_ref=dst, send_sem=send_sem, recv_sem=recv_sem, device_id=to,
                                        device_id_type=MESH_ID)


def _gather_ops(mine_refs, out_refs, sems, layers):
    send_sems, recv_sems, fsend_sems, frecv_sems = sems
    x, y, c, k, chips = _place()
    sib = (x, y, 1 - c)
    pairs = [(b, n) for n in range(3) for b in range(len(mine_refs)) if layers[b] is not None]

    def slot(n):
        return 2 * chips[n][0] + chips[n][1]

    def ici(b, n, dst_chip):
        return _remote(mine_refs[b].at[layers[b], c], out_refs[b].at[dst_chip, layers[b], c], send_sems.at[b, n],
                       recv_sems.at[b, n], (*chips[n], c))

    def d2d(b, n, half):
        piece = out_refs[b].at[slot(n), layers[b], half]
        return _remote(piece, piece, fsend_sems.at[b, n], frecv_sems.at[b, n], sib)

    def start():
        for b, n in pairs:
            ici(b, n, k).start()

    def hand_over():
        for b, n in pairs:
            ici(b, n, slot(n)).wait_recv()
            d2d(b, n, c).start()

    def drain():
        for b, n in pairs:
            d2d(b, n, 1 - c).wait_recv()
        for b, n in pairs:
            ici(b, n, k).wait_send()
            d2d(b, n, c).wait_send()

    return start, hand_over, drain


def _gather_first_layer(mine):
    nb = len(mine)

    def body(*refs):
        start, hand_over, drain = _gather_ops(refs[:nb], refs[nb:2 * nb], refs[2 * nb:], [0] + [None] * (nb - 1))
        start()
        hand_over()
        drain()

    return pl.pallas_call(
        body, name="gather_first_layer", in_specs=[ANY] * nb, out_specs=[ANY] * nb,
        out_shape=[_sds((4,) + a.shape, a.dtype) for a in mine],
        scratch_shapes=[pltpu.SemaphoreType.DMA((nb, 3))] * 4)(*mine)


def _swap_halves(bigs, wholes=()):
    nb, n = len(bigs), len(bigs) + len(wholes)

    def body(*refs):
        src, dst = refs[:n], refs[n:2 * n]
        send_sems, recv_sems = refs[2 * n:]
        x, y, c, _, _ = _place()
        sib = (x, y, 1 - c)
        cps = [_remote(src[b].at[:, 1 - c] if b < nb else src[b], dst[b], send_sems.at[b], recv_sems.at[b], sib)
               for b in range(n)]
        for cp in cps:
            cp.start()
        for cp in cps:
            cp.wait()

    return pl.pallas_call(
        body, name="swap_halves", in_specs=[ANY] * n, out_specs=[ANY] * n,
        out_shape=[_sds((4,) + a.shape[2:], a.dtype) for a in bigs] + [_sds(a.shape, a.dtype) for a in wholes],
        scratch_shapes=[pltpu.SemaphoreType.DMA((n,))] * 2)(*bigs, *wholes)


def _sum_tile(r):
    return max(cand for cand in range(16, 641, 16) if r % cand == 0)


def _pair_sum(big, rbig, c):
    _, _, r, w = big.shape
    tr = _sum_tile(r)

    def body(c_ref, big_ref, rbig_ref, p_ref):
        p_ref[...] = (big_ref[...].astype(F32) + rbig_ref[...].astype(F32)).astype(BF16)

    grid_spec = pltpu.PrefetchScalarGridSpec(
        num_scalar_prefetch=1, grid=(4, r // tr),
        in_specs=[pl.BlockSpec((None, None, tr, w), lambda j, i, cr: (j, cr[0], i, 0)),
                  pl.BlockSpec((None, tr, w), lambda j, i, cr: (j, i, 0))],
        out_specs=pl.BlockSpec((None, tr, w), lambda j, i, cr: (j, i, 0)))
    return pl.pallas_call(body, name="pair_sum", grid_spec=grid_spec, out_shape=_sds((4, r, w), BF16),
                          compiler_params=_cp("parallel", "parallel"))(c, big, rbig)


def _small_sum(parts):
    n, ns, _ = parts.shape

    def body(p_ref, o_ref):
        s = p_ref[0]
        for j in range(1, n):
            s = s + p_ref[j]
        o_ref[...] = s

    return pl.pallas_call(body, name="small_sum", out_shape=_sds((ns, 128), F32))(parts)


def _exchange_ops(p_refs, rb_refs, sems, small=None):
    send_sems, recv_sems = sems[0], sems[1]
    nb = len(p_refs)
    x, y, c, k, chips = _place()

    def copies(landing):
        out = []
        for n, (cx, cy) in enumerate(chips):
            to, kj = (cx, cy, c), 2 * cx + cy
            for b in range(nb):
                out.append(_remote(p_refs[b].at[k if landing else kj], rb_refs[b].at[n], send_sems.at[b, n],
                                   recv_sems.at[b, n], to))
            if small:
                out.append(_remote(small[0], small[1].at[kj if landing else k], send_sems.at[nb, n], recv_sems.at[nb, n], to))
        return out

    def local():
        return pltpu.make_async_copy(small[0], small[1].at[k], sems[2])

    def start():
        if small:
            local().start()
        for cp in copies(False):
            cp.start()

    def finish():
        for cp in copies(True):
            cp.wait_recv()
        for cp in copies(False):
            cp.wait_send()
        if small:
            local().wait()

    return start, finish


def _chip_exchange(ps, small):
    nb = len(ps)
    ns = small.shape[0]

    def body(*refs):
        start, finish = _exchange_ops(refs[:nb], refs[nb + 1:2 * nb + 1], refs[2 * nb + 2:], (refs[nb], refs[2 * nb + 1]))
        start()
        finish()

    return pl.pallas_call(
        body, name="chip_exchange", in_specs=[ANY] * (nb + 1), out_specs=[ANY] * (nb + 1),
        out_shape=[_sds((3,) + a.shape[1:], a.dtype) for a in ps] + [_sds((4, ns, 128), small.dtype)],
        scratch_shapes=[pltpu.SemaphoreType.DMA((nb + 1, 3))] * 2 + [pltpu.SemaphoreType.DMA(())])(*ps, small)


def _chip_sum(p, rb, chip):
    _, r, w = p.shape
    tr = _sum_tile(r)

    def body(k_ref, p_ref, rb_ref, o_ref):
        acc = p_ref[...].astype(F32)
        for j in range(3):
            acc = acc + rb_ref[j].astype(F32)
        o_ref[...] = acc

    grid_spec = pltpu.PrefetchScalarGridSpec(
        num_scalar_prefetch=1, grid=(r // tr,),
        in_specs=[pl.BlockSpec((None, tr, w), lambda i, kr: (kr[0], i, 0)), pl.BlockSpec((3, tr, w), lambda i, kr: (0, i, 0))],
        out_specs=pl.BlockSpec((tr, w), lambda i, kr: (i, 0)))
    return pl.pallas_call(body, name="chip_sum", grid_spec=grid_spec, out_shape=_sds((r, w), F32),
                          compiler_params=_cp("parallel"))(chip, p, rb)


def _send_to_sibling(reds):
    nb = len(reds)

    def body(*refs):
        red_refs, out_refs = refs[:nb], refs[nb:2 * nb]
        send_sems, recv_sems = refs[2 * nb:]
        x, y, c, _, _ = _place()
        cps = [_remote(red_refs[b], out_refs[b], send_sems.at[b], recv_sems.at[b], (x, y, 1 - c)) for b in range(nb)]
        for cp in cps:
            cp.start()
        for cp in cps:
            cp.wait()

    return pl.pallas_call(
        body, name="send_to_sibling", in_specs=[ANY] * nb, out_specs=[ANY] * nb,
        out_shape=[_sds(a.shape, a.dtype) for a in reds], scratch_shapes=[pltpu.SemaphoreType.DMA((nb,))] * 2)(*reds)


def _adam_math(w, g, m, v):
    nm = ADAM_B1 * m + (1.0 - ADAM_B1) * g
    nv = ADAM_B2 * v + (1.0 - ADAM_B2) * (g * g)
    m_hat = nm / (1.0 - ADAM_B1 ** ADAM_STEP)
    v_hat = nv / (1.0 - ADAM_B2 ** ADAM_STEP)
    return -ADAM_LR * (m_hat / (jnp.sqrt(v_hat) + ADAM_EPS) + ADAM_WD * w), nm, nv


def _adamw_shard(w, m, v, owns, others, c, name, pick=None):
    depth, r, n = w.shape
    nbatch = len(owns)
    tr = max(cand for cand in range(8, min(r, 256) + 1, 8) if r % cand == 0)
    npad = owns[0].shape[-1]

    def body(c_ref, w_ref, m_ref, v_ref, *refs):
        g_ref, d_ref, nm_ref, nv_ref = refs[2 * nbatch:]
        l = pl.program_id(0)
        mine = (l % 2) == c_ref[0]
        g = jnp.where(mine, refs[0][...], refs[nbatch][...])
        for b in range(1, nbatch):
            g = jnp.where(l // 2 == b, jnp.where(mine, refs[b][...], refs[nbatch + b][...]), g)
        g = g[:, :n]
        g_ref[...] = g
        d_ref[...], nm_ref[...], nv_ref[...] = _adam_math(w_ref[...], g, m_ref[...], v_ref[...])

    blk = pl.BlockSpec((None, tr, n), lambda l, i, cr: (l, i, 0))
    if pick is None:
        gblk = pl.BlockSpec((tr, npad), lambda l, i, cr: (i, 0))
    else:
        gblk = pl.BlockSpec((None, tr, npad), lambda l, i, cr: (pick, i, 0))
    grid_spec = pltpu.PrefetchScalarGridSpec(num_scalar_prefetch=1, grid=(depth, r // tr),
                                             in_specs=[blk] * 3 + [gblk] * (2 * nbatch), out_specs=[blk] * 4)
    return pl.pallas_call(body, name=name, grid_spec=grid_spec, out_shape=[_sds(w.shape, F32)] * 4,
                          compiler_params=_cp("parallel", "parallel"))(c, w, m, v, *owns, *others)


def _pad_ffn_shards(w_gate, w_up, w_down):
    depth = w_gate.shape[0]
    hr = HP // 2

    def gu_body(g_ref, u_ref, o_ref):
        for which, ref in enumerate((g_ref, u_ref)):
            o_ref[which, 0:HS, :] = ref[...].astype(BF16)
            o_ref[which, HS:HP, :] = jnp.zeros((HP - HS, D), BF16)

    blk = pl.BlockSpec((None, HS, D), lambda l: (l, 0, 0))
    gu = pl.pallas_call(
        gu_body, name="pad_gate_up", grid=(depth,), in_specs=[blk, blk],
        out_specs=pl.BlockSpec((None, 2, HP, D), lambda l: (l, 0, 0, 0)),
        out_shape=_sds((depth, 2, HP, D), BF16), compiler_params=_cp("parallel"))(w_gate, w_up)

    def down_body(w_ref, o_ref):
        o_ref[0] = w_ref[0:hr, :].astype(BF16)
        o_ref[1, 0:HS - hr, :] = w_ref[hr:HS, :].astype(BF16)
        o_ref[1, HS - hr:hr, :] = jnp.zeros((HP - HS, D), BF16)

    down = pl.pallas_call(
        down_body, name="pad_down", grid=(depth,), in_specs=[pl.BlockSpec((None, HS, D), lambda l: (l, 0, 0))],
        out_specs=pl.BlockSpec((None, 2, hr, D), lambda l: (l, 0, 0, 0)),
        out_shape=_sds((depth, 2, hr, D), BF16), compiler_params=_cp("parallel"))(w_down)
    return gu, down


def _adamw_small(w, g, m, v):
    r = w.shape[0]
    tr = max(cand for cand in range(8, 513, 8) if r % cand == 0)

    def body(w_ref, g_ref, m_ref, v_ref, d_ref, nm_ref, nv_ref):
        d_ref[...], nm_ref[...], nv_ref[...] = _adam_math(w_ref[...], g_ref[...], m_ref[...], v_ref[...])

    blk = pl.BlockSpec((tr, 128), lambda i: (i, 0))
    return pl.pallas_call(body, name="adamw_small", grid=(r // tr,), in_specs=[blk] * 4, out_specs=[blk] * 3,
                          out_shape=[_sds(w.shape, F32)] * 3, compiler_params=_cp("parallel"))(w, g, m, v)


def _to_pack(a, name):
    depth = a.shape[0]
    if name in ROW_SHARDED:
        return jnp.swapaxes(a.reshape(depth, 4, -1, D), 0, 1)
    return jnp.transpose(a.reshape(depth, a.shape[1], 4, a.shape[2] // 4), (2, 0, 1, 3)).reshape(4, depth, -1, D)


def _pack_rows(parts, lead, dtype, tail=None):
    pieces, at = [], 0
    for n, off, rows in PACK:
        if off > at:
            pieces.append(jnp.zeros(lead + (off - at, D), dtype))
        pieces.append(parts[n].astype(dtype))
        at = off + rows
    if tail is not None:
        pieces.append(tail)
        at += tail.shape[-2]
    pieces.append(jnp.zeros(lead + (PACK_ROWS - at, D), dtype))
    return jnp.concatenate(pieces, axis=len(lead))


def _pack_weight_shards(sh):
    depth = sh["w_in"].shape[0]
    parts = {n: sh[n].reshape(depth, rows, D) for n, _, rows in PACK}
    conv = lax.bitcast_convert_type(sh["conv_w"].reshape(depth, 3 * 64), BF16).reshape(depth, 1, 384)
    flat = _pack_rows(parts, (depth,), BF16, tail=jnp.pad(conv, ((0, 0), (0, 0), (0, D - 384))))
    return flat.reshape(depth, 2, PACK_ROWS // 2, D)


def _unpack_weights(gathered, l, shard_shapes):
    depth = 1
    flat = gathered[:, l].reshape(4, 1, PACK_ROWS, D)
    full = {}
    for n, off, rows in PACK:
        shp = shard_shapes[n][1:]
        piece = flat[:, :, off:off + rows, :].reshape((4, depth) + shp)
        if n in ROW_SHARDED:
            full[n] = jnp.transpose(piece, (1, 0, 2, 3)).reshape(depth, 4 * shp[0], shp[1])
        else:
            full[n] = jnp.transpose(piece, (1, 2, 0, 3)).reshape(depth, shp[0], 4 * shp[1])
    conv = lax.bitcast_convert_type(flat[:, :, CONV_ROW, :384].reshape(4, depth, 192, 2), F32)
    full["conv_w"] = jnp.transpose(conv.reshape(4, depth, 3, 64), (1, 2, 0, 3)).reshape(depth, 3, CVW)
    return full


def _pack_grad_shards(g):
    depth = g["w_in"].shape[0]
    return _pack_rows({n: _to_pack(g[n], n) for n, _, _ in PACK}, (4, depth), BF16)


def _pack_small(arrs, names_shapes, depth):
    flat = jnp.concatenate([arrs[n].reshape(depth, -1) for n, _ in names_shapes], axis=1).reshape(-1)
    rows = -(-flat.shape[0] // 1024) * 8
    return jnp.pad(flat, (0, rows * 128 - flat.shape[0])).reshape(rows, 128)


def _unpack_small(packed, names_shapes, depth):
    per_layer = sum(math.prod(s) for _, s in names_shapes)
    flat = packed.reshape(-1)[:depth * per_layer].reshape(depth, per_layer)
    out, off = {}, 0
    for n, s in names_shapes:
        size = math.prod(s)
        out[n] = flat[:, off:off + size].reshape((depth,) + s)
        off += size
    return out


def kernel(x, positions, mix_pre_g, mix_post_g, ffn_pre_g, ffn_post_g, w_in, q_norm_g, w_uq, kv_norm_g, w_ukv, sg_ln_g, sg_ln_b, w_sp, b_sp, conv_w, out_norm_g, w_out, w_gate, w_up, w_down, loss_target, m_mix_pre_g, m_mix_post_g, m_ffn_pre_g, m_ffn_post_g, m_w_in, m_q_norm_g, m_w_uq, m_kv_norm_g, m_w_ukv, m_sg_ln_g, m_sg_ln_b, m_w_sp, m_b_sp, m_conv_w, m_out_norm_g, m_w_out, m_w_gate, m_w_up, m_w_down, v_mix_pre_g, v_mix_post_g, v_ffn_pre_g, v_ffn_post_g, v_w_in, v_q_norm_g, v_w_uq, v_kv_norm_g, v_w_ukv, v_sg_ln_g, v_sg_ln_b, v_w_sp, v_b_sp, v_conv_w, v_out_norm_g, v_w_out, v_w_gate, v_w_up, v_w_down):
    w = dict(mix_pre_g=mix_pre_g, mix_post_g=mix_post_g, ffn_pre_g=ffn_pre_g, ffn_post_g=ffn_post_g, w_in=w_in,
             q_norm_g=q_norm_g, w_uq=w_uq, kv_norm_g=kv_norm_g, w_ukv=w_ukv, sg_ln_g=sg_ln_g, sg_ln_b=sg_ln_b, w_sp=w_sp,
             b_sp=b_sp, conv_w=conv_w, out_norm_g=out_norm_g, w_out=w_out, w_gate=w_gate, w_up=w_up, w_down=w_down)
    m = dict(mix_pre_g=m_mix_pre_g, mix_post_g=m_mix_post_g, ffn_pre_g=m_ffn_pre_g, ffn_post_g=m_ffn_post_g, w_in=m_w_in,
             q_norm_g=m_q_norm_g, w_uq=m_w_uq, kv_norm_g=m_kv_norm_g, w_ukv=m_w_ukv, sg_ln_g=m_sg_ln_g, sg_ln_b=m_sg_ln_b,
             w_sp=m_w_sp, b_sp=m_b_sp, conv_w=m_conv_w, out_norm_g=m_out_norm_g, w_out=m_w_out, w_gate=m_w_gate,
             w_up=m_w_up, w_down=m_w_down)
    v = dict(mix_pre_g=v_mix_pre_g, mix_post_g=v_mix_post_g, ffn_pre_g=v_ffn_pre_g, ffn_post_g=v_ffn_post_g, w_in=v_w_in,
             q_norm_g=v_q_norm_g, w_uq=v_w_uq, kv_norm_g=v_kv_norm_g, w_ukv=v_w_ukv, sg_ln_g=v_sg_ln_g, sg_ln_b=v_sg_ln_b,
             w_sp=v_w_sp, b_sp=v_b_sp, conv_w=v_conv_w, out_norm_g=v_out_norm_g, w_out=v_w_out, w_gate=v_w_gate,
             w_up=v_w_up, w_down=v_w_down)
    depth = w_in.shape[0]
    c = lax.axis_index("c").astype(jnp.int32).reshape(1)
    chip = (2 * lax.axis_index("x") + lax.axis_index("y")).astype(jnp.int32)

    nbatch = depth // 2

    mine = [_pack_weight_shards(w), *_pad_ffn_shards(jnp.swapaxes(w_gate, 1, 2), jnp.swapaxes(w_up, 1, 2), w_down)]
    bufs = [lax.dynamic_update_slice(g, a[None], (chip,) + (0,) * a.ndim)
            for g, a in zip(_gather_first_layer(mine), mine)]

    small_grads = [None] * nbatch
    small_pair = []

    def front(b, grads, g_gu, g_down):
        small_grads[b] = grads
        bigs = [_pack_grad_shards(grads), g_gu.reshape(4, 2, 2 * HP, D), g_down]
        if b > 0:
            rbigs = _swap_halves(bigs)
        else:
            small = _pack_small({n: jnp.concatenate([g[n] for g in small_grads]) for n, _ in SMALL}, SMALL, depth)
            *rbigs, rsmall = _swap_halves(bigs, [small])
            small_pair.append(_small_sum(jnp.stack([small, rsmall])))
        return [_pair_sum(a, r, c) for a, r in zip(bigs, rbigs)]

    loss, dx, ps, sent = _local_step(x[0], positions[0], loss_target[0], w, mine, bufs,
                                     {n: w[n].shape for n, _, _ in PACK}, front=front)
    loss = lax.psum(loss, ("x", "y", "c"))

    *sent[0], rs = _chip_exchange(ps[0], small_pair[0])
    own = [[_chip_sum(p, rb, chip.reshape(1)) for p, rb in zip(ps[b], sent[b])] for b in range(nbatch)]
    other = [_send_to_sibling(o) for o in own]
    g_small = _unpack_small(_small_sum(rs), SMALL, depth)
    g_small["conv_w"] = lax.dynamic_slice_in_dim(g_small["conv_w"], chip * 64, 64, axis=2)

    gw, delta, new_m, new_v = dict(g_small), {}, {}, {}

    def adam(n, pieces, pick=None, turned=False):
        view = (lambda a: jnp.swapaxes(a, 1, 2)) if turned else (lambda a: a)
        outs = _adamw_shard(view(w[n]), view(m[n]), view(v[n]), [pieces(o) for o in own], [pieces(o) for o in other], c,
                            "adamw_" + n, pick)
        gw[n], delta[n], new_m[n], new_v[n] = [view(o) for o in outs]

    for n, off, rows in PACK:
        adam(n, lambda o: o[0][off:off + rows, :].reshape(w[n].shape[1:]))
    adam("w_gate", lambda o: o[1].reshape(2, HP, D), 0, turned=True)
    adam("w_up", lambda o: o[1].reshape(2, HP, D), 1, turned=True)
    adam("w_down", lambda o: o[2])
    small_local = tuple((n, w[n].shape[1:]) for n, _ in SMALL)
    d_, m_, v_ = _adamw_small(_pack_small(w, small_local, depth), _pack_small(gw, small_local, depth),
                              _pack_small(m, small_local, depth), _pack_small(v, small_local, depth))
    delta.update(_unpack_small(d_, small_local, depth))
    new_m.update(_unpack_small(m_, small_local, depth))
    new_v.update(_unpack_small(v_, small_local, depth))

    return (loss, dx[None], *[gw[n] for n in WEIGHTS], *[delta[n] for n in WEIGHTS], *[new_m[n] for n in WEIGHTS],
            *[new_v[n] for n in WEIGHTS])
```

```python
import math

import jax
import jax.numpy as jnp
from jax import lax
from jax.experimental import pallas as pl
from jax.experimental.pallas import tpu as pltpu

F32 = jnp.float32
BF16 = jnp.bfloat16

D = 1024
HEADS = 8
NOPE = 64
ROPE = 32
VD = 64
QR = 384
KVR = 256
SGW = 256
CVW = 256
CHUNK = 128
DFF = 2816
EPS = 1e-6
ROPE_THETA = 10000.0
LOG2E = 1.4426950408889634
LN2 = 0.6931471805599453
QSCALE = (NOPE + ROPE) ** -0.5 * LOG2E
ZA = 768
ZB = 1280
QW = HEADS * 128
KVW = HEADS * 128 + HEADS * VD
NEG = -1e30
GC0 = 0.7978845608028654
GC1 = 0.044715

ADAM_LR = 0.001
ADAM_B1 = 0.9
ADAM_B2 = 0.999
ADAM_EPS = 1e-08
ADAM_WD = 0.01
ADAM_STEP = 10

V7X_VMEM_LIMIT = 52 * 1024 * 1024
ROW_TILE = 512
ATT_TILE = 512
ATT_HEADS = 4

NT = (((1,), (1,)), ((), ()))
TN = (((0,), (0,)), ((), ()))

HS = DFF // 4
HP = 768
DFFP = 4 * HP

PACK = (("w_in", 0, 488), ("w_out", 512, 256), ("w_ukv", 768, 64), ("w_uq", 832, 72))
CONV_ROW = 904
PACK_ROWS = 928
ROW_SHARDED = ("w_out",)
SMALL = (("mix_pre_g", (D,)), ("mix_post_g", (D,)), ("ffn_pre_g", (D,)), ("ffn_post_g", (D,)), ("q_norm_g", (QR,)),
         ("kv_norm_g", (KVR,)), ("sg_ln_g", (SGW,)), ("sg_ln_b", (SGW,)), ("w_sp", (4, CHUNK, CHUNK)), ("b_sp", (4, CHUNK)),
         ("conv_w", (3, CVW)), ("out_norm_g", (D,)))
WEIGHTS = ["mix_pre_g", "mix_post_g", "ffn_pre_g", "ffn_post_g", "w_in", "q_norm_g", "w_uq", "kv_norm_g", "w_ukv", "sg_ln_g",
           "sg_ln_b", "w_sp", "b_sp", "conv_w", "out_norm_g", "w_out", "w_gate", "w_up", "w_down"]

MESH_ID = pl.DeviceIdType.MESH
ANY = pl.BlockSpec(memory_space=pl.ANY)


def _cp(*sem):
    return pltpu.CompilerParams(dimension_semantics=sem, vmem_limit_bytes=V7X_VMEM_LIMIT)


def _sds(shape, dtype):
    return jax.ShapeDtypeStruct(shape, dtype)


def _row(tm, n):
    return pl.BlockSpec((tm, n), lambda i: (i, 0))


def _lyr(l, *shape):
    return pl.BlockSpec((None,) + shape, lambda *_: (l,) + (0,) * len(shape))


def _wl(a, l):
    return 0 if a.shape[0] == 1 else l


def _pcall(body, name, grid, ins, in_specs, out_specs, out_shape, sem, scratch=(), prevs=None):
    prevs = {k: v for k, v in (prevs or {}).items() if v is not None}
    order = sorted(prevs)
    n_in = len(ins)

    def wrapped(*refs):
        return body(*refs[:n_in], *refs[n_in + len(order):])

    return pl.pallas_call(
        wrapped, name=name, grid=grid, in_specs=list(in_specs) + [ANY] * len(order), out_specs=out_specs,
        out_shape=out_shape, scratch_shapes=list(scratch),
        input_output_aliases={n_in + i: k for i, k in enumerate(order)},
        compiler_params=_cp(*sem))(*ins, *[prevs[k] for k in order])


def _rms(x, g):
    r = lax.rsqrt(jnp.mean(x * x, axis=-1, keepdims=True) + EPS)
    return x * r * g


def _rms_bwd(x, g, dy):
    r = lax.rsqrt(jnp.mean(x * x, axis=-1, keepdims=True) + EPS)
    xh = x * r
    dg = jnp.sum(dy * xh, axis=0, keepdims=True)
    dxh = dy * g
    dx = r * (dxh - xh * jnp.mean(dxh * xh, axis=-1, keepdims=True))
    return dx, dg


def _sigmoid(x):
    return 0.5 * jnp.tanh(0.5 * x) + 0.5


def _gelu(x):
    return 0.5 * x * (1.0 + jnp.tanh(GC0 * (x + GC1 * x * x * x)))


def _gelu_grad(x):
    t = jnp.tanh(GC0 * (x + GC1 * x * x * x))
    return 0.5 * (1.0 + t) + 0.5 * x * (1.0 - t * t) * GC0 * (1.0 + 3.0 * GC1 * x * x)


def _rope(xb, c, s1, s2):
    return xb * c + pltpu.roll(xb, 112, 1) * s1 + pltpu.roll(xb, 16, 1) * s2


def _rope_bwd(dy, c, s1, s2):
    return dy * c + pltpu.roll(dy * s1, 16, 1) + pltpu.roll(dy * s2, 112, 1)


def _group_masks(shape):
    lane = lax.broadcasted_iota(jnp.int32, shape, 1)
    return [(lane >= 64 * g) & (lane < 64 * g + 64) for g in range(shape[1] // 64)]


def _group_mean(v, masks):
    out = jnp.zeros_like(v)
    for m in masks:
        s = jnp.sum(jnp.where(m, v, 0.0), axis=-1, keepdims=True) * (1.0 / 64.0)
        out = jnp.where(m, s, out)
    return out


def _pick_row(blk, idx):
    row = lax.broadcasted_iota(jnp.int32, blk.shape, 0)
    return jnp.sum(jnp.where(row == idx, blk, 0.0), axis=0, keepdims=True)


def _shift_down(y, k, first_rows):
    out = pltpu.roll(y, k, 0)
    row = lax.broadcasted_iota(jnp.int32, y.shape, 0)
    for idx in range(k):
        out = jnp.where(row == idx, first_rows[idx], out)
    return out


def _shift_up(y, k, last_rows):
    n = y.shape[0]
    out = pltpu.roll(y, n - k, 0)
    row = lax.broadcasted_iota(jnp.int32, y.shape, 0)
    for idx in range(k):
        out = jnp.where(row == n - k + idx, last_rows[idx], out)
    return out


def _tril_mask():
    r = lax.broadcasted_iota(jnp.int32, (CHUNK, CHUNK), 0)
    c = lax.broadcasted_iota(jnp.int32, (CHUNK, CHUNK), 1)
    return r >= c


def _sgu_forward(zu, zv, g_ln, b_ln, wc_bf, bsp, masks, cmasks):
    u = _gelu(zu)
    vv = _gelu(zv)
    mu = _group_mean(vv, masks)
    dv = vv - mu
    rs = lax.rsqrt(_group_mean(dv * dv, masks) + EPS)
    xh = dv * rs
    vn = xh * g_ln + b_ln
    chunks = []
    for ci in range(zu.shape[0] // CHUNK):
        vc = vn[ci * CHUNK:(ci + 1) * CHUNK, :]
        acc = bsp
        for g in range(4):
            acc = acc + jnp.dot(wc_bf[g], jnp.where(cmasks[g], vc, 0.0).astype(BF16), preferred_element_type=F32)
        chunks.append(acc)
    mixed = jnp.concatenate(chunks, axis=0) if len(chunks) > 1 else chunks[0]
    return u, vv, xh, rs, vn, mixed


def _conv_forward(gc, hh, prev_gc, prev_hh, first_tile, cw):
    yv = gc * hh
    prev = jnp.where(first_tile, 0.0, prev_gc * prev_hh)
    p6, p7 = _pick_row(prev, 6), _pick_row(prev, 7)
    sh1 = _shift_down(yv, 1, [p7])
    sh2 = _shift_down(yv, 2, [p6, p7])
    conv = sh2 * cw[0:1, :] + sh1 * cw[1:2, :] + yv * cw[2:3, :]
    return yv, sh1, sh2, conv


def _acc_init(step, *refs):
    @pl.when(step == 0)
    def _():
        for r in refs:
            r[...] = jnp.zeros(r.shape, r.dtype)


def _in_proj(x, p, l):
    t = x.shape[0]
    tm = min(ROW_TILE, t)

    def body(x_ref, g_ref, wa_ref, wb_ref, h_ref, za_ref, zb_ref):
        h = _rms(x_ref[...], g_ref[...]).astype(BF16)
        h_ref[...] = h
        za_ref[...] = jnp.dot(h, wa_ref[...], preferred_element_type=F32)
        zb_ref[...] = jnp.dot(h, wb_ref[...], preferred_element_type=F32)

    return _pcall(
        body, "in_proj", (t // tm,), [x, p["mix_pre_g"], p["w_in_a"], p["w_in_b"]],
        [_row(tm, D), _lyr(l, 1, D), _lyr(_wl(p["w_in_a"], l), D, ZA), _lyr(_wl(p["w_in_b"], l), D, ZB)],
        [_row(tm, D), _row(tm, ZA), _row(tm, ZB)],
        [_sds((t, D), BF16), _sds((t, ZA), F32), _sds((t, ZB), F32)], ("parallel",))


def _mla_prep(za, p, l, tabs):
    t = za.shape[0]
    tm = min(ROW_TILE, t)

    def body(z_ref, gq_ref, gkv_ref, wuq_ref, wukv_ref, c_ref, s1_ref, s2_ref, cq_ref, ckv_ref, q_ref, k_ref, v_ref):
        z = z_ref[...]
        cq = _rms(z[:, :QR], gq_ref[...]).astype(BF16)
        ckv = _rms(z[:, QR:QR + KVR], gkv_ref[...]).astype(BF16)
        cq_ref[...] = cq
        ckv_ref[...] = ckv
        c, s1, s2 = c_ref[...], s1_ref[...], s2_ref[...]
        kr = _rope(z[:, QR + KVR:], c, s1, s2)
        q = jnp.dot(cq, wuq_ref[...], preferred_element_type=F32)
        kv = jnp.dot(ckv, wukv_ref[...], preferred_element_type=F32)
        for h in range(HEADS):
            sl = slice(128 * h, 128 * h + 128)
            q_ref[:, sl] = (_rope(q[:, sl], c, s1, s2) * QSCALE).astype(BF16)
            k_ref[:, sl] = (kv[:, sl] + kr).astype(BF16)
        v_ref[...] = kv[:, QW:].astype(BF16)

    return _pcall(
        body, "mla_prep", (t // tm,), [za, p["q_norm_g"], p["kv_norm_g"], p["w_uq"], p["w_ukv"], *tabs],
        [_row(tm, ZA), _lyr(l, 1, QR), _lyr(l, 1, KVR), _lyr(_wl(p["w_uq"], l), QR, QW), _lyr(_wl(p["w_ukv"], l), KVR, KVW),
         _row(tm, 128), _row(tm, 128), _row(tm, 128)],
        [_row(tm, QR), _row(tm, KVR), _row(tm, QW), _row(tm, QW), _row(tm, HEADS * VD)],
        [_sds((t, QR), BF16), _sds((t, KVR), BF16), _sds((t, QW), BF16), _sds((t, QW), BF16),
         _sds((t, HEADS * VD), BF16)], ("parallel",))


def _att_tile(t):
    return min(ATT_TILE, max(t // 2, 128))


def _causal_keep(tq, i, j):
    row = lax.broadcasted_iota(jnp.int32, (tq, tq), 0) + i * tq
    col = lax.broadcasted_iota(jnp.int32, (tq, tq), 1) + j * tq
    return col <= row


def _attn_fwd(qs, k, v, fetch=None):
    t = qs.shape[0]
    tq = _att_tile(t)
    nq = t // tq
    rep = tq // 128
    groups = HEADS // ATT_HEADS
    mine, bufs, fetch_layer = fetch if fetch else ((), (), None)
    nb = len(mine)

    steps = [(i, j) for i in range(nq) for j in range(i + 1)]
    i_of = jnp.asarray([s[0] for s in steps], jnp.int32)
    j_of = jnp.asarray([s[1] for s in steps], jnp.int32)

    def body(i_ref, j_ref, q_ref, k_ref, v_ref, *refs):
        o_ref, lse_ref = refs[2 * nb:2 * nb + 2]
        m_s, l_s, acc_s = refs[3 * nb + 2:3 * nb + 5]
        step_no = pl.program_id(1)
        i, j = i_ref[step_no], j_ref[step_no]
        if fetch:
            start, hand_over, drain = _gather_ops(refs[:nb], refs[2 * nb + 2:3 * nb + 2], refs[3 * nb + 5:], fetch_layer)
            pr = pl.program_id(0)
            pl.when((pr == 0) & (step_no == 0))(start)
            pl.when((pr == groups - 1) & (step_no == 3 * len(steps) // 4))(hand_over)
            pl.when((pr == groups - 1) & (step_no == len(steps) - 1))(drain)

        @pl.when(j == 0)
        def _():
            m_s[...] = jnp.full(m_s.shape, NEG, F32)
            l_s[...] = jnp.zeros(l_s.shape, F32)
            acc_s[...] = jnp.zeros(acc_s.shape, F32)

        def step(masked):
            keep = _causal_keep(tq, i, j) if masked else None
            for hh in range(ATT_HEADS):
                sl = slice(128 * hh, 128 * hh + 128)
                vv = v_ref[:, 128 * (hh // 2):128 * (hh // 2) + 128]
                s = lax.dot_general(q_ref[:, sl], k_ref[:, sl], NT, preferred_element_type=F32)
                if masked:
                    s = jnp.where(keep, s, NEG)
                m_old = m_s[hh]
                m_new = jnp.maximum(m_old, jnp.max(s, axis=-1, keepdims=True))
                alpha = jnp.exp2(m_old - m_new)
                p = jnp.exp2(s - jnp.tile(m_new, (1, rep)))
                l_s[hh] = alpha * l_s[hh] + jnp.sum(p, axis=-1, keepdims=True)
                acc_s[hh] = alpha * acc_s[hh] + jnp.dot(p.astype(BF16), vv, preferred_element_type=F32)
                m_s[hh] = m_new

        @pl.when(j < i)
        def _():
            step(False)

        @pl.when(j == i)
        def _():
            step(True)
            lane = lax.broadcasted_iota(jnp.int32, (tq, 128), 1)
            for pp in range(ATT_HEADS // 2):
                a, b = 2 * pp, 2 * pp + 1
                o_ref[:, 128 * pp:128 * pp + 128] = jnp.where(lane < VD, acc_s[a] / l_s[a], acc_s[b] / l_s[b])
            for hh in range(ATT_HEADS):
                lse_ref[hh] = (m_s[hh] + jnp.log2(l_s[hh]))[:, 0:1]

    qw, vw = 128 * ATT_HEADS, VD * ATT_HEADS
    stat = pltpu.VMEM((ATT_HEADS, tq, 128), F32)
    grid_spec = pltpu.PrefetchScalarGridSpec(
        num_scalar_prefetch=2, grid=(groups, len(steps)),
        in_specs=[pl.BlockSpec((tq, qw), lambda p, s, it, jt: (it[s], p)),
                  pl.BlockSpec((tq, qw), lambda p, s, it, jt: (jt[s], p)),
                  pl.BlockSpec((tq, vw), lambda p, s, it, jt: (jt[s], p))] + [ANY] * (2 * nb),
        out_specs=[pl.BlockSpec((tq, vw), lambda p, s, it, jt: (it[s], p)),
                   pl.BlockSpec((ATT_HEADS, tq, 1), lambda p, s, it, jt: (p, it[s], 0))] + [ANY] * nb,
        scratch_shapes=[stat, stat, stat] + ([pltpu.SemaphoreType.DMA((nb, 3))] * 4 if fetch else []))
    outs = pl.pallas_call(
        body, name="attn_fwd_fetch" if fetch else "attn_fwd", grid_spec=grid_spec,
        out_shape=[_sds((t, HEADS * VD), F32), _sds((HEADS, t, 1), F32)] + [_sds(b.shape, b.dtype) for b in bufs],
        input_output_aliases={5 + nb + b: 2 + b for b in range(nb)},
        compiler_params=_cp("arbitrary", "arbitrary"))(i_of, j_of, qs, k, v, *mine, *bufs)
    return outs[0], outs[1], list(outs[2:])


def _mixer_fwd(zb, ya, p, l):
    t = zb.shape[0]
    tm = min(ROW_TILE, t)
    hb = tm // 8

    def body(zb_ref, zprev_ref, ya_ref, gln_ref, bln_ref, wsp_ref, bsp_ref, cw_ref, go_ref, mix_ref, yb_ref, yc_ref):
        i = pl.program_id(0)
        masks = _group_masks((tm, SGW))
        cmasks = _group_masks((CHUNK, SGW))
        tril = _tril_mask()
        wc_bf = [jnp.where(tril, wsp_ref[g], 0.0).astype(BF16) for g in range(4)]
        u, _, _, _, _, mixed = _sgu_forward(zb_ref[:, 0:256], zb_ref[:, 256:512], gln_ref[...], bln_ref[...], wc_bf,
                                            bsp_ref[...], masks, cmasks)
        yb = u * mixed
        _, _, _, conv = _conv_forward(zb_ref[:, 768:1024], zb_ref[:, 1024:1280], zprev_ref[:, 768:1024],
                                      zprev_ref[:, 1024:1280], i == 0, cw_ref[...])
        yc = zb_ref[:, 512:768] * conv
        yb_ref[...] = yb
        yc_ref[...] = yc
        go = go_ref[...]
        mix_ref[:, 0:512] = _rms(ya_ref[...], go[:, 0:512]).astype(BF16)
        mix_ref[:, 512:768] = _rms(yb, go[:, 512:768]).astype(BF16)
        mix_ref[:, 768:1024] = _rms(yc, go[:, 768:1024]).astype(BF16)

    return _pcall(
        body, "mixer_fwd", (t // tm,),
        [zb, zb, ya, p["sg_ln_g"], p["sg_ln_b"], p["w_sp"], p["b_sp"], p["conv_w"], p["out_norm_g"]],
        [_row(tm, ZB), pl.BlockSpec((8, ZB), lambda i: (jnp.maximum(i * hb - 1, 0), 0)), _row(tm, 512),
         _lyr(l, 1, SGW), _lyr(l, 1, SGW), _lyr(l, 4, CHUNK, CHUNK), _lyr(l, CHUNK, SGW), _lyr(_wl(p["conv_w"], l), 3, CVW), _lyr(l, 1, D)],
        [_row(tm, D), _row(tm, SGW), _row(tm, CVW)],
        [_sds((t, D), BF16), _sds((t, SGW), F32), _sds((t, CVW), F32)], ("parallel",))


def _out_proj(mix, x, p, l):
    t = x.shape[0]
    tm = min(ROW_TILE, t)

    def body(mix_ref, w_ref, x_ref, gp_ref, gf_ref, o_ref, x2_ref, h2_ref):
        o = jnp.dot(mix_ref[...], w_ref[...], preferred_element_type=F32)
        o_ref[...] = o
        x2 = x_ref[...] + _rms(o, gp_ref[...])
        x2_ref[...] = x2
        h2_ref[...] = _rms(x2, gf_ref[...]).astype(BF16)

    return _pcall(
        body, "out_proj", (t // tm,), [mix, p["w_out"], x, p["mix_post_g"], p["ffn_pre_g"]],
        [_row(tm, D), _lyr(_wl(p["w_out"], l), D, D), _row(tm, D), _lyr(l, 1, D), _lyr(l, 1, D)],
        [_row(tm, D), _row(tm, D), _row(tm, D)],
        [_sds((t, D), F32), _sds((t, D), F32), _sds((t, D), BF16)], ("parallel",))


def _gu_all(l, which):
    return pl.BlockSpec((4, None, None, HP, D), lambda *_: (0, l, which, 0, 0))


def _down_all(l):
    return pl.BlockSpec((4, None, HP, D), lambda *_: (0, l, 0, 0))


def _ffn_up(h2, p, l):
    t = h2.shape[0]
    tm = min(ROW_TILE, t)

    def body(h_ref, wg_ref, wu_ref, a_ref, b_ref, s_ref):
        h = h_ref[...]
        a = lax.dot_general(h, wg_ref[...], NT, preferred_element_type=F32)
        b = lax.dot_general(h, wu_ref[...], NT, preferred_element_type=F32)
        a_ref[...] = a.astype(BF16)
        b_ref[...] = b.astype(BF16)
        s_ref[...] = (a * _sigmoid(a) * b).astype(BF16)

    blk = pl.BlockSpec((tm, HP), lambda k, i: (i, k))
    wblk = lambda which: pl.BlockSpec((None, None, None, HP, D), lambda k, i: (k, l, which, 0, 0))
    return _pcall(
        body, "ffn_up", (4, t // tm), [h2, p["w_gu"], p["w_gu"]],
        [pl.BlockSpec((tm, D), lambda k, i: (i, 0)), wblk(0), wblk(1)], [blk, blk, blk],
        [_sds((t, DFFP), BF16)] * 3, ("parallel", "parallel"))


def _ffn_down(s, x2, p, l):
    t = x2.shape[0]
    tm = min(ROW_TILE, t)

    def body(s_ref, w_ref, x_ref, g_ref, f_ref, x3_ref):
        f = jnp.dot(s_ref[:, 0:HP], w_ref[0], preferred_element_type=F32)
        for k in range(1, 4):
            f = f + jnp.dot(s_ref[:, k * HP:(k + 1) * HP], w_ref[k], preferred_element_type=F32)
        f_ref[...] = f
        x3_ref[...] = x_ref[...] + _rms(f, g_ref[...])

    return _pcall(
        body, "ffn_down", (t // tm,), [s, p["w_down"], x2, p["ffn_post_g"]],
        [_row(tm, DFFP), _down_all(l), _row(tm, D), _lyr(l, 1, D)], [_row(tm, D), _row(tm, D)],
        [_sds((t, D), F32), _sds((t, D), F32)], ("parallel",))


def _loss_head(y, target):
    t = y.shape[0]
    tm = min(ROW_TILE, t)

    def body(y_ref, t_ref, dy_ref, acc_ref):
        e = y_ref[...] - t_ref[...]
        dy_ref[...] = e * (1.0 / D)
        sq = jnp.sum(e * e, axis=0, keepdims=True)
        part = sq[:, 0:128]
        for b in range(1, D // 128):
            part = part + sq[:, 128 * b:128 * b + 128]
        _acc_init(pl.program_id(0), acc_ref)
        acc_ref[...] += part

    return _pcall(body, "loss_head", (t // tm,), [y, target], [_row(tm, D), _row(tm, D)],
                  [_row(tm, D), pl.BlockSpec((1, 128), lambda i: (0, 0))],
                  [_sds((t, D), F32), _sds((1, 128), F32)], ("arbitrary",))


def _ffn_down_bwd(dx3, sv, p, l, depth, gb):
    t = dx3.shape[0]
    tm = min(256, t)

    def body(dx_ref, f_ref, g_ref, w_ref, a_ref, b_ref, df_ref, da_ref, db_ref, dg_ref):
        _acc_init(pl.program_id(0), dg_ref)
        df, dg = _rms_bwd(f_ref[...], g_ref[...], dx_ref[...])
        dg_ref[...] += dg
        df = df.astype(BF16)
        df_ref[...] = df
        for k in range(4):
            sl = slice(k * HP, (k + 1) * HP)
            ds = lax.dot_general(df, w_ref[k], NT, preferred_element_type=F32)
            av = a_ref[:, sl].astype(F32)
            sig = _sigmoid(av)
            da_ref[:, sl] = (ds * b_ref[:, sl].astype(F32) * (sig * (1.0 + av * (1.0 - sig)))).astype(BF16)
            db_ref[:, sl] = (ds * (av * sig)).astype(BF16)

    df, da, db, gb["ffn_post_g"] = _pcall(
        body, "ffn_down_bwd", (t // tm,), [dx3, sv["f"], p["ffn_post_g"], p["w_down"], sv["a"], sv["b"]],
        [_row(tm, D), _row(tm, D), _lyr(l, 1, D), _down_all(l), _row(tm, DFFP), _row(tm, DFFP)],
        [_row(tm, D), _row(tm, DFFP), _row(tm, DFFP), _lyr(l.g, 1, D)],
        [_sds((t, D), BF16), _sds((t, DFFP), BF16), _sds((t, DFFP), BF16), _sds((depth, 1, D), F32)], ("arbitrary",),
        prevs={3: gb.get("ffn_post_g")})
    return df, da, db


def _ffn_up_bwd(da, db, dx3, sv, p, l, depth, gb):
    t = dx3.shape[0]
    tm = min(256, t)

    def body(da_ref, db_ref, wg_ref, wu_ref, x_ref, dx3_ref, g_ref, dx2_ref, dg_ref):
        _acc_init(pl.program_id(0), dg_ref)
        dh = jnp.zeros((tm, D), F32)
        for k in range(4):
            sl = slice(k * HP, (k + 1) * HP)
            dh = dh + jnp.dot(da_ref[:, sl], wg_ref[k], preferred_element_type=F32)
            dh = dh + jnp.dot(db_ref[:, sl], wu_ref[k], preferred_element_type=F32)
        dx, dg = _rms_bwd(x_ref[...], g_ref[...], dh)
        dg_ref[...] += dg
        dx2_ref[...] = dx3_ref[...] + dx

    dx2, gb["ffn_pre_g"] = _pcall(
        body, "ffn_up_bwd", (t // tm,), [da, db, p["w_gu"], p["w_gu"], sv["x2"], dx3, p["ffn_pre_g"]],
        [_row(tm, DFFP), _row(tm, DFFP), _gu_all(l, 0), _gu_all(l, 1), _row(tm, D), _row(tm, D), _lyr(l, 1, D)],
        [_row(tm, D), _lyr(l.g, 1, D)], [_sds((t, D), F32), _sds((depth, 1, D), F32)], ("arbitrary",),
        prevs={1: gb.get("ffn_pre_g")})
    return dx2


def _out_proj_bwd(dx2, sv, p, l, depth, gb):
    t = dx2.shape[0]
    tm = min(ROW_TILE, t)

    def body(dx_ref, o_ref, g_ref, w_ref, do_ref, dmix_ref, dg_ref):
        _acc_init(pl.program_id(0), dg_ref)
        do, dg = _rms_bwd(o_ref[...], g_ref[...], dx_ref[...])
        dg_ref[...] += dg
        do = do.astype(BF16)
        do_ref[...] = do
        dmix_ref[...] = lax.dot_general(do, w_ref[...], NT, preferred_element_type=F32)

    do, dmix, gb["mix_post_g"] = _pcall(
        body, "out_proj_bwd", (t // tm,), [dx2, sv["o"], p["mix_post_g"], p["w_out"]],
        [_row(tm, D), _row(tm, D), _lyr(l, 1, D), _lyr(_wl(p["w_out"], l), D, D)], [_row(tm, D), _row(tm, D), _lyr(l.g, 1, D)],
        [_sds((t, D), BF16), _sds((t, D), F32), _sds((depth, 1, D), F32)], ("arbitrary",),
        prevs={2: gb.get("mix_post_g")})
    return do, dmix


def _mixer_bwd(dmix, sv, p, l, depth, gb):
    zb = sv["zb"]
    t = zb.shape[0]
    tm = min(ROW_TILE, t)
    hb = tm // 8
    last_blk = t // 8 - 1
    nsteps = t // tm

    def body(dmix_ref, ya_ref, yb_ref, yc_ref, zb_ref, zprev_ref, znext_ref, ycn_ref, dmn_ref,
             gln_ref, bln_ref, wsp_ref, bsp_ref, cw_ref, go_ref,
             dya_ref, dzb_ref, delta_ref, dgo_ref, dgln_ref, dbln_ref, dwsp_ref, dbsp_ref, dcw_ref):
        i = pl.program_id(0)
        _acc_init(i, dgo_ref, dgln_ref, dbln_ref, dwsp_ref, dbsp_ref, dcw_ref)
        go = go_ref[...]
        dmix = dmix_ref[...]

        ya = ya_ref[...]
        dya, dga = _rms_bwd(ya, go[:, 0:512], dmix[:, 0:512])
        dyb, dgb_ = _rms_bwd(yb_ref[...], go[:, 512:768], dmix[:, 512:768])
        dyc, dgc_ = _rms_bwd(yc_ref[...], go[:, 768:1024], dmix[:, 768:1024])
        dgo_ref[:, 0:512] += dga
        dgo_ref[:, 512:768] += dgb_
        dgo_ref[:, 768:1024] += dgc_
        dya = dya * LN2
        dya_ref[...] = dya.astype(BF16)
        prod = dya * ya
        hmasks = _group_masks((tm, 512))
        for h in range(HEADS):
            delta_ref[h] = jnp.sum(jnp.where(hmasks[h], prod, 0.0), axis=-1, keepdims=True)

        masks = _group_masks((tm, SGW))
        cmasks = _group_masks((CHUNK, SGW))
        tril = _tril_mask()
        wc_bf = [jnp.where(tril, wsp_ref[g], 0.0).astype(BF16) for g in range(4)]
        zu, zv = zb_ref[:, 0:256], zb_ref[:, 256:512]
        g_ln = gln_ref[...]
        u, _, xh, rs, vn, mixed = _sgu_forward(zu, zv, g_ln, bln_ref[...], wc_bf, bsp_ref[...], masks, cmasks)
        du = dyb * mixed
        dmixed = dyb * u
        dvn_chunks = []
        dbsp = jnp.zeros((CHUNK, SGW), F32)
        for ci in range(tm // CHUNK):
            rows = slice(ci * CHUNK, (ci + 1) * CHUNK)
            dm_c = dmixed[rows, :]
            vn_c = vn[rows, :].astype(BF16)
            dbsp = dbsp + dm_c
            dvn_c = jnp.zeros((CHUNK, SGW), F32)
            for g in range(4):
                dm_g = jnp.where(cmasks[g], dm_c, 0.0).astype(BF16)
                dw = lax.dot_general(dm_g, vn_c, NT, preferred_element_type=F32)
                dwsp_ref[g] += jnp.where(tril, dw, 0.0)
                dvn_c = dvn_c + lax.dot_general(wc_bf[g], dm_g, TN, preferred_element_type=F32)
            dvn_chunks.append(dvn_c)
        dbsp_ref[...] += dbsp
        dvn = jnp.concatenate(dvn_chunks, axis=0) if len(dvn_chunks) > 1 else dvn_chunks[0]
        dgln_ref[...] += jnp.sum(dvn * xh, axis=0, keepdims=True)
        dbln_ref[...] += jnp.sum(dvn, axis=0, keepdims=True)
        dxh = dvn * g_ln
        dvv = rs * (dxh - _group_mean(dxh, masks) - xh * _group_mean(dxh * xh, masks))
        dzb_ref[:, 0:256] = (du * _gelu_grad(zu)).astype(BF16)
        dzb_ref[:, 256:512] = (dvv * _gelu_grad(zv)).astype(BF16)

        cwv = cw_ref[...]
        gb_, gc, hh = zb_ref[:, 512:768], zb_ref[:, 768:1024], zb_ref[:, 1024:1280]
        yv, sh1, sh2, conv = _conv_forward(gc, hh, zprev_ref[:, 768:1024], zprev_ref[:, 1024:1280], i == 0, cwv)
        dconv = dyc * gb_
        dzb_ref[:, 512:768] = (dyc * conv).astype(BF16)
        dcw_ref[0:1, :] += jnp.sum(dconv * sh2, axis=0, keepdims=True)
        dcw_ref[1:2, :] += jnp.sum(dconv * sh1, axis=0, keepdims=True)
        dcw_ref[2:3, :] += jnp.sum(dconv * yv, axis=0, keepdims=True)
        dycn, _ = _rms_bwd(ycn_ref[...], go[:, 768:1024], dmn_ref[...])
        dconv_next = jnp.where(i == nsteps - 1, 0.0, dycn * znext_ref[:, 512:768])
        n0, n1 = _pick_row(dconv_next, 0), _pick_row(dconv_next, 1)
        dyv = dconv * cwv[2:3, :] + _shift_up(dconv, 1, [n0]) * cwv[1:2, :] + _shift_up(dconv, 2, [n0, n1]) * cwv[0:1, :]
        dzb_ref[:, 768:1024] = (dyv * hh).astype(BF16)
        dzb_ref[:, 1024:1280] = (dyv * gc).astype(BF16)

    prev_map = lambda i: (jnp.maximum(i * hb - 1, 0), 0)
    next_map = lambda i: (jnp.minimum((i + 1) * hb, last_blk), 0)
    names = ("out_norm_g", "sg_ln_g", "sg_ln_b", "w_sp", "b_sp_t", "conv_w")
    shapes = ((1, D), (1, SGW), (1, SGW), (4, CHUNK, CHUNK), (CHUNK, SGW), (3, CVW))
    outs = _pcall(
        body, "mixer_bwd", (nsteps,),
        [dmix, sv["ya"], sv["yb"], sv["yc"], zb, zb, zb, sv["yc"], dmix, p["sg_ln_g"], p["sg_ln_b"], p["w_sp"], p["b_sp"],
         p["conv_w"], p["out_norm_g"]],
        [_row(tm, D), _row(tm, 512), _row(tm, SGW), _row(tm, CVW), _row(tm, ZB),
         pl.BlockSpec((8, ZB), prev_map), pl.BlockSpec((8, ZB), next_map), pl.BlockSpec((8, CVW), next_map),
         pl.BlockSpec((8, 256), lambda i: (jnp.minimum((i + 1) * hb, last_blk), 3)),
         _lyr(l, 1, SGW), _lyr(l, 1, SGW), _lyr(l, 4, CHUNK, CHUNK), _lyr(l, CHUNK, SGW), _lyr(_wl(p["conv_w"], l), 3, CVW), _lyr(l, 1, D)],
        [_row(tm, 512), _row(tm, ZB), pl.BlockSpec((HEADS, tm, 1), lambda i: (0, i, 0))] + [_lyr(l.g, *s) for s in shapes],
        [_sds((t, 512), BF16), _sds((t, ZB), BF16), _sds((HEADS, t, 1), F32)] + [_sds((depth,) + s, F32) for s in shapes],
        ("arbitrary",), prevs={3 + n: gb.get(name) for n, name in enumerate(names)})
    for n, name in enumerate(names):
        gb[name] = outs[3 + n]
    return outs[0], outs[1], outs[2]


def _attn_bwd(qs, k, v, dya, lse, delta, ride=()):
    t = qs.shape[0]
    tq = _att_tile(t)
    nq = t // tq
    nb = len(ride)
    groups = HEADS // ATT_HEADS

    steps = [(j, i) for j in range(nq) for i in range(j, nq)]
    j_of = jnp.asarray([s[0] for s in steps], jnp.int32)
    i_of = jnp.asarray([s[1] for s in steps], jnp.int32)

    def body(j_ref, i_ref, q_ref, k_ref, v_ref, do_ref, lse_ref, dl_ref, *refs):
        dq_ref, dk_ref, dv_ref = refs[nb:nb + 3]
        dq_s, dk_s, dv_s = refs[2 * nb + 3:2 * nb + 6]
        step_no = pl.program_id(1)
        j, i = j_ref[step_no], i_ref[step_no]
        if ride:
            start, finish = _exchange_ops(refs[:nb], refs[nb + 3:2 * nb + 3], refs[2 * nb + 6:])
            pr = pl.program_id(0)
            pl.when((pr == 0) & (step_no == 0))(start)
            pl.when((pr == groups - 1) & (step_no == len(steps) - 1))(finish)

        @pl.when(step_no == 0)
        def _():
            dq_s[...] = jnp.zeros(dq_s.shape, F32)

        def step(masked):
            keep = _causal_keep(tq, 0, 0) if masked else None
            lane = lax.broadcasted_iota(jnp.int32, (tq, 128), 1)
            rows = pl.ds(pl.multiple_of(i * tq, tq), tq)
            for hh in range(ATT_HEADS):
                sl = slice(128 * hh, 128 * hh + 128)
                pair = slice(128 * (hh // 2), 128 * (hh // 2) + 128)
                vv, do = v_ref[:, pair], do_ref[:, pair]
                qq, kk = q_ref[:, sl], k_ref[:, sl]
                s = lax.dot_general(qq, kk, NT, preferred_element_type=F32)
                p = jnp.exp2(s - lse_ref[hh])
                if masked:
                    p = jnp.where(keep, p, 0.0)
                do_h = jnp.where((lane < VD) if hh % 2 == 0 else (lane >= VD), do, jnp.zeros_like(do))
                dp = lax.dot_general(do_h, vv, NT, preferred_element_type=F32)
                ds = (p * (dp - dl_ref[hh])).astype(BF16)
                dv_s[:, pair] += lax.dot_general(p.astype(BF16), do_h, TN, preferred_element_type=F32)
                dk_s[:, sl] += lax.dot_general(ds, qq, TN, preferred_element_type=F32)
                dq_s[rows, sl] += jnp.dot(ds, kk, preferred_element_type=F32)

        @pl.when(i == j)
        def _():
            dk_s[...] = jnp.zeros(dk_s.shape, F32)
            dv_s[...] = jnp.zeros(dv_s.shape, F32)
            step(True)

        @pl.when(i > j)
        def _():
            step(False)

        @pl.when(i == nq - 1)
        def _():
            dk_ref[...] = dk_s[...].astype(BF16)
            dv_ref[...] = (dv_s[...] * LOG2E).astype(BF16)

        @pl.when(step_no == len(steps) - 1)
        def _():
            dq_ref[...] = dq_s[...].astype(BF16)

    qw, vw = 128 * ATT_HEADS, VD * ATT_HEADS
    qrow = lambda p, s, jt, it: (it[s], p)
    krow = lambda p, s, jt, it: (jt[s], p)
    col_spec = pl.BlockSpec((ATT_HEADS, tq, 1), lambda p, s, jt, it: (p, it[s], 0))
    grid_spec = pltpu.PrefetchScalarGridSpec(
        num_scalar_prefetch=2, grid=(groups, len(steps)),
        in_specs=[pl.BlockSpec((tq, qw), qrow), pl.BlockSpec((tq, qw), krow), pl.BlockSpec((tq, vw), krow),
                  pl.BlockSpec((tq, vw), qrow), col_spec, col_spec] + [ANY] * nb,
        out_specs=[pl.BlockSpec((t, qw), lambda p, s, jt, it: (0, p)), pl.BlockSpec((tq, qw), krow),
                   pl.BlockSpec((tq, vw), krow)] + [ANY] * nb,
        scratch_shapes=[pltpu.VMEM((t, qw), F32), pltpu.VMEM((tq, qw), F32), pltpu.VMEM((tq, vw), F32)]
        + ([pltpu.SemaphoreType.DMA((nb, 3))] * 2 if ride else []))
    outs = pl.pallas_call(
        body, name="attn_bwd_ride" if ride else "attn_bwd", grid_spec=grid_spec,
        out_shape=[_sds((t, QW), BF16), _sds((t, QW), BF16), _sds((t, HEADS * VD), BF16)]
        + [_sds((3,) + a.shape[1:], a.dtype) for a in ride],
        compiler_params=_cp("arbitrary", "arbitrary"))(j_of, i_of, qs, k, v, dya, lse, delta, *ride)
    return outs[0], outs[1], outs[2], list(outs[3:])


def _mla_prep_bwd(dqs, dk, dv, sv, p, l, depth, gb, tabs):
    za = sv["za"]
    t = za.shape[0]
    tm = min(ROW_TILE, t)

    def body(dq_ref, dk_ref, dv_ref, z_ref, gq_ref, gkv_ref, wuq_ref, wukv_ref, c_ref, s1_ref, s2_ref,
             dza_ref, dqp_ref, dkv_ref, dgq_ref, dgkv_ref):
        _acc_init(pl.program_id(0), dgq_ref, dgkv_ref)
        c, s1, s2 = c_ref[...], s1_ref[...], s2_ref[...]
        lane = lax.broadcasted_iota(jnp.int32, (tm, 128), 1)
        rope_lanes = (lane >= NOPE) & (lane < NOPE + ROPE)
        dkr = jnp.zeros((tm, 128), F32)
        for h in range(HEADS):
            sl = slice(128 * h, 128 * h + 128)
            dqp_ref[:, sl] = _rope_bwd(dq_ref[:, sl].astype(F32) * QSCALE, c, s1, s2).astype(BF16)
            dkh = dk_ref[:, sl]
            dkv_ref[:, sl] = dkh
            dkr = dkr + jnp.where(rope_lanes, dkh.astype(F32), 0.0)
        dkv_ref[:, QW:] = dv_ref[...]
        z = z_ref[...]
        dcq = lax.dot_general(dqp_ref[...], wuq_ref[...], NT, preferred_element_type=F32)
        dzq, dgq = _rms_bwd(z[:, :QR], gq_ref[...], dcq)
        dckv = lax.dot_general(dkv_ref[...], wukv_ref[...], NT, preferred_element_type=F32)
        dzkv, dgkv = _rms_bwd(z[:, QR:QR + KVR], gkv_ref[...], dckv)
        dgq_ref[...] += dgq
        dgkv_ref[...] += dgkv
        dza_ref[:, :QR] = dzq.astype(BF16)
        dza_ref[:, QR:QR + KVR] = dzkv.astype(BF16)
        dza_ref[:, QR + KVR:] = _rope_bwd(dkr, c, s1, s2).astype(BF16)

    dza, dqp, dkv, gb["q_norm_g"], gb["kv_norm_g"] = _pcall(
        body, "mla_prep_bwd", (t // tm,),
        [dqs, dk, dv, za, p["q_norm_g"], p["kv_norm_g"], p["w_uq"], p["w_ukv"], *tabs],
        [_row(tm, QW), _row(tm, QW), _row(tm, HEADS * VD), _row(tm, ZA), _lyr(l, 1, QR), _lyr(l, 1, KVR),
         _lyr(_wl(p["w_uq"], l), QR, QW), _lyr(_wl(p["w_ukv"], l), KVR, KVW), _row(tm, 128), _row(tm, 128), _row(tm, 128)],
        [_row(tm, ZA), _row(tm, QW), _row(tm, KVW), _lyr(l.g, 1, QR), _lyr(l.g, 1, KVR)],
        [_sds((t, ZA), BF16), _sds((t, QW), BF16), _sds((t, KVW), BF16), _sds((depth, 1, QR), F32),
         _sds((depth, 1, KVR), F32)], ("arbitrary",), prevs={3: gb.get("q_norm_g"), 4: gb.get("kv_norm_g")})
    return dza, dqp, dkv


def _in_proj_bwd(dza, dzb, dx2, sv, p, l, depth, gb):
    t = dx2.shape[0]
    tm = min(ROW_TILE, t)

    def body(dza_ref, dzb_ref, wa_ref, wb_ref, x_ref, dx2_ref, g_ref, dx_ref, dg_ref):
        _acc_init(pl.program_id(0), dg_ref)
        dh = (lax.dot_general(dza_ref[...], wa_ref[...], NT, preferred_element_type=F32)
              + lax.dot_general(dzb_ref[...], wb_ref[...], NT, preferred_element_type=F32))
        dx, dg = _rms_bwd(x_ref[...], g_ref[...], dh)
        dg_ref[...] += dg
        dx_ref[...] = dx2_ref[...] + dx

    dx, gb["mix_pre_g"] = _pcall(
        body, "in_proj_bwd", (t // tm,), [dza, dzb, p["w_in_a"], p["w_in_b"], sv["x"], dx2, p["mix_pre_g"]],
        [_row(tm, ZA), _row(tm, ZB), _lyr(_wl(p["w_in_a"], l), D, ZA), _lyr(_wl(p["w_in_b"], l), D, ZB), _row(tm, D), _row(tm, D), _lyr(l, 1, D)],
        [_row(tm, D), _lyr(l.g, 1, D)], [_sds((t, D), F32), _sds((depth, 1, D), F32)], ("arbitrary",),
        prevs={1: gb.get("mix_pre_g")})
    return dx


def _mm_tn(a, b, tn, name, l, depth, gb):
    t, k = a.shape
    n = b.shape[1]
    tt = min(ROW_TILE, t)

    def body(a_ref, b_ref, o_ref):
        _acc_init(pl.program_id(1), o_ref)
        o_ref[...] += lax.dot_general(a_ref[...], b_ref[...], TN, preferred_element_type=F32)

    gb[name] = _pcall(
        body, "d" + name, (n // tn, t // tt), [a, b],
        [pl.BlockSpec((tt, k), lambda j, s: (s, 0)), pl.BlockSpec((tt, tn), lambda j, s: (s, j))],
        pl.BlockSpec((None, k, tn), lambda j, s: (l.g, 0, j)), _sds((depth, k, n), F32), ("parallel", "arbitrary"),
        prevs={0: gb.get(name)})


def _dw_ffn(a, b, kind, l, depth, gb):
    t = a.shape[0]
    tt = min(ROW_TILE, t)
    nsteps = t // tt

    def body(a_ref, b_ref, o_ref, acc):
        s = pl.program_id(0)
        _acc_init(s, acc)
        acc[...] += lax.dot_general(a_ref[...], b_ref[...], TN, preferred_element_type=F32)

        @pl.when(s == nsteps - 1)
        def _():
            for k in range(4):
                o_ref[k] = acc[k * HP:(k + 1) * HP, :].astype(BF16)

    rows = lambda n: pl.BlockSpec((tt, n), lambda s: (s, 0))
    if kind == "down":
        name = "down"
        out_spec = pl.BlockSpec((4, None, HP, D), lambda s: (0, l.g, 0, 0))
        out_shape = _sds((4, depth, HP, D), BF16)
    else:
        which = 0 if kind == "gate" else 1
        name = "gu"
        out_spec = pl.BlockSpec((4, None, None, HP, D), lambda s: (0, l.g, which, 0, 0))
        out_shape = _sds((4, depth, 2, HP, D), BF16)
    gb[name] = _pcall(body, "dw_" + kind, (nsteps,), [a, b], [rows(DFFP), rows(D)], out_spec, out_shape, ("arbitrary",),
                      scratch=[pltpu.VMEM((DFFP, D), F32)], prevs={0: gb.get(name)})


def _ffn_views(bufs):
    return {"w_gu": bufs[1], "w_down": bufs[2].reshape(bufs[2].shape[:2] + (HP, D))}


def _layer_fwd(x, p, l, tabs, fetch):
    h1, za, zb = _in_proj(x, p, l)
    cqn, ckvn, qs, k, v = _mla_prep(za, p, l, tabs)
    ya, lse, bufs = _attn_fwd(qs, k, v, fetch)
    if fetch:
        p = {**p, **_ffn_views(bufs)}
    mix, yb, yc = _mixer_fwd(zb, ya, p, l)
    o, x2, h2 = _out_proj(mix, x, p, l)
    a, b, s = _ffn_up(h2, p, l)
    f, x3 = _ffn_down(s, x2, p, l)
    saved = dict(x=x, h1=h1, za=za, zb=zb, cqn=cqn, ckvn=ckvn, qs=qs, k=k, v=v, ya=ya, lse=lse, mix=mix, yb=yb, yc=yc,
                 o=o, x2=x2, h2=h2, a=a, b=b, s=s, f=f)
    return x3, saved, bufs if fetch else None


class _Layer(int):
    def __new__(cls, l, g):
        obj = int.__new__(cls, l)
        obj.g = g
        return obj


def _layer_bwd(dx3, p, sv, l, depth, gb, tabs, ride=()):
    df, da, db = _ffn_down_bwd(dx3, sv, p, l, depth, gb)
    _dw_ffn(sv["s"], df, "down", l, depth, gb)
    dx2 = _ffn_up_bwd(da, db, dx3, sv, p, l, depth, gb)
    _dw_ffn(da, sv["h2"], "gate", l, depth, gb)
    _dw_ffn(db, sv["h2"], "up", l, depth, gb)
    do, dmix = _out_proj_bwd(dx2, sv, p, l, depth, gb)
    _mm_tn(sv["mix"], do, D, "w_out", l, depth, gb)
    dya, dzb, delta = _mixer_bwd(dmix, sv, p, l, depth, gb)
    dqs, dk, dv, sent = _attn_bwd(sv["qs"], sv["k"], sv["v"], dya, sv["lse"], delta, ride)
    dza, dqp, dkv = _mla_prep_bwd(dqs, dk, dv, sv, p, l, depth, gb, tabs)
    _mm_tn(sv["cqn"], dqp, QW, "w_uq", l, depth, gb)
    _mm_tn(sv["ckvn"], dkv, KVW, "w_ukv", l, depth, gb)
    _mm_tn(sv["h1"], dza, ZA, "w_in_a", l, depth, gb)
    _mm_tn(sv["h1"], dzb, ZB, "w_in_b", l, depth, gb)
    return _in_proj_bwd(dza, dzb, dx2, sv, p, l, depth, gb), sent


def _rope_tables(positions):
    inv_freq = 1.0 / (ROPE_THETA ** (jnp.arange(0, ROPE // 2, dtype=F32) / (ROPE // 2)))
    ang = positions.astype(F32)[:, None] * inv_freq
    cos, sin = jnp.cos(ang), jnp.sin(ang)
    t = positions.shape[0]
    one, zero = jnp.ones((t, 64), F32), jnp.zeros((t, 16), F32)
    c = jnp.concatenate([one, cos, cos, one[:, :32]], axis=1)
    s1 = jnp.concatenate([zero, zero, zero, zero, -sin, zero, zero, zero], axis=1)
    s2 = jnp.concatenate([zero, zero, zero, zero, zero, sin, zero, zero], axis=1)
    return c, s1, s2


def _mixer_weight_params(full):
    w_in = full["w_in"]
    depth = w_in.shape[0]
    zpad = lambda n: jnp.zeros((depth, D, n), w_in.dtype)
    kv = full["w_ukv"].reshape(depth, KVR, HEADS, NOPE + VD)
    return {
        "w_in_a": jnp.concatenate([w_in[:, :, :640], zpad(64), w_in[:, :, 640:672], zpad(32)], axis=2),
        "w_in_b": w_in[:, :, 672:],
        "w_uq": jnp.pad(full["w_uq"].reshape(depth, QR, HEADS, NOPE + ROPE),
                        ((0, 0), (0, 0), (0, 0), (0, 32))).reshape(depth, QR, QW),
        "w_ukv": jnp.concatenate([jnp.pad(kv[..., :NOPE], ((0, 0), (0, 0), (0, 0), (0, 64))).reshape(depth, KVR, QW),
                                  kv[..., NOPE:].reshape(depth, KVR, HEADS * VD)], axis=2),
        "w_out": full["w_out"], "conv_w": full["conv_w"],
    }


def _small_params(w):
    p = {"w_sp": w["w_sp"], "b_sp": jnp.repeat(jnp.swapaxes(w["b_sp"], 1, 2), 64, axis=2)}
    for n in ("mix_pre_g", "mix_post_g", "ffn_pre_g", "ffn_post_g", "q_norm_g", "kv_norm_g", "sg_ln_g", "sg_ln_b",
              "out_norm_g"):
        p[n] = w[n][:, None, :]
    return p


def _natural_grads(gb):
    depth = gb["w_in_a"].shape[0]
    ga, kv = gb["w_in_a"], gb["w_ukv"]
    out = {
        "w_in": jnp.concatenate([ga[:, :, :640], ga[:, :, 704:736], gb["w_in_b"]], axis=2),
        "w_uq": gb["w_uq"].reshape(depth, QR, HEADS, 128)[..., :NOPE + ROPE].reshape(depth, QR, HEADS * (NOPE + ROPE)),
        "w_ukv": jnp.concatenate([kv[:, :, :QW].reshape(depth, KVR, HEADS, 128)[..., :NOPE],
                                  kv[:, :, QW:].reshape(depth, KVR, HEADS, VD)], axis=3).reshape(depth, KVR, -1),
        "b_sp": jnp.swapaxes(gb["b_sp_t"].reshape(depth, CHUNK, 4, 64).sum(axis=-1), 1, 2),
    }
    for n in ("w_out", "w_sp", "conv_w"):
        out[n] = gb[n]
    for n in ("mix_pre_g", "mix_post_g", "ffn_pre_g", "ffn_post_g", "q_norm_g", "kv_norm_g", "sg_ln_g", "sg_ln_b",
              "out_norm_g"):
        out[n] = gb[n][:, 0, :]
    return out


def _local_step(x, positions, target, small, mine, bufs, shard_shapes, fetch=True, front=None):
    depth = small["w_sp"].shape[0]
    tabs = _rope_tables(positions)
    ps = _small_params(small)
    saved, mixer_w = [], []
    for l in range(depth):
        mixer_w.append(_mixer_weight_params(_unpack_weights(bufs[0], l, shard_shapes)))
        p = {**ps, **mixer_w[l], **_ffn_views(bufs)}
        layers = [l + 1 if l + 1 < depth else None, l, l]
        x, sv, fetched = _layer_fwd(x, p, l, tabs, (mine, bufs, layers) if fetch else None)
        bufs = fetched or bufs
        saved.append(sv)
    dx, acc = _loss_head(x, target)
    loss = (0.5 / D) * jnp.sum(acc)
    nbatch = depth // 2
    gbs = [{} for _ in range(nbatch)]
    fronts, sent = [None] * nbatch, [None] * nbatch
    ride = ()
    for l in reversed(range(depth)):
        b = l // 2
        dx, got = _layer_bwd(dx, {**ps, **mixer_w[l], **_ffn_views(bufs)}, saved[l], _Layer(l, l % 2), 2, gbs[b], tabs, ride)
        if ride:
            sent[b + 1], ride = got, ()
        if l % 2 == 0:
            done = (_natural_grads(gbs[b]), gbs[b]["gu"], gbs[b]["down"])
            fronts[b] = front(b, *done) if front else done
            if front and b > 0:
                ride = tuple(fronts[b])
    return loss, dx, fronts, sent


def _place():
    x, y, c = lax.axis_index("x"), lax.axis_index("y"), lax.axis_index("c")
    chips = [(1 - x, y), (x, 1 - y), (1 - x, 1 - y)]
    return x, y, c, 2 * x + y, chips


def _remote(src, dst, send_sem, recv_sem, to):
    return pltpu.make_async_remote_copy(src_ref=src, dst_ref=dst, send_sem=send_sem, recv_sem=recv_sem, device_id=to,
                                        device_id_type=MESH_ID)


def _gather_ops(mine_refs, out_refs, sems, layers):
    send_sems, recv_sems, fsend_sems, frecv_sems = sems
    x, y, c, k, chips = _place()
    sib = (x, y, 1 - c)
    pairs = [(b, n) for n in range(3) for b in range(len(mine_refs)) if layers[b] is not None]

    def slot(n):
        return 2 * chips[n][0] + chips[n][1]

    def ici(b, n, dst_chip):
        return _remote(mine_refs[b].at[layers[b], c], out_refs[b].at[dst_chip, layers[b], c], send_sems.at[b, n],
                       recv_sems.at[b, n], (*chips[n], c))

    def d2d(b, n, half):
        piece = out_refs[b].at[slot(n), layers[b], half]
        return _remote(piece, piece, fsend_sems.at[b, n], frecv_sems.at[b, n], sib)

    def start():
        for b, n in pairs:
            ici(b, n, k).start()

    def hand_over():
        for b, n in pairs:
            ici(b, n, slot(n)).wait_recv()
            d2d(b, n, c).start()

    def drain():
        for b, n in pairs:
            d2d(b, n, 1 - c).wait_recv()
        for b, n in pairs:
            ici(b, n, k).wait_send()
            d2d(b, n, c).wait_send()

    return start, hand_over, drain


def _gather_first_layer(mine):
    nb = len(mine)

    def body(*refs):
        start, hand_over, drain = _gather_ops(refs[:nb], refs[nb:2 * nb], refs[2 * nb:], [0] + [None] * (nb - 1))
        start()
        hand_over()
        drain()

    return pl.pallas_call(
        body, name="gather_first_layer", in_specs=[ANY] * nb, out_specs=[ANY] * nb,
        out_shape=[_sds((4,) + a.shape, a.dtype) for a in mine],
        scratch_shapes=[pltpu.SemaphoreType.DMA((nb, 3))] * 4)(*mine)


def _swap_halves(bigs, wholes=()):
    nb, n = len(bigs), len(bigs) + len(wholes)

    def body(*refs):
        src, dst = refs[:n], refs[n:2 * n]
        send_sems, recv_sems = refs[2 * n:]
        x, y, c, _, _ = _place()
        sib = (x, y, 1 - c)
        cps = [_remote(src[b].at[:, 1 - c] if b < nb else src[b], dst[b], send_sems.at[b], recv_sems.at[b], sib)
               for b in range(n)]
        for cp in cps:
            cp.start()
        for cp in cps:
            cp.wait()

    return pl.pallas_call(
        body, name="swap_halves", in_specs=[ANY] * n, out_specs=[ANY] * n,
        out_shape=[_sds((4,) + a.shape[2:], a.dtype) for a in bigs] + [_sds(a.shape, a.dtype) for a in wholes],
        scratch_shapes=[pltpu.SemaphoreType.DMA((n,))] * 2)(*bigs, *wholes)


def _sum_tile(r):
    return max(cand for cand in range(16, 641, 16) if r % cand == 0)


def _pair_sum(big, rbig, c):
    _, _, r, w = big.shape
    tr = _sum_tile(r)

    def body(c_ref, big_ref, rbig_ref, p_ref):
        p_ref[...] = (big_ref[...].astype(F32) + rbig_ref[...].astype(F32)).astype(BF16)

    grid_spec = pltpu.PrefetchScalarGridSpec(
        num_scalar_prefetch=1, grid=(4, r // tr),
        in_specs=[pl.BlockSpec((None, None, tr, w), lambda j, i, cr: (j, cr[0], i, 0)),
                  pl.BlockSpec((None, tr, w), lambda j, i, cr: (j, i, 0))],
        out_specs=pl.BlockSpec((None, tr, w), lambda j, i, cr: (j, i, 0)))
    return pl.pallas_call(body, name="pair_sum", grid_spec=grid_spec, out_shape=_sds((4, r, w), BF16),
                          compiler_params=_cp("parallel", "parallel"))(c, big, rbig)


def _small_sum(parts):
    n, ns, _ = parts.shape

    def body(p_ref, o_ref):
        s = p_ref[0]
        for j in range(1, n):
            s = s + p_ref[j]
        o_ref[...] = s

    return pl.pallas_call(body, name="small_sum", out_shape=_sds((ns, 128), F32))(parts)


def _exchange_ops(p_refs, rb_refs, sems, small=None):
    send_sems, recv_sems = sems[0], sems[1]
    nb = len(p_refs)
    x, y, c, k, chips = _place()

    def copies(landing):
        out = []
        for n, (cx, cy) in enumerate(chips):
            to, kj = (cx, cy, c), 2 * cx + cy
            for b in range(nb):
                out.append(_remote(p_refs[b].at[k if landing else kj], rb_refs[b].at[n], send_sems.at[b, n],
                                   recv_sems.at[b, n], to))
            if small:
                out.append(_remote(small[0], small[1].at[kj if landing else k], send_sems.at[nb, n], recv_sems.at[nb, n], to))
        return out

    def local():
        return pltpu.make_async_copy(small[0], small[1].at[k], sems[2])

    def start():
        if small:
            local().start()
        for cp in copies(False):
            cp.start()

    def finish():
        for cp in copies(True):
            cp.wait_recv()
        for cp in copies(False):
            cp.wait_send()
        if small:
            local().wait()

    return start, finish


def _chip_exchange(ps, small):
    nb = len(ps)
    ns = small.shape[0]

    def body(*refs):
        start, finish = _exchange_ops(refs[:nb], refs[nb + 1:2 * nb + 1], refs[2 * nb + 2:], (refs[nb], refs[2 * nb + 1]))
        start()
        finish()

    return pl.pallas_call(
        body, name="chip_exchange", in_specs=[ANY] * (nb + 1), out_specs=[ANY] * (nb + 1),
        out_shape=[_sds((3,) + a.shape[1:], a.dtype) for a in ps] + [_sds((4, ns, 128), small.dtype)],
        scratch_shapes=[pltpu.SemaphoreType.DMA((nb + 1, 3))] * 2 + [pltpu.SemaphoreType.DMA(())])(*ps, small)


def _chip_sum(p, rb, chip):
    _, r, w = p.shape
    tr = _sum_tile(r)

    def body(k_ref, p_ref, rb_ref, o_ref):
        acc = p_ref[...].astype(F32)
        for j in range(3):
            acc = acc + rb_ref[j].astype(F32)
        o_ref[...] = acc

    grid_spec = pltpu.PrefetchScalarGridSpec(
        num_scalar_prefetch=1, grid=(r // tr,),
        in_specs=[pl.BlockSpec((None, tr, w), lambda i, kr: (kr[0], i, 0)), pl.BlockSpec((3, tr, w), lambda i, kr: (0, i, 0))],
        out_specs=pl.BlockSpec((tr, w), lambda i, kr: (i, 0)))
    return pl.pallas_call(body, name="chip_sum", grid_spec=grid_spec, out_shape=_sds((r, w), F32),
                          compiler_params=_cp("parallel"))(chip, p, rb)


def _send_to_sibling(reds):
    nb = len(reds)

    def body(*refs):
        red_refs, out_refs = refs[:nb], refs[nb:2 * nb]
        send_sems, recv_sems = refs[2 * nb:]
        x, y, c, _, _ = _place()
        cps = [_remote(red_refs[b], out_refs[b], send_sems.at[b], recv_sems.at[b], (x, y, 1 - c)) for b in range(nb)]
        for cp in cps:
            cp.start()
        for cp in cps:
            cp.wait()

    return pl.pallas_call(
        body, name="send_to_sibling", in_specs=[ANY] * nb, out_specs=[ANY] * nb,
        out_shape=[_sds(a.shape, a.dtype) for a in reds], scratch_shapes=[pltpu.SemaphoreType.DMA((nb,))] * 2)(*reds)


def _adam_math(w, g, m, v):
    nm = ADAM_B1 * m + (1.0 - ADAM_B1) * g
    nv = ADAM_B2 * v + (1.0 - ADAM_B2) * (g * g)
    m_hat = nm / (1.0 - ADAM_B1 ** ADAM_STEP)
    v_hat = nv / (1.0 - ADAM_B2 ** ADAM_STEP)
    return -ADAM_LR * (m_hat / (jnp.sqrt(v_hat) + ADAM_EPS) + ADAM_WD * w), nm, nv


def _adamw_shard(w, m, v, owns, others, c, name, pick=None):
    depth, r, n = w.shape
    nbatch = len(owns)
    tr = max(cand for cand in range(8, min(r, 256) + 1, 8) if r % cand == 0)
    npad = owns[0].shape[-1]

    def body(c_ref, w_ref, m_ref, v_ref, *refs):
        g_ref, d_ref, nm_ref, nv_ref = refs[2 * nbatch:]
        l = pl.program_id(0)
        mine = (l % 2) == c_ref[0]
        g = jnp.where(mine, refs[0][...], refs[nbatch][...])
        for b in range(1, nbatch):
            g = jnp.where(l // 2 == b, jnp.where(mine, refs[b][...], refs[nbatch + b][...]), g)
        g = g[:, :n]
        g_ref[...] = g
        d_ref[...], nm_ref[...], nv_ref[...] = _adam_math(w_ref[...], g, m_ref[...], v_ref[...])

    blk = pl.BlockSpec((None, tr, n), lambda l, i, cr: (l, i, 0))
    if pick is None:
        gblk = pl.BlockSpec((tr, npad), lambda l, i, cr: (i, 0))
    else:
        gblk = pl.BlockSpec((None, tr, npad), lambda l, i, cr: (pick, i, 0))
    grid_spec = pltpu.PrefetchScalarGridSpec(num_scalar_prefetch=1, grid=(depth, r // tr),
                                             in_specs=[blk] * 3 + [gblk] * (2 * nbatch), out_specs=[blk] * 4)
    return pl.pallas_call(body, name=name, grid_spec=grid_spec, out_shape=[_sds(w.shape, F32)] * 4,
                          compiler_params=_cp("parallel", "parallel"))(c, w, m, v, *owns, *others)


def _pad_ffn_shards(w_gate, w_up, w_down):
    depth = w_gate.shape[0]
    hr = HP // 2

    def gu_body(g_ref, u_ref, o_ref):
        for which, ref in enumerate((g_ref, u_ref)):
            o_ref[which, 0:HS, :] = ref[...].astype(BF16)
            o_ref[which, HS:HP, :] = jnp.zeros((HP - HS, D), BF16)

    blk = pl.BlockSpec((None, HS, D), lambda l: (l, 0, 0))
    gu = pl.pallas_call(
        gu_body, name="pad_gate_up", grid=(depth,), in_specs=[blk, blk],
        out_specs=pl.BlockSpec((None, 2, HP, D), lambda l: (l, 0, 0, 0)),
        out_shape=_sds((depth, 2, HP, D), BF16), compiler_params=_cp("parallel"))(w_gate, w_up)

    def down_body(w_ref, o_ref):
        o_ref[0] = w_ref[0:hr, :].astype(BF16)
        o_ref[1, 0:HS - hr, :] = w_ref[hr:HS, :].astype(BF16)
        o_ref[1, HS - hr:hr, :] = jnp.zeros((HP - HS, D), BF16)

    down = pl.pallas_call(
        down_body, name="pad_down", grid=(depth,), in_specs=[pl.BlockSpec((None, HS, D), lambda l: (l, 0, 0))],
        out_specs=pl.BlockSpec((None, 2, hr, D), lambda l: (l, 0, 0, 0)),
        out_shape=_sds((depth, 2, hr, D), BF16), compiler_params=_cp("parallel"))(w_down)
    return gu, down


def _adamw_small(w, g, m, v):
    r = w.shape[0]
    tr = max(cand for cand in range(8, 513, 8) if r % cand == 0)

    def body(w_ref, g_ref, m_ref, v_ref, d_ref, nm_ref, nv_ref):
        d_ref[...], nm_ref[...], nv_ref[...] = _adam_math(w_ref[...], g_ref[...], m_ref[...], v_ref[...])

    blk = pl.BlockSpec((tr, 128), lambda i: (i, 0))
    return pl.pallas_call(body, name="adamw_small", grid=(r // tr,), in_specs=[blk] * 4, out_specs=[blk] * 3,
                          out_shape=[_sds(w.shape, F32)] * 3, compiler_params=_cp("parallel"))(w, g, m, v)


def _to_pack(a, name):
    depth = a.shape[0]
    if name in ROW_SHARDED:
        return jnp.swapaxes(a.reshape(depth, 4, -1, D), 0, 1)
    return jnp.transpose(a.reshape(depth, a.shape[1], 4, a.shape[2] // 4), (2, 0, 1, 3)).reshape(4, depth, -1, D)


def _pack_rows(parts, lead, dtype, tail=None):
    pieces, at = [], 0
    for n, off, rows in PACK:
        if off > at:
            pieces.append(jnp.zeros(lead + (off - at, D), dtype))
        pieces.append(parts[n].astype(dtype))
        at = off + rows
    if tail is not None:
        pieces.append(tail)
        at += tail.shape[-2]
    pieces.append(jnp.zeros(lead + (PACK_ROWS - at, D), dtype))
    return jnp.concatenate(pieces, axis=len(lead))


def _pack_weight_shards(sh):
    depth = sh["w_in"].shape[0]
    parts = {n: sh[n].reshape(depth, rows, D) for n, _, rows in PACK}
    conv = lax.bitcast_convert_type(sh["conv_w"].reshape(depth, 3 * 64), BF16).reshape(depth, 1, 384)
    flat = _pack_rows(parts, (depth,), BF16, tail=jnp.pad(conv, ((0, 0), (0, 0), (0, D - 384))))
    return flat.reshape(depth, 2, PACK_ROWS // 2, D)


def _unpack_weights(gathered, l, shard_shapes):
    depth = 1
    flat = gathered[:, l].reshape(4, 1, PACK_ROWS, D)
    full = {}
    for n, off, rows in PACK:
        shp = shard_shapes[n][1:]
        piece = flat[:, :, off:off + rows, :].reshape((4, depth) + shp)
        if n in ROW_SHARDED:
            full[n] = jnp.transpose(piece, (1, 0, 2, 3)).reshape(depth, 4 * shp[0], shp[1])
        else:
            full[n] = jnp.transpose(piece, (1, 2, 0, 3)).reshape(depth, shp[0], 4 * shp[1])
    conv = lax.bitcast_convert_type(flat[:, :, CONV_ROW, :384].reshape(4, depth, 192, 2), F32)
    full["conv_w"] = jnp.transpose(conv.reshape(4, depth, 3, 64), (1, 2, 0, 3)).reshape(depth, 3, CVW)
    return full


def _pack_grad_shards(g):
    depth = g["w_in"].shape[0]
    return _pack_rows({n: _to_pack(g[n], n) for n, _, _ in PACK}, (4, depth), BF16)


def _pack_small(arrs, names_shapes, depth):
    flat = jnp.concatenate([arrs[n].reshape(depth, -1) for n, _ in names_shapes], axis=1).reshape(-1)
    rows = -(-flat.shape[0] // 1024) * 8
    return jnp.pad(flat, (0, rows * 128 - flat.shape[0])).reshape(rows, 128)


def _unpack_small(packed, names_shapes, depth):
    per_layer = sum(math.prod(s) for _, s in names_shapes)
    flat = packed.reshape(-1)[:depth * per_layer].reshape(depth, per_layer)
    out, off = {}, 0
    for n, s in names_shapes:
        size = math.prod(s)
        out[n] = flat[:, off:off + size].reshape((depth,) + s)
        off += size
    return out


def kernel(x, positions, mix_pre_g, mix_post_g, ffn_pre_g, ffn_post_g, w_in, q_norm_g, w_uq, kv_norm_g, w_ukv, sg_ln_g, sg_ln_b, w_sp, b_sp, conv_w, out_norm_g, w_out, w_gate, w_up, w_down, loss_target, m_mix_pre_g, m_mix_post_g, m_ffn_pre_g, m_ffn_post_g, m_w_in, m_q_norm_g, m_w_uq, m_kv_norm_g, m_w_ukv, m_sg_ln_g, m_sg_ln_b, m_w_sp, m_b_sp, m_conv_w, m_out_norm_g, m_w_out, m_w_gate, m_w_up, m_w_down, v_mix_pre_g, v_mix_post_g, v_ffn_pre_g, v_ffn_post_g, v_w_in, v_q_norm_g, v_w_uq, v_kv_norm_g, v_w_ukv, v_sg_ln_g, v_sg_ln_b, v_w_sp, v_b_sp, v_conv_w, v_out_norm_g, v_w_out, v_w_gate, v_w_up, v_w_down):
    w = dict(mix_pre_g=mix_pre_g, mix_post_g=mix_post_g, ffn_pre_g=ffn_pre_g, ffn_post_g=ffn_post_g, w_in=w_in,
             q_norm_g=q_norm_g, w_uq=w_uq, kv_norm_g=kv_norm_g, w_ukv=w_ukv, sg_ln_g=sg_ln_g, sg_ln_b=sg_ln_b, w_sp=w_sp,
             b_sp=b_sp, conv_w=conv_w, out_norm_g=out_norm_g, w_out=w_out, w_gate=w_gate, w_up=w_up, w_down=w_down)
    m = dict(mix_pre_g=m_mix_pre_g, mix_post_g=m_mix_post_g, ffn_pre_g=m_ffn_pre_g, ffn_post_g=m_ffn_post_g, w_in=m_w_in,
             q_norm_g=m_q_norm_g, w_uq=m_w_uq, kv_norm_g=m_kv_norm_g, w_ukv=m_w_ukv, sg_ln_g=m_sg_ln_g, sg_ln_b=m_sg_ln_b,
             w_sp=m_w_sp, b_sp=m_b_sp, conv_w=m_conv_w, out_norm_g=m_out_norm_g, w_out=m_w_out, w_gate=m_w_gate,
             w_up=m_w_up, w_down=m_w_down)
    v = dict(mix_pre_g=v_mix_pre_g, mix_post_g=v_mix_post_g, ffn_pre_g=v_ffn_pre_g, ffn_post_g=v_ffn_post_g, w_in=v_w_in,
             q_norm_g=v_q_norm_g, w_uq=v_w_uq, kv_norm_g=v_kv_norm_g, w_ukv=v_w_ukv, sg_ln_g=v_sg_ln_g, sg_ln_b=v_sg_ln_b,
             w_sp=v_w_sp, b_sp=v_b_sp, conv_w=v_conv_w, out_norm_g=v_out_norm_g, w_out=v_w_out, w_gate=v_w_gate,
             w_up=v_w_up, w_down=v_w_down)
    depth = w_in.shape[0]
    c = lax.axis_index("c").astype(jnp.int32).reshape(1)
    chip = (2 * lax.axis_index("x") + lax.axis_index("y")).astype(jnp.int32)

    nbatch = depth // 2

    mine = [_pack_weight_shards(w), *_pad_ffn_shards(jnp.swapaxes(w_gate, 1, 2), jnp.swapaxes(w_up, 1, 2), w_down)]
    bufs = [lax.dynamic_update_slice(g, a[None], (chip,) + (0,) * a.ndim)
            for g, a in zip(_gather_first_layer(mine), mine)]

    small_grads = [None] * nbatch
    small_pair = []

    def front(b, grads, g_gu, g_down):
        small_grads[b] = grads
        bigs = [_pack_grad_shards(grads), g_gu.reshape(4, 2, 2 * HP, D), g_down]
        if b > 0:
            rbigs = _swap_halves(bigs)
        else:
            small = _pack_small({n: jnp.concatenate([g[n] for g in small_grads]) for n, _ in SMALL}, SMALL, depth)
            *rbigs, rsmall = _swap_halves(bigs, [small])
            small_pair.append(_small_sum(jnp.stack([small, rsmall])))
        return [_pair_sum(a, r, c) for a, r in zip(bigs, rbigs)]

    loss, dx, ps, sent = _local_step(x[0], positions[0], loss_target[0], w, mine, bufs,
                                     {n: w[n].shape for n, _, _ in PACK}, front=front)
    loss = lax.psum(loss, ("x", "y", "c"))

    *sent[0], rs = _chip_exchange(ps[0], small_pair[0])
    own = [[_chip_sum(p, rb, chip.reshape(1)) for p, rb in zip(ps[b], sent[b])] for b in range(nbatch)]
    other = [_send_to_sibling(o) for o in own]
    g_small = _unpack_small(_small_sum(rs), SMALL, depth)
    g_small["conv_w"] = lax.dynamic_slice_in_dim(g_small["conv_w"], chip * 64, 64, axis=2)

    gw, delta, new_m, new_v = dict(g_small), {}, {}, {}

    def adam(n, pieces, pick=None, turned=False):
        view = (lambda a: jnp.swapaxes(a, 1, 2)) if turned else (lambda a: a)
        outs = _adamw_shard(view(w[n]), view(m[n]), view(v[n]), [pieces(o) for o in own], [pieces(o) for o in other], c,
                            "adamw_" + n, pick)
        gw[n], delta[n], new_m[n], new_v[n] = [view(o) for o in outs]

    for n, off, rows in PACK:
        adam(n, lambda o: o[0][off:off + rows, :].reshape(w[n].shape[1:]))
    adam("w_gate", lambda o: o[1].reshape(2, HP, D), 0, turned=True)
    adam("w_up", lambda o: o[1].reshape(2, HP, D), 1, turned=True)
    adam("w_down", lambda o: o[2])
    small_local = tuple((n, w[n].shape[1:]) for n, _ in SMALL)
    d_, m_, v_ = _adamw_small(_pack_small(w, small_local, depth), _pack_small(gw, small_local, depth),
                              _pack_small(m, small_local, depth), _pack_small(v, small_local, depth))
    delta.update(_unpack_small(d_, small_local, depth))
    new_m.update(_unpack_small(m_, small_local, depth))
    new_v.update(_unpack_small(v_, small_local, depth))

    return (loss, dx[None], *[gw[n] for n in WEIGHTS], *[delta[n] for n in WEIGHTS], *[new_m[n] for n in WEIGHTS],
            *[new_v[n] for n in WEIGHTS])
```

```python
import math

import jax
import jax.numpy as jnp
from jax import lax
from jax.experimental import pallas as pl
from jax.experimental.pallas import tpu as pltpu

F32 = jnp.float32
BF16 = jnp.bfloat16

D = 1024
HEADS = 8
NOPE = 64
ROPE = 32
VD = 64
QR = 384
KVR = 256
SGW = 256
CVW = 256
CHUNK = 128
DFF = 2816
EPS = 1e-6
ROPE_THETA = 10000.0
LOG2E = 1.4426950408889634
LN2 = 0.6931471805599453
QSCALE = (NOPE + ROPE) ** -0.5 * LOG2E
ZA = 768
ZB = 1280
QW = HEADS * 128
KVW = HEADS * 128 + HEADS * VD
NEG = -1e30
GC0 = 0.7978845608028654
GC1 = 0.044715

ADAM_LR = 0.001
ADAM_B1 = 0.9
ADAM_B2 = 0.999
ADAM_EPS = 1e-08
ADAM_WD = 0.01
ADAM_STEP = 10

V7X_VMEM_LIMIT = 52 * 1024 * 1024
ROW_TILE = 512
ATT_TILE = 512
ATT_HEADS = 4

NT = (((1,), (1,)), ((), ()))
TN = (((0,), (0,)), ((), ()))

HS = DFF // 4
HP = 768
DFFP = 4 * HP

PACK = (("w_in", 0, 488), ("w_out", 512, 256), ("w_ukv", 768, 64), ("w_uq", 832, 72))
CONV_ROW = 904
PACK_ROWS = 928
ROW_SHARDED = ("w_out",)
SMALL = (("mix_pre_g", (D,)), ("mix_post_g", (D,)), ("ffn_pre_g", (D,)), ("ffn_post_g", (D,)), ("q_norm_g", (QR,)),
         ("kv_norm_g", (KVR,)), ("sg_ln_g", (SGW,)), ("sg_ln_b", (SGW,)), ("w_sp", (4, CHUNK, CHUNK)), ("b_sp", (4, CHUNK)),
         ("conv_w", (3, CVW)), ("out_norm_g", (D,)))
WEIGHTS = ["mix_pre_g", "mix_post_g", "ffn_pre_g", "ffn_post_g", "w_in", "q_norm_g", "w_uq", "kv_norm_g", "w_ukv", "sg_ln_g",
           "sg_ln_b", "w_sp", "b_sp", "conv_w", "out_norm_g", "w_out", "w_gate", "w_up", "w_down"]

MESH_ID = pl.DeviceIdType.MESH
ANY = pl.BlockSpec(memory_space=pl.ANY)


def _cp(*sem):
    return pltpu.CompilerParams(dimension_semantics=sem, vmem_limit_bytes=V7X_VMEM_LIMIT)


def _sds(shape, dtype):
    return jax.ShapeDtypeStruct(shape, dtype)


def _row(tm, n):
    return pl.BlockSpec((tm, n), lambda i: (i, 0))


def _lyr(l, *shape):
    return pl.BlockSpec((None,) + shape, lambda *_: (l,) + (0,) * len(shape))


def _wl(a, l):
    return 0 if a.shape[0] == 1 else l


def _pcall(body, name, grid, ins, in_specs, out_specs, out_shape, sem, scratch=(), prevs=None):
    prevs = {k: v for k, v in (prevs or {}).items() if v is not None}
    order = sorted(prevs)
    n_in = len(ins)

    def wrapped(*refs):
        return body(*refs[:n_in], *refs[n_in + len(order):])

    return pl.pallas_call(
        wrapped, name=name, grid=grid, in_specs=list(in_specs) + [ANY] * len(order), out_specs=out_specs,
        out_shape=out_shape, scratch_shapes=list(scratch),
        input_output_aliases={n_in + i: k for i, k in enumerate(order)},
        compiler_params=_cp(*sem))(*ins, *[prevs[k] for k in order])


def _rms(x, g):
    r = lax.rsqrt(jnp.mean(x * x, axis=-1, keepdims=True) + EPS)
    return x * r * g


def _rms_bwd(x, g, dy):
    r = lax.rsqrt(jnp.mean(x * x, axis=-1, keepdims=True) + EPS)
    xh = x * r
    dg = jnp.sum(dy * xh, axis=0, keepdims=True)
    dxh = dy * g
    dx = r * (dxh - xh * jnp.mean(dxh * xh, axis=-1, keepdims=True))
    return dx, dg


def _sigmoid(x):
    return 0.5 * jnp.tanh(0.5 * x) + 0.5


def _gelu(x):
    return 0.5 * x * (1.0 + jnp.tanh(GC0 * (x + GC1 * x * x * x)))


def _gelu_grad(x):
    t = jnp.tanh(GC0 * (x + GC1 * x * x * x))
    return 0.5 * (1.0 + t) + 0.5 * x * (1.0 - t * t) * GC0 * (1.0 + 3.0 * GC1 * x * x)


def _rope(xb, c, s1, s2):
    return xb * c + pltpu.roll(xb, 112, 1) * s1 + pltpu.roll(xb, 16, 1) * s2


def _rope_bwd(dy, c, s1, s2):
    return dy * c + pltpu.roll(dy * s1, 16, 1) + pltpu.roll(dy * s2, 112, 1)


def _group_masks(shape):
    lane = lax.broadcasted_iota(jnp.int32, shape, 1)
    return [(lane >= 64 * g) & (lane < 64 * g + 64) for g in range(shape[1] // 64)]


def _group_mean(v, masks):
    out = jnp.zeros_like(v)
    for m in masks:
        s = jnp.sum(jnp.where(m, v, 0.0), axis=-1, keepdims=True) * (1.0 / 64.0)
        out = jnp.where(m, s, out)
    return out


def _pick_row(blk, idx):
    row = lax.broadcasted_iota(jnp.int32, blk.shape, 0)
    return jnp.sum(jnp.where(row == idx, blk, 0.0), axis=0, keepdims=True)


def _shift_down(y, k, first_rows):
    out = pltpu.roll(y, k, 0)
    row = lax.broadcasted_iota(jnp.int32, y.shape, 0)
    for idx in range(k):
        out = jnp.where(row == idx, first_rows[idx], out)
    return out


def _shift_up(y, k, last_rows):
    n = y.shape[0]
    out = pltpu.roll(y, n - k, 0)
    row = lax.broadcasted_iota(jnp.int32, y.shape, 0)
    for idx in range(k):
        out = jnp.where(row == n - k + idx, last_rows[idx], out)
    return out


def _tril_mask():
    r = lax.broadcasted_iota(jnp.int32, (CHUNK, CHUNK), 0)
    c = lax.broadcasted_iota(jnp.int32, (CHUNK, CHUNK), 1)
    return r >= c


def _sgu_forward(zu, zv, g_ln, b_ln, wc_bf, bsp, masks, cmasks):
    u = _gelu(zu)
    vv = _gelu(zv)
    mu = _group_mean(vv, masks)
    dv = vv - mu
    rs = lax.rsqrt(_group_mean(dv * dv, masks) + EPS)
    xh = dv * rs
    vn = xh * g_ln + b_ln
    chunks = []
    for ci in range(zu.shape[0] // CHUNK):
        vc = vn[ci * CHUNK:(ci + 1) * CHUNK, :]
        acc = bsp
        for g in range(4):
            acc = acc + jnp.dot(wc_bf[g], jnp.where(cmasks[g], vc, 0.0).astype(BF16), preferred_element_type=F32)
        chunks.append(acc)
    mixed = jnp.concatenate(chunks, axis=0) if len(chunks) > 1 else chunks[0]
    return u, vv, xh, rs, vn, mixed


def _conv_forward(gc, hh, prev_gc, prev_hh, first_tile, cw):
    yv = gc * hh
    prev = jnp.where(first_tile, 0.0, prev_gc * prev_hh)
    p6, p7 = _pick_row(prev, 6), _pick_row(prev, 7)
    sh1 = _shift_down(yv, 1, [p7])
    sh2 = _shift_down(yv, 2, [p6, p7])
    conv = sh2 * cw[0:1, :] + sh1 * cw[1:2, :] + yv * cw[2:3, :]
    return yv, sh1, sh2, conv


def _acc_init(step, *refs):
    @pl.when(step == 0)
    def _():
        for r in refs:
            r[...] = jnp.zeros(r.shape, r.dtype)


def _in_proj(x, p, l):
    t = x.shape[0]
    tm = min(ROW_TILE, t)

    def body(x_ref, g_ref, wa_ref, wb_ref, h_ref, za_ref, zb_ref):
        h = _rms(x_ref[...], g_ref[...]).astype(BF16)
        h_ref[...] = h
        za_ref[...] = jnp.dot(h, wa_ref[...], preferred_element_type=F32)
        zb_ref[...] = jnp.dot(h, wb_ref[...], preferred_element_type=F32)

    return _pcall(
        body, "in_proj", (t // tm,), [x, p["mix_pre_g"], p["w_in_a"], p["w_in_b"]],
        [_row(tm, D), _lyr(l, 1, D), _lyr(_wl(p["w_in_a"], l), D, ZA), _lyr(_wl(p["w_in_b"], l), D, ZB)],
        [_row(tm, D), _row(tm, ZA), _row(tm, ZB)],
        [_sds((t, D), BF16), _sds((t, ZA), F32), _sds((t, ZB), F32)], ("parallel",))


def _mla_prep(za, p, l, tabs):
    t = za.shape[0]
    tm = min(ROW_TILE, t)

    def body(z_ref, gq_ref, gkv_ref, wuq_ref, wukv_ref, c_ref, s1_ref, s2_ref, cq_ref, ckv_ref, q_ref, k_ref, v_ref):
        z = z_ref[...]
        cq = _rms(z[:, :QR], gq_ref[...]).astype(BF16)
        ckv = _rms(z[:, QR:QR + KVR], gkv_ref[...]).astype(BF16)
        cq_ref[...] = cq
        ckv_ref[...] = ckv
        c, s1, s2 = c_ref[...], s1_ref[...], s2_ref[...]
        kr = _rope(z[:, QR + KVR:], c, s1, s2)
        q = jnp.dot(cq, wuq_ref[...], preferred_element_type=F32)
        kv = jnp.dot(ckv, wukv_ref[...], preferred_element_type=F32)
        for h in range(HEADS):
            sl = slice(128 * h, 128 * h + 128)
            q_ref[:, sl] = (_rope(q[:, sl], c, s1, s2) * QSCALE).astype(BF16)
            k_ref[:, sl] = (kv[:, sl] + kr).astype(BF16)
        v_ref[...] = kv[:, QW:].astype(BF16)

    return _pcall(
        body, "mla_prep", (t // tm,), [za, p["q_norm_g"], p["kv_norm_g"], p["w_uq"], p["w_ukv"], *tabs],
        [_row(tm, ZA), _lyr(l, 1, QR), _lyr(l, 1, KVR), _lyr(_wl(p["w_uq"], l), QR, QW), _lyr(_wl(p["w_ukv"], l), KVR, KVW),
         _row(tm, 128), _row(tm, 128), _row(tm, 128)],
        [_row(tm, QR), _row(tm, KVR), _row(tm, QW), _row(tm, QW), _row(tm, HEADS * VD)],
        [_sds((t, QR), BF16), _sds((t, KVR), BF16), _sds((t, QW), BF16), _sds((t, QW), BF16),
         _sds((t, HEADS * VD), BF16)], ("parallel",))


def _att_tile(t):
    return min(ATT_TILE, max(t // 2, 128))


def _causal_keep(tq, i, j):
    row = lax.broadcasted_iota(jnp.int32, (tq, tq), 0) + i * tq
    col = lax.broadcasted_iota(jnp.int32, (tq, tq), 1) + j * tq
    return col <= row


def _attn_fwd(qs, k, v, fetch=None):
    t = qs.shape[0]
    tq = _att_tile(t)
    nq = t // tq
    rep = tq // 128
    groups = HEADS // ATT_HEADS
    mine, bufs, fetch_layer = fetch if fetch else ((), (), None)
    nb = len(mine)

    steps = [(i, j) for i in range(nq) for j in range(i + 1)]
    i_of = jnp.asarray([s[0] for s in steps], jnp.int32)
    j_of = jnp.asarray([s[1] for s in steps], jnp.int32)

    def body(i_ref, j_ref, q_ref, k_ref, v_ref, *refs):
        o_ref, lse_ref = refs[2 * nb:2 * nb + 2]
        m_s, l_s, acc_s = refs[3 * nb + 2:3 * nb + 5]
        step_no = pl.program_id(1)
        i, j = i_ref[step_no], j_ref[step_no]
        if fetch:
            start, hand_over, drain = _gather_ops(refs[:nb], refs[2 * nb + 2:3 * nb + 2], refs[3 * nb + 5:], fetch_layer)
            pr = pl.program_id(0)
            pl.when((pr == 0) & (step_no == 0))(start)
            pl.when((pr == groups - 1) & (step_no == 3 * len(steps) // 4))(hand_over)
            pl.when((pr == groups - 1) & (step_no == len(steps) - 1))(drain)

        @pl.when(j == 0)
        def _():
            m_s[...] = jnp.full(m_s.shape, NEG, F32)
            l_s[...] = jnp.zeros(l_s.shape, F32)
            acc_s[...] = jnp.zeros(acc_s.shape, F32)

        def step(masked):
            keep = _causal_keep(tq, i, j) if masked else None
            for hh in range(ATT_HEADS):
                sl = slice(128 * hh, 128 * hh + 128)
                vv = v_ref[:, 128 * (hh // 2):128 * (hh // 2) + 128]
                s = lax.dot_general(q_ref[:, sl], k_ref[:, sl], NT, preferred_element_type=F32)
                if masked:
                    s = jnp.where(keep, s, NEG)
                m_old = m_s[hh]
                m_new = jnp.maximum(m_old, jnp.max(s, axis=-1, keepdims=True))
                alpha = jnp.exp2(m_old - m_new)
                p = jnp.exp2(s - jnp.tile(m_new, (1, rep)))
                l_s[hh] = alpha * l_s[hh] + jnp.sum(p, axis=-1, keepdims=True)
                acc_s[hh] = alpha * acc_s[hh] + jnp.dot(p.astype(BF16), vv, preferred_element_type=F32)
                m_s[hh] = m_new

        @pl.when(j < i)
        def _():
            step(False)

        @pl.when(j == i)
        def _():
            step(True)
            lane = lax.broadcasted_iota(jnp.int32, (tq, 128), 1)
            for pp in range(ATT_HEADS // 2):
                a, b = 2 * pp, 2 * pp + 1
                o_ref[:, 128 * pp:128 * pp + 128] = jnp.where(lane < VD, acc_s[a] / l_s[a], acc_s[b] / l_s[b])
            for hh in range(ATT_HEADS):
                lse_ref[hh] = (m_s[hh] + jnp.log2(l_s[hh]))[:, 0:1]

    qw, vw = 128 * ATT_HEADS, VD * ATT_HEADS
    stat = pltpu.VMEM((ATT_HEADS, tq, 128), F32)
    grid_spec = pltpu.PrefetchScalarGridSpec(
        num_scalar_prefetch=2, grid=(groups, len(steps)),
        in_specs=[pl.BlockSpec((tq, qw), lambda p, s, it, jt: (it[s], p)),
                  pl.BlockSpec((tq, qw), lambda p, s, it, jt: (jt[s], p)),
                  pl.BlockSpec((tq, vw), lambda p, s, it, jt: (jt[s], p))] + [ANY] * (2 * nb),
        out_specs=[pl.BlockSpec((tq, vw), lambda p, s, it, jt: (it[s], p)),
                   pl.BlockSpec((ATT_HEADS, tq, 1), lambda p, s, it, jt: (p, it[s], 0))] + [ANY] * nb,
        scratch_shapes=[stat, stat, stat] + ([pltpu.SemaphoreType.DMA((nb, 3))] * 4 if fetch else []))
    outs = pl.pallas_call(
        body, name="attn_fwd_fetch" if fetch else "attn_fwd", grid_spec=grid_spec,
        out_shape=[_sds((t, HEADS * VD), F32), _sds((HEADS, t, 1), F32)] + [_sds(b.shape, b.dtype) for b in bufs],
        input_output_aliases={5 + nb + b: 2 + b for b in range(nb)},
        compiler_params=_cp("arbitrary", "arbitrary"))(i_of, j_of, qs, k, v, *mine, *bufs)
    return outs[0], outs[1], list(outs[2:])


def _mixer_fwd(zb, ya, p, l):
    t = zb.shape[0]
    tm = min(ROW_TILE, t)
    hb = tm // 8

    def body(zb_ref, zprev_ref, ya_ref, gln_ref, bln_ref, wsp_ref, bsp_ref, cw_ref, go_ref, mix_ref, yb_ref, yc_ref):
        i = pl.program_id(0)
        masks = _group_masks((tm, SGW))
        cmasks = _group_masks((CHUNK, SGW))
        tril = _tril_mask()
        wc_bf = [jnp.where(tril, wsp_ref[g], 0.0).astype(BF16) for g in range(4)]
        u, _, _, _, _, mixed = _sgu_forward(zb_ref[:, 0:256], zb_ref[:, 256:512], gln_ref[...], bln_ref[...], wc_bf,
                                            bsp_ref[...], masks, cmasks)
        yb = u * mixed
        _, _, _, conv = _conv_forward(zb_ref[:, 768:1024], zb_ref[:, 1024:1280], zprev_ref[:, 768:1024],
                                      zprev_ref[:, 1024:1280], i == 0, cw_ref[...])
        yc = zb_ref[:, 512:768] * conv
        yb_ref[...] = yb
        yc_ref[...] = yc
        go = go_ref[...]
        mix_ref[:, 0:512] = _rms(ya_ref[...], go[:, 0:512]).astype(BF16)
        mix_ref[:, 512:768] = _rms(yb, go[:, 512:768]).astype(BF16)
        mix_ref[:, 768:1024] = _rms(yc, go[:, 768:1024]).astype(BF16)

    return _pcall(
        body, "mixer_fwd", (t // tm,),
        [zb, zb, ya, p["sg_ln_g"], p["sg_ln_b"], p["w_sp"], p["b_sp"], p["conv_w"], p["out_norm_g"]],
        [_row(tm, ZB), pl.BlockSpec((8, ZB), lambda i: (jnp.maximum(i * hb - 1, 0), 0)), _row(tm, 512),
         _lyr(l, 1, SGW), _lyr(l, 1, SGW), _lyr(l, 4, CHUNK, CHUNK), _lyr(l, CHUNK, SGW), _lyr(_wl(p["conv_w"], l), 3, CVW), _lyr(l, 1, D)],
        [_row(tm, D), _row(tm, SGW), _row(tm, CVW)],
        [_sds((t, D), BF16), _sds((t, SGW), F32), _sds((t, CVW), F32)], ("parallel",))


def _out_proj(mix, x, p, l):
    t = x.shape[0]
    tm = min(ROW_TILE, t)

    def body(mix_ref, w_ref, x_ref, gp_ref, gf_ref, o_ref, x2_ref, h2_ref):
        o = jnp.dot(mix_ref[...], w_ref[...], preferred_element_type=F32)
        o_ref[...] = o
        x2 = x_ref[...] + _rms(o, gp_ref[...])
        x2_ref[...] = x2
        h2_ref[...] = _rms(x2, gf_ref[...]).astype(BF16)

    return _pcall(
        body, "out_proj", (t // tm,), [mix, p["w_out"], x, p["mix_post_g"], p["ffn_pre_g"]],
        [_row(tm, D), _lyr(_wl(p["w_out"], l), D, D), _row(tm, D), _lyr(l, 1, D), _lyr(l, 1, D)],
        [_row(tm, D), _row(tm, D), _row(tm, D)],
        [_sds((t, D), F32), _sds((t, D), F32), _sds((t, D), BF16)], ("parallel",))


def _gu_all(l, which):
    return pl.BlockSpec((4, None, None, HP, D), lambda *_: (0, l, which, 0, 0))


def _down_all(l):
    return pl.BlockSpec((4, None, HP, D), lambda *_: (0, l, 0, 0))


def _ffn_up(h2, p, l):
    t = h2.shape[0]
    tm = min(ROW_TILE, t)

    def body(h_ref, wg_ref, wu_ref, a_ref, b_ref, s_ref):
        h = h_ref[...]
        a = lax.dot_general(h, wg_ref[...], NT, preferred_element_type=F32)
        b = lax.dot_general(h, wu_ref[...], NT, preferred_element_type=F32)
        a_ref[...] = a.astype(BF16)
        b_ref[...] = b.astype(BF16)
        s_ref[...] = (a * _sigmoid(a) * b).astype(BF16)

    blk = pl.BlockSpec((tm, HP), lambda k, i: (i, k))
    wblk = lambda which: pl.BlockSpec((None, None, None, HP, D), lambda k, i: (k, l, which, 0, 0))
    return _pcall(
        body, "ffn_up", (4, t // tm), [h2, p["w_gu"], p["w_gu"]],
        [pl.BlockSpec((tm, D), lambda k, i: (i, 0)), wblk(0), wblk(1)], [blk, blk, blk],
        [_sds((t, DFFP), BF16)] * 3, ("parallel", "parallel"))


def _ffn_down(s, x2, p, l):
    t = x2.shape[0]
    tm = min(ROW_TILE, t)

    def body(s_ref, w_ref, x_ref, g_ref, f_ref, x3_ref):
        f = jnp.dot(s_ref[:, 0:HP], w_ref[0], preferred_element_type=F32)
        for k in range(1, 4):
            f = f + jnp.dot(s_ref[:, k * HP:(k + 1) * HP], w_ref[k], preferred_element_type=F32)
        f_ref[...] = f
        x3_ref[...] = x_ref[...] + _rms(f, g_ref[...])

    return _pcall(
        body, "ffn_down", (t // tm,), [s, p["w_down"], x2, p["ffn_post_g"]],
        [_row(tm, DFFP), _down_all(l), _row(tm, D), _lyr(l, 1, D)], [_row(tm, D), _row(tm, D)],
        [_sds((t, D), F32), _sds((t, D), F32)], ("parallel",))


def _loss_head(y, target):
    t = y.shape[0]
    tm = min(ROW_TILE, t)

    def body(y_ref, t_ref, dy_ref, acc_ref):
        e = y_ref[...] - t_ref[...]
        dy_ref[...] = e * (1.0 / D)
        sq = jnp.sum(e * e, axis=0, keepdims=True)
        part = sq[:, 0:128]
        for b in range(1, D // 128):
            part = part + sq[:, 128 * b:128 * b + 128]
        _acc_init(pl.program_id(0), acc_ref)
        acc_ref[...] += part

    return _pcall(body, "loss_head", (t // tm,), [y, target], [_row(tm, D), _row(tm, D)],
                  [_row(tm, D), pl.BlockSpec((1, 128), lambda i: (0, 0))],
                  [_sds((t, D), F32), _sds((1, 128), F32)], ("arbitrary",))


def _ffn_down_bwd(dx3, sv, p, l, depth, gb):
    t = dx3.shape[0]
    tm = min(256, t)

    def body(dx_ref, f_ref, g_ref, w_ref, a_ref, b_ref, df_ref, da_ref, db_ref, dg_ref):
        _acc_init(pl.program_id(0), dg_ref)
        df, dg = _rms_bwd(f_ref[...], g_ref[...], dx_ref[...])
        dg_ref[...] += dg
        df = df.astype(BF16)
        df_ref[...] = df
        for k in range(4):
            sl = slice(k * HP, (k + 1) * HP)
            ds = lax.dot_general(df, w_ref[k], NT, preferred_element_type=F32)
            av = a_ref[:, sl].astype(F32)
            sig = _sigmoid(av)
            da_ref[:, sl] = (ds * b_ref[:, sl].astype(F32) * (sig * (1.0 + av * (1.0 - sig)))).astype(BF16)
            db_ref[:, sl] = (ds * (av * sig)).astype(BF16)

    df, da, db, gb["ffn_post_g"] = _pcall(
        body, "ffn_down_bwd", (t // tm,), [dx3, sv["f"], p["ffn_post_g"], p["w_down"], sv["a"], sv["b"]],
        [_row(tm, D), _row(tm, D), _lyr(l, 1, D), _down_all(l), _row(tm, DFFP), _row(tm, DFFP)],
        [_row(tm, D), _row(tm, DFFP), _row(tm, DFFP), _lyr(l.g, 1, D)],
        [_sds((t, D), BF16), _sds((t, DFFP), BF16), _sds((t, DFFP), BF16), _sds((depth, 1, D), F32)], ("arbitrary",),
        prevs={3: gb.get("ffn_post_g")})
    return df, da, db


def _ffn_up_bwd(da, db, dx3, sv, p, l, depth, gb):
    t = dx3.shape[0]
    tm = min(256, t)

    def body(da_ref, db_ref, wg_ref, wu_ref, x_ref, dx3_ref, g_ref, dx2_ref, dg_ref):
        _acc_init(pl.program_id(0), dg_ref)
        dh = jnp.zeros((tm, D), F32)
        for k in range(4):
            sl = slice(k * HP, (k + 1) * HP)
            dh = dh + jnp.dot(da_ref[:, sl], wg_ref[k], preferred_element_type=F32)
            dh = dh + jnp.dot(db_ref[:, sl], wu_ref[k], preferred_element_type=F32)
        dx, dg = _rms_bwd(x_ref[...], g_ref[...], dh)
        dg_ref[...] += dg
        dx2_ref[...] = dx3_ref[...] + dx

    dx2, gb["ffn_pre_g"] = _pcall(
        body, "ffn_up_bwd", (t // tm,), [da, db, p["w_gu"], p["w_gu"], sv["x2"], dx3, p["ffn_pre_g"]],
        [_row(tm, DFFP), _row(tm, DFFP), _gu_all(l, 0), _gu_all(l, 1), _row(tm, D), _row(tm, D), _lyr(l, 1, D)],
        [_row(tm, D), _lyr(l.g, 1, D)], [_sds((t, D), F32), _sds((depth, 1, D), F32)], ("arbitrary",),
        prevs={1: gb.get("ffn_pre_g")})
    return dx2


def _out_proj_bwd(dx2, sv, p, l, depth, gb):
    t = dx2.shape[0]
    tm = min(ROW_TILE, t)

    def body(dx_ref, o_ref, g_ref, w_ref, do_ref, dmix_ref, dg_ref):
        _acc_init(pl.program_id(0), dg_ref)
        do, dg = _rms_bwd(o_ref[...], g_ref[...], dx_ref[...])
        dg_ref[...] += dg
        do = do.astype(BF16)
        do_ref[...] = do
        dmix_ref[...] = lax.dot_general(do, w_ref[...], NT, preferred_element_type=F32)

    do, dmix, gb["mix_post_g"] = _pcall(
        body, "out_proj_bwd", (t // tm,), [dx2, sv["o"], p["mix_post_g"], p["w_out"]],
        [_row(tm, D), _row(tm, D), _lyr(l, 1, D), _lyr(_wl(p["w_out"], l), D, D)], [_row(tm, D), _row(tm, D), _lyr(l.g, 1, D)],
        [_sds((t, D), BF16), _sds((t, D), F32), _sds((depth, 1, D), F32)], ("arbitrary",),
        prevs={2: gb.get("mix_post_g")})
    return do, dmix


def _mixer_bwd(dmix, sv, p, l, depth, gb):
    zb = sv["zb"]
    t = zb.shape[0]
    tm = min(ROW_TILE, t)
    hb = tm // 8
    last_blk = t // 8 - 1
    nsteps = t // tm

    def body(dmix_ref, ya_ref, yb_ref, yc_ref, zb_ref, zprev_ref, znext_ref, ycn_ref, dmn_ref,
             gln_ref, bln_ref, wsp_ref, bsp_ref, cw_ref, go_ref,
             dya_ref, dzb_ref, delta_ref, dgo_ref, dgln_ref, dbln_ref, dwsp_ref, dbsp_ref, dcw_ref):
        i = pl.program_id(0)
        _acc_init(i, dgo_ref, dgln_ref, dbln_ref, dwsp_ref, dbsp_ref, dcw_ref)
        go = go_ref[...]
        dmix = dmix_ref[...]

        ya = ya_ref[...]
        dya, dga = _rms_bwd(ya, go[:, 0:512], dmix[:, 0:512])
        dyb, dgb_ = _rms_bwd(yb_ref[...], go[:, 512:768], dmix[:, 512:768])
        dyc, dgc_ = _rms_bwd(yc_ref[...], go[:, 768:1024], dmix[:, 768:1024])
        dgo_ref[:, 0:512] += dga
        dgo_ref[:, 512:768] += dgb_
        dgo_ref[:, 768:1024] += dgc_
        dya = dya * LN2
        dya_ref[...] = dya.astype(BF16)
        prod = dya * ya
        hmasks = _group_masks((tm, 512))
        for h in range(HEADS):
            delta_ref[h] = jnp.sum(jnp.where(hmasks[h], prod, 0.0), axis=-1, keepdims=True)

        masks = _group_masks((tm, SGW))
        cmasks = _group_masks((CHUNK, SGW))
        tril = _tril_mask()
        wc_bf = [jnp.where(tril, wsp_ref[g], 0.0).astype(BF16) for g in range(4)]
        zu, zv = zb_ref[:, 0:256], zb_ref[:, 256:512]
        g_ln = gln_ref[...]
        u, _, xh, rs, vn, mixed = _sgu_forward(zu, zv, g_ln, bln_ref[...], wc_bf, bsp_ref[...], masks, cmasks)
        du = dyb * mixed
        dmixed = dyb * u
        dvn_chunks = []
        dbsp = jnp.zeros((CHUNK, SGW), F32)
        for ci in range(tm // CHUNK):
            rows = slice(ci * CHUNK, (ci + 1) * CHUNK)
            dm_c = dmixed[rows, :]
            vn_c = vn[rows, :].astype(BF16)
            dbsp = dbsp + dm_c
            dvn_c = jnp.zeros((CHUNK, SGW), F32)
            for g in range(4):
                dm_g = jnp.where(cmasks[g], dm_c, 0.0).astype(BF16)
                dw = lax.dot_general(dm_g, vn_c, NT, preferred_element_type=F32)
                dwsp_ref[g] += jnp.where(tril, dw, 0.0)
                dvn_c = dvn_c + lax.dot_general(wc_bf[g], dm_g, TN, preferred_element_type=F32)
            dvn_chunks.append(dvn_c)
        dbsp_ref[...] += dbsp
        dvn = jnp.concatenate(dvn_chunks, axis=0) if len(dvn_chunks) > 1 else dvn_chunks[0]
        dgln_ref[...] += jnp.sum(dvn * xh, axis=0, keepdims=True)
        dbln_ref[...] += jnp.sum(dvn, axis=0, keepdims=True)
        dxh = dvn * g_ln
        dvv = rs * (dxh - _group_mean(dxh, masks) - xh * _group_mean(dxh * xh, masks))
        dzb_ref[:, 0:256] = (du * _gelu_grad(zu)).astype(BF16)
        dzb_ref[:, 256:512] = (dvv * _gelu_grad(zv)).astype(BF16)

        cwv = cw_ref[...]
        gb_, gc, hh = zb_ref[:, 512:768], zb_ref[:, 768:1024], zb_ref[:, 1024:1280]
        yv, sh1, sh2, conv = _conv_forward(gc, hh, zprev_ref[:, 768:1024], zprev_ref[:, 1024:1280], i == 0, cwv)
        dconv = dyc * gb_
        dzb_ref[:, 512:768] = (dyc * conv).astype(BF16)
        dcw_ref[0:1, :] += jnp.sum(dconv * sh2, axis=0, keepdims=True)
        dcw_ref[1:2, :] += jnp.sum(dconv * sh1, axis=0, keepdims=True)
        dcw_ref[2:3, :] += jnp.sum(dconv * yv, axis=0, keepdims=True)
        dycn, _ = _rms_bwd(ycn_ref[...], go[:, 768:1024], dmn_ref[...])
        dconv_next = jnp.where(i == nsteps - 1, 0.0, dycn * znext_ref[:, 512:768])
        n0, n1 = _pick_row(dconv_next, 0), _pick_row(dconv_next, 1)
        dyv = dconv * cwv[2:3, :] + _shift_up(dconv, 1, [n0]) * cwv[1:2, :] + _shift_up(dconv, 2, [n0, n1]) * cwv[0:1, :]
        dzb_ref[:, 768:1024] = (dyv * hh).astype(BF16)
        dzb_ref[:, 1024:1280] = (dyv * gc).astype(BF16)

    prev_map = lambda i: (jnp.maximum(i * hb - 1, 0), 0)
    next_map = lambda i: (jnp.minimum((i + 1) * hb, last_blk), 0)
    names = ("out_norm_g", "sg_ln_g", "sg_ln_b", "w_sp", "b_sp_t", "conv_w")
    shapes = ((1, D), (1, SGW), (1, SGW), (4, CHUNK, CHUNK), (CHUNK, SGW), (3, CVW))
    outs = _pcall(
        body, "mixer_bwd", (nsteps,),
        [dmix, sv["ya"], sv["yb"], sv["yc"], zb, zb, zb, sv["yc"], dmix, p["sg_ln_g"], p["sg_ln_b"], p["w_sp"], p["b_sp"],
         p["conv_w"], p["out_norm_g"]],
        [_row(tm, D), _row(tm, 512), _row(tm, SGW), _row(tm, CVW), _row(tm, ZB),
         pl.BlockSpec((8, ZB), prev_map), pl.BlockSpec((8, ZB), next_map), pl.BlockSpec((8, CVW), next_map),
         pl.BlockSpec((8, 256), lambda i: (jnp.minimum((i + 1) * hb, last_blk), 3)),
         _lyr(l, 1, SGW), _lyr(l, 1, SGW), _lyr(l, 4, CHUNK, CHUNK), _lyr(l, CHUNK, SGW), _lyr(_wl(p["conv_w"], l), 3, CVW), _lyr(l, 1, D)],
        [_row(tm, 512), _row(tm, ZB), pl.BlockSpec((HEADS, tm, 1), lambda i: (0, i, 0))] + [_lyr(l.g, *s) for s in shapes],
        [_sds((t, 512), BF16), _sds((t, ZB), BF16), _sds((HEADS, t, 1), F32)] + [_sds((depth,) + s, F32) for s in shapes],
        ("arbitrary",), prevs={3 + n: gb.get(name) for n, name in enumerate(names)})
    for n, name in enumerate(names):
        gb[name] = outs[3 + n]
    return outs[0], outs[1], outs[2]


def _attn_bwd(qs, k, v, dya, lse, delta, ride=()):
    t = qs.shape[0]
    tq = _att_tile(t)
    nq = t // tq
    nb = len(ride)
    groups = HEADS // ATT_HEADS

    steps = [(j, i) for j in range(nq) for i in range(j, nq)]
    j_of = jnp.asarray([s[0] for s in steps], jnp.int32)
    i_of = jnp.asarray([s[1] for s in steps], jnp.int32)

    def body(j_ref, i_ref, q_ref, k_ref, v_ref, do_ref, lse_ref, dl_ref, *refs):
        dq_ref, dk_ref, dv_ref = refs[nb:nb + 3]
        dq_s, dk_s, dv_s = refs[2 * nb + 3:2 * nb + 6]
        step_no = pl.program_id(1)
        j, i = j_ref[step_no], i_ref[step_no]
        if ride:
            start, finish = _exchange_ops(refs[:nb], refs[nb + 3:2 * nb + 3], refs[2 * nb + 6:])
            pr = pl.program_id(0)
            pl.when((pr == 0) & (step_no == 0))(start)
            pl.when((pr == groups - 1) & (step_no == len(steps) - 1))(finish)

        @pl.when(step_no == 0)
        def _():
            dq_s[...] = jnp.zeros(dq_s.shape, F32)

        def step(masked):
            keep = _causal_keep(tq, 0, 0) if masked else None
            lane = lax.broadcasted_iota(jnp.int32, (tq, 128), 1)
            rows = pl.ds(pl.multiple_of(i * tq, tq), tq)
            for hh in range(ATT_HEADS):
                sl = slice(128 * hh, 128 * hh + 128)
                pair = slice(128 * (hh // 2), 128 * (hh // 2) + 128)
                vv, do = v_ref[:, pair], do_ref[:, pair]
                qq, kk = q_ref[:, sl], k_ref[:, sl]
                s = lax.dot_general(qq, kk, NT, preferred_element_type=F32)
                p = jnp.exp2(s - lse_ref[hh])
                if masked:
                    p = jnp.where(keep, p, 0.0)
                do_h = jnp.where((lane < VD) if hh % 2 == 0 else (lane >= VD), do, jnp.zeros_like(do))
                dp = lax.dot_general(do_h, vv, NT, preferred_element_type=F32)
                ds = (p * (dp - dl_ref[hh])).astype(BF16)
                dv_s[:, pair] += lax.dot_general(p.astype(BF16), do_h, TN, preferred_element_type=F32)
                dk_s[:, sl] += lax.dot_general(ds, qq, TN, preferred_element_type=F32)
                dq_s[rows, sl] += jnp.dot(ds, kk, preferred_element_type=F32)

        @pl.when(i == j)
        def _():
            dk_s[...] = jnp.zeros(dk_s.shape, F32)
            dv_s[...] = jnp.zeros(dv_s.shape, F32)
            step(True)

        @pl.when(i > j)
        def _():
            step(False)

        @pl.when(i == nq - 1)
        def _():
            dk_ref[...] = dk_s[...].astype(BF16)
            dv_ref[...] = (dv_s[...] * LOG2E).astype(BF16)

        @pl.when(step_no == len(steps) - 1)
        def _():
            dq_ref[...] = dq_s[...].astype(BF16)

    qw, vw = 128 * ATT_HEADS, VD * ATT_HEADS
    qrow = lambda p, s, jt, it: (it[s], p)
    krow = lambda p, s, jt, it: (jt[s], p)
    col_spec = pl.BlockSpec((ATT_HEADS, tq, 1), lambda p, s, jt, it: (p, it[s], 0))
    grid_spec = pltpu.PrefetchScalarGridSpec(
        num_scalar_prefetch=2, grid=(groups, len(steps)),
        in_specs=[pl.BlockSpec((tq, qw), qrow), pl.BlockSpec((tq, qw), krow), pl.BlockSpec((tq, vw), krow),
                  pl.BlockSpec((tq, vw), qrow), col_spec, col_spec] + [ANY] * nb,
        out_specs=[pl.BlockSpec((t, qw), lambda p, s, jt, it: (0, p)), pl.BlockSpec((tq, qw), krow),
                   pl.BlockSpec((tq, vw), krow)] + [ANY] * nb,
        scratch_shapes=[pltpu.VMEM((t, qw), F32), pltpu.VMEM((tq, qw), F32), pltpu.VMEM((tq, vw), F32)]
        + ([pltpu.SemaphoreType.DMA((nb, 3))] * 2 if ride else []))
    outs = pl.pallas_call(
        body, name="attn_bwd_ride" if ride else "attn_bwd", grid_spec=grid_spec,
        out_shape=[_sds((t, QW), BF16), _sds((t, QW), BF16), _sds((t, HEADS * VD), BF16)]
        + [_sds((3,) + a.shape[1:], a.dtype) for a in ride],
        compiler_params=_cp("arbitrary", "arbitrary"))(j_of, i_of, qs, k, v, dya, lse, delta, *ride)
    return outs[0], outs[1], outs[2], list(outs[3:])


def _mla_prep_bwd(dqs, dk, dv, sv, p, l, depth, gb, tabs):
    za = sv["za"]
    t = za.shape[0]
    tm = min(ROW_TILE, t)

    def body(dq_ref, dk_ref, dv_ref, z_ref, gq_ref, gkv_ref, wuq_ref, wukv_ref, c_ref, s1_ref, s2_ref,
             dza_ref, dqp_ref, dkv_ref, dgq_ref, dgkv_ref):
        _acc_init(pl.program_id(0), dgq_ref, dgkv_ref)
        c, s1, s2 = c_ref[...], s1_ref[...], s2_ref[...]
        lane = lax.broadcasted_iota(jnp.int32, (tm, 128), 1)
        rope_lanes = (lane >= NOPE) & (lane < NOPE + ROPE)
        dkr = jnp.zeros((tm, 128), F32)
        for h in range(HEADS):
            sl = slice(128 * h, 128 * h + 128)
            dqp_ref[:, sl] = _rope_bwd(dq_ref[:, sl].astype(F32) * QSCALE, c, s1, s2).astype(BF16)
            dkh = dk_ref[:, sl]
            dkv_ref[:, sl] = dkh
            dkr = dkr + jnp.where(rope_lanes, dkh.astype(F32), 0.0)
        dkv_ref[:, QW:] = dv_ref[...]
        z = z_ref[...]
        dcq = lax.dot_general(dqp_ref[...], wuq_ref[...], NT, preferred_element_type=F32)
        dzq, dgq = _rms_bwd(z[:, :QR], gq_ref[...], dcq)
        dckv = lax.dot_general(dkv_ref[...], wukv_ref[...], NT, preferred_element_type=F32)
        dzkv, dgkv = _rms_bwd(z[:, QR:QR + KVR], gkv_ref[...], dckv)
        dgq_ref[...] += dgq
        dgkv_ref[...] += dgkv
        dza_ref[:, :QR] = dzq.astype(BF16)
        dza_ref[:, QR:QR + KVR] = dzkv.astype(BF16)
        dza_ref[:, QR + KVR:] = _rope_bwd(dkr, c, s1, s2).astype(BF16)

    dza, dqp, dkv, gb["q_norm_g"], gb["kv_norm_g"] = _pcall(
        body, "mla_prep_bwd", (t // tm,),
        [dqs, dk, dv, za, p["q_norm_g"], p["kv_norm_g"], p["w_uq"], p["w_ukv"], *tabs],
        [_row(tm, QW), _row(tm, QW), _row(tm, HEADS * VD), _row(tm, ZA), _lyr(l, 1, QR), _lyr(l, 1, KVR),
         _lyr(_wl(p["w_uq"], l), QR, QW), _lyr(_wl(p["w_ukv"], l), KVR, KVW), _row(tm, 128), _row(tm, 128), _row(tm, 128)],
        [_row(tm, ZA), _row(tm, QW), _row(tm, KVW), _lyr(l.g, 1, QR), _lyr(l.g, 1, KVR)],
        [_sds((t, ZA), BF16), _sds((t, QW), BF16), _sds((t, KVW), BF16), _sds((depth, 1, QR), F32),
         _sds((depth, 1, KVR), F32)], ("arbitrary",), prevs={3: gb.get("q_norm_g"), 4: gb.get("kv_norm_g")})
    return dza, dqp, dkv


def _in_proj_bwd(dza, dzb, dx2, sv, p, l, depth, gb):
    t = dx2.shape[0]
    tm = min(ROW_TILE, t)

    def body(dza_ref, dzb_ref, wa_ref, wb_ref, x_ref, dx2_ref, g_ref, dx_ref, dg_ref):
        _acc_init(pl.program_id(0), dg_ref)
        dh = (lax.dot_general(dza_ref[...], wa_ref[...], NT, preferred_element_type=F32)
              + lax.dot_general(dzb_ref[...], wb_ref[...], NT, preferred_element_type=F32))
        dx, dg = _rms_bwd(x_ref[...], g_ref[...], dh)
        dg_ref[...] += dg
        dx_ref[...] = dx2_ref[...] + dx

    dx, gb["mix_pre_g"] = _pcall(
        body, "in_proj_bwd", (t // tm,), [dza, dzb, p["w_in_a"], p["w_in_b"], sv["x"], dx2, p["mix_pre_g"]],
        [_row(tm, ZA), _row(tm, ZB), _lyr(_wl(p["w_in_a"], l), D, ZA), _lyr(_wl(p["w_in_b"], l), D, ZB), _row(tm, D), _row(tm, D), _lyr(l, 1, D)],
        [_row(tm, D), _lyr(l.g, 1, D)], [_sds((t, D), F32), _sds((depth, 1, D), F32)], ("arbitrary",),
        prevs={1: gb.get("mix_pre_g")})
    return dx


def _mm_tn(a, b, tn, name, l, depth, gb):
    t, k = a.shape
    n = b.shape[1]
    tt = min(ROW_TILE, t)

    def body(a_ref, b_ref, o_ref):
        _acc_init(pl.program_id(1), o_ref)
        o_ref[...] += lax.dot_general(a_ref[...], b_ref[...], TN, preferred_element_type=F32)

    gb[name] = _pcall(
        body, "d" + name, (n // tn, t // tt), [a, b],
        [pl.BlockSpec((tt, k), lambda j, s: (s, 0)), pl.BlockSpec((tt, tn), lambda j, s: (s, j))],
        pl.BlockSpec((None, k, tn), lambda j, s: (l.g, 0, j)), _sds((depth, k, n), F32), ("parallel", "arbitrary"),
        prevs={0: gb.get(name)})


def _dw_ffn(a, b, kind, l, depth, gb):
    t = a.shape[0]
    tt = min(ROW_TILE, t)
    nsteps = t // tt

    def body(a_ref, b_ref, o_ref, acc):
        s = pl.program_id(0)
        _acc_init(s, acc)
        acc[...] += lax.dot_general(a_ref[...], b_ref[...], TN, preferred_element_type=F32)

        @pl.when(s == nsteps - 1)
        def _():
            for k in range(4):
                o_ref[k] = acc[k * HP:(k + 1) * HP, :].astype(BF16)

    rows = lambda n: pl.BlockSpec((tt, n), lambda s: (s, 0))
    if kind == "down":
        name = "down"
        out_spec = pl.BlockSpec((4, None, HP, D), lambda s: (0, l.g, 0, 0))
        out_shape = _sds((4, depth, HP, D), BF16)
    else:
        which = 0 if kind == "gate" else 1
        name = "gu"
        out_spec = pl.BlockSpec((4, None, None, HP, D), lambda s: (0, l.g, which, 0, 0))
        out_shape = _sds((4, depth, 2, HP, D), BF16)
    gb[name] = _pcall(body, "dw_" + kind, (nsteps,), [a, b], [rows(DFFP), rows(D)], out_spec, out_shape, ("arbitrary",),
                      scratch=[pltpu.VMEM((DFFP, D), F32)], prevs={0: gb.get(name)})


def _ffn_views(bufs):
    return {"w_gu": bufs[1], "w_down": bufs[2].reshape(bufs[2].shape[:2] + (HP, D))}


def _layer_fwd(x, p, l, tabs, fetch):
    h1, za, zb = _in_proj(x, p, l)
    cqn, ckvn, qs, k, v = _mla_prep(za, p, l, tabs)
    ya, lse, bufs = _attn_fwd(qs, k, v, fetch)
    if fetch:
        p = {**p, **_ffn_views(bufs)}
    mix, yb, yc = _mixer_fwd(zb, ya, p, l)
    o, x2, h2 = _out_proj(mix, x, p, l)
    a, b, s = _ffn_up(h2, p, l)
    f, x3 = _ffn_down(s, x2, p, l)
    saved = dict(x=x, h1=h1, za=za, zb=zb, cqn=cqn, ckvn=ckvn, qs=qs, k=k, v=v, ya=ya, lse=lse, mix=mix, yb=yb, yc=yc,
                 o=o, x2=x2, h2=h2, a=a, b=b, s=s, f=f)
    return x3, saved, bufs if fetch else None


class _Layer(int):
    def __new__(cls, l, g):
        obj = int.__new__(cls, l)
        obj.g = g
        return obj


def _layer_bwd(dx3, p, sv, l, depth, gb, tabs, ride=()):
    df, da, db = _ffn_down_bwd(dx3, sv, p, l, depth, gb)
    _dw_ffn(sv["s"], df, "down", l, depth, gb)
    dx2 = _ffn_up_bwd(da, db, dx3, sv, p, l, depth, gb)
    _dw_ffn(da, sv["h2"], "gate", l, depth, gb)
    _dw_ffn(db, sv["h2"], "up", l, depth, gb)
    do, dmix = _out_proj_bwd(dx2, sv, p, l, depth, gb)
    _mm_tn(sv["mix"], do, D, "w_out", l, depth, gb)
    dya, dzb, delta = _mixer_bwd(dmix, sv, p, l, depth, gb)
    dqs, dk, dv, sent = _attn_bwd(sv["qs"], sv["k"], sv["v"], dya, sv["lse"], delta, ride)
    dza, dqp, dkv = _mla_prep_bwd(dqs, dk, dv, sv, p, l, depth, gb, tabs)
    _mm_tn(sv["cqn"], dqp, QW, "w_uq", l, depth, gb)
    _mm_tn(sv["ckvn"], dkv, KVW, "w_ukv", l, depth, gb)
    _mm_tn(sv["h1"], dza, ZA, "w_in_a", l, depth, gb)
    _mm_tn(sv["h1"], dzb, ZB, "w_in_b", l, depth, gb)
    return _in_proj_bwd(dza, dzb, dx2, sv, p, l, depth, gb), sent


def _rope_tables(positions):
    inv_freq = 1.0 / (ROPE_THETA ** (jnp.arange(0, ROPE // 2, dtype=F32) / (ROPE // 2)))
    ang = positions.astype(F32)[:, None] * inv_freq
    cos, sin = jnp.cos(ang), jnp.sin(ang)
    t = positions.shape[0]
    one, zero = jnp.ones((t, 64), F32), jnp.zeros((t, 16), F32)
    c = jnp.concatenate([one, cos, cos, one[:, :32]], axis=1)
    s1 = jnp.concatenate([zero, zero, zero, zero, -sin, zero, zero, zero], axis=1)
    s2 = jnp.concatenate([zero, zero, zero, zero, zero, sin, zero, zero], axis=1)
    return c, s1, s2


def _mixer_weight_params(full):
    w_in = full["w_in"]
    depth = w_in.shape[0]
    zpad = lambda n: jnp.zeros((depth, D, n), w_in.dtype)
    kv = full["w_ukv"].reshape(depth, KVR, HEADS, NOPE + VD)
    return {
        "w_in_a": jnp.concatenate([w_in[:, :, :640], zpad(64), w_in[:, :, 640:672], zpad(32)], axis=2),
        "w_in_b": w_in[:, :, 672:],
        "w_uq": jnp.pad(full["w_uq"].reshape(depth, QR, HEADS, NOPE + ROPE),
                        ((0, 0), (0, 0), (0, 0), (0, 32))).reshape(depth, QR, QW),
        "w_ukv": jnp.concatenate([jnp.pad(kv[..., :NOPE], ((0, 0), (0, 0), (0, 0), (0, 64))).reshape(depth, KVR, QW),
                                  kv[..., NOPE:].reshape(depth, KVR, HEADS * VD)], axis=2),
        "w_out": full["w_out"], "conv_w": full["conv_w"],
    }


def _small_params(w):
    p = {"w_sp": w["w_sp"], "b_sp": jnp.repeat(jnp.swapaxes(w["b_sp"], 1, 2), 64, axis=2)}
    for n in ("mix_pre_g", "mix_post_g", "ffn_pre_g", "ffn_post_g", "q_norm_g", "kv_norm_g", "sg_ln_g", "sg_ln_b",
              "out_norm_g"):
        p[n] = w[n][:, None, :]
    return p


def _natural_grads(gb):
    depth = gb["w_in_a"].shape[0]
    ga, kv = gb["w_in_a"], gb["w_ukv"]
    out = {
        "w_in": jnp.concatenate([ga[:, :, :640], ga[:, :, 704:736], gb["w_in_b"]], axis=2),
        "w_uq": gb["w_uq"].reshape(depth, QR, HEADS, 128)[..., :NOPE + ROPE].reshape(depth, QR, HEADS * (NOPE + ROPE)),
        "w_ukv": jnp.concatenate([kv[:, :, :QW].reshape(depth, KVR, HEADS, 128)[..., :NOPE],
                                  kv[:, :, QW:].reshape(depth, KVR, HEADS, VD)], axis=3).reshape(depth, KVR, -1),
        "b_sp": jnp.swapaxes(gb["b_sp_t"].reshape(depth, CHUNK, 4, 64).sum(axis=-1), 1, 2),
    }
    for n in ("w_out", "w_sp", "conv_w"):
        out[n] = gb[n]
    for n in ("mix_pre_g", "mix_post_g", "ffn_pre_g", "ffn_post_g", "q_norm_g", "kv_norm_g", "sg_ln_g", "sg_ln_b",
              "out_norm_g"):
        out[n] = gb[n][:, 0, :]
    return out


def _local_step(x, positions, target, small, mine, bufs, shard_shapes, fetch=True, front=None):
    depth = small["w_sp"].shape[0]
    tabs = _rope_tables(positions)
    ps = _small_params(small)
    saved, mixer_w = [], []
    for l in range(depth):
        mixer_w.append(_mixer_weight_params(_unpack_weights(bufs[0], l, shard_shapes)))
        p = {**ps, **mixer_w[l], **_ffn_views(bufs)}
        layers = [l + 1 if l + 1 < depth else None, l, l]
        x, sv, fetched = _layer_fwd(x, p, l, tabs, (mine, bufs, layers) if fetch else None)
        bufs = fetched or bufs
        saved.append(sv)
    dx, acc = _loss_head(x, target)
    loss = (0.5 / D) * jnp.sum(acc)
    gbs = [{} for _ in range(depth)]
    fronts, sent = [None] * depth, [None] * depth
    ride = ()
    for l in reversed(range(depth)):
        dx, got = _layer_bwd(dx, {**ps, **mixer_w[l], **_ffn_views(bufs)}, saved[l], _Layer(l, 0), 1, gbs[l], tabs, ride)
        if ride:
            sent[l + 1], ride = got, ()
        done = (_natural_grads(gbs[l]), gbs[l]["gu"], gbs[l]["down"])
        fronts[l] = front(l, *done) if front else done
        if front and l > 0:
            ride = tuple(fronts[l])
    return loss, dx, fronts, sent


def _place():
    x, y, c = lax.axis_index("x"), lax.axis_index("y"), lax.axis_index("c")
    chips = [(1 - x, y), (x, 1 - y), (1 - x, 1 - y)]
    return x, y, c, 2 * x + y, chips


def _remote(src, dst, send_sem, recv_sem, to):
    return pltpu.make_async_remote_copy(src_ref=src, dst_ref=dst, send_sem=send_sem, recv_sem=recv_sem, device_id=to,
                                        device_id_type=MESH_ID)


def _gather_ops(mine_refs, out_refs, sems, layers):
    send_sems, recv_sems, fsend_sems, frecv_sems = sems
    x, y, c, k, chips = _place()
    sib = (x, y, 1 - c)
    pairs = [(b, n) for n in range(3) for b in range(len(mine_refs)) if layers[b] is not None]

    def slot(n):
        return 2 * chips[n][0] + chips[n][1]

    def ici(b, n, dst_chip):
        return _remote(mine_refs[b].at[layers[b], c], out_refs[b].at[dst_chip, layers[b], c], send_sems.at[b, n],
                       recv_sems.at[b, n], (*chips[n], c))

    def d2d(b, n, half):
        piece = out_refs[b].at[slot(n), layers[b], half]
        return _remote(piece, piece, fsend_sems.at[b, n], frecv_sems.at[b, n], sib)

    def start():
        for b, n in pairs:
            ici(b, n, k).start()

    def hand_over():
        for b, n in pairs:
            ici(b, n, slot(n)).wait_recv()
            d2d(b, n, c).start()

    def drain():
        for b, n in pairs:
            d2d(b, n, 1 - c).wait_recv()
        for b, n in pairs:
            ici(b, n, k).wait_send()
            d2d(b, n, c).wait_send()

    return start, hand_over, drain


def _gather_first_layer(mine):
    nb = len(mine)

    def body(*refs):
        start, hand_over, drain = _gather_ops(refs[:nb], refs[nb:2 * nb], refs[2 * nb:], [0] + [None] * (nb - 1))
        start()
        hand_over()
        drain()

    return pl.pallas_call(
        body, name="gather_first_layer", in_specs=[ANY] * nb, out_specs=[ANY] * nb,
        out_shape=[_sds((4,) + a.shape, a.dtype) for a in mine],
        scratch_shapes=[pltpu.SemaphoreType.DMA((nb, 3))] * 4)(*mine)


def _swap_halves(bigs, wholes=()):
    nb, n = len(bigs), len(bigs) + len(wholes)

    def body(*refs):
        src, dst = refs[:n], refs[n:2 * n]
        send_sems, recv_sems = refs[2 * n:]
        x, y, c, _, _ = _place()
        sib = (x, y, 1 - c)
        cps = [_remote(src[b].at[:, 1 - c] if b < nb else src[b], dst[b], send_sems.at[b], recv_sems.at[b], sib)
               for b in range(n)]
        for cp in cps:
            cp.start()
        for cp in cps:
            cp.wait()

    return pl.pallas_call(
        body, name="swap_halves", in_specs=[ANY] * n, out_specs=[ANY] * n,
        out_shape=[_sds((4,) + a.shape[2:], a.dtype) for a in bigs] + [_sds(a.shape, a.dtype) for a in wholes],
        scratch_shapes=[pltpu.SemaphoreType.DMA((n,))] * 2)(*bigs, *wholes)


def _sum_tile(r):
    return max(cand for cand in range(16, 641, 16) if r % cand == 0)


def _pair_sum(big, rbig, c):
    _, _, r, w = big.shape
    tr = _sum_tile(r)

    def body(c_ref, big_ref, rbig_ref, p_ref):
        p_ref[...] = (big_ref[...].astype(F32) + rbig_ref[...].astype(F32)).astype(BF16)

    grid_spec = pltpu.PrefetchScalarGridSpec(
        num_scalar_prefetch=1, grid=(4, r // tr),
        in_specs=[pl.BlockSpec((None, None, tr, w), lambda j, i, cr: (j, cr[0], i, 0)),
                  pl.BlockSpec((None, tr, w), lambda j, i, cr: (j, i, 0))],
        out_specs=pl.BlockSpec((None, tr, w), lambda j, i, cr: (j, i, 0)))
    return pl.pallas_call(body, name="pair_sum", grid_spec=grid_spec, out_shape=_sds((4, r, w), BF16),
                          compiler_params=_cp("parallel", "parallel"))(c, big, rbig)


def _small_sum(parts):
    n, ns, _ = parts.shape

    def body(p_ref, o_ref):
        s = p_ref[0]
        for j in range(1, n):
            s = s + p_ref[j]
        o_ref[...] = s

    return pl.pallas_call(body, name="small_sum", out_shape=_sds((ns, 128), F32))(parts)


def _exchange_ops(p_refs, rb_refs, sems, small=None):
    send_sems, recv_sems = sems[0], sems[1]
    nb = len(p_refs)
    x, y, c, k, chips = _place()

    def copies(landing):
        out = []
        for n, (cx, cy) in enumerate(chips):
            to, kj = (cx, cy, c), 2 * cx + cy
            for b in range(nb):
                out.append(_remote(p_refs[b].at[k if landing else kj], rb_refs[b].at[n], send_sems.at[b, n],
                                   recv_sems.at[b, n], to))
            if small:
                out.append(_remote(small[0], small[1].at[kj if landing else k], send_sems.at[nb, n], recv_sems.at[nb, n], to))
        return out

    def local():
        return pltpu.make_async_copy(small[0], small[1].at[k], sems[2])

    def start():
        if small:
            local().start()
        for cp in copies(False):
            cp.start()

    def finish():
        for cp in copies(True):
            cp.wait_recv()
        for cp in copies(False):
            cp.wait_send()
        if small:
            local().wait()

    return start, finish


def _chip_exchange(ps, small):
    nb = len(ps)
    ns = small.shape[0]

    def body(*refs):
        start, finish = _exchange_ops(refs[:nb], refs[nb + 1:2 * nb + 1], refs[2 * nb + 2:], (refs[nb], refs[2 * nb + 1]))
        start()
        finish()

    return pl.pallas_call(
        body, name="chip_exchange", in_specs=[ANY] * (nb + 1), out_specs=[ANY] * (nb + 1),
        out_shape=[_sds((3,) + a.shape[1:], a.dtype) for a in ps] + [_sds((4, ns, 128), small.dtype)],
        scratch_shapes=[pltpu.SemaphoreType.DMA((nb + 1, 3))] * 2 + [pltpu.SemaphoreType.DMA(())])(*ps, small)


def _chip_sum(p, rb, chip):
    _, r, w = p.shape
    tr = _sum_tile(r)

    def body(k_ref, p_ref, rb_ref, o_ref):
        acc = p_ref[...].astype(F32)
        for j in range(3):
            acc = acc + rb_ref[j].astype(F32)
        o_ref[...] = acc

    grid_spec = pltpu.PrefetchScalarGridSpec(
        num_scalar_prefetch=1, grid=(r // tr,),
        in_specs=[pl.BlockSpec((None, tr, w), lambda i, kr: (kr[0], i, 0)), pl.BlockSpec((3, tr, w), lambda i, kr: (0, i, 0))],
        out_specs=pl.BlockSpec((tr, w), lambda i, kr: (i, 0)))
    return pl.pallas_call(body, name="chip_sum", grid_spec=grid_spec, out_shape=_sds((r, w), F32),
                          compiler_params=_cp("parallel"))(chip, p, rb)


def _send_to_sibling(reds):
    nb = len(reds)

    def body(*refs):
        red_refs, out_refs = refs[:nb], refs[nb:2 * nb]
        send_sems, recv_sems = refs[2 * nb:]
        x, y, c, _, _ = _place()
        cps = [_remote(red_refs[b], out_refs[b], send_sems.at[b], recv_sems.at[b], (x, y, 1 - c)) for b in range(nb)]
        for cp in cps:
            cp.start()
        for cp in cps:
            cp.wait()

    return pl.pallas_call(
        body, name="send_to_sibling", in_specs=[ANY] * nb, out_specs=[ANY] * nb,
        out_shape=[_sds(a.shape, a.dtype) for a in reds], scratch_shapes=[pltpu.SemaphoreType.DMA((nb,))] * 2)(*reds)


def _adam_math(w, g, m, v):
    nm = ADAM_B1 * m + (1.0 - ADAM_B1) * g
    nv = ADAM_B2 * v + (1.0 - ADAM_B2) * (g * g)
    m_hat = nm / (1.0 - ADAM_B1 ** ADAM_STEP)
    v_hat = nv / (1.0 - ADAM_B2 ** ADAM_STEP)
    return -ADAM_LR * (m_hat / (jnp.sqrt(v_hat) + ADAM_EPS) + ADAM_WD * w), nm, nv


def _adamw_shard(w, m, v, srcs, c, name, owner=0, split=None):
    depth, r, n = w.shape
    unit = math.gcd(split, r - split) if split else r
    tr = max(cand for cand in range(8, min(unit, 256) + 1, 8) if unit % cand == 0)
    sb = split // tr if split else None
    npad = srcs[0][0].shape[-1]

    def body(c_ref, w_ref, m_ref, v_ref, *refs):
        g_ref, d_ref, nm_ref, nv_ref = refs[2 * depth:]
        l, i = pl.program_id(0), pl.program_id(1)
        half = (i >= sb).astype(jnp.int32) if split else owner
        mine = c_ref[0] == half
        g = jnp.where(mine, refs[0][...], refs[1][...])
        for b in range(1, depth):
            g = jnp.where(l == b, jnp.where(mine, refs[2 * b][...], refs[2 * b + 1][...]), g)
        g = g[:, :n]
        g_ref[...] = g
        d_ref[...], nm_ref[...], nv_ref[...] = _adam_math(w_ref[...], g, m_ref[...], v_ref[...])

    def source(b):
        def index(l, i, cr):
            blk = jnp.where(i >= sb, i - sb, i) if split else i
            return (jnp.where(l == b, blk, 0), 0)
        return pl.BlockSpec((tr, npad), index)

    blk = pl.BlockSpec((None, tr, n), lambda l, i, cr: (l, i, 0))
    grid_spec = pltpu.PrefetchScalarGridSpec(
        num_scalar_prefetch=1, grid=(depth, r // tr),
        in_specs=[blk] * 3 + [source(b) for b in range(depth) for _ in range(2)], out_specs=[blk] * 4)
    return pl.pallas_call(body, name=name, grid_spec=grid_spec, out_shape=[_sds(w.shape, F32)] * 4,
                          compiler_params=_cp("parallel", "parallel"))(c, w, m, v, *[a for pair in srcs for a in pair])


def _pad_ffn_shards(w_gate, w_up, w_down):
    depth = w_gate.shape[0]
    hr = HP // 2

    def gu_body(g_ref, u_ref, o_ref):
        for which, ref in enumerate((g_ref, u_ref)):
            o_ref[which, 0:HS, :] = ref[...].astype(BF16)
            o_ref[which, HS:HP, :] = jnp.zeros((HP - HS, D), BF16)

    blk = pl.BlockSpec((None, HS, D), lambda l: (l, 0, 0))
    gu = pl.pallas_call(
        gu_body, name="pad_gate_up", grid=(depth,), in_specs=[blk, blk],
        out_specs=pl.BlockSpec((None, 2, HP, D), lambda l: (l, 0, 0, 0)),
        out_shape=_sds((depth, 2, HP, D), BF16), compiler_params=_cp("parallel"))(w_gate, w_up)

    def down_body(w_ref, o_ref):
        o_ref[0] = w_ref[0:hr, :].astype(BF16)
        o_ref[1, 0:HS - hr, :] = w_ref[hr:HS, :].astype(BF16)
        o_ref[1, HS - hr:hr, :] = jnp.zeros((HP - HS, D), BF16)

    down = pl.pallas_call(
        down_body, name="pad_down", grid=(depth,), in_specs=[pl.BlockSpec((None, HS, D), lambda l: (l, 0, 0))],
        out_specs=pl.BlockSpec((None, 2, hr, D), lambda l: (l, 0, 0, 0)),
        out_shape=_sds((depth, 2, hr, D), BF16), compiler_params=_cp("parallel"))(w_down)
    return gu, down


def _adamw_small(w, g, m, v):
    r = w.shape[0]
    tr = max(cand for cand in range(8, 513, 8) if r % cand == 0)

    def body(w_ref, g_ref, m_ref, v_ref, d_ref, nm_ref, nv_ref):
        d_ref[...], nm_ref[...], nv_ref[...] = _adam_math(w_ref[...], g_ref[...], m_ref[...], v_ref[...])

    blk = pl.BlockSpec((tr, 128), lambda i: (i, 0))
    return pl.pallas_call(body, name="adamw_small", grid=(r // tr,), in_specs=[blk] * 4, out_specs=[blk] * 3,
                          out_shape=[_sds(w.shape, F32)] * 3, compiler_params=_cp("parallel"))(w, g, m, v)


def _to_pack(a, name):
    depth = a.shape[0]
    if name in ROW_SHARDED:
        return jnp.swapaxes(a.reshape(depth, 4, -1, D), 0, 1)
    return jnp.transpose(a.reshape(depth, a.shape[1], 4, a.shape[2] // 4), (2, 0, 1, 3)).reshape(4, depth, -1, D)


def _pack_rows(parts, lead, dtype, tail=None):
    pieces, at = [], 0
    for n, off, rows in PACK:
        if off > at:
            pieces.append(jnp.zeros(lead + (off - at, D), dtype))
        pieces.append(parts[n].astype(dtype))
        at = off + rows
    if tail is not None:
        pieces.append(tail)
        at += tail.shape[-2]
    pieces.append(jnp.zeros(lead + (PACK_ROWS - at, D), dtype))
    return jnp.concatenate(pieces, axis=len(lead))


def _pack_weight_shards(sh):
    depth = sh["w_in"].shape[0]
    parts = {n: sh[n].reshape(depth, rows, D) for n, _, rows in PACK}
    conv = lax.bitcast_convert_type(sh["conv_w"].reshape(depth, 3 * 64), BF16).reshape(depth, 1, 384)
    flat = _pack_rows(parts, (depth,), BF16, tail=jnp.pad(conv, ((0, 0), (0, 0), (0, D - 384))))
    return flat.reshape(depth, 2, PACK_ROWS // 2, D)


def _unpack_weights(gathered, l, shard_shapes):
    depth = 1
    flat = gathered[:, l].reshape(4, 1, PACK_ROWS, D)
    full = {}
    for n, off, rows in PACK:
        shp = shard_shapes[n][1:]
        piece = flat[:, :, off:off + rows, :].reshape((4, depth) + shp)
        if n in ROW_SHARDED:
            full[n] = jnp.transpose(piece, (1, 0, 2, 3)).reshape(depth, 4 * shp[0], shp[1])
        else:
            full[n] = jnp.transpose(piece, (1, 2, 0, 3)).reshape(depth, shp[0], 4 * shp[1])
    conv = lax.bitcast_convert_type(flat[:, :, CONV_ROW, :384].reshape(4, depth, 192, 2), F32)
    full["conv_w"] = jnp.transpose(conv.reshape(4, depth, 3, 64), (1, 2, 0, 3)).reshape(depth, 3, CVW)
    return full


def _pack_grad_shards(g):
    depth = g["w_in"].shape[0]
    return _pack_rows({n: _to_pack(g[n], n) for n, _, _ in PACK}, (4, depth), BF16)


def _pack_small(arrs, names_shapes, depth):
    flat = jnp.concatenate([arrs[n].reshape(depth, -1) for n, _ in names_shapes], axis=1).reshape(-1)
    rows = -(-flat.shape[0] // 1024) * 8
    return jnp.pad(flat, (0, rows * 128 - flat.shape[0])).reshape(rows, 128)


def _unpack_small(packed, names_shapes, depth):
    per_layer = sum(math.prod(s) for _, s in names_shapes)
    flat = packed.reshape(-1)[:depth * per_layer].reshape(depth, per_layer)
    out, off = {}, 0
    for n, s in names_shapes:
        size = math.prod(s)
        out[n] = flat[:, off:off + size].reshape((depth,) + s)
        off += size
    return out


def kernel(x, positions, mix_pre_g, mix_post_g, ffn_pre_g, ffn_post_g, w_in, q_norm_g, w_uq, kv_norm_g, w_ukv, sg_ln_g, sg_ln_b, w_sp, b_sp, conv_w, out_norm_g, w_out, w_gate, w_up, w_down, loss_target, m_mix_pre_g, m_mix_post_g, m_ffn_pre_g, m_ffn_post_g, m_w_in, m_q_norm_g, m_w_uq, m_kv_norm_g, m_w_ukv, m_sg_ln_g, m_sg_ln_b, m_w_sp, m_b_sp, m_conv_w, m_out_norm_g, m_w_out, m_w_gate, m_w_up, m_w_down, v_mix_pre_g, v_mix_post_g, v_ffn_pre_g, v_ffn_post_g, v_w_in, v_q_norm_g, v_w_uq, v_kv_norm_g, v_w_ukv, v_sg_ln_g, v_sg_ln_b, v_w_sp, v_b_sp, v_conv_w, v_out_norm_g, v_w_out, v_w_gate, v_w_up, v_w_down):
    w = dict(mix_pre_g=mix_pre_g, mix_post_g=mix_post_g, ffn_pre_g=ffn_pre_g, ffn_post_g=ffn_post_g, w_in=w_in,
             q_norm_g=q_norm_g, w_uq=w_uq, kv_norm_g=kv_norm_g, w_ukv=w_ukv, sg_ln_g=sg_ln_g, sg_ln_b=sg_ln_b, w_sp=w_sp,
             b_sp=b_sp, conv_w=conv_w, out_norm_g=out_norm_g, w_out=w_out, w_gate=w_gate, w_up=w_up, w_down=w_down)
    m = dict(mix_pre_g=m_mix_pre_g, mix_post_g=m_mix_post_g, ffn_pre_g=m_ffn_pre_g, ffn_post_g=m_ffn_post_g, w_in=m_w_in,
             q_norm_g=m_q_norm_g, w_uq=m_w_uq, kv_norm_g=m_kv_norm_g, w_ukv=m_w_ukv, sg_ln_g=m_sg_ln_g, sg_ln_b=m_sg_ln_b,
             w_sp=m_w_sp, b_sp=m_b_sp, conv_w=m_conv_w, out_norm_g=m_out_norm_g, w_out=m_w_out, w_gate=m_w_gate,
             w_up=m_w_up, w_down=m_w_down)
    v = dict(mix_pre_g=v_mix_pre_g, mix_post_g=v_mix_post_g, ffn_pre_g=v_ffn_pre_g, ffn_post_g=v_ffn_post_g, w_in=v_w_in,
             q_norm_g=v_q_norm_g, w_uq=v_w_uq, kv_norm_g=v_kv_norm_g, w_ukv=v_w_ukv, sg_ln_g=v_sg_ln_g, sg_ln_b=v_sg_ln_b,
             w_sp=v_w_sp, b_sp=v_b_sp, conv_w=v_conv_w, out_norm_g=v_out_norm_g, w_out=v_w_out, w_gate=v_w_gate,
             w_up=v_w_up, w_down=v_w_down)
    depth = w_in.shape[0]
    c = lax.axis_index("c").astype(jnp.int32).reshape(1)
    chip = (2 * lax.axis_index("x") + lax.axis_index("y")).astype(jnp.int32)

    mine = [_pack_weight_shards(w), *_pad_ffn_shards(jnp.swapaxes(w_gate, 1, 2), jnp.swapaxes(w_up, 1, 2), w_down)]
    bufs = [lax.dynamic_update_slice(g, a[None], (chip,) + (0,) * a.ndim)
            for g, a in zip(_gather_first_layer(mine), mine)]

    small_grads = [None] * depth
    small_pair = []

    def front(l, grads, g_gu, g_down):
        small_grads[l] = grads
        bigs = [_pack_grad_shards(grads).reshape(4, 2, PACK_ROWS // 2, D), g_gu.reshape(4, 2, HP, D),
                g_down.reshape(4, 2, HP // 2, D)]
        if l > 0:
            rbigs = _swap_halves(bigs)
        else:
            small = _pack_small({n: jnp.concatenate([g[n] for g in small_grads]) for n, _ in SMALL}, SMALL, depth)
            *rbigs, rsmall = _swap_halves(bigs, [small])
            small_pair.append(_small_sum(jnp.stack([small, rsmall])))
        return [_pair_sum(a, r, c) for a, r in zip(bigs, rbigs)]

    loss, dx, ps, sent = _local_step(x[0], positions[0], loss_target[0], w, mine, bufs,
                                     {n: w[n].shape for n, _, _ in PACK}, front=front)
    loss = lax.psum(loss, ("x", "y", "c"))

    *sent[0], rs = _chip_exchange(ps[0], small_pair[0])
    own = [[_chip_sum(p, rb, chip.reshape(1)) for p, rb in zip(ps[l], sent[l])] for l in range(depth)]
    other = [_send_to_sibling(o) for o in own]
    g_small = _unpack_small(_small_sum(rs), SMALL, depth)
    g_small["conv_w"] = lax.dynamic_slice_in_dim(g_small["conv_w"], chip * 64, 64, axis=2)

    gw, delta, new_m, new_v = dict(g_small), {}, {}, {}

    def adam(n, srcs, turned=False, **where):
        view = (lambda a: jnp.swapaxes(a, 1, 2)) if turned else (lambda a: a)
        outs = _adamw_shard(view(w[n]), view(m[n]), view(v[n]), srcs, c, "adamw_" + n, **where)
        gw[n], delta[n], new_m[n], new_v[n] = [view(o) for o in outs]

    first = c[0] == 0
    packs = [jnp.concatenate([jnp.where(first, o[0], s[0]), jnp.where(first, s[0], o[0])]) for o, s in zip(own, other)]
    for n, off, rows in PACK:
        pieces = [pk[off:off + rows, :].reshape(w[n].shape[1:]) for pk in packs]
        adam(n, [(pc, pc) for pc in pieces], owner=0)
    adam("w_gate", [(o[1], s[1]) for o, s in zip(own, other)], turned=True, owner=0)
    adam("w_up", [(o[1], s[1]) for o, s in zip(own, other)], turned=True, owner=1)
    adam("w_down", [(o[2], s[2]) for o, s in zip(own, other)], split=HP // 2)
    small_local = tuple((n, w[n].shape[1:]) for n, _ in SMALL)
    d_, m_, v_ = _adamw_small(_pack_small(w, small_local, depth), _pack_small(gw, small_local, depth),
                              _pack_small(m, small_local, depth), _pack_small(v, small_local, depth))
    delta.update(_unpack_small(d_, small_local, depth))
    new_m.update(_unpack_small(m_, small_local, depth))
    new_v.update(_unpack_small(v_, small_local, depth))

    return (loss, dx[None], *[gw[n] for n in WEIGHTS], *[delta[n] for n in WEIGHTS], *[new_m[n] for n in WEIGHTS],
            *[new_v[n] for n in WEIGHTS])
```

```python
import math

import jax
import jax.numpy as jnp
from jax import lax
from jax.experimental import pallas as pl
from jax.experimental.pallas import tpu as pltpu

F32 = jnp.float32
BF16 = jnp.bfloat16

D = 1024
HEADS = 8
NOPE = 64
ROPE = 32
VD = 64
QR = 384
KVR = 256
SGW = 256
CVW = 256
CHUNK = 128
DFF = 2816
EPS = 1e-6
ROPE_THETA = 10000.0
LOG2E = 1.4426950408889634
LN2 = 0.6931471805599453
QSCALE = (NOPE + ROPE) ** -0.5 * LOG2E
ZA = 768
ZB = 1280
QW = HEADS * 128
KVW = HEADS * 128 + HEADS * VD
NEG = -1e30
GC0 = 0.7978845608028654
GC1 = 0.044715

ADAM_LR = 0.001
ADAM_B1 = 0.9
ADAM_B2 = 0.999
ADAM_EPS = 1e-08
ADAM_WD = 0.01
ADAM_STEP = 10

V7X_VMEM_LIMIT = 52 * 1024 * 1024
ROW_TILE = 512
ATT_TILE = 512
ATT_HEADS = 4

NT = (((1,), (1,)), ((), ()))
TN = (((0,), (0,)), ((), ()))

HS = DFF // 4
HP = 768
DFFP = 4 * HP

PACK = (("w_in", 0, 488), ("w_out", 512, 256), ("w_ukv", 768, 64), ("w_uq", 832, 72))
CONV_ROW = 904
PACK_ROWS = 928
ROW_SHARDED = ("w_out",)
SMALL = (("mix_pre_g", (D,)), ("mix_post_g", (D,)), ("ffn_pre_g", (D,)), ("ffn_post_g", (D,)), ("q_norm_g", (QR,)),
         ("kv_norm_g", (KVR,)), ("sg_ln_g", (SGW,)), ("sg_ln_b", (SGW,)), ("w_sp", (4, CHUNK, CHUNK)), ("b_sp", (4, CHUNK)),
         ("conv_w", (3, CVW)), ("out_norm_g", (D,)))
WEIGHTS = ["mix_pre_g", "mix_post_g", "ffn_pre_g", "ffn_post_g", "w_in", "q_norm_g", "w_uq", "kv_norm_g", "w_ukv", "sg_ln_g",
           "sg_ln_b", "w_sp", "b_sp", "conv_w", "out_norm_g", "w_out", "w_gate", "w_up", "w_down"]

MESH_ID = pl.DeviceIdType.MESH
ANY = pl.BlockSpec(memory_space=pl.ANY)


def _cp(*sem):
    return pltpu.CompilerParams(dimension_semantics=sem, vmem_limit_bytes=V7X_VMEM_LIMIT)


def _sds(shape, dtype):
    return jax.ShapeDtypeStruct(shape, dtype)


def _row(tm, n):
    return pl.BlockSpec((tm, n), lambda i: (i, 0))


def _lyr(l, *shape):
    return pl.BlockSpec((None,) + shape, lambda *_: (l,) + (0,) * len(shape))


def _wl(a, l):
    return 0 if a.shape[0] == 1 else l


def _pcall(body, name, grid, ins, in_specs, out_specs, out_shape, sem, scratch=(), prevs=None):
    prevs = {k: v for k, v in (prevs or {}).items() if v is not None}
    order = sorted(prevs)
    n_in = len(ins)

    def wrapped(*refs):
        return body(*refs[:n_in], *refs[n_in + len(order):])

    return pl.pallas_call(
        wrapped, name=name, grid=grid, in_specs=list(in_specs) + [ANY] * len(order), out_specs=out_specs,
        out_shape=out_shape, scratch_shapes=list(scratch),
        input_output_aliases={n_in + i: k for i, k in enumerate(order)},
        compiler_params=_cp(*sem))(*ins, *[prevs[k] for k in order])


def _rms(x, g):
    r = lax.rsqrt(jnp.mean(x * x, axis=-1, keepdims=True) + EPS)
    return x * r * g


def _rms_bwd(x, g, dy):
    r = lax.rsqrt(jnp.mean(x * x, axis=-1, keepdims=True) + EPS)
    xh = x * r
    dg = jnp.sum(dy * xh, axis=0, keepdims=True)
    dxh = dy * g
    dx = r * (dxh - xh * jnp.mean(dxh * xh, axis=-1, keepdims=True))
    return dx, dg


def _sigmoid(x):
    return 0.5 * jnp.tanh(0.5 * x) + 0.5


def _gelu(x):
    return 0.5 * x * (1.0 + jnp.tanh(GC0 * (x + GC1 * x * x * x)))


def _gelu_grad(x):
    t = jnp.tanh(GC0 * (x + GC1 * x * x * x))
    return 0.5 * (1.0 + t) + 0.5 * x * (1.0 - t * t) * GC0 * (1.0 + 3.0 * GC1 * x * x)


def _rope(xb, c, s1, s2):
    return xb * c + pltpu.roll(xb, 112, 1) * s1 + pltpu.roll(xb, 16, 1) * s2


def _rope_bwd(dy, c, s1, s2):
    return dy * c + pltpu.roll(dy * s1, 16, 1) + pltpu.roll(dy * s2, 112, 1)


def _group_masks(shape):
    lane = lax.broadcasted_iota(jnp.int32, shape, 1)
    return [(lane >= 64 * g) & (lane < 64 * g + 64) for g in range(shape[1] // 64)]


def _group_mean(v, masks):
    out = jnp.zeros_like(v)
    for m in masks:
        s = jnp.sum(jnp.where(m, v, 0.0), axis=-1, keepdims=True) * (1.0 / 64.0)
        out = jnp.where(m, s, out)
    return out


def _pick_row(blk, idx):
    row = lax.broadcasted_iota(jnp.int32, blk.shape, 0)
    return jnp.sum(jnp.where(row == idx, blk, 0.0), axis=0, keepdims=True)


def _shift_down(y, k, first_rows):
    out = pltpu.roll(y, k, 0)
    row = lax.broadcasted_iota(jnp.int32, y.shape, 0)
    for idx in range(k):
        out = jnp.where(row == idx, first_rows[idx], out)
    return out


def _shift_up(y, k, last_rows):
    n = y.shape[0]
    out = pltpu.roll(y, n - k, 0)
    row = lax.broadcasted_iota(jnp.int32, y.shape, 0)
    for idx in range(k):
        out = jnp.where(row == n - k + idx, last_rows[idx], out)
    return out


def _tril_mask():
    r = lax.broadcasted_iota(jnp.int32, (CHUNK, CHUNK), 0)
    c = lax.broadcasted_iota(jnp.int32, (CHUNK, CHUNK), 1)
    return r >= c


def _sgu_forward(zu, zv, g_ln, b_ln, wc_bf, bsp, masks, cmasks):
    u = _gelu(zu)
    vv = _gelu(zv)
    mu = _group_mean(vv, masks)
    dv = vv - mu
    rs = lax.rsqrt(_group_mean(dv * dv, masks) + EPS)
    xh = dv * rs
    vn = xh * g_ln + b_ln
    chunks = []
    for ci in range(zu.shape[0] // CHUNK):
        vc = vn[ci * CHUNK:(ci + 1) * CHUNK, :]
        acc = bsp
        for g in range(4):
            acc = acc + jnp.dot(wc_bf[g], jnp.where(cmasks[g], vc, 0.0).astype(BF16), preferred_element_type=F32)
        chunks.append(acc)
    mixed = jnp.concatenate(chunks, axis=0) if len(chunks) > 1 else chunks[0]
    return u, vv, xh, rs, vn, mixed


def _conv_forward(gc, hh, prev_gc, prev_hh, first_tile, cw):
    yv = gc * hh
    prev = jnp.where(first_tile, 0.0, prev_gc * prev_hh)
    p6, p7 = _pick_row(prev, 6), _pick_row(prev, 7)
    sh1 = _shift_down(yv, 1, [p7])
    sh2 = _shift_down(yv, 2, [p6, p7])
    conv = sh2 * cw[0:1, :] + sh1 * cw[1:2, :] + yv * cw[2:3, :]
    return yv, sh1, sh2, conv


def _acc_init(step, *refs):
    @pl.when(step == 0)
    def _():
        for r in refs:
            r[...] = jnp.zeros(r.shape, r.dtype)


def _in_proj(x, p, l):
    t = x.shape[0]
    tm = min(ROW_TILE, t)

    def body(x_ref, g_ref, wa_ref, wb_ref, h_ref, za_ref, zb_ref):
        h = _rms(x_ref[...], g_ref[...]).astype(BF16)
        h_ref[...] = h
        za_ref[...] = jnp.dot(h, wa_ref[...], preferred_element_type=F32)
        zb_ref[...] = jnp.dot(h, wb_ref[...], preferred_element_type=F32)

    return _pcall(
        body, "in_proj", (t // tm,), [x, p["mix_pre_g"], p["w_in_a"], p["w_in_b"]],
        [_row(tm, D), _lyr(l, 1, D), _lyr(_wl(p["w_in_a"], l), D, ZA), _lyr(_wl(p["w_in_b"], l), D, ZB)],
        [_row(tm, D), _row(tm, ZA), _row(tm, ZB)],
        [_sds((t, D), BF16), _sds((t, ZA), F32), _sds((t, ZB), F32)], ("parallel",))


def _mla_prep(za, p, l, tabs):
    t = za.shape[0]
    tm = min(ROW_TILE, t)

    def body(z_ref, gq_ref, gkv_ref, wuq_ref, wukv_ref, c_ref, s1_ref, s2_ref, cq_ref, ckv_ref, q_ref, k_ref, v_ref):
        z = z_ref[...]
        cq = _rms(z[:, :QR], gq_ref[...]).astype(BF16)
        ckv = _rms(z[:, QR:QR + KVR], gkv_ref[...]).astype(BF16)
        cq_ref[...] = cq
        ckv_ref[...] = ckv
        c, s1, s2 = c_ref[...], s1_ref[...], s2_ref[...]
        kr = _rope(z[:, QR + KVR:], c, s1, s2)
        q = jnp.dot(cq, wuq_ref[...], preferred_element_type=F32)
        kv = jnp.dot(ckv, wukv_ref[...], preferred_element_type=F32)
        for h in range(HEADS):
            sl = slice(128 * h, 128 * h + 128)
            q_ref[:, sl] = (_rope(q[:, sl], c, s1, s2) * QSCALE).astype(BF16)
            k_ref[:, sl] = (kv[:, sl] + kr).astype(BF16)
        v_ref[...] = kv[:, QW:].astype(BF16)

    return _pcall(
        body, "mla_prep", (t // tm,), [za, p["q_norm_g"], p["kv_norm_g"], p["w_uq"], p["w_ukv"], *tabs],
        [_row(tm, ZA), _lyr(l, 1, QR), _lyr(l, 1, KVR), _lyr(_wl(p["w_uq"], l), QR, QW), _lyr(_wl(p["w_ukv"], l), KVR, KVW),
         _row(tm, 128), _row(tm, 128), _row(tm, 128)],
        [_row(tm, QR), _row(tm, KVR), _row(tm, QW), _row(tm, QW), _row(tm, HEADS * VD)],
        [_sds((t, QR), BF16), _sds((t, KVR), BF16), _sds((t, QW), BF16), _sds((t, QW), BF16),
         _sds((t, HEADS * VD), BF16)], ("parallel",))


def _att_tile(t):
    return min(ATT_TILE, max(t // 2, 128))


def _causal_keep(tq, i, j):
    row = lax.broadcasted_iota(jnp.int32, (tq, tq), 0) + i * tq
    col = lax.broadcasted_iota(jnp.int32, (tq, tq), 1) + j * tq
    return col <= row


def _attn_fwd(qs, k, v, fetch=None):
    t = qs.shape[0]
    tq = _att_tile(t)
    nq = t // tq
    rep = tq // 128
    groups = HEADS // ATT_HEADS
    mine, bufs, fetch_layer = fetch if fetch else ((), (), None)
    nb = len(mine)

    steps = [(i, j) for i in range(nq) for j in range(i + 1)]
    i_of = jnp.asarray([s[0] for s in steps], jnp.int32)
    j_of = jnp.asarray([s[1] for s in steps], jnp.int32)

    def body(i_ref, j_ref, q_ref, k_ref, v_ref, *refs):
        o_ref, lse_ref = refs[2 * nb:2 * nb + 2]
        m_s, l_s, acc_s = refs[3 * nb + 2:3 * nb + 5]
        step_no = pl.program_id(1)
        i, j = i_ref[step_no], j_ref[step_no]
        if fetch:
            start, hand_over, drain = _gather_ops(refs[:nb], refs[2 * nb + 2:3 * nb + 2], refs[3 * nb + 5:], fetch_layer)
            pr = pl.program_id(0)
            pl.when((pr == 0) & (step_no == 0))(start)
            pl.when((pr == groups - 1) & (step_no == 3 * len(steps) // 4))(hand_over)
            pl.when((pr == groups - 1) & (step_no == len(steps) - 1))(drain)

        @pl.when(j == 0)
        def _():
            m_s[...] = jnp.full(m_s.shape, NEG, F32)
            l_s[...] = jnp.zeros(l_s.shape, F32)
            acc_s[...] = jnp.zeros(acc_s.shape, F32)

        def step(masked):
            keep = _causal_keep(tq, i, j) if masked else None
            for hh in range(ATT_HEADS):
                sl = slice(128 * hh, 128 * hh + 128)
                vv = v_ref[:, 128 * (hh // 2):128 * (hh // 2) + 128]
                s = lax.dot_general(q_ref[:, sl], k_ref[:, sl], NT, preferred_element_type=F32)
                if masked:
                    s = jnp.where(keep, s, NEG)
                m_old = m_s[hh]
                m_new = jnp.maximum(m_old, jnp.max(s, axis=-1, keepdims=True))
                alpha = jnp.exp2(m_old - m_new)
                p = jnp.exp2(s - jnp.tile(m_new, (1, rep)))
                l_s[hh] = alpha * l_s[hh] + jnp.sum(p, axis=-1, keepdims=True)
                acc_s[hh] = alpha * acc_s[hh] + jnp.dot(p.astype(BF16), vv, preferred_element_type=F32)
                m_s[hh] = m_new

        @pl.when(j < i)
        def _():
            step(False)

        @pl.when(j == i)
        def _():
            step(True)
            lane = lax.broadcasted_iota(jnp.int32, (tq, 128), 1)
            for pp in range(ATT_HEADS // 2):
                a, b = 2 * pp, 2 * pp + 1
                o_ref[:, 128 * pp:128 * pp + 128] = jnp.where(lane < VD, acc_s[a] / l_s[a], acc_s[b] / l_s[b])
            for hh in range(ATT_HEADS):
                lse_ref[hh] = (m_s[hh] + jnp.log2(l_s[hh]))[:, 0:1]

    qw, vw = 128 * ATT_HEADS, VD * ATT_HEADS
    stat = pltpu.VMEM((ATT_HEADS, tq, 128), F32)
    grid_spec = pltpu.PrefetchScalarGridSpec(
        num_scalar_prefetch=2, grid=(groups, len(steps)),
        in_specs=[pl.BlockSpec((tq, qw), lambda p, s, it, jt: (it[s], p)),
                  pl.BlockSpec((tq, qw), lambda p, s, it, jt: (jt[s], p)),
                  pl.BlockSpec((tq, vw), lambda p, s, it, jt: (jt[s], p))] + [ANY] * (2 * nb),
        out_specs=[pl.BlockSpec((tq, vw), lambda p, s, it, jt: (it[s], p)),
                   pl.BlockSpec((ATT_HEADS, tq, 1), lambda p, s, it, jt: (p, it[s], 0))] + [ANY] * nb,
        scratch_shapes=[stat, stat, stat] + ([pltpu.SemaphoreType.DMA((nb, 3))] * 4 if fetch else []))
    outs = pl.pallas_call(
        body, name="attn_fwd_fetch" if fetch else "attn_fwd", grid_spec=grid_spec,
        out_shape=[_sds((t, HEADS * VD), F32), _sds((HEADS, t, 1), F32)] + [_sds(b.shape, b.dtype) for b in bufs],
        input_output_aliases={5 + nb + b: 2 + b for b in range(nb)},
        compiler_params=_cp("arbitrary", "arbitrary"))(i_of, j_of, qs, k, v, *mine, *bufs)
    return outs[0], outs[1], list(outs[2:])


def _mixer_fwd(zb, ya, p, l):
    t = zb.shape[0]
    tm = min(ROW_TILE, t)
    hb = tm // 8

    def body(zb_ref, zprev_ref, ya_ref, gln_ref, bln_ref, wsp_ref, bsp_ref, cw_ref, go_ref, mix_ref, yb_ref, yc_ref):
        i = pl.program_id(0)
        masks = _group_masks((tm, SGW))
        cmasks = _group_masks((CHUNK, SGW))
        tril = _tril_mask()
        wc_bf = [jnp.where(tril, wsp_ref[g], 0.0).astype(BF16) for g in range(4)]
        u, _, _, _, _, mixed = _sgu_forward(zb_ref[:, 0:256], zb_ref[:, 256:512], gln_ref[...], bln_ref[...], wc_bf,
                                            bsp_ref[...], masks, cmasks)
        yb = u * mixed
        _, _, _, conv = _conv_forward(zb_ref[:, 768:1024], zb_ref[:, 1024:1280], zprev_ref[:, 768:1024],
                                      zprev_ref[:, 1024:1280], i == 0, cw_ref[...])
        yc = zb_ref[:, 512:768] * conv
        yb_ref[...] = yb
        yc_ref[...] = yc
        go = go_ref[...]
        mix_ref[:, 0:512] = _rms(ya_ref[...], go[:, 0:512]).astype(BF16)
        mix_ref[:, 512:768] = _rms(yb, go[:, 512:768]).astype(BF16)
        mix_ref[:, 768:1024] = _rms(yc, go[:, 768:1024]).astype(BF16)

    return _pcall(
        body, "mixer_fwd", (t // tm,),
        [zb, zb, ya, p["sg_ln_g"], p["sg_ln_b"], p["w_sp"], p["b_sp"], p["conv_w"], p["out_norm_g"]],
        [_row(tm, ZB), pl.BlockSpec((8, ZB), lambda i: (jnp.maximum(i * hb - 1, 0), 0)), _row(tm, 512),
         _lyr(l, 1, SGW), _lyr(l, 1, SGW), _lyr(l, 4, CHUNK, CHUNK), _lyr(l, CHUNK, SGW), _lyr(_wl(p["conv_w"], l), 3, CVW), _lyr(l, 1, D)],
        [_row(tm, D), _row(tm, SGW), _row(tm, CVW)],
        [_sds((t, D), BF16), _sds((t, SGW), F32), _sds((t, CVW), F32)], ("parallel",))


def _out_proj(mix, x, p, l):
    t = x.shape[0]
    tm = min(ROW_TILE, t)

    def body(mix_ref, w_ref, x_ref, gp_ref, gf_ref, o_ref, x2_ref, h2_ref):
        o = jnp.dot(mix_ref[...], w_ref[...], preferred_element_type=F32)
        o_ref[...] = o
        x2 = x_ref[...] + _rms(o, gp_ref[...])
        x2_ref[...] = x2
        h2_ref[...] = _rms(x2, gf_ref[...]).astype(BF16)

    return _pcall(
        body, "out_proj", (t // tm,), [mix, p["w_out"], x, p["mix_post_g"], p["ffn_pre_g"]],
        [_row(tm, D), _lyr(_wl(p["w_out"], l), D, D), _row(tm, D), _lyr(l, 1, D), _lyr(l, 1, D)],
        [_row(tm, D), _row(tm, D), _row(tm, D)],
        [_sds((t, D), F32), _sds((t, D), F32), _sds((t, D), BF16)], ("parallel",))


def _gu_all(l, which):
    return pl.BlockSpec((4, None, None, HP, D), lambda *_: (0, l, which, 0, 0))


def _down_all(l):
    return pl.BlockSpec((4, None, HP, D), lambda *_: (0, l, 0, 0))


def _ffn_up(h2, p, l):
    t = h2.shape[0]
    tm = min(ROW_TILE, t)

    def body(h_ref, wg_ref, wu_ref, a_ref, b_ref, s_ref):
        h = h_ref[...]
        a = lax.dot_general(h, wg_ref[...], NT, preferred_element_type=F32)
        b = lax.dot_general(h, wu_ref[...], NT, preferred_element_type=F32)
        a_ref[...] = a.astype(BF16)
        b_ref[...] = b.astype(BF16)
        s_ref[...] = (a * _sigmoid(a) * b).astype(BF16)

    blk = pl.BlockSpec((tm, HP), lambda k, i: (i, k))
    wblk = lambda which: pl.BlockSpec((None, None, None, HP, D), lambda k, i: (k, l, which, 0, 0))
    return _pcall(
        body, "ffn_up", (4, t // tm), [h2, p["w_gu"], p["w_gu"]],
        [pl.BlockSpec((tm, D), lambda k, i: (i, 0)), wblk(0), wblk(1)], [blk, blk, blk],
        [_sds((t, DFFP), BF16)] * 3, ("parallel", "parallel"))


def _ffn_down(s, x2, p, l):
    t = x2.shape[0]
    tm = min(ROW_TILE, t)

    def body(s_ref, w_ref, x_ref, g_ref, f_ref, x3_ref):
        f = jnp.dot(s_ref[:, 0:HP], w_ref[0], preferred_element_type=F32)
        for k in range(1, 4):
            f = f + jnp.dot(s_ref[:, k * HP:(k + 1) * HP], w_ref[k], preferred_element_type=F32)
        f_ref[...] = f
        x3_ref[...] = x_ref[...] + _rms(f, g_ref[...])

    return _pcall(
        body, "ffn_down", (t // tm,), [s, p["w_down"], x2, p["ffn_post_g"]],
        [_row(tm, DFFP), _down_all(l), _row(tm, D), _lyr(l, 1, D)], [_row(tm, D), _row(tm, D)],
        [_sds((t, D), F32), _sds((t, D), F32)], ("parallel",))


def _loss_head(y, target):
    t = y.shape[0]
    tm = min(ROW_TILE, t)

    def body(y_ref, t_ref, dy_ref, acc_ref):
        e = y_ref[...] - t_ref[...]
        dy_ref[...] = e * (1.0 / D)
        sq = jnp.sum(e * e, axis=0, keepdims=True)
        part = sq[:, 0:128]
        for b in range(1, D // 128):
            part = part + sq[:, 128 * b:128 * b + 128]
        _acc_init(pl.program_id(0), acc_ref)
        acc_ref[...] += part

    return _pcall(body, "loss_head", (t // tm,), [y, target], [_row(tm, D), _row(tm, D)],
                  [_row(tm, D), pl.BlockSpec((1, 128), lambda i: (0, 0))],
                  [_sds((t, D), F32), _sds((1, 128), F32)], ("arbitrary",))


def _ffn_down_bwd(dx3, sv, p, l, depth, gb):
    t = dx3.shape[0]
    tm = min(256, t)

    def body(dx_ref, f_ref, g_ref, w_ref, a_ref, b_ref, df_ref, da_ref, db_ref, dg_ref):
        _acc_init(pl.program_id(0), dg_ref)
        df, dg = _rms_bwd(f_ref[...], g_ref[...], dx_ref[...])
        dg_ref[...] += dg
        df = df.astype(BF16)
        df_ref[...] = df
        for k in range(4):
            sl = slice(k * HP, (k + 1) * HP)
            ds = lax.dot_general(df, w_ref[k], NT, preferred_element_type=F32)
            av = a_ref[:, sl].astype(F32)
            sig = _sigmoid(av)
            da_ref[:, sl] = (ds * b_ref[:, sl].astype(F32) * (sig * (1.0 + av * (1.0 - sig)))).astype(BF16)
            db_ref[:, sl] = (ds * (av * sig)).astype(BF16)

    df, da, db, gb["ffn_post_g"] = _pcall(
        body, "ffn_down_bwd", (t // tm,), [dx3, sv["f"], p["ffn_post_g"], p["w_down"], sv["a"], sv["b"]],
        [_row(tm, D), _row(tm, D), _lyr(l, 1, D), _down_all(l), _row(tm, DFFP), _row(tm, DFFP)],
        [_row(tm, D), _row(tm, DFFP), _row(tm, DFFP), _lyr(l.g, 1, D)],
        [_sds((t, D), BF16), _sds((t, DFFP), BF16), _sds((t, DFFP), BF16), _sds((depth, 1, D), F32)], ("arbitrary",),
        prevs={3: gb.get("ffn_post_g")})
    return df, da, db


def _ffn_up_bwd(da, db, dx3, sv, p, l, depth, gb):
    t = dx3.shape[0]
    tm = min(256, t)

    def body(da_ref, db_ref, wg_ref, wu_ref, x_ref, dx3_ref, g_ref, dx2_ref, dg_ref):
        _acc_init(pl.program_id(0), dg_ref)
        dh = jnp.zeros((tm, D), F32)
        for k in range(4):
            sl = slice(k * HP, (k + 1) * HP)
            dh = dh + jnp.dot(da_ref[:, sl], wg_ref[k], preferred_element_type=F32)
            dh = dh + jnp.dot(db_ref[:, sl], wu_ref[k], preferred_element_type=F32)
        dx, dg = _rms_bwd(x_ref[...], g_ref[...], dh)
        dg_ref[...] += dg
        dx2_ref[...] = dx3_ref[...] + dx

    dx2, gb["ffn_pre_g"] = _pcall(
        body, "ffn_up_bwd", (t // tm,), [da, db, p["w_gu"], p["w_gu"], sv["x2"], dx3, p["ffn_pre_g"]],
        [_row(tm, DFFP), _row(tm, DFFP), _gu_all(l, 0), _gu_all(l, 1), _row(tm, D), _row(tm, D), _lyr(l, 1, D)],
        [_row(tm, D), _lyr(l.g, 1, D)], [_sds((t, D), F32), _sds((depth, 1, D), F32)], ("arbitrary",),
        prevs={1: gb.get("ffn_pre_g")})
    return dx2


def _out_proj_bwd(dx2, sv, p, l, depth, gb):
    t = dx2.shape[0]
    tm = min(ROW_TILE, t)

    def body(dx_ref, o_ref, g_ref, w_ref, do_ref, dmix_ref, dg_ref):
        _acc_init(pl.program_id(0), dg_ref)
        do, dg = _rms_bwd(o_ref[...], g_ref[...], dx_ref[...])
        dg_ref[...] += dg
        do = do.astype(BF16)
        do_ref[...] = do
        dmix_ref[...] = lax.dot_general(do, w_ref[...], NT, preferred_element_type=F32)

    do, dmix, gb["mix_post_g"] = _pcall(
        body, "out_proj_bwd", (t // tm,), [dx2, sv["o"], p["mix_post_g"], p["w_out"]],
        [_row(tm, D), _row(tm, D), _lyr(l, 1, D), _lyr(_wl(p["w_out"], l), D, D)], [_row(tm, D), _row(tm, D), _lyr(l.g, 1, D)],
        [_sds((t, D), BF16), _sds((t, D), F32), _sds((depth, 1, D), F32)], ("arbitrary",),
        prevs={2: gb.get("mix_post_g")})
    return do, dmix


def _mixer_bwd(dmix, sv, p, l, depth, gb):
    zb = sv["zb"]
    t = zb.shape[0]
    tm = min(ROW_TILE, t)
    hb = tm // 8
    last_blk = t // 8 - 1
    nsteps = t // tm

    def body(dmix_ref, ya_ref, yb_ref, yc_ref, zb_ref, zprev_ref, znext_ref, ycn_ref, dmn_ref,
             gln_ref, bln_ref, wsp_ref, bsp_ref, cw_ref, go_ref,
             dya_ref, dzb_ref, delta_ref, dgo_ref, dgln_ref, dbln_ref, dwsp_ref, dbsp_ref, dcw_ref):
        i = pl.program_id(0)
        _acc_init(i, dgo_ref, dgln_ref, dbln_ref, dwsp_ref, dbsp_ref, dcw_ref)
        go = go_ref[...]
        dmix = dmix_ref[...]

        ya = ya_ref[...]
        dya, dga = _rms_bwd(ya, go[:, 0:512], dmix[:, 0:512])
        dyb, dgb_ = _rms_bwd(yb_ref[...], go[:, 512:768], dmix[:, 512:768])
        dyc, dgc_ = _rms_bwd(yc_ref[...], go[:, 768:1024], dmix[:, 768:1024])
        dgo_ref[:, 0:512] += dga
        dgo_ref[:, 512:768] += dgb_
        dgo_ref[:, 768:1024] += dgc_
        dya = dya * LN2
        dya_ref[...] = dya.astype(BF16)
        prod = dya * ya
        hmasks = _group_masks((tm, 512))
        for h in range(HEADS):
            delta_ref[h] = jnp.sum(jnp.where(hmasks[h], prod, 0.0), axis=-1, keepdims=True)

        masks = _group_masks((tm, SGW))
        cmasks = _group_masks((CHUNK, SGW))
        tril = _tril_mask()
        wc_bf = [jnp.where(tril, wsp_ref[g], 0.0).astype(BF16) for g in range(4)]
        zu, zv = zb_ref[:, 0:256], zb_ref[:, 256:512]
        g_ln = gln_ref[...]
        u, _, xh, rs, vn, mixed = _sgu_forward(zu, zv, g_ln, bln_ref[...], wc_bf, bsp_ref[...], masks, cmasks)
        du = dyb * mixed
        dmixed = dyb * u
        dvn_chunks = []
        dbsp = jnp.zeros((CHUNK, SGW), F32)
        for ci in range(tm // CHUNK):
            rows = slice(ci * CHUNK, (ci + 1) * CHUNK)
            dm_c = dmixed[rows, :]
            vn_c = vn[rows, :].astype(BF16)
            dbsp = dbsp + dm_c
            dvn_c = jnp.zeros((CHUNK, SGW), F32)
            for g in range(4):
                dm_g = jnp.where(cmasks[g], dm_c, 0.0).astype(BF16)
                dw = lax.dot_general(dm_g, vn_c, NT, preferred_element_type=F32)
                dwsp_ref[g] += jnp.where(tril, dw, 0.0)
                dvn_c = dvn_c + lax.dot_general(wc_bf[g], dm_g, TN, preferred_element_type=F32)
            dvn_chunks.append(dvn_c)
        dbsp_ref[...] += dbsp
        dvn = jnp.concatenate(dvn_chunks, axis=0) if len(dvn_chunks) > 1 else dvn_chunks[0]
        dgln_ref[...] += jnp.sum(dvn * xh, axis=0, keepdims=True)
        dbln_ref[...] += jnp.sum(dvn, axis=0, keepdims=True)
        dxh = dvn * g_ln
        dvv = rs * (dxh - _group_mean(dxh, masks) - xh * _group_mean(dxh * xh, masks))
        dzb_ref[:, 0:256] = (du * _gelu_grad(zu)).astype(BF16)
        dzb_ref[:, 256:512] = (dvv * _gelu_grad(zv)).astype(BF16)

        cwv = cw_ref[...]
        gb_, gc, hh = zb_ref[:, 512:768], zb_ref[:, 768:1024], zb_ref[:, 1024:1280]
        yv, sh1, sh2, conv = _conv_forward(gc, hh, zprev_ref[:, 768:1024], zprev_ref[:, 1024:1280], i == 0, cwv)
        dconv = dyc * gb_
        dzb_ref[:, 512:768] = (dyc * conv).astype(BF16)
        dcw_ref[0:1, :] += jnp.sum(dconv * sh2, axis=0, keepdims=True)
        dcw_ref[1:2, :] += jnp.sum(dconv * sh1, axis=0, keepdims=True)
        dcw_ref[2:3, :] += jnp.sum(dconv * yv, axis=0, keepdims=True)
        dycn, _ = _rms_bwd(ycn_ref[...], go[:, 768:1024], dmn_ref[...])
        dconv_next = jnp.where(i == nsteps - 1, 0.0, dycn * znext_ref[:, 512:768])
        n0, n1 = _pick_row(dconv_next, 0), _pick_row(dconv_next, 1)
        dyv = dconv * cwv[2:3, :] + _shift_up(dconv, 1, [n0]) * cwv[1:2, :] + _shift_up(dconv, 2, [n0, n1]) * cwv[0:1, :]
        dzb_ref[:, 768:1024] = (dyv * hh).astype(BF16)
        dzb_ref[:, 1024:1280] = (dyv * gc).astype(BF16)

    prev_map = lambda i: (jnp.maximum(i * hb - 1, 0), 0)
    next_map = lambda i: (jnp.minimum((i + 1) * hb, last_blk), 0)
    names = ("out_norm_g", "sg_ln_g", "sg_ln_b", "w_sp", "b_sp_t", "conv_w")
    shapes = ((1, D), (1, SGW), (1, SGW), (4, CHUNK, CHUNK), (CHUNK, SGW), (3, CVW))
    outs = _pcall(
        body, "mixer_bwd", (nsteps,),
        [dmix, sv["ya"], sv["yb"], sv["yc"], zb, zb, zb, sv["yc"], dmix, p["sg_ln_g"], p["sg_ln_b"], p["w_sp"], p["b_sp"],
         p["conv_w"], p["out_norm_g"]],
        [_row(tm, D), _row(tm, 512), _row(tm, SGW), _row(tm, CVW), _row(tm, ZB),
         pl.BlockSpec((8, ZB), prev_map), pl.BlockSpec((8, ZB), next_map), pl.BlockSpec((8, CVW), next_map),
         pl.BlockSpec((8, 256), lambda i: (jnp.minimum((i + 1) * hb, last_blk), 3)),
         _lyr(l, 1, SGW), _lyr(l, 1, SGW), _lyr(l, 4, CHUNK, CHUNK), _lyr(l, CHUNK, SGW), _lyr(_wl(p["conv_w"], l), 3, CVW), _lyr(l, 1, D)],
        [_row(tm, 512), _row(tm, ZB), pl.BlockSpec((HEADS, tm, 1), lambda i: (0, i, 0))] + [_lyr(l.g, *s) for s in shapes],
        [_sds((t, 512), BF16), _sds((t, ZB), BF16), _sds((HEADS, t, 1), F32)] + [_sds((depth,) + s, F32) for s in shapes],
        ("arbitrary",), prevs={3 + n: gb.get(name) for n, name in enumerate(names)})
    for n, name in enumerate(names):
        gb[name] = outs[3 + n]
    return outs[0], outs[1], outs[2]


def _attn_bwd(qs, k, v, dya, lse, delta, ride=()):
    t = qs.shape[0]
    tq = _att_tile(t)
    nq = t // tq
    nb = len(ride)
    groups = HEADS // ATT_HEADS

    steps = [(j, i) for j in range(nq) for i in range(j, nq)]
    j_of = jnp.asarray([s[0] for s in steps], jnp.int32)
    i_of = jnp.asarray([s[1] for s in steps], jnp.int32)

    def body(j_ref, i_ref, q_ref, k_ref, v_ref, do_ref, lse_ref, dl_ref, *refs):
        dq_ref, dk_ref, dv_ref = refs[nb:nb + 3]
        dq_s, dk_s, dv_s = refs[2 * nb + 3:2 * nb + 6]
        step_no = pl.program_id(1)
        j, i = j_ref[step_no], i_ref[step_no]
        if ride:
            start, finish = _exchange_ops(refs[:nb], refs[nb + 3:2 * nb + 3], refs[2 * nb + 6:])
            pr = pl.program_id(0)
            pl.when((pr == 0) & (step_no == 0))(start)
            pl.when((pr == groups - 1) & (step_no == len(steps) - 1))(finish)

        @pl.when(step_no == 0)
        def _():
            dq_s[...] = jnp.zeros(dq_s.shape, F32)

        def step(masked):
            keep = _causal_keep(tq, 0, 0) if masked else None
            lane = lax.broadcasted_iota(jnp.int32, (tq, 128), 1)
            rows = pl.ds(pl.multiple_of(i * tq, tq), tq)
            for hh in range(ATT_HEADS):
                sl = slice(128 * hh, 128 * hh + 128)
                pair = slice(128 * (hh // 2), 128 * (hh // 2) + 128)
                vv, do = v_ref[:, pair], do_ref[:, pair]
                qq, kk = q_ref[:, sl], k_ref[:, sl]
                s = lax.dot_general(qq, kk, NT, preferred_element_type=F32)
                p = jnp.exp2(s - lse_ref[hh])
                if masked:
                    p = jnp.where(keep, p, 0.0)
                do_h = jnp.where((lane < VD) if hh % 2 == 0 else (lane >= VD), do, jnp.zeros_like(do))
                dp = lax.dot_general(do_h, vv, NT, preferred_element_type=F32)
                ds = (p * (dp - dl_ref[hh])).astype(BF16)
                dv_s[:, pair] += lax.dot_general(p.astype(BF16), do_h, TN, preferred_element_type=F32)
                dk_s[:, sl] += lax.dot_general(ds, qq, TN, preferred_element_type=F32)
                dq_s[rows, sl] += jnp.dot(ds, kk, preferred_element_type=F32)

        @pl.when(i == j)
        def _():
            dk_s[...] = jnp.zeros(dk_s.shape, F32)
            dv_s[...] = jnp.zeros(dv_s.shape, F32)
            step(True)

        @pl.when(i > j)
        def _():
            step(False)

        @pl.when(i == nq - 1)
        def _():
            dk_ref[...] = dk_s[...].astype(BF16)
            dv_ref[...] = (dv_s[...] * LOG2E).astype(BF16)

        @pl.when(step_no == len(steps) - 1)
        def _():
            dq_ref[...] = dq_s[...].astype(BF16)

    qw, vw = 128 * ATT_HEADS, VD * ATT_HEADS
    qrow = lambda p, s, jt, it: (it[s], p)
    krow = lambda p, s, jt, it: (jt[s], p)
    col_spec = pl.BlockSpec((ATT_HEADS, tq, 1), lambda p, s, jt, it: (p, it[s], 0))
    grid_spec = pltpu.PrefetchScalarGridSpec(
        num_scalar_prefetch=2, grid=(groups, len(steps)),
        in_specs=[pl.BlockSpec((tq, qw), qrow), pl.BlockSpec((tq, qw), krow), pl.BlockSpec((tq, vw), krow),
                  pl.BlockSpec((tq, vw), qrow), col_spec, col_spec] + [ANY] * nb,
        out_specs=[pl.BlockSpec((t, qw), lambda p, s, jt, it: (0, p)), pl.BlockSpec((tq, qw), krow),
                   pl.BlockSpec((tq, vw), krow)] + [ANY] * nb,
        scratch_shapes=[pltpu.VMEM((t, qw), F32), pltpu.VMEM((tq, qw), F32), pltpu.VMEM((tq, vw), F32)]
        + ([pltpu.SemaphoreType.DMA((nb, 3))] * 2 if ride else []))
    outs = pl.pallas_call(
        body, name="attn_bwd_ride" if ride else "attn_bwd", grid_spec=grid_spec,
        out_shape=[_sds((t, QW), BF16), _sds((t, QW), BF16), _sds((t, HEADS * VD), BF16)]
        + [_sds((3,) + a.shape[1:], a.dtype) for a in ride],
        compiler_params=_cp("arbitrary", "arbitrary"))(j_of, i_of, qs, k, v, dya, lse, delta, *ride)
    return outs[0], outs[1], outs[2], list(outs[3:])


def _mla_prep_bwd(dqs, dk, dv, sv, p, l, depth, gb, tabs):
    za = sv["za"]
    t = za.shape[0]
    tm = min(ROW_TILE, t)

    def body(dq_ref, dk_ref, dv_ref, z_ref, gq_ref, gkv_ref, wuq_ref, wukv_ref, c_ref, s1_ref, s2_ref,
             dza_ref, dqp_ref, dkv_ref, dgq_ref, dgkv_ref):
        _acc_init(pl.program_id(0), dgq_ref, dgkv_ref)
        c, s1, s2 = c_ref[...], s1_ref[...], s2_ref[...]
        lane = lax.broadcasted_iota(jnp.int32, (tm, 128), 1)
        rope_lanes = (lane >= NOPE) & (lane < NOPE + ROPE)
        dkr = jnp.zeros((tm, 128), F32)
        for h in range(HEADS):
            sl = slice(128 * h, 128 * h + 128)
            dqp_ref[:, sl] = _rope_bwd(dq_ref[:, sl].astype(F32) * QSCALE, c, s1, s2).astype(BF16)
            dkh = dk_ref[:, sl]
            dkv_ref[:, sl] = dkh
            dkr = dkr + jnp.where(rope_lanes, dkh.astype(F32), 0.0)
        dkv_ref[:, QW:] = dv_ref[...]
        z = z_ref[...]
        dcq = lax.dot_general(dqp_ref[...], wuq_ref[...], NT, preferred_element_type=F32)
        dzq, dgq = _rms_bwd(z[:, :QR], gq_ref[...], dcq)
        dckv = lax.dot_general(dkv_ref[...], wukv_ref[...], NT, preferred_element_type=F32)
        dzkv, dgkv = _rms_bwd(z[:, QR:QR + KVR], gkv_ref[...], dckv)
        dgq_ref[...] += dgq
        dgkv_ref[...] += dgkv
        dza_ref[:, :QR] = dzq.astype(BF16)
        dza_ref[:, QR:QR + KVR] = dzkv.astype(BF16)
        dza_ref[:, QR + KVR:] = _rope_bwd(dkr, c, s1, s2).astype(BF16)

    dza, dqp, dkv, gb["q_norm_g"], gb["kv_norm_g"] = _pcall(
        body, "mla_prep_bwd", (t // tm,),
        [dqs, dk, dv, za, p["q_norm_g"], p["kv_norm_g"], p["w_uq"], p["w_ukv"], *tabs],
        [_row(tm, QW), _row(tm, QW), _row(tm, HEADS * VD), _row(tm, ZA), _lyr(l, 1, QR), _lyr(l, 1, KVR),
         _lyr(_wl(p["w_uq"], l), QR, QW), _lyr(_wl(p["w_ukv"], l), KVR, KVW), _row(tm, 128), _row(tm, 128), _row(tm, 128)],
        [_row(tm, ZA), _row(tm, QW), _row(tm, KVW), _lyr(l.g, 1, QR), _lyr(l.g, 1, KVR)],
        [_sds((t, ZA), BF16), _sds((t, QW), BF16), _sds((t, KVW), BF16), _sds((depth, 1, QR), F32),
         _sds((depth, 1, KVR), F32)], ("arbitrary",), prevs={3: gb.get("q_norm_g"), 4: gb.get("kv_norm_g")})
    return dza, dqp, dkv


def _in_proj_bwd(dza, dzb, dx2, sv, p, l, depth, gb):
    t = dx2.shape[0]
    tm = min(ROW_TILE, t)

    def body(dza_ref, dzb_ref, wa_ref, wb_ref, x_ref, dx2_ref, g_ref, dx_ref, dg_ref):
        _acc_init(pl.program_id(0), dg_ref)
        dh = (lax.dot_general(dza_ref[...], wa_ref[...], NT, preferred_element_type=F32)
              + lax.dot_general(dzb_ref[...], wb_ref[...], NT, preferred_element_type=F32))
        dx, dg = _rms_bwd(x_ref[...], g_ref[...], dh)
        dg_ref[...] += dg
        dx_ref[...] = dx2_ref[...] + dx

    dx, gb["mix_pre_g"] = _pcall(
        body, "in_proj_bwd", (t // tm,), [dza, dzb, p["w_in_a"], p["w_in_b"], sv["x"], dx2, p["mix_pre_g"]],
        [_row(tm, ZA), _row(tm, ZB), _lyr(_wl(p["w_in_a"], l), D, ZA), _lyr(_wl(p["w_in_b"], l), D, ZB), _row(tm, D), _row(tm, D), _lyr(l, 1, D)],
        [_row(tm, D), _lyr(l.g, 1, D)], [_sds((t, D), F32), _sds((depth, 1, D), F32)], ("arbitrary",),
        prevs={1: gb.get("mix_pre_g")})
    return dx


def _mm_tn(a, b, tn, name, l, depth, gb):
    t, k = a.shape
    n = b.shape[1]
    tt = min(ROW_TILE, t)

    def body(a_ref, b_ref, o_ref):
        _acc_init(pl.program_id(1), o_ref)
        o_ref[...] += lax.dot_general(a_ref[...], b_ref[...], TN, preferred_element_type=F32)

    gb[name] = _pcall(
        body, "d" + name, (n // tn, t // tt), [a, b],
        [pl.BlockSpec((tt, k), lambda j, s: (s, 0)), pl.BlockSpec((tt, tn), lambda j, s: (s, j))],
        pl.BlockSpec((None, k, tn), lambda j, s: (l.g, 0, j)), _sds((depth, k, n), F32), ("parallel", "arbitrary"),
        prevs={0: gb.get(name)})


def _dw_ffn(a, b, kind, l, depth, gb):
    t = a.shape[0]
    tt = min(ROW_TILE, t)
    nsteps = t // tt

    def body(a_ref, b_ref, o_ref, acc):
        s = pl.program_id(0)
        _acc_init(s, acc)
        acc[...] += lax.dot_general(a_ref[...], b_ref[...], TN, preferred_element_type=F32)

        @pl.when(s == nsteps - 1)
        def _():
            for k in range(4):
                o_ref[k] = acc[k * HP:(k + 1) * HP, :].astype(BF16)

    rows = lambda n: pl.BlockSpec((tt, n), lambda s: (s, 0))
    if kind == "down":
        name = "down"
        out_spec = pl.BlockSpec((4, None, HP, D), lambda s: (0, l.g, 0, 0))
        out_shape = _sds((4, depth, HP, D), BF16)
    else:
        which = 0 if kind == "gate" else 1
        name = "gu"
        out_spec = pl.BlockSpec((4, None, None, HP, D), lambda s: (0, l.g, which, 0, 0))
        out_shape = _sds((4, depth, 2, HP, D), BF16)
    gb[name] = _pcall(body, "dw_" + kind, (nsteps,), [a, b], [rows(DFFP), rows(D)], out_spec, out_shape, ("arbitrary",),
                      scratch=[pltpu.VMEM((DFFP, D), F32)], prevs={0: gb.get(name)})


def _ffn_views(bufs):
    return {"w_gu": bufs[1], "w_down": bufs[2].reshape(bufs[2].shape[:2] + (HP, D))}


def _layer_fwd(x, p, l, tabs, fetch):
    h1, za, zb = _in_proj(x, p, l)
    cqn, ckvn, qs, k, v = _mla_prep(za, p, l, tabs)
    ya, lse, bufs = _attn_fwd(qs, k, v, fetch)
    if fetch:
        p = {**p, **_ffn_views(bufs)}
    mix, yb, yc = _mixer_fwd(zb, ya, p, l)
    o, x2, h2 = _out_proj(mix, x, p, l)
    a, b, s = _ffn_up(h2, p, l)
    f, x3 = _ffn_down(s, x2, p, l)
    saved = dict(x=x, h1=h1, za=za, zb=zb, cqn=cqn, ckvn=ckvn, qs=qs, k=k, v=v, ya=ya, lse=lse, mix=mix, yb=yb, yc=yc,
                 o=o, x2=x2, h2=h2, a=a, b=b, s=s, f=f)
    return x3, saved, bufs if fetch else None


class _Layer(int):
    def __new__(cls, l, g):
        obj = int.__new__(cls, l)
        obj.g = g
        return obj


def _layer_bwd(dx3, p, sv, l, depth, gb, tabs, ride=(), ffn_front=None):
    df, da, db = _ffn_down_bwd(dx3, sv, p, l, depth, gb)
    _dw_ffn(sv["s"], df, "down", l, depth, gb)
    dx2 = _ffn_up_bwd(da, db, dx3, sv, p, l, depth, gb)
    _dw_ffn(da, sv["h2"], "gate", l, depth, gb)
    _dw_ffn(db, sv["h2"], "up", l, depth, gb)
    ffn_blocks = list(ffn_front(gb["gu"], gb["down"])) if ffn_front else []
    do, dmix = _out_proj_bwd(dx2, sv, p, l, depth, gb)
    _mm_tn(sv["mix"], do, D, "w_out", l, depth, gb)
    dya, dzb, delta = _mixer_bwd(dmix, sv, p, l, depth, gb)
    dqs, dk, dv, sent = _attn_bwd(sv["qs"], sv["k"], sv["v"], dya, sv["lse"], delta, tuple(ffn_blocks) + tuple(ride))
    dza, dqp, dkv = _mla_prep_bwd(dqs, dk, dv, sv, p, l, depth, gb, tabs)
    _mm_tn(sv["cqn"], dqp, QW, "w_uq", l, depth, gb)
    _mm_tn(sv["ckvn"], dkv, KVW, "w_ukv", l, depth, gb)
    _mm_tn(sv["h1"], dza, ZA, "w_in_a", l, depth, gb)
    _mm_tn(sv["h1"], dzb, ZB, "w_in_b", l, depth, gb)
    nf = len(ffn_blocks)
    return _in_proj_bwd(dza, dzb, dx2, sv, p, l, depth, gb), ffn_blocks, sent[:nf], sent[nf:]


def _rope_tables(positions):
    inv_freq = 1.0 / (ROPE_THETA ** (jnp.arange(0, ROPE // 2, dtype=F32) / (ROPE // 2)))
    ang = positions.astype(F32)[:, None] * inv_freq
    cos, sin = jnp.cos(ang), jnp.sin(ang)
    t = positions.shape[0]
    one, zero = jnp.ones((t, 64), F32), jnp.zeros((t, 16), F32)
    c = jnp.concatenate([one, cos, cos, one[:, :32]], axis=1)
    s1 = jnp.concatenate([zero, zero, zero, zero, -sin, zero, zero, zero], axis=1)
    s2 = jnp.concatenate([zero, zero, zero, zero, zero, sin, zero, zero], axis=1)
    return c, s1, s2


def _mixer_weight_params(full):
    w_in = full["w_in"]
    depth = w_in.shape[0]
    zpad = lambda n: jnp.zeros((depth, D, n), w_in.dtype)
    kv = full["w_ukv"].reshape(depth, KVR, HEADS, NOPE + VD)
    return {
        "w_in_a": jnp.concatenate([w_in[:, :, :640], zpad(64), w_in[:, :, 640:672], zpad(32)], axis=2),
        "w_in_b": w_in[:, :, 672:],
        "w_uq": jnp.pad(full["w_uq"].reshape(depth, QR, HEADS, NOPE + ROPE),
                        ((0, 0), (0, 0), (0, 0), (0, 32))).reshape(depth, QR, QW),
        "w_ukv": jnp.concatenate([jnp.pad(kv[..., :NOPE], ((0, 0), (0, 0), (0, 0), (0, 64))).reshape(depth, KVR, QW),
                                  kv[..., NOPE:].reshape(depth, KVR, HEADS * VD)], axis=2),
        "w_out": full["w_out"], "conv_w": full["conv_w"],
    }


def _small_params(w):
    p = {"w_sp": w["w_sp"], "b_sp": jnp.repeat(jnp.swapaxes(w["b_sp"], 1, 2), 64, axis=2)}
    for n in ("mix_pre_g", "mix_post_g", "ffn_pre_g", "ffn_post_g", "q_norm_g", "kv_norm_g", "sg_ln_g", "sg_ln_b",
              "out_norm_g"):
        p[n] = w[n][:, None, :]
    return p


def _natural_grads(gb):
    depth = gb["w_in_a"].shape[0]
    ga, kv = gb["w_in_a"], gb["w_ukv"]
    out = {
        "w_in": jnp.concatenate([ga[:, :, :640], ga[:, :, 704:736], gb["w_in_b"]], axis=2),
        "w_uq": gb["w_uq"].reshape(depth, QR, HEADS, 128)[..., :NOPE + ROPE].reshape(depth, QR, HEADS * (NOPE + ROPE)),
        "w_ukv": jnp.concatenate([kv[:, :, :QW].reshape(depth, KVR, HEADS, 128)[..., :NOPE],
                                  kv[:, :, QW:].reshape(depth, KVR, HEADS, VD)], axis=3).reshape(depth, KVR, -1),
        "b_sp": jnp.swapaxes(gb["b_sp_t"].reshape(depth, CHUNK, 4, 64).sum(axis=-1), 1, 2),
    }
    for n in ("w_out", "w_sp", "conv_w"):
        out[n] = gb[n]
    for n in ("mix_pre_g", "mix_post_g", "ffn_pre_g", "ffn_post_g", "q_norm_g", "kv_norm_g", "sg_ln_g", "sg_ln_b",
              "out_norm_g"):
        out[n] = gb[n][:, 0, :]
    return out


def _local_step(x, positions, target, small, mine, bufs, shard_shapes, fetch=True, front=None):
    depth = small["w_sp"].shape[0]
    tabs = _rope_tables(positions)
    ps = _small_params(small)
    saved, mixer_w = [], []
    for l in range(depth):
        mixer_w.append(_mixer_weight_params(_unpack_weights(bufs[0], l, shard_shapes)))
        p = {**ps, **mixer_w[l], **_ffn_views(bufs)}
        layers = [l + 1 if l + 1 < depth else None, l, l]
        x, sv, fetched = _layer_fwd(x, p, l, tabs, (mine, bufs, layers) if fetch else None)
        bufs = fetched or bufs
        saved.append(sv)
    dx, acc = _loss_head(x, target)
    loss = (0.5 / D) * jnp.sum(acc)
    gbs = [{} for _ in range(depth)]
    out = [{} for _ in range(depth)]
    ride = ()
    for l in reversed(range(depth)):
        ffn_front = (lambda gu, down, l=l: front[0](l, gu, down)) if front else None
        dx, out[l]["ffn"], out[l]["sent_ffn"], got = _layer_bwd(
            dx, {**ps, **mixer_w[l], **_ffn_views(bufs)}, saved[l], _Layer(l, 0), 1, gbs[l], tabs, ride, ffn_front)
        if ride:
            out[l + 1]["sent_mixer"], ride = got, ()
        grads = _natural_grads(gbs[l])
        if front:
            out[l]["mixer"] = front[1](l, grads)
            if l > 0:
                ride = tuple(out[l]["mixer"])
        else:
            out[l]["grads"] = (grads, gbs[l]["gu"], gbs[l]["down"])
    return loss, dx, out


def _place():
    x, y, c = lax.axis_index("x"), lax.axis_index("y"), lax.axis_index("c")
    chips = [(1 - x, y), (x, 1 - y), (1 - x, 1 - y)]
    return x, y, c, 2 * x + y, chips


def _remote(src, dst, send_sem, recv_sem, to):
    return pltpu.make_async_remote_copy(src_ref=src, dst_ref=dst, send_sem=send_sem, recv_sem=recv_sem, device_id=to,
                                        device_id_type=MESH_ID)


def _gather_ops(mine_refs, out_refs, sems, layers):
    send_sems, recv_sems, fsend_sems, frecv_sems = sems
    x, y, c, k, chips = _place()
    sib = (x, y, 1 - c)
    pairs = [(b, n) for n in range(3) for b in range(len(mine_refs)) if layers[b] is not None]

    def slot(n):
        return 2 * chips[n][0] + chips[n][1]

    def ici(b, n, dst_chip):
        return _remote(mine_refs[b].at[layers[b], c], out_refs[b].at[dst_chip, layers[b], c], send_sems.at[b, n],
                       recv_sems.at[b, n], (*chips[n], c))

    def d2d(b, n, half):
        piece = out_refs[b].at[slot(n), layers[b], half]
        return _remote(piece, piece, fsend_sems.at[b, n], frecv_sems.at[b, n], sib)

    def start():
        for b, n in pairs:
            ici(b, n, k).start()

    def hand_over():
        for b, n in pairs:
            ici(b, n, slot(n)).wait_recv()
            d2d(b, n, c).start()

    def drain():
        for b, n in pairs:
            d2d(b, n, 1 - c).wait_recv()
        for b, n in pairs:
            ici(b, n, k).wait_send()
            d2d(b, n, c).wait_send()

    return start, hand_over, drain


def _gather_first_layer(mine):
    nb = len(mine)

    def body(*refs):
        start, hand_over, drain = _gather_ops(refs[:nb], refs[nb:2 * nb], refs[2 * nb:], [0] + [None] * (nb - 1))
        start()
        hand_over()
        drain()

    return pl.pallas_call(
        body, name="gather_first_layer", in_specs=[ANY] * nb, out_specs=[ANY] * nb,
        out_shape=[_sds((4,) + a.shape, a.dtype) for a in mine],
        scratch_shapes=[pltpu.SemaphoreType.DMA((nb, 3))] * 4)(*mine)


def _swap_halves(bigs, wholes=()):
    nb, n = len(bigs), len(bigs) + len(wholes)

    def body(*refs):
        src, dst = refs[:n], refs[n:2 * n]
        send_sems, recv_sems = refs[2 * n:]
        x, y, c, _, _ = _place()
        sib = (x, y, 1 - c)
        cps = [_remote(src[b].at[:, 1 - c] if b < nb else src[b], dst[b], send_sems.at[b], recv_sems.at[b], sib)
               for b in range(n)]
        for cp in cps:
            cp.start()
        for cp in cps:
            cp.wait()

    return pl.pallas_call(
        body, name="swap_halves", in_specs=[ANY] * n, out_specs=[ANY] * n,
        out_shape=[_sds((4,) + a.shape[2:], a.dtype) for a in bigs] + [_sds(a.shape, a.dtype) for a in wholes],
        scratch_shapes=[pltpu.SemaphoreType.DMA((n,))] * 2)(*bigs, *wholes)


def _sum_tile(r):
    return max(cand for cand in range(16, 641, 16) if r % cand == 0)


def _pair_sum(big, rbig, c):
    _, _, r, w = big.shape
    tr = _sum_tile(r)

    def body(c_ref, big_ref, rbig_ref, p_ref):
        p_ref[...] = (big_ref[...].astype(F32) + rbig_ref[...].astype(F32)).astype(BF16)

    grid_spec = pltpu.PrefetchScalarGridSpec(
        num_scalar_prefetch=1, grid=(4, r // tr),
        in_specs=[pl.BlockSpec((None, None, tr, w), lambda j, i, cr: (j, cr[0], i, 0)),
                  pl.BlockSpec((None, tr, w), lambda j, i, cr: (j, i, 0))],
        out_specs=pl.BlockSpec((None, tr, w), lambda j, i, cr: (j, i, 0)))
    return pl.pallas_call(body, name="pair_sum", grid_spec=grid_spec, out_shape=_sds((4, r, w), BF16),
                          compiler_params=_cp("parallel", "parallel"))(c, big, rbig)


def _small_sum(parts):
    n, ns, _ = parts.shape

    def body(p_ref, o_ref):
        s = p_ref[0]
        for j in range(1, n):
            s = s + p_ref[j]
        o_ref[...] = s

    return pl.pallas_call(body, name="small_sum", out_shape=_sds((ns, 128), F32))(parts)


def _exchange_ops(p_refs, rb_refs, sems, small=None):
    send_sems, recv_sems = sems[0], sems[1]
    nb = len(p_refs)
    x, y, c, k, chips = _place()

    def copies(landing):
        out = []
        for n, (cx, cy) in enumerate(chips):
            to, kj = (cx, cy, c), 2 * cx + cy
            for b in range(nb):
                out.append(_remote(p_refs[b].at[k if landing else kj], rb_refs[b].at[n], send_sems.at[b, n],
                                   recv_sems.at[b, n], to))
            if small:
                out.append(_remote(small[0], small[1].at[kj if landing else k], send_sems.at[nb, n], recv_sems.at[nb, n], to))
        return out

    def local():
        return pltpu.make_async_copy(small[0], small[1].at[k], sems[2])

    def start():
        if small:
            local().start()
        for cp in copies(False):
            cp.start()

    def finish():
        for cp in copies(True):
            cp.wait_recv()
        for cp in copies(False):
            cp.wait_send()
        if small:
            local().wait()

    return start, finish


def _chip_exchange(ps, small):
    nb = len(ps)
    ns = small.shape[0]

    def body(*refs):
        start, finish = _exchange_ops(refs[:nb], refs[nb + 1:2 * nb + 1], refs[2 * nb + 2:], (refs[nb], refs[2 * nb + 1]))
        start()
        finish()

    return pl.pallas_call(
        body, name="chip_exchange", in_specs=[ANY] * (nb + 1), out_specs=[ANY] * (nb + 1),
        out_shape=[_sds((3,) + a.shape[1:], a.dtype) for a in ps] + [_sds((4, ns, 128), small.dtype)],
        scratch_shapes=[pltpu.SemaphoreType.DMA((nb + 1, 3))] * 2 + [pltpu.SemaphoreType.DMA(())])(*ps, small)


def _chip_sum(p, rb, chip):
    _, r, w = p.shape
    tr = _sum_tile(r)

    def body(k_ref, p_ref, rb_ref, o_ref):
        acc = p_ref[...].astype(F32)
        for j in range(3):
            acc = acc + rb_ref[j].astype(F32)
        o_ref[...] = acc

    grid_spec = pltpu.PrefetchScalarGridSpec(
        num_scalar_prefetch=1, grid=(r // tr,),
        in_specs=[pl.BlockSpec((None, tr, w), lambda i, kr: (kr[0], i, 0)), pl.BlockSpec((3, tr, w), lambda i, kr: (0, i, 0))],
        out_specs=pl.BlockSpec((tr, w), lambda i, kr: (i, 0)))
    return pl.pallas_call(body, name="chip_sum", grid_spec=grid_spec, out_shape=_sds((r, w), F32),
                          compiler_params=_cp("parallel"))(chip, p, rb)


def _send_to_sibling(reds):
    nb = len(reds)

    def body(*refs):
        red_refs, out_refs = refs[:nb], refs[nb:2 * nb]
        send_sems, recv_sems = refs[2 * nb:]
        x, y, c, _, _ = _place()
        cps = [_remote(red_refs[b], out_refs[b], send_sems.at[b], recv_sems.at[b], (x, y, 1 - c)) for b in range(nb)]
        for cp in cps:
            cp.start()
        for cp in cps:
            cp.wait()

    return pl.pallas_call(
        body, name="send_to_sibling", in_specs=[ANY] * nb, out_specs=[ANY] * nb,
        out_shape=[_sds(a.shape, a.dtype) for a in reds], scratch_shapes=[pltpu.SemaphoreType.DMA((nb,))] * 2)(*reds)


def _adam_math(w, g, m, v):
    nm = ADAM_B1 * m + (1.0 - ADAM_B1) * g
    nv = ADAM_B2 * v + (1.0 - ADAM_B2) * (g * g)
    m_hat = nm / (1.0 - ADAM_B1 ** ADAM_STEP)
    v_hat = nv / (1.0 - ADAM_B2 ** ADAM_STEP)
    return -ADAM_LR * (m_hat / (jnp.sqrt(v_hat) + ADAM_EPS) + ADAM_WD * w), nm, nv


def _adamw_shard(w, m, v, srcs, c, name, owner=0, split=None):
    depth, r, n = w.shape
    unit = math.gcd(split, r - split) if split else r
    tr = max(cand for cand in range(8, min(unit, 256) + 1, 8) if unit % cand == 0)
    sb = split // tr if split else None
    npad = srcs[0][0].shape[-1]

    def body(c_ref, w_ref, m_ref, v_ref, *refs):
        g_ref, d_ref, nm_ref, nv_ref = refs[2 * depth:]
        l, i = pl.program_id(0), pl.program_id(1)
        half = (i >= sb).astype(jnp.int32) if split else owner
        mine = c_ref[0] == half
        g = jnp.where(mine, refs[0][...], refs[1][...])
        for b in range(1, depth):
            g = jnp.where(l == b, jnp.where(mine, refs[2 * b][...], refs[2 * b + 1][...]), g)
        g = g[:, :n]
        g_ref[...] = g
        d_ref[...], nm_ref[...], nv_ref[...] = _adam_math(w_ref[...], g, m_ref[...], v_ref[...])

    def source(b):
        def index(l, i, cr):
            blk = jnp.where(i >= sb, i - sb, i) if split else i
            return (jnp.where(l == b, blk, 0), 0)
        return pl.BlockSpec((tr, npad), index)

    blk = pl.BlockSpec((None, tr, n), lambda l, i, cr: (l, i, 0))
    grid_spec = pltpu.PrefetchScalarGridSpec(
        num_scalar_prefetch=1, grid=(depth, r // tr),
        in_specs=[blk] * 3 + [source(b) for b in range(depth) for _ in range(2)], out_specs=[blk] * 4)
    return pl.pallas_call(body, name=name, grid_spec=grid_spec, out_shape=[_sds(w.shape, F32)] * 4,
                          compiler_params=_cp("parallel", "parallel"))(c, w, m, v, *[a for pair in srcs for a in pair])


def _pad_ffn_shards(w_gate, w_up, w_down):
    depth = w_gate.shape[0]
    hr = HP // 2

    def gu_body(g_ref, u_ref, o_ref):
        for which, ref in enumerate((g_ref, u_ref)):
            o_ref[which, 0:HS, :] = ref[...].astype(BF16)
            o_ref[which, HS:HP, :] = jnp.zeros((HP - HS, D), BF16)

    blk = pl.BlockSpec((None, HS, D), lambda l: (l, 0, 0))
    gu = pl.pallas_call(
        gu_body, name="pad_gate_up", grid=(depth,), in_specs=[blk, blk],
        out_specs=pl.BlockSpec((None, 2, HP, D), lambda l: (l, 0, 0, 0)),
        out_shape=_sds((depth, 2, HP, D), BF16), compiler_params=_cp("parallel"))(w_gate, w_up)

    def down_body(w_ref, o_ref):
        o_ref[0] = w_ref[0:hr, :].astype(BF16)
        o_ref[1, 0:HS - hr, :] = w_ref[hr:HS, :].astype(BF16)
        o_ref[1, HS - hr:hr, :] = jnp.zeros((HP - HS, D), BF16)

    down = pl.pallas_call(
        down_body, name="pad_down", grid=(depth,), in_specs=[pl.BlockSpec((None, HS, D), lambda l: (l, 0, 0))],
        out_specs=pl.BlockSpec((None, 2, hr, D), lambda l: (l, 0, 0, 0)),
        out_shape=_sds((depth, 2, hr, D), BF16), compiler_params=_cp("parallel"))(w_down)
    return gu, down


def _adamw_small(w, g, m, v):
    r = w.shape[0]
    tr = max(cand for cand in range(8, 513, 8) if r % cand == 0)

    def body(w_ref, g_ref, m_ref, v_ref, d_ref, nm_ref, nv_ref):
        d_ref[...], nm_ref[...], nv_ref[...] = _adam_math(w_ref[...], g_ref[...], m_ref[...], v_ref[...])

    blk = pl.BlockSpec((tr, 128), lambda i: (i, 0))
    return pl.pallas_call(body, name="adamw_small", grid=(r // tr,), in_specs=[blk] * 4, out_specs=[blk] * 3,
                          out_shape=[_sds(w.shape, F32)] * 3, compiler_params=_cp("parallel"))(w, g, m, v)


def _to_pack(a, name):
    depth = a.shape[0]
    if name in ROW_SHARDED:
        return jnp.swapaxes(a.reshape(depth, 4, -1, D), 0, 1)
    return jnp.transpose(a.reshape(depth, a.shape[1], 4, a.shape[2] // 4), (2, 0, 1, 3)).reshape(4, depth, -1, D)


def _pack_rows(parts, lead, dtype, tail=None):
    pieces, at = [], 0
    for n, off, rows in PACK:
        if off > at:
            pieces.append(jnp.zeros(lead + (off - at, D), dtype))
        pieces.append(parts[n].astype(dtype))
        at = off + rows
    if tail is not None:
        pieces.append(tail)
        at += tail.shape[-2]
    pieces.append(jnp.zeros(lead + (PACK_ROWS - at, D), dtype))
    return jnp.concatenate(pieces, axis=len(lead))


def _pack_weight_shards(sh):
    depth = sh["w_in"].shape[0]
    parts = {n: sh[n].reshape(depth, rows, D) for n, _, rows in PACK}
    conv = lax.bitcast_convert_type(sh["conv_w"].reshape(depth, 3 * 64), BF16).reshape(depth, 1, 384)
    flat = _pack_rows(parts, (depth,), BF16, tail=jnp.pad(conv, ((0, 0), (0, 0), (0, D - 384))))
    return flat.reshape(depth, 2, PACK_ROWS // 2, D)


def _unpack_weights(gathered, l, shard_shapes):
    depth = 1
    flat = gathered[:, l].reshape(4, 1, PACK_ROWS, D)
    full = {}
    for n, off, rows in PACK:
        shp = shard_shapes[n][1:]
        piece = flat[:, :, off:off + rows, :].reshape((4, depth) + shp)
        if n in ROW_SHARDED:
            full[n] = jnp.transpose(piece, (1, 0, 2, 3)).reshape(depth, 4 * shp[0], shp[1])
        else:
            full[n] = jnp.transpose(piece, (1, 2, 0, 3)).reshape(depth, shp[0], 4 * shp[1])
    conv = lax.bitcast_convert_type(flat[:, :, CONV_ROW, :384].reshape(4, depth, 192, 2), F32)
    full["conv_w"] = jnp.transpose(conv.reshape(4, depth, 3, 64), (1, 2, 0, 3)).reshape(depth, 3, CVW)
    return full


def _pack_grad_shards(g):
    depth = g["w_in"].shape[0]
    return _pack_rows({n: _to_pack(g[n], n) for n, _, _ in PACK}, (4, depth), BF16)


def _pack_small(arrs, names_shapes, depth):
    flat = jnp.concatenate([arrs[n].reshape(depth, -1) for n, _ in names_shapes], axis=1).reshape(-1)
    rows = -(-flat.shape[0] // 1024) * 8
    return jnp.pad(flat, (0, rows * 128 - flat.shape[0])).reshape(rows, 128)


def _unpack_small(packed, names_shapes, depth):
    per_layer = sum(math.prod(s) for _, s in names_shapes)
    flat = packed.reshape(-1)[:depth * per_layer].reshape(depth, per_layer)
    out, off = {}, 0
    for n, s in names_shapes:
        size = math.prod(s)
        out[n] = flat[:, off:off + size].reshape((depth,) + s)
        off += size
    return out


def kernel(x, positions, mix_pre_g, mix_post_g, ffn_pre_g, ffn_post_g, w_in, q_norm_g, w_uq, kv_norm_g, w_ukv, sg_ln_g, sg_ln_b, w_sp, b_sp, conv_w, out_norm_g, w_out, w_gate, w_up, w_down, loss_target, m_mix_pre_g, m_mix_post_g, m_ffn_pre_g, m_ffn_post_g, m_w_in, m_q_norm_g, m_w_uq, m_kv_norm_g, m_w_ukv, m_sg_ln_g, m_sg_ln_b, m_w_sp, m_b_sp, m_conv_w, m_out_norm_g, m_w_out, m_w_gate, m_w_up, m_w_down, v_mix_pre_g, v_mix_post_g, v_ffn_pre_g, v_ffn_post_g, v_w_in, v_q_norm_g, v_w_uq, v_kv_norm_g, v_w_ukv, v_sg_ln_g, v_sg_ln_b, v_w_sp, v_b_sp, v_conv_w, v_out_norm_g, v_w_out, v_w_gate, v_w_up, v_w_down):
    w = dict(mix_pre_g=mix_pre_g, mix_post_g=mix_post_g, ffn_pre_g=ffn_pre_g, ffn_post_g=ffn_post_g, w_in=w_in,
             q_norm_g=q_norm_g, w_uq=w_uq, kv_norm_g=kv_norm_g, w_ukv=w_ukv, sg_ln_g=sg_ln_g, sg_ln_b=sg_ln_b, w_sp=w_sp,
             b_sp=b_sp, conv_w=conv_w, out_norm_g=out_norm_g, w_out=w_out, w_gate=w_gate, w_up=w_up, w_down=w_down)
    m = dict(mix_pre_g=m_mix_pre_g, mix_post_g=m_mix_post_g, ffn_pre_g=m_ffn_pre_g, ffn_post_g=m_ffn_post_g, w_in=m_w_in,
             q_norm_g=m_q_norm_g, w_uq=m_w_uq, kv_norm_g=m_kv_norm_g, w_ukv=m_w_ukv, sg_ln_g=m_sg_ln_g, sg_ln_b=m_sg_ln_b,
             w_sp=m_w_sp, b_sp=m_b_sp, conv_w=m_conv_w, out_norm_g=m_out_norm_g, w_out=m_w_out, w_gate=m_w_gate,
             w_up=m_w_up, w_down=m_w_down)
    v = dict(mix_pre_g=v_mix_pre_g, mix_post_g=v_mix_post_g, ffn_pre_g=v_ffn_pre_g, ffn_post_g=v_ffn_post_g, w_in=v_w_in,
             q_norm_g=v_q_norm_g, w_uq=v_w_uq, kv_norm_g=v_kv_norm_g, w_ukv=v_w_ukv, sg_ln_g=v_sg_ln_g, sg_ln_b=v_sg_ln_b,
             w_sp=v_w_sp, b_sp=v_b_sp, conv_w=v_conv_w, out_norm_g=v_out_norm_g, w_out=v_w_out, w_gate=v_w_gate,
             w_up=v_w_up, w_down=v_w_down)
    depth = w_in.shape[0]
    c = lax.axis_index("c").astype(jnp.int32).reshape(1)
    chip = (2 * lax.axis_index("x") + lax.axis_index("y")).astype(jnp.int32)

    mine = [_pack_weight_shards(w), *_pad_ffn_shards(jnp.swapaxes(w_gate, 1, 2), jnp.swapaxes(w_up, 1, 2), w_down)]
    bufs = [lax.dynamic_update_slice(g, a[None], (chip,) + (0,) * a.ndim)
            for g, a in zip(_gather_first_layer(mine), mine)]

    small_grads = [None] * depth
    small_pair = []

    def ffn_front(l, g_gu, g_down):
        bigs = [g_gu.reshape(4, 2, HP, D), g_down.reshape(4, 2, HP // 2, D)]
        return [_pair_sum(a, r, c) for a, r in zip(bigs, _swap_halves(bigs))]

    def mixer_front(l, grads):
        small_grads[l] = grads
        bigs = [_pack_grad_shards(grads).reshape(4, 2, PACK_ROWS // 2, D)]
        if l > 0:
            rbigs = _swap_halves(bigs)
        else:
            small = _pack_small({n: jnp.concatenate([g[n] for g in small_grads]) for n, _ in SMALL}, SMALL, depth)
            *rbigs, rsmall = _swap_halves(bigs, [small])
            small_pair.append(_small_sum(jnp.stack([small, rsmall])))
        return [_pair_sum(a, r, c) for a, r in zip(bigs, rbigs)]

    loss, dx, red = _local_step(x[0], positions[0], loss_target[0], w, mine, bufs,
                                {n: w[n].shape for n, _, _ in PACK}, front=(ffn_front, mixer_front))
    loss = lax.psum(loss, ("x", "y", "c"))

    *red[0]["sent_mixer"], rs = _chip_exchange(red[0]["mixer"], small_pair[0])
    own = [[_chip_sum(p, rb, chip.reshape(1))
            for p, rb in zip(r["mixer"] + r["ffn"], list(r["sent_mixer"]) + list(r["sent_ffn"]))] for r in red]
    other = [_send_to_sibling(o) for o in own]
    g_small = _unpack_small(_small_sum(rs), SMALL, depth)
    g_small["conv_w"] = lax.dynamic_slice_in_dim(g_small["conv_w"], chip * 64, 64, axis=2)

    gw, delta, new_m, new_v = dict(g_small), {}, {}, {}

    def adam(n, srcs, turned=False, **where):
        view = (lambda a: jnp.swapaxes(a, 1, 2)) if turned else (lambda a: a)
        outs = _adamw_shard(view(w[n]), view(m[n]), view(v[n]), srcs, c, "adamw_" + n, **where)
        gw[n], delta[n], new_m[n], new_v[n] = [view(o) for o in outs]

    first = c[0] == 0
    packs = [jnp.concatenate([jnp.where(first, o[0], s[0]), jnp.where(first, s[0], o[0])]) for o, s in zip(own, other)]
    for n, off, rows in PACK:
        pieces = [pk[off:off + rows, :].reshape(w[n].shape[1:]) for pk in packs]
        adam(n, [(pc, pc) for pc in pieces], owner=0)
    adam("w_gate", [(o[1], s[1]) for o, s in zip(own, other)], turned=True, owner=0)
    adam("w_up", [(o[1], s[1]) for o, s in zip(own, other)], turned=True, owner=1)
    adam("w_down", [(o[2], s[2]) for o, s in zip(own, other)], split=HP // 2)
    small_local = tuple((n, w[n].shape[1:]) for n, _ in SMALL)
    d_, m_, v_ = _adamw_small(_pack_small(w, small_local, depth), _pack_small(gw, small_local, depth),
                              _pack_small(m, small_local, depth), _pack_small(v, small_local, depth))
    delta.update(_unpack_small(d_, small_local, depth))
    new_m.update(_unpack_small(m_, small_local, depth))
    new_v.update(_unpack_small(v_, small_local, depth))

    return (loss, dx[None], *[gw[n] for n in WEIGHTS], *[delta[n] for n in WEIGHTS], *[new_m[n] for n in WEIGHTS],
            *[new_v[n] for n in WEIGHTS])
```

```python
import math

import jax
import jax.numpy as jnp
from jax import lax
from jax.experimental import pallas as pl
from jax.experimental.pallas import tpu as pltpu

F32 = jnp.float32
BF16 = jnp.bfloat16

D = 1024
HEADS = 8
NOPE = 64
ROPE = 32
VD = 64
QR = 384
KVR = 256
SGW = 256
CVW = 256
CHUNK = 128
DFF = 2816
EPS = 1e-6
ROPE_THETA = 10000.0
LOG2E = 1.4426950408889634
LN2 = 0.6931471805599453
QSCALE = (NOPE + ROPE) ** -0.5 * LOG2E
ZA = 768
ZB = 1280
QW = HEADS * 128
KVW = HEADS * 128 + HEADS * VD
NEG = -1e30
GC0 = 0.7978845608028654
GC1 = 0.044715

ADAM_LR = 0.001
ADAM_B1 = 0.9
ADAM_B2 = 0.999
ADAM_EPS = 1e-08
ADAM_WD = 0.01
ADAM_STEP = 10

V7X_VMEM_LIMIT = 52 * 1024 * 1024
ROW_TILE = 512
ATT_TILE = 512
ATT_HEADS = 4

NT = (((1,), (1,)), ((), ()))
TN = (((0,), (0,)), ((), ()))

HS = DFF // 4
HP = 768
DFFP = 4 * HP

PACK = (("w_in", 0, 488), ("w_out", 512, 256), ("w_ukv", 768, 64), ("w_uq", 832, 72))
CONV_ROW = 904
PACK_ROWS = 928
ROW_SHARDED = ("w_out",)
SMALL = (("mix_pre_g", (D,)), ("mix_post_g", (D,)), ("ffn_pre_g", (D,)), ("ffn_post_g", (D,)), ("q_norm_g", (QR,)),
         ("kv_norm_g", (KVR,)), ("sg_ln_g", (SGW,)), ("sg_ln_b", (SGW,)), ("w_sp", (4, CHUNK, CHUNK)), ("b_sp", (4, CHUNK)),
         ("conv_w", (3, CVW)), ("out_norm_g", (D,)))
WEIGHTS = ["mix_pre_g", "mix_post_g", "ffn_pre_g", "ffn_post_g", "w_in", "q_norm_g", "w_uq", "kv_norm_g", "w_ukv", "sg_ln_g",
           "sg_ln_b", "w_sp", "b_sp", "conv_w", "out_norm_g", "w_out", "w_gate", "w_up", "w_down"]

MESH_ID = pl.DeviceIdType.MESH
ANY = pl.BlockSpec(memory_space=pl.ANY)


def _cp(*sem):
    return pltpu.CompilerParams(dimension_semantics=sem, vmem_limit_bytes=V7X_VMEM_LIMIT)


def _sds(shape, dtype):
    return jax.ShapeDtypeStruct(shape, dtype)


def _row(tm, n):
    return pl.BlockSpec((tm, n), lambda i: (i, 0))


def _lyr(l, *shape):
    return pl.BlockSpec((None,) + shape, lambda *_: (l,) + (0,) * len(shape))


def _wl(a, l):
    return 0 if a.shape[0] == 1 else l


def _pcall(body, name, grid, ins, in_specs, out_specs, out_shape, sem, scratch=(), prevs=None):
    prevs = {k: v for k, v in (prevs or {}).items() if v is not None}
    order = sorted(prevs)
    n_in = len(ins)

    def wrapped(*refs):
        return body(*refs[:n_in], *refs[n_in + len(order):])

    return pl.pallas_call(
        wrapped, name=name, grid=grid, in_specs=list(in_specs) + [ANY] * len(order), out_specs=out_specs,
        out_shape=out_shape, scratch_shapes=list(scratch),
        input_output_aliases={n_in + i: k for i, k in enumerate(order)},
        compiler_params=_cp(*sem))(*ins, *[prevs[k] for k in order])


def _rms(x, g):
    r = lax.rsqrt(jnp.mean(x * x, axis=-1, keepdims=True) + EPS)
    return x * r * g


def _rms_bwd(x, g, dy):
    r = lax.rsqrt(jnp.mean(x * x, axis=-1, keepdims=True) + EPS)
    xh = x * r
    dg = jnp.sum(dy * xh, axis=0, keepdims=True)
    dxh = dy * g
    dx = r * (dxh - xh * jnp.mean(dxh * xh, axis=-1, keepdims=True))
    return dx, dg


def _sigmoid(x):
    return 0.5 * jnp.tanh(0.5 * x) + 0.5


def _gelu(x):
    return 0.5 * x * (1.0 + jnp.tanh(GC0 * (x + GC1 * x * x * x)))


def _gelu_grad(x):
    t = jnp.tanh(GC0 * (x + GC1 * x * x * x))
    return 0.5 * (1.0 + t) + 0.5 * x * (1.0 - t * t) * GC0 * (1.0 + 3.0 * GC1 * x * x)


def _rope(xb, c, s1, s2):
    return xb * c + pltpu.roll(xb, 112, 1) * s1 + pltpu.roll(xb, 16, 1) * s2


def _rope_bwd(dy, c, s1, s2):
    return dy * c + pltpu.roll(dy * s1, 16, 1) + pltpu.roll(dy * s2, 112, 1)


def _group_masks(shape):
    lane = lax.broadcasted_iota(jnp.int32, shape, 1)
    return [(lane >= 64 * g) & (lane < 64 * g + 64) for g in range(shape[1] // 64)]


def _group_mean(v, masks):
    out = jnp.zeros_like(v)
    for m in masks:
        s = jnp.sum(jnp.where(m, v, 0.0), axis=-1, keepdims=True) * (1.0 / 64.0)
        out = jnp.where(m, s, out)
    return out


def _pick_row(blk, idx):
    row = lax.broadcasted_iota(jnp.int32, blk.shape, 0)
    return jnp.sum(jnp.where(row == idx, blk, 0.0), axis=0, keepdims=True)


def _shift_down(y, k, first_rows):
    out = pltpu.roll(y, k, 0)
    row = lax.broadcasted_iota(jnp.int32, y.shape, 0)
    for idx in range(k):
        out = jnp.where(row == idx, first_rows[idx], out)
    return out


def _shift_up(y, k, last_rows):
    n = y.shape[0]
    out = pltpu.roll(y, n - k, 0)
    row = lax.broadcasted_iota(jnp.int32, y.shape, 0)
    for idx in range(k):
        out = jnp.where(row == n - k + idx, last_rows[idx], out)
    return out


def _tril_mask():
    r = lax.broadcasted_iota(jnp.int32, (CHUNK, CHUNK), 0)
    c = lax.broadcasted_iota(jnp.int32, (CHUNK, CHUNK), 1)
    return r >= c


def _sgu_forward(zu, zv, g_ln, b_ln, wc_bf, bsp, masks, cmasks):
    u = _gelu(zu)
    vv = _gelu(zv)
    mu = _group_mean(vv, masks)
    dv = vv - mu
    rs = lax.rsqrt(_group_mean(dv * dv, masks) + EPS)
    xh = dv * rs
    vn = xh * g_ln + b_ln
    chunks = []
    for ci in range(zu.shape[0] // CHUNK):
        vc = vn[ci * CHUNK:(ci + 1) * CHUNK, :]
        acc = bsp
        for g in range(4):
            acc = acc + jnp.dot(wc_bf[g], jnp.where(cmasks[g], vc, 0.0).astype(BF16), preferred_element_type=F32)
        chunks.append(acc)
    mixed = jnp.concatenate(chunks, axis=0) if len(chunks) > 1 else chunks[0]
    return u, vv, xh, rs, vn, mixed


def _conv_forward(gc, hh, prev_gc, prev_hh, first_tile, cw):
    yv = gc * hh
    prev = jnp.where(first_tile, 0.0, prev_gc * prev_hh)
    p6, p7 = _pick_row(prev, 6), _pick_row(prev, 7)
    sh1 = _shift_down(yv, 1, [p7])
    sh2 = _shift_down(yv, 2, [p6, p7])
    conv = sh2 * cw[0:1, :] + sh1 * cw[1:2, :] + yv * cw[2:3, :]
    return yv, sh1, sh2, conv


def _acc_init(step, *refs):
    @pl.when(step == 0)
    def _():
        for r in refs:
            r[...] = jnp.zeros(r.shape, r.dtype)


def _in_proj(x, p, l):
    t = x.shape[0]
    tm = min(ROW_TILE, t)

    def body(x_ref, g_ref, wa_ref, wb_ref, h_ref, za_ref, zb_ref):
        h = _rms(x_ref[...], g_ref[...]).astype(BF16)
        h_ref[...] = h
        za_ref[...] = lax.dot_general(h, wa_ref[...], NT, preferred_element_type=F32)
        zb_ref[...] = lax.dot_general(h, wb_ref[...], NT, preferred_element_type=F32)

    return _pcall(
        body, "in_proj", (t // tm,), [x, p["mix_pre_g"], p["w_in_a"], p["w_in_b"]],
        [_row(tm, D), _lyr(l, 1, D), _lyr(_wl(p["w_in_a"], l), ZA, D), _lyr(_wl(p["w_in_b"], l), ZB, D)],
        [_row(tm, D), _row(tm, ZA), _row(tm, ZB)],
        [_sds((t, D), BF16), _sds((t, ZA), F32), _sds((t, ZB), F32)], ("parallel",))


def _mla_prep(za, p, l, tabs):
    t = za.shape[0]
    tm = min(ROW_TILE, t)

    def body(z_ref, gq_ref, gkv_ref, wuq_ref, wukv_ref, c_ref, s1_ref, s2_ref, cq_ref, ckv_ref, q_ref, k_ref, v_ref):
        z = z_ref[...]
        cq = _rms(z[:, :QR], gq_ref[...]).astype(BF16)
        ckv = _rms(z[:, QR:QR + KVR], gkv_ref[...]).astype(BF16)
        cq_ref[...] = cq
        ckv_ref[...] = ckv
        c, s1, s2 = c_ref[...], s1_ref[...], s2_ref[...]
        kr = _rope(z[:, QR + KVR:], c, s1, s2)
        q = jnp.dot(cq, wuq_ref[...], preferred_element_type=F32)
        kv = jnp.dot(ckv, wukv_ref[...], preferred_element_type=F32)
        for h in range(HEADS):
            sl = slice(128 * h, 128 * h + 128)
            q_ref[:, sl] = (_rope(q[:, sl], c, s1, s2) * QSCALE).astype(BF16)
            k_ref[:, sl] = (kv[:, sl] + kr).astype(BF16)
        v_ref[...] = kv[:, QW:].astype(BF16)

    return _pcall(
        body, "mla_prep", (t // tm,), [za, p["q_norm_g"], p["kv_norm_g"], p["w_uq"], p["w_ukv"], *tabs],
        [_row(tm, ZA), _lyr(l, 1, QR), _lyr(l, 1, KVR), _lyr(_wl(p["w_uq"], l), QR, QW), _lyr(_wl(p["w_ukv"], l), KVR, KVW),
         _row(tm, 128), _row(tm, 128), _row(tm, 128)],
        [_row(tm, QR), _row(tm, KVR), _row(tm, QW), _row(tm, QW), _row(tm, HEADS * VD)],
        [_sds((t, QR), BF16), _sds((t, KVR), BF16), _sds((t, QW), BF16), _sds((t, QW), BF16),
         _sds((t, HEADS * VD), BF16)], ("parallel",))


def _att_tile(t):
    return min(ATT_TILE, max(t // 2, 128))


def _causal_keep(tq, i, j):
    row = lax.broadcasted_iota(jnp.int32, (tq, tq), 0) + i * tq
    col = lax.broadcasted_iota(jnp.int32, (tq, tq), 1) + j * tq
    return col <= row


def _attn_fwd(qs, k, v, fetch=None):
    t = qs.shape[0]
    tq = _att_tile(t)
    nq = t // tq
    rep = tq // 128
    groups = HEADS // ATT_HEADS
    mine, bufs, fetch_layer = fetch if fetch else ((), (), None)
    nb = len(mine)

    steps = [(i, j) for i in range(nq) for j in range(i + 1)]
    i_of = jnp.asarray([s[0] for s in steps], jnp.int32)
    j_of = jnp.asarray([s[1] for s in steps], jnp.int32)

    def body(i_ref, j_ref, q_ref, k_ref, v_ref, *refs):
        o_ref, lse_ref = refs[2 * nb:2 * nb + 2]
        m_s, l_s, acc_s = refs[3 * nb + 2:3 * nb + 5]
        step_no = pl.program_id(1)
        i, j = i_ref[step_no], j_ref[step_no]
        if fetch:
            start, hand_over, drain = _gather_ops(refs[:nb], refs[2 * nb + 2:3 * nb + 2], refs[3 * nb + 5:], fetch_layer)
            pr = pl.program_id(0)
            pl.when((pr == 0) & (step_no == 0))(start)
            pl.when((pr == groups - 1) & (step_no == 3 * len(steps) // 4))(hand_over)
            pl.when((pr == groups - 1) & (step_no == len(steps) - 1))(drain)

        @pl.when(j == 0)
        def _():
            m_s[...] = jnp.full(m_s.shape, NEG, F32)
            l_s[...] = jnp.zeros(l_s.shape, F32)
            acc_s[...] = jnp.zeros(acc_s.shape, F32)

        def step(masked):
            keep = _causal_keep(tq, i, j) if masked else None
            for hh in range(ATT_HEADS):
                sl = slice(128 * hh, 128 * hh + 128)
                vv = v_ref[:, 128 * (hh // 2):128 * (hh // 2) + 128]
                s = lax.dot_general(q_ref[:, sl], k_ref[:, sl], NT, preferred_element_type=F32)
                if masked:
                    s = jnp.where(keep, s, NEG)
                m_old = m_s[hh]
                m_new = jnp.maximum(m_old, jnp.max(s, axis=-1, keepdims=True))
                alpha = jnp.exp2(m_old - m_new)
                p = jnp.exp2(s - jnp.tile(m_new, (1, rep)))
                l_s[hh] = alpha * l_s[hh] + jnp.sum(p, axis=-1, keepdims=True)
                acc_s[hh] = alpha * acc_s[hh] + jnp.dot(p.astype(BF16), vv, preferred_element_type=F32)
                m_s[hh] = m_new

        @pl.when(j < i)
        def _():
            step(False)

        @pl.when(j == i)
        def _():
            step(True)
            lane = lax.broadcasted_iota(jnp.int32, (tq, 128), 1)
            for pp in range(ATT_HEADS // 2):
                a, b = 2 * pp, 2 * pp + 1
                o_ref[:, 128 * pp:128 * pp + 128] = jnp.where(lane < VD, acc_s[a] / l_s[a], acc_s[b] / l_s[b])
            for hh in range(ATT_HEADS):
                lse_ref[hh] = (m_s[hh] + jnp.log2(l_s[hh]))[:, 0:1]

    qw, vw = 128 * ATT_HEADS, VD * ATT_HEADS
    stat = pltpu.VMEM((ATT_HEADS, tq, 128), F32)
    grid_spec = pltpu.PrefetchScalarGridSpec(
        num_scalar_prefetch=2, grid=(groups, len(steps)),
        in_specs=[pl.BlockSpec((tq, qw), lambda p, s, it, jt: (it[s], p)),
                  pl.BlockSpec((tq, qw), lambda p, s, it, jt: (jt[s], p)),
                  pl.BlockSpec((tq, vw), lambda p, s, it, jt: (jt[s], p))] + [ANY] * (2 * nb),
        out_specs=[pl.BlockSpec((tq, vw), lambda p, s, it, jt: (it[s], p)),
                   pl.BlockSpec((ATT_HEADS, tq, 1), lambda p, s, it, jt: (p, it[s], 0))] + [ANY] * nb,
        scratch_shapes=[stat, stat, stat] + ([pltpu.SemaphoreType.DMA((nb, 3))] * 4 if fetch else []))
    outs = pl.pallas_call(
        body, name="attn_fwd_fetch" if fetch else "attn_fwd", grid_spec=grid_spec,
        out_shape=[_sds((t, HEADS * VD), F32), _sds((HEADS, t, 1), F32)] + [_sds(b.shape, b.dtype) for b in bufs],
        input_output_aliases={5 + nb + b: 2 + b for b in range(nb)},
        compiler_params=_cp("arbitrary", "arbitrary"))(i_of, j_of, qs, k, v, *mine, *bufs)
    return outs[0], outs[1], list(outs[2:])


def _mixer_fwd(zb, ya, p, l):
    t = zb.shape[0]
    tm = min(ROW_TILE, t)
    hb = tm // 8

    def body(zb_ref, zprev_ref, ya_ref, gln_ref, bln_ref, wsp_ref, bsp_ref, cw_ref, go_ref, mix_ref, yb_ref, yc_ref):
        i = pl.program_id(0)
        masks = _group_masks((tm, SGW))
        cmasks = _group_masks((CHUNK, SGW))
        tril = _tril_mask()
        wc_bf = [jnp.where(tril, wsp_ref[g], 0.0).astype(BF16) for g in range(4)]
        u, _, _, _, _, mixed = _sgu_forward(zb_ref[:, 0:256], zb_ref[:, 256:512], gln_ref[...], bln_ref[...], wc_bf,
                                            bsp_ref[...], masks, cmasks)
        yb = u * mixed
        _, _, _, conv = _conv_forward(zb_ref[:, 768:1024], zb_ref[:, 1024:1280], zprev_ref[:, 768:1024],
                                      zprev_ref[:, 1024:1280], i == 0, cw_ref[...])
        yc = zb_ref[:, 512:768] * conv
        yb_ref[...] = yb
        yc_ref[...] = yc
        go = go_ref[...]
        mix_ref[:, 0:512] = _rms(ya_ref[...], go[:, 0:512]).astype(BF16)
        mix_ref[:, 512:768] = _rms(yb, go[:, 512:768]).astype(BF16)
        mix_ref[:, 768:1024] = _rms(yc, go[:, 768:1024]).astype(BF16)

    return _pcall(
        body, "mixer_fwd", (t // tm,),
        [zb, zb, ya, p["sg_ln_g"], p["sg_ln_b"], p["w_sp"], p["b_sp"], p["conv_w"], p["out_norm_g"]],
        [_row(tm, ZB), pl.BlockSpec((8, ZB), lambda i: (jnp.maximum(i * hb - 1, 0), 0)), _row(tm, 512),
         _lyr(l, 1, SGW), _lyr(l, 1, SGW), _lyr(l, 4, CHUNK, CHUNK), _lyr(l, CHUNK, SGW), _lyr(_wl(p["conv_w"], l), 3, CVW), _lyr(l, 1, D)],
        [_row(tm, D), _row(tm, SGW), _row(tm, CVW)],
        [_sds((t, D), BF16), _sds((t, SGW), F32), _sds((t, CVW), F32)], ("parallel",))


def _out_proj(mix, x, p, l):
    t = x.shape[0]
    tm = min(ROW_TILE, t)

    def body(mix_ref, w_ref, x_ref, gp_ref, gf_ref, o_ref, x2_ref, h2_ref):
        o = jnp.dot(mix_ref[...], w_ref[...], preferred_element_type=F32)
        o_ref[...] = o
        x2 = x_ref[...] + _rms(o, gp_ref[...])
        x2_ref[...] = x2
        h2_ref[...] = _rms(x2, gf_ref[...]).astype(BF16)

    return _pcall(
        body, "out_proj", (t // tm,), [mix, p["w_out"], x, p["mix_post_g"], p["ffn_pre_g"]],
        [_row(tm, D), _lyr(_wl(p["w_out"], l), D, D), _row(tm, D), _lyr(l, 1, D), _lyr(l, 1, D)],
        [_row(tm, D), _row(tm, D), _row(tm, D)],
        [_sds((t, D), F32), _sds((t, D), F32), _sds((t, D), BF16)], ("parallel",))


def _gu_all(l, which):
    return pl.BlockSpec((4, None, None, HP, D), lambda *_: (0, l, which, 0, 0))


def _down_all(l):
    return pl.BlockSpec((4, None, HP, D), lambda *_: (0, l, 0, 0))


def _ffn_up(h2, p, l):
    t = h2.shape[0]
    tm = min(ROW_TILE, t)

    def body(h_ref, wg_ref, wu_ref, a_ref, b_ref, s_ref):
        h = h_ref[...]
        a = lax.dot_general(h, wg_ref[...], NT, preferred_element_type=F32)
        b = lax.dot_general(h, wu_ref[...], NT, preferred_element_type=F32)
        a_ref[...] = a.astype(BF16)
        b_ref[...] = b.astype(BF16)
        s_ref[...] = (a * _sigmoid(a) * b).astype(BF16)

    blk = pl.BlockSpec((tm, HP), lambda k, i: (i, k))
    wblk = lambda which: pl.BlockSpec((None, None, None, HP, D), lambda k, i: (k, l, which, 0, 0))
    return _pcall(
        body, "ffn_up", (4, t // tm), [h2, p["w_gu"], p["w_gu"]],
        [pl.BlockSpec((tm, D), lambda k, i: (i, 0)), wblk(0), wblk(1)], [blk, blk, blk],
        [_sds((t, DFFP), BF16)] * 3, ("parallel", "parallel"))


def _ffn_down(s, x2, p, l):
    t = x2.shape[0]
    tm = min(ROW_TILE, t)

    def body(s_ref, w_ref, x_ref, g_ref, f_ref, x3_ref):
        f = jnp.dot(s_ref[:, 0:HP], w_ref[0], preferred_element_type=F32)
        for k in range(1, 4):
            f = f + jnp.dot(s_ref[:, k * HP:(k + 1) * HP], w_ref[k], preferred_element_type=F32)
        f_ref[...] = f
        x3_ref[...] = x_ref[...] + _rms(f, g_ref[...])

    return _pcall(
        body, "ffn_down", (t // tm,), [s, p["w_down"], x2, p["ffn_post_g"]],
        [_row(tm, DFFP), _down_all(l), _row(tm, D), _lyr(l, 1, D)], [_row(tm, D), _row(tm, D)],
        [_sds((t, D), F32), _sds((t, D), F32)], ("parallel",))


def _loss_head(y, target):
    t = y.shape[0]
    tm = min(ROW_TILE, t)

    def body(y_ref, t_ref, dy_ref, acc_ref):
        e = y_ref[...] - t_ref[...]
        dy_ref[...] = e * (1.0 / D)
        sq = jnp.sum(e * e, axis=0, keepdims=True)
        part = sq[:, 0:128]
        for b in range(1, D // 128):
            part = part + sq[:, 128 * b:128 * b + 128]
        _acc_init(pl.program_id(0), acc_ref)
        acc_ref[...] += part

    return _pcall(body, "loss_head", (t // tm,), [y, target], [_row(tm, D), _row(tm, D)],
                  [_row(tm, D), pl.BlockSpec((1, 128), lambda i: (0, 0))],
                  [_sds((t, D), F32), _sds((1, 128), F32)], ("arbitrary",))


def _ffn_down_bwd(dx3, sv, p, l, depth, gb):
    t = dx3.shape[0]
    tm = min(256, t)

    def body(dx_ref, f_ref, g_ref, w_ref, a_ref, b_ref, df_ref, da_ref, db_ref, dg_ref):
        _acc_init(pl.program_id(0), dg_ref)
        df, dg = _rms_bwd(f_ref[...], g_ref[...], dx_ref[...])
        dg_ref[...] += dg
        df = df.astype(BF16)
        df_ref[...] = df
        for k in range(4):
            sl = slice(k * HP, (k + 1) * HP)
            ds = lax.dot_general(df, w_ref[k], NT, preferred_element_type=F32)
            av = a_ref[:, sl].astype(F32)
            sig = _sigmoid(av)
            da_ref[:, sl] = (ds * b_ref[:, sl].astype(F32) * (sig * (1.0 + av * (1.0 - sig)))).astype(BF16)
            db_ref[:, sl] = (ds * (av * sig)).astype(BF16)

    df, da, db, gb["ffn_post_g"] = _pcall(
        body, "ffn_down_bwd", (t // tm,), [dx3, sv["f"], p["ffn_post_g"], p["w_down"], sv["a"], sv["b"]],
        [_row(tm, D), _row(tm, D), _lyr(l, 1, D), _down_all(l), _row(tm, DFFP), _row(tm, DFFP)],
        [_row(tm, D), _row(tm, DFFP), _row(tm, DFFP), _lyr(l.g, 1, D)],
        [_sds((t, D), BF16), _sds((t, DFFP), BF16), _sds((t, DFFP), BF16), _sds((depth, 1, D), F32)], ("arbitrary",),
        prevs={3: gb.get("ffn_post_g")})
    return df, da, db


def _ffn_up_bwd(da, db, dx3, sv, p, l, depth, gb):
    t = dx3.shape[0]
    tm = min(256, t)

    def body(da_ref, db_ref, wg_ref, wu_ref, x_ref, dx3_ref, g_ref, dx2_ref, dg_ref):
        _acc_init(pl.program_id(0), dg_ref)
        dh = jnp.zeros((tm, D), F32)
        for k in range(4):
            sl = slice(k * HP, (k + 1) * HP)
            dh = dh + jnp.dot(da_ref[:, sl], wg_ref[k], preferred_element_type=F32)
            dh = dh + jnp.dot(db_ref[:, sl], wu_ref[k], preferred_element_type=F32)
        dx, dg = _rms_bwd(x_ref[...], g_ref[...], dh)
        dg_ref[...] += dg
        dx2_ref[...] = dx3_ref[...] + dx

    dx2, gb["ffn_pre_g"] = _pcall(
        body, "ffn_up_bwd", (t // tm,), [da, db, p["w_gu"], p["w_gu"], sv["x2"], dx3, p["ffn_pre_g"]],
        [_row(tm, DFFP), _row(tm, DFFP), _gu_all(l, 0), _gu_all(l, 1), _row(tm, D), _row(tm, D), _lyr(l, 1, D)],
        [_row(tm, D), _lyr(l.g, 1, D)], [_sds((t, D), F32), _sds((depth, 1, D), F32)], ("arbitrary",),
        prevs={1: gb.get("ffn_pre_g")})
    return dx2


def _out_proj_bwd(dx2, sv, p, l, depth, gb):
    t = dx2.shape[0]
    tm = min(ROW_TILE, t)

    def body(dx_ref, o_ref, g_ref, w_ref, do_ref, dmix_ref, dg_ref):
        _acc_init(pl.program_id(0), dg_ref)
        do, dg = _rms_bwd(o_ref[...], g_ref[...], dx_ref[...])
        dg_ref[...] += dg
        do = do.astype(BF16)
        do_ref[...] = do
        dmix_ref[...] = lax.dot_general(do, w_ref[...], NT, preferred_element_type=F32)

    do, dmix, gb["mix_post_g"] = _pcall(
        body, "out_proj_bwd", (t // tm,), [dx2, sv["o"], p["mix_post_g"], p["w_out"]],
        [_row(tm, D), _row(tm, D), _lyr(l, 1, D), _lyr(_wl(p["w_out"], l), D, D)], [_row(tm, D), _row(tm, D), _lyr(l.g, 1, D)],
        [_sds((t, D), BF16), _sds((t, D), F32), _sds((depth, 1, D), F32)], ("arbitrary",),
        prevs={2: gb.get("mix_post_g")})
    return do, dmix


def _mixer_bwd(dmix, sv, p, l, depth, gb):
    zb = sv["zb"]
    t = zb.shape[0]
    tm = min(ROW_TILE, t)
    hb = tm // 8
    last_blk = t // 8 - 1
    nsteps = t // tm

    def body(dmix_ref, ya_ref, yb_ref, yc_ref, zb_ref, zprev_ref, znext_ref, ycn_ref, dmn_ref,
             gln_ref, bln_ref, wsp_ref, bsp_ref, cw_ref, go_ref,
             dya_ref, dzb_ref, delta_ref, dgo_ref, dgln_ref, dbln_ref, dwsp_ref, dbsp_ref, dcw_ref):
        i = pl.program_id(0)
        _acc_init(i, dgo_ref, dgln_ref, dbln_ref, dwsp_ref, dbsp_ref, dcw_ref)
        go = go_ref[...]
        dmix = dmix_ref[...]

        ya = ya_ref[...]
        dya, dga = _rms_bwd(ya, go[:, 0:512], dmix[:, 0:512])
        dyb, dgb_ = _rms_bwd(yb_ref[...], go[:, 512:768], dmix[:, 512:768])
        dyc, dgc_ = _rms_bwd(yc_ref[...], go[:, 768:1024], dmix[:, 768:1024])
        dgo_ref[:, 0:512] += dga
        dgo_ref[:, 512:768] += dgb_
        dgo_ref[:, 768:1024] += dgc_
        dya = dya * LN2
        dya_ref[...] = dya.astype(BF16)
        prod = dya * ya
        hmasks = _group_masks((tm, 512))
        for h in range(HEADS):
            delta_ref[h] = jnp.sum(jnp.where(hmasks[h], prod, 0.0), axis=-1, keepdims=True)

        masks = _group_masks((tm, SGW))
        cmasks = _group_masks((CHUNK, SGW))
        tril = _tril_mask()
        wc_bf = [jnp.where(tril, wsp_ref[g], 0.0).astype(BF16) for g in range(4)]
        zu, zv = zb_ref[:, 0:256], zb_ref[:, 256:512]
        g_ln = gln_ref[...]
        u, _, xh, rs, vn, mixed = _sgu_forward(zu, zv, g_ln, bln_ref[...], wc_bf, bsp_ref[...], masks, cmasks)
        du = dyb * mixed
        dmixed = dyb * u
        dvn_chunks = []
        dbsp = jnp.zeros((CHUNK, SGW), F32)
        for ci in range(tm // CHUNK):
            rows = slice(ci * CHUNK, (ci + 1) * CHUNK)
            dm_c = dmixed[rows, :]
            vn_c = vn[rows, :].astype(BF16)
            dbsp = dbsp + dm_c
            dvn_c = jnp.zeros((CHUNK, SGW), F32)
            for g in range(4):
                dm_g = jnp.where(cmasks[g], dm_c, 0.0).astype(BF16)
                dw = lax.dot_general(dm_g, vn_c, NT, preferred_element_type=F32)
                dwsp_ref[g] += jnp.where(tril, dw, 0.0)
                dvn_c = dvn_c + lax.dot_general(wc_bf[g], dm_g, TN, preferred_element_type=F32)
            dvn_chunks.append(dvn_c)
        dbsp_ref[...] += dbsp
        dvn = jnp.concatenate(dvn_chunks, axis=0) if len(dvn_chunks) > 1 else dvn_chunks[0]
        dgln_ref[...] += jnp.sum(dvn * xh, axis=0, keepdims=True)
        dbln_ref[...] += jnp.sum(dvn, axis=0, keepdims=True)
        dxh = dvn * g_ln
        dvv = rs * (dxh - _group_mean(dxh, masks) - xh * _group_mean(dxh * xh, masks))
        dzb_ref[:, 0:256] = (du * _gelu_grad(zu)).astype(BF16)
        dzb_ref[:, 256:512] = (dvv * _gelu_grad(zv)).astype(BF16)

        cwv = cw_ref[...]
        gb_, gc, hh = zb_ref[:, 512:768], zb_ref[:, 768:1024], zb_ref[:, 1024:1280]
        yv, sh1, sh2, conv = _conv_forward(gc, hh, zprev_ref[:, 768:1024], zprev_ref[:, 1024:1280], i == 0, cwv)
        dconv = dyc * gb_
        dzb_ref[:, 512:768] = (dyc * conv).astype(BF16)
        dcw_ref[0:1, :] += jnp.sum(dconv * sh2, axis=0, keepdims=True)
        dcw_ref[1:2, :] += jnp.sum(dconv * sh1, axis=0, keepdims=True)
        dcw_ref[2:3, :] += jnp.sum(dconv * yv, axis=0, keepdims=True)
        dycn, _ = _rms_bwd(ycn_ref[...], go[:, 768:1024], dmn_ref[...])
        dconv_next = jnp.where(i == nsteps - 1, 0.0, dycn * znext_ref[:, 512:768])
        n0, n1 = _pick_row(dconv_next, 0), _pick_row(dconv_next, 1)
        dyv = dconv * cwv[2:3, :] + _shift_up(dconv, 1, [n0]) * cwv[1:2, :] + _shift_up(dconv, 2, [n0, n1]) * cwv[0:1, :]
        dzb_ref[:, 768:1024] = (dyv * hh).astype(BF16)
        dzb_ref[:, 1024:1280] = (dyv * gc).astype(BF16)

    prev_map = lambda i: (jnp.maximum(i * hb - 1, 0), 0)
    next_map = lambda i: (jnp.minimum((i + 1) * hb, last_blk), 0)
    names = ("out_norm_g", "sg_ln_g", "sg_ln_b", "w_sp", "b_sp_t", "conv_w")
    shapes = ((1, D), (1, SGW), (1, SGW), (4, CHUNK, CHUNK), (CHUNK, SGW), (3, CVW))
    outs = _pcall(
        body, "mixer_bwd", (nsteps,),
        [dmix, sv["ya"], sv["yb"], sv["yc"], zb, zb, zb, sv["yc"], dmix, p["sg_ln_g"], p["sg_ln_b"], p["w_sp"], p["b_sp"],
         p["conv_w"], p["out_norm_g"]],
        [_row(tm, D), _row(tm, 512), _row(tm, SGW), _row(tm, CVW), _row(tm, ZB),
         pl.BlockSpec((8, ZB), prev_map), pl.BlockSpec((8, ZB), next_map), pl.BlockSpec((8, CVW), next_map),
         pl.BlockSpec((8, 256), lambda i: (jnp.minimum((i + 1) * hb, last_blk), 3)),
         _lyr(l, 1, SGW), _lyr(l, 1, SGW), _lyr(l, 4, CHUNK, CHUNK), _lyr(l, CHUNK, SGW), _lyr(_wl(p["conv_w"], l), 3, CVW), _lyr(l, 1, D)],
        [_row(tm, 512), _row(tm, ZB), pl.BlockSpec((HEADS, tm, 1), lambda i: (0, i, 0))] + [_lyr(l.g, *s) for s in shapes],
        [_sds((t, 512), BF16), _sds((t, ZB), BF16), _sds((HEADS, t, 1), F32)] + [_sds((depth,) + s, F32) for s in shapes],
        ("arbitrary",), prevs={3 + n: gb.get(name) for n, name in enumerate(names)})
    for n, name in enumerate(names):
        gb[name] = outs[3 + n]
    return outs[0], outs[1], outs[2]


def _attn_bwd(qs, k, v, dya, lse, delta, ride=()):
    t = qs.shape[0]
    tq = _att_tile(t)
    nq = t // tq
    nb = len(ride)
    groups = HEADS // ATT_HEADS

    steps = [(j, i) for j in range(nq) for i in range(j, nq)]
    j_of = jnp.asarray([s[0] for s in steps], jnp.int32)
    i_of = jnp.asarray([s[1] for s in steps], jnp.int32)

    def body(j_ref, i_ref, q_ref, k_ref, v_ref, do_ref, lse_ref, dl_ref, *refs):
        dq_ref, dk_ref, dv_ref = refs[nb:nb + 3]
        dq_s, dk_s, dv_s = refs[2 * nb + 3:2 * nb + 6]
        step_no = pl.program_id(1)
        j, i = j_ref[step_no], i_ref[step_no]
        if ride:
            start, finish = _exchange_ops(refs[:nb], refs[nb + 3:2 * nb + 3], refs[2 * nb + 6:])
            pr = pl.program_id(0)
            pl.when((pr == 0) & (step_no == 0))(start)
            pl.when((pr == groups - 1) & (step_no == len(steps) - 1))(finish)

        @pl.when(step_no == 0)
        def _():
            dq_s[...] = jnp.zeros(dq_s.shape, F32)

        def step(masked):
            keep = _causal_keep(tq, 0, 0) if masked else None
            lane = lax.broadcasted_iota(jnp.int32, (tq, 128), 1)
            rows = pl.ds(pl.multiple_of(i * tq, tq), tq)
            for hh in range(ATT_HEADS):
                sl = slice(128 * hh, 128 * hh + 128)
                pair = slice(128 * (hh // 2), 128 * (hh // 2) + 128)
                vv, do = v_ref[:, pair], do_ref[:, pair]
                qq, kk = q_ref[:, sl], k_ref[:, sl]
                s = lax.dot_general(qq, kk, NT, preferred_element_type=F32)
                p = jnp.exp2(s - lse_ref[hh])
                if masked:
                    p = jnp.where(keep, p, 0.0)
                do_h = jnp.where((lane < VD) if hh % 2 == 0 else (lane >= VD), do, jnp.zeros_like(do))
                dp = lax.dot_general(do_h, vv, NT, preferred_element_type=F32)
                ds = (p * (dp - dl_ref[hh])).astype(BF16)
                dv_s[:, pair] += lax.dot_general(p.astype(BF16), do_h, TN, preferred_element_type=F32)
                dk_s[:, sl] += lax.dot_general(ds, qq, TN, preferred_element_type=F32)
                dq_s[rows, sl] += jnp.dot(ds, kk, preferred_element_type=F32)

        @pl.when(i == j)
        def _():
            dk_s[...] = jnp.zeros(dk_s.shape, F32)
            dv_s[...] = jnp.zeros(dv_s.shape, F32)
            step(True)

        @pl.when(i > j)
        def _():
            step(False)

        @pl.when(i == nq - 1)
        def _():
            dk_ref[...] = dk_s[...].astype(BF16)
            dv_ref[...] = (dv_s[...] * LOG2E).astype(BF16)

        @pl.when(step_no == len(steps) - 1)
        def _():
            dq_ref[...] = dq_s[...].astype(BF16)

    qw, vw = 128 * ATT_HEADS, VD * ATT_HEADS
    qrow = lambda p, s, jt, it: (it[s], p)
    krow = lambda p, s, jt, it: (jt[s], p)
    col_spec = pl.BlockSpec((ATT_HEADS, tq, 1), lambda p, s, jt, it: (p, it[s], 0))
    grid_spec = pltpu.PrefetchScalarGridSpec(
        num_scalar_prefetch=2, grid=(groups, len(steps)),
        in_specs=[pl.BlockSpec((tq, qw), qrow), pl.BlockSpec((tq, qw), krow), pl.BlockSpec((tq, vw), krow),
                  pl.BlockSpec((tq, vw), qrow), col_spec, col_spec] + [ANY] * nb,
        out_specs=[pl.BlockSpec((t, qw), lambda p, s, jt, it: (0, p)), pl.BlockSpec((tq, qw), krow),
                   pl.BlockSpec((tq, vw), krow)] + [ANY] * nb,
        scratch_shapes=[pltpu.VMEM((t, qw), F32), pltpu.VMEM((tq, qw), F32), pltpu.VMEM((tq, vw), F32)]
        + ([pltpu.SemaphoreType.DMA((nb, 3))] * 2 if ride else []))
    outs = pl.pallas_call(
        body, name="attn_bwd_ride" if ride else "attn_bwd", grid_spec=grid_spec,
        out_shape=[_sds((t, QW), BF16), _sds((t, QW), BF16), _sds((t, HEADS * VD), BF16)]
        + [_sds((3,) + a.shape[1:], a.dtype) for a in ride],
        compiler_params=_cp("arbitrary", "arbitrary"))(j_of, i_of, qs, k, v, dya, lse, delta, *ride)
    return outs[0], outs[1], outs[2], list(outs[3:])


def _mla_prep_bwd(dqs, dk, dv, sv, p, l, depth, gb, tabs):
    za = sv["za"]
    t = za.shape[0]
    tm = min(ROW_TILE, t)

    def body(dq_ref, dk_ref, dv_ref, z_ref, gq_ref, gkv_ref, wuq_ref, wukv_ref, c_ref, s1_ref, s2_ref,
             dza_ref, dqp_ref, dkv_ref, dgq_ref, dgkv_ref):
        _acc_init(pl.program_id(0), dgq_ref, dgkv_ref)
        c, s1, s2 = c_ref[...], s1_ref[...], s2_ref[...]
        lane = lax.broadcasted_iota(jnp.int32, (tm, 128), 1)
        rope_lanes = (lane >= NOPE) & (lane < NOPE + ROPE)
        dkr = jnp.zeros((tm, 128), F32)
        for h in range(HEADS):
            sl = slice(128 * h, 128 * h + 128)
            dqp_ref[:, sl] = _rope_bwd(dq_ref[:, sl].astype(F32) * QSCALE, c, s1, s2).astype(BF16)
            dkh = dk_ref[:, sl]
            dkv_ref[:, sl] = dkh
            dkr = dkr + jnp.where(rope_lanes, dkh.astype(F32), 0.0)
        dkv_ref[:, QW:] = dv_ref[...]
        z = z_ref[...]
        dcq = lax.dot_general(dqp_ref[...], wuq_ref[...], NT, preferred_element_type=F32)
        dzq, dgq = _rms_bwd(z[:, :QR], gq_ref[...], dcq)
        dckv = lax.dot_general(dkv_ref[...], wukv_ref[...], NT, preferred_element_type=F32)
        dzkv, dgkv = _rms_bwd(z[:, QR:QR + KVR], gkv_ref[...], dckv)
        dgq_ref[...] += dgq
        dgkv_ref[...] += dgkv
        dza_ref[:, :QR] = dzq.astype(BF16)
        dza_ref[:, QR:QR + KVR] = dzkv.astype(BF16)
        dza_ref[:, QR + KVR:] = _rope_bwd(dkr, c, s1, s2).astype(BF16)

    dza, dqp, dkv, gb["q_norm_g"], gb["kv_norm_g"] = _pcall(
        body, "mla_prep_bwd", (t // tm,),
        [dqs, dk, dv, za, p["q_norm_g"], p["kv_norm_g"], p["w_uq"], p["w_ukv"], *tabs],
        [_row(tm, QW), _row(tm, QW), _row(tm, HEADS * VD), _row(tm, ZA), _lyr(l, 1, QR), _lyr(l, 1, KVR),
         _lyr(_wl(p["w_uq"], l), QR, QW), _lyr(_wl(p["w_ukv"], l), KVR, KVW), _row(tm, 128), _row(tm, 128), _row(tm, 128)],
        [_row(tm, ZA), _row(tm, QW), _row(tm, KVW), _lyr(l.g, 1, QR), _lyr(l.g, 1, KVR)],
        [_sds((t, ZA), BF16), _sds((t, QW), BF16), _sds((t, KVW), BF16), _sds((depth, 1, QR), F32),
         _sds((depth, 1, KVR), F32)], ("arbitrary",), prevs={3: gb.get("q_norm_g"), 4: gb.get("kv_norm_g")})
    return dza, dqp, dkv


def _in_proj_bwd(dza, dzb, dx2, sv, p, l, depth, gb):
    t = dx2.shape[0]
    tm = min(ROW_TILE, t)

    def body(dza_ref, dzb_ref, wa_ref, wb_ref, x_ref, dx2_ref, g_ref, dx_ref, dg_ref):
        _acc_init(pl.program_id(0), dg_ref)
        dh = (jnp.dot(dza_ref[...], wa_ref[...], preferred_element_type=F32)
              + jnp.dot(dzb_ref[...], wb_ref[...], preferred_element_type=F32))
        dx, dg = _rms_bwd(x_ref[...], g_ref[...], dh)
        dg_ref[...] += dg
        dx_ref[...] = dx2_ref[...] + dx

    dx, gb["mix_pre_g"] = _pcall(
        body, "in_proj_bwd", (t // tm,), [dza, dzb, p["w_in_a"], p["w_in_b"], sv["x"], dx2, p["mix_pre_g"]],
        [_row(tm, ZA), _row(tm, ZB), _lyr(_wl(p["w_in_a"], l), ZA, D), _lyr(_wl(p["w_in_b"], l), ZB, D), _row(tm, D), _row(tm, D), _lyr(l, 1, D)],
        [_row(tm, D), _lyr(l.g, 1, D)], [_sds((t, D), F32), _sds((depth, 1, D), F32)], ("arbitrary",),
        prevs={1: gb.get("mix_pre_g")})
    return dx


def _mm_tn(a, b, tn, name, l, depth, gb):
    t, k = a.shape
    n = b.shape[1]
    tt = min(ROW_TILE, t)

    def body(a_ref, b_ref, o_ref):
        _acc_init(pl.program_id(1), o_ref)
        o_ref[...] += lax.dot_general(a_ref[...], b_ref[...], TN, preferred_element_type=F32)

    gb[name] = _pcall(
        body, "d" + name, (n // tn, t // tt), [a, b],
        [pl.BlockSpec((tt, k), lambda j, s: (s, 0)), pl.BlockSpec((tt, tn), lambda j, s: (s, j))],
        pl.BlockSpec((None, k, tn), lambda j, s: (l.g, 0, j)), _sds((depth, k, n), F32), ("parallel", "arbitrary"),
        prevs={0: gb.get(name)})


def _dw_ffn(a, b, kind, l, depth, gb):
    t = a.shape[0]
    tt = min(ROW_TILE, t)
    nsteps = t // tt

    def body(a_ref, b_ref, o_ref, acc):
        s = pl.program_id(0)
        _acc_init(s, acc)
        acc[...] += lax.dot_general(a_ref[...], b_ref[...], TN, preferred_element_type=F32)

        @pl.when(s == nsteps - 1)
        def _():
            for k in range(4):
                o_ref[k] = acc[k * HP:(k + 1) * HP, :].astype(BF16)

    rows = lambda n: pl.BlockSpec((tt, n), lambda s: (s, 0))
    if kind == "down":
        name = "down"
        out_spec = pl.BlockSpec((4, None, HP, D), lambda s: (0, l.g, 0, 0))
        out_shape = _sds((4, depth, HP, D), BF16)
    else:
        which = 0 if kind == "gate" else 1
        name = "gu"
        out_spec = pl.BlockSpec((4, None, None, HP, D), lambda s: (0, l.g, which, 0, 0))
        out_shape = _sds((4, depth, 2, HP, D), BF16)
    gb[name] = _pcall(body, "dw_" + kind, (nsteps,), [a, b], [rows(DFFP), rows(D)], out_spec, out_shape, ("arbitrary",),
                      scratch=[pltpu.VMEM((DFFP, D), F32)], prevs={0: gb.get(name)})


def _ffn_views(bufs):
    return {"w_gu": bufs[1], "w_down": bufs[2].reshape(bufs[2].shape[:2] + (HP, D))}


def _layer_fwd(x, p, l, tabs, fetch):
    h1, za, zb = _in_proj(x, p, l)
    cqn, ckvn, qs, k, v = _mla_prep(za, p, l, tabs)
    ya, lse, bufs = _attn_fwd(qs, k, v, fetch)
    if fetch:
        p = {**p, **_ffn_views(bufs)}
    mix, yb, yc = _mixer_fwd(zb, ya, p, l)
    o, x2, h2 = _out_proj(mix, x, p, l)
    a, b, s = _ffn_up(h2, p, l)
    f, x3 = _ffn_down(s, x2, p, l)
    saved = dict(x=x, h1=h1, za=za, zb=zb, cqn=cqn, ckvn=ckvn, qs=qs, k=k, v=v, ya=ya, lse=lse, mix=mix, yb=yb, yc=yc,
                 o=o, x2=x2, h2=h2, a=a, b=b, s=s, f=f)
    return x3, saved, bufs if fetch else None


class _Layer(int):
    def __new__(cls, l, g):
        obj = int.__new__(cls, l)
        obj.g = g
        return obj


def _layer_bwd(dx3, p, sv, l, depth, gb, tabs, ride=(), ffn_front=None):
    df, da, db = _ffn_down_bwd(dx3, sv, p, l, depth, gb)
    _dw_ffn(sv["s"], df, "down", l, depth, gb)
    dx2 = _ffn_up_bwd(da, db, dx3, sv, p, l, depth, gb)
    _dw_ffn(da, sv["h2"], "gate", l, depth, gb)
    _dw_ffn(db, sv["h2"], "up", l, depth, gb)
    ffn_blocks = list(ffn_front(gb["gu"], gb["down"])) if ffn_front else []
    do, dmix = _out_proj_bwd(dx2, sv, p, l, depth, gb)
    _mm_tn(sv["mix"], do, D, "w_out", l, depth, gb)
    dya, dzb, delta = _mixer_bwd(dmix, sv, p, l, depth, gb)
    dqs, dk, dv, sent = _attn_bwd(sv["qs"], sv["k"], sv["v"], dya, sv["lse"], delta, tuple(ffn_blocks) + tuple(ride))
    dza, dqp, dkv = _mla_prep_bwd(dqs, dk, dv, sv, p, l, depth, gb, tabs)
    _mm_tn(sv["cqn"], dqp, QW, "w_uq", l, depth, gb)
    _mm_tn(sv["ckvn"], dkv, KVW, "w_ukv", l, depth, gb)
    _mm_tn(dza, sv["h1"], D, "w_in_a", l, depth, gb)
    _mm_tn(dzb, sv["h1"], D, "w_in_b", l, depth, gb)
    nf = len(ffn_blocks)
    return _in_proj_bwd(dza, dzb, dx2, sv, p, l, depth, gb), ffn_blocks, sent[:nf], sent[nf:]


def _rope_tables(positions):
    inv_freq = 1.0 / (ROPE_THETA ** (jnp.arange(0, ROPE // 2, dtype=F32) / (ROPE // 2)))
    ang = positions.astype(F32)[:, None] * inv_freq
    cos, sin = jnp.cos(ang), jnp.sin(ang)
    t = positions.shape[0]
    one, zero = jnp.ones((t, 64), F32), jnp.zeros((t, 16), F32)
    c = jnp.concatenate([one, cos, cos, one[:, :32]], axis=1)
    s1 = jnp.concatenate([zero, zero, zero, zero, -sin, zero, zero, zero], axis=1)
    s2 = jnp.concatenate([zero, zero, zero, zero, zero, sin, zero, zero], axis=1)
    return c, s1, s2


def _mixer_weight_params(full):
    w_in = full["w_in_t"]
    depth = w_in.shape[0]
    zpad = lambda n: jnp.zeros((depth, n, D), w_in.dtype)
    kv = full["w_ukv"].reshape(depth, KVR, HEADS, NOPE + VD)
    return {
        "w_in_a": jnp.concatenate([w_in[:, :640], zpad(64), w_in[:, 640:672], zpad(32)], axis=1),
        "w_in_b": w_in[:, 672:],
        "w_uq": jnp.pad(full["w_uq"].reshape(depth, QR, HEADS, NOPE + ROPE),
                        ((0, 0), (0, 0), (0, 0), (0, 32))).reshape(depth, QR, QW),
        "w_ukv": jnp.concatenate([jnp.pad(kv[..., :NOPE], ((0, 0), (0, 0), (0, 0), (0, 64))).reshape(depth, KVR, QW),
                                  kv[..., NOPE:].reshape(depth, KVR, HEADS * VD)], axis=2),
        "w_out": full["w_out"], "conv_w": full["conv_w"],
    }


def _small_params(w):
    p = {"w_sp": w["w_sp"], "b_sp": jnp.repeat(jnp.swapaxes(w["b_sp"], 1, 2), 64, axis=2)}
    for n in ("mix_pre_g", "mix_post_g", "ffn_pre_g", "ffn_post_g", "q_norm_g", "kv_norm_g", "sg_ln_g", "sg_ln_b",
              "out_norm_g"):
        p[n] = w[n][:, None, :]
    return p


def _natural_grads(gb):
    depth = gb["w_in_a"].shape[0]
    ga, kv = gb["w_in_a"], gb["w_ukv"]
    out = {
        "w_in_t": jnp.concatenate([ga[:, :640], ga[:, 704:736], gb["w_in_b"]], axis=1),
        "w_uq": gb["w_uq"].reshape(depth, QR, HEADS, 128)[..., :NOPE + ROPE].reshape(depth, QR, HEADS * (NOPE + ROPE)),
        "w_ukv": jnp.concatenate([kv[:, :, :QW].reshape(depth, KVR, HEADS, 128)[..., :NOPE],
                                  kv[:, :, QW:].reshape(depth, KVR, HEADS, VD)], axis=3).reshape(depth, KVR, -1),
        "b_sp": jnp.swapaxes(gb["b_sp_t"].reshape(depth, CHUNK, 4, 64).sum(axis=-1), 1, 2),
    }
    for n in ("w_out", "w_sp", "conv_w"):
        out[n] = gb[n]
    for n in ("mix_pre_g", "mix_post_g", "ffn_pre_g", "ffn_post_g", "q_norm_g", "kv_norm_g", "sg_ln_g", "sg_ln_b",
              "out_norm_g"):
        out[n] = gb[n][:, 0, :]
    return out


def _local_step(x, positions, target, small, mine, bufs, shard_shapes, fetch=True, front=None):
    depth = small["w_sp"].shape[0]
    tabs = _rope_tables(positions)
    ps = _small_params(small)
    saved, mixer_w = [], []
    for l in range(depth):
        mixer_w.append(_mixer_weight_params(_unpack_weights(bufs[0], l, shard_shapes)))
        p = {**ps, **mixer_w[l], **_ffn_views(bufs)}
        layers = [l + 1 if l + 1 < depth else None, l, l]
        x, sv, fetched = _layer_fwd(x, p, l, tabs, (mine, bufs, layers) if fetch else None)
        bufs = fetched or bufs
        saved.append(sv)
    dx, acc = _loss_head(x, target)
    loss = (0.5 / D) * jnp.sum(acc)
    gbs = [{} for _ in range(depth)]
    out = [{} for _ in range(depth)]
    ride = ()
    for l in reversed(range(depth)):
        ffn_front = (lambda gu, down, l=l: front[0](l, gu, down)) if front else None
        dx, out[l]["ffn"], out[l]["sent_ffn"], got = _layer_bwd(
            dx, {**ps, **mixer_w[l], **_ffn_views(bufs)}, saved[l], _Layer(l, 0), 1, gbs[l], tabs, ride, ffn_front)
        if ride:
            out[l + 1]["sent_mixer"], ride = got, ()
        grads = _natural_grads(gbs[l])
        if front:
            out[l]["mixer"] = front[1](l, grads)
            if l > 0:
                ride = tuple(out[l]["mixer"])
        else:
            out[l]["grads"] = (grads, gbs[l]["gu"], gbs[l]["down"])
    return loss, dx, out


def _place():
    x, y, c = lax.axis_index("x"), lax.axis_index("y"), lax.axis_index("c")
    chips = [(1 - x, y), (x, 1 - y), (1 - x, 1 - y)]
    return x, y, c, 2 * x + y, chips


def _remote(src, dst, send_sem, recv_sem, to):
    return pltpu.make_async_remote_copy(src_ref=src, dst_ref=dst, send_sem=send_sem, recv_sem=recv_sem, device_id=to,
                                        device_id_type=MESH_ID)


def _gather_ops(mine_refs, out_refs, sems, layers):
    send_sems, recv_sems, fsend_sems, frecv_sems = sems
    x, y, c, k, chips = _place()
    sib = (x, y, 1 - c)
    pairs = [(b, n) for n in range(3) for b in range(len(mine_refs)) if layers[b] is not None]

    def slot(n):
        return 2 * chips[n][0] + chips[n][1]

    def ici(b, n, dst_chip):
        return _remote(mine_refs[b].at[layers[b], c], out_refs[b].at[dst_chip, layers[b], c], send_sems.at[b, n],
                       recv_sems.at[b, n], (*chips[n], c))

    def d2d(b, n, half):
        piece = out_refs[b].at[slot(n), layers[b], half]
        return _remote(piece, piece, fsend_sems.at[b, n], frecv_sems.at[b, n], sib)

    def start():
        for b, n in pairs:
            ici(b, n, k).start()

    def hand_over():
        for b, n in pairs:
            ici(b, n, slot(n)).wait_recv()
            d2d(b, n, c).start()

    def drain():
        for b, n in pairs:
            d2d(b, n, 1 - c).wait_recv()
        for b, n in pairs:
            ici(b, n, k).wait_send()
            d2d(b, n, c).wait_send()

    return start, hand_over, drain


def _gather_first_layer(mine):
    nb = len(mine)

    def body(*refs):
        start, hand_over, drain = _gather_ops(refs[:nb], refs[nb:2 * nb], refs[2 * nb:], [0] + [None] * (nb - 1))
        start()
        hand_over()
        drain()

    return pl.pallas_call(
        body, name="gather_first_layer", in_specs=[ANY] * nb, out_specs=[ANY] * nb,
        out_shape=[_sds((4,) + a.shape, a.dtype) for a in mine],
        scratch_shapes=[pltpu.SemaphoreType.DMA((nb, 3))] * 4)(*mine)


def _swap_halves(bigs, wholes=()):
    nb, n = len(bigs), len(bigs) + len(wholes)

    def body(*refs):
        src, dst = refs[:n], refs[n:2 * n]
        send_sems, recv_sems = refs[2 * n:]
        x, y, c, _, _ = _place()
        sib = (x, y, 1 - c)
        cps = [_remote(src[b].at[:, 1 - c] if b < nb else src[b], dst[b], send_sems.at[b], recv_sems.at[b], sib)
               for b in range(n)]
        for cp in cps:
            cp.start()
        for cp in cps:
            cp.wait()

    return pl.pallas_call(
        body, name="swap_halves", in_specs=[ANY] * n, out_specs=[ANY] * n,
        out_shape=[_sds((4,) + a.shape[2:], a.dtype) for a in bigs] + [_sds(a.shape, a.dtype) for a in wholes],
        scratch_shapes=[pltpu.SemaphoreType.DMA((n,))] * 2)(*bigs, *wholes)


def _sum_tile(r):
    return max(cand for cand in range(16, 641, 16) if r % cand == 0)


def _pair_sum(big, rbig, c):
    _, _, r, w = big.shape
    tr = _sum_tile(r)

    def body(c_ref, big_ref, rbig_ref, p_ref):
        p_ref[...] = (big_ref[...].astype(F32) + rbig_ref[...].astype(F32)).astype(BF16)

    grid_spec = pltpu.PrefetchScalarGridSpec(
        num_scalar_prefetch=1, grid=(4, r // tr),
        in_specs=[pl.BlockSpec((None, None, tr, w), lambda j, i, cr: (j, cr[0], i, 0)),
                  pl.BlockSpec((None, tr, w), lambda j, i, cr: (j, i, 0))],
        out_specs=pl.BlockSpec((None, tr, w), lambda j, i, cr: (j, i, 0)))
    return pl.pallas_call(body, name="pair_sum", grid_spec=grid_spec, out_shape=_sds((4, r, w), BF16),
                          compiler_params=_cp("parallel", "parallel"))(c, big, rbig)


def _small_sum(parts):
    n, ns, _ = parts.shape

    def body(p_ref, o_ref):
        s = p_ref[0]
        for j in range(1, n):
            s = s + p_ref[j]
        o_ref[...] = s

    return pl.pallas_call(body, name="small_sum", out_shape=_sds((ns, 128), F32))(parts)


def _exchange_ops(p_refs, rb_refs, sems, small=None):
    send_sems, recv_sems = sems[0], sems[1]
    nb = len(p_refs)
    x, y, c, k, chips = _place()

    def copies(landing):
        out = []
        for n, (cx, cy) in enumerate(chips):
            to, kj = (cx, cy, c), 2 * cx + cy
            for b in range(nb):
                out.append(_remote(p_refs[b].at[k if landing else kj], rb_refs[b].at[n], send_sems.at[b, n],
                                   recv_sems.at[b, n], to))
            if small:
                out.append(_remote(small[0], small[1].at[kj if landing else k], send_sems.at[nb, n], recv_sems.at[nb, n], to))
        return out

    def local():
        return pltpu.make_async_copy(small[0], small[1].at[k], sems[2])

    def start():
        if small:
            local().start()
        for cp in copies(False):
            cp.start()

    def finish():
        for cp in copies(True):
            cp.wait_recv()
        for cp in copies(False):
            cp.wait_send()
        if small:
            local().wait()

    return start, finish


def _chip_exchange(ps, small):
    nb = len(ps)
    ns = small.shape[0]

    def body(*refs):
        start, finish = _exchange_ops(refs[:nb], refs[nb + 1:2 * nb + 1], refs[2 * nb + 2:], (refs[nb], refs[2 * nb + 1]))
        start()
        finish()

    return pl.pallas_call(
        body, name="chip_exchange", in_specs=[ANY] * (nb + 1), out_specs=[ANY] * (nb + 1),
        out_shape=[_sds((3,) + a.shape[1:], a.dtype) for a in ps] + [_sds((4, ns, 128), small.dtype)],
        scratch_shapes=[pltpu.SemaphoreType.DMA((nb + 1, 3))] * 2 + [pltpu.SemaphoreType.DMA(())])(*ps, small)


def _chip_sum(p, rb, chip):
    _, r, w = p.shape
    tr = _sum_tile(r)

    def body(k_ref, p_ref, rb_ref, o_ref):
        acc = p_ref[...].astype(F32)
        for j in range(3):
            acc = acc + rb_ref[j].astype(F32)
        o_ref[...] = acc

    grid_spec = pltpu.PrefetchScalarGridSpec(
        num_scalar_prefetch=1, grid=(r // tr,),
        in_specs=[pl.BlockSpec((None, tr, w), lambda i, kr: (kr[0], i, 0)), pl.BlockSpec((3, tr, w), lambda i, kr: (0, i, 0))],
        out_specs=pl.BlockSpec((tr, w), lambda i, kr: (i, 0)))
    return pl.pallas_call(body, name="chip_sum", grid_spec=grid_spec, out_shape=_sds((r, w), F32),
                          compiler_params=_cp("parallel"))(chip, p, rb)


def _send_to_sibling(reds):
    nb = len(reds)

    def body(*refs):
        red_refs, out_refs = refs[:nb], refs[nb:2 * nb]
        send_sems, recv_sems = refs[2 * nb:]
        x, y, c, _, _ = _place()
        cps = [_remote(red_refs[b], out_refs[b], send_sems.at[b], recv_sems.at[b], (x, y, 1 - c)) for b in range(nb)]
        for cp in cps:
            cp.start()
        for cp in cps:
            cp.wait()

    return pl.pallas_call(
        body, name="send_to_sibling", in_specs=[ANY] * nb, out_specs=[ANY] * nb,
        out_shape=[_sds(a.shape, a.dtype) for a in reds], scratch_shapes=[pltpu.SemaphoreType.DMA((nb,))] * 2)(*reds)


def _adam_math(w, g, m, v):
    nm = ADAM_B1 * m + (1.0 - ADAM_B1) * g
    nv = ADAM_B2 * v + (1.0 - ADAM_B2) * (g * g)
    m_hat = nm / (1.0 - ADAM_B1 ** ADAM_STEP)
    v_hat = nv / (1.0 - ADAM_B2 ** ADAM_STEP)
    return -ADAM_LR * (m_hat / (jnp.sqrt(v_hat) + ADAM_EPS) + ADAM_WD * w), nm, nv


def _adamw_shard(w, m, v, srcs, c, name, owner=0, split=None):
    depth, r, n = w.shape
    unit = math.gcd(split, r - split) if split else r
    tr = max(cand for cand in range(8, min(unit, 256) + 1, 8) if unit % cand == 0)
    sb = split // tr if split else None
    npad = srcs[0][0].shape[-1]

    def body(c_ref, w_ref, m_ref, v_ref, *refs):
        g_ref, d_ref, nm_ref, nv_ref = refs[2 * depth:]
        l, i = pl.program_id(0), pl.program_id(1)
        half = (i >= sb).astype(jnp.int32) if split else owner
        mine = c_ref[0] == half
        g = jnp.where(mine, refs[0][...], refs[1][...])
        for b in range(1, depth):
            g = jnp.where(l == b, jnp.where(mine, refs[2 * b][...], refs[2 * b + 1][...]), g)
        g = g[:, :n]
        g_ref[...] = g
        d_ref[...], nm_ref[...], nv_ref[...] = _adam_math(w_ref[...], g, m_ref[...], v_ref[...])

    def source(b):
        def index(l, i, cr):
            blk = jnp.where(i >= sb, i - sb, i) if split else i
            return (jnp.where(l == b, blk, 0), 0)
        return pl.BlockSpec((tr, npad), index)

    blk = pl.BlockSpec((None, tr, n), lambda l, i, cr: (l, i, 0))
    grid_spec = pltpu.PrefetchScalarGridSpec(
        num_scalar_prefetch=1, grid=(depth, r // tr),
        in_specs=[blk] * 3 + [source(b) for b in range(depth) for _ in range(2)], out_specs=[blk] * 4)
    return pl.pallas_call(body, name=name, grid_spec=grid_spec, out_shape=[_sds(w.shape, F32)] * 4,
                          compiler_params=_cp("parallel", "parallel"))(c, w, m, v, *[a for pair in srcs for a in pair])


def _pad_ffn_shards(w_gate, w_up, w_down):
    depth = w_gate.shape[0]
    hr = HP // 2

    def gu_body(g_ref, u_ref, o_ref):
        for which, ref in enumerate((g_ref, u_ref)):
            o_ref[which, 0:HS, :] = ref[...].astype(BF16)
            o_ref[which, HS:HP, :] = jnp.zeros((HP - HS, D), BF16)

    blk = pl.BlockSpec((None, HS, D), lambda l: (l, 0, 0))
    gu = pl.pallas_call(
        gu_body, name="pad_gate_up", grid=(depth,), in_specs=[blk, blk],
        out_specs=pl.BlockSpec((None, 2, HP, D), lambda l: (l, 0, 0, 0)),
        out_shape=_sds((depth, 2, HP, D), BF16), compiler_params=_cp("parallel"))(w_gate, w_up)

    def down_body(w_ref, o_ref):
        o_ref[0] = w_ref[0:hr, :].astype(BF16)
        o_ref[1, 0:HS - hr, :] = w_ref[hr:HS, :].astype(BF16)
        o_ref[1, HS - hr:hr, :] = jnp.zeros((HP - HS, D), BF16)

    down = pl.pallas_call(
        down_body, name="pad_down", grid=(depth,), in_specs=[pl.BlockSpec((None, HS, D), lambda l: (l, 0, 0))],
        out_specs=pl.BlockSpec((None, 2, hr, D), lambda l: (l, 0, 0, 0)),
        out_shape=_sds((depth, 2, hr, D), BF16), compiler_params=_cp("parallel"))(w_down)
    return gu, down


def _adamw_rows(w, g, m, v, name):
    r, n = w.shape
    tr = max(cand for cand in range(8, 513, 8) if r % cand == 0)

    def body(w_ref, g_ref, m_ref, v_ref, d_ref, nm_ref, nv_ref):
        d_ref[...], nm_ref[...], nv_ref[...] = _adam_math(w_ref[...], g_ref[...], m_ref[...], v_ref[...])

    blk = pl.BlockSpec((tr, n), lambda i: (i, 0))
    return pl.pallas_call(body, name=name, grid=(r // tr,), in_specs=[blk] * 4, out_specs=[blk] * 3,
                          out_shape=[_sds(w.shape, F32)] * 3, compiler_params=_cp("parallel"))(w, g, m, v)


def _to_pack(a, name):
    depth = a.shape[0]
    if name in ROW_SHARDED:
        return jnp.swapaxes(a.reshape(depth, 4, -1, D), 0, 1)
    return jnp.transpose(a.reshape(depth, a.shape[1], 4, a.shape[2] // 4), (2, 0, 1, 3)).reshape(4, depth, -1, D)


def _pack_rows(parts, lead, dtype, tail=None):
    pieces, at = [], 0
    for n, off, rows in PACK:
        if off > at:
            pieces.append(jnp.zeros(lead + (off - at, D), dtype))
        pieces.append(parts[n].astype(dtype))
        at = off + rows
    if tail is not None:
        pieces.append(tail)
        at += tail.shape[-2]
    pieces.append(jnp.zeros(lead + (PACK_ROWS - at, D), dtype))
    return jnp.concatenate(pieces, axis=len(lead))


def _pack_weight_shards(sh):
    depth = sh["w_in"].shape[0]
    parts = {n: sh[n].reshape(depth, rows, D) for n, _, rows in PACK}
    parts["w_in"] = jnp.swapaxes(sh["w_in"], 1, 2)
    conv = lax.bitcast_convert_type(sh["conv_w"].reshape(depth, 3 * 64), BF16).reshape(depth, 1, 384)
    flat = _pack_rows(parts, (depth,), BF16, tail=jnp.pad(conv, ((0, 0), (0, 0), (0, D - 384))))
    return flat.reshape(depth, 2, PACK_ROWS // 2, D)


def _unpack_weights(gathered, l, shard_shapes):
    depth = 1
    flat = gathered[:, l].reshape(4, 1, PACK_ROWS, D)
    full = {}
    for n, off, rows in PACK:
        if n == "w_in":
            full["w_in_t"] = jnp.swapaxes(flat[:, :, off:off + rows, :], 0, 1).reshape(depth, 4 * rows, D)
            continue
        shp = shard_shapes[n][1:]
        piece = flat[:, :, off:off + rows, :].reshape((4, depth) + shp)
        if n in ROW_SHARDED:
            full[n] = jnp.transpose(piece, (1, 0, 2, 3)).reshape(depth, 4 * shp[0], shp[1])
        else:
            full[n] = jnp.transpose(piece, (1, 2, 0, 3)).reshape(depth, shp[0], 4 * shp[1])
    conv = lax.bitcast_convert_type(flat[:, :, CONV_ROW, :384].reshape(4, depth, 192, 2), F32)
    full["conv_w"] = jnp.transpose(conv.reshape(4, depth, 3, 64), (1, 2, 0, 3)).reshape(depth, 3, CVW)
    return full


def _pack_grad_shards(g):
    depth = g["w_in_t"].shape[0]
    parts = {n: _to_pack(g[n], n) for n, _, _ in PACK if n != "w_in"}
    parts["w_in"] = jnp.swapaxes(g["w_in_t"].reshape(depth, 4, -1, D), 0, 1)
    return _pack_rows(parts, (4, depth), BF16)


def _pack_small(arrs, names_shapes, depth):
    flat = jnp.concatenate([arrs[n].reshape(depth, -1) for n, _ in names_shapes], axis=1).reshape(-1)
    rows = -(-flat.shape[0] // 1024) * 8
    return jnp.pad(flat, (0, rows * 128 - flat.shape[0])).reshape(rows, 128)


def _unpack_small(packed, names_shapes, depth):
    per_layer = sum(math.prod(s) for _, s in names_shapes)
    flat = packed.reshape(-1)[:depth * per_layer].reshape(depth, per_layer)
    out, off = {}, 0
    for n, s in names_shapes:
        size = math.prod(s)
        out[n] = flat[:, off:off + size].reshape((depth,) + s)
        off += size
    return out


def kernel(x, positions, mix_pre_g, mix_post_g, ffn_pre_g, ffn_post_g, w_in, q_norm_g, w_uq, kv_norm_g, w_ukv, sg_ln_g, sg_ln_b, w_sp, b_sp, conv_w, out_norm_g, w_out, w_gate, w_up, w_down, loss_target, m_mix_pre_g, m_mix_post_g, m_ffn_pre_g, m_ffn_post_g, m_w_in, m_q_norm_g, m_w_uq, m_kv_norm_g, m_w_ukv, m_sg_ln_g, m_sg_ln_b, m_w_sp, m_b_sp, m_conv_w, m_out_norm_g, m_w_out, m_w_gate, m_w_up, m_w_down, v_mix_pre_g, v_mix_post_g, v_ffn_pre_g, v_ffn_post_g, v_w_in, v_q_norm_g, v_w_uq, v_kv_norm_g, v_w_ukv, v_sg_ln_g, v_sg_ln_b, v_w_sp, v_b_sp, v_conv_w, v_out_norm_g, v_w_out, v_w_gate, v_w_up, v_w_down):
    w = dict(mix_pre_g=mix_pre_g, mix_post_g=mix_post_g, ffn_pre_g=ffn_pre_g, ffn_post_g=ffn_post_g, w_in=w_in,
             q_norm_g=q_norm_g, w_uq=w_uq, kv_norm_g=kv_norm_g, w_ukv=w_ukv, sg_ln_g=sg_ln_g, sg_ln_b=sg_ln_b, w_sp=w_sp,
             b_sp=b_sp, conv_w=conv_w, out_norm_g=out_norm_g, w_out=w_out, w_gate=w_gate, w_up=w_up, w_down=w_down)
    m = dict(mix_pre_g=m_mix_pre_g, mix_post_g=m_mix_post_g, ffn_pre_g=m_ffn_pre_g, ffn_post_g=m_ffn_post_g, w_in=m_w_in,
             q_norm_g=m_q_norm_g, w_uq=m_w_uq, kv_norm_g=m_kv_norm_g, w_ukv=m_w_ukv, sg_ln_g=m_sg_ln_g, sg_ln_b=m_sg_ln_b,
             w_sp=m_w_sp, b_sp=m_b_sp, conv_w=m_conv_w, out_norm_g=m_out_norm_g, w_out=m_w_out, w_gate=m_w_gate,
             w_up=m_w_up, w_down=m_w_down)
    v = dict(mix_pre_g=v_mix_pre_g, mix_post_g=v_mix_post_g, ffn_pre_g=v_ffn_pre_g, ffn_post_g=v_ffn_post_g, w_in=v_w_in,
             q_norm_g=v_q_norm_g, w_uq=v_w_uq, kv_norm_g=v_kv_norm_g, w_ukv=v_w_ukv, sg_ln_g=v_sg_ln_g, sg_ln_b=v_sg_ln_b,
             w_sp=v_w_sp, b_sp=v_b_sp, conv_w=v_conv_w, out_norm_g=v_out_norm_g, w_out=v_w_out, w_gate=v_w_gate,
             w_up=v_w_up, w_down=v_w_down)
    depth = w_in.shape[0]
    c = lax.axis_index("c").astype(jnp.int32).reshape(1)
    chip = (2 * lax.axis_index("x") + lax.axis_index("y")).astype(jnp.int32)

    mine = [_pack_weight_shards(w), *_pad_ffn_shards(jnp.swapaxes(w_gate, 1, 2), jnp.swapaxes(w_up, 1, 2), w_down)]
    bufs = [lax.dynamic_update_slice(g, a[None], (chip,) + (0,) * a.ndim)
            for g, a in zip(_gather_first_layer(mine), mine)]

    small_grads = [None] * depth
    small_pair = []

    def ffn_front(l, g_gu, g_down):
        bigs = [g_gu.reshape(4, 2, HP, D), g_down.reshape(4, 2, HP // 2, D)]
        return [_pair_sum(a, r, c) for a, r in zip(bigs, _swap_halves(bigs))]

    def mixer_front(l, grads):
        small_grads[l] = grads
        bigs = [_pack_grad_shards(grads).reshape(4, 2, PACK_ROWS // 2, D)]
        if l > 0:
            rbigs = _swap_halves(bigs)
        else:
            small = _pack_small({n: jnp.concatenate([g[n] for g in small_grads]) for n, _ in SMALL}, SMALL, depth)
            *rbigs, rsmall = _swap_halves(bigs, [small])
            small_pair.append(_small_sum(jnp.stack([small, rsmall])))
        return [_pair_sum(a, r, c) for a, r in zip(bigs, rbigs)]

    loss, dx, red = _local_step(x[0], positions[0], loss_target[0], w, mine, bufs,
                                {n: w[n].shape for n, _, _ in PACK}, front=(ffn_front, mixer_front))
    loss = lax.psum(loss, ("x", "y", "c"))

    *red[0]["sent_mixer"], rs = _chip_exchange(red[0]["mixer"], small_pair[0])
    own = [[_chip_sum(p, rb, chip.reshape(1))
            for p, rb in zip(r["mixer"] + r["ffn"], list(r["sent_mixer"]) + list(r["sent_ffn"]))] for r in red]
    other = [_send_to_sibling(o) for o in own]
    g_small = _unpack_small(_small_sum(rs), SMALL, depth)
    g_small["conv_w"] = lax.dynamic_slice_in_dim(g_small["conv_w"], chip * 64, 64, axis=2)

    gw, delta, new_m, new_v = dict(g_small), {}, {}, {}

    def adam(n, srcs, turned=False, **where):
        view = (lambda a: jnp.swapaxes(a, 1, 2)) if turned else (lambda a: a)
        outs = _adamw_shard(view(w[n]), view(m[n]), view(v[n]), srcs, c, "adamw_" + n, **where)
        gw[n], delta[n], new_m[n], new_v[n] = [view(o) for o in outs]

    first = c[0] == 0
    packs = [jnp.concatenate([jnp.where(first, o[0], s[0]), jnp.where(first, s[0], o[0])]) for o, s in zip(own, other)]
    for n, off, rows in PACK[1:]:
        pieces = [pk[off:off + rows, :].reshape(w[n].shape[1:]) for pk in packs]
        adam(n, [(pc, pc) for pc in pieces], owner=0)
    rows_in = PACK[0][2]
    turn = lambda a: jnp.swapaxes(a, 1, 2).reshape(depth * rows_in, D)
    back = lambda a: jnp.swapaxes(a.reshape(depth, rows_in, D), 1, 2)
    g_in = jnp.concatenate([pk[:rows_in, :] for pk in packs])
    outs = _adamw_rows(turn(w["w_in"]), g_in, turn(m["w_in"]), turn(v["w_in"]), "adamw_w_in")
    gw["w_in"], delta["w_in"], new_m["w_in"], new_v["w_in"] = [back(a) for a in (g_in, *outs)]
    adam("w_gate", [(o[1], s[1]) for o, s in zip(own, other)], turned=True, owner=0)
    adam("w_up", [(o[1], s[1]) for o, s in zip(own, other)], turned=True, owner=1)
    adam("w_down", [(o[2], s[2]) for o, s in zip(own, other)], split=HP // 2)
    small_local = tuple((n, w[n].shape[1:]) for n, _ in SMALL)
    d_, m_, v_ = _adamw_rows(_pack_small(w, small_local, depth), _pack_small(gw, small_local, depth),
                             _pack_small(m, small_local, depth), _pack_small(v, small_local, depth), "adamw_small")
    delta.update(_unpack_small(d_, small_local, depth))
    new_m.update(_unpack_small(m_, small_local, depth))
    new_v.update(_unpack_small(v_, small_local, depth))

    return (loss, dx[None], *[gw[n] for n in WEIGHTS], *[delta[n] for n in WEIGHTS], *[new_m[n] for n in WEIGHTS],
            *[new_v[n] for n in WEIGHTS])
```

```python
import math

import jax
import jax.numpy as jnp
from jax import lax
from jax.experimental import pallas as pl
from jax.experimental.pallas import tpu as pltpu

F32 = jnp.float32
BF16 = jnp.bfloat16

D = 1024
HEADS = 8
NOPE = 64
ROPE = 32
VD = 64
QR = 384
KVR = 256
SGW = 256
CVW = 256
CHUNK = 128
DFF = 2816
EPS = 1e-6
ROPE_THETA = 10000.0
LOG2E = 1.4426950408889634
LN2 = 0.6931471805599453
QSCALE = (NOPE + ROPE) ** -0.5 * LOG2E
ZA = 768
ZB = 1280
QW = HEADS * 128
KVW = HEADS * 128 + HEADS * VD
NEG = -1e30
GC0 = 0.7978845608028654
GC1 = 0.044715

ADAM_LR = 0.001
ADAM_B1 = 0.9
ADAM_B2 = 0.999
ADAM_EPS = 1e-08
ADAM_WD = 0.01
ADAM_STEP = 10

V7X_VMEM_LIMIT = 52 * 1024 * 1024
ROW_TILE = 512
ATT_TILE = 512
ATT_HEADS = 4

NT = (((1,), (1,)), ((), ()))
TN = (((0,), (0,)), ((), ()))

HS = DFF // 4
HP = 768
DFFP = 4 * HP

PACK = (("w_in", 0, 488), ("w_out", 512, 256), ("w_ukv", 768, 64), ("w_uq", 832, 72))
CONV_ROW = 904
PACK_ROWS = 928
ROW_SHARDED = ("w_out",)
SMALL = (("mix_pre_g", (D,)), ("mix_post_g", (D,)), ("ffn_pre_g", (D,)), ("ffn_post_g", (D,)), ("q_norm_g", (QR,)),
         ("kv_norm_g", (KVR,)), ("sg_ln_g", (SGW,)), ("sg_ln_b", (SGW,)), ("w_sp", (4, CHUNK, CHUNK)), ("b_sp", (4, CHUNK)),
         ("conv_w", (3, CVW)), ("out_norm_g", (D,)))
WEIGHTS = ["mix_pre_g", "mix_post_g", "ffn_pre_g", "ffn_post_g", "w_in", "q_norm_g", "w_uq", "kv_norm_g", "w_ukv", "sg_ln_g",
           "sg_ln_b", "w_sp", "b_sp", "conv_w", "out_norm_g", "w_out", "w_gate", "w_up", "w_down"]

MESH_ID = pl.DeviceIdType.MESH
ANY = pl.BlockSpec(memory_space=pl.ANY)


def _cp(*sem):
    return pltpu.CompilerParams(dimension_semantics=sem, vmem_limit_bytes=V7X_VMEM_LIMIT)


def _sds(shape, dtype):
    return jax.ShapeDtypeStruct(shape, dtype)


def _row(tm, n):
    return pl.BlockSpec((tm, n), lambda i: (i, 0))


def _lyr(l, *shape):
    return pl.BlockSpec((None,) + shape, lambda *_: (l,) + (0,) * len(shape))


def _wl(a, l):
    return 0 if a.shape[0] == 1 else l


def _pcall(body, name, grid, ins, in_specs, out_specs, out_shape, sem, scratch=(), prevs=None):
    prevs = {k: v for k, v in (prevs or {}).items() if v is not None}
    order = sorted(prevs)
    n_in = len(ins)

    def wrapped(*refs):
        return body(*refs[:n_in], *refs[n_in + len(order):])

    return pl.pallas_call(
        wrapped, name=name, grid=grid, in_specs=list(in_specs) + [ANY] * len(order), out_specs=out_specs,
        out_shape=out_shape, scratch_shapes=list(scratch),
        input_output_aliases={n_in + i: k for i, k in enumerate(order)},
        compiler_params=_cp(*sem))(*ins, *[prevs[k] for k in order])


def _rms(x, g):
    r = lax.rsqrt(jnp.mean(x * x, axis=-1, keepdims=True) + EPS)
    return x * r * g


def _rms_bwd(x, g, dy):
    r = lax.rsqrt(jnp.mean(x * x, axis=-1, keepdims=True) + EPS)
    xh = x * r
    dg = jnp.sum(dy * xh, axis=0, keepdims=True)
    dxh = dy * g
    dx = r * (dxh - xh * jnp.mean(dxh * xh, axis=-1, keepdims=True))
    return dx, dg


def _sigmoid(x):
    return 0.5 * jnp.tanh(0.5 * x) + 0.5


def _gelu(x):
    return 0.5 * x * (1.0 + jnp.tanh(GC0 * (x + GC1 * x * x * x)))


def _gelu_grad(x):
    t = jnp.tanh(GC0 * (x + GC1 * x * x * x))
    return 0.5 * (1.0 + t) + 0.5 * x * (1.0 - t * t) * GC0 * (1.0 + 3.0 * GC1 * x * x)


def _rope(xb, c, s1, s2):
    return xb * c + pltpu.roll(xb, 112, 1) * s1 + pltpu.roll(xb, 16, 1) * s2


def _rope_bwd(dy, c, s1, s2):
    return dy * c + pltpu.roll(dy * s1, 16, 1) + pltpu.roll(dy * s2, 112, 1)


def _group_masks(shape):
    lane = lax.broadcasted_iota(jnp.int32, shape, 1)
    return [(lane >= 64 * g) & (lane < 64 * g + 64) for g in range(shape[1] // 64)]


def _group_mean(v, masks):
    out = jnp.zeros_like(v)
    for m in masks:
        s = jnp.sum(jnp.where(m, v, 0.0), axis=-1, keepdims=True) * (1.0 / 64.0)
        out = jnp.where(m, s, out)
    return out


def _pick_row(blk, idx):
    row = lax.broadcasted_iota(jnp.int32, blk.shape, 0)
    return jnp.sum(jnp.where(row == idx, blk, 0.0), axis=0, keepdims=True)


def _shift_down(y, k, first_rows):
    out = pltpu.roll(y, k, 0)
    row = lax.broadcasted_iota(jnp.int32, y.shape, 0)
    for idx in range(k):
        out = jnp.where(row == idx, first_rows[idx], out)
    return out


def _shift_up(y, k, last_rows):
    n = y.shape[0]
    out = pltpu.roll(y, n - k, 0)
    row = lax.broadcasted_iota(jnp.int32, y.shape, 0)
    for idx in range(k):
        out = jnp.where(row == n - k + idx, last_rows[idx], out)
    return out


def _tril_mask():
    r = lax.broadcasted_iota(jnp.int32, (CHUNK, CHUNK), 0)
    c = lax.broadcasted_iota(jnp.int32, (CHUNK, CHUNK), 1)
    return r >= c


def _sgu_forward(zu, zv, g_ln, b_ln, wc_bf, bsp, masks, cmasks):
    u = _gelu(zu)
    vv = _gelu(zv)
    mu = _group_mean(vv, masks)
    dv = vv - mu
    rs = lax.rsqrt(_group_mean(dv * dv, masks) + EPS)
    xh = dv * rs
    vn = xh * g_ln + b_ln
    chunks = []
    for ci in range(zu.shape[0] // CHUNK):
        vc = vn[ci * CHUNK:(ci + 1) * CHUNK, :]
        acc = bsp
        for g in range(4):
            acc = acc + jnp.dot(wc_bf[g], jnp.where(cmasks[g], vc, 0.0).astype(BF16), preferred_element_type=F32)
        chunks.append(acc)
    mixed = jnp.concatenate(chunks, axis=0) if len(chunks) > 1 else chunks[0]
    return u, vv, xh, rs, vn, mixed


def _conv_forward(gc, hh, prev_gc, prev_hh, first_tile, cw):
    yv = gc * hh
    prev = jnp.where(first_tile, 0.0, prev_gc * prev_hh)
    p6, p7 = _pick_row(prev, 6), _pick_row(prev, 7)
    sh1 = _shift_down(yv, 1, [p7])
    sh2 = _shift_down(yv, 2, [p6, p7])
    conv = sh2 * cw[0:1, :] + sh1 * cw[1:2, :] + yv * cw[2:3, :]
    return yv, sh1, sh2, conv


def _acc_init(step, *refs):
    @pl.when(step == 0)
    def _():
        for r in refs:
            r[...] = jnp.zeros(r.shape, r.dtype)


def _in_proj(x, p, l):
    t = x.shape[0]
    tm = min(ROW_TILE, t)

    def body(x_ref, g_ref, wa_ref, wb_ref, h_ref, za_ref, zb_ref):
        h = _rms(x_ref[...], g_ref[...]).astype(BF16)
        h_ref[...] = h
        za_ref[...] = lax.dot_general(h, wa_ref[...], NT, preferred_element_type=F32)
        zb_ref[...] = lax.dot_general(h, wb_ref[...], NT, preferred_element_type=F32)

    return _pcall(
        body, "in_proj", (t // tm,), [x, p["mix_pre_g"], p["w_in_a"], p["w_in_b"]],
        [_row(tm, D), _lyr(l, 1, D), _lyr(_wl(p["w_in_a"], l), ZA, D), _lyr(_wl(p["w_in_b"], l), ZB, D)],
        [_row(tm, D), _row(tm, ZA), _row(tm, ZB)],
        [_sds((t, D), BF16), _sds((t, ZA), F32), _sds((t, ZB), F32)], ("parallel",))


def _mla_prep(za, p, l, tabs):
    t = za.shape[0]
    tm = min(ROW_TILE, t)

    def body(z_ref, gq_ref, gkv_ref, wuq_ref, wukv_ref, c_ref, s1_ref, s2_ref, cq_ref, ckv_ref, q_ref, k_ref, v_ref):
        z = z_ref[...]
        cq = _rms(z[:, :QR], gq_ref[...]).astype(BF16)
        ckv = _rms(z[:, QR:QR + KVR], gkv_ref[...]).astype(BF16)
        cq_ref[...] = cq
        ckv_ref[...] = ckv
        c, s1, s2 = c_ref[...], s1_ref[...], s2_ref[...]
        kr = _rope(z[:, QR + KVR:], c, s1, s2)
        q = jnp.dot(cq, wuq_ref[...], preferred_element_type=F32)
        kv = jnp.dot(ckv, wukv_ref[...], preferred_element_type=F32)
        for h in range(HEADS):
            sl = slice(128 * h, 128 * h + 128)
            q_ref[:, sl] = (_rope(q[:, sl], c, s1, s2) * QSCALE).astype(BF16)
            k_ref[:, sl] = (kv[:, sl] + kr).astype(BF16)
        v_ref[...] = kv[:, QW:].astype(BF16)

    return _pcall(
        body, "mla_prep", (t // tm,), [za, p["q_norm_g"], p["kv_norm_g"], p["w_uq"], p["w_ukv"], *tabs],
        [_row(tm, ZA), _lyr(l, 1, QR), _lyr(l, 1, KVR), _lyr(_wl(p["w_uq"], l), QR, QW), _lyr(_wl(p["w_ukv"], l), KVR, KVW),
         _row(tm, 128), _row(tm, 128), _row(tm, 128)],
        [_row(tm, QR), _row(tm, KVR), _row(tm, QW), _row(tm, QW), _row(tm, HEADS * VD)],
        [_sds((t, QR), BF16), _sds((t, KVR), BF16), _sds((t, QW), BF16), _sds((t, QW), BF16),
         _sds((t, HEADS * VD), BF16)], ("parallel",))


def _att_tile(t):
    return min(ATT_TILE, max(t // 2, 128))


def _causal_keep(tq, i, j):
    row = lax.broadcasted_iota(jnp.int32, (tq, tq), 0) + i * tq
    col = lax.broadcasted_iota(jnp.int32, (tq, tq), 1) + j * tq
    return col <= row


def _attn_fwd(qs, k, v, fetch=None):
    t = qs.shape[0]
    tq = _att_tile(t)
    nq = t // tq
    rep = tq // 128
    groups = HEADS // ATT_HEADS
    mine, bufs, fetch_layer = fetch if fetch else ((), (), None)
    nb = len(mine)

    steps = [(i, j) for i in range(nq) for j in range(i + 1)]
    i_of = jnp.asarray([s[0] for s in steps], jnp.int32)
    j_of = jnp.asarray([s[1] for s in steps], jnp.int32)

    def body(i_ref, j_ref, q_ref, k_ref, v_ref, *refs):
        o_ref, lse_ref = refs[2 * nb:2 * nb + 2]
        m_s, l_s, acc_s = refs[3 * nb + 2:3 * nb + 5]
        step_no = pl.program_id(1)
        i, j = i_ref[step_no], j_ref[step_no]
        if fetch:
            start, hand_over, drain = _gather_ops(refs[:nb], refs[2 * nb + 2:3 * nb + 2], refs[3 * nb + 5:], fetch_layer)
            pr = pl.program_id(0)
            pl.when((pr == 0) & (step_no == 0))(start)
            pl.when((pr == groups - 1) & (step_no == 3 * len(steps) // 4))(hand_over)
            pl.when((pr == groups - 1) & (step_no == len(steps) - 1))(drain)

        @pl.when(j == 0)
        def _():
            m_s[...] = jnp.full(m_s.shape, NEG, F32)
            l_s[...] = jnp.zeros(l_s.shape, F32)
            acc_s[...] = jnp.zeros(acc_s.shape, F32)

        def step(masked):
            keep = _causal_keep(tq, i, j) if masked else None
            for hh in range(ATT_HEADS):
                sl = slice(128 * hh, 128 * hh + 128)
                vv = v_ref[:, 128 * (hh // 2):128 * (hh // 2) + 128]
                s = lax.dot_general(q_ref[:, sl], k_ref[:, sl], NT, preferred_element_type=F32)
                if masked:
                    s = jnp.where(keep, s, NEG)
                m_old = m_s[hh]
                m_new = jnp.maximum(m_old, jnp.max(s, axis=-1, keepdims=True))
                alpha = jnp.exp2(m_old - m_new)
                p = jnp.exp2(s - jnp.tile(m_new, (1, rep)))
                l_s[hh] = alpha * l_s[hh] + jnp.sum(p, axis=-1, keepdims=True)
                acc_s[hh] = alpha * acc_s[hh] + jnp.dot(p.astype(BF16), vv, preferred_element_type=F32)
                m_s[hh] = m_new

        @pl.when(j < i)
        def _():
            step(False)

        @pl.when(j == i)
        def _():
            step(True)
            lane = lax.broadcasted_iota(jnp.int32, (tq, 128), 1)
            for pp in range(ATT_HEADS // 2):
                a, b = 2 * pp, 2 * pp + 1
                o_ref[:, 128 * pp:128 * pp + 128] = jnp.where(lane < VD, acc_s[a] / l_s[a], acc_s[b] / l_s[b])
            for hh in range(ATT_HEADS):
                lse_ref[hh] = (m_s[hh] + jnp.log2(l_s[hh]))[:, 0:1]

    qw, vw = 128 * ATT_HEADS, VD * ATT_HEADS
    stat = pltpu.VMEM((ATT_HEADS, tq, 128), F32)
    grid_spec = pltpu.PrefetchScalarGridSpec(
        num_scalar_prefetch=2, grid=(groups, len(steps)),
        in_specs=[pl.BlockSpec((tq, qw), lambda p, s, it, jt: (it[s], p)),
                  pl.BlockSpec((tq, qw), lambda p, s, it, jt: (jt[s], p)),
                  pl.BlockSpec((tq, vw), lambda p, s, it, jt: (jt[s], p))] + [ANY] * (2 * nb),
        out_specs=[pl.BlockSpec((tq, vw), lambda p, s, it, jt: (it[s], p)),
                   pl.BlockSpec((ATT_HEADS, tq, 1), lambda p, s, it, jt: (p, it[s], 0))] + [ANY] * nb,
        scratch_shapes=[stat, stat, stat] + ([pltpu.SemaphoreType.DMA((nb, 3))] * 4 if fetch else []))
    outs = pl.pallas_call(
        body, name="attn_fwd_fetch" if fetch else "attn_fwd", grid_spec=grid_spec,
        out_shape=[_sds((t, HEADS * VD), F32), _sds((HEADS, t, 1), F32)] + [_sds(b.shape, b.dtype) for b in bufs],
        input_output_aliases={5 + nb + b: 2 + b for b in range(nb)},
        compiler_params=_cp("arbitrary", "arbitrary"))(i_of, j_of, qs, k, v, *mine, *bufs)
    return outs[0], outs[1], list(outs[2:])


def _mixer_fwd(zb, ya, p, l):
    t = zb.shape[0]
    tm = min(ROW_TILE, t)
    hb = tm // 8

    def body(zb_ref, zprev_ref, ya_ref, gln_ref, bln_ref, wsp_ref, bsp_ref, cw_ref, go_ref, mix_ref, yb_ref, yc_ref):
        i = pl.program_id(0)
        masks = _group_masks((tm, SGW))
        cmasks = _group_masks((CHUNK, SGW))
        tril = _tril_mask()
        wc_bf = [jnp.where(tril, wsp_ref[g], 0.0).astype(BF16) for g in range(4)]
        u, _, _, _, _, mixed = _sgu_forward(zb_ref[:, 0:256], zb_ref[:, 256:512], gln_ref[...], bln_ref[...], wc_bf,
                                            bsp_ref[...], masks, cmasks)
        yb = u * mixed
        _, _, _, conv = _conv_forward(zb_ref[:, 768:1024], zb_ref[:, 1024:1280], zprev_ref[:, 768:1024],
                                      zprev_ref[:, 1024:1280], i == 0, cw_ref[...])
        yc = zb_ref[:, 512:768] * conv
        yb_ref[...] = yb
        yc_ref[...] = yc
        go = go_ref[...]
        mix_ref[:, 0:512] = _rms(ya_ref[...], go[:, 0:512]).astype(BF16)
        mix_ref[:, 512:768] = _rms(yb, go[:, 512:768]).astype(BF16)
        mix_ref[:, 768:1024] = _rms(yc, go[:, 768:1024]).astype(BF16)

    return _pcall(
        body, "mixer_fwd", (t // tm,),
        [zb, zb, ya, p["sg_ln_g"], p["sg_ln_b"], p["w_sp"], p["b_sp"], p["conv_w"], p["out_norm_g"]],
        [_row(tm, ZB), pl.BlockSpec((8, ZB), lambda i: (jnp.maximum(i * hb - 1, 0), 0)), _row(tm, 512),
         _lyr(l, 1, SGW), _lyr(l, 1, SGW), _lyr(l, 4, CHUNK, CHUNK), _lyr(l, CHUNK, SGW), _lyr(_wl(p["conv_w"], l), 3, CVW), _lyr(l, 1, D)],
        [_row(tm, D), _row(tm, SGW), _row(tm, CVW)],
        [_sds((t, D), BF16), _sds((t, SGW), F32), _sds((t, CVW), F32)], ("parallel",))


def _out_proj(mix, x, p, l):
    t = x.shape[0]
    tm = min(ROW_TILE, t)

    def body(mix_ref, w_ref, x_ref, gp_ref, gf_ref, o_ref, x2_ref, h2_ref):
        o = jnp.dot(mix_ref[...], w_ref[...], preferred_element_type=F32)
        o_ref[...] = o
        x2 = x_ref[...] + _rms(o, gp_ref[...])
        x2_ref[...] = x2
        h2_ref[...] = _rms(x2, gf_ref[...]).astype(BF16)

    return _pcall(
        body, "out_proj", (t // tm,), [mix, p["w_out"], x, p["mix_post_g"], p["ffn_pre_g"]],
        [_row(tm, D), _lyr(_wl(p["w_out"], l), D, D), _row(tm, D), _lyr(l, 1, D), _lyr(l, 1, D)],
        [_row(tm, D), _row(tm, D), _row(tm, D)],
        [_sds((t, D), F32), _sds((t, D), F32), _sds((t, D), BF16)], ("parallel",))


def _gu_all(l, which):
    return pl.BlockSpec((4, None, None, HP, D), lambda *_: (0, l, which, 0, 0))


def _down_all(l):
    return pl.BlockSpec((4, None, HP, D), lambda *_: (0, l, 0, 0))


def _ffn_up(h2, p, l):
    t = h2.shape[0]
    tm = min(ROW_TILE, t)

    def body(h_ref, wg_ref, wu_ref, a_ref, b_ref, s_ref):
        h = h_ref[...]
        a = lax.dot_general(h, wg_ref[...], NT, preferred_element_type=F32)
        b = lax.dot_general(h, wu_ref[...], NT, preferred_element_type=F32)
        a_ref[...] = a.astype(BF16)
        b_ref[...] = b.astype(BF16)
        s_ref[...] = (a * _sigmoid(a) * b).astype(BF16)

    blk = pl.BlockSpec((tm, HP), lambda k, i: (i, k))
    wblk = lambda which: pl.BlockSpec((None, None, None, HP, D), lambda k, i: (k, l, which, 0, 0))
    return _pcall(
        body, "ffn_up", (4, t // tm), [h2, p["w_gu"], p["w_gu"]],
        [pl.BlockSpec((tm, D), lambda k, i: (i, 0)), wblk(0), wblk(1)], [blk, blk, blk],
        [_sds((t, DFFP), BF16)] * 3, ("parallel", "parallel"))


def _ffn_down(s, x2, p, l):
    t = x2.shape[0]
    tm = min(ROW_TILE, t)

    def body(s_ref, w_ref, x_ref, g_ref, f_ref, x3_ref):
        f = jnp.dot(s_ref[:, 0:HP], w_ref[0], preferred_element_type=F32)
        for k in range(1, 4):
            f = f + jnp.dot(s_ref[:, k * HP:(k + 1) * HP], w_ref[k], preferred_element_type=F32)
        f_ref[...] = f
        x3_ref[...] = x_ref[...] + _rms(f, g_ref[...])

    return _pcall(
        body, "ffn_down", (t // tm,), [s, p["w_down"], x2, p["ffn_post_g"]],
        [_row(tm, DFFP), _down_all(l), _row(tm, D), _lyr(l, 1, D)], [_row(tm, D), _row(tm, D)],
        [_sds((t, D), F32), _sds((t, D), F32)], ("parallel",))


def _loss_head(y, target):
    t = y.shape[0]
    tm = min(ROW_TILE, t)

    def body(y_ref, t_ref, dy_ref, acc_ref):
        e = y_ref[...] - t_ref[...]
        dy_ref[...] = e * (1.0 / D)
        sq = jnp.sum(e * e, axis=0, keepdims=True)
        part = sq[:, 0:128]
        for b in range(1, D // 128):
            part = part + sq[:, 128 * b:128 * b + 128]
        _acc_init(pl.program_id(0), acc_ref)
        acc_ref[...] += part

    return _pcall(body, "loss_head", (t // tm,), [y, target], [_row(tm, D), _row(tm, D)],
                  [_row(tm, D), pl.BlockSpec((1, 128), lambda i: (0, 0))],
                  [_sds((t, D), F32), _sds((1, 128), F32)], ("arbitrary",))


def _ffn_down_bwd(dx3, sv, p, l, depth, gb):
    t = dx3.shape[0]
    tm = min(256, t)

    def body(dx_ref, f_ref, g_ref, w_ref, a_ref, b_ref, df_ref, da_ref, db_ref, dg_ref):
        _acc_init(pl.program_id(0), dg_ref)
        df, dg = _rms_bwd(f_ref[...], g_ref[...], dx_ref[...])
        dg_ref[...] += dg
        df = df.astype(BF16)
        df_ref[...] = df
        for k in range(4):
            sl = slice(k * HP, (k + 1) * HP)
            ds = lax.dot_general(df, w_ref[k], NT, preferred_element_type=F32)
            av = a_ref[:, sl].astype(F32)
            sig = _sigmoid(av)
            da_ref[:, sl] = (ds * b_ref[:, sl].astype(F32) * (sig * (1.0 + av * (1.0 - sig)))).astype(BF16)
            db_ref[:, sl] = (ds * (av * sig)).astype(BF16)

    df, da, db, gb["ffn_post_g"] = _pcall(
        body, "ffn_down_bwd", (t // tm,), [dx3, sv["f"], p["ffn_post_g"], p["w_down"], sv["a"], sv["b"]],
        [_row(tm, D), _row(tm, D), _lyr(l, 1, D), _down_all(l), _row(tm, DFFP), _row(tm, DFFP)],
        [_row(tm, D), _row(tm, DFFP), _row(tm, DFFP), _lyr(l.g, 1, D)],
        [_sds((t, D), BF16), _sds((t, DFFP), BF16), _sds((t, DFFP), BF16), _sds((depth, 1, D), F32)], ("arbitrary",),
        prevs={3: gb.get("ffn_post_g")})
    return df, da, db


def _ffn_up_bwd(da, db, dx3, sv, p, l, depth, gb):
    t = dx3.shape[0]
    tm = min(256, t)

    def body(da_ref, db_ref, wg_ref, wu_ref, x_ref, dx3_ref, g_ref, dx2_ref, dg_ref):
        _acc_init(pl.program_id(0), dg_ref)
        dh = jnp.zeros((tm, D), F32)
        for k in range(4):
            sl = slice(k * HP, (k + 1) * HP)
            dh = dh + jnp.dot(da_ref[:, sl], wg_ref[k], preferred_element_type=F32)
            dh = dh + jnp.dot(db_ref[:, sl], wu_ref[k], preferred_element_type=F32)
        dx, dg = _rms_bwd(x_ref[...], g_ref[...], dh)
        dg_ref[...] += dg
        dx2_ref[...] = dx3_ref[...] + dx

    dx2, gb["ffn_pre_g"] = _pcall(
        body, "ffn_up_bwd", (t // tm,), [da, db, p["w_gu"], p["w_gu"], sv["x2"], dx3, p["ffn_pre_g"]],
        [_row(tm, DFFP), _row(tm, DFFP), _gu_all(l, 0), _gu_all(l, 1), _row(tm, D), _row(tm, D), _lyr(l, 1, D)],
        [_row(tm, D), _lyr(l.g, 1, D)], [_sds((t, D), F32), _sds((depth, 1, D), F32)], ("arbitrary",),
        prevs={1: gb.get("ffn_pre_g")})
    return dx2


def _out_proj_bwd(dx2, sv, p, l, depth, gb):
    t = dx2.shape[0]
    tm = min(ROW_TILE, t)

    def body(dx_ref, o_ref, g_ref, w_ref, do_ref, dmix_ref, dg_ref):
        _acc_init(pl.program_id(0), dg_ref)
        do, dg = _rms_bwd(o_ref[...], g_ref[...], dx_ref[...])
        dg_ref[...] += dg
        do = do.astype(BF16)
        do_ref[...] = do
        dmix_ref[...] = lax.dot_general(do, w_ref[...], NT, preferred_element_type=F32)

    do, dmix, gb["mix_post_g"] = _pcall(
        body, "out_proj_bwd", (t // tm,), [dx2, sv["o"], p["mix_post_g"], p["w_out"]],
        [_row(tm, D), _row(tm, D), _lyr(l, 1, D), _lyr(_wl(p["w_out"], l), D, D)], [_row(tm, D), _row(tm, D), _lyr(l.g, 1, D)],
        [_sds((t, D), BF16), _sds((t, D), F32), _sds((depth, 1, D), F32)], ("arbitrary",),
        prevs={2: gb.get("mix_post_g")})
    return do, dmix


def _mixer_bwd(dmix, sv, p, l, depth, gb):
    zb = sv["zb"]
    t = zb.shape[0]
    tm = min(ROW_TILE, t)
    hb = tm // 8
    last_blk = t // 8 - 1
    nsteps = t // tm

    def body(dmix_ref, ya_ref, yb_ref, yc_ref, zb_ref, zprev_ref, znext_ref, ycn_ref, dmn_ref,
             gln_ref, bln_ref, wsp_ref, bsp_ref, cw_ref, go_ref,
             dya_ref, dzb_ref, delta_ref, dgo_ref, dgln_ref, dbln_ref, dwsp_ref, dbsp_ref, dcw_ref):
        i = pl.program_id(0)
        _acc_init(i, dgo_ref, dgln_ref, dbln_ref, dwsp_ref, dbsp_ref, dcw_ref)
        go = go_ref[...]
        dmix = dmix_ref[...]

        ya = ya_ref[...]
        dya, dga = _rms_bwd(ya, go[:, 0:512], dmix[:, 0:512])
        dyb, dgb_ = _rms_bwd(yb_ref[...], go[:, 512:768], dmix[:, 512:768])
        dyc, dgc_ = _rms_bwd(yc_ref[...], go[:, 768:1024], dmix[:, 768:1024])
        dgo_ref[:, 0:512] += dga
        dgo_ref[:, 512:768] += dgb_
        dgo_ref[:, 768:1024] += dgc_
        dya = dya * LN2
        dya_ref[...] = dya.astype(BF16)
        prod = dya * ya
        hmasks = _group_masks((tm, 512))
        for h in range(HEADS):
            delta_ref[h] = jnp.sum(jnp.where(hmasks[h], prod, 0.0), axis=-1, keepdims=True)

        masks = _group_masks((tm, SGW))
        cmasks = _group_masks((CHUNK, SGW))
        tril = _tril_mask()
        wc_bf = [jnp.where(tril, wsp_ref[g], 0.0).astype(BF16) for g in range(4)]
        zu, zv = zb_ref[:, 0:256], zb_ref[:, 256:512]
        g_ln = gln_ref[...]
        u, _, xh, rs, vn, mixed = _sgu_forward(zu, zv, g_ln, bln_ref[...], wc_bf, bsp_ref[...], masks, cmasks)
        du = dyb * mixed
        dmixed = dyb * u
        dvn_chunks = []
        dbsp = jnp.zeros((CHUNK, SGW), F32)
        for ci in range(tm // CHUNK):
            rows = slice(ci * CHUNK, (ci + 1) * CHUNK)
            dm_c = dmixed[rows, :]
            vn_c = vn[rows, :].astype(BF16)
            dbsp = dbsp + dm_c
            dvn_c = jnp.zeros((CHUNK, SGW), F32)
            for g in range(4):
                dm_g = jnp.where(cmasks[g], dm_c, 0.0).astype(BF16)
                dw = lax.dot_general(dm_g, vn_c, NT, preferred_element_type=F32)
                dwsp_ref[g] += jnp.where(tril, dw, 0.0)
                dvn_c = dvn_c + lax.dot_general(wc_bf[g], dm_g, TN, preferred_element_type=F32)
            dvn_chunks.append(dvn_c)
        dbsp_ref[...] += dbsp
        dvn = jnp.concatenate(dvn_chunks, axis=0) if len(dvn_chunks) > 1 else dvn_chunks[0]
        dgln_ref[...] += jnp.sum(dvn * xh, axis=0, keepdims=True)
        dbln_ref[...] += jnp.sum(dvn, axis=0, keepdims=True)
        dxh = dvn * g_ln
        dvv = rs * (dxh - _group_mean(dxh, masks) - xh * _group_mean(dxh * xh, masks))
        dzb_ref[:, 0:256] = (du * _gelu_grad(zu)).astype(BF16)
        dzb_ref[:, 256:512] = (dvv * _gelu_grad(zv)).astype(BF16)

        cwv = cw_ref[...]
        gb_, gc, hh = zb_ref[:, 512:768], zb_ref[:, 768:1024], zb_ref[:, 1024:1280]
        yv, sh1, sh2, conv = _conv_forward(gc, hh, zprev_ref[:, 768:1024], zprev_ref[:, 1024:1280], i == 0, cwv)
        dconv = dyc * gb_
        dzb_ref[:, 512:768] = (dyc * conv).astype(BF16)
        dcw_ref[0:1, :] += jnp.sum(dconv * sh2, axis=0, keepdims=True)
        dcw_ref[1:2, :] += jnp.sum(dconv * sh1, axis=0, keepdims=True)
        dcw_ref[2:3, :] += jnp.sum(dconv * yv, axis=0, keepdims=True)
        dycn, _ = _rms_bwd(ycn_ref[...], go[:, 768:1024], dmn_ref[...])
        dconv_next = jnp.where(i == nsteps - 1, 0.0, dycn * znext_ref[:, 512:768])
        n0, n1 = _pick_row(dconv_next, 0), _pick_row(dconv_next, 1)
        dyv = dconv * cwv[2:3, :] + _shift_up(dconv, 1, [n0]) * cwv[1:2, :] + _shift_up(dconv, 2, [n0, n1]) * cwv[0:1, :]
        dzb_ref[:, 768:1024] = (dyv * hh).astype(BF16)
        dzb_ref[:, 1024:1280] = (dyv * gc).astype(BF16)

    prev_map = lambda i: (jnp.maximum(i * hb - 1, 0), 0)
    next_map = lambda i: (jnp.minimum((i + 1) * hb, last_blk), 0)
    names = ("out_norm_g", "sg_ln_g", "sg_ln_b", "w_sp", "b_sp_t", "conv_w")
    shapes = ((1, D), (1, SGW), (1, SGW), (4, CHUNK, CHUNK), (CHUNK, SGW), (3, CVW))
    outs = _pcall(
        body, "mixer_bwd", (nsteps,),
        [dmix, sv["ya"], sv["yb"], sv["yc"], zb, zb, zb, sv["yc"], dmix, p["sg_ln_g"], p["sg_ln_b"], p["w_sp"], p["b_sp"],
         p["conv_w"], p["out_norm_g"]],
        [_row(tm, D), _row(tm, 512), _row(tm, SGW), _row(tm, CVW), _row(tm, ZB),
         pl.BlockSpec((8, ZB), prev_map), pl.BlockSpec((8, ZB), next_map), pl.BlockSpec((8, CVW), next_map),
         pl.BlockSpec((8, 256), lambda i: (jnp.minimum((i + 1) * hb, last_blk), 3)),
         _lyr(l, 1, SGW), _lyr(l, 1, SGW), _lyr(l, 4, CHUNK, CHUNK), _lyr(l, CHUNK, SGW), _lyr(_wl(p["conv_w"], l), 3, CVW), _lyr(l, 1, D)],
        [_row(tm, 512), _row(tm, ZB), pl.BlockSpec((HEADS, tm, 1), lambda i: (0, i, 0))] + [_lyr(l.g, *s) for s in shapes],
        [_sds((t, 512), BF16), _sds((t, ZB), BF16), _sds((HEADS, t, 1), F32)] + [_sds((depth,) + s, F32) for s in shapes],
        ("arbitrary",), prevs={3 + n: gb.get(name) for n, name in enumerate(names)})
    for n, name in enumerate(names):
        gb[name] = outs[3 + n]
    return outs[0], outs[1], outs[2]


def _attn_bwd(qs, k, v, dya, lse, delta, ride=()):
    t = qs.shape[0]
    tq = _att_tile(t)
    nq = t // tq
    nb = len(ride)
    groups = HEADS // ATT_HEADS

    steps = [(j, i) for j in range(nq) for i in range(j, nq)]
    j_of = jnp.asarray([s[0] for s in steps], jnp.int32)
    i_of = jnp.asarray([s[1] for s in steps], jnp.int32)

    def body(j_ref, i_ref, q_ref, k_ref, v_ref, do_ref, lse_ref, dl_ref, *refs):
        dq_ref, dk_ref, dv_ref = refs[nb:nb + 3]
        dq_s, dk_s, dv_s = refs[2 * nb + 3:2 * nb + 6]
        step_no = pl.program_id(1)
        j, i = j_ref[step_no], i_ref[step_no]
        if ride:
            start, finish = _exchange_ops(refs[:nb], refs[nb + 3:2 * nb + 3], refs[2 * nb + 6:])
            pr = pl.program_id(0)
            pl.when((pr == 0) & (step_no == 0))(start)
            pl.when((pr == groups - 1) & (step_no == len(steps) - 1))(finish)

        @pl.when(step_no == 0)
        def _():
            dq_s[...] = jnp.zeros(dq_s.shape, F32)

        def step(masked):
            keep = _causal_keep(tq, 0, 0) if masked else None
            lane = lax.broadcasted_iota(jnp.int32, (tq, 128), 1)
            rows = pl.ds(pl.multiple_of(i * tq, tq), tq)
            for hh in range(ATT_HEADS):
                sl = slice(128 * hh, 128 * hh + 128)
                pair = slice(128 * (hh // 2), 128 * (hh // 2) + 128)
                vv, do = v_ref[:, pair], do_ref[:, pair]
                qq, kk = q_ref[:, sl], k_ref[:, sl]
                s = lax.dot_general(qq, kk, NT, preferred_element_type=F32)
                p = jnp.exp2(s - lse_ref[hh])
                if masked:
                    p = jnp.where(keep, p, 0.0)
                do_h = jnp.where((lane < VD) if hh % 2 == 0 else (lane >= VD), do, jnp.zeros_like(do))
                dp = lax.dot_general(do_h, vv, NT, preferred_element_type=F32)
                ds = (p * (dp - dl_ref[hh])).astype(BF16)
                dv_s[:, pair] += lax.dot_general(p.astype(BF16), do_h, TN, preferred_element_type=F32)
                dk_s[:, sl] += lax.dot_general(ds, qq, TN, preferred_element_type=F32)
                dq_s[rows, sl] += jnp.dot(ds, kk, preferred_element_type=F32)

        @pl.when(i == j)
        def _():
            dk_s[...] = jnp.zeros(dk_s.shape, F32)
            dv_s[...] = jnp.zeros(dv_s.shape, F32)
            step(True)

        @pl.when(i > j)
        def _():
            step(False)

        @pl.when(i == nq - 1)
        def _():
            dk_ref[...] = dk_s[...].astype(BF16)
            dv_ref[...] = (dv_s[...] * LOG2E).astype(BF16)

        @pl.when(step_no == len(steps) - 1)
        def _():
            dq_ref[...] = dq_s[...].astype(BF16)

    qw, vw = 128 * ATT_HEADS, VD * ATT_HEADS
    qrow = lambda p, s, jt, it: (it[s], p)
    krow = lambda p, s, jt, it: (jt[s], p)
    col_spec = pl.BlockSpec((ATT_HEADS, tq, 1), lambda p, s, jt, it: (p, it[s], 0))
    grid_spec = pltpu.PrefetchScalarGridSpec(
        num_scalar_prefetch=2, grid=(groups, len(steps)),
        in_specs=[pl.BlockSpec((tq, qw), qrow), pl.BlockSpec((tq, qw), krow), pl.BlockSpec((tq, vw), krow),
                  pl.BlockSpec((tq, vw), qrow), col_spec, col_spec] + [ANY] * nb,
        out_specs=[pl.BlockSpec((t, qw), lambda p, s, jt, it: (0, p)), pl.BlockSpec((tq, qw), krow),
                   pl.BlockSpec((tq, vw), krow)] + [ANY] * nb,
        scratch_shapes=[pltpu.VMEM((t, qw), F32), pltpu.VMEM((tq, qw), F32), pltpu.VMEM((tq, vw), F32)]
        + ([pltpu.SemaphoreType.DMA((nb, 3))] * 2 if ride else []))
    outs = pl.pallas_call(
        body, name="attn_bwd_ride" if ride else "attn_bwd", grid_spec=grid_spec,
        out_shape=[_sds((t, QW), BF16), _sds((t, QW), BF16), _sds((t, HEADS * VD), BF16)]
        + [_sds((3,) + a.shape[1:], a.dtype) for a in ride],
        compiler_params=_cp("arbitrary", "arbitrary"))(j_of, i_of, qs, k, v, dya, lse, delta, *ride)
    return outs[0], outs[1], outs[2], list(outs[3:])


def _mla_prep_bwd(dqs, dk, dv, sv, p, l, depth, gb, tabs):
    za = sv["za"]
    t = za.shape[0]
    tm = min(ROW_TILE, t)

    def body(dq_ref, dk_ref, dv_ref, z_ref, gq_ref, gkv_ref, wuq_ref, wukv_ref, c_ref, s1_ref, s2_ref,
             dza_ref, dqp_ref, dkv_ref, dgq_ref, dgkv_ref):
        _acc_init(pl.program_id(0), dgq_ref, dgkv_ref)
        c, s1, s2 = c_ref[...], s1_ref[...], s2_ref[...]
        lane = lax.broadcasted_iota(jnp.int32, (tm, 128), 1)
        rope_lanes = (lane >= NOPE) & (lane < NOPE + ROPE)
        dkr = jnp.zeros((tm, 128), F32)
        for h in range(HEADS):
            sl = slice(128 * h, 128 * h + 128)
            dqp_ref[:, sl] = _rope_bwd(dq_ref[:, sl].astype(F32) * QSCALE, c, s1, s2).astype(BF16)
            dkh = dk_ref[:, sl]
            dkv_ref[:, sl] = dkh
            dkr = dkr + jnp.where(rope_lanes, dkh.astype(F32), 0.0)
        dkv_ref[:, QW:] = dv_ref[...]
        z = z_ref[...]
        dcq = lax.dot_general(dqp_ref[...], wuq_ref[...], NT, preferred_element_type=F32)
        dzq, dgq = _rms_bwd(z[:, :QR], gq_ref[...], dcq)
        dckv = lax.dot_general(dkv_ref[...], wukv_ref[...], NT, preferred_element_type=F32)
        dzkv, dgkv = _rms_bwd(z[:, QR:QR + KVR], gkv_ref[...], dckv)
        dgq_ref[...] += dgq
        dgkv_ref[...] += dgkv
        dza_ref[:, :QR] = dzq.astype(BF16)
        dza_ref[:, QR:QR + KVR] = dzkv.astype(BF16)
        dza_ref[:, QR + KVR:] = _rope_bwd(dkr, c, s1, s2).astype(BF16)

    dza, dqp, dkv, gb["q_norm_g"], gb["kv_norm_g"] = _pcall(
        body, "mla_prep_bwd", (t // tm,),
        [dqs, dk, dv, za, p["q_norm_g"], p["kv_norm_g"], p["w_uq"], p["w_ukv"], *tabs],
        [_row(tm, QW), _row(tm, QW), _row(tm, HEADS * VD), _row(tm, ZA), _lyr(l, 1, QR), _lyr(l, 1, KVR),
         _lyr(_wl(p["w_uq"], l), QR, QW), _lyr(_wl(p["w_ukv"], l), KVR, KVW), _row(tm, 128), _row(tm, 128), _row(tm, 128)],
        [_row(tm, ZA), _row(tm, QW), _row(tm, KVW), _lyr(l.g, 1, QR), _lyr(l.g, 1, KVR)],
        [_sds((t, ZA), BF16), _sds((t, QW), BF16), _sds((t, KVW), BF16), _sds((depth, 1, QR), F32),
         _sds((depth, 1, KVR), F32)], ("arbitrary",), prevs={3: gb.get("q_norm_g"), 4: gb.get("kv_norm_g")})
    return dza, dqp, dkv


def _in_proj_bwd(dza, dzb, dx2, sv, p, l, depth, gb):
    t = dx2.shape[0]
    tm = min(ROW_TILE, t)

    def body(dza_ref, dzb_ref, wa_ref, wb_ref, x_ref, dx2_ref, g_ref, dx_ref, dg_ref):
        _acc_init(pl.program_id(0), dg_ref)
        dh = (jnp.dot(dza_ref[...], wa_ref[...], preferred_element_type=F32)
              + jnp.dot(dzb_ref[...], wb_ref[...], preferred_element_type=F32))
        dx, dg = _rms_bwd(x_ref[...], g_ref[...], dh)
        dg_ref[...] += dg
        dx_ref[...] = dx2_ref[...] + dx

    dx, gb["mix_pre_g"] = _pcall(
        body, "in_proj_bwd", (t // tm,), [dza, dzb, p["w_in_a"], p["w_in_b"], sv["x"], dx2, p["mix_pre_g"]],
        [_row(tm, ZA), _row(tm, ZB), _lyr(_wl(p["w_in_a"], l), ZA, D), _lyr(_wl(p["w_in_b"], l), ZB, D), _row(tm, D), _row(tm, D), _lyr(l, 1, D)],
        [_row(tm, D), _lyr(l.g, 1, D)], [_sds((t, D), F32), _sds((depth, 1, D), F32)], ("arbitrary",),
        prevs={1: gb.get("mix_pre_g")})
    return dx


def _mm_tn(a, b, tn, name, l, depth, gb):
    t, k = a.shape
    n = b.shape[1]
    tt = min(ROW_TILE, t)

    def body(a_ref, b_ref, o_ref):
        _acc_init(pl.program_id(1), o_ref)
        o_ref[...] += lax.dot_general(a_ref[...], b_ref[...], TN, preferred_element_type=F32)

    gb[name] = _pcall(
        body, "d" + name, (n // tn, t // tt), [a, b],
        [pl.BlockSpec((tt, k), lambda j, s: (s, 0)), pl.BlockSpec((tt, tn), lambda j, s: (s, j))],
        pl.BlockSpec((None, k, tn), lambda j, s: (l.g, 0, j)), _sds((depth, k, n), F32), ("parallel", "arbitrary"),
        prevs={0: gb.get(name)})


def _dw_ffn(a, b, kind, l, depth, gb):
    t = a.shape[0]
    tt = min(ROW_TILE, t)
    nsteps = t // tt

    def body(a_ref, b_ref, o_ref, acc):
        s = pl.program_id(0)
        _acc_init(s, acc)
        acc[...] += lax.dot_general(a_ref[...], b_ref[...], TN, preferred_element_type=F32)

        @pl.when(s == nsteps - 1)
        def _():
            for k in range(4):
                o_ref[k] = acc[k * HP:(k + 1) * HP, :].astype(BF16)

    rows = lambda n: pl.BlockSpec((tt, n), lambda s: (s, 0))
    if kind == "down":
        name = "down"
        out_spec = pl.BlockSpec((4, None, HP, D), lambda s: (0, l.g, 0, 0))
        out_shape = _sds((4, depth, HP, D), BF16)
    else:
        which = 0 if kind == "gate" else 1
        name = "gu"
        out_spec = pl.BlockSpec((4, None, None, HP, D), lambda s: (0, l.g, which, 0, 0))
        out_shape = _sds((4, depth, 2, HP, D), BF16)
    gb[name] = _pcall(body, "dw_" + kind, (nsteps,), [a, b], [rows(DFFP), rows(D)], out_spec, out_shape, ("arbitrary",),
                      scratch=[pltpu.VMEM((DFFP, D), F32)], prevs={0: gb.get(name)})


def _ffn_views(bufs):
    return {"w_gu": bufs[1], "w_down": bufs[2].reshape(bufs[2].shape[:2] + (HP, D))}


def _layer_fwd(x, p, l, tabs, fetch):
    h1, za, zb = _in_proj(x, p, l)
    cqn, ckvn, qs, k, v = _mla_prep(za, p, l, tabs)
    ya, lse, bufs = _attn_fwd(qs, k, v, fetch)
    if fetch:
        p = {**p, **_ffn_views(bufs)}
    mix, yb, yc = _mixer_fwd(zb, ya, p, l)
    o, x2, h2 = _out_proj(mix, x, p, l)
    a, b, s = _ffn_up(h2, p, l)
    f, x3 = _ffn_down(s, x2, p, l)
    saved = dict(x=x, h1=h1, za=za, zb=zb, cqn=cqn, ckvn=ckvn, qs=qs, k=k, v=v, ya=ya, lse=lse, mix=mix, yb=yb, yc=yc,
                 o=o, x2=x2, h2=h2, a=a, b=b, s=s, f=f)
    return x3, saved, bufs if fetch else None


class _Layer(int):
    def __new__(cls, l, g):
        obj = int.__new__(cls, l)
        obj.g = g
        return obj


def _layer_bwd(dx3, p, sv, l, depth, gb, tabs, ride=(), ffn_front=None):
    df, da, db = _ffn_down_bwd(dx3, sv, p, l, depth, gb)
    _dw_ffn(sv["s"], df, "down", l, depth, gb)
    dx2 = _ffn_up_bwd(da, db, dx3, sv, p, l, depth, gb)
    _dw_ffn(da, sv["h2"], "gate", l, depth, gb)
    _dw_ffn(db, sv["h2"], "up", l, depth, gb)
    ffn_blocks = list(ffn_front(gb["gu"], gb["down"])) if ffn_front else []
    do, dmix = _out_proj_bwd(dx2, sv, p, l, depth, gb)
    _mm_tn(sv["mix"], do, D, "w_out", l, depth, gb)
    dya, dzb, delta = _mixer_bwd(dmix, sv, p, l, depth, gb)
    dqs, dk, dv, sent = _attn_bwd(sv["qs"], sv["k"], sv["v"], dya, sv["lse"], delta, tuple(ffn_blocks) + tuple(ride))
    dza, dqp, dkv = _mla_prep_bwd(dqs, dk, dv, sv, p, l, depth, gb, tabs)
    _mm_tn(sv["cqn"], dqp, QW, "w_uq", l, depth, gb)
    _mm_tn(sv["ckvn"], dkv, KVW, "w_ukv", l, depth, gb)
    _mm_tn(dza, sv["h1"], D, "w_in_a", l, depth, gb)
    _mm_tn(dzb, sv["h1"], D, "w_in_b", l, depth, gb)
    nf = len(ffn_blocks)
    return _in_proj_bwd(dza, dzb, dx2, sv, p, l, depth, gb), ffn_blocks, sent[:nf], sent[nf:]


def _rope_tables(positions):
    inv_freq = 1.0 / (ROPE_THETA ** (jnp.arange(0, ROPE // 2, dtype=F32) / (ROPE // 2)))
    ang = positions.astype(F32)[:, None] * inv_freq
    cos, sin = jnp.cos(ang), jnp.sin(ang)
    t = positions.shape[0]
    one, zero = jnp.ones((t, 64), F32), jnp.zeros((t, 16), F32)
    c = jnp.concatenate([one, cos, cos, one[:, :32]], axis=1)
    s1 = jnp.concatenate([zero, zero, zero, zero, -sin, zero, zero, zero], axis=1)
    s2 = jnp.concatenate([zero, zero, zero, zero, zero, sin, zero, zero], axis=1)
    return c, s1, s2


def _mixer_weight_params(full):
    w_in = full["w_in_t"]
    depth = w_in.shape[0]
    zpad = lambda n: jnp.zeros((depth, n, D), w_in.dtype)
    kv = full["w_ukv"].reshape(depth, KVR, HEADS, NOPE + VD)
    return {
        "w_in_a": jnp.concatenate([w_in[:, :640], zpad(64), w_in[:, 640:672], zpad(32)], axis=1),
        "w_in_b": w_in[:, 672:],
        "w_uq": jnp.pad(full["w_uq"].reshape(depth, QR, HEADS, NOPE + ROPE),
                        ((0, 0), (0, 0), (0, 0), (0, 32))).reshape(depth, QR, QW),
        "w_ukv": jnp.concatenate([jnp.pad(kv[..., :NOPE], ((0, 0), (0, 0), (0, 0), (0, 64))).reshape(depth, KVR, QW),
                                  kv[..., NOPE:].reshape(depth, KVR, HEADS * VD)], axis=2),
        "w_out": full["w_out"], "conv_w": full["conv_w"],
    }


def _small_params(w):
    p = {"w_sp": w["w_sp"], "b_sp": jnp.repeat(jnp.swapaxes(w["b_sp"], 1, 2), 64, axis=2)}
    for n in ("mix_pre_g", "mix_post_g", "ffn_pre_g", "ffn_post_g", "q_norm_g", "kv_norm_g", "sg_ln_g", "sg_ln_b",
              "out_norm_g"):
        p[n] = w[n][:, None, :]
    return p


def _natural_grads(gb):
    depth = gb["w_in_a"].shape[0]
    ga, kv = gb["w_in_a"], gb["w_ukv"]
    out = {
        "w_in_t": jnp.concatenate([ga[:, :640], ga[:, 704:736], gb["w_in_b"]], axis=1),
        "w_uq": gb["w_uq"].reshape(depth, QR, HEADS, 128)[..., :NOPE + ROPE].reshape(depth, QR, HEADS * (NOPE + ROPE)),
        "w_ukv": jnp.concatenate([kv[:, :, :QW].reshape(depth, KVR, HEADS, 128)[..., :NOPE],
                                  kv[:, :, QW:].reshape(depth, KVR, HEADS, VD)], axis=3).reshape(depth, KVR, -1),
        "b_sp": jnp.swapaxes(gb["b_sp_t"].reshape(depth, CHUNK, 4, 64).sum(axis=-1), 1, 2),
    }
    for n in ("w_out", "w_sp", "conv_w"):
        out[n] = gb[n]
    for n in ("mix_pre_g", "mix_post_g", "ffn_pre_g", "ffn_post_g", "q_norm_g", "kv_norm_g", "sg_ln_g", "sg_ln_b",
              "out_norm_g"):
        out[n] = gb[n][:, 0, :]
    return out


def _local_step(x, positions, target, small, mine, bufs, shard_shapes, fetch=True, front=None):
    depth = small["w_sp"].shape[0]
    tabs = _rope_tables(positions)
    ps = _small_params(small)
    saved, mixer_w = [], []
    for l in range(depth):
        mixer_w.append(_mixer_weight_params(_unpack_weights(bufs[0], l, shard_shapes)))
        p = {**ps, **mixer_w[l], **_ffn_views(bufs)}
        layers = [l + 1 if l + 1 < depth else None, l, l]
        x, sv, fetched = _layer_fwd(x, p, l, tabs, (mine, bufs, layers) if fetch else None)
        bufs = fetched or bufs
        saved.append(sv)
    dx, acc = _loss_head(x, target)
    loss = (0.5 / D) * jnp.sum(acc)
    gbs = [{} for _ in range(depth)]
    out = [{} for _ in range(depth)]
    ride = ()
    for l in reversed(range(depth)):
        ffn_front = (lambda gu, down, l=l: front[0](l, gu, down)) if front else None
        dx, out[l]["ffn"], out[l]["sent_ffn"], got = _layer_bwd(
            dx, {**ps, **mixer_w[l], **_ffn_views(bufs)}, saved[l], _Layer(l, 0), 1, gbs[l], tabs, ride, ffn_front)
        if ride:
            out[l + 1]["sent_mixer"], ride = got, ()
        grads = _natural_grads(gbs[l])
        if front:
            out[l]["mixer"] = front[1](l, grads)
            if l > 0:
                ride = tuple(out[l]["mixer"])
        else:
            out[l]["grads"] = (grads, gbs[l]["gu"], gbs[l]["down"])
    return loss, dx, out


def _place():
    x, y, c = lax.axis_index("x"), lax.axis_index("y"), lax.axis_index("c")
    chips = [(1 - x, y), (x, 1 - y), (1 - x, 1 - y)]
    return x, y, c, 2 * x + y, chips


def _remote(src, dst, send_sem, recv_sem, to):
    return pltpu.make_async_remote_copy(src_ref=src, dst_ref=dst, send_sem=send_sem, recv_sem=recv_sem, device_id=to,
                                        device_id_type=MESH_ID)


def _gather_ops(mine_refs, out_refs, sems, layers):
    send_sems, recv_sems, fsend_sems, frecv_sems = sems
    x, y, c, k, chips = _place()
    sib = (x, y, 1 - c)
    pairs = [(b, n) for n in range(3) for b in range(len(mine_refs)) if layers[b] is not None]

    def slot(n):
        return 2 * chips[n][0] + chips[n][1]

    def ici(b, n, dst_chip):
        return _remote(mine_refs[b].at[layers[b], c], out_refs[b].at[dst_chip, layers[b], c], send_sems.at[b, n],
                       recv_sems.at[b, n], (*chips[n], c))

    def d2d(b, n, half):
        piece = out_refs[b].at[slot(n), layers[b], half]
        return _remote(piece, piece, fsend_sems.at[b, n], frecv_sems.at[b, n], sib)

    def start():
        for b, n in pairs:
            ici(b, n, k).start()

    def hand_over():
        for b, n in pairs:
            ici(b, n, slot(n)).wait_recv()
            d2d(b, n, c).start()

    def drain():
        for b, n in pairs:
            d2d(b, n, 1 - c).wait_recv()
        for b, n in pairs:
            ici(b, n, k).wait_send()
            d2d(b, n, c).wait_send()

    return start, hand_over, drain


def _gather_first_layer(mine):
    nb = len(mine)

    def body(*refs):
        start, hand_over, drain = _gather_ops(refs[:nb], refs[nb:2 * nb], refs[2 * nb:], [0] + [None] * (nb - 1))
        start()
        hand_over()
        drain()

    return pl.pallas_call(
        body, name="gather_first_layer", in_specs=[ANY] * nb, out_specs=[ANY] * nb,
        out_shape=[_sds((4,) + a.shape, a.dtype) for a in mine],
        scratch_shapes=[pltpu.SemaphoreType.DMA((nb, 3))] * 4)(*mine)


def _swap_halves(bigs, wholes=()):
    nb, n = len(bigs), len(bigs) + len(wholes)

    def body(*refs):
        src, dst = refs[:n], refs[n:2 * n]
        send_sems, recv_sems = refs[2 * n:]
        x, y, c, _, _ = _place()
        sib = (x, y, 1 - c)
        cps = [_remote(src[b].at[:, 1 - c] if b < nb else src[b], dst[b], send_sems.at[b], recv_sems.at[b], sib)
               for b in range(n)]
        for cp in cps:
            cp.start()
        for cp in cps:
            cp.wait()

    return pl.pallas_call(
        body, name="swap_halves", in_specs=[ANY] * n, out_specs=[ANY] * n,
        out_shape=[_sds((4,) + a.shape[2:], a.dtype) for a in bigs] + [_sds(a.shape, a.dtype) for a in wholes],
        scratch_shapes=[pltpu.SemaphoreType.DMA((n,))] * 2)(*bigs, *wholes)


def _sum_tile(r):
    return max(cand for cand in range(16, 641, 16) if r % cand == 0)


def _pair_sum(big, rbig, c):
    _, _, r, w = big.shape
    tr = _sum_tile(r)

    def body(c_ref, big_ref, rbig_ref, p_ref):
        p_ref[...] = (big_ref[...].astype(F32) + rbig_ref[...].astype(F32)).astype(BF16)

    grid_spec = pltpu.PrefetchScalarGridSpec(
        num_scalar_prefetch=1, grid=(4, r // tr),
        in_specs=[pl.BlockSpec((None, None, tr, w), lambda j, i, cr: (j, cr[0], i, 0)),
                  pl.BlockSpec((None, tr, w), lambda j, i, cr: (j, i, 0))],
        out_specs=pl.BlockSpec((None, tr, w), lambda j, i, cr: (j, i, 0)))
    return pl.pallas_call(body, name="pair_sum", grid_spec=grid_spec, out_shape=_sds((4, r, w), BF16),
                          compiler_params=_cp("parallel", "parallel"))(c, big, rbig)


def _small_sum(parts):
    n, ns, _ = parts.shape

    def body(p_ref, o_ref):
        s = p_ref[0]
        for j in range(1, n):
            s = s + p_ref[j]
        o_ref[...] = s

    return pl.pallas_call(body, name="small_sum", out_shape=_sds((ns, 128), F32))(parts)


def _exchange_ops(p_refs, rb_refs, sems, small=None):
    send_sems, recv_sems = sems[0], sems[1]
    nb = len(p_refs)
    x, y, c, k, chips = _place()

    def copies(landing):
        out = []
        for n, (cx, cy) in enumerate(chips):
            to, kj = (cx, cy, c), 2 * cx + cy
            for b in range(nb):
                out.append(_remote(p_refs[b].at[k if landing else kj], rb_refs[b].at[n], send_sems.at[b, n],
                                   recv_sems.at[b, n], to))
            if small:
                out.append(_remote(small[0], small[1].at[kj if landing else k], send_sems.at[nb, n], recv_sems.at[nb, n], to))
        return out

    def local():
        return pltpu.make_async_copy(small[0], small[1].at[k], sems[2])

    def start():
        if small:
            local().start()
        for cp in copies(False):
            cp.start()

    def finish():
        for cp in copies(True):
            cp.wait_recv()
        for cp in copies(False):
            cp.wait_send()
        if small:
            local().wait()

    return start, finish


def _chip_exchange(ps, small):
    nb = len(ps)
    ns = small.shape[0]

    def body(*refs):
        start, finish = _exchange_ops(refs[:nb], refs[nb + 1:2 * nb + 1], refs[2 * nb + 2:], (refs[nb], refs[2 * nb + 1]))
        start()
        finish()

    return pl.pallas_call(
        body, name="chip_exchange", in_specs=[ANY] * (nb + 1), out_specs=[ANY] * (nb + 1),
        out_shape=[_sds((3,) + a.shape[1:], a.dtype) for a in ps] + [_sds((4, ns, 128), small.dtype)],
        scratch_shapes=[pltpu.SemaphoreType.DMA((nb + 1, 3))] * 2 + [pltpu.SemaphoreType.DMA(())])(*ps, small)


def _chip_sum(p, rb, chip):
    _, r, w = p.shape
    tr = _sum_tile(r)

    def body(k_ref, p_ref, rb_ref, o_ref):
        acc = p_ref[...].astype(F32)
        for j in range(3):
            acc = acc + rb_ref[j].astype(F32)
        o_ref[...] = acc

    grid_spec = pltpu.PrefetchScalarGridSpec(
        num_scalar_prefetch=1, grid=(r // tr,),
        in_specs=[pl.BlockSpec((None, tr, w), lambda i, kr: (kr[0], i, 0)), pl.BlockSpec((3, tr, w), lambda i, kr: (0, i, 0))],
        out_specs=pl.BlockSpec((tr, w), lambda i, kr: (i, 0)))
    return pl.pallas_call(body, name="chip_sum", grid_spec=grid_spec, out_shape=_sds((r, w), F32),
                          compiler_params=_cp("parallel"))(chip, p, rb)


def _send_to_sibling(reds):
    nb = len(reds)

    def body(*refs):
        red_refs, out_refs = refs[:nb], refs[nb:2 * nb]
        send_sems, recv_sems = refs[2 * nb:]
        x, y, c, _, _ = _place()
        cps = [_remote(red_refs[b], out_refs[b], send_sems.at[b], recv_sems.at[b], (x, y, 1 - c)) for b in range(nb)]
        for cp in cps:
            cp.start()
        for cp in cps:
            cp.wait()

    return pl.pallas_call(
        body, name="send_to_sibling", in_specs=[ANY] * nb, out_specs=[ANY] * nb,
        out_shape=[_sds(a.shape, a.dtype) for a in reds], scratch_shapes=[pltpu.SemaphoreType.DMA((nb,))] * 2)(*reds)


def _adam_math(w, g, m, v):
    nm = ADAM_B1 * m + (1.0 - ADAM_B1) * g
    nv = ADAM_B2 * v + (1.0 - ADAM_B2) * (g * g)
    m_hat = nm / (1.0 - ADAM_B1 ** ADAM_STEP)
    v_hat = nv / (1.0 - ADAM_B2 ** ADAM_STEP)
    return -ADAM_LR * (m_hat / (jnp.sqrt(v_hat) + ADAM_EPS) + ADAM_WD * w), nm, nv


def _adamw_shard(w, m, v, srcs, c, name, owner=0, split=None):
    depth, r, n = w.shape
    unit = math.gcd(split, r - split) if split else r
    tr = max(cand for cand in range(8, min(unit, 256) + 1, 8) if unit % cand == 0)
    sb = split // tr if split else None
    npad = srcs[0][0].shape[-1]

    def body(c_ref, w_ref, m_ref, v_ref, *refs):
        g_ref, d_ref, nm_ref, nv_ref = refs[2 * depth:]
        l, i = pl.program_id(0), pl.program_id(1)
        half = (i >= sb).astype(jnp.int32) if split else owner
        mine = c_ref[0] == half
        g = jnp.where(mine, refs[0][...], refs[1][...])
        for b in range(1, depth):
            g = jnp.where(l == b, jnp.where(mine, refs[2 * b][...], refs[2 * b + 1][...]), g)
        g = g[:, :n]
        g_ref[...] = g
        d_ref[...], nm_ref[...], nv_ref[...] = _adam_math(w_ref[...], g, m_ref[...], v_ref[...])

    def source(b):
        def index(l, i, cr):
            blk = jnp.where(i >= sb, i - sb, i) if split else i
            return (jnp.where(l == b, blk, 0), 0)
        return pl.BlockSpec((tr, npad), index)

    blk = pl.BlockSpec((None, tr, n), lambda l, i, cr: (l, i, 0))
    grid_spec = pltpu.PrefetchScalarGridSpec(
        num_scalar_prefetch=1, grid=(depth, r // tr),
        in_specs=[blk] * 3 + [source(b) for b in range(depth) for _ in range(2)], out_specs=[blk] * 4)
    return pl.pallas_call(body, name=name, grid_spec=grid_spec, out_shape=[_sds(w.shape, F32)] * 4,
                          compiler_params=_cp("parallel", "parallel"))(c, w, m, v, *[a for pair in srcs for a in pair])


def _pad_ffn_shards(w_gate, w_up, w_down):
    depth = w_gate.shape[0]
    hr = HP // 2

    def gu_body(g_ref, u_ref, o_ref):
        for which, ref in enumerate((g_ref, u_ref)):
            o_ref[which, 0:HS, :] = ref[...].astype(BF16)
            o_ref[which, HS:HP, :] = jnp.zeros((HP - HS, D), BF16)

    blk = pl.BlockSpec((None, HS, D), lambda l: (l, 0, 0))
    gu = pl.pallas_call(
        gu_body, name="pad_gate_up", grid=(depth,), in_specs=[blk, blk],
        out_specs=pl.BlockSpec((None, 2, HP, D), lambda l: (l, 0, 0, 0)),
        out_shape=_sds((depth, 2, HP, D), BF16), compiler_params=_cp("parallel"))(w_gate, w_up)

    def down_body(w_ref, o_ref):
        o_ref[0] = w_ref[0:hr, :].astype(BF16)
        o_ref[1, 0:HS - hr, :] = w_ref[hr:HS, :].astype(BF16)
        o_ref[1, HS - hr:hr, :] = jnp.zeros((HP - HS, D), BF16)

    down = pl.pallas_call(
        down_body, name="pad_down", grid=(depth,), in_specs=[pl.BlockSpec((None, HS, D), lambda l: (l, 0, 0))],
        out_specs=pl.BlockSpec((None, 2, hr, D), lambda l: (l, 0, 0, 0)),
        out_shape=_sds((depth, 2, hr, D), BF16), compiler_params=_cp("parallel"))(w_down)
    return gu, down


def _adamw_rows(w, g, m, v, name):
    r, n = w.shape
    tr = max(cand for cand in range(8, 513, 8) if r % cand == 0)

    def body(w_ref, g_ref, m_ref, v_ref, d_ref, nm_ref, nv_ref):
        d_ref[...], nm_ref[...], nv_ref[...] = _adam_math(w_ref[...], g_ref[...], m_ref[...], v_ref[...])

    blk = pl.BlockSpec((tr, n), lambda i: (i, 0))
    return pl.pallas_call(body, name=name, grid=(r // tr,), in_specs=[blk] * 4, out_specs=[blk] * 3,
                          out_shape=[_sds(w.shape, F32)] * 3, compiler_params=_cp("parallel"))(w, g, m, v)


def _to_pack(a, name):
    depth = a.shape[0]
    if name in ROW_SHARDED:
        return jnp.swapaxes(a.reshape(depth, 4, -1, D), 0, 1)
    return jnp.transpose(a.reshape(depth, a.shape[1], 4, a.shape[2] // 4), (2, 0, 1, 3)).reshape(4, depth, -1, D)


def _pack_rows(parts, lead, dtype, tail=None):
    pieces, at = [], 0
    for n, off, rows in PACK:
        if off > at:
            pieces.append(jnp.zeros(lead + (off - at, D), dtype))
        pieces.append(parts[n].astype(dtype))
        at = off + rows
    if tail is not None:
        pieces.append(tail)
        at += tail.shape[-2]
    pieces.append(jnp.zeros(lead + (PACK_ROWS - at, D), dtype))
    return jnp.concatenate(pieces, axis=len(lead))


def _pack_weight_shards(sh):
    depth = sh["w_in"].shape[0]
    parts = {n: sh[n].reshape(depth, rows, D) for n, _, rows in PACK}
    parts["w_in"] = jnp.swapaxes(sh["w_in"], 1, 2)
    conv = lax.bitcast_convert_type(sh["conv_w"].reshape(depth, 3 * 64), BF16).reshape(depth, 1, 384)
    flat = _pack_rows(parts, (depth,), BF16, tail=jnp.pad(conv, ((0, 0), (0, 0), (0, D - 384))))
    return flat.reshape(depth, 2, PACK_ROWS // 2, D)


def _unpack_weights(gathered, l, shard_shapes):
    depth = 1
    flat = gathered[:, l].reshape(4, 1, PACK_ROWS, D)
    full = {}
    for n, off, rows in PACK:
        if n == "w_in":
            full["w_in_t"] = jnp.swapaxes(flat[:, :, off:off + rows, :], 0, 1).reshape(depth, 4 * rows, D)
            continue
        shp = shard_shapes[n][1:]
        piece = flat[:, :, off:off + rows, :].reshape((4, depth) + shp)
        if n in ROW_SHARDED:
            full[n] = jnp.transpose(piece, (1, 0, 2, 3)).reshape(depth, 4 * shp[0], shp[1])
        else:
            full[n] = jnp.transpose(piece, (1, 2, 0, 3)).reshape(depth, shp[0], 4 * shp[1])
    conv = lax.bitcast_convert_type(flat[:, :, CONV_ROW, :384].reshape(4, depth, 192, 2), F32)
    full["conv_w"] = jnp.transpose(conv.reshape(4, depth, 3, 64), (1, 2, 0, 3)).reshape(depth, 3, CVW)
    return full


def _pack_grad_shards(g):
    depth = g["w_in_t"].shape[0]
    parts = {n: _to_pack(g[n], n) for n, _, _ in PACK if n != "w_in"}
    parts["w_in"] = jnp.swapaxes(g["w_in_t"].reshape(depth, 4, -1, D), 0, 1)
    return _pack_rows(parts, (4, depth), BF16)


def _pack_small(arrs, names_shapes, depth):
    flat = jnp.concatenate([arrs[n].reshape(depth, -1) for n, _ in names_shapes], axis=1).reshape(-1)
    rows = -(-flat.shape[0] // 1024) * 8
    return jnp.pad(flat, (0, rows * 128 - flat.shape[0])).reshape(rows, 128)


def _unpack_small(packed, names_shapes, depth):
    per_layer = sum(math.prod(s) for _, s in names_shapes)
    flat = packed.reshape(-1)[:depth * per_layer].reshape(depth, per_layer)
    out, off = {}, 0
    for n, s in names_shapes:
        size = math.prod(s)
        out[n] = flat[:, off:off + size].reshape((depth,) + s)
        off += size
    return out


def kernel(x, positions, mix_pre_g, mix_post_g, ffn_pre_g, ffn_post_g, w_in, q_norm_g, w_uq, kv_norm_g, w_ukv, sg_ln_g, sg_ln_b, w_sp, b_sp, conv_w, out_norm_g, w_out, w_gate, w_up, w_down, loss_target, m_mix_pre_g, m_mix_post_g, m_ffn_pre_g, m_ffn_post_g, m_w_in, m_q_norm_g, m_w_uq, m_kv_norm_g, m_w_ukv, m_sg_ln_g, m_sg_ln_b, m_w_sp, m_b_sp, m_conv_w, m_out_norm_g, m_w_out, m_w_gate, m_w_up, m_w_down, v_mix_pre_g, v_mix_post_g, v_ffn_pre_g, v_ffn_post_g, v_w_in, v_q_norm_g, v_w_uq, v_kv_norm_g, v_w_ukv, v_sg_ln_g, v_sg_ln_b, v_w_sp, v_b_sp, v_conv_w, v_out_norm_g, v_w_out, v_w_gate, v_w_up, v_w_down):
    w = dict(mix_pre_g=mix_pre_g, mix_post_g=mix_post_g, ffn_pre_g=ffn_pre_g, ffn_post_g=ffn_post_g, w_in=w_in,
             q_norm_g=q_norm_g, w_uq=w_uq, kv_norm_g=kv_norm_g, w_ukv=w_ukv, sg_ln_g=sg_ln_g, sg_ln_b=sg_ln_b, w_sp=w_sp,
             b_sp=b_sp, conv_w=conv_w, out_norm_g=out_norm_g, w_out=w_out, w_gate=w_gate, w_up=w_up, w_down=w_down)
    m = dict(mix_pre_g=m_mix_pre_g, mix_post_g=m_mix_post_g, ffn_pre_g=m_ffn_pre_g, ffn_post_g=m_ffn_post_g, w_in=m_w_in,
             q_norm_g=m_q_norm_g, w_uq=m_w_uq, kv_norm_g=m_kv_norm_g, w_ukv=m_w_ukv, sg_ln_g=m_sg_ln_g, sg_ln_b=m_sg_ln_b,
             w_sp=m_w_sp, b_sp=m_b_sp, conv_w=m_conv_w, out_norm_g=m_out_norm_g, w_out=m_w_out, w_gate=m_w_gate,
             w_up=m_w_up, w_down=m_w_down)
    v = dict(mix_pre_g=v_mix_pre_g, mix_post_g=v_mix_post_g, ffn_pre_g=v_ffn_pre_g, ffn_post_g=v_ffn_post_g, w_in=v_w_in,
             q_norm_g=v_q_norm_g, w_uq=v_w_uq, kv_norm_g=v_kv_norm_g, w_ukv=v_w_ukv, sg_ln_g=v_sg_ln_g, sg_ln_b=v_sg_ln_b,
             w_sp=v_w_sp, b_sp=v_b_sp, conv_w=v_conv_w, out_norm_g=v_out_norm_g, w_out=v_w_out, w_gate=v_w_gate,
             w_up=v_w_up, w_down=v_w_down)
    depth = w_in.shape[0]
    c = lax.axis_index("c").astype(jnp.int32).reshape(1)
    chip = (2 * lax.axis_index("x") + lax.axis_index("y")).astype(jnp.int32)

    mine = [_pack_weight_shards(w), *_pad_ffn_shards(jnp.swapaxes(w_gate, 1, 2), jnp.swapaxes(w_up, 1, 2), w_down)]
    bufs = [lax.dynamic_update_slice(g, a[None], (chip,) + (0,) * a.ndim)
            for g, a in zip(_gather_first_layer(mine), mine)]

    small_grads = [None] * depth
    small_pair = []

    def ffn_front(l, g_gu, g_down):
        bigs = [g_gu.reshape(4, 2, HP, D), g_down.reshape(4, 2, HP // 2, D)]
        return [_pair_sum(a, r, c) for a, r in zip(bigs, _swap_halves(bigs))]

    def mixer_front(l, grads):
        small_grads[l] = grads
        bigs = [_pack_grad_shards(grads).reshape(4, 2, PACK_ROWS // 2, D)]
        if l > 0:
            rbigs = _swap_halves(bigs)
        else:
            small = _pack_small({n: jnp.concatenate([g[n] for g in small_grads]) for n, _ in SMALL}, SMALL, depth)
            *rbigs, rsmall = _swap_halves(bigs, [small])
            small_pair.append(_small_sum(jnp.stack([small, rsmall])))
        return [_pair_sum(a, r, c) for a, r in zip(bigs, rbigs)]

    loss, dx, red = _local_step(x[0], positions[0], loss_target[0], w, mine, bufs,
                                {n: w[n].shape for n, _, _ in PACK}, front=(ffn_front, mixer_front))
    loss = lax.psum(loss, ("x", "y", "c"))

    *red[0]["sent_mixer"], rs = _chip_exchange(red[0]["mixer"], small_pair[0])
    own = [[_chip_sum(p, rb, chip.reshape(1))
            for p, rb in zip(r["mixer"] + r["ffn"], list(r["sent_mixer"]) + list(r["sent_ffn"]))] for r in red]
    flat = _send_to_sibling([a for o in own for a in o])
    other = [flat[3 * l:3 * l + 3] for l in range(depth)]
    g_small = _unpack_small(_small_sum(rs), SMALL, depth)
    g_small["conv_w"] = lax.dynamic_slice_in_dim(g_small["conv_w"], chip * 64, 64, axis=2)

    gw, delta, new_m, new_v = dict(g_small), {}, {}, {}

    def adam(n, srcs, turned=False, **where):
        view = (lambda a: jnp.swapaxes(a, 1, 2)) if turned else (lambda a: a)
        outs = _adamw_shard(view(w[n]), view(m[n]), view(v[n]), srcs, c, "adamw_" + n, **where)
        gw[n], delta[n], new_m[n], new_v[n] = [view(o) for o in outs]

    first = c[0] == 0
    packs = [jnp.concatenate([jnp.where(first, o[0], s[0]), jnp.where(first, s[0], o[0])]) for o, s in zip(own, other)]
    for n, off, rows in PACK[1:]:
        pieces = [pk[off:off + rows, :].reshape(w[n].shape[1:]) for pk in packs]
        adam(n, [(pc, pc) for pc in pieces], owner=0)
    rows_in = PACK[0][2]
    turn = lambda a: jnp.swapaxes(a, 1, 2).reshape(depth * rows_in, D)
    back = lambda a: jnp.swapaxes(a.reshape(depth, rows_in, D), 1, 2)
    g_in = jnp.concatenate([pk[:rows_in, :] for pk in packs])
    outs = _adamw_rows(turn(w["w_in"]), g_in, turn(m["w_in"]), turn(v["w_in"]), "adamw_w_in")
    gw["w_in"], delta["w_in"], new_m["w_in"], new_v["w_in"] = [back(a) for a in (g_in, *outs)]
    adam("w_gate", [(o[1], s[1]) for o, s in zip(own, other)], turned=True, owner=0)
    adam("w_up", [(o[1], s[1]) for o, s in zip(own, other)], turned=True, owner=1)
    adam("w_down", [(o[2], s[2]) for o, s in zip(own, other)], split=HP // 2)
    small_local = tuple((n, w[n].shape[1:]) for n, _ in SMALL)
    d_, m_, v_ = _adamw_rows(_pack_small(w, small_local, depth), _pack_small(gw, small_local, depth),
                             _pack_small(m, small_local, depth), _pack_small(v, small_local, depth), "adamw_small")
    delta.update(_unpack_small(d_, small_local, depth))
    new_m.update(_unpack_small(m_, small_local, depth))
    new_v.update(_unpack_small(v_, small_local, depth))

    return (loss, dx[None], *[gw[n] for n in WEIGHTS], *[delta[n] for n in WEIGHTS], *[new_m[n] for n in WEIGHTS],
            *[new_v[n] for n in WEIGHTS])
```

```python
import math

import jax
import jax.numpy as jnp
from jax import lax
from jax.experimental import pallas as pl
from jax.experimental.pallas import tpu as pltpu

F32 = jnp.float32
BF16 = jnp.bfloat16

D = 1024
HEADS = 8
NOPE = 64
ROPE = 32
VD = 64
QR = 384
KVR = 256
SGW = 256
CVW = 256
CHUNK = 128
DFF = 2816
EPS = 1e-6
ROPE_THETA = 10000.0
LOG2E = 1.4426950408889634
LN2 = 0.6931471805599453
QSCALE = (NOPE + ROPE) ** -0.5 * LOG2E
ZA = 768
ZB = 1280
QW = HEADS * 128
KVW = HEADS * 128 + HEADS * VD
NEG = -1e30
GC0 = 0.7978845608028654
GC1 = 0.044715

ADAM_LR = 0.001
ADAM_B1 = 0.9
ADAM_B2 = 0.999
ADAM_EPS = 1e-08
ADAM_WD = 0.01
ADAM_STEP = 10

V7X_VMEM_LIMIT = 52 * 1024 * 1024
ROW_TILE = 512
ATT_TILE = 512
ATT_HEADS = 4
ATT_HEADS_FWD = 8

NT = (((1,), (1,)), ((), ()))
TN = (((0,), (0,)), ((), ()))

HS = DFF // 4
HP = 768
DFFP = 4 * HP

PACK = (("w_in", 0, 488), ("w_out", 512, 256), ("w_ukv", 768, 64), ("w_uq", 832, 72))
CONV_ROW = 904
PACK_ROWS = 928
ROW_SHARDED = ("w_out",)
SMALL = (("mix_pre_g", (D,)), ("mix_post_g", (D,)), ("ffn_pre_g", (D,)), ("ffn_post_g", (D,)), ("q_norm_g", (QR,)),
         ("kv_norm_g", (KVR,)), ("sg_ln_g", (SGW,)), ("sg_ln_b", (SGW,)), ("w_sp", (4, CHUNK, CHUNK)), ("b_sp", (4, CHUNK)),
         ("conv_w", (3, CVW)), ("out_norm_g", (D,)))
WEIGHTS = ["mix_pre_g", "mix_post_g", "ffn_pre_g", "ffn_post_g", "w_in", "q_norm_g", "w_uq", "kv_norm_g", "w_ukv", "sg_ln_g",
           "sg_ln_b", "w_sp", "b_sp", "conv_w", "out_norm_g", "w_out", "w_gate", "w_up", "w_down"]

MESH_ID = pl.DeviceIdType.MESH
ANY = pl.BlockSpec(memory_space=pl.ANY)


def _cp(*sem):
    return pltpu.CompilerParams(dimension_semantics=sem, vmem_limit_bytes=V7X_VMEM_LIMIT)


def _sds(shape, dtype):
    return jax.ShapeDtypeStruct(shape, dtype)


def _row(tm, n):
    return pl.BlockSpec((tm, n), lambda i: (i, 0))


def _lyr(l, *shape):
    return pl.BlockSpec((None,) + shape, lambda *_: (l,) + (0,) * len(shape))


def _wl(a, l):
    return 0 if a.shape[0] == 1 else l


def _pcall(body, name, grid, ins, in_specs, out_specs, out_shape, sem, scratch=(), prevs=None):
    prevs = {k: v for k, v in (prevs or {}).items() if v is not None}
    order = sorted(prevs)
    n_in = len(ins)

    def wrapped(*refs):
        return body(*refs[:n_in], *refs[n_in + len(order):])

    return pl.pallas_call(
        wrapped, name=name, grid=grid, in_specs=list(in_specs) + [ANY] * len(order), out_specs=out_specs,
        out_shape=out_shape, scratch_shapes=list(scratch),
        input_output_aliases={n_in + i: k for i, k in enumerate(order)},
        compiler_params=_cp(*sem))(*ins, *[prevs[k] for k in order])


def _rms(x, g):
    r = lax.rsqrt(jnp.mean(x * x, axis=-1, keepdims=True) + EPS)
    return x * r * g


def _rms_bwd(x, g, dy):
    r = lax.rsqrt(jnp.mean(x * x, axis=-1, keepdims=True) + EPS)
    xh = x * r
    dg = jnp.sum(dy * xh, axis=0, keepdims=True)
    dxh = dy * g
    dx = r * (dxh - xh * jnp.mean(dxh * xh, axis=-1, keepdims=True))
    return dx, dg


def _sigmoid(x):
    return 0.5 * jnp.tanh(0.5 * x) + 0.5


def _gelu(x):
    return 0.5 * x * (1.0 + jnp.tanh(GC0 * (x + GC1 * x * x * x)))


def _gelu_grad(x):
    t = jnp.tanh(GC0 * (x + GC1 * x * x * x))
    return 0.5 * (1.0 + t) + 0.5 * x * (1.0 - t * t) * GC0 * (1.0 + 3.0 * GC1 * x * x)


def _rope(xb, c, s1, s2):
    return xb * c + pltpu.roll(xb, 112, 1) * s1 + pltpu.roll(xb, 16, 1) * s2


def _rope_bwd(dy, c, s1, s2):
    return dy * c + pltpu.roll(dy * s1, 16, 1) + pltpu.roll(dy * s2, 112, 1)


def _group_masks(shape):
    lane = lax.broadcasted_iota(jnp.int32, shape, 1)
    return [(lane >= 64 * g) & (lane < 64 * g + 64) for g in range(shape[1] // 64)]


def _group_mean(v, masks):
    out = jnp.zeros_like(v)
    for m in masks:
        s = jnp.sum(jnp.where(m, v, 0.0), axis=-1, keepdims=True) * (1.0 / 64.0)
        out = jnp.where(m, s, out)
    return out


def _pick_row(blk, idx):
    row = lax.broadcasted_iota(jnp.int32, blk.shape, 0)
    return jnp.sum(jnp.where(row == idx, blk, 0.0), axis=0, keepdims=True)


def _shift_down(y, k, first_rows):
    out = pltpu.roll(y, k, 0)
    row = lax.broadcasted_iota(jnp.int32, y.shape, 0)
    for idx in range(k):
        out = jnp.where(row == idx, first_rows[idx], out)
    return out


def _shift_up(y, k, last_rows):
    n = y.shape[0]
    out = pltpu.roll(y, n - k, 0)
    row = lax.broadcasted_iota(jnp.int32, y.shape, 0)
    for idx in range(k):
        out = jnp.where(row == n - k + idx, last_rows[idx], out)
    return out


def _tril_mask():
    r = lax.broadcasted_iota(jnp.int32, (CHUNK, CHUNK), 0)
    c = lax.broadcasted_iota(jnp.int32, (CHUNK, CHUNK), 1)
    return r >= c


def _sgu_forward(zu, zv, g_ln, b_ln, wc_bf, bsp, masks, cmasks):
    u = _gelu(zu)
    vv = _gelu(zv)
    mu = _group_mean(vv, masks)
    dv = vv - mu
    rs = lax.rsqrt(_group_mean(dv * dv, masks) + EPS)
    xh = dv * rs
    vn = xh * g_ln + b_ln
    chunks = []
    for ci in range(zu.shape[0] // CHUNK):
        vc = vn[ci * CHUNK:(ci + 1) * CHUNK, :]
        acc = bsp
        for g in range(4):
            acc = acc + jnp.dot(wc_bf[g], jnp.where(cmasks[g], vc, 0.0).astype(BF16), preferred_element_type=F32)
        chunks.append(acc)
    mixed = jnp.concatenate(chunks, axis=0) if len(chunks) > 1 else chunks[0]
    return u, vv, xh, rs, vn, mixed


def _conv_forward(gc, hh, prev_gc, prev_hh, first_tile, cw):
    yv = gc * hh
    prev = jnp.where(first_tile, 0.0, prev_gc * prev_hh)
    p6, p7 = _pick_row(prev, 6), _pick_row(prev, 7)
    sh1 = _shift_down(yv, 1, [p7])
    sh2 = _shift_down(yv, 2, [p6, p7])
    conv = sh2 * cw[0:1, :] + sh1 * cw[1:2, :] + yv * cw[2:3, :]
    return yv, sh1, sh2, conv


def _acc_init(step, *refs):
    @pl.when(step == 0)
    def _():
        for r in refs:
            r[...] = jnp.zeros(r.shape, r.dtype)


def _in_proj(x, p, l):
    t = x.shape[0]
    tm = min(ROW_TILE, t)

    def body(x_ref, g_ref, wa_ref, wb_ref, h_ref, za_ref, zb_ref):
        h = _rms(x_ref[...], g_ref[...]).astype(BF16)
        h_ref[...] = h
        za_ref[...] = lax.dot_general(h, wa_ref[...], NT, preferred_element_type=F32)
        zb_ref[...] = lax.dot_general(h, wb_ref[...], NT, preferred_element_type=F32)

    return _pcall(
        body, "in_proj", (t // tm,), [x, p["mix_pre_g"], p["w_in_a"], p["w_in_b"]],
        [_row(tm, D), _lyr(l, 1, D), _lyr(_wl(p["w_in_a"], l), ZA, D), _lyr(_wl(p["w_in_b"], l), ZB, D)],
        [_row(tm, D), _row(tm, ZA), _row(tm, ZB)],
        [_sds((t, D), BF16), _sds((t, ZA), F32), _sds((t, ZB), F32)], ("parallel",))


def _mla_prep(za, p, l, tabs):
    t = za.shape[0]
    tm = min(ROW_TILE, t)

    def body(z_ref, gq_ref, gkv_ref, wuq_ref, wukv_ref, c_ref, s1_ref, s2_ref, cq_ref, ckv_ref, q_ref, k_ref, v_ref):
        z = z_ref[...]
        cq = _rms(z[:, :QR], gq_ref[...]).astype(BF16)
        ckv = _rms(z[:, QR:QR + KVR], gkv_ref[...]).astype(BF16)
        cq_ref[...] = cq
        ckv_ref[...] = ckv
        c, s1, s2 = c_ref[...], s1_ref[...], s2_ref[...]
        kr = _rope(z[:, QR + KVR:], c, s1, s2)
        q = jnp.dot(cq, wuq_ref[...], preferred_element_type=F32)
        kv = jnp.dot(ckv, wukv_ref[...], preferred_element_type=F32)
        for h in range(HEADS):
            sl = slice(128 * h, 128 * h + 128)
            q_ref[:, sl] = (_rope(q[:, sl], c, s1, s2) * QSCALE).astype(BF16)
            k_ref[:, sl] = (kv[:, sl] + kr).astype(BF16)
        v_ref[...] = kv[:, QW:].astype(BF16)

    return _pcall(
        body, "mla_prep", (t // tm,), [za, p["q_norm_g"], p["kv_norm_g"], p["w_uq"], p["w_ukv"], *tabs],
        [_row(tm, ZA), _lyr(l, 1, QR), _lyr(l, 1, KVR), _lyr(_wl(p["w_uq"], l), QR, QW), _lyr(_wl(p["w_ukv"], l), KVR, KVW),
         _row(tm, 128), _row(tm, 128), _row(tm, 128)],
        [_row(tm, QR), _row(tm, KVR), _row(tm, QW), _row(tm, QW), _row(tm, HEADS * VD)],
        [_sds((t, QR), BF16), _sds((t, KVR), BF16), _sds((t, QW), BF16), _sds((t, QW), BF16),
         _sds((t, HEADS * VD), BF16)], ("parallel",))


def _att_tile(t):
    return min(ATT_TILE, max(t // 2, 128))


def _causal_keep(tq, i, j):
    row = lax.broadcasted_iota(jnp.int32, (tq, tq), 0) + i * tq
    col = lax.broadcasted_iota(jnp.int32, (tq, tq), 1) + j * tq
    return col <= row


def _attn_fwd(qs, k, v, fetch=None):
    t = qs.shape[0]
    tq = _att_tile(t)
    nq = t // tq
    rep = tq // 128
    heads = ATT_HEADS_FWD
    groups = HEADS // heads
    mine, bufs, fetch_layer = fetch if fetch else ((), (), None)
    nb = len(mine)

    steps = [(i, j) for i in range(nq) for j in range(i + 1)]
    i_of = jnp.asarray([s[0] for s in steps], jnp.int32)
    j_of = jnp.asarray([s[1] for s in steps], jnp.int32)

    def body(i_ref, j_ref, q_ref, k_ref, v_ref, *refs):
        o_ref, lse_ref = refs[2 * nb:2 * nb + 2]
        m_s, l_s, acc_s = refs[3 * nb + 2:3 * nb + 5]
        step_no = pl.program_id(1)
        i, j = i_ref[step_no], j_ref[step_no]
        if fetch:
            start, hand_over, drain = _gather_ops(refs[:nb], refs[2 * nb + 2:3 * nb + 2], refs[3 * nb + 5:], fetch_layer)
            pr = pl.program_id(0)
            pl.when((pr == 0) & (step_no == 0))(start)
            pl.when((pr == groups - 1) & (step_no == len(steps) * (8 - groups) // 8))(hand_over)
            pl.when((pr == groups - 1) & (step_no == len(steps) - 1))(drain)

        @pl.when(j == 0)
        def _():
            m_s[...] = jnp.full(m_s.shape, NEG, F32)
            l_s[...] = jnp.zeros(l_s.shape, F32)
            acc_s[...] = jnp.zeros(acc_s.shape, F32)

        def step(masked):
            keep = _causal_keep(tq, i, j) if masked else None
            for hh in range(heads):
                sl = slice(128 * hh, 128 * hh + 128)
                vv = v_ref[:, 128 * (hh // 2):128 * (hh // 2) + 128]
                s = lax.dot_general(q_ref[:, sl], k_ref[:, sl], NT, preferred_element_type=F32)
                if masked:
                    s = jnp.where(keep, s, NEG)
                m_old = m_s[hh]
                m_new = jnp.maximum(m_old, jnp.max(s, axis=-1, keepdims=True))
                alpha = jnp.exp2(m_old - m_new)
                p = jnp.exp2(s - jnp.tile(m_new, (1, rep)))
                l_s[hh] = alpha * l_s[hh] + jnp.sum(p, axis=-1, keepdims=True)
                acc_s[hh] = alpha * acc_s[hh] + jnp.dot(p.astype(BF16), vv, preferred_element_type=F32)
                m_s[hh] = m_new

        @pl.when(j < i)
        def _():
            step(False)

        @pl.when(j == i)
        def _():
            step(True)
            lane = lax.broadcasted_iota(jnp.int32, (tq, 128), 1)
            for pp in range(heads // 2):
                a, b = 2 * pp, 2 * pp + 1
                o_ref[:, 128 * pp:128 * pp + 128] = jnp.where(lane < VD, acc_s[a] / l_s[a], acc_s[b] / l_s[b])
            for hh in range(heads):
                lse_ref[hh] = (m_s[hh] + jnp.log2(l_s[hh]))[:, 0:1]

    qw, vw = 128 * heads, VD * heads
    stat = pltpu.VMEM((heads, tq, 128), F32)
    grid_spec = pltpu.PrefetchScalarGridSpec(
        num_scalar_prefetch=2, grid=(groups, len(steps)),
        in_specs=[pl.BlockSpec((tq, qw), lambda p, s, it, jt: (it[s], p)),
                  pl.BlockSpec((tq, qw), lambda p, s, it, jt: (jt[s], p)),
                  pl.BlockSpec((tq, vw), lambda p, s, it, jt: (jt[s], p))] + [ANY] * (2 * nb),
        out_specs=[pl.BlockSpec((tq, vw), lambda p, s, it, jt: (it[s], p)),
                   pl.BlockSpec((heads, tq, 1), lambda p, s, it, jt: (p, it[s], 0))] + [ANY] * nb,
        scratch_shapes=[stat, stat, stat] + ([pltpu.SemaphoreType.DMA((nb, 3))] * 4 if fetch else []))
    outs = pl.pallas_call(
        body, name="attn_fwd_fetch" if fetch else "attn_fwd", grid_spec=grid_spec,
        out_shape=[_sds((t, HEADS * VD), F32), _sds((HEADS, t, 1), F32)] + [_sds(b.shape, b.dtype) for b in bufs],
        input_output_aliases={5 + nb + b: 2 + b for b in range(nb)},
        compiler_params=_cp("arbitrary", "arbitrary"))(i_of, j_of, qs, k, v, *mine, *bufs)
    return outs[0], outs[1], list(outs[2:])


def _mixer_fwd(zb, ya, p, l):
    t = zb.shape[0]
    tm = min(ROW_TILE, t)
    hb = tm // 8

    def body(zb_ref, zprev_ref, ya_ref, gln_ref, bln_ref, wsp_ref, bsp_ref, cw_ref, go_ref, mix_ref, yb_ref, yc_ref):
        i = pl.program_id(0)
        masks = _group_masks((tm, SGW))
        cmasks = _group_masks((CHUNK, SGW))
        tril = _tril_mask()
        wc_bf = [jnp.where(tril, wsp_ref[g], 0.0).astype(BF16) for g in range(4)]
        u, _, _, _, _, mixed = _sgu_forward(zb_ref[:, 0:256], zb_ref[:, 256:512], gln_ref[...], bln_ref[...], wc_bf,
                                            bsp_ref[...], masks, cmasks)
        yb = u * mixed
        _, _, _, conv = _conv_forward(zb_ref[:, 768:1024], zb_ref[:, 1024:1280], zprev_ref[:, 768:1024],
                                      zprev_ref[:, 1024:1280], i == 0, cw_ref[...])
        yc = zb_ref[:, 512:768] * conv
        yb_ref[...] = yb
        yc_ref[...] = yc
        go = go_ref[...]
        mix_ref[:, 0:512] = _rms(ya_ref[...], go[:, 0:512]).astype(BF16)
        mix_ref[:, 512:768] = _rms(yb, go[:, 512:768]).astype(BF16)
        mix_ref[:, 768:1024] = _rms(yc, go[:, 768:1024]).astype(BF16)

    return _pcall(
        body, "mixer_fwd", (t // tm,),
        [zb, zb, ya, p["sg_ln_g"], p["sg_ln_b"], p["w_sp"], p["b_sp"], p["conv_w"], p["out_norm_g"]],
        [_row(tm, ZB), pl.BlockSpec((8, ZB), lambda i: (jnp.maximum(i * hb - 1, 0), 0)), _row(tm, 512),
         _lyr(l, 1, SGW), _lyr(l, 1, SGW), _lyr(l, 4, CHUNK, CHUNK), _lyr(l, CHUNK, SGW), _lyr(_wl(p["conv_w"], l), 3, CVW), _lyr(l, 1, D)],
        [_row(tm, D), _row(tm, SGW), _row(tm, CVW)],
        [_sds((t, D), BF16), _sds((t, SGW), F32), _sds((t, CVW), F32)], ("parallel",))


def _out_proj(mix, x, p, l):
    t = x.shape[0]
    tm = min(ROW_TILE, t)

    def body(mix_ref, w_ref, x_ref, gp_ref, gf_ref, o_ref, x2_ref, h2_ref):
        o = jnp.dot(mix_ref[...], w_ref[...], preferred_element_type=F32)
        o_ref[...] = o
        x2 = x_ref[...] + _rms(o, gp_ref[...])
        x2_ref[...] = x2
        h2_ref[...] = _rms(x2, gf_ref[...]).astype(BF16)

    return _pcall(
        body, "out_proj", (t // tm,), [mix, p["w_out"], x, p["mix_post_g"], p["ffn_pre_g"]],
        [_row(tm, D), _lyr(_wl(p["w_out"], l), D, D), _row(tm, D), _lyr(l, 1, D), _lyr(l, 1, D)],
        [_row(tm, D), _row(tm, D), _row(tm, D)],
        [_sds((t, D), F32), _sds((t, D), F32), _sds((t, D), BF16)], ("parallel",))


def _gu_all(l, which):
    return pl.BlockSpec((4, None, None, HP, D), lambda *_: (0, l, which, 0, 0))


def _down_all(l):
    return pl.BlockSpec((4, None, HP, D), lambda *_: (0, l, 0, 0))


def _ffn_up(h2, p, l):
    t = h2.shape[0]
    tm = min(ROW_TILE, t)

    def body(h_ref, wg_ref, wu_ref, a_ref, b_ref, s_ref):
        h = h_ref[...]
        a = lax.dot_general(h, wg_ref[...], NT, preferred_element_type=F32)
        b = lax.dot_general(h, wu_ref[...], NT, preferred_element_type=F32)
        a_ref[...] = a.astype(BF16)
        b_ref[...] = b.astype(BF16)
        s_ref[...] = (a * _sigmoid(a) * b).astype(BF16)

    blk = pl.BlockSpec((tm, HP), lambda k, i: (i, k))
    wblk = lambda which: pl.BlockSpec((None, None, None, HP, D), lambda k, i: (k, l, which, 0, 0))
    return _pcall(
        body, "ffn_up", (4, t // tm), [h2, p["w_gu"], p["w_gu"]],
        [pl.BlockSpec((tm, D), lambda k, i: (i, 0)), wblk(0), wblk(1)], [blk, blk, blk],
        [_sds((t, DFFP), BF16)] * 3, ("parallel", "parallel"))


def _ffn_down(s, x2, p, l):
    t = x2.shape[0]
    tm = min(ROW_TILE, t)

    def body(s_ref, w_ref, x_ref, g_ref, f_ref, x3_ref):
        f = jnp.dot(s_ref[:, 0:HP], w_ref[0], preferred_element_type=F32)
        for k in range(1, 4):
            f = f + jnp.dot(s_ref[:, k * HP:(k + 1) * HP], w_ref[k], preferred_element_type=F32)
        f_ref[...] = f
        x3_ref[...] = x_ref[...] + _rms(f, g_ref[...])

    return _pcall(
        body, "ffn_down", (t // tm,), [s, p["w_down"], x2, p["ffn_post_g"]],
        [_row(tm, DFFP), _down_all(l), _row(tm, D), _lyr(l, 1, D)], [_row(tm, D), _row(tm, D)],
        [_sds((t, D), F32), _sds((t, D), F32)], ("parallel",))


def _loss_head(y, target):
    t = y.shape[0]
    tm = min(ROW_TILE, t)

    def body(y_ref, t_ref, dy_ref, acc_ref):
        e = y_ref[...] - t_ref[...]
        dy_ref[...] = e * (1.0 / D)
        sq = jnp.sum(e * e, axis=0, keepdims=True)
        part = sq[:, 0:128]
        for b in range(1, D // 128):
            part = part + sq[:, 128 * b:128 * b + 128]
        _acc_init(pl.program_id(0), acc_ref)
        acc_ref[...] += part

    return _pcall(body, "loss_head", (t // tm,), [y, target], [_row(tm, D), _row(tm, D)],
                  [_row(tm, D), pl.BlockSpec((1, 128), lambda i: (0, 0))],
                  [_sds((t, D), F32), _sds((1, 128), F32)], ("arbitrary",))


def _ffn_down_bwd(dx3, sv, p, l, depth, gb):
    t = dx3.shape[0]
    tm = min(256, t)

    def body(dx_ref, f_ref, g_ref, w_ref, a_ref, b_ref, df_ref, da_ref, db_ref, dg_ref):
        _acc_init(pl.program_id(0), dg_ref)
        df, dg = _rms_bwd(f_ref[...], g_ref[...], dx_ref[...])
        dg_ref[...] += dg
        df = df.astype(BF16)
        df_ref[...] = df
        for k in range(4):
            sl = slice(k * HP, (k + 1) * HP)
            ds = lax.dot_general(df, w_ref[k], NT, preferred_element_type=F32)
            av = a_ref[:, sl].astype(F32)
            sig = _sigmoid(av)
            da_ref[:, sl] = (ds * b_ref[:, sl].astype(F32) * (sig * (1.0 + av * (1.0 - sig)))).astype(BF16)
            db_ref[:, sl] = (ds * (av * sig)).astype(BF16)

    df, da, db, gb["ffn_post_g"] = _pcall(
        body, "ffn_down_bwd", (t // tm,), [dx3, sv["f"], p["ffn_post_g"], p["w_down"], sv["a"], sv["b"]],
        [_row(tm, D), _row(tm, D), _lyr(l, 1, D), _down_all(l), _row(tm, DFFP), _row(tm, DFFP)],
        [_row(tm, D), _row(tm, DFFP), _row(tm, DFFP), _lyr(l.g, 1, D)],
        [_sds((t, D), BF16), _sds((t, DFFP), BF16), _sds((t, DFFP), BF16), _sds((depth, 1, D), F32)], ("arbitrary",),
        prevs={3: gb.get("ffn_post_g")})
    return df, da, db


def _ffn_up_bwd(da, db, dx3, sv, p, l, depth, gb):
    t = dx3.shape[0]
    tm = min(256, t)

    def body(da_ref, db_ref, wg_ref, wu_ref, x_ref, dx3_ref, g_ref, dx2_ref, dg_ref):
        _acc_init(pl.program_id(0), dg_ref)
        dh = jnp.zeros((tm, D), F32)
        for k in range(4):
            sl = slice(k * HP, (k + 1) * HP)
            dh = dh + jnp.dot(da_ref[:, sl], wg_ref[k], preferred_element_type=F32)
            dh = dh + jnp.dot(db_ref[:, sl], wu_ref[k], preferred_element_type=F32)
        dx, dg = _rms_bwd(x_ref[...], g_ref[...], dh)
        dg_ref[...] += dg
        dx2_ref[...] = dx3_ref[...] + dx

    dx2, gb["ffn_pre_g"] = _pcall(
        body, "ffn_up_bwd", (t // tm,), [da, db, p["w_gu"], p["w_gu"], sv["x2"], dx3, p["ffn_pre_g"]],
        [_row(tm, DFFP), _row(tm, DFFP), _gu_all(l, 0), _gu_all(l, 1), _row(tm, D), _row(tm, D), _lyr(l, 1, D)],
        [_row(tm, D), _lyr(l.g, 1, D)], [_sds((t, D), F32), _sds((depth, 1, D), F32)], ("arbitrary",),
        prevs={1: gb.get("ffn_pre_g")})
    return dx2


def _out_proj_bwd(dx2, sv, p, l, depth, gb):
    t = dx2.shape[0]
    tm = min(ROW_TILE, t)

    def body(dx_ref, o_ref, g_ref, w_ref, do_ref, dmix_ref, dg_ref):
        _acc_init(pl.program_id(0), dg_ref)
        do, dg = _rms_bwd(o_ref[...], g_ref[...], dx_ref[...])
        dg_ref[...] += dg
        do = do.astype(BF16)
        do_ref[...] = do
        dmix_ref[...] = lax.dot_general(do, w_ref[...], NT, preferred_element_type=F32)

    do, dmix, gb["mix_post_g"] = _pcall(
        body, "out_proj_bwd", (t // tm,), [dx2, sv["o"], p["mix_post_g"], p["w_out"]],
        [_row(tm, D), _row(tm, D), _lyr(l, 1, D), _lyr(_wl(p["w_out"], l), D, D)], [_row(tm, D), _row(tm, D), _lyr(l.g, 1, D)],
        [_sds((t, D), BF16), _sds((t, D), F32), _sds((depth, 1, D), F32)], ("arbitrary",),
        prevs={2: gb.get("mix_post_g")})
    return do, dmix


def _mixer_bwd(dmix, sv, p, l, depth, gb):
    zb = sv["zb"]
    t = zb.shape[0]
    tm = min(ROW_TILE, t)
    hb = tm // 8
    last_blk = t // 8 - 1
    nsteps = t // tm

    def body(dmix_ref, ya_ref, yb_ref, yc_ref, zb_ref, zprev_ref, znext_ref, ycn_ref, dmn_ref,
             gln_ref, bln_ref, wsp_ref, bsp_ref, cw_ref, go_ref,
             dya_ref, dzb_ref, delta_ref, dgo_ref, dgln_ref, dbln_ref, dwsp_ref, dbsp_ref, dcw_ref):
        i = pl.program_id(0)
        _acc_init(i, dgo_ref, dgln_ref, dbln_ref, dwsp_ref, dbsp_ref, dcw_ref)
        go = go_ref[...]
        dmix = dmix_ref[...]

        ya = ya_ref[...]
        dya, dga = _rms_bwd(ya, go[:, 0:512], dmix[:, 0:512])
        dyb, dgb_ = _rms_bwd(yb_ref[...], go[:, 512:768], dmix[:, 512:768])
        dyc, dgc_ = _rms_bwd(yc_ref[...], go[:, 768:1024], dmix[:, 768:1024])
        dgo_ref[:, 0:512] += dga
        dgo_ref[:, 512:768] += dgb_
        dgo_ref[:, 768:1024] += dgc_
        dya = dya * LN2
        dya_ref[...] = dya.astype(BF16)
        prod = dya * ya
        hmasks = _group_masks((tm, 512))
        for h in range(HEADS):
            delta_ref[h] = jnp.sum(jnp.where(hmasks[h], prod, 0.0), axis=-1, keepdims=True)

        masks = _group_masks((tm, SGW))
        cmasks = _group_masks((CHUNK, SGW))
        tril = _tril_mask()
        wc_bf = [jnp.where(tril, wsp_ref[g], 0.0).astype(BF16) for g in range(4)]
        zu, zv = zb_ref[:, 0:256], zb_ref[:, 256:512]
        g_ln = gln_ref[...]
        u, _, xh, rs, vn, mixed = _sgu_forward(zu, zv, g_ln, bln_ref[...], wc_bf, bsp_ref[...], masks, cmasks)
        du = dyb * mixed
        dmixed = dyb * u
        dvn_chunks = []
        dbsp = jnp.zeros((CHUNK, SGW), F32)
        for ci in range(tm // CHUNK):
            rows = slice(ci * CHUNK, (ci + 1) * CHUNK)
            dm_c = dmixed[rows, :]
            vn_c = vn[rows, :].astype(BF16)
            dbsp = dbsp + dm_c
            dvn_c = jnp.zeros((CHUNK, SGW), F32)
            for g in range(4):
                dm_g = jnp.where(cmasks[g], dm_c, 0.0).astype(BF16)
                dw = lax.dot_general(dm_g, vn_c, NT, preferred_element_type=F32)
                dwsp_ref[g] += jnp.where(tril, dw, 0.0)
                dvn_c = dvn_c + lax.dot_general(wc_bf[g], dm_g, TN, preferred_element_type=F32)
            dvn_chunks.append(dvn_c)
        dbsp_ref[...] += dbsp
        dvn = jnp.concatenate(dvn_chunks, axis=0) if len(dvn_chunks) > 1 else dvn_chunks[0]
        dgln_ref[...] += jnp.sum(dvn * xh, axis=0, keepdims=True)
        dbln_ref[...] += jnp.sum(dvn, axis=0, keepdims=True)
        dxh = dvn * g_ln
        dvv = rs * (dxh - _group_mean(dxh, masks) - xh * _group_mean(dxh * xh, masks))
        dzb_ref[:, 0:256] = (du * _gelu_grad(zu)).astype(BF16)
        dzb_ref[:, 256:512] = (dvv * _gelu_grad(zv)).astype(BF16)

        cwv = cw_ref[...]
        gb_, gc, hh = zb_ref[:, 512:768], zb_ref[:, 768:1024], zb_ref[:, 1024:1280]
        yv, sh1, sh2, conv = _conv_forward(gc, hh, zprev_ref[:, 768:1024], zprev_ref[:, 1024:1280], i == 0, cwv)
        dconv = dyc * gb_
        dzb_ref[:, 512:768] = (dyc * conv).astype(BF16)
        dcw_ref[0:1, :] += jnp.sum(dconv * sh2, axis=0, keepdims=True)
        dcw_ref[1:2, :] += jnp.sum(dconv * sh1, axis=0, keepdims=True)
        dcw_ref[2:3, :] += jnp.sum(dconv * yv, axis=0, keepdims=True)
        dycn, _ = _rms_bwd(ycn_ref[...], go[:, 768:1024], dmn_ref[...])
        dconv_next = jnp.where(i == nsteps - 1, 0.0, dycn * znext_ref[:, 512:768])
        n0, n1 = _pick_row(dconv_next, 0), _pick_row(dconv_next, 1)
        dyv = dconv * cwv[2:3, :] + _shift_up(dconv, 1, [n0]) * cwv[1:2, :] + _shift_up(dconv, 2, [n0, n1]) * cwv[0:1, :]
        dzb_ref[:, 768:1024] = (dyv * hh).astype(BF16)
        dzb_ref[:, 1024:1280] = (dyv * gc).astype(BF16)

    prev_map = lambda i: (jnp.maximum(i * hb - 1, 0), 0)
    next_map = lambda i: (jnp.minimum((i + 1) * hb, last_blk), 0)
    names = ("out_norm_g", "sg_ln_g", "sg_ln_b", "w_sp", "b_sp_t", "conv_w")
    shapes = ((1, D), (1, SGW), (1, SGW), (4, CHUNK, CHUNK), (CHUNK, SGW), (3, CVW))
    outs = _pcall(
        body, "mixer_bwd", (nsteps,),
        [dmix, sv["ya"], sv["yb"], sv["yc"], zb, zb, zb, sv["yc"], dmix, p["sg_ln_g"], p["sg_ln_b"], p["w_sp"], p["b_sp"],
         p["conv_w"], p["out_norm_g"]],
        [_row(tm, D), _row(tm, 512), _row(tm, SGW), _row(tm, CVW), _row(tm, ZB),
         pl.BlockSpec((8, ZB), prev_map), pl.BlockSpec((8, ZB), next_map), pl.BlockSpec((8, CVW), next_map),
         pl.BlockSpec((8, 256), lambda i: (jnp.minimum((i + 1) * hb, last_blk), 3)),
         _lyr(l, 1, SGW), _lyr(l, 1, SGW), _lyr(l, 4, CHUNK, CHUNK), _lyr(l, CHUNK, SGW), _lyr(_wl(p["conv_w"], l), 3, CVW), _lyr(l, 1, D)],
        [_row(tm, 512), _row(tm, ZB), pl.BlockSpec((HEADS, tm, 1), lambda i: (0, i, 0))] + [_lyr(l.g, *s) for s in shapes],
        [_sds((t, 512), BF16), _sds((t, ZB), BF16), _sds((HEADS, t, 1), F32)] + [_sds((depth,) + s, F32) for s in shapes],
        ("arbitrary",), prevs={3 + n: gb.get(name) for n, name in enumerate(names)})
    for n, name in enumerate(names):
        gb[name] = outs[3 + n]
    return outs[0], outs[1], outs[2]


def _attn_bwd(qs, k, v, dya, lse, delta, ride=()):
    t = qs.shape[0]
    tq = _att_tile(t)
    nq = t // tq
    nb = len(ride)
    groups = HEADS // ATT_HEADS

    steps = [(j, i) for j in range(nq) for i in range(j, nq)]
    j_of = jnp.asarray([s[0] for s in steps], jnp.int32)
    i_of = jnp.asarray([s[1] for s in steps], jnp.int32)

    def body(j_ref, i_ref, q_ref, k_ref, v_ref, do_ref, lse_ref, dl_ref, *refs):
        dq_ref, dk_ref, dv_ref = refs[nb:nb + 3]
        dq_s, dk_s, dv_s = refs[2 * nb + 3:2 * nb + 6]
        step_no = pl.program_id(1)
        j, i = j_ref[step_no], i_ref[step_no]
        if ride:
            start, finish = _exchange_ops(refs[:nb], refs[nb + 3:2 * nb + 3], refs[2 * nb + 6:])
            pr = pl.program_id(0)
            pl.when((pr == 0) & (step_no == 0))(start)
            pl.when((pr == groups - 1) & (step_no == len(steps) - 1))(finish)

        @pl.when(step_no == 0)
        def _():
            dq_s[...] = jnp.zeros(dq_s.shape, F32)

        def step(masked):
            keep = _causal_keep(tq, 0, 0) if masked else None
            lane = lax.broadcasted_iota(jnp.int32, (tq, 128), 1)
            rows = pl.ds(pl.multiple_of(i * tq, tq), tq)
            for hh in range(ATT_HEADS):
                sl = slice(128 * hh, 128 * hh + 128)
                pair = slice(128 * (hh // 2), 128 * (hh // 2) + 128)
                vv, do = v_ref[:, pair], do_ref[:, pair]
                qq, kk = q_ref[:, sl], k_ref[:, sl]
                s = lax.dot_general(qq, kk, NT, preferred_element_type=F32)
                p = jnp.exp2(s - lse_ref[hh])
                if masked:
                    p = jnp.where(keep, p, 0.0)
                do_h = jnp.where((lane < VD) if hh % 2 == 0 else (lane >= VD), do, jnp.zeros_like(do))
                dp = lax.dot_general(do_h, vv, NT, preferred_element_type=F32)
                ds = (p * (dp - dl_ref[hh])).astype(BF16)
                dv_s[:, pair] += lax.dot_general(p.astype(BF16), do_h, TN, preferred_element_type=F32)
                dk_s[:, sl] += lax.dot_general(ds, qq, TN, preferred_element_type=F32)
                dq_s[rows, sl] += jnp.dot(ds, kk, preferred_element_type=F32)

        @pl.when(i == j)
        def _():
            dk_s[...] = jnp.zeros(dk_s.shape, F32)
            dv_s[...] = jnp.zeros(dv_s.shape, F32)
            step(True)

        @pl.when(i > j)
        def _():
            step(False)

        @pl.when(i == nq - 1)
        def _():
            dk_ref[...] = dk_s[...].astype(BF16)
            dv_ref[...] = (dv_s[...] * LOG2E).astype(BF16)

        @pl.when(step_no == len(steps) - 1)
        def _():
            dq_ref[...] = dq_s[...].astype(BF16)

    qw, vw = 128 * ATT_HEADS, VD * ATT_HEADS
    qrow = lambda p, s, jt, it: (it[s], p)
    krow = lambda p, s, jt, it: (jt[s], p)
    col_spec = pl.BlockSpec((ATT_HEADS, tq, 1), lambda p, s, jt, it: (p, it[s], 0))
    grid_spec = pltpu.PrefetchScalarGridSpec(
        num_scalar_prefetch=2, grid=(groups, len(steps)),
        in_specs=[pl.BlockSpec((tq, qw), qrow), pl.BlockSpec((tq, qw), krow), pl.BlockSpec((tq, vw), krow),
                  pl.BlockSpec((tq, vw), qrow), col_spec, col_spec] + [ANY] * nb,
        out_specs=[pl.BlockSpec((t, qw), lambda p, s, jt, it: (0, p)), pl.BlockSpec((tq, qw), krow),
                   pl.BlockSpec((tq, vw), krow)] + [ANY] * nb,
        scratch_shapes=[pltpu.VMEM((t, qw), F32), pltpu.VMEM((tq, qw), F32), pltpu.VMEM((tq, vw), F32)]
        + ([pltpu.SemaphoreType.DMA((nb, 3))] * 2 if ride else []))
    outs = pl.pallas_call(
        body, name="attn_bwd_ride" if ride else "attn_bwd", grid_spec=grid_spec,
        out_shape=[_sds((t, QW), BF16), _sds((t, QW), BF16), _sds((t, HEADS * VD), BF16)]
        + [_sds((3,) + a.shape[1:], a.dtype) for a in ride],
        compiler_params=_cp("arbitrary", "arbitrary"))(j_of, i_of, qs, k, v, dya, lse, delta, *ride)
    return outs[0], outs[1], outs[2], list(outs[3:])


def _mla_prep_bwd(dqs, dk, dv, sv, p, l, depth, gb, tabs):
    za = sv["za"]
    t = za.shape[0]
    tm = min(ROW_TILE, t)

    def body(dq_ref, dk_ref, dv_ref, z_ref, gq_ref, gkv_ref, wuq_ref, wukv_ref, c_ref, s1_ref, s2_ref,
             dza_ref, dqp_ref, dkv_ref, dgq_ref, dgkv_ref):
        _acc_init(pl.program_id(0), dgq_ref, dgkv_ref)
        c, s1, s2 = c_ref[...], s1_ref[...], s2_ref[...]
        lane = lax.broadcasted_iota(jnp.int32, (tm, 128), 1)
        rope_lanes = (lane >= NOPE) & (lane < NOPE + ROPE)
        dkr = jnp.zeros((tm, 128), F32)
        for h in range(HEADS):
            sl = slice(128 * h, 128 * h + 128)
            dqp_ref[:, sl] = _rope_bwd(dq_ref[:, sl].astype(F32) * QSCALE, c, s1, s2).astype(BF16)
            dkh = dk_ref[:, sl]
            dkv_ref[:, sl] = dkh
            dkr = dkr + jnp.where(rope_lanes, dkh.astype(F32), 0.0)
        dkv_ref[:, QW:] = dv_ref[...]
        z = z_ref[...]
        dcq = lax.dot_general(dqp_ref[...], wuq_ref[...], NT, preferred_element_type=F32)
        dzq, dgq = _rms_bwd(z[:, :QR], gq_ref[...], dcq)
        dckv = lax.dot_general(dkv_ref[...], wukv_ref[...], NT, preferred_element_type=F32)
        dzkv, dgkv = _rms_bwd(z[:, QR:QR + KVR], gkv_ref[...], dckv)
        dgq_ref[...] += dgq
        dgkv_ref[...] += dgkv
        dza_ref[:, :QR] = dzq.astype(BF16)
        dza_ref[:, QR:QR + KVR] = dzkv.astype(BF16)
        dza_ref[:, QR + KVR:] = _rope_bwd(dkr, c, s1, s2).astype(BF16)

    dza, dqp, dkv, gb["q_norm_g"], gb["kv_norm_g"] = _pcall(
        body, "mla_prep_bwd", (t // tm,),
        [dqs, dk, dv, za, p["q_norm_g"], p["kv_norm_g"], p["w_uq"], p["w_ukv"], *tabs],
        [_row(tm, QW), _row(tm, QW), _row(tm, HEADS * VD), _row(tm, ZA), _lyr(l, 1, QR), _lyr(l, 1, KVR),
         _lyr(_wl(p["w_uq"], l), QR, QW), _lyr(_wl(p["w_ukv"], l), KVR, KVW), _row(tm, 128), _row(tm, 128), _row(tm, 128)],
        [_row(tm, ZA), _row(tm, QW), _row(tm, KVW), _lyr(l.g, 1, QR), _lyr(l.g, 1, KVR)],
        [_sds((t, ZA), BF16), _sds((t, QW), BF16), _sds((t, KVW), BF16), _sds((depth, 1, QR), F32),
         _sds((depth, 1, KVR), F32)], ("arbitrary",), prevs={3: gb.get("q_norm_g"), 4: gb.get("kv_norm_g")})
    return dza, dqp, dkv


def _in_proj_bwd(dza, dzb, dx2, sv, p, l, depth, gb):
    t = dx2.shape[0]
    tm = min(ROW_TILE, t)

    def body(dza_ref, dzb_ref, wa_ref, wb_ref, x_ref, dx2_ref, g_ref, dx_ref, dg_ref):
        _acc_init(pl.program_id(0), dg_ref)
        dh = (jnp.dot(dza_ref[...], wa_ref[...], preferred_element_type=F32)
              + jnp.dot(dzb_ref[...], wb_ref[...], preferred_element_type=F32))
        dx, dg = _rms_bwd(x_ref[...], g_ref[...], dh)
        dg_ref[...] += dg
        dx_ref[...] = dx2_ref[...] + dx

    dx, gb["mix_pre_g"] = _pcall(
        body, "in_proj_bwd", (t // tm,), [dza, dzb, p["w_in_a"], p["w_in_b"], sv["x"], dx2, p["mix_pre_g"]],
        [_row(tm, ZA), _row(tm, ZB), _lyr(_wl(p["w_in_a"], l), ZA, D), _lyr(_wl(p["w_in_b"], l), ZB, D), _row(tm, D), _row(tm, D), _lyr(l, 1, D)],
        [_row(tm, D), _lyr(l.g, 1, D)], [_sds((t, D), F32), _sds((depth, 1, D), F32)], ("arbitrary",),
        prevs={1: gb.get("mix_pre_g")})
    return dx


def _mm_tn(a, b, tn, name, l, depth, gb):
    t, k = a.shape
    n = b.shape[1]
    tt = min(ROW_TILE, t)

    def body(a_ref, b_ref, o_ref):
        _acc_init(pl.program_id(1), o_ref)
        o_ref[...] += lax.dot_general(a_ref[...], b_ref[...], TN, preferred_element_type=F32)

    gb[name] = _pcall(
        body, "d" + name, (n // tn, t // tt), [a, b],
        [pl.BlockSpec((tt, k), lambda j, s: (s, 0)), pl.BlockSpec((tt, tn), lambda j, s: (s, j))],
        pl.BlockSpec((None, k, tn), lambda j, s: (l.g, 0, j)), _sds((depth, k, n), F32), ("parallel", "arbitrary"),
        prevs={0: gb.get(name)})


def _dw_ffn(a, b, kind, l, depth, gb):
    t = a.shape[0]
    tt = min(ROW_TILE, t)
    nsteps = t // tt

    def body(a_ref, b_ref, o_ref, acc):
        s = pl.program_id(0)
        _acc_init(s, acc)
        acc[...] += lax.dot_general(a_ref[...], b_ref[...], TN, preferred_element_type=F32)

        @pl.when(s == nsteps - 1)
        def _():
            for k in range(4):
                o_ref[k] = acc[k * HP:(k + 1) * HP, :].astype(BF16)

    rows = lambda n: pl.BlockSpec((tt, n), lambda s: (s, 0))
    if kind == "down":
        name = "down"
        out_spec = pl.BlockSpec((4, None, HP, D), lambda s: (0, l.g, 0, 0))
        out_shape = _sds((4, depth, HP, D), BF16)
    else:
        which = 0 if kind == "gate" else 1
        name = "gu"
        out_spec = pl.BlockSpec((4, None, None, HP, D), lambda s: (0, l.g, which, 0, 0))
        out_shape = _sds((4, depth, 2, HP, D), BF16)
    gb[name] = _pcall(body, "dw_" + kind, (nsteps,), [a, b], [rows(DFFP), rows(D)], out_spec, out_shape, ("arbitrary",),
                      scratch=[pltpu.VMEM((DFFP, D), F32)], prevs={0: gb.get(name)})


def _ffn_views(bufs):
    return {"w_gu": bufs[1], "w_down": bufs[2].reshape(bufs[2].shape[:2] + (HP, D))}


def _layer_fwd(x, p, l, tabs, fetch):
    h1, za, zb = _in_proj(x, p, l)
    cqn, ckvn, qs, k, v = _mla_prep(za, p, l, tabs)
    ya, lse, bufs = _attn_fwd(qs, k, v, fetch)
    if fetch:
        p = {**p, **_ffn_views(bufs)}
    mix, yb, yc = _mixer_fwd(zb, ya, p, l)
    o, x2, h2 = _out_proj(mix, x, p, l)
    a, b, s = _ffn_up(h2, p, l)
    f, x3 = _ffn_down(s, x2, p, l)
    saved = dict(x=x, h1=h1, za=za, zb=zb, cqn=cqn, ckvn=ckvn, qs=qs, k=k, v=v, ya=ya, lse=lse, mix=mix, yb=yb, yc=yc,
                 o=o, x2=x2, h2=h2, a=a, b=b, s=s, f=f)
    return x3, saved, bufs if fetch else None


class _Layer(int):
    def __new__(cls, l, g):
        obj = int.__new__(cls, l)
        obj.g = g
        return obj


def _layer_bwd(dx3, p, sv, l, depth, gb, tabs, ride=(), ffn_front=None):
    df, da, db = _ffn_down_bwd(dx3, sv, p, l, depth, gb)
    _dw_ffn(sv["s"], df, "down", l, depth, gb)
    dx2 = _ffn_up_bwd(da, db, dx3, sv, p, l, depth, gb)
    _dw_ffn(da, sv["h2"], "gate", l, depth, gb)
    _dw_ffn(db, sv["h2"], "up", l, depth, gb)
    ffn_blocks = list(ffn_front(gb["gu"], gb["down"])) if ffn_front else []
    do, dmix = _out_proj_bwd(dx2, sv, p, l, depth, gb)
    _mm_tn(sv["mix"], do, D, "w_out", l, depth, gb)
    dya, dzb, delta = _mixer_bwd(dmix, sv, p, l, depth, gb)
    dqs, dk, dv, sent = _attn_bwd(sv["qs"], sv["k"], sv["v"], dya, sv["lse"], delta, tuple(ffn_blocks) + tuple(ride))
    dza, dqp, dkv = _mla_prep_bwd(dqs, dk, dv, sv, p, l, depth, gb, tabs)
    _mm_tn(sv["cqn"], dqp, QW, "w_uq", l, depth, gb)
    _mm_tn(sv["ckvn"], dkv, KVW, "w_ukv", l, depth, gb)
    _mm_tn(dza, sv["h1"], D, "w_in_a", l, depth, gb)
    _mm_tn(dzb, sv["h1"], D, "w_in_b", l, depth, gb)
    nf = len(ffn_blocks)
    return _in_proj_bwd(dza, dzb, dx2, sv, p, l, depth, gb), ffn_blocks, sent[:nf], sent[nf:]


def _rope_tables(positions):
    inv_freq = 1.0 / (ROPE_THETA ** (jnp.arange(0, ROPE // 2, dtype=F32) / (ROPE // 2)))
    ang = positions.astype(F32)[:, None] * inv_freq
    cos, sin = jnp.cos(ang), jnp.sin(ang)
    t = positions.shape[0]
    one, zero = jnp.ones((t, 64), F32), jnp.zeros((t, 16), F32)
    c = jnp.concatenate([one, cos, cos, one[:, :32]], axis=1)
    s1 = jnp.concatenate([zero, zero, zero, zero, -sin, zero, zero, zero], axis=1)
    s2 = jnp.concatenate([zero, zero, zero, zero, zero, sin, zero, zero], axis=1)
    return c, s1, s2


def _mixer_weight_params(full):
    w_in = full["w_in_t"]
    depth = w_in.shape[0]
    zpad = lambda n: jnp.zeros((depth, n, D), w_in.dtype)
    kv = full["w_ukv"].reshape(depth, KVR, HEADS, NOPE + VD)
    return {
        "w_in_a": jnp.concatenate([w_in[:, :640], zpad(64), w_in[:, 640:672], zpad(32)], axis=1),
        "w_in_b": w_in[:, 672:],
        "w_uq": jnp.pad(full["w_uq"].reshape(depth, QR, HEADS, NOPE + ROPE),
                        ((0, 0), (0, 0), (0, 0), (0, 32))).reshape(depth, QR, QW),
        "w_ukv": jnp.concatenate([jnp.pad(kv[..., :NOPE], ((0, 0), (0, 0), (0, 0), (0, 64))).reshape(depth, KVR, QW),
                                  kv[..., NOPE:].reshape(depth, KVR, HEADS * VD)], axis=2),
        "w_out": full["w_out"], "conv_w": full["conv_w"],
    }


def _small_params(w):
    p = {"w_sp": w["w_sp"], "b_sp": jnp.repeat(jnp.swapaxes(w["b_sp"], 1, 2), 64, axis=2)}
    for n in ("mix_pre_g", "mix_post_g", "ffn_pre_g", "ffn_post_g", "q_norm_g", "kv_norm_g", "sg_ln_g", "sg_ln_b",
              "out_norm_g"):
        p[n] = w[n][:, None, :]
    return p


def _natural_grads(gb):
    depth = gb["w_in_a"].shape[0]
    ga, kv = gb["w_in_a"], gb["w_ukv"]
    out = {
        "w_in_t": jnp.concatenate([ga[:, :640], ga[:, 704:736], gb["w_in_b"]], axis=1),
        "w_uq": gb["w_uq"].reshape(depth, QR, HEADS, 128)[..., :NOPE + ROPE].reshape(depth, QR, HEADS * (NOPE + ROPE)),
        "w_ukv": jnp.concatenate([kv[:, :, :QW].reshape(depth, KVR, HEADS, 128)[..., :NOPE],
                                  kv[:, :, QW:].reshape(depth, KVR, HEADS, VD)], axis=3).reshape(depth, KVR, -1),
        "b_sp": jnp.swapaxes(gb["b_sp_t"].reshape(depth, CHUNK, 4, 64).sum(axis=-1), 1, 2),
    }
    for n in ("w_out", "w_sp", "conv_w"):
        out[n] = gb[n]
    for n in ("mix_pre_g", "mix_post_g", "ffn_pre_g", "ffn_post_g", "q_norm_g", "kv_norm_g", "sg_ln_g", "sg_ln_b",
              "out_norm_g"):
        out[n] = gb[n][:, 0, :]
    return out


def _local_step(x, positions, target, small, mine, bufs, shard_shapes, fetch=True, front=None):
    depth = small["w_sp"].shape[0]
    tabs = _rope_tables(positions)
    ps = _small_params(small)
    saved, mixer_w = [], []
    for l in range(depth):
        mixer_w.append(_mixer_weight_params(_unpack_weights(bufs[0], l, shard_shapes)))
        p = {**ps, **mixer_w[l], **_ffn_views(bufs)}
        layers = [l + 1 if l + 1 < depth else None, l, l]
        x, sv, fetched = _layer_fwd(x, p, l, tabs, (mine, bufs, layers) if fetch else None)
        bufs = fetched or bufs
        saved.append(sv)
    dx, acc = _loss_head(x, target)
    loss = (0.5 / D) * jnp.sum(acc)
    gbs = [{} for _ in range(depth)]
    out = [{} for _ in range(depth)]
    ride = ()
    for l in reversed(range(depth)):
        ffn_front = (lambda gu, down, l=l: front[0](l, gu, down)) if front else None
        dx, out[l]["ffn"], out[l]["sent_ffn"], got = _layer_bwd(
            dx, {**ps, **mixer_w[l], **_ffn_views(bufs)}, saved[l], _Layer(l, 0), 1, gbs[l], tabs, ride, ffn_front)
        if ride:
            out[l + 1]["sent_mixer"], ride = got, ()
        grads = _natural_grads(gbs[l])
        if front:
            out[l]["mixer"] = front[1](l, grads)
            if l > 0:
                ride = tuple(out[l]["mixer"])
        else:
            out[l]["grads"] = (grads, gbs[l]["gu"], gbs[l]["down"])
    return loss, dx, out


def _place():
    x, y, c = lax.axis_index("x"), lax.axis_index("y"), lax.axis_index("c")
    chips = [(1 - x, y), (x, 1 - y), (1 - x, 1 - y)]
    return x, y, c, 2 * x + y, chips


def _remote(src, dst, send_sem, recv_sem, to):
    return pltpu.make_async_remote_copy(src_ref=src, dst_ref=dst, send_sem=send_sem, recv_sem=recv_sem, device_id=to,
                                        device_id_type=MESH_ID)


def _gather_ops(mine_refs, out_refs, sems, layers):
    send_sems, recv_sems, fsend_sems, frecv_sems = sems
    x, y, c, k, chips = _place()
    sib = (x, y, 1 - c)
    pairs = [(b, n) for n in range(3) for b in range(len(mine_refs)) if layers[b] is not None]

    def slot(n):
        return 2 * chips[n][0] + chips[n][1]

    def ici(b, n, dst_chip):
        return _remote(mine_refs[b].at[layers[b], c], out_refs[b].at[dst_chip, layers[b], c], send_sems.at[b, n],
                       recv_sems.at[b, n], (*chips[n], c))

    def d2d(b, n, half):
        piece = out_refs[b].at[slot(n), layers[b], half]
        return _remote(piece, piece, fsend_sems.at[b, n], frecv_sems.at[b, n], sib)

    def start():
        for b, n in pairs:
            ici(b, n, k).start()

    def hand_over():
        for b, n in pairs:
            ici(b, n, slot(n)).wait_recv()
            d2d(b, n, c).start()

    def drain():
        for b, n in pairs:
            d2d(b, n, 1 - c).wait_recv()
        for b, n in pairs:
            ici(b, n, k).wait_send()
            d2d(b, n, c).wait_send()

    return start, hand_over, drain


def _gather_first_layer(mine):
    nb = len(mine)

    def body(*refs):
        start, hand_over, drain = _gather_ops(refs[:nb], refs[nb:2 * nb], refs[2 * nb:], [0] + [None] * (nb - 1))
        start()
        hand_over()
        drain()

    return pl.pallas_call(
        body, name="gather_first_layer", in_specs=[ANY] * nb, out_specs=[ANY] * nb,
        out_shape=[_sds((4,) + a.shape, a.dtype) for a in mine],
        scratch_shapes=[pltpu.SemaphoreType.DMA((nb, 3))] * 4)(*mine)


def _swap_halves(bigs, wholes=()):
    nb, n = len(bigs), len(bigs) + len(wholes)

    def body(*refs):
        src, dst = refs[:n], refs[n:2 * n]
        send_sems, recv_sems = refs[2 * n:]
        x, y, c, _, _ = _place()
        sib = (x, y, 1 - c)
        cps = [_remote(src[b].at[:, 1 - c] if b < nb else src[b], dst[b], send_sems.at[b], recv_sems.at[b], sib)
               for b in range(n)]
        for cp in cps:
            cp.start()
        for cp in cps:
            cp.wait()

    return pl.pallas_call(
        body, name="swap_halves", in_specs=[ANY] * n, out_specs=[ANY] * n,
        out_shape=[_sds((4,) + a.shape[2:], a.dtype) for a in bigs] + [_sds(a.shape, a.dtype) for a in wholes],
        scratch_shapes=[pltpu.SemaphoreType.DMA((n,))] * 2)(*bigs, *wholes)


def _sum_tile(r):
    return max(cand for cand in range(16, 641, 16) if r % cand == 0)


def _pair_sum(big, rbig, c):
    _, _, r, w = big.shape
    tr = _sum_tile(r)

    def body(c_ref, big_ref, rbig_ref, p_ref):
        p_ref[...] = (big_ref[...].astype(F32) + rbig_ref[...].astype(F32)).astype(BF16)

    grid_spec = pltpu.PrefetchScalarGridSpec(
        num_scalar_prefetch=1, grid=(4, r // tr),
        in_specs=[pl.BlockSpec((None, None, tr, w), lambda j, i, cr: (j, cr[0], i, 0)),
                  pl.BlockSpec((None, tr, w), lambda j, i, cr: (j, i, 0))],
        out_specs=pl.BlockSpec((None, tr, w), lambda j, i, cr: (j, i, 0)))
    return pl.pallas_call(body, name="pair_sum", grid_spec=grid_spec, out_shape=_sds((4, r, w), BF16),
                          compiler_params=_cp("parallel", "parallel"))(c, big, rbig)


def _small_sum(parts):
    n, ns, _ = parts.shape

    def body(p_ref, o_ref):
        s = p_ref[0]
        for j in range(1, n):
            s = s + p_ref[j]
        o_ref[...] = s

    return pl.pallas_call(body, name="small_sum", out_shape=_sds((ns, 128), F32))(parts)


def _exchange_ops(p_refs, rb_refs, sems, small=None):
    send_sems, recv_sems = sems[0], sems[1]
    nb = len(p_refs)
    x, y, c, k, chips = _place()

    def copies(landing):
        out = []
        for n, (cx, cy) in enumerate(chips):
            to, kj = (cx, cy, c), 2 * cx + cy
            for b in range(nb):
                out.append(_remote(p_refs[b].at[k if landing else kj], rb_refs[b].at[n], send_sems.at[b, n],
                                   recv_sems.at[b, n], to))
            if small:
                out.append(_remote(small[0], small[1].at[kj if landing else k], send_sems.at[nb, n], recv_sems.at[nb, n], to))
        return out

    def local():
        return pltpu.make_async_copy(small[0], small[1].at[k], sems[2])

    def start():
        if small:
            local().start()
        for cp in copies(False):
            cp.start()

    def finish():
        for cp in copies(True):
            cp.wait_recv()
        for cp in copies(False):
            cp.wait_send()
        if small:
            local().wait()

    return start, finish


def _chip_exchange(ps, small):
    nb = len(ps)
    ns = small.shape[0]

    def body(*refs):
        start, finish = _exchange_ops(refs[:nb], refs[nb + 1:2 * nb + 1], refs[2 * nb + 2:], (refs[nb], refs[2 * nb + 1]))
        start()
        finish()

    return pl.pallas_call(
        body, name="chip_exchange", in_specs=[ANY] * (nb + 1), out_specs=[ANY] * (nb + 1),
        out_shape=[_sds((3,) + a.shape[1:], a.dtype) for a in ps] + [_sds((4, ns, 128), small.dtype)],
        scratch_shapes=[pltpu.SemaphoreType.DMA((nb + 1, 3))] * 2 + [pltpu.SemaphoreType.DMA(())])(*ps, small)


def _chip_sum(p, rb, chip):
    _, r, w = p.shape
    tr = _sum_tile(r)

    def body(k_ref, p_ref, rb_ref, o_ref):
        acc = p_ref[...].astype(F32)
        for j in range(3):
            acc = acc + rb_ref[j].astype(F32)
        o_ref[...] = acc

    grid_spec = pltpu.PrefetchScalarGridSpec(
        num_scalar_prefetch=1, grid=(r // tr,),
        in_specs=[pl.BlockSpec((None, tr, w), lambda i, kr: (kr[0], i, 0)), pl.BlockSpec((3, tr, w), lambda i, kr: (0, i, 0))],
        out_specs=pl.BlockSpec((tr, w), lambda i, kr: (i, 0)))
    return pl.pallas_call(body, name="chip_sum", grid_spec=grid_spec, out_shape=_sds((r, w), F32),
                          compiler_params=_cp("parallel"))(chip, p, rb)


def _send_to_sibling(reds):
    nb = len(reds)

    def body(*refs):
        red_refs, out_refs = refs[:nb], refs[nb:2 * nb]
        send_sems, recv_sems = refs[2 * nb:]
        x, y, c, _, _ = _place()
        cps = [_remote(red_refs[b], out_refs[b], send_sems.at[b], recv_sems.at[b], (x, y, 1 - c)) for b in range(nb)]
        for cp in cps:
            cp.start()
        for cp in cps:
            cp.wait()

    return pl.pallas_call(
        body, name="send_to_sibling", in_specs=[ANY] * nb, out_specs=[ANY] * nb,
        out_shape=[_sds(a.shape, a.dtype) for a in reds], scratch_shapes=[pltpu.SemaphoreType.DMA((nb,))] * 2)(*reds)


def _adam_math(w, g, m, v):
    nm = ADAM_B1 * m + (1.0 - ADAM_B1) * g
    nv = ADAM_B2 * v + (1.0 - ADAM_B2) * (g * g)
    m_hat = nm / (1.0 - ADAM_B1 ** ADAM_STEP)
    v_hat = nv / (1.0 - ADAM_B2 ** ADAM_STEP)
    return -ADAM_LR * (m_hat / (jnp.sqrt(v_hat) + ADAM_EPS) + ADAM_WD * w), nm, nv


def _adamw_shard(w, m, v, srcs, c, name, owner=0, split=None):
    depth, r, n = w.shape
    unit = math.gcd(split, r - split) if split else r
    tr = max(cand for cand in range(8, min(unit, 256) + 1, 8) if unit % cand == 0)
    sb = split // tr if split else None
    npad = srcs[0][0].shape[-1]

    def body(c_ref, w_ref, m_ref, v_ref, *refs):
        g_ref, d_ref, nm_ref, nv_ref = refs[2 * depth:]
        l, i = pl.program_id(0), pl.program_id(1)
        half = (i >= sb).astype(jnp.int32) if split else owner
        mine = c_ref[0] == half
        g = jnp.where(mine, refs[0][...], refs[1][...])
        for b in range(1, depth):
            g = jnp.where(l == b, jnp.where(mine, refs[2 * b][...], refs[2 * b + 1][...]), g)
        g = g[:, :n]
        g_ref[...] = g
        d_ref[...], nm_ref[...], nv_ref[...] = _adam_math(w_ref[...], g, m_ref[...], v_ref[...])

    def source(b):
        def index(l, i, cr):
            blk = jnp.where(i >= sb, i - sb, i) if split else i
            return (jnp.where(l == b, blk, 0), 0)
        return pl.BlockSpec((tr, npad), index)

    blk = pl.BlockSpec((None, tr, n), lambda l, i, cr: (l, i, 0))
    grid_spec = pltpu.PrefetchScalarGridSpec(
        num_scalar_prefetch=1, grid=(depth, r // tr),
        in_specs=[blk] * 3 + [source(b) for b in range(depth) for _ in range(2)], out_specs=[blk] * 4)
    return pl.pallas_call(body, name=name, grid_spec=grid_spec, out_shape=[_sds(w.shape, F32)] * 4,
                          compiler_params=_cp("parallel", "parallel"))(c, w, m, v, *[a for pair in srcs for a in pair])


def _pad_ffn_shards(w_gate, w_up, w_down):
    depth = w_gate.shape[0]
    hr = HP // 2

    def gu_body(g_ref, u_ref, o_ref):
        for which, ref in enumerate((g_ref, u_ref)):
            o_ref[which, 0:HS, :] = ref[...].astype(BF16)
            o_ref[which, HS:HP, :] = jnp.zeros((HP - HS, D), BF16)

    blk = pl.BlockSpec((None, HS, D), lambda l: (l, 0, 0))
    gu = pl.pallas_call(
        gu_body, name="pad_gate_up", grid=(depth,), in_specs=[blk, blk],
        out_specs=pl.BlockSpec((None, 2, HP, D), lambda l: (l, 0, 0, 0)),
        out_shape=_sds((depth, 2, HP, D), BF16), compiler_params=_cp("parallel"))(w_gate, w_up)

    def down_body(w_ref, o_ref):
        o_ref[0] = w_ref[0:hr, :].astype(BF16)
        o_ref[1, 0:HS - hr, :] = w_ref[hr:HS, :].astype(BF16)
        o_ref[1, HS - hr:hr, :] = jnp.zeros((HP - HS, D), BF16)

    down = pl.pallas_call(
        down_body, name="pad_down", grid=(depth,), in_specs=[pl.BlockSpec((None, HS, D), lambda l: (l, 0, 0))],
        out_specs=pl.BlockSpec((None, 2, hr, D), lambda l: (l, 0, 0, 0)),
        out_shape=_sds((depth, 2, hr, D), BF16), compiler_params=_cp("parallel"))(w_down)
    return gu, down


def _adamw_rows(w, g, m, v, name):
    r, n = w.shape
    tr = max(cand for cand in range(8, 513, 8) if r % cand == 0)

    def body(w_ref, g_ref, m_ref, v_ref, d_ref, nm_ref, nv_ref):
        d_ref[...], nm_ref[...], nv_ref[...] = _adam_math(w_ref[...], g_ref[...], m_ref[...], v_ref[...])

    blk = pl.BlockSpec((tr, n), lambda i: (i, 0))
    return pl.pallas_call(body, name=name, grid=(r // tr,), in_specs=[blk] * 4, out_specs=[blk] * 3,
                          out_shape=[_sds(w.shape, F32)] * 3, compiler_params=_cp("parallel"))(w, g, m, v)


def _to_pack(a, name):
    depth = a.shape[0]
    if name in ROW_SHARDED:
        return jnp.swapaxes(a.reshape(depth, 4, -1, D), 0, 1)
    return jnp.transpose(a.reshape(depth, a.shape[1], 4, a.shape[2] // 4), (2, 0, 1, 3)).reshape(4, depth, -1, D)


def _pack_rows(parts, lead, dtype, tail=None):
    pieces, at = [], 0
    for n, off, rows in PACK:
        if off > at:
            pieces.append(jnp.zeros(lead + (off - at, D), dtype))
        pieces.append(parts[n].astype(dtype))
        at = off + rows
    if tail is not None:
        pieces.append(tail)
        at += tail.shape[-2]
    pieces.append(jnp.zeros(lead + (PACK_ROWS - at, D), dtype))
    return jnp.concatenate(pieces, axis=len(lead))


def _pack_weight_shards(sh):
    depth = sh["w_in"].shape[0]
    parts = {n: sh[n].reshape(depth, rows, D) for n, _, rows in PACK}
    parts["w_in"] = jnp.swapaxes(sh["w_in"], 1, 2)
    conv = lax.bitcast_convert_type(sh["conv_w"].reshape(depth, 3 * 64), BF16).reshape(depth, 1, 384)
    flat = _pack_rows(parts, (depth,), BF16, tail=jnp.pad(conv, ((0, 0), (0, 0), (0, D - 384))))
    return flat.reshape(depth, 2, PACK_ROWS // 2, D)


def _unpack_weights(gathered, l, shard_shapes):
    depth = 1
    flat = gathered[:, l].reshape(4, 1, PACK_ROWS, D)
    full = {}
    for n, off, rows in PACK:
        if n == "w_in":
            full["w_in_t"] = jnp.swapaxes(flat[:, :, off:off + rows, :], 0, 1).reshape(depth, 4 * rows, D)
            continue
        shp = shard_shapes[n][1:]
        piece = flat[:, :, off:off + rows, :].reshape((4, depth) + shp)
        if n in ROW_SHARDED:
            full[n] = jnp.transpose(piece, (1, 0, 2, 3)).reshape(depth, 4 * shp[0], shp[1])
        else:
            full[n] = jnp.transpose(piece, (1, 2, 0, 3)).reshape(depth, shp[0], 4 * shp[1])
    conv = lax.bitcast_convert_type(flat[:, :, CONV_ROW, :384].reshape(4, depth, 192, 2), F32)
    full["conv_w"] = jnp.transpose(conv.reshape(4, depth, 3, 64), (1, 2, 0, 3)).reshape(depth, 3, CVW)
    return full


def _pack_grad_shards(g):
    depth = g["w_in_t"].shape[0]
    parts = {n: _to_pack(g[n], n) for n, _, _ in PACK if n != "w_in"}
    parts["w_in"] = jnp.swapaxes(g["w_in_t"].reshape(depth, 4, -1, D), 0, 1)
    return _pack_rows(parts, (4, depth), BF16)


def _pack_small(arrs, names_shapes, depth):
    flat = jnp.concatenate([arrs[n].reshape(depth, -1) for n, _ in names_shapes], axis=1).reshape(-1)
    rows = -(-flat.shape[0] // 1024) * 8
    return jnp.pad(flat, (0, rows * 128 - flat.shape[0])).reshape(rows, 128)


def _unpack_small(packed, names_shapes, depth):
    per_layer = sum(math.prod(s) for _, s in names_shapes)
    flat = packed.reshape(-1)[:depth * per_layer].reshape(depth, per_layer)
    out, off = {}, 0
    for n, s in names_shapes:
        size = math.prod(s)
        out[n] = flat[:, off:off + size].reshape((depth,) + s)
        off += size
    return out


def kernel(x, positions, mix_pre_g, mix_post_g, ffn_pre_g, ffn_post_g, w_in, q_norm_g, w_uq, kv_norm_g, w_ukv, sg_ln_g, sg_ln_b, w_sp, b_sp, conv_w, out_norm_g, w_out, w_gate, w_up, w_down, loss_target, m_mix_pre_g, m_mix_post_g, m_ffn_pre_g, m_ffn_post_g, m_w_in, m_q_norm_g, m_w_uq, m_kv_norm_g, m_w_ukv, m_sg_ln_g, m_sg_ln_b, m_w_sp, m_b_sp, m_conv_w, m_out_norm_g, m_w_out, m_w_gate, m_w_up, m_w_down, v_mix_pre_g, v_mix_post_g, v_ffn_pre_g, v_ffn_post_g, v_w_in, v_q_norm_g, v_w_uq, v_kv_norm_g, v_w_ukv, v_sg_ln_g, v_sg_ln_b, v_w_sp, v_b_sp, v_conv_w, v_out_norm_g, v_w_out, v_w_gate, v_w_up, v_w_down):
    w = dict(mix_pre_g=mix_pre_g, mix_post_g=mix_post_g, ffn_pre_g=ffn_pre_g, ffn_post_g=ffn_post_g, w_in=w_in,
             q_norm_g=q_norm_g, w_uq=w_uq, kv_norm_g=kv_norm_g, w_ukv=w_ukv, sg_ln_g=sg_ln_g, sg_ln_b=sg_ln_b, w_sp=w_sp,
             b_sp=b_sp, conv_w=conv_w, out_norm_g=out_norm_g, w_out=w_out, w_gate=w_gate, w_up=w_up, w_down=w_down)
    m = dict(mix_pre_g=m_mix_pre_g, mix_post_g=m_mix_post_g, ffn_pre_g=m_ffn_pre_g, ffn_post_g=m_ffn_post_g, w_in=m_w_in,
             q_norm_g=m_q_norm_g, w_uq=m_w_uq, kv_norm_g=m_kv_norm_g, w_ukv=m_w_ukv, sg_ln_g=m_sg_ln_g, sg_ln_b=m_sg_ln_b,
             w_sp=m_w_sp, b_sp=m_b_sp, conv_w=m_conv_w, out_norm_g=m_out_norm_g, w_out=m_w_out, w_gate=m_w_gate,
             w_up=m_w_up, w_down=m_w_down)
    v = dict(mix_pre_g=v_mix_pre_g, mix_post_g=v_mix_post_g, ffn_pre_g=v_ffn_pre_g, ffn_post_g=v_ffn_post_g, w_in=v_w_in,
             q_norm_g=v_q_norm_g, w_uq=v_w_uq, kv_norm_g=v_kv_norm_g, w_ukv=v_w_ukv, sg_ln_g=v_sg_ln_g, sg_ln_b=v_sg_ln_b,
             w_sp=v_w_sp, b_sp=v_b_sp, conv_w=v_conv_w, out_norm_g=v_out_norm_g, w_out=v_w_out, w_gate=v_w_gate,
             w_up=v_w_up, w_down=v_w_down)
    depth = w_in.shape[0]
    c = lax.axis_index("c").astype(jnp.int32).reshape(1)
    chip = (2 * lax.axis_index("x") + lax.axis_index("y")).astype(jnp.int32)

    mine = [_pack_weight_shards(w), *_pad_ffn_shards(jnp.swapaxes(w_gate, 1, 2), jnp.swapaxes(w_up, 1, 2), w_down)]
    bufs = [lax.dynamic_update_slice(g, a[None], (chip,) + (0,) * a.ndim)
            for g, a in zip(_gather_first_layer(mine), mine)]

    small_grads = [None] * depth
    small_pair = []

    def ffn_front(l, g_gu, g_down):
        bigs = [g_gu.reshape(4, 2, HP, D), g_down.reshape(4, 2, HP // 2, D)]
        return [_pair_sum(a, r, c) for a, r in zip(bigs, _swap_halves(bigs))]

    def mixer_front(l, grads):
        small_grads[l] = grads
        bigs = [_pack_grad_shards(grads).reshape(4, 2, PACK_ROWS // 2, D)]
        if l > 0:
            rbigs = _swap_halves(bigs)
        else:
            small = _pack_small({n: jnp.concatenate([g[n] for g in small_grads]) for n, _ in SMALL}, SMALL, depth)
            *rbigs, rsmall = _swap_halves(bigs, [small])
            small_pair.append(_small_sum(jnp.stack([small, rsmall])))
        return [_pair_sum(a, r, c) for a, r in zip(bigs, rbigs)]

    loss, dx, red = _local_step(x[0], positions[0], loss_target[0], w, mine, bufs,
                                {n: w[n].shape for n, _, _ in PACK}, front=(ffn_front, mixer_front))
    loss = lax.psum(loss, ("x", "y", "c"))

    *red[0]["sent_mixer"], rs = _chip_exchange(red[0]["mixer"], small_pair[0])
    own = [[_chip_sum(p, rb, chip.reshape(1))
            for p, rb in zip(r["mixer"] + r["ffn"], list(r["sent_mixer"]) + list(r["sent_ffn"]))] for r in red]
    flat = _send_to_sibling([a for o in own for a in o])
    other = [flat[3 * l:3 * l + 3] for l in range(depth)]
    g_small = _unpack_small(_small_sum(rs), SMALL, depth)
    g_small["conv_w"] = lax.dynamic_slice_in_dim(g_small["conv_w"], chip * 64, 64, axis=2)

    gw, delta, new_m, new_v = dict(g_small), {}, {}, {}

    def adam(n, srcs, turned=False, **where):
        view = (lambda a: jnp.swapaxes(a, 1, 2)) if turned else (lambda a: a)
        outs = _adamw_shard(view(w[n]), view(m[n]), view(v[n]), srcs, c, "adamw_" + n, **where)
        gw[n], delta[n], new_m[n], new_v[n] = [view(o) for o in outs]

    first = c[0] == 0
    packs = [jnp.concatenate([jnp.where(first, o[0], s[0]), jnp.where(first, s[0], o[0])]) for o, s in zip(own, other)]
    for n, off, rows in PACK[1:]:
        pieces = [pk[off:off + rows, :].reshape(w[n].shape[1:]) for pk in packs]
        adam(n, [(pc, pc) for pc in pieces], owner=0)
    rows_in = PACK[0][2]
    turn = lambda a: jnp.swapaxes(a, 1, 2).reshape(depth * rows_in, D)
    back = lambda a: jnp.swapaxes(a.reshape(depth, rows_in, D), 1, 2)
    g_in = jnp.concatenate([pk[:rows_in, :] for pk in packs])
    outs = _adamw_rows(turn(w["w_in"]), g_in, turn(m["w_in"]), turn(v["w_in"]), "adamw_w_in")
    gw["w_in"], delta["w_in"], new_m["w_in"], new_v["w_in"] = [back(a) for a in (g_in, *outs)]
    adam("w_gate", [(o[1], s[1]) for o, s in zip(own, other)], turned=True, owner=0)
    adam("w_up", [(o[1], s[1]) for o, s in zip(own, other)], turned=True, owner=1)
    adam("w_down", [(o[2], s[2]) for o, s in zip(own, other)], split=HP // 2)
    small_local = tuple((n, w[n].shape[1:]) for n, _ in SMALL)
    d_, m_, v_ = _adamw_rows(_pack_small(w, small_local, depth), _pack_small(gw, small_local, depth),
                             _pack_small(m, small_local, depth), _pack_small(v, small_local, depth), "adamw_small")
    delta.update(_unpack_small(d_, small_local, depth))
    new_m.update(_unpack_small(m_, small_local, depth))
    new_v.update(_unpack_small(v_, small_local, depth))

    return (loss, dx[None], *[gw[n] for n in WEIGHTS], *[delta[n] for n in WEIGHTS], *[new_m[n] for n in WEIGHTS],
            *[new_v[n] for n in WEIGHTS])
```

```python
import math

import jax
import jax.numpy as jnp
from jax import lax
from jax.experimental import pallas as pl
from jax.experimental.pallas import tpu as pltpu

F32 = jnp.float32
BF16 = jnp.bfloat16

D = 1024
HEADS = 8
NOPE = 64
ROPE = 32
VD = 64
QR = 384
KVR = 256
SGW = 256
CVW = 256
CHUNK = 128
DFF = 2816
EPS = 1e-6
ROPE_THETA = 10000.0
LOG2E = 1.4426950408889634
LN2 = 0.6931471805599453
QSCALE = (NOPE + ROPE) ** -0.5 * LOG2E
ZA = 768
ZB = 1280
QW = HEADS * 128
KVW = HEADS * 128 + HEADS * VD
NEG = -1e30
GC0 = 0.7978845608028654
GC1 = 0.044715

ADAM_LR = 0.001
ADAM_B1 = 0.9
ADAM_B2 = 0.999
ADAM_EPS = 1e-08
ADAM_WD = 0.01
ADAM_STEP = 10

V7X_VMEM_LIMIT = 52 * 1024 * 1024
ROW_TILE = 512
ATT_TILE = 512
ATT_HEADS = 4

NT = (((1,), (1,)), ((), ()))
TN = (((0,), (0,)), ((), ()))

HS = DFF // 4
HP = 768
DFFP = 4 * HP

PACK = (("w_in", 0, 488), ("w_out", 512, 256), ("w_ukv", 768, 64), ("w_uq", 832, 72))
CONV_ROW = 904
PACK_ROWS = 928
ROW_SHARDED = ("w_out",)
SMALL = (("mix_pre_g", (D,)), ("mix_post_g", (D,)), ("ffn_pre_g", (D,)), ("ffn_post_g", (D,)), ("q_norm_g", (QR,)),
         ("kv_norm_g", (KVR,)), ("sg_ln_g", (SGW,)), ("sg_ln_b", (SGW,)), ("w_sp", (4, CHUNK, CHUNK)), ("b_sp", (4, CHUNK)),
         ("conv_w", (3, CVW)), ("out_norm_g", (D,)))
WEIGHTS = ["mix_pre_g", "mix_post_g", "ffn_pre_g", "ffn_post_g", "w_in", "q_norm_g", "w_uq", "kv_norm_g", "w_ukv", "sg_ln_g",
           "sg_ln_b", "w_sp", "b_sp", "conv_w", "out_norm_g", "w_out", "w_gate", "w_up", "w_down"]

MESH_ID = pl.DeviceIdType.MESH
ANY = pl.BlockSpec(memory_space=pl.ANY)


def _cp(*sem):
    return pltpu.CompilerParams(dimension_semantics=sem, vmem_limit_bytes=V7X_VMEM_LIMIT)


def _sds(shape, dtype):
    return jax.ShapeDtypeStruct(shape, dtype)


def _row(tm, n):
    return pl.BlockSpec((tm, n), lambda i: (i, 0))


def _lyr(l, *shape):
    return pl.BlockSpec((None,) + shape, lambda *_: (l,) + (0,) * len(shape))


def _wl(a, l):
    return 0 if a.shape[0] == 1 else l


def _pcall(body, name, grid, ins, in_specs, out_specs, out_shape, sem, scratch=(), prevs=None):
    prevs = {k: v for k, v in (prevs or {}).items() if v is not None}
    order = sorted(prevs)
    n_in = len(ins)

    def wrapped(*refs):
        return body(*refs[:n_in], *refs[n_in + len(order):])

    return pl.pallas_call(
        wrapped, name=name, grid=grid, in_specs=list(in_specs) + [ANY] * len(order), out_specs=out_specs,
        out_shape=out_shape, scratch_shapes=list(scratch),
        input_output_aliases={n_in + i: k for i, k in enumerate(order)},
        compiler_params=_cp(*sem))(*ins, *[prevs[k] for k in order])


def _rms(x, g):
    r = lax.rsqrt(jnp.mean(x * x, axis=-1, keepdims=True) + EPS)
    return x * r * g


def _rms_bwd(x, g, dy):
    r = lax.rsqrt(jnp.mean(x * x, axis=-1, keepdims=True) + EPS)
    xh = x * r
    dg = jnp.sum(dy * xh, axis=0, keepdims=True)
    dxh = dy * g
    dx = r * (dxh - xh * jnp.mean(dxh * xh, axis=-1, keepdims=True))
    return dx, dg


def _sigmoid(x):
    return 0.5 * jnp.tanh(0.5 * x) + 0.5


def _gelu(x):
    return 0.5 * x * (1.0 + jnp.tanh(GC0 * (x + GC1 * x * x * x)))


def _gelu_grad(x):
    t = jnp.tanh(GC0 * (x + GC1 * x * x * x))
    return 0.5 * (1.0 + t) + 0.5 * x * (1.0 - t * t) * GC0 * (1.0 + 3.0 * GC1 * x * x)


def _rope(xb, c, s1, s2):
    return xb * c + pltpu.roll(xb, 112, 1) * s1 + pltpu.roll(xb, 16, 1) * s2


def _rope_bwd(dy, c, s1, s2):
    return dy * c + pltpu.roll(dy * s1, 16, 1) + pltpu.roll(dy * s2, 112, 1)


def _group_masks(shape):
    lane = lax.broadcasted_iota(jnp.int32, shape, 1)
    return [(lane >= 64 * g) & (lane < 64 * g + 64) for g in range(shape[1] // 64)]


def _group_mean(v, masks):
    out = jnp.zeros_like(v)
    for m in masks:
        s = jnp.sum(jnp.where(m, v, 0.0), axis=-1, keepdims=True) * (1.0 / 64.0)
        out = jnp.where(m, s, out)
    return out


def _pick_row(blk, idx):
    row = lax.broadcasted_iota(jnp.int32, blk.shape, 0)
    return jnp.sum(jnp.where(row == idx, blk, 0.0), axis=0, keepdims=True)


def _shift_down(y, k, first_rows):
    out = pltpu.roll(y, k, 0)
    row = lax.broadcasted_iota(jnp.int32, y.shape, 0)
    for idx in range(k):
        out = jnp.where(row == idx, first_rows[idx], out)
    return out


def _shift_up(y, k, last_rows):
    n = y.shape[0]
    out = pltpu.roll(y, n - k, 0)
    row = lax.broadcasted_iota(jnp.int32, y.shape, 0)
    for idx in range(k):
        out = jnp.where(row == n - k + idx, last_rows[idx], out)
    return out


def _tril_mask():
    r = lax.broadcasted_iota(jnp.int32, (CHUNK, CHUNK), 0)
    c = lax.broadcasted_iota(jnp.int32, (CHUNK, CHUNK), 1)
    return r >= c


def _sgu_forward(zu, zv, g_ln, b_ln, wc_bf, bsp, masks, cmasks):
    u = _gelu(zu)
    vv = _gelu(zv)
    mu = _group_mean(vv, masks)
    dv = vv - mu
    rs = lax.rsqrt(_group_mean(dv * dv, masks) + EPS)
    xh = dv * rs
    vn = xh * g_ln + b_ln
    chunks = []
    for ci in range(zu.shape[0] // CHUNK):
        vc = vn[ci * CHUNK:(ci + 1) * CHUNK, :]
        acc = bsp
        for g in range(4):
            acc = acc + jnp.dot(wc_bf[g], jnp.where(cmasks[g], vc, 0.0).astype(BF16), preferred_element_type=F32)
        chunks.append(acc)
    mixed = jnp.concatenate(chunks, axis=0) if len(chunks) > 1 else chunks[0]
    return u, vv, xh, rs, vn, mixed


def _conv_forward(gc, hh, prev_gc, prev_hh, first_tile, cw):
    yv = gc * hh
    prev = jnp.where(first_tile, 0.0, prev_gc * prev_hh)
    p6, p7 = _pick_row(prev, 6), _pick_row(prev, 7)
    sh1 = _shift_down(yv, 1, [p7])
    sh2 = _shift_down(yv, 2, [p6, p7])
    conv = sh2 * cw[0:1, :] + sh1 * cw[1:2, :] + yv * cw[2:3, :]
    return yv, sh1, sh2, conv


def _acc_init(step, *refs):
    @pl.when(step == 0)
    def _():
        for r in refs:
            r[...] = jnp.zeros(r.shape, r.dtype)


def _in_proj(x, p, l):
    t = x.shape[0]
    tm = min(ROW_TILE, t)

    def body(x_ref, g_ref, wa_ref, wb_ref, h_ref, za_ref, zb_ref):
        h = _rms(x_ref[...], g_ref[...]).astype(BF16)
        h_ref[...] = h
        za_ref[...] = lax.dot_general(h, wa_ref[...], NT, preferred_element_type=F32)
        zb_ref[...] = lax.dot_general(h, wb_ref[...], NT, preferred_element_type=F32)

    return _pcall(
        body, "in_proj", (t // tm,), [x, p["mix_pre_g"], p["w_in_a"], p["w_in_b"]],
        [_row(tm, D), _lyr(l, 1, D), _lyr(_wl(p["w_in_a"], l), ZA, D), _lyr(_wl(p["w_in_b"], l), ZB, D)],
        [_row(tm, D), _row(tm, ZA), _row(tm, ZB)],
        [_sds((t, D), BF16), _sds((t, ZA), F32), _sds((t, ZB), F32)], ("parallel",))


def _mla_prep(za, p, l, tabs):
    t = za.shape[0]
    tm = min(ROW_TILE, t)

    def body(z_ref, gq_ref, gkv_ref, wuq_ref, wukv_ref, c_ref, s1_ref, s2_ref, cq_ref, ckv_ref, q_ref, k_ref, v_ref):
        z = z_ref[...]
        cq = _rms(z[:, :QR], gq_ref[...]).astype(BF16)
        ckv = _rms(z[:, QR:QR + KVR], gkv_ref[...]).astype(BF16)
        cq_ref[...] = cq
        ckv_ref[...] = ckv
        c, s1, s2 = c_ref[...], s1_ref[...], s2_ref[...]
        kr = _rope(z[:, QR + KVR:], c, s1, s2)
        q = jnp.dot(cq, wuq_ref[...], preferred_element_type=F32)
        kv = jnp.dot(ckv, wukv_ref[...], preferred_element_type=F32)
        for h in range(HEADS):
            sl = slice(128 * h, 128 * h + 128)
            q_ref[:, sl] = (_rope(q[:, sl], c, s1, s2) * QSCALE).astype(BF16)
            k_ref[:, sl] = (kv[:, sl] + kr).astype(BF16)
        v_ref[...] = kv[:, QW:].astype(BF16)

    return _pcall(
        body, "mla_prep", (t // tm,), [za, p["q_norm_g"], p["kv_norm_g"], p["w_uq"], p["w_ukv"], *tabs],
        [_row(tm, ZA), _lyr(l, 1, QR), _lyr(l, 1, KVR), _lyr(_wl(p["w_uq"], l), QR, QW), _lyr(_wl(p["w_ukv"], l), KVR, KVW),
         _row(tm, 128), _row(tm, 128), _row(tm, 128)],
        [_row(tm, QR), _row(tm, KVR), _row(tm, QW), _row(tm, QW), _row(tm, HEADS * VD)],
        [_sds((t, QR), BF16), _sds((t, KVR), BF16), _sds((t, QW), BF16), _sds((t, QW), BF16),
         _sds((t, HEADS * VD), BF16)], ("parallel",))


def _att_tile(t):
    return min(ATT_TILE, max(t // 2, 128))


def _causal_keep(tq, i, j):
    row = lax.broadcasted_iota(jnp.int32, (tq, tq), 0) + i * tq
    col = lax.broadcasted_iota(jnp.int32, (tq, tq), 1) + j * tq
    return col <= row


def _attn_fwd(qs, k, v, fetch=None):
    t = qs.shape[0]
    tq = _att_tile(t)
    nq = t // tq
    rep = tq // 128
    groups = HEADS // ATT_HEADS
    mine, bufs, fetch_layer = fetch if fetch else ((), (), None)
    nb = len(mine)

    steps = [(i, j) for i in range(nq) for j in range(i + 1)]
    i_of = jnp.asarray([s[0] for s in steps], jnp.int32)
    j_of = jnp.asarray([s[1] for s in steps], jnp.int32)

    def body(i_ref, j_ref, q_ref, k_ref, v_ref, *refs):
        o_ref, lse_ref = refs[2 * nb:2 * nb + 2]
        m_s, l_s, acc_s = refs[3 * nb + 2:3 * nb + 5]
        step_no = pl.program_id(1)
        i, j = i_ref[step_no], j_ref[step_no]
        if fetch:
            start, hand_over, drain = _gather_ops(refs[:nb], refs[2 * nb + 2:3 * nb + 2], refs[3 * nb + 5:], fetch_layer)
            pr = pl.program_id(0)
            pl.when((pr == 0) & (step_no == 0))(start)
            pl.when((pr == groups - 1) & (step_no == 3 * len(steps) // 4))(hand_over)
            pl.when((pr == groups - 1) & (step_no == len(steps) - 1))(drain)

        @pl.when(j == 0)
        def _():
            m_s[...] = jnp.full(m_s.shape, NEG, F32)
            l_s[...] = jnp.zeros(l_s.shape, F32)
            acc_s[...] = jnp.zeros(acc_s.shape, F32)

        def step(masked):
            half = tq // 2
            for hh in range(ATT_HEADS):
                sl = slice(128 * hh, 128 * hh + 128)
                vv = v_ref[:, 128 * (hh // 2):128 * (hh // 2) + 128]
                kk = k_ref[:, sl]
                for r0 in (0, half):
                    rows = slice(r0, r0 + half)
                    s = lax.dot_general(q_ref[rows, sl], kk, NT, preferred_element_type=F32)
                    if masked:
                        row = lax.broadcasted_iota(jnp.int32, (half, tq), 0) + r0
                        col = lax.broadcasted_iota(jnp.int32, (half, tq), 1)
                        s = jnp.where(col <= row, s, NEG)
                    m_old = m_s[hh, rows]
                    m_new = jnp.maximum(m_old, jnp.max(s, axis=-1, keepdims=True))
                    alpha = jnp.exp2(m_old - m_new)
                    p = jnp.exp2(s - jnp.tile(m_new, (1, rep)))
                    l_s[hh, rows] = alpha * l_s[hh, rows] + jnp.sum(p, axis=-1, keepdims=True)
                    acc_s[hh, rows] = alpha * acc_s[hh, rows] + jnp.dot(p.astype(BF16), vv, preferred_element_type=F32)
                    m_s[hh, rows] = m_new

        @pl.when(j < i)
        def _():
            step(False)

        @pl.when(j == i)
        def _():
            step(True)
            lane = lax.broadcasted_iota(jnp.int32, (tq, 128), 1)
            for pp in range(ATT_HEADS // 2):
                a, b = 2 * pp, 2 * pp + 1
                o_ref[:, 128 * pp:128 * pp + 128] = jnp.where(lane < VD, acc_s[a] / l_s[a], acc_s[b] / l_s[b])
            for hh in range(ATT_HEADS):
                lse_ref[hh] = (m_s[hh] + jnp.log2(l_s[hh]))[:, 0:1]

    qw, vw = 128 * ATT_HEADS, VD * ATT_HEADS
    stat = pltpu.VMEM((ATT_HEADS, tq, 128), F32)
    grid_spec = pltpu.PrefetchScalarGridSpec(
        num_scalar_prefetch=2, grid=(groups, len(steps)),
        in_specs=[pl.BlockSpec((tq, qw), lambda p, s, it, jt: (it[s], p)),
                  pl.BlockSpec((tq, qw), lambda p, s, it, jt: (jt[s], p)),
                  pl.BlockSpec((tq, vw), lambda p, s, it, jt: (jt[s], p))] + [ANY] * (2 * nb),
        out_specs=[pl.BlockSpec((tq, vw), lambda p, s, it, jt: (it[s], p)),
                   pl.BlockSpec((ATT_HEADS, tq, 1), lambda p, s, it, jt: (p, it[s], 0))] + [ANY] * nb,
        scratch_shapes=[stat, stat, stat] + ([pltpu.SemaphoreType.DMA((nb, 3))] * 4 if fetch else []))
    outs = pl.pallas_call(
        body, name="attn_fwd_fetch" if fetch else "attn_fwd", grid_spec=grid_spec,
        out_shape=[_sds((t, HEADS * VD), F32), _sds((HEADS, t, 1), F32)] + [_sds(b.shape, b.dtype) for b in bufs],
        input_output_aliases={5 + nb + b: 2 + b for b in range(nb)},
        compiler_params=_cp("arbitrary", "arbitrary"))(i_of, j_of, qs, k, v, *mine, *bufs)
    return outs[0], outs[1], list(outs[2:])


def _mixer_fwd(zb, ya, p, l):
    t = zb.shape[0]
    tm = min(ROW_TILE, t)
    hb = tm // 8

    def body(zb_ref, zprev_ref, ya_ref, gln_ref, bln_ref, wsp_ref, bsp_ref, cw_ref, go_ref, mix_ref, yb_ref, yc_ref):
        i = pl.program_id(0)
        masks = _group_masks((tm, SGW))
        cmasks = _group_masks((CHUNK, SGW))
        tril = _tril_mask()
        wc_bf = [jnp.where(tril, wsp_ref[g], 0.0).astype(BF16) for g in range(4)]
        u, _, _, _, _, mixed = _sgu_forward(zb_ref[:, 0:256], zb_ref[:, 256:512], gln_ref[...], bln_ref[...], wc_bf,
                                            bsp_ref[...], masks, cmasks)
        yb = u * mixed
        _, _, _, conv = _conv_forward(zb_ref[:, 768:1024], zb_ref[:, 1024:1280], zprev_ref[:, 768:1024],
                                      zprev_ref[:, 1024:1280], i == 0, cw_ref[...])
        yc = zb_ref[:, 512:768] * conv
        yb_ref[...] = yb
        yc_ref[...] = yc
        go = go_ref[...]
        mix_ref[:, 0:512] = _rms(ya_ref[...], go[:, 0:512]).astype(BF16)
        mix_ref[:, 512:768] = _rms(yb, go[:, 512:768]).astype(BF16)
        mix_ref[:, 768:1024] = _rms(yc, go[:, 768:1024]).astype(BF16)

    return _pcall(
        body, "mixer_fwd", (t // tm,),
        [zb, zb, ya, p["sg_ln_g"], p["sg_ln_b"], p["w_sp"], p["b_sp"], p["conv_w"], p["out_norm_g"]],
        [_row(tm, ZB), pl.BlockSpec((8, ZB), lambda i: (jnp.maximum(i * hb - 1, 0), 0)), _row(tm, 512),
         _lyr(l, 1, SGW), _lyr(l, 1, SGW), _lyr(l, 4, CHUNK, CHUNK), _lyr(l, CHUNK, SGW), _lyr(_wl(p["conv_w"], l), 3, CVW), _lyr(l, 1, D)],
        [_row(tm, D), _row(tm, SGW), _row(tm, CVW)],
        [_sds((t, D), BF16), _sds((t, SGW), F32), _sds((t, CVW), F32)], ("parallel",))


def _out_proj(mix, x, p, l):
    t = x.shape[0]
    tm = min(ROW_TILE, t)

    def body(mix_ref, w_ref, x_ref, gp_ref, gf_ref, o_ref, x2_ref, h2_ref):
        o = jnp.dot(mix_ref[...], w_ref[...], preferred_element_type=F32)
        o_ref[...] = o
        x2 = x_ref[...] + _rms(o, gp_ref[...])
        x2_ref[...] = x2
        h2_ref[...] = _rms(x2, gf_ref[...]).astype(BF16)

    return _pcall(
        body, "out_proj", (t // tm,), [mix, p["w_out"], x, p["mix_post_g"], p["ffn_pre_g"]],
        [_row(tm, D), _lyr(_wl(p["w_out"], l), D, D), _row(tm, D), _lyr(l, 1, D), _lyr(l, 1, D)],
        [_row(tm, D), _row(tm, D), _row(tm, D)],
        [_sds((t, D), F32), _sds((t, D), F32), _sds((t, D), BF16)], ("parallel",))


def _gu_all(l, which):
    return pl.BlockSpec((4, None, None, HP, D), lambda *_: (0, l, which, 0, 0))


def _down_all(l):
    return pl.BlockSpec((4, None, HP, D), lambda *_: (0, l, 0, 0))


def _ffn_up(h2, p, l):
    t = h2.shape[0]
    tm = min(ROW_TILE, t)

    def body(h_ref, wg_ref, wu_ref, a_ref, b_ref, s_ref):
        h = h_ref[...]
        a = lax.dot_general(h, wg_ref[...], NT, preferred_element_type=F32)
        b = lax.dot_general(h, wu_ref[...], NT, preferred_element_type=F32)
        a_ref[...] = a.astype(BF16)
        b_ref[...] = b.astype(BF16)
        s_ref[...] = (a * _sigmoid(a) * b).astype(BF16)

    blk = pl.BlockSpec((tm, HP), lambda k, i: (i, k))
    wblk = lambda which: pl.BlockSpec((None, None, None, HP, D), lambda k, i: (k, l, which, 0, 0))
    return _pcall(
        body, "ffn_up", (4, t // tm), [h2, p["w_gu"], p["w_gu"]],
        [pl.BlockSpec((tm, D), lambda k, i: (i, 0)), wblk(0), wblk(1)], [blk, blk, blk],
        [_sds((t, DFFP), BF16)] * 3, ("parallel", "parallel"))


def _ffn_down(s, x2, p, l):
    t = x2.shape[0]
    tm = min(ROW_TILE, t)

    def body(s_ref, w_ref, x_ref, g_ref, f_ref, x3_ref):
        f = jnp.dot(s_ref[:, 0:HP], w_ref[0], preferred_element_type=F32)
        for k in range(1, 4):
            f = f + jnp.dot(s_ref[:, k * HP:(k + 1) * HP], w_ref[k], preferred_element_type=F32)
        f_ref[...] = f
        x3_ref[...] = x_ref[...] + _rms(f, g_ref[...])

    return _pcall(
        body, "ffn_down", (t // tm,), [s, p["w_down"], x2, p["ffn_post_g"]],
        [_row(tm, DFFP), _down_all(l), _row(tm, D), _lyr(l, 1, D)], [_row(tm, D), _row(tm, D)],
        [_sds((t, D), F32), _sds((t, D), F32)], ("parallel",))


def _loss_head(y, target):
    t = y.shape[0]
    tm = min(ROW_TILE, t)

    def body(y_ref, t_ref, dy_ref, acc_ref):
        e = y_ref[...] - t_ref[...]
        dy_ref[...] = e * (1.0 / D)
        sq = jnp.sum(e * e, axis=0, keepdims=True)
        part = sq[:, 0:128]
        for b in range(1, D // 128):
            part = part + sq[:, 128 * b:128 * b + 128]
        _acc_init(pl.program_id(0), acc_ref)
        acc_ref[...] += part

    return _pcall(body, "loss_head", (t // tm,), [y, target], [_row(tm, D), _row(tm, D)],
                  [_row(tm, D), pl.BlockSpec((1, 128), lambda i: (0, 0))],
                  [_sds((t, D), F32), _sds((1, 128), F32)], ("arbitrary",))


def _ffn_down_bwd(dx3, sv, p, l, depth, gb):
    t = dx3.shape[0]
    tm = min(256, t)

    def body(dx_ref, f_ref, g_ref, w_ref, a_ref, b_ref, df_ref, da_ref, db_ref, dg_ref):
        _acc_init(pl.program_id(0), dg_ref)
        df, dg = _rms_bwd(f_ref[...], g_ref[...], dx_ref[...])
        dg_ref[...] += dg
        df = df.astype(BF16)
        df_ref[...] = df
        for k in range(4):
            sl = slice(k * HP, (k + 1) * HP)
            ds = lax.dot_general(df, w_ref[k], NT, preferred_element_type=F32)
            av = a_ref[:, sl].astype(F32)
            sig = _sigmoid(av)
            da_ref[:, sl] = (ds * b_ref[:, sl].astype(F32) * (sig * (1.0 + av * (1.0 - sig)))).astype(BF16)
            db_ref[:, sl] = (ds * (av * sig)).astype(BF16)

    df, da, db, gb["ffn_post_g"] = _pcall(
        body, "ffn_down_bwd", (t // tm,), [dx3, sv["f"], p["ffn_post_g"], p["w_down"], sv["a"], sv["b"]],
        [_row(tm, D), _row(tm, D), _lyr(l, 1, D), _down_all(l), _row(tm, DFFP), _row(tm, DFFP)],
        [_row(tm, D), _row(tm, DFFP), _row(tm, DFFP), _lyr(l.g, 1, D)],
        [_sds((t, D), BF16), _sds((t, DFFP), BF16), _sds((t, DFFP), BF16), _sds((depth, 1, D), F32)], ("arbitrary",),
        prevs={3: gb.get("ffn_post_g")})
    return df, da, db


def _ffn_up_bwd(da, db, dx3, sv, p, l, depth, gb):
    t = dx3.shape[0]
    tm = min(256, t)

    def body(da_ref, db_ref, wg_ref, wu_ref, x_ref, dx3_ref, g_ref, dx2_ref, dg_ref):
        _acc_init(pl.program_id(0), dg_ref)
        dh = jnp.zeros((tm, D), F32)
        for k in range(4):
            sl = slice(k * HP, (k + 1) * HP)
            dh = dh + jnp.dot(da_ref[:, sl], wg_ref[k], preferred_element_type=F32)
            dh = dh + jnp.dot(db_ref[:, sl], wu_ref[k], preferred_element_type=F32)
        dx, dg = _rms_bwd(x_ref[...], g_ref[...], dh)
        dg_ref[...] += dg
        dx2_ref[...] = dx3_ref[...] + dx

    dx2, gb["ffn_pre_g"] = _pcall(
        body, "ffn_up_bwd", (t // tm,), [da, db, p["w_gu"], p["w_gu"], sv["x2"], dx3, p["ffn_pre_g"]],
        [_row(tm, DFFP), _row(tm, DFFP), _gu_all(l, 0), _gu_all(l, 1), _row(tm, D), _row(tm, D), _lyr(l, 1, D)],
        [_row(tm, D), _lyr(l.g, 1, D)], [_sds((t, D), F32), _sds((depth, 1, D), F32)], ("arbitrary",),
        prevs={1: gb.get("ffn_pre_g")})
    return dx2


def _out_proj_bwd(dx2, sv, p, l, depth, gb):
    t = dx2.shape[0]
    tm = min(ROW_TILE, t)

    def body(dx_ref, o_ref, g_ref, w_ref, do_ref, dmix_ref, dg_ref):
        _acc_init(pl.program_id(0), dg_ref)
        do, dg = _rms_bwd(o_ref[...], g_ref[...], dx_ref[...])
        dg_ref[...] += dg
        do = do.astype(BF16)
        do_ref[...] = do
        dmix_ref[...] = lax.dot_general(do, w_ref[...], NT, preferred_element_type=F32)

    do, dmix, gb["mix_post_g"] = _pcall(
        body, "out_proj_bwd", (t // tm,), [dx2, sv["o"], p["mix_post_g"], p["w_out"]],
        [_row(tm, D), _row(tm, D), _lyr(l, 1, D), _lyr(_wl(p["w_out"], l), D, D)], [_row(tm, D), _row(tm, D), _lyr(l.g, 1, D)],
        [_sds((t, D), BF16), _sds((t, D), F32), _sds((depth, 1, D), F32)], ("arbitrary",),
        prevs={2: gb.get("mix_post_g")})
    return do, dmix


def _mixer_bwd(dmix, sv, p, l, depth, gb):
    zb = sv["zb"]
    t = zb.shape[0]
    tm = min(ROW_TILE, t)
    hb = tm // 8
    last_blk = t // 8 - 1
    nsteps = t // tm

    def body(dmix_ref, ya_ref, yb_ref, yc_ref, zb_ref, zprev_ref, znext_ref, ycn_ref, dmn_ref,
             gln_ref, bln_ref, wsp_ref, bsp_ref, cw_ref, go_ref,
             dya_ref, dzb_ref, delta_ref, dgo_ref, dgln_ref, dbln_ref, dwsp_ref, dbsp_ref, dcw_ref):
        i = pl.program_id(0)
        _acc_init(i, dgo_ref, dgln_ref, dbln_ref, dwsp_ref, dbsp_ref, dcw_ref)
        go = go_ref[...]
        dmix = dmix_ref[...]

        ya = ya_ref[...]
        dya, dga = _rms_bwd(ya, go[:, 0:512], dmix[:, 0:512])
        dyb, dgb_ = _rms_bwd(yb_ref[...], go[:, 512:768], dmix[:, 512:768])
        dyc, dgc_ = _rms_bwd(yc_ref[...], go[:, 768:1024], dmix[:, 768:1024])
        dgo_ref[:, 0:512] += dga
        dgo_ref[:, 512:768] += dgb_
        dgo_ref[:, 768:1024] += dgc_
        dya = dya * LN2
        dya_ref[...] = dya.astype(BF16)
        prod = dya * ya
        hmasks = _group_masks((tm, 512))
        for h in range(HEADS):
            delta_ref[h] = jnp.sum(jnp.where(hmasks[h], prod, 0.0), axis=-1, keepdims=True)

        masks = _group_masks((tm, SGW))
        cmasks = _group_masks((CHUNK, SGW))
        tril = _tril_mask()
        wc_bf = [jnp.where(tril, wsp_ref[g], 0.0).astype(BF16) for g in range(4)]
        zu, zv = zb_ref[:, 0:256], zb_ref[:, 256:512]
        g_ln = gln_ref[...]
        u, _, xh, rs, vn, mixed = _sgu_forward(zu, zv, g_ln, bln_ref[...], wc_bf, bsp_ref[...], masks, cmasks)
        du = dyb * mixed
        dmixed = dyb * u
        dvn_chunks = []
        dbsp = jnp.zeros((CHUNK, SGW), F32)
        for ci in range(tm // CHUNK):
            rows = slice(ci * CHUNK, (ci + 1) * CHUNK)
            dm_c = dmixed[rows, :]
            vn_c = vn[rows, :].astype(BF16)
            dbsp = dbsp + dm_c
            dvn_c = jnp.zeros((CHUNK, SGW), F32)
            for g in range(4):
                dm_g = jnp.where(cmasks[g], dm_c, 0.0).astype(BF16)
                dw = lax.dot_general(dm_g, vn_c, NT, preferred_element_type=F32)
                dwsp_ref[g] += jnp.where(tril, dw, 0.0)
                dvn_c = dvn_c + lax.dot_general(wc_bf[g], dm_g, TN, preferred_element_type=F32)
            dvn_chunks.append(dvn_c)
        dbsp_ref[...] += dbsp
        dvn = jnp.concatenate(dvn_chunks, axis=0) if len(dvn_chunks) > 1 else dvn_chunks[0]
        dgln_ref[...] += jnp.sum(dvn * xh, axis=0, keepdims=True)
        dbln_ref[...] += jnp.sum(dvn, axis=0, keepdims=True)
        dxh = dvn * g_ln
        dvv = rs * (dxh - _group_mean(dxh, masks) - xh * _group_mean(dxh * xh, masks))
        dzb_ref[:, 0:256] = (du * _gelu_grad(zu)).astype(BF16)
        dzb_ref[:, 256:512] = (dvv * _gelu_grad(zv)).astype(BF16)

        cwv = cw_ref[...]
        gb_, gc, hh = zb_ref[:, 512:768], zb_ref[:, 768:1024], zb_ref[:, 1024:1280]
        yv, sh1, sh2, conv = _conv_forward(gc, hh, zprev_ref[:, 768:1024], zprev_ref[:, 1024:1280], i == 0, cwv)
        dconv = dyc * gb_
        dzb_ref[:, 512:768] = (dyc * conv).astype(BF16)
        dcw_ref[0:1, :] += jnp.sum(dconv * sh2, axis=0, keepdims=True)
        dcw_ref[1:2, :] += jnp.sum(dconv * sh1, axis=0, keepdims=True)
        dcw_ref[2:3, :] += jnp.sum(dconv * yv, axis=0, keepdims=True)
        dycn, _ = _rms_bwd(ycn_ref[...], go[:, 768:1024], dmn_ref[...])
        dconv_next = jnp.where(i == nsteps - 1, 0.0, dycn * znext_ref[:, 512:768])
        n0, n1 = _pick_row(dconv_next, 0), _pick_row(dconv_next, 1)
        dyv = dconv * cwv[2:3, :] + _shift_up(dconv, 1, [n0]) * cwv[1:2, :] + _shift_up(dconv, 2, [n0, n1]) * cwv[0:1, :]
        dzb_ref[:, 768:1024] = (dyv * hh).astype(BF16)
        dzb_ref[:, 1024:1280] = (dyv * gc).astype(BF16)

    prev_map = lambda i: (jnp.maximum(i * hb - 1, 0), 0)
    next_map = lambda i: (jnp.minimum((i + 1) * hb, last_blk), 0)
    names = ("out_norm_g", "sg_ln_g", "sg_ln_b", "w_sp", "b_sp_t", "conv_w")
    shapes = ((1, D), (1, SGW), (1, SGW), (4, CHUNK, CHUNK), (CHUNK, SGW), (3, CVW))
    outs = _pcall(
        body, "mixer_bwd", (nsteps,),
        [dmix, sv["ya"], sv["yb"], sv["yc"], zb, zb, zb, sv["yc"], dmix, p["sg_ln_g"], p["sg_ln_b"], p["w_sp"], p["b_sp"],
         p["conv_w"], p["out_norm_g"]],
        [_row(tm, D), _row(tm, 512), _row(tm, SGW), _row(tm, CVW), _row(tm, ZB),
         pl.BlockSpec((8, ZB), prev_map), pl.BlockSpec((8, ZB), next_map), pl.BlockSpec((8, CVW), next_map),
         pl.BlockSpec((8, 256), lambda i: (jnp.minimum((i + 1) * hb, last_blk), 3)),
         _lyr(l, 1, SGW), _lyr(l, 1, SGW), _lyr(l, 4, CHUNK, CHUNK), _lyr(l, CHUNK, SGW), _lyr(_wl(p["conv_w"], l), 3, CVW), _lyr(l, 1, D)],
        [_row(tm, 512), _row(tm, ZB), pl.BlockSpec((HEADS, tm, 1), lambda i: (0, i, 0))] + [_lyr(l.g, *s) for s in shapes],
        [_sds((t, 512), BF16), _sds((t, ZB), BF16), _sds((HEADS, t, 1), F32)] + [_sds((depth,) + s, F32) for s in shapes],
        ("arbitrary",), prevs={3 + n: gb.get(name) for n, name in enumerate(names)})
    for n, name in enumerate(names):
        gb[name] = outs[3 + n]
    return outs[0], outs[1], outs[2]


def _attn_bwd(qs, k, v, dya, lse, delta, ride=()):
    t = qs.shape[0]
    tq = _att_tile(t)
    nq = t // tq
    nb = len(ride)
    groups = HEADS // ATT_HEADS

    steps = [(j, i) for j in range(nq) for i in range(j, nq)]
    j_of = jnp.asarray([s[0] for s in steps], jnp.int32)
    i_of = jnp.asarray([s[1] for s in steps], jnp.int32)

    def body(j_ref, i_ref, q_ref, k_ref, v_ref, do_ref, lse_ref, dl_ref, *refs):
        dq_ref, dk_ref, dv_ref = refs[nb:nb + 3]
        dq_s, dk_s, dv_s = refs[2 * nb + 3:2 * nb + 6]
        step_no = pl.program_id(1)
        j, i = j_ref[step_no], i_ref[step_no]
        if ride:
            start, finish = _exchange_ops(refs[:nb], refs[nb + 3:2 * nb + 3], refs[2 * nb + 6:])
            pr = pl.program_id(0)
            pl.when((pr == 0) & (step_no == 0))(start)
            pl.when((pr == groups - 1) & (step_no == len(steps) - 1))(finish)

        @pl.when(step_no == 0)
        def _():
            dq_s[...] = jnp.zeros(dq_s.shape, F32)

        def step(masked):
            keep = _causal_keep(tq, 0, 0) if masked else None
            lane = lax.broadcasted_iota(jnp.int32, (tq, 128), 1)
            rows = pl.ds(pl.multiple_of(i * tq, tq), tq)
            for hh in range(ATT_HEADS):
                sl = slice(128 * hh, 128 * hh + 128)
                pair = slice(128 * (hh // 2), 128 * (hh // 2) + 128)
                vv, do = v_ref[:, pair], do_ref[:, pair]
                qq, kk = q_ref[:, sl], k_ref[:, sl]
                s = lax.dot_general(qq, kk, NT, preferred_element_type=F32)
                p = jnp.exp2(s - lse_ref[hh])
                if masked:
                    p = jnp.where(keep, p, 0.0)
                do_h = jnp.where((lane < VD) if hh % 2 == 0 else (lane >= VD), do, jnp.zeros_like(do))
                dp = lax.dot_general(do_h, vv, NT, preferred_element_type=F32)
                ds = (p * (dp - dl_ref[hh])).astype(BF16)
                dv_s[:, pair] += lax.dot_general(p.astype(BF16), do_h, TN, preferred_element_type=F32)
                dk_s[:, sl] += lax.dot_general(ds, qq, TN, preferred_element_type=F32)
                dq_s[rows, sl] += jnp.dot(ds, kk, preferred_element_type=F32)

        @pl.when(i == j)
        def _():
            dk_s[...] = jnp.zeros(dk_s.shape, F32)
            dv_s[...] = jnp.zeros(dv_s.shape, F32)
            step(True)

        @pl.when(i > j)
        def _():
            step(False)

        @pl.when(i == nq - 1)
        def _():
            dk_ref[...] = dk_s[...].astype(BF16)
            dv_ref[...] = (dv_s[...] * LOG2E).astype(BF16)

        @pl.when(step_no == len(steps) - 1)
        def _():
            dq_ref[...] = dq_s[...].astype(BF16)

    qw, vw = 128 * ATT_HEADS, VD * ATT_HEADS
    qrow = lambda p, s, jt, it: (it[s], p)
    krow = lambda p, s, jt, it: (jt[s], p)
    col_spec = pl.BlockSpec((ATT_HEADS, tq, 1), lambda p, s, jt, it: (p, it[s], 0))
    grid_spec = pltpu.PrefetchScalarGridSpec(
        num_scalar_prefetch=2, grid=(groups, len(steps)),
        in_specs=[pl.BlockSpec((tq, qw), qrow), pl.BlockSpec((tq, qw), krow), pl.BlockSpec((tq, vw), krow),
                  pl.BlockSpec((tq, vw), qrow), col_spec, col_spec] + [ANY] * nb,
        out_specs=[pl.BlockSpec((t, qw), lambda p, s, jt, it: (0, p)), pl.BlockSpec((tq, qw), krow),
                   pl.BlockSpec((tq, vw), krow)] + [ANY] * nb,
        scratch_shapes=[pltpu.VMEM((t, qw), F32), pltpu.VMEM((tq, qw), F32), pltpu.VMEM((tq, vw), F32)]
        + ([pltpu.SemaphoreType.DMA((nb, 3))] * 2 if ride else []))
    outs = pl.pallas_call(
        body, name="attn_bwd_ride" if ride else "attn_bwd", grid_spec=grid_spec,
        out_shape=[_sds((t, QW), BF16), _sds((t, QW), BF16), _sds((t, HEADS * VD), BF16)]
        + [_sds((3,) + a.shape[1:], a.dtype) for a in ride],
        compiler_params=_cp("arbitrary", "arbitrary"))(j_of, i_of, qs, k, v, dya, lse, delta, *ride)
    return outs[0], outs[1], outs[2], list(outs[3:])


def _mla_prep_bwd(dqs, dk, dv, sv, p, l, depth, gb, tabs):
    za = sv["za"]
    t = za.shape[0]
    tm = min(ROW_TILE, t)

    def body(dq_ref, dk_ref, dv_ref, z_ref, gq_ref, gkv_ref, wuq_ref, wukv_ref, c_ref, s1_ref, s2_ref,
             dza_ref, dqp_ref, dkv_ref, dgq_ref, dgkv_ref):
        _acc_init(pl.program_id(0), dgq_ref, dgkv_ref)
        c, s1, s2 = c_ref[...], s1_ref[...], s2_ref[...]
        lane = lax.broadcasted_iota(jnp.int32, (tm, 128), 1)
        rope_lanes = (lane >= NOPE) & (lane < NOPE + ROPE)
        dkr = jnp.zeros((tm, 128), F32)
        for h in range(HEADS):
            sl = slice(128 * h, 128 * h + 128)
            dqp_ref[:, sl] = _rope_bwd(dq_ref[:, sl].astype(F32) * QSCALE, c, s1, s2).astype(BF16)
            dkh = dk_ref[:, sl]
            dkv_ref[:, sl] = dkh
            dkr = dkr + jnp.where(rope_lanes, dkh.astype(F32), 0.0)
        dkv_ref[:, QW:] = dv_ref[...]
        z = z_ref[...]
        dcq = lax.dot_general(dqp_ref[...], wuq_ref[...], NT, preferred_element_type=F32)
        dzq, dgq = _rms_bwd(z[:, :QR], gq_ref[...], dcq)
        dckv = lax.dot_general(dkv_ref[...], wukv_ref[...], NT, preferred_element_type=F32)
        dzkv, dgkv = _rms_bwd(z[:, QR:QR + KVR], gkv_ref[...], dckv)
        dgq_ref[...] += dgq
        dgkv_ref[...] += dgkv
        dza_ref[:, :QR] = dzq.astype(BF16)
        dza_ref[:, QR:QR + KVR] = dzkv.astype(BF16)
        dza_ref[:, QR + KVR:] = _rope_bwd(dkr, c, s1, s2).astype(BF16)

    dza, dqp, dkv, gb["q_norm_g"], gb["kv_norm_g"] = _pcall(
        body, "mla_prep_bwd", (t // tm,),
        [dqs, dk, dv, za, p["q_norm_g"], p["kv_norm_g"], p["w_uq"], p["w_ukv"], *tabs],
        [_row(tm, QW), _row(tm, QW), _row(tm, HEADS * VD), _row(tm, ZA), _lyr(l, 1, QR), _lyr(l, 1, KVR),
         _lyr(_wl(p["w_uq"], l), QR, QW), _lyr(_wl(p["w_ukv"], l), KVR, KVW), _row(tm, 128), _row(tm, 128), _row(tm, 128)],
        [_row(tm, ZA), _row(tm, QW), _row(tm, KVW), _lyr(l.g, 1, QR), _lyr(l.g, 1, KVR)],
        [_sds((t, ZA), BF16), _sds((t, QW), BF16), _sds((t, KVW), BF16), _sds((depth, 1, QR), F32),
         _sds((depth, 1, KVR), F32)], ("arbitrary",), prevs={3: gb.get("q_norm_g"), 4: gb.get("kv_norm_g")})
    return dza, dqp, dkv


def _in_proj_bwd(dza, dzb, dx2, sv, p, l, depth, gb):
    t = dx2.shape[0]
    tm = min(ROW_TILE, t)

    def body(dza_ref, dzb_ref, wa_ref, wb_ref, x_ref, dx2_ref, g_ref, dx_ref, dg_ref):
        _acc_init(pl.program_id(0), dg_ref)
        dh = (jnp.dot(dza_ref[...], wa_ref[...], preferred_element_type=F32)
              + jnp.dot(dzb_ref[...], wb_ref[...], preferred_element_type=F32))
        dx, dg = _rms_bwd(x_ref[...], g_ref[...], dh)
        dg_ref[...] += dg
        dx_ref[...] = dx2_ref[...] + dx

    dx, gb["mix_pre_g"] = _pcall(
        body, "in_proj_bwd", (t // tm,), [dza, dzb, p["w_in_a"], p["w_in_b"], sv["x"], dx2, p["mix_pre_g"]],
        [_row(tm, ZA), _row(tm, ZB), _lyr(_wl(p["w_in_a"], l), ZA, D), _lyr(_wl(p["w_in_b"], l), ZB, D), _row(tm, D), _row(tm, D), _lyr(l, 1, D)],
        [_row(tm, D), _lyr(l.g, 1, D)], [_sds((t, D), F32), _sds((depth, 1, D), F32)], ("arbitrary",),
        prevs={1: gb.get("mix_pre_g")})
    return dx


def _mm_tn(a, b, tn, name, l, depth, gb):
    t, k = a.shape
    n = b.shape[1]
    tt = min(ROW_TILE, t)

    def body(a_ref, b_ref, o_ref):
        _acc_init(pl.program_id(1), o_ref)
        o_ref[...] += lax.dot_general(a_ref[...], b_ref[...], TN, preferred_element_type=F32)

    gb[name] = _pcall(
        body, "d" + name, (n // tn, t // tt), [a, b],
        [pl.BlockSpec((tt, k), lambda j, s: (s, 0)), pl.BlockSpec((tt, tn), lambda j, s: (s, j))],
        pl.BlockSpec((None, k, tn), lambda j, s: (l.g, 0, j)), _sds((depth, k, n), F32), ("parallel", "arbitrary"),
        prevs={0: gb.get(name)})


def _dw_ffn(a, b, kind, l, depth, gb):
    t = a.shape[0]
    tt = min(ROW_TILE, t)
    nsteps = t // tt

    def body(a_ref, b_ref, o_ref, acc):
        s = pl.program_id(0)
        _acc_init(s, acc)
        acc[...] += lax.dot_general(a_ref[...], b_ref[...], TN, preferred_element_type=F32)

        @pl.when(s == nsteps - 1)
        def _():
            for k in range(4):
                o_ref[k] = acc[k * HP:(k + 1) * HP, :].astype(BF16)

    rows = lambda n: pl.BlockSpec((tt, n), lambda s: (s, 0))
    if kind == "down":
        name = "down"
        out_spec = pl.BlockSpec((4, None, HP, D), lambda s: (0, l.g, 0, 0))
        out_shape = _sds((4, depth, HP, D), BF16)
    else:
        which = 0 if kind == "gate" else 1
        name = "gu"
        out_spec = pl.BlockSpec((4, None, None, HP, D), lambda s: (0, l.g, which, 0, 0))
        out_shape = _sds((4, depth, 2, HP, D), BF16)
    gb[name] = _pcall(body, "dw_" + kind, (nsteps,), [a, b], [rows(DFFP), rows(D)], out_spec, out_shape, ("arbitrary",),
                      scratch=[pltpu.VMEM((DFFP, D), F32)], prevs={0: gb.get(name)})


def _ffn_views(bufs):
    return {"w_gu": bufs[1], "w_down": bufs[2].reshape(bufs[2].shape[:2] + (HP, D))}


def _layer_fwd(x, p, l, tabs, fetch):
    h1, za, zb = _in_proj(x, p, l)
    cqn, ckvn, qs, k, v = _mla_prep(za, p, l, tabs)
    ya, lse, bufs = _attn_fwd(qs, k, v, fetch)
    if fetch:
        p = {**p, **_ffn_views(bufs)}
    mix, yb, yc = _mixer_fwd(zb, ya, p, l)
    o, x2, h2 = _out_proj(mix, x, p, l)
    a, b, s = _ffn_up(h2, p, l)
    f, x3 = _ffn_down(s, x2, p, l)
    saved = dict(x=x, h1=h1, za=za, zb=zb, cqn=cqn, ckvn=ckvn, qs=qs, k=k, v=v, ya=ya, lse=lse, mix=mix, yb=yb, yc=yc,
                 o=o, x2=x2, h2=h2, a=a, b=b, s=s, f=f)
    return x3, saved, bufs if fetch else None


class _Layer(int):
    def __new__(cls, l, g):
        obj = int.__new__(cls, l)
        obj.g = g
        return obj


def _layer_bwd(dx3, p, sv, l, depth, gb, tabs, ride=(), ffn_front=None):
    df, da, db = _ffn_down_bwd(dx3, sv, p, l, depth, gb)
    _dw_ffn(sv["s"], df, "down", l, depth, gb)
    dx2 = _ffn_up_bwd(da, db, dx3, sv, p, l, depth, gb)
    _dw_ffn(da, sv["h2"], "gate", l, depth, gb)
    _dw_ffn(db, sv["h2"], "up", l, depth, gb)
    ffn_blocks = list(ffn_front(gb["gu"], gb["down"])) if ffn_front else []
    do, dmix = _out_proj_bwd(dx2, sv, p, l, depth, gb)
    _mm_tn(sv["mix"], do, D, "w_out", l, depth, gb)
    dya, dzb, delta = _mixer_bwd(dmix, sv, p, l, depth, gb)
    dqs, dk, dv, sent = _attn_bwd(sv["qs"], sv["k"], sv["v"], dya, sv["lse"], delta, tuple(ffn_blocks) + tuple(ride))
    dza, dqp, dkv = _mla_prep_bwd(dqs, dk, dv, sv, p, l, depth, gb, tabs)
    _mm_tn(sv["cqn"], dqp, QW, "w_uq", l, depth, gb)
    _mm_tn(sv["ckvn"], dkv, KVW, "w_ukv", l, depth, gb)
    _mm_tn(dza, sv["h1"], D, "w_in_a", l, depth, gb)
    _mm_tn(dzb, sv["h1"], D, "w_in_b", l, depth, gb)
    nf = len(ffn_blocks)
    return _in_proj_bwd(dza, dzb, dx2, sv, p, l, depth, gb), ffn_blocks, sent[:nf], sent[nf:]


def _rope_tables(positions):
    inv_freq = 1.0 / (ROPE_THETA ** (jnp.arange(0, ROPE // 2, dtype=F32) / (ROPE // 2)))
    ang = positions.astype(F32)[:, None] * inv_freq
    cos, sin = jnp.cos(ang), jnp.sin(ang)
    t = positions.shape[0]
    one, zero = jnp.ones((t, 64), F32), jnp.zeros((t, 16), F32)
    c = jnp.concatenate([one, cos, cos, one[:, :32]], axis=1)
    s1 = jnp.concatenate([zero, zero, zero, zero, -sin, zero, zero, zero], axis=1)
    s2 = jnp.concatenate([zero, zero, zero, zero, zero, sin, zero, zero], axis=1)
    return c, s1, s2


def _mixer_weight_params(full):
    w_in = full["w_in_t"]
    depth = w_in.shape[0]
    zpad = lambda n: jnp.zeros((depth, n, D), w_in.dtype)
    kv = full["w_ukv"].reshape(depth, KVR, HEADS, NOPE + VD)
    return {
        "w_in_a": jnp.concatenate([w_in[:, :640], zpad(64), w_in[:, 640:672], zpad(32)], axis=1),
        "w_in_b": w_in[:, 672:],
        "w_uq": jnp.pad(full["w_uq"].reshape(depth, QR, HEADS, NOPE + ROPE),
                        ((0, 0), (0, 0), (0, 0), (0, 32))).reshape(depth, QR, QW),
        "w_ukv": jnp.concatenate([jnp.pad(kv[..., :NOPE], ((0, 0), (0, 0), (0, 0), (0, 64))).reshape(depth, KVR, QW),
                                  kv[..., NOPE:].reshape(depth, KVR, HEADS * VD)], axis=2),
        "w_out": full["w_out"], "conv_w": full["conv_w"],
    }


def _small_params(w):
    p = {"w_sp": w["w_sp"], "b_sp": jnp.repeat(jnp.swapaxes(w["b_sp"], 1, 2), 64, axis=2)}
    for n in ("mix_pre_g", "mix_post_g", "ffn_pre_g", "ffn_post_g", "q_norm_g", "kv_norm_g", "sg_ln_g", "sg_ln_b",
              "out_norm_g"):
        p[n] = w[n][:, None, :]
    return p


def _natural_grads(gb):
    depth = gb["w_in_a"].shape[0]
    ga, kv = gb["w_in_a"], gb["w_ukv"]
    out = {
        "w_in_t": jnp.concatenate([ga[:, :640], ga[:, 704:736], gb["w_in_b"]], axis=1),
        "w_uq": gb["w_uq"].reshape(depth, QR, HEADS, 128)[..., :NOPE + ROPE].reshape(depth, QR, HEADS * (NOPE + ROPE)),
        "w_ukv": jnp.concatenate([kv[:, :, :QW].reshape(depth, KVR, HEADS, 128)[..., :NOPE],
                                  kv[:, :, QW:].reshape(depth, KVR, HEADS, VD)], axis=3).reshape(depth, KVR, -1),
        "b_sp": jnp.swapaxes(gb["b_sp_t"].reshape(depth, CHUNK, 4, 64).sum(axis=-1), 1, 2),
    }
    for n in ("w_out", "w_sp", "conv_w"):
        out[n] = gb[n]
    for n in ("mix_pre_g", "mix_post_g", "ffn_pre_g", "ffn_post_g", "q_norm_g", "kv_norm_g", "sg_ln_g", "sg_ln_b",
              "out_norm_g"):
        out[n] = gb[n][:, 0, :]
    return out


def _local_step(x, positions, target, small, mine, bufs, shard_shapes, fetch=True, front=None):
    depth = small["w_sp"].shape[0]
    tabs = _rope_tables(positions)
    ps = _small_params(small)
    saved, mixer_w = [], []
    for l in range(depth):
        mixer_w.append(_mixer_weight_params(_unpack_weights(bufs[0], l, shard_shapes)))
        p = {**ps, **mixer_w[l], **_ffn_views(bufs)}
        layers = [l + 1 if l + 1 < depth else None, l, l]
        x, sv, fetched = _layer_fwd(x, p, l, tabs, (mine, bufs, layers) if fetch else None)
        bufs = fetched or bufs
        saved.append(sv)
    dx, acc = _loss_head(x, target)
    loss = (0.5 / D) * jnp.sum(acc)
    gbs = [{} for _ in range(depth)]
    out = [{} for _ in range(depth)]
    ride = ()
    for l in reversed(range(depth)):
        ffn_front = (lambda gu, down, l=l: front[0](l, gu, down)) if front else None
        dx, out[l]["ffn"], out[l]["sent_ffn"], got = _layer_bwd(
            dx, {**ps, **mixer_w[l], **_ffn_views(bufs)}, saved[l], _Layer(l, 0), 1, gbs[l], tabs, ride, ffn_front)
        if ride:
            out[l + 1]["sent_mixer"], ride = got, ()
        grads = _natural_grads(gbs[l])
        if front:
            out[l]["mixer"] = front[1](l, grads)
            if l > 0:
                ride = tuple(out[l]["mixer"])
        else:
            out[l]["grads"] = (grads, gbs[l]["gu"], gbs[l]["down"])
    return loss, dx, out


def _place():
    x, y, c = lax.axis_index("x"), lax.axis_index("y"), lax.axis_index("c")
    chips = [(1 - x, y), (x, 1 - y), (1 - x, 1 - y)]
    return x, y, c, 2 * x + y, chips


def _remote(src, dst, send_sem, recv_sem, to):
    return pltpu.make_async_remote_copy(src_ref=src, dst_ref=dst, send_sem=send_sem, recv_sem=recv_sem, device_id=to,
                                        device_id_type=MESH_ID)


def _gather_ops(mine_refs, out_refs, sems, layers):
    send_sems, recv_sems, fsend_sems, frecv_sems = sems
    x, y, c, k, chips = _place()
    sib = (x, y, 1 - c)
    pairs = [(b, n) for n in range(3) for b in range(len(mine_refs)) if layers[b] is not None]

    def slot(n):
        return 2 * chips[n][0] + chips[n][1]

    def ici(b, n, dst_chip):
        return _remote(mine_refs[b].at[layers[b], c], out_refs[b].at[dst_chip, layers[b], c], send_sems.at[b, n],
                       recv_sems.at[b, n], (*chips[n], c))

    def d2d(b, n, half):
        piece = out_refs[b].at[slot(n), layers[b], half]
        return _remote(piece, piece, fsend_sems.at[b, n], frecv_sems.at[b, n], sib)

    def start():
        for b, n in pairs:
            ici(b, n, k).start()

    def hand_over():
        for b, n in pairs:
            ici(b, n, slot(n)).wait_recv()
            d2d(b, n, c).start()

    def drain():
        for b, n in pairs:
            d2d(b, n, 1 - c).wait_recv()
        for b, n in pairs:
            ici(b, n, k).wait_send()
            d2d(b, n, c).wait_send()

    return start, hand_over, drain


def _gather_first_layer(mine):
    nb = len(mine)

    def body(*refs):
        start, hand_over, drain = _gather_ops(refs[:nb], refs[nb:2 * nb], refs[2 * nb:], [0] + [None] * (nb - 1))
        start()
        hand_over()
        drain()

    return pl.pallas_call(
        body, name="gather_first_layer", in_specs=[ANY] * nb, out_specs=[ANY] * nb,
        out_shape=[_sds((4,) + a.shape, a.dtype) for a in mine],
        scratch_shapes=[pltpu.SemaphoreType.DMA((nb, 3))] * 4)(*mine)


def _swap_halves(bigs, wholes=()):
    nb, n = len(bigs), len(bigs) + len(wholes)

    def body(*refs):
        src, dst = refs[:n], refs[n:2 * n]
        send_sems, recv_sems = refs[2 * n:]
        x, y, c, _, _ = _place()
        sib = (x, y, 1 - c)
        cps = [_remote(src[b].at[:, 1 - c] if b < nb else src[b], dst[b], send_sems.at[b], recv_sems.at[b], sib)
               for b in range(n)]
        for cp in cps:
            cp.start()
        for cp in cps:
            cp.wait()

    return pl.pallas_call(
        body, name="swap_halves", in_specs=[ANY] * n, out_specs=[ANY] * n,
        out_shape=[_sds((4,) + a.shape[2:], a.dtype) for a in bigs] + [_sds(a.shape, a.dtype) for a in wholes],
        scratch_shapes=[pltpu.SemaphoreType.DMA((n,))] * 2)(*bigs, *wholes)


def _sum_tile(r):
    return max(cand for cand in range(16, 641, 16) if r % cand == 0)


def _pair_sum(big, rbig, c):
    _, _, r, w = big.shape
    tr = _sum_tile(r)

    def body(c_ref, big_ref, rbig_ref, p_ref):
        p_ref[...] = (big_ref[...].astype(F32) + rbig_ref[...].astype(F32)).astype(BF16)

    grid_spec = pltpu.PrefetchScalarGridSpec(
        num_scalar_prefetch=1, grid=(4, r // tr),
        in_specs=[pl.BlockSpec((None, None, tr, w), lambda j, i, cr: (j, cr[0], i, 0)),
                  pl.BlockSpec((None, tr, w), lambda j, i, cr: (j, i, 0))],
        out_specs=pl.BlockSpec((None, tr, w), lambda j, i, cr: (j, i, 0)))
    return pl.pallas_call(body, name="pair_sum", grid_spec=grid_spec, out_shape=_sds((4, r, w), BF16),
                          compiler_params=_cp("parallel", "parallel"))(c, big, rbig)


def _small_sum(parts):
    n, ns, _ = parts.shape

    def body(p_ref, o_ref):
        s = p_ref[0]
        for j in range(1, n):
            s = s + p_ref[j]
        o_ref[...] = s

    return pl.pallas_call(body, name="small_sum", out_shape=_sds((ns, 128), F32))(parts)


def _exchange_ops(p_refs, rb_refs, sems, small=None):
    send_sems, recv_sems = sems[0], sems[1]
    nb = len(p_refs)
    x, y, c, k, chips = _place()

    def copies(landing):
        out = []
        for n, (cx, cy) in enumerate(chips):
            to, kj = (cx, cy, c), 2 * cx + cy
            for b in range(nb):
                out.append(_remote(p_refs[b].at[k if landing else kj], rb_refs[b].at[n], send_sems.at[b, n],
                                   recv_sems.at[b, n], to))
            if small:
                out.append(_remote(small[0], small[1].at[kj if landing else k], send_sems.at[nb, n], recv_sems.at[nb, n], to))
        return out

    def local():
        return pltpu.make_async_copy(small[0], small[1].at[k], sems[2])

    def start():
        if small:
            local().start()
        for cp in copies(False):
            cp.start()

    def finish():
        for cp in copies(True):
            cp.wait_recv()
        for cp in copies(False):
            cp.wait_send()
        if small:
            local().wait()

    return start, finish


def _chip_exchange(ps, small):
    nb = len(ps)
    ns = small.shape[0]

    def body(*refs):
        start, finish = _exchange_ops(refs[:nb], refs[nb + 1:2 * nb + 1], refs[2 * nb + 2:], (refs[nb], refs[2 * nb + 1]))
        start()
        finish()

    return pl.pallas_call(
        body, name="chip_exchange", in_specs=[ANY] * (nb + 1), out_specs=[ANY] * (nb + 1),
        out_shape=[_sds((3,) + a.shape[1:], a.dtype) for a in ps] + [_sds((4, ns, 128), small.dtype)],
        scratch_shapes=[pltpu.SemaphoreType.DMA((nb + 1, 3))] * 2 + [pltpu.SemaphoreType.DMA(())])(*ps, small)


def _chip_sum(p, rb, chip):
    _, r, w = p.shape
    tr = _sum_tile(r)

    def body(k_ref, p_ref, rb_ref, o_ref):
        acc = p_ref[...].astype(F32)
        for j in range(3):
            acc = acc + rb_ref[j].astype(F32)
        o_ref[...] = acc

    grid_spec = pltpu.PrefetchScalarGridSpec(
        num_scalar_prefetch=1, grid=(r // tr,),
        in_specs=[pl.BlockSpec((None, tr, w), lambda i, kr: (kr[0], i, 0)), pl.BlockSpec((3, tr, w), lambda i, kr: (0, i, 0))],
        out_specs=pl.BlockSpec((tr, w), lambda i, kr: (i, 0)))
    return pl.pallas_call(body, name="chip_sum", grid_spec=grid_spec, out_shape=_sds((r, w), F32),
                          compiler_params=_cp("parallel"))(chip, p, rb)


def _send_to_sibling(reds):
    nb = len(reds)

    def body(*refs):
        red_refs, out_refs = refs[:nb], refs[nb:2 * nb]
        send_sems, recv_sems = refs[2 * nb:]
        x, y, c, _, _ = _place()
        cps = [_remote(red_refs[b], out_refs[b], send_sems.at[b], recv_sems.at[b], (x, y, 1 - c)) for b in range(nb)]
        for cp in cps:
            cp.start()
        for cp in cps:
            cp.wait()

    return pl.pallas_call(
        body, name="send_to_sibling", in_specs=[ANY] * nb, out_specs=[ANY] * nb,
        out_shape=[_sds(a.shape, a.dtype) for a in reds], scratch_shapes=[pltpu.SemaphoreType.DMA((nb,))] * 2)(*reds)


def _adam_math(w, g, m, v):
    nm = ADAM_B1 * m + (1.0 - ADAM_B1) * g
    nv = ADAM_B2 * v + (1.0 - ADAM_B2) * (g * g)
    m_hat = nm / (1.0 - ADAM_B1 ** ADAM_STEP)
    v_hat = nv / (1.0 - ADAM_B2 ** ADAM_STEP)
    return -ADAM_LR * (m_hat / (jnp.sqrt(v_hat) + ADAM_EPS) + ADAM_WD * w), nm, nv


def _adamw_shard(w, m, v, srcs, c, name, owner=0, split=None):
    depth, r, n = w.shape
    unit = math.gcd(split, r - split) if split else r
    tr = max(cand for cand in range(8, min(unit, 256) + 1, 8) if unit % cand == 0)
    sb = split // tr if split else None
    npad = srcs[0][0].shape[-1]

    def body(c_ref, w_ref, m_ref, v_ref, *refs):
        g_ref, d_ref, nm_ref, nv_ref = refs[2 * depth:]
        l, i = pl.program_id(0), pl.program_id(1)
        half = (i >= sb).astype(jnp.int32) if split else owner
        mine = c_ref[0] == half
        g = jnp.where(mine, refs[0][...], refs[1][...])
        for b in range(1, depth):
            g = jnp.where(l == b, jnp.where(mine, refs[2 * b][...], refs[2 * b + 1][...]), g)
        g = g[:, :n]
        g_ref[...] = g
        d_ref[...], nm_ref[...], nv_ref[...] = _adam_math(w_ref[...], g, m_ref[...], v_ref[...])

    def source(b):
        def index(l, i, cr):
            blk = jnp.where(i >= sb, i - sb, i) if split else i
            return (jnp.where(l == b, blk, 0), 0)
        return pl.BlockSpec((tr, npad), index)

    blk = pl.BlockSpec((None, tr, n), lambda l, i, cr: (l, i, 0))
    grid_spec = pltpu.PrefetchScalarGridSpec(
        num_scalar_prefetch=1, grid=(depth, r // tr),
        in_specs=[blk] * 3 + [source(b) for b in range(depth) for _ in range(2)], out_specs=[blk] * 4)
    return pl.pallas_call(body, name=name, grid_spec=grid_spec, out_shape=[_sds(w.shape, F32)] * 4,
                          compiler_params=_cp("parallel", "parallel"))(c, w, m, v, *[a for pair in srcs for a in pair])


def _pad_ffn_shards(w_gate, w_up, w_down):
    depth = w_gate.shape[0]
    hr = HP // 2

    def gu_body(g_ref, u_ref, o_ref):
        for which, ref in enumerate((g_ref, u_ref)):
            o_ref[which, 0:HS, :] = ref[...].astype(BF16)
            o_ref[which, HS:HP, :] = jnp.zeros((HP - HS, D), BF16)

    blk = pl.BlockSpec((None, HS, D), lambda l: (l, 0, 0))
    gu = pl.pallas_call(
        gu_body, name="pad_gate_up", grid=(depth,), in_specs=[blk, blk],
        out_specs=pl.BlockSpec((None, 2, HP, D), lambda l: (l, 0, 0, 0)),
        out_shape=_sds((depth, 2, HP, D), BF16), compiler_params=_cp("parallel"))(w_gate, w_up)

    def down_body(w_ref, o_ref):
        o_ref[0] = w_ref[0:hr, :].astype(BF16)
        o_ref[1, 0:HS - hr, :] = w_ref[hr:HS, :].astype(BF16)
        o_ref[1, HS - hr:hr, :] = jnp.zeros((HP - HS, D), BF16)

    down = pl.pallas_call(
        down_body, name="pad_down", grid=(depth,), in_specs=[pl.BlockSpec((None, HS, D), lambda l: (l, 0, 0))],
        out_specs=pl.BlockSpec((None, 2, hr, D), lambda l: (l, 0, 0, 0)),
        out_shape=_sds((depth, 2, hr, D), BF16), compiler_params=_cp("parallel"))(w_down)
    return gu, down


def _adamw_rows(w, g, m, v, name):
    r, n = w.shape
    tr = max(cand for cand in range(8, 513, 8) if r % cand == 0)

    def body(w_ref, g_ref, m_ref, v_ref, d_ref, nm_ref, nv_ref):
        d_ref[...], nm_ref[...], nv_ref[...] = _adam_math(w_ref[...], g_ref[...], m_ref[...], v_ref[...])

    blk = pl.BlockSpec((tr, n), lambda i: (i, 0))
    return pl.pallas_call(body, name=name, grid=(r // tr,), in_specs=[blk] * 4, out_specs=[blk] * 3,
                          out_shape=[_sds(w.shape, F32)] * 3, compiler_params=_cp("parallel"))(w, g, m, v)


def _to_pack(a, name):
    depth = a.shape[0]
    if name in ROW_SHARDED:
        return jnp.swapaxes(a.reshape(depth, 4, -1, D), 0, 1)
    return jnp.transpose(a.reshape(depth, a.shape[1], 4, a.shape[2] // 4), (2, 0, 1, 3)).reshape(4, depth, -1, D)


def _pack_rows(parts, lead, dtype, tail=None):
    pieces, at = [], 0
    for n, off, rows in PACK:
        if off > at:
            pieces.append(jnp.zeros(lead + (off - at, D), dtype))
        pieces.append(parts[n].astype(dtype))
        at = off + rows
    if tail is not None:
        pieces.append(tail)
        at += tail.shape[-2]
    pieces.append(jnp.zeros(lead + (PACK_ROWS - at, D), dtype))
    return jnp.concatenate(pieces, axis=len(lead))


def _pack_weight_shards(sh):
    depth = sh["w_in"].shape[0]
    parts = {n: sh[n].reshape(depth, rows, D) for n, _, rows in PACK}
    parts["w_in"] = jnp.swapaxes(sh["w_in"], 1, 2)
    conv = lax.bitcast_convert_type(sh["conv_w"].reshape(depth, 3 * 64), BF16).reshape(depth, 1, 384)
    flat = _pack_rows(parts, (depth,), BF16, tail=jnp.pad(conv, ((0, 0), (0, 0), (0, D - 384))))
    return flat.reshape(depth, 2, PACK_ROWS // 2, D)


def _unpack_weights(gathered, l, shard_shapes):
    depth = 1
    flat = gathered[:, l].reshape(4, 1, PACK_ROWS, D)
    full = {}
    for n, off, rows in PACK:
        if n == "w_in":
            full["w_in_t"] = jnp.swapaxes(flat[:, :, off:off + rows, :], 0, 1).reshape(depth, 4 * rows, D)
            continue
        shp = shard_shapes[n][1:]
        piece = flat[:, :, off:off + rows, :].reshape((4, depth) + shp)
        if n in ROW_SHARDED:
            full[n] = jnp.transpose(piece, (1, 0, 2, 3)).reshape(depth, 4 * shp[0], shp[1])
        else:
            full[n] = jnp.transpose(piece, (1, 2, 0, 3)).reshape(depth, shp[0], 4 * shp[1])
    conv = lax.bitcast_convert_type(flat[:, :, CONV_ROW, :384].reshape(4, depth, 192, 2), F32)
    full["conv_w"] = jnp.transpose(conv.reshape(4, depth, 3, 64), (1, 2, 0, 3)).reshape(depth, 3, CVW)
    return full


def _pack_grad_shards(g):
    depth = g["w_in_t"].shape[0]
    parts = {n: _to_pack(g[n], n) for n, _, _ in PACK if n != "w_in"}
    parts["w_in"] = jnp.swapaxes(g["w_in_t"].reshape(depth, 4, -1, D), 0, 1)
    return _pack_rows(parts, (4, depth), BF16)


def _pack_small(arrs, names_shapes, depth):
    flat = jnp.concatenate([arrs[n].reshape(depth, -1) for n, _ in names_shapes], axis=1).reshape(-1)
    rows = -(-flat.shape[0] // 1024) * 8
    return jnp.pad(flat, (0, rows * 128 - flat.shape[0])).reshape(rows, 128)


def _unpack_small(packed, names_shapes, depth):
    per_layer = sum(math.prod(s) for _, s in names_shapes)
    flat = packed.reshape(-1)[:depth * per_layer].reshape(depth, per_layer)
    out, off = {}, 0
    for n, s in names_shapes:
        size = math.prod(s)
        out[n] = flat[:, off:off + size].reshape((depth,) + s)
        off += size
    return out


def kernel(x, positions, mix_pre_g, mix_post_g, ffn_pre_g, ffn_post_g, w_in, q_norm_g, w_uq, kv_norm_g, w_ukv, sg_ln_g, sg_ln_b, w_sp, b_sp, conv_w, out_norm_g, w_out, w_gate, w_up, w_down, loss_target, m_mix_pre_g, m_mix_post_g, m_ffn_pre_g, m_ffn_post_g, m_w_in, m_q_norm_g, m_w_uq, m_kv_norm_g, m_w_ukv, m_sg_ln_g, m_sg_ln_b, m_w_sp, m_b_sp, m_conv_w, m_out_norm_g, m_w_out, m_w_gate, m_w_up, m_w_down, v_mix_pre_g, v_mix_post_g, v_ffn_pre_g, v_ffn_post_g, v_w_in, v_q_norm_g, v_w_uq, v_kv_norm_g, v_w_ukv, v_sg_ln_g, v_sg_ln_b, v_w_sp, v_b_sp, v_conv_w, v_out_norm_g, v_w_out, v_w_gate, v_w_up, v_w_down):
    w = dict(mix_pre_g=mix_pre_g, mix_post_g=mix_post_g, ffn_pre_g=ffn_pre_g, ffn_post_g=ffn_post_g, w_in=w_in,
             q_norm_g=q_norm_g, w_uq=w_uq, kv_norm_g=kv_norm_g, w_ukv=w_ukv, sg_ln_g=sg_ln_g, sg_ln_b=sg_ln_b, w_sp=w_sp,
             b_sp=b_sp, conv_w=conv_w, out_norm_g=out_norm_g, w_out=w_out, w_gate=w_gate, w_up=w_up, w_down=w_down)
    m = dict(mix_pre_g=m_mix_pre_g, mix_post_g=m_mix_post_g, ffn_pre_g=m_ffn_pre_g, ffn_post_g=m_ffn_post_g, w_in=m_w_in,
             q_norm_g=m_q_norm_g, w_uq=m_w_uq, kv_norm_g=m_kv_norm_g, w_ukv=m_w_ukv, sg_ln_g=m_sg_ln_g, sg_ln_b=m_sg_ln_b,
             w_sp=m_w_sp, b_sp=m_b_sp, conv_w=m_conv_w, out_norm_g=m_out_norm_g, w_out=m_w_out, w_gate=m_w_gate,
             w_up=m_w_up, w_down=m_w_down)
    v = dict(mix_pre_g=v_mix_pre_g, mix_post_g=v_mix_post_g, ffn_pre_g=v_ffn_pre_g, ffn_post_g=v_ffn_post_g, w_in=v_w_in,
             q_norm_g=v_q_norm_g, w_uq=v_w_uq, kv_norm_g=v_kv_norm_g, w_ukv=v_w_ukv, sg_ln_g=v_sg_ln_g, sg_ln_b=v_sg_ln_b,
             w_sp=v_w_sp, b_sp=v_b_sp, conv_w=v_conv_w, out_norm_g=v_out_norm_g, w_out=v_w_out, w_gate=v_w_gate,
             w_up=v_w_up, w_down=v_w_down)
    depth = w_in.shape[0]
    c = lax.axis_index("c").astype(jnp.int32).reshape(1)
    chip = (2 * lax.axis_index("x") + lax.axis_index("y")).astype(jnp.int32)

    mine = [_pack_weight_shards(w), *_pad_ffn_shards(jnp.swapaxes(w_gate, 1, 2), jnp.swapaxes(w_up, 1, 2), w_down)]
    bufs = [lax.dynamic_update_slice(g, a[None], (chip,) + (0,) * a.ndim)
            for g, a in zip(_gather_first_layer(mine), mine)]

    small_grads = [None] * depth
    small_pair = []

    def ffn_front(l, g_gu, g_down):
        bigs = [g_gu.reshape(4, 2, HP, D), g_down.reshape(4, 2, HP // 2, D)]
        return [_pair_sum(a, r, c) for a, r in zip(bigs, _swap_halves(bigs))]

    def mixer_front(l, grads):
        small_grads[l] = grads
        bigs = [_pack_grad_shards(grads).reshape(4, 2, PACK_ROWS // 2, D)]
        if l > 0:
            rbigs = _swap_halves(bigs)
        else:
            small = _pack_small({n: jnp.concatenate([g[n] for g in small_grads]) for n, _ in SMALL}, SMALL, depth)
            *rbigs, rsmall = _swap_halves(bigs, [small])
            small_pair.append(_small_sum(jnp.stack([small, rsmall])))
        return [_pair_sum(a, r, c) for a, r in zip(bigs, rbigs)]

    loss, dx, red = _local_step(x[0], positions[0], loss_target[0], w, mine, bufs,
                                {n: w[n].shape for n, _, _ in PACK}, front=(ffn_front, mixer_front))
    loss = lax.psum(loss, ("x", "y", "c"))

    *red[0]["sent_mixer"], rs = _chip_exchange(red[0]["mixer"], small_pair[0])
    own = [[_chip_sum(p, rb, chip.reshape(1))
            for p, rb in zip(r["mixer"] + r["ffn"], list(r["sent_mixer"]) + list(r["sent_ffn"]))] for r in red]
    flat = _send_to_sibling([a for o in own for a in o])
    other = [flat[3 * l:3 * l + 3] for l in range(depth)]
    g_small = _unpack_small(_small_sum(rs), SMALL, depth)
    g_small["conv_w"] = lax.dynamic_slice_in_dim(g_small["conv_w"], chip * 64, 64, axis=2)

    gw, delta, new_m, new_v = dict(g_small), {}, {}, {}

    def adam(n, srcs, turned=False, **where):
        view = (lambda a: jnp.swapaxes(a, 1, 2)) if turned else (lambda a: a)
        outs = _adamw_shard(view(w[n]), view(m[n]), view(v[n]), srcs, c, "adamw_" + n, **where)
        gw[n], delta[n], new_m[n], new_v[n] = [view(o) for o in outs]

    first = c[0] == 0
    packs = [jnp.concatenate([jnp.where(first, o[0], s[0]), jnp.where(first, s[0], o[0])]) for o, s in zip(own, other)]
    for n, off, rows in PACK[1:]:
        pieces = [pk[off:off + rows, :].reshape(w[n].shape[1:]) for pk in packs]
        adam(n, [(pc, pc) for pc in pieces], owner=0)
    rows_in = PACK[0][2]
    turn = lambda a: jnp.swapaxes(a, 1, 2).reshape(depth * rows_in, D)
    back = lambda a: jnp.swapaxes(a.reshape(depth, rows_in, D), 1, 2)
    g_in = jnp.concatenate([pk[:rows_in, :] for pk in packs])
    outs = _adamw_rows(turn(w["w_in"]), g_in, turn(m["w_in"]), turn(v["w_in"]), "adamw_w_in")
    gw["w_in"], delta["w_in"], new_m["w_in"], new_v["w_in"] = [back(a) for a in (g_in, *outs)]
    adam("w_gate", [(o[1], s[1]) for o, s in zip(own, other)], turned=True, owner=0)
    adam("w_up", [(o[1], s[1]) for o, s in zip(own, other)], turned=True, owner=1)
    adam("w_down", [(o[2], s[2]) for o, s in zip(own, other)], split=HP // 2)
    small_local = tuple((n, w[n].shape[1:]) for n, _ in SMALL)
    d_, m_, v_ = _adamw_rows(_pack_small(w, small_local, depth), _pack_small(gw, small_local, depth),
                             _pack_small(m, small_local, depth), _pack_small(v, small_local, depth), "adamw_small")
    delta.update(_unpack_small(d_, small_local, depth))
    new_m.update(_unpack_small(m_, small_local, depth))
    new_v.update(_unpack_small(v_, small_local, depth))

    return (loss, dx[None], *[gw[n] for n in WEIGHTS], *[delta[n] for n in WEIGHTS], *[new_m[n] for n in WEIGHTS],
            *[new_v[n] for n in WEIGHTS])
```

```python
import math

import jax
import jax.numpy as jnp
from jax import lax
from jax.experimental import pallas as pl
from jax.experimental.pallas import tpu as pltpu

F32 = jnp.float32
BF16 = jnp.bfloat16

D = 1024
HEADS = 8
NOPE = 64
ROPE = 32
VD = 64
QR = 384
KVR = 256
SGW = 256
CVW = 256
CHUNK = 128
DFF = 2816
EPS = 1e-6
ROPE_THETA = 10000.0
LOG2E = 1.4426950408889634
LN2 = 0.6931471805599453
QSCALE = (NOPE + ROPE) ** -0.5 * LOG2E
ZA = 768
ZB = 1280
QW = HEADS * 128
KVW = HEADS * 128 + HEADS * VD
NEG = -1e30
GC0 = 0.7978845608028654
GC1 = 0.044715

ADAM_LR = 0.001
ADAM_B1 = 0.9
ADAM_B2 = 0.999
ADAM_EPS = 1e-08
ADAM_WD = 0.01
ADAM_STEP = 10

V7X_VMEM_LIMIT = 52 * 1024 * 1024
ROW_TILE = 512
ATT_TILE = 512
ATT_HEADS = 4

NT = (((1,), (1,)), ((), ()))
TN = (((0,), (0,)), ((), ()))

HS = DFF // 4
HP = 768
DFFP = 4 * HP

PACK = (("w_in", 0, 488), ("w_out", 512, 256), ("w_ukv", 768, 64), ("w_uq", 832, 72))
CONV_ROW = 904
PACK_ROWS = 928
ROW_SHARDED = ("w_out",)
SMALL = (("mix_pre_g", (D,)), ("mix_post_g", (D,)), ("ffn_pre_g", (D,)), ("ffn_post_g", (D,)), ("q_norm_g", (QR,)),
         ("kv_norm_g", (KVR,)), ("sg_ln_g", (SGW,)), ("sg_ln_b", (SGW,)), ("w_sp", (4, CHUNK, CHUNK)), ("b_sp", (4, CHUNK)),
         ("conv_w", (3, CVW)), ("out_norm_g", (D,)))
WEIGHTS = ["mix_pre_g", "mix_post_g", "ffn_pre_g", "ffn_post_g", "w_in", "q_norm_g", "w_uq", "kv_norm_g", "w_ukv", "sg_ln_g",
           "sg_ln_b", "w_sp", "b_sp", "conv_w", "out_norm_g", "w_out", "w_gate", "w_up", "w_down"]

MESH_ID = pl.DeviceIdType.MESH
ANY = pl.BlockSpec(memory_space=pl.ANY)


def _cp(*sem):
    return pltpu.CompilerParams(dimension_semantics=sem, vmem_limit_bytes=V7X_VMEM_LIMIT)


def _sds(shape, dtype):
    return jax.ShapeDtypeStruct(shape, dtype)


def _row(tm, n):
    return pl.BlockSpec((tm, n), lambda i: (i, 0))


def _lyr(l, *shape):
    return pl.BlockSpec((None,) + shape, lambda *_: (l,) + (0,) * len(shape))


def _wl(a, l):
    return 0 if a.shape[0] == 1 else l


def _pcall(body, name, grid, ins, in_specs, out_specs, out_shape, sem, scratch=(), prevs=None):
    prevs = {k: v for k, v in (prevs or {}).items() if v is not None}
    order = sorted(prevs)
    n_in = len(ins)

    def wrapped(*refs):
        return body(*refs[:n_in], *refs[n_in + len(order):])

    return pl.pallas_call(
        wrapped, name=name, grid=grid, in_specs=list(in_specs) + [ANY] * len(order), out_specs=out_specs,
        out_shape=out_shape, scratch_shapes=list(scratch),
        input_output_aliases={n_in + i: k for i, k in enumerate(order)},
        compiler_params=_cp(*sem))(*ins, *[prevs[k] for k in order])


def _rms(x, g):
    r = lax.rsqrt(jnp.mean(x * x, axis=-1, keepdims=True) + EPS)
    return x * r * g


def _rms_bwd(x, g, dy):
    r = lax.rsqrt(jnp.mean(x * x, axis=-1, keepdims=True) + EPS)
    xh = x * r
    dg = jnp.sum(dy * xh, axis=0, keepdims=True)
    dxh = dy * g
    dx = r * (dxh - xh * jnp.mean(dxh * xh, axis=-1, keepdims=True))
    return dx, dg


def _sigmoid(x):
    return 0.5 * jnp.tanh(0.5 * x) + 0.5


def _gelu(x):
    return 0.5 * x * (1.0 + jnp.tanh(GC0 * (x + GC1 * x * x * x)))


def _gelu_grad(x):
    t = jnp.tanh(GC0 * (x + GC1 * x * x * x))
    return 0.5 * (1.0 + t) + 0.5 * x * (1.0 - t * t) * GC0 * (1.0 + 3.0 * GC1 * x * x)


def _rope(xb, c, s1, s2):
    return xb * c + pltpu.roll(xb, 112, 1) * s1 + pltpu.roll(xb, 16, 1) * s2


def _rope_bwd(dy, c, s1, s2):
    return dy * c + pltpu.roll(dy * s1, 16, 1) + pltpu.roll(dy * s2, 112, 1)


def _group_masks(shape):
    lane = lax.broadcasted_iota(jnp.int32, shape, 1)
    return [(lane >= 64 * g) & (lane < 64 * g + 64) for g in range(shape[1] // 64)]


def _group_mean(v, masks):
    out = jnp.zeros_like(v)
    for m in masks:
        s = jnp.sum(jnp.where(m, v, 0.0), axis=-1, keepdims=True) * (1.0 / 64.0)
        out = jnp.where(m, s, out)
    return out


def _pick_row(blk, idx):
    row = lax.broadcasted_iota(jnp.int32, blk.shape, 0)
    return jnp.sum(jnp.where(row == idx, blk, 0.0), axis=0, keepdims=True)


def _shift_down(y, k, first_rows):
    out = pltpu.roll(y, k, 0)
    row = lax.broadcasted_iota(jnp.int32, y.shape, 0)
    for idx in range(k):
        out = jnp.where(row == idx, first_rows[idx], out)
    return out


def _shift_up(y, k, last_rows):
    n = y.shape[0]
    out = pltpu.roll(y, n - k, 0)
    row = lax.broadcasted_iota(jnp.int32, y.shape, 0)
    for idx in range(k):
        out = jnp.where(row == n - k + idx, last_rows[idx], out)
    return out


def _tril_mask():
    r = lax.broadcasted_iota(jnp.int32, (CHUNK, CHUNK), 0)
    c = lax.broadcasted_iota(jnp.int32, (CHUNK, CHUNK), 1)
    return r >= c


def _sgu_forward(zu, zv, g_ln, b_ln, wc_bf, bsp, masks, cmasks):
    u = _gelu(zu)
    vv = _gelu(zv)
    mu = _group_mean(vv, masks)
    dv = vv - mu
    rs = lax.rsqrt(_group_mean(dv * dv, masks) + EPS)
    xh = dv * rs
    vn = xh * g_ln + b_ln
    chunks = []
    for ci in range(zu.shape[0] // CHUNK):
        vc = vn[ci * CHUNK:(ci + 1) * CHUNK, :]
        acc = bsp
        for g in range(4):
            acc = acc + jnp.dot(wc_bf[g], jnp.where(cmasks[g], vc, 0.0).astype(BF16), preferred_element_type=F32)
        chunks.append(acc)
    mixed = jnp.concatenate(chunks, axis=0) if len(chunks) > 1 else chunks[0]
    return u, vv, xh, rs, vn, mixed


def _conv_forward(gc, hh, prev_gc, prev_hh, first_tile, cw):
    yv = gc * hh
    prev = jnp.where(first_tile, 0.0, prev_gc * prev_hh)
    p6, p7 = _pick_row(prev, 6), _pick_row(prev, 7)
    sh1 = _shift_down(yv, 1, [p7])
    sh2 = _shift_down(yv, 2, [p6, p7])
    conv = sh2 * cw[0:1, :] + sh1 * cw[1:2, :] + yv * cw[2:3, :]
    return yv, sh1, sh2, conv


def _acc_init(step, *refs):
    @pl.when(step == 0)
    def _():
        for r in refs:
            r[...] = jnp.zeros(r.shape, r.dtype)


def _in_proj(x, p, l):
    t = x.shape[0]
    tm = min(ROW_TILE, t)

    def body(x_ref, g_ref, wa_ref, wb_ref, h_ref, za_ref, zb_ref):
        h = _rms(x_ref[...], g_ref[...]).astype(BF16)
        h_ref[...] = h
        za_ref[...] = lax.dot_general(h, wa_ref[...], NT, preferred_element_type=F32)
        zb_ref[...] = lax.dot_general(h, wb_ref[...], NT, preferred_element_type=F32)

    return _pcall(
        body, "in_proj", (t // tm,), [x, p["mix_pre_g"], p["w_in_a"], p["w_in_b"]],
        [_row(tm, D), _lyr(l, 1, D), _lyr(_wl(p["w_in_a"], l), ZA, D), _lyr(_wl(p["w_in_b"], l), ZB, D)],
        [_row(tm, D), _row(tm, ZA), _row(tm, ZB)],
        [_sds((t, D), BF16), _sds((t, ZA), F32), _sds((t, ZB), F32)], ("parallel",))


def _mla_prep(za, p, l, tabs):
    t = za.shape[0]
    tm = min(ROW_TILE, t)

    def body(z_ref, gq_ref, gkv_ref, wuq_ref, wukv_ref, c_ref, s1_ref, s2_ref, cq_ref, ckv_ref, q_ref, k_ref, v_ref):
        z = z_ref[...]
        cq = _rms(z[:, :QR], gq_ref[...]).astype(BF16)
        ckv = _rms(z[:, QR:QR + KVR], gkv_ref[...]).astype(BF16)
        cq_ref[...] = cq
        ckv_ref[...] = ckv
        c, s1, s2 = c_ref[...], s1_ref[...], s2_ref[...]
        kr = _rope(z[:, QR + KVR:], c, s1, s2)
        q = jnp.dot(cq, wuq_ref[...], preferred_element_type=F32)
        kv = jnp.dot(ckv, wukv_ref[...], preferred_element_type=F32)
        for h in range(HEADS):
            sl = slice(128 * h, 128 * h + 128)
            q_ref[:, sl] = (_rope(q[:, sl], c, s1, s2) * QSCALE).astype(BF16)
            k_ref[:, sl] = (kv[:, sl] + kr).astype(BF16)
        v_ref[...] = kv[:, QW:].astype(BF16)

    return _pcall(
        body, "mla_prep", (t // tm,), [za, p["q_norm_g"], p["kv_norm_g"], p["w_uq"], p["w_ukv"], *tabs],
        [_row(tm, ZA), _lyr(l, 1, QR), _lyr(l, 1, KVR), _lyr(_wl(p["w_uq"], l), QR, QW), _lyr(_wl(p["w_ukv"], l), KVR, KVW),
         _row(tm, 128), _row(tm, 128), _row(tm, 128)],
        [_row(tm, QR), _row(tm, KVR), _row(tm, QW), _row(tm, QW), _row(tm, HEADS * VD)],
        [_sds((t, QR), BF16), _sds((t, KVR), BF16), _sds((t, QW), BF16), _sds((t, QW), BF16),
         _sds((t, HEADS * VD), BF16)], ("parallel",))


def _att_tile(t):
    return min(ATT_TILE, max(t // 2, 128))


def _causal_keep(tq, i, j):
    row = lax.broadcasted_iota(jnp.int32, (tq, tq), 0) + i * tq
    col = lax.broadcasted_iota(jnp.int32, (tq, tq), 1) + j * tq
    return col <= row


def _attn_fwd(qs, k, v, fetch=None):
    t = qs.shape[0]
    tq = _att_tile(t)
    nq = t // tq
    rep = tq // 128
    groups = HEADS // ATT_HEADS
    mine, bufs, fetch_layer = fetch if fetch else ((), (), None)
    nb = len(mine)

    steps = [(i, j) for i in range(nq) for j in range(i + 1)]
    i_of = jnp.asarray([s[0] for s in steps], jnp.int32)
    j_of = jnp.asarray([s[1] for s in steps], jnp.int32)

    def body(i_ref, j_ref, q_ref, k_ref, v_ref, *refs):
        o_ref, lse_ref = refs[2 * nb:2 * nb + 2]
        m_s, l_s, acc_s = refs[3 * nb + 2:3 * nb + 5]
        step_no = pl.program_id(1)
        i, j = i_ref[step_no], j_ref[step_no]
        if fetch:
            start, hand_over, drain = _gather_ops(refs[:nb], refs[2 * nb + 2:3 * nb + 2], refs[3 * nb + 5:], fetch_layer)
            pr = pl.program_id(0)
            pl.when((pr == 0) & (step_no == 0))(start)
            pl.when((pr == groups - 1) & (step_no == 3 * len(steps) // 4))(hand_over)
            pl.when((pr == groups - 1) & (step_no == len(steps) - 1))(drain)

        @pl.when(j == 0)
        def _():
            m_s[...] = jnp.full(m_s.shape, NEG, F32)
            l_s[...] = jnp.zeros(l_s.shape, F32)
            acc_s[...] = jnp.zeros(acc_s.shape, F32)

        def step(masked):
            keep = _causal_keep(tq, i, j) if masked else None
            for hh in range(ATT_HEADS):
                sl = slice(128 * hh, 128 * hh + 128)
                vv = v_ref[:, 128 * (hh // 2):128 * (hh // 2) + 128]
                s = lax.dot_general(q_ref[:, sl], k_ref[:, sl], NT, preferred_element_type=F32)
                if masked:
                    s = jnp.where(keep, s, NEG)
                m_old = m_s[hh]
                m_new = jnp.maximum(m_old, jnp.max(s, axis=-1, keepdims=True))
                alpha = jnp.exp2(m_old - m_new)
                p = jnp.exp2(s - jnp.tile(m_new, (1, rep)))
                l_s[hh] = alpha * l_s[hh] + jnp.sum(p, axis=-1, keepdims=True)
                acc_s[hh] = alpha * acc_s[hh] + jnp.dot(p.astype(BF16), vv, preferred_element_type=F32)
                m_s[hh] = m_new

        @pl.when(j < i)
        def _():
            step(False)

        @pl.when(j == i)
        def _():
            step(True)
            lane = lax.broadcasted_iota(jnp.int32, (tq, 128), 1)
            for pp in range(ATT_HEADS // 2):
                a, b = 2 * pp, 2 * pp + 1
                o_ref[:, 128 * pp:128 * pp + 128] = jnp.where(lane < VD, acc_s[a] / l_s[a], acc_s[b] / l_s[b])
            for hh in range(ATT_HEADS):
                lse_ref[hh] = (m_s[hh] + jnp.log2(l_s[hh]))[:, 0:1]

    qw, vw = 128 * ATT_HEADS, VD * ATT_HEADS
    stat = pltpu.VMEM((ATT_HEADS, tq, 128), F32)
    grid_spec = pltpu.PrefetchScalarGridSpec(
        num_scalar_prefetch=2, grid=(groups, len(steps)),
        in_specs=[pl.BlockSpec((tq, qw), lambda p, s, it, jt: (it[s], p)),
                  pl.BlockSpec((tq, qw), lambda p, s, it, jt: (jt[s], p)),
                  pl.BlockSpec((tq, vw), lambda p, s, it, jt: (jt[s], p))] + [ANY] * (2 * nb),
        out_specs=[pl.BlockSpec((tq, vw), lambda p, s, it, jt: (it[s], p)),
                   pl.BlockSpec((ATT_HEADS, tq, 1), lambda p, s, it, jt: (p, it[s], 0))] + [ANY] * nb,
        scratch_shapes=[stat, stat, stat] + ([pltpu.SemaphoreType.DMA((nb, 3))] * 4 if fetch else []))
    outs = pl.pallas_call(
        body, name="attn_fwd_fetch" if fetch else "attn_fwd", grid_spec=grid_spec,
        out_shape=[_sds((t, HEADS * VD), F32), _sds((HEADS, t, 1), F32)] + [_sds(b.shape, b.dtype) for b in bufs],
        input_output_aliases={5 + nb + b: 2 + b for b in range(nb)},
        compiler_params=_cp("arbitrary", "arbitrary"))(i_of, j_of, qs, k, v, *mine, *bufs)
    return outs[0], outs[1], list(outs[2:])


def _mixer_fwd(zb, ya, p, l):
    t = zb.shape[0]
    tm = min(ROW_TILE, t)
    hb = tm // 8

    def body(zb_ref, zprev_ref, ya_ref, gln_ref, bln_ref, wsp_ref, bsp_ref, cw_ref, go_ref, mix_ref, yb_ref, yc_ref):
        i = pl.program_id(0)
        masks = _group_masks((tm, SGW))
        cmasks = _group_masks((CHUNK, SGW))
        tril = _tril_mask()
        wc_bf = [jnp.where(tril, wsp_ref[g], 0.0).astype(BF16) for g in range(4)]
        u, _, _, _, _, mixed = _sgu_forward(zb_ref[:, 0:256], zb_ref[:, 256:512], gln_ref[...], bln_ref[...], wc_bf,
                                            bsp_ref[...], masks, cmasks)
        yb = u * mixed
        _, _, _, conv = _conv_forward(zb_ref[:, 768:1024], zb_ref[:, 1024:1280], zprev_ref[:, 768:1024],
                                      zprev_ref[:, 1024:1280], i == 0, cw_ref[...])
        yc = zb_ref[:, 512:768] * conv
        yb_ref[...] = yb
        yc_ref[...] = yc
        go = go_ref[...]
        mix_ref[:, 0:512] = _rms(ya_ref[...], go[:, 0:512]).astype(BF16)
        mix_ref[:, 512:768] = _rms(yb, go[:, 512:768]).astype(BF16)
        mix_ref[:, 768:1024] = _rms(yc, go[:, 768:1024]).astype(BF16)

    return _pcall(
        body, "mixer_fwd", (t // tm,),
        [zb, zb, ya, p["sg_ln_g"], p["sg_ln_b"], p["w_sp"], p["b_sp"], p["conv_w"], p["out_norm_g"]],
        [_row(tm, ZB), pl.BlockSpec((8, ZB), lambda i: (jnp.maximum(i * hb - 1, 0), 0)), _row(tm, 512),
         _lyr(l, 1, SGW), _lyr(l, 1, SGW), _lyr(l, 4, CHUNK, CHUNK), _lyr(l, CHUNK, SGW), _lyr(_wl(p["conv_w"], l), 3, CVW), _lyr(l, 1, D)],
        [_row(tm, D), _row(tm, SGW), _row(tm, CVW)],
        [_sds((t, D), BF16), _sds((t, SGW), F32), _sds((t, CVW), F32)], ("parallel",))


def _out_proj(mix, x, p, l):
    t = x.shape[0]
    tm = min(ROW_TILE, t)

    def body(mix_ref, w_ref, x_ref, gp_ref, gf_ref, o_ref, x2_ref, h2_ref):
        o = jnp.dot(mix_ref[...], w_ref[...], preferred_element_type=F32)
        o_ref[...] = o
        x2 = x_ref[...] + _rms(o, gp_ref[...])
        x2_ref[...] = x2
        h2_ref[...] = _rms(x2, gf_ref[...]).astype(BF16)

    return _pcall(
        body, "out_proj", (t // tm,), [mix, p["w_out"], x, p["mix_post_g"], p["ffn_pre_g"]],
        [_row(tm, D), _lyr(_wl(p["w_out"], l), D, D), _row(tm, D), _lyr(l, 1, D), _lyr(l, 1, D)],
        [_row(tm, D), _row(tm, D), _row(tm, D)],
        [_sds((t, D), F32), _sds((t, D), F32), _sds((t, D), BF16)], ("parallel",))


def _gu_all(l, which):
    return pl.BlockSpec((4, None, None, HP, D), lambda *_: (0, l, which, 0, 0))


def _down_all(l):
    return pl.BlockSpec((4, None, HP, D), lambda *_: (0, l, 0, 0))


def _ffn_up(h2, p, l):
    t = h2.shape[0]
    tm = min(ROW_TILE, t)

    def body(h_ref, wg_ref, wu_ref, a_ref, b_ref, s_ref):
        h = h_ref[...]
        a = lax.dot_general(h, wg_ref[...], NT, preferred_element_type=F32)
        b = lax.dot_general(h, wu_ref[...], NT, preferred_element_type=F32)
        a_ref[...] = a.astype(BF16)
        b_ref[...] = b.astype(BF16)
        s_ref[...] = (a * _sigmoid(a) * b).astype(BF16)

    blk = pl.BlockSpec((tm, HP), lambda k, i: (i, k))
    wblk = lambda which: pl.BlockSpec((None, None, None, HP, D), lambda k, i: (k, l, which, 0, 0))
    return _pcall(
        body, "ffn_up", (4, t // tm), [h2, p["w_gu"], p["w_gu"]],
        [pl.BlockSpec((tm, D), lambda k, i: (i, 0)), wblk(0), wblk(1)], [blk, blk, blk],
        [_sds((t, DFFP), BF16)] * 3, ("parallel", "parallel"))


def _ffn_down(s, x2, p, l):
    t = x2.shape[0]
    tm = min(ROW_TILE, t)

    def body(s_ref, w_ref, x_ref, g_ref, f_ref, x3_ref):
        f = jnp.dot(s_ref[:, 0:HP], w_ref[0], preferred_element_type=F32)
        for k in range(1, 4):
            f = f + jnp.dot(s_ref[:, k * HP:(k + 1) * HP], w_ref[k], preferred_element_type=F32)
        f_ref[...] = f
        x3_ref[...] = x_ref[...] + _rms(f, g_ref[...])

    return _pcall(
        body, "ffn_down", (t // tm,), [s, p["w_down"], x2, p["ffn_post_g"]],
        [_row(tm, DFFP), _down_all(l), _row(tm, D), _lyr(l, 1, D)], [_row(tm, D), _row(tm, D)],
        [_sds((t, D), F32), _sds((t, D), F32)], ("parallel",))


def _loss_head(y, target):
    t = y.shape[0]
    tm = min(ROW_TILE, t)

    def body(y_ref, t_ref, dy_ref, acc_ref):
        e = y_ref[...] - t_ref[...]
        dy_ref[...] = e * (1.0 / D)
        sq = jnp.sum(e * e, axis=0, keepdims=True)
        part = sq[:, 0:128]
        for b in range(1, D // 128):
            part = part + sq[:, 128 * b:128 * b + 128]
        _acc_init(pl.program_id(0), acc_ref)
        acc_ref[...] += part

    return _pcall(body, "loss_head", (t // tm,), [y, target], [_row(tm, D), _row(tm, D)],
                  [_row(tm, D), pl.BlockSpec((1, 128), lambda i: (0, 0))],
                  [_sds((t, D), F32), _sds((1, 128), F32)], ("arbitrary",))


def _ffn_down_bwd(dx3, sv, p, l, depth, gb):
    t = dx3.shape[0]
    tm = min(256, t)

    def body(dx_ref, f_ref, g_ref, w_ref, a_ref, b_ref, df_ref, da_ref, db_ref, dg_ref):
        _acc_init(pl.program_id(0), dg_ref)
        df, dg = _rms_bwd(f_ref[...], g_ref[...], dx_ref[...])
        dg_ref[...] += dg
        df = df.astype(BF16)
        df_ref[...] = df
        for k in range(4):
            sl = slice(k * HP, (k + 1) * HP)
            ds = lax.dot_general(df, w_ref[k], NT, preferred_element_type=F32)
            av = a_ref[:, sl].astype(F32)
            sig = _sigmoid(av)
            da_ref[:, sl] = (ds * b_ref[:, sl].astype(F32) * (sig * (1.0 + av * (1.0 - sig)))).astype(BF16)
            db_ref[:, sl] = (ds * (av * sig)).astype(BF16)

    df, da, db, gb["ffn_post_g"] = _pcall(
        body, "ffn_down_bwd", (t // tm,), [dx3, sv["f"], p["ffn_post_g"], p["w_down"], sv["a"], sv["b"]],
        [_row(tm, D), _row(tm, D), _lyr(l, 1, D), _down_all(l), _row(tm, DFFP), _row(tm, DFFP)],
        [_row(tm, D), _row(tm, DFFP), _row(tm, DFFP), _lyr(l.g, 1, D)],
        [_sds((t, D), BF16), _sds((t, DFFP), BF16), _sds((t, DFFP), BF16), _sds((depth, 1, D), F32)], ("arbitrary",),
        prevs={3: gb.get("ffn_post_g")})
    return df, da, db


def _ffn_up_bwd(da, db, dx3, sv, p, l, depth, gb):
    t = dx3.shape[0]
    tm = min(256, t)

    def body(da_ref, db_ref, wg_ref, wu_ref, x_ref, dx3_ref, g_ref, dx2_ref, dg_ref):
        _acc_init(pl.program_id(0), dg_ref)
        dh = jnp.zeros((tm, D), F32)
        for k in range(4):
            sl = slice(k * HP, (k + 1) * HP)
            dh = dh + jnp.dot(da_ref[:, sl], wg_ref[k], preferred_element_type=F32)
            dh = dh + jnp.dot(db_ref[:, sl], wu_ref[k], preferred_element_type=F32)
        dx, dg = _rms_bwd(x_ref[...], g_ref[...], dh)
        dg_ref[...] += dg
        dx2_ref[...] = dx3_ref[...] + dx

    dx2, gb["ffn_pre_g"] = _pcall(
        body, "ffn_up_bwd", (t // tm,), [da, db, p["w_gu"], p["w_gu"], sv["x2"], dx3, p["ffn_pre_g"]],
        [_row(tm, DFFP), _row(tm, DFFP), _gu_all(l, 0), _gu_all(l, 1), _row(tm, D), _row(tm, D), _lyr(l, 1, D)],
        [_row(tm, D), _lyr(l.g, 1, D)], [_sds((t, D), F32), _sds((depth, 1, D), F32)], ("arbitrary",),
        prevs={1: gb.get("ffn_pre_g")})
    return dx2


def _out_proj_bwd(dx2, sv, p, l, depth, gb):
    t = dx2.shape[0]
    tm = min(ROW_TILE, t)

    def body(dx_ref, o_ref, g_ref, w_ref, do_ref, dmix_ref, dg_ref):
        _acc_init(pl.program_id(0), dg_ref)
        do, dg = _rms_bwd(o_ref[...], g_ref[...], dx_ref[...])
        dg_ref[...] += dg
        do = do.astype(BF16)
        do_ref[...] = do
        dmix_ref[...] = lax.dot_general(do, w_ref[...], NT, preferred_element_type=F32)

    do, dmix, gb["mix_post_g"] = _pcall(
        body, "out_proj_bwd", (t // tm,), [dx2, sv["o"], p["mix_post_g"], p["w_out"]],
        [_row(tm, D), _row(tm, D), _lyr(l, 1, D), _lyr(_wl(p["w_out"], l), D, D)], [_row(tm, D), _row(tm, D), _lyr(l.g, 1, D)],
        [_sds((t, D), BF16), _sds((t, D), F32), _sds((depth, 1, D), F32)], ("arbitrary",),
        prevs={2: gb.get("mix_post_g")})
    return do, dmix


def _mixer_bwd(dmix, sv, p, l, depth, gb):
    zb = sv["zb"]
    t = zb.shape[0]
    tm = min(ROW_TILE, t)
    hb = tm // 8
    last_blk = t // 8 - 1
    nsteps = t // tm

    def body(dmix_ref, ya_ref, yb_ref, yc_ref, zb_ref, zprev_ref, znext_ref, ycn_ref, dmn_ref,
             gln_ref, bln_ref, wsp_ref, bsp_ref, cw_ref, go_ref,
             dya_ref, dzb_ref, delta_ref, dgo_ref, dgln_ref, dbln_ref, dwsp_ref, dbsp_ref, dcw_ref):
        i = pl.program_id(0)
        _acc_init(i, dgo_ref, dgln_ref, dbln_ref, dwsp_ref, dbsp_ref, dcw_ref)
        go = go_ref[...]
        dmix = dmix_ref[...]

        ya = ya_ref[...]
        dya, dga = _rms_bwd(ya, go[:, 0:512], dmix[:, 0:512])
        dyb, dgb_ = _rms_bwd(yb_ref[...], go[:, 512:768], dmix[:, 512:768])
        dyc, dgc_ = _rms_bwd(yc_ref[...], go[:, 768:1024], dmix[:, 768:1024])
        dgo_ref[:, 0:512] += dga
        dgo_ref[:, 512:768] += dgb_
        dgo_ref[:, 768:1024] += dgc_
        dya = dya * LN2
        lane128 = lax.broadcasted_iota(jnp.int32, (tm, 128), 1)
        for h in range(HEADS):
            blk = dya[:, 128 * (h // 2):128 * (h // 2) + 128]
            own = (lane128 < VD) if h % 2 == 0 else (lane128 >= VD)
            dya_ref[:, 128 * h:128 * h + 128] = jnp.where(own, blk, 0.0).astype(BF16)
        prod = dya * ya
        hmasks = _group_masks((tm, 512))
        for h in range(HEADS):
            delta_ref[h] = jnp.sum(jnp.where(hmasks[h], prod, 0.0), axis=-1, keepdims=True)

        masks = _group_masks((tm, SGW))
        cmasks = _group_masks((CHUNK, SGW))
        tril = _tril_mask()
        wc_bf = [jnp.where(tril, wsp_ref[g], 0.0).astype(BF16) for g in range(4)]
        zu, zv = zb_ref[:, 0:256], zb_ref[:, 256:512]
        g_ln = gln_ref[...]
        u, _, xh, rs, vn, mixed = _sgu_forward(zu, zv, g_ln, bln_ref[...], wc_bf, bsp_ref[...], masks, cmasks)
        du = dyb * mixed
        dmixed = dyb * u
        dvn_chunks = []
        dbsp = jnp.zeros((CHUNK, SGW), F32)
        for ci in range(tm // CHUNK):
            rows = slice(ci * CHUNK, (ci + 1) * CHUNK)
            dm_c = dmixed[rows, :]
            vn_c = vn[rows, :].astype(BF16)
            dbsp = dbsp + dm_c
            dvn_c = jnp.zeros((CHUNK, SGW), F32)
            for g in range(4):
                dm_g = jnp.where(cmasks[g], dm_c, 0.0).astype(BF16)
                dw = lax.dot_general(dm_g, vn_c, NT, preferred_element_type=F32)
                dwsp_ref[g] += jnp.where(tril, dw, 0.0)
                dvn_c = dvn_c + lax.dot_general(wc_bf[g], dm_g, TN, preferred_element_type=F32)
            dvn_chunks.append(dvn_c)
        dbsp_ref[...] += dbsp
        dvn = jnp.concatenate(dvn_chunks, axis=0) if len(dvn_chunks) > 1 else dvn_chunks[0]
        dgln_ref[...] += jnp.sum(dvn * xh, axis=0, keepdims=True)
        dbln_ref[...] += jnp.sum(dvn, axis=0, keepdims=True)
        dxh = dvn * g_ln
        dvv = rs * (dxh - _group_mean(dxh, masks) - xh * _group_mean(dxh * xh, masks))
        dzb_ref[:, 0:256] = (du * _gelu_grad(zu)).astype(BF16)
        dzb_ref[:, 256:512] = (dvv * _gelu_grad(zv)).astype(BF16)

        cwv = cw_ref[...]
        gb_, gc, hh = zb_ref[:, 512:768], zb_ref[:, 768:1024], zb_ref[:, 1024:1280]
        yv, sh1, sh2, conv = _conv_forward(gc, hh, zprev_ref[:, 768:1024], zprev_ref[:, 1024:1280], i == 0, cwv)
        dconv = dyc * gb_
        dzb_ref[:, 512:768] = (dyc * conv).astype(BF16)
        dcw_ref[0:1, :] += jnp.sum(dconv * sh2, axis=0, keepdims=True)
        dcw_ref[1:2, :] += jnp.sum(dconv * sh1, axis=0, keepdims=True)
        dcw_ref[2:3, :] += jnp.sum(dconv * yv, axis=0, keepdims=True)
        dycn, _ = _rms_bwd(ycn_ref[...], go[:, 768:1024], dmn_ref[...])
        dconv_next = jnp.where(i == nsteps - 1, 0.0, dycn * znext_ref[:, 512:768])
        n0, n1 = _pick_row(dconv_next, 0), _pick_row(dconv_next, 1)
        dyv = dconv * cwv[2:3, :] + _shift_up(dconv, 1, [n0]) * cwv[1:2, :] + _shift_up(dconv, 2, [n0, n1]) * cwv[0:1, :]
        dzb_ref[:, 768:1024] = (dyv * hh).astype(BF16)
        dzb_ref[:, 1024:1280] = (dyv * gc).astype(BF16)

    prev_map = lambda i: (jnp.maximum(i * hb - 1, 0), 0)
    next_map = lambda i: (jnp.minimum((i + 1) * hb, last_blk), 0)
    names = ("out_norm_g", "sg_ln_g", "sg_ln_b", "w_sp", "b_sp_t", "conv_w")
    shapes = ((1, D), (1, SGW), (1, SGW), (4, CHUNK, CHUNK), (CHUNK, SGW), (3, CVW))
    outs = _pcall(
        body, "mixer_bwd", (nsteps,),
        [dmix, sv["ya"], sv["yb"], sv["yc"], zb, zb, zb, sv["yc"], dmix, p["sg_ln_g"], p["sg_ln_b"], p["w_sp"], p["b_sp"],
         p["conv_w"], p["out_norm_g"]],
        [_row(tm, D), _row(tm, 512), _row(tm, SGW), _row(tm, CVW), _row(tm, ZB),
         pl.BlockSpec((8, ZB), prev_map), pl.BlockSpec((8, ZB), next_map), pl.BlockSpec((8, CVW), next_map),
         pl.BlockSpec((8, 256), lambda i: (jnp.minimum((i + 1) * hb, last_blk), 3)),
         _lyr(l, 1, SGW), _lyr(l, 1, SGW), _lyr(l, 4, CHUNK, CHUNK), _lyr(l, CHUNK, SGW), _lyr(_wl(p["conv_w"], l), 3, CVW), _lyr(l, 1, D)],
        [_row(tm, QW), _row(tm, ZB), pl.BlockSpec((HEADS, tm, 1), lambda i: (0, i, 0))] + [_lyr(l.g, *s) for s in shapes],
        [_sds((t, QW), BF16), _sds((t, ZB), BF16), _sds((HEADS, t, 1), F32)] + [_sds((depth,) + s, F32) for s in shapes],
        ("arbitrary",), prevs={3 + n: gb.get(name) for n, name in enumerate(names)})
    for n, name in enumerate(names):
        gb[name] = outs[3 + n]
    return outs[0], outs[1], outs[2]


def _attn_bwd(qs, k, v, dya, lse, delta, ride=()):
    t = qs.shape[0]
    tq = _att_tile(t)
    nq = t // tq
    nb = len(ride)
    groups = HEADS // ATT_HEADS

    steps = [(j, i) for j in range(nq) for i in range(j, nq)]
    j_of = jnp.asarray([s[0] for s in steps], jnp.int32)
    i_of = jnp.asarray([s[1] for s in steps], jnp.int32)

    def body(j_ref, i_ref, q_ref, k_ref, v_ref, do_ref, lse_ref, dl_ref, *refs):
        dq_ref, dk_ref, dv_ref = refs[nb:nb + 3]
        dq_s, dk_s, dv_s = refs[2 * nb + 3:2 * nb + 6]
        step_no = pl.program_id(1)
        j, i = j_ref[step_no], i_ref[step_no]
        if ride:
            start, finish = _exchange_ops(refs[:nb], refs[nb + 3:2 * nb + 3], refs[2 * nb + 6:])
            pr = pl.program_id(0)
            pl.when((pr == 0) & (step_no == 0))(start)
            pl.when((pr == groups - 1) & (step_no == len(steps) - 1))(finish)

        @pl.when(step_no == 0)
        def _():
            dq_s[...] = jnp.zeros(dq_s.shape, F32)

        def step(masked):
            keep = _causal_keep(tq, 0, 0) if masked else None
            lane = lax.broadcasted_iota(jnp.int32, (tq, 128), 1)
            rows = pl.ds(pl.multiple_of(i * tq, tq), tq)
            for hh in range(ATT_HEADS):
                sl = slice(128 * hh, 128 * hh + 128)
                pair = slice(128 * (hh // 2), 128 * (hh // 2) + 128)
                vv, do_h = v_ref[:, pair], do_ref[:, sl]
                qq, kk = q_ref[:, sl], k_ref[:, sl]
                s = lax.dot_general(qq, kk, NT, preferred_element_type=F32)
                p = jnp.exp2(s - lse_ref[hh])
                if masked:
                    p = jnp.where(keep, p, 0.0)
                dp = lax.dot_general(do_h, vv, NT, preferred_element_type=F32)
                ds = (p * (dp - dl_ref[hh])).astype(BF16)
                dv_s[:, pair] += lax.dot_general(p.astype(BF16), do_h, TN, preferred_element_type=F32)
                dk_s[:, sl] += lax.dot_general(ds, qq, TN, preferred_element_type=F32)
                dq_s[rows, sl] += jnp.dot(ds, kk, preferred_element_type=F32)

        @pl.when(i == j)
        def _():
            dk_s[...] = jnp.zeros(dk_s.shape, F32)
            dv_s[...] = jnp.zeros(dv_s.shape, F32)
            step(True)

        @pl.when(i > j)
        def _():
            step(False)

        @pl.when(i == nq - 1)
        def _():
            dk_ref[...] = dk_s[...].astype(BF16)
            dv_ref[...] = (dv_s[...] * LOG2E).astype(BF16)

        @pl.when(step_no == len(steps) - 1)
        def _():
            dq_ref[...] = dq_s[...].astype(BF16)

    qw, vw = 128 * ATT_HEADS, VD * ATT_HEADS
    qrow = lambda p, s, jt, it: (it[s], p)
    krow = lambda p, s, jt, it: (jt[s], p)
    col_spec = pl.BlockSpec((ATT_HEADS, tq, 1), lambda p, s, jt, it: (p, it[s], 0))
    grid_spec = pltpu.PrefetchScalarGridSpec(
        num_scalar_prefetch=2, grid=(groups, len(steps)),
        in_specs=[pl.BlockSpec((tq, qw), qrow), pl.BlockSpec((tq, qw), krow), pl.BlockSpec((tq, vw), krow),
                  pl.BlockSpec((tq, qw), qrow), col_spec, col_spec] + [ANY] * nb,
        out_specs=[pl.BlockSpec((t, qw), lambda p, s, jt, it: (0, p)), pl.BlockSpec((tq, qw), krow),
                   pl.BlockSpec((tq, vw), krow)] + [ANY] * nb,
        scratch_shapes=[pltpu.VMEM((t, qw), F32), pltpu.VMEM((tq, qw), F32), pltpu.VMEM((tq, vw), F32)]
        + ([pltpu.SemaphoreType.DMA((nb, 3))] * 2 if ride else []))
    outs = pl.pallas_call(
        body, name="attn_bwd_ride" if ride else "attn_bwd", grid_spec=grid_spec,
        out_shape=[_sds((t, QW), BF16), _sds((t, QW), BF16), _sds((t, HEADS * VD), BF16)]
        + [_sds((3,) + a.shape[1:], a.dtype) for a in ride],
        compiler_params=_cp("arbitrary", "arbitrary"))(j_of, i_of, qs, k, v, dya, lse, delta, *ride)
    return outs[0], outs[1], outs[2], list(outs[3:])


def _mla_prep_bwd(dqs, dk, dv, sv, p, l, depth, gb, tabs):
    za = sv["za"]
    t = za.shape[0]
    tm = min(ROW_TILE, t)

    def body(dq_ref, dk_ref, dv_ref, z_ref, gq_ref, gkv_ref, wuq_ref, wukv_ref, c_ref, s1_ref, s2_ref,
             dza_ref, dqp_ref, dkv_ref, dgq_ref, dgkv_ref):
        _acc_init(pl.program_id(0), dgq_ref, dgkv_ref)
        c, s1, s2 = c_ref[...], s1_ref[...], s2_ref[...]
        lane = lax.broadcasted_iota(jnp.int32, (tm, 128), 1)
        rope_lanes = (lane >= NOPE) & (lane < NOPE + ROPE)
        dkr = jnp.zeros((tm, 128), F32)
        for h in range(HEADS):
            sl = slice(128 * h, 128 * h + 128)
            dqp_ref[:, sl] = _rope_bwd(dq_ref[:, sl].astype(F32) * QSCALE, c, s1, s2).astype(BF16)
            dkh = dk_ref[:, sl]
            dkv_ref[:, sl] = dkh
            dkr = dkr + jnp.where(rope_lanes, dkh.astype(F32), 0.0)
        dkv_ref[:, QW:] = dv_ref[...]
        z = z_ref[...]
        dcq = lax.dot_general(dqp_ref[...], wuq_ref[...], NT, preferred_element_type=F32)
        dzq, dgq = _rms_bwd(z[:, :QR], gq_ref[...], dcq)
        dckv = lax.dot_general(dkv_ref[...], wukv_ref[...], NT, preferred_element_type=F32)
        dzkv, dgkv = _rms_bwd(z[:, QR:QR + KVR], gkv_ref[...], dckv)
        dgq_ref[...] += dgq
        dgkv_ref[...] += dgkv
        dza_ref[:, :QR] = dzq.astype(BF16)
        dza_ref[:, QR:QR + KVR] = dzkv.astype(BF16)
        dza_ref[:, QR + KVR:] = _rope_bwd(dkr, c, s1, s2).astype(BF16)

    dza, dqp, dkv, gb["q_norm_g"], gb["kv_norm_g"] = _pcall(
        body, "mla_prep_bwd", (t // tm,),
        [dqs, dk, dv, za, p["q_norm_g"], p["kv_norm_g"], p["w_uq"], p["w_ukv"], *tabs],
        [_row(tm, QW), _row(tm, QW), _row(tm, HEADS * VD), _row(tm, ZA), _lyr(l, 1, QR), _lyr(l, 1, KVR),
         _lyr(_wl(p["w_uq"], l), QR, QW), _lyr(_wl(p["w_ukv"], l), KVR, KVW), _row(tm, 128), _row(tm, 128), _row(tm, 128)],
        [_row(tm, ZA), _row(tm, QW), _row(tm, KVW), _lyr(l.g, 1, QR), _lyr(l.g, 1, KVR)],
        [_sds((t, ZA), BF16), _sds((t, QW), BF16), _sds((t, KVW), BF16), _sds((depth, 1, QR), F32),
         _sds((depth, 1, KVR), F32)], ("arbitrary",), prevs={3: gb.get("q_norm_g"), 4: gb.get("kv_norm_g")})
    return dza, dqp, dkv


def _in_proj_bwd(dza, dzb, dx2, sv, p, l, depth, gb):
    t = dx2.shape[0]
    tm = min(ROW_TILE, t)

    def body(dza_ref, dzb_ref, wa_ref, wb_ref, x_ref, dx2_ref, g_ref, dx_ref, dg_ref):
        _acc_init(pl.program_id(0), dg_ref)
        dh = (jnp.dot(dza_ref[...], wa_ref[...], preferred_element_type=F32)
              + jnp.dot(dzb_ref[...], wb_ref[...], preferred_element_type=F32))
        dx, dg = _rms_bwd(x_ref[...], g_ref[...], dh)
        dg_ref[...] += dg
        dx_ref[...] = dx2_ref[...] + dx

    dx, gb["mix_pre_g"] = _pcall(
        body, "in_proj_bwd", (t // tm,), [dza, dzb, p["w_in_a"], p["w_in_b"], sv["x"], dx2, p["mix_pre_g"]],
        [_row(tm, ZA), _row(tm, ZB), _lyr(_wl(p["w_in_a"], l), ZA, D), _lyr(_wl(p["w_in_b"], l), ZB, D), _row(tm, D), _row(tm, D), _lyr(l, 1, D)],
        [_row(tm, D), _lyr(l.g, 1, D)], [_sds((t, D), F32), _sds((depth, 1, D), F32)], ("arbitrary",),
        prevs={1: gb.get("mix_pre_g")})
    return dx


def _mm_tn(a, b, tn, name, l, depth, gb):
    t, k = a.shape
    n = b.shape[1]
    tt = min(ROW_TILE, t)

    def body(a_ref, b_ref, o_ref):
        _acc_init(pl.program_id(1), o_ref)
        o_ref[...] += lax.dot_general(a_ref[...], b_ref[...], TN, preferred_element_type=F32)

    gb[name] = _pcall(
        body, "d" + name, (n // tn, t // tt), [a, b],
        [pl.BlockSpec((tt, k), lambda j, s: (s, 0)), pl.BlockSpec((tt, tn), lambda j, s: (s, j))],
        pl.BlockSpec((None, k, tn), lambda j, s: (l.g, 0, j)), _sds((depth, k, n), F32), ("parallel", "arbitrary"),
        prevs={0: gb.get(name)})


def _dw_ffn(a, b, kind, l, depth, gb):
    t = a.shape[0]
    tt = min(ROW_TILE, t)
    nsteps = t // tt

    def body(a_ref, b_ref, o_ref, acc):
        s = pl.program_id(0)
        _acc_init(s, acc)
        acc[...] += lax.dot_general(a_ref[...], b_ref[...], TN, preferred_element_type=F32)

        @pl.when(s == nsteps - 1)
        def _():
            for k in range(4):
                o_ref[k] = acc[k * HP:(k + 1) * HP, :].astype(BF16)

    rows = lambda n: pl.BlockSpec((tt, n), lambda s: (s, 0))
    if kind == "down":
        name = "down"
        out_spec = pl.BlockSpec((4, None, HP, D), lambda s: (0, l.g, 0, 0))
        out_shape = _sds((4, depth, HP, D), BF16)
    else:
        which = 0 if kind == "gate" else 1
        name = "gu"
        out_spec = pl.BlockSpec((4, None, None, HP, D), lambda s: (0, l.g, which, 0, 0))
        out_shape = _sds((4, depth, 2, HP, D), BF16)
    gb[name] = _pcall(body, "dw_" + kind, (nsteps,), [a, b], [rows(DFFP), rows(D)], out_spec, out_shape, ("arbitrary",),
                      scratch=[pltpu.VMEM((DFFP, D), F32)], prevs={0: gb.get(name)})


def _ffn_views(bufs):
    return {"w_gu": bufs[1], "w_down": bufs[2].reshape(bufs[2].shape[:2] + (HP, D))}


def _layer_fwd(x, p, l, tabs, fetch):
    h1, za, zb = _in_proj(x, p, l)
    cqn, ckvn, qs, k, v = _mla_prep(za, p, l, tabs)
    ya, lse, bufs = _attn_fwd(qs, k, v, fetch)
    if fetch:
        p = {**p, **_ffn_views(bufs)}
    mix, yb, yc = _mixer_fwd(zb, ya, p, l)
    o, x2, h2 = _out_proj(mix, x, p, l)
    a, b, s = _ffn_up(h2, p, l)
    f, x3 = _ffn_down(s, x2, p, l)
    saved = dict(x=x, h1=h1, za=za, zb=zb, cqn=cqn, ckvn=ckvn, qs=qs, k=k, v=v, ya=ya, lse=lse, mix=mix, yb=yb, yc=yc,
                 o=o, x2=x2, h2=h2, a=a, b=b, s=s, f=f)
    return x3, saved, bufs if fetch else None


class _Layer(int):
    def __new__(cls, l, g):
        obj = int.__new__(cls, l)
        obj.g = g
        return obj


def _layer_bwd(dx3, p, sv, l, depth, gb, tabs, ride=(), ffn_front=None):
    df, da, db = _ffn_down_bwd(dx3, sv, p, l, depth, gb)
    _dw_ffn(sv["s"], df, "down", l, depth, gb)
    dx2 = _ffn_up_bwd(da, db, dx3, sv, p, l, depth, gb)
    _dw_ffn(da, sv["h2"], "gate", l, depth, gb)
    _dw_ffn(db, sv["h2"], "up", l, depth, gb)
    ffn_blocks = list(ffn_front(gb["gu"], gb["down"])) if ffn_front else []
    do, dmix = _out_proj_bwd(dx2, sv, p, l, depth, gb)
    _mm_tn(sv["mix"], do, D, "w_out", l, depth, gb)
    dya, dzb, delta = _mixer_bwd(dmix, sv, p, l, depth, gb)
    dqs, dk, dv, sent = _attn_bwd(sv["qs"], sv["k"], sv["v"], dya, sv["lse"], delta, tuple(ffn_blocks) + tuple(ride))
    dza, dqp, dkv = _mla_prep_bwd(dqs, dk, dv, sv, p, l, depth, gb, tabs)
    _mm_tn(sv["cqn"], dqp, QW, "w_uq", l, depth, gb)
    _mm_tn(sv["ckvn"], dkv, KVW, "w_ukv", l, depth, gb)
    _mm_tn(dza, sv["h1"], D, "w_in_a", l, depth, gb)
    _mm_tn(dzb, sv["h1"], D, "w_in_b", l, depth, gb)
    nf = len(ffn_blocks)
    return _in_proj_bwd(dza, dzb, dx2, sv, p, l, depth, gb), ffn_blocks, sent[:nf], sent[nf:]


def _rope_tables(positions):
    inv_freq = 1.0 / (ROPE_THETA ** (jnp.arange(0, ROPE // 2, dtype=F32) / (ROPE // 2)))
    ang = positions.astype(F32)[:, None] * inv_freq
    cos, sin = jnp.cos(ang), jnp.sin(ang)
    t = positions.shape[0]
    one, zero = jnp.ones((t, 64), F32), jnp.zeros((t, 16), F32)
    c = jnp.concatenate([one, cos, cos, one[:, :32]], axis=1)
    s1 = jnp.concatenate([zero, zero, zero, zero, -sin, zero, zero, zero], axis=1)
    s2 = jnp.concatenate([zero, zero, zero, zero, zero, sin, zero, zero], axis=1)
    return c, s1, s2


def _mixer_weight_params(full):
    w_in = full["w_in_t"]
    depth = w_in.shape[0]
    zpad = lambda n: jnp.zeros((depth, n, D), w_in.dtype)
    kv = full["w_ukv"].reshape(depth, KVR, HEADS, NOPE + VD)
    return {
        "w_in_a": jnp.concatenate([w_in[:, :640], zpad(64), w_in[:, 640:672], zpad(32)], axis=1),
        "w_in_b": w_in[:, 672:],
        "w_uq": jnp.pad(full["w_uq"].reshape(depth, QR, HEADS, NOPE + ROPE),
                        ((0, 0), (0, 0), (0, 0), (0, 32))).reshape(depth, QR, QW),
        "w_ukv": jnp.concatenate([jnp.pad(kv[..., :NOPE], ((0, 0), (0, 0), (0, 0), (0, 64))).reshape(depth, KVR, QW),
                                  kv[..., NOPE:].reshape(depth, KVR, HEADS * VD)], axis=2),
        "w_out": full["w_out"], "conv_w": full["conv_w"],
    }


def _small_params(w):
    p = {"w_sp": w["w_sp"], "b_sp": jnp.repeat(jnp.swapaxes(w["b_sp"], 1, 2), 64, axis=2)}
    for n in ("mix_pre_g", "mix_post_g", "ffn_pre_g", "ffn_post_g", "q_norm_g", "kv_norm_g", "sg_ln_g", "sg_ln_b",
              "out_norm_g"):
        p[n] = w[n][:, None, :]
    return p


def _natural_grads(gb):
    depth = gb["w_in_a"].shape[0]
    ga, kv = gb["w_in_a"], gb["w_ukv"]
    out = {
        "w_in_t": jnp.concatenate([ga[:, :640], ga[:, 704:736], gb["w_in_b"]], axis=1),
        "w_uq": gb["w_uq"].reshape(depth, QR, HEADS, 128)[..., :NOPE + ROPE].reshape(depth, QR, HEADS * (NOPE + ROPE)),
        "w_ukv": jnp.concatenate([kv[:, :, :QW].reshape(depth, KVR, HEADS, 128)[..., :NOPE],
                                  kv[:, :, QW:].reshape(depth, KVR, HEADS, VD)], axis=3).reshape(depth, KVR, -1),
        "b_sp": jnp.swapaxes(gb["b_sp_t"].reshape(depth, CHUNK, 4, 64).sum(axis=-1), 1, 2),
    }
    for n in ("w_out", "w_sp", "conv_w"):
        out[n] = gb[n]
    for n in ("mix_pre_g", "mix_post_g", "ffn_pre_g", "ffn_post_g", "q_norm_g", "kv_norm_g", "sg_ln_g", "sg_ln_b",
              "out_norm_g"):
        out[n] = gb[n][:, 0, :]
    return out


def _local_step(x, positions, target, small, mine, bufs, shard_shapes, fetch=True, front=None):
    depth = small["w_sp"].shape[0]
    tabs = _rope_tables(positions)
    ps = _small_params(small)
    saved, mixer_w = [], []
    for l in range(depth):
        mixer_w.append(_mixer_weight_params(_unpack_weights(bufs[0], l, shard_shapes)))
        p = {**ps, **mixer_w[l], **_ffn_views(bufs)}
        layers = [l + 1 if l + 1 < depth else None, l, l]
        x, sv, fetched = _layer_fwd(x, p, l, tabs, (mine, bufs, layers) if fetch else None)
        bufs = fetched or bufs
        saved.append(sv)
    dx, acc = _loss_head(x, target)
    loss = (0.5 / D) * jnp.sum(acc)
    gbs = [{} for _ in range(depth)]
    out = [{} for _ in range(depth)]
    ride = ()
    for l in reversed(range(depth)):
        ffn_front = (lambda gu, down, l=l: front[0](l, gu, down)) if front else None
        dx, out[l]["ffn"], out[l]["sent_ffn"], got = _layer_bwd(
            dx, {**ps, **mixer_w[l], **_ffn_views(bufs)}, saved[l], _Layer(l, 0), 1, gbs[l], tabs, ride, ffn_front)
        if ride:
            out[l + 1]["sent_mixer"], ride = got, ()
        grads = _natural_grads(gbs[l])
        if front:
            out[l]["mixer"] = front[1](l, grads)
            if l > 0:
                ride = tuple(out[l]["mixer"])
        else:
            out[l]["grads"] = (grads, gbs[l]["gu"], gbs[l]["down"])
    return loss, dx, out


def _place():
    x, y, c = lax.axis_index("x"), lax.axis_index("y"), lax.axis_index("c")
    chips = [(1 - x, y), (x, 1 - y), (1 - x, 1 - y)]
    return x, y, c, 2 * x + y, chips


def _remote(src, dst, send_sem, recv_sem, to):
    return pltpu.make_async_remote_copy(src_ref=src, dst_ref=dst, send_sem=send_sem, recv_sem=recv_sem, device_id=to,
                                        device_id_type=MESH_ID)


def _gather_ops(mine_refs, out_refs, sems, layers):
    send_sems, recv_sems, fsend_sems, frecv_sems = sems
    x, y, c, k, chips = _place()
    sib = (x, y, 1 - c)
    pairs = [(b, n) for n in range(3) for b in range(len(mine_refs)) if layers[b] is not None]

    def slot(n):
        return 2 * chips[n][0] + chips[n][1]

    def ici(b, n, dst_chip):
        return _remote(mine_refs[b].at[layers[b], c], out_refs[b].at[dst_chip, layers[b], c], send_sems.at[b, n],
                       recv_sems.at[b, n], (*chips[n], c))

    def d2d(b, n, half):
        piece = out_refs[b].at[slot(n), layers[b], half]
        return _remote(piece, piece, fsend_sems.at[b, n], frecv_sems.at[b, n], sib)

    def start():
        for b, n in pairs:
            ici(b, n, k).start()

    def hand_over():
        for b, n in pairs:
            ici(b, n, slot(n)).wait_recv()
            d2d(b, n, c).start()

    def drain():
        for b, n in pairs:
            d2d(b, n, 1 - c).wait_recv()
        for b, n in pairs:
            ici(b, n, k).wait_send()
            d2d(b, n, c).wait_send()

    return start, hand_over, drain


def _gather_first_layer(mine):
    nb = len(mine)

    def body(*refs):
        start, hand_over, drain = _gather_ops(refs[:nb], refs[nb:2 * nb], refs[2 * nb:], [0] + [None] * (nb - 1))
        start()
        hand_over()
        drain()

    return pl.pallas_call(
        body, name="gather_first_layer", in_specs=[ANY] * nb, out_specs=[ANY] * nb,
        out_shape=[_sds((4,) + a.shape, a.dtype) for a in mine],
        scratch_shapes=[pltpu.SemaphoreType.DMA((nb, 3))] * 4)(*mine)


def _swap_halves(bigs, wholes=()):
    nb, n = len(bigs), len(bigs) + len(wholes)

    def body(*refs):
        src, dst = refs[:n], refs[n:2 * n]
        send_sems, recv_sems = refs[2 * n:]
        x, y, c, _, _ = _place()
        sib = (x, y, 1 - c)
        cps = [_remote(src[b].at[:, 1 - c] if b < nb else src[b], dst[b], send_sems.at[b], recv_sems.at[b], sib)
               for b in range(n)]
        for cp in cps:
            cp.start()
        for cp in cps:
            cp.wait()

    return pl.pallas_call(
        body, name="swap_halves", in_specs=[ANY] * n, out_specs=[ANY] * n,
        out_shape=[_sds((4,) + a.shape[2:], a.dtype) for a in bigs] + [_sds(a.shape, a.dtype) for a in wholes],
        scratch_shapes=[pltpu.SemaphoreType.DMA((n,))] * 2)(*bigs, *wholes)


def _sum_tile(r):
    return max(cand for cand in range(16, 641, 16) if r % cand == 0)


def _pair_sum(big, rbig, c):
    _, _, r, w = big.shape
    tr = _sum_tile(r)

    def body(c_ref, big_ref, rbig_ref, p_ref):
        p_ref[...] = (big_ref[...].astype(F32) + rbig_ref[...].astype(F32)).astype(BF16)

    grid_spec = pltpu.PrefetchScalarGridSpec(
        num_scalar_prefetch=1, grid=(4, r // tr),
        in_specs=[pl.BlockSpec((None, None, tr, w), lambda j, i, cr: (j, cr[0], i, 0)),
                  pl.BlockSpec((None, tr, w), lambda j, i, cr: (j, i, 0))],
        out_specs=pl.BlockSpec((None, tr, w), lambda j, i, cr: (j, i, 0)))
    return pl.pallas_call(body, name="pair_sum", grid_spec=grid_spec, out_shape=_sds((4, r, w), BF16),
                          compiler_params=_cp("parallel", "parallel"))(c, big, rbig)


def _small_sum(parts):
    n, ns, _ = parts.shape

    def body(p_ref, o_ref):
        s = p_ref[0]
        for j in range(1, n):
            s = s + p_ref[j]
        o_ref[...] = s

    return pl.pallas_call(body, name="small_sum", out_shape=_sds((ns, 128), F32))(parts)


def _exchange_ops(p_refs, rb_refs, sems, small=None):
    send_sems, recv_sems = sems[0], sems[1]
    nb = len(p_refs)
    x, y, c, k, chips = _place()

    def copies(landing):
        out = []
        for n, (cx, cy) in enumerate(chips):
            to, kj = (cx, cy, c), 2 * cx + cy
            for b in range(nb):
                out.append(_remote(p_refs[b].at[k if landing else kj], rb_refs[b].at[n], send_sems.at[b, n],
                                   recv_sems.at[b, n], to))
            if small:
                out.append(_remote(small[0], small[1].at[kj if landing else k], send_sems.at[nb, n], recv_sems.at[nb, n], to))
        return out

    def local():
        return pltpu.make_async_copy(small[0], small[1].at[k], sems[2])

    def start():
        if small:
            local().start()
        for cp in copies(False):
            cp.start()

    def finish():
        for cp in copies(True):
            cp.wait_recv()
        for cp in copies(False):
            cp.wait_send()
        if small:
            local().wait()

    return start, finish


def _chip_exchange(ps, small):
    nb = len(ps)
    ns = small.shape[0]

    def body(*refs):
        start, finish = _exchange_ops(refs[:nb], refs[nb + 1:2 * nb + 1], refs[2 * nb + 2:], (refs[nb], refs[2 * nb + 1]))
        start()
        finish()

    return pl.pallas_call(
        body, name="chip_exchange", in_specs=[ANY] * (nb + 1), out_specs=[ANY] * (nb + 1),
        out_shape=[_sds((3,) + a.shape[1:], a.dtype) for a in ps] + [_sds((4, ns, 128), small.dtype)],
        scratch_shapes=[pltpu.SemaphoreType.DMA((nb + 1, 3))] * 2 + [pltpu.SemaphoreType.DMA(())])(*ps, small)


def _chip_sum(p, rb, chip):
    _, r, w = p.shape
    tr = _sum_tile(r)

    def body(k_ref, p_ref, rb_ref, o_ref):
        acc = p_ref[...].astype(F32)
        for j in range(3):
            acc = acc + rb_ref[j].astype(F32)
        o_ref[...] = acc

    grid_spec = pltpu.PrefetchScalarGridSpec(
        num_scalar_prefetch=1, grid=(r // tr,),
        in_specs=[pl.BlockSpec((None, tr, w), lambda i, kr: (kr[0], i, 0)), pl.BlockSpec((3, tr, w), lambda i, kr: (0, i, 0))],
        out_specs=pl.BlockSpec((tr, w), lambda i, kr: (i, 0)))
    return pl.pallas_call(body, name="chip_sum", grid_spec=grid_spec, out_shape=_sds((r, w), F32),
                          compiler_params=_cp("parallel"))(chip, p, rb)


def _send_to_sibling(reds):
    nb = len(reds)

    def body(*refs):
        red_refs, out_refs = refs[:nb], refs[nb:2 * nb]
        send_sems, recv_sems = refs[2 * nb:]
        x, y, c, _, _ = _place()
        cps = [_remote(red_refs[b], out_refs[b], send_sems.at[b], recv_sems.at[b], (x, y, 1 - c)) for b in range(nb)]
        for cp in cps:
            cp.start()
        for cp in cps:
            cp.wait()

    return pl.pallas_call(
        body, name="send_to_sibling", in_specs=[ANY] * nb, out_specs=[ANY] * nb,
        out_shape=[_sds(a.shape, a.dtype) for a in reds], scratch_shapes=[pltpu.SemaphoreType.DMA((nb,))] * 2)(*reds)


def _adam_math(w, g, m, v):
    nm = ADAM_B1 * m + (1.0 - ADAM_B1) * g
    nv = ADAM_B2 * v + (1.0 - ADAM_B2) * (g * g)
    m_hat = nm / (1.0 - ADAM_B1 ** ADAM_STEP)
    v_hat = nv / (1.0 - ADAM_B2 ** ADAM_STEP)
    return -ADAM_LR * (m_hat / (jnp.sqrt(v_hat) + ADAM_EPS) + ADAM_WD * w), nm, nv


def _adamw_shard(w, m, v, srcs, c, name, owner=0, split=None):
    depth, r, n = w.shape
    unit = math.gcd(split, r - split) if split else r
    tr = max(cand for cand in range(8, min(unit, 256) + 1, 8) if unit % cand == 0)
    sb = split // tr if split else None
    npad = srcs[0][0].shape[-1]

    def body(c_ref, w_ref, m_ref, v_ref, *refs):
        g_ref, d_ref, nm_ref, nv_ref = refs[2 * depth:]
        l, i = pl.program_id(0), pl.program_id(1)
        half = (i >= sb).astype(jnp.int32) if split else owner
        mine = c_ref[0] == half
        g = jnp.where(mine, refs[0][...], refs[1][...])
        for b in range(1, depth):
            g = jnp.where(l == b, jnp.where(mine, refs[2 * b][...], refs[2 * b + 1][...]), g)
        g = g[:, :n]
        g_ref[...] = g
        d_ref[...], nm_ref[...], nv_ref[...] = _adam_math(w_ref[...], g, m_ref[...], v_ref[...])

    def source(b):
        def index(l, i, cr):
            blk = jnp.where(i >= sb, i - sb, i) if split else i
            return (jnp.where(l == b, blk, 0), 0)
        return pl.BlockSpec((tr, npad), index)

    blk = pl.BlockSpec((None, tr, n), lambda l, i, cr: (l, i, 0))
    grid_spec = pltpu.PrefetchScalarGridSpec(
        num_scalar_prefetch=1, grid=(depth, r // tr),
        in_specs=[blk] * 3 + [source(b) for b in range(depth) for _ in range(2)], out_specs=[blk] * 4)
    return pl.pallas_call(body, name=name, grid_spec=grid_spec, out_shape=[_sds(w.shape, F32)] * 4,
                          compiler_params=_cp("parallel", "parallel"))(c, w, m, v, *[a for pair in srcs for a in pair])


def _pad_ffn_shards(w_gate, w_up, w_down):
    depth = w_gate.shape[0]
    hr = HP // 2

    def gu_body(g_ref, u_ref, o_ref):
        for which, ref in enumerate((g_ref, u_ref)):
            o_ref[which, 0:HS, :] = ref[...].astype(BF16)
            o_ref[which, HS:HP, :] = jnp.zeros((HP - HS, D), BF16)

    blk = pl.BlockSpec((None, HS, D), lambda l: (l, 0, 0))
    gu = pl.pallas_call(
        gu_body, name="pad_gate_up", grid=(depth,), in_specs=[blk, blk],
        out_specs=pl.BlockSpec((None, 2, HP, D), lambda l: (l, 0, 0, 0)),
        out_shape=_sds((depth, 2, HP, D), BF16), compiler_params=_cp("parallel"))(w_gate, w_up)

    def down_body(w_ref, o_ref):
        o_ref[0] = w_ref[0:hr, :].astype(BF16)
        o_ref[1, 0:HS - hr, :] = w_ref[hr:HS, :].astype(BF16)
        o_ref[1, HS - hr:hr, :] = jnp.zeros((HP - HS, D), BF16)

    down = pl.pallas_call(
        down_body, name="pad_down", grid=(depth,), in_specs=[pl.BlockSpec((None, HS, D), lambda l: (l, 0, 0))],
        out_specs=pl.BlockSpec((None, 2, hr, D), lambda l: (l, 0, 0, 0)),
        out_shape=_sds((depth, 2, hr, D), BF16), compiler_params=_cp("parallel"))(w_down)
    return gu, down


def _adamw_rows(w, g, m, v, name):
    r, n = w.shape
    tr = max(cand for cand in range(8, 513, 8) if r % cand == 0)

    def body(w_ref, g_ref, m_ref, v_ref, d_ref, nm_ref, nv_ref):
        d_ref[...], nm_ref[...], nv_ref[...] = _adam_math(w_ref[...], g_ref[...], m_ref[...], v_ref[...])

    blk = pl.BlockSpec((tr, n), lambda i: (i, 0))
    return pl.pallas_call(body, name=name, grid=(r // tr,), in_specs=[blk] * 4, out_specs=[blk] * 3,
                          out_shape=[_sds(w.shape, F32)] * 3, compiler_params=_cp("parallel"))(w, g, m, v)


def _to_pack(a, name):
    depth = a.shape[0]
    if name in ROW_SHARDED:
        return jnp.swapaxes(a.reshape(depth, 4, -1, D), 0, 1)
    return jnp.transpose(a.reshape(depth, a.shape[1], 4, a.shape[2] // 4), (2, 0, 1, 3)).reshape(4, depth, -1, D)


def _pack_rows(parts, lead, dtype, tail=None):
    pieces, at = [], 0
    for n, off, rows in PACK:
        if off > at:
            pieces.append(jnp.zeros(lead + (off - at, D), dtype))
        pieces.append(parts[n].astype(dtype))
        at = off + rows
    if tail is not None:
        pieces.append(tail)
        at += tail.shape[-2]
    pieces.append(jnp.zeros(lead + (PACK_ROWS - at, D), dtype))
    return jnp.concatenate(pieces, axis=len(lead))


def _pack_weight_shards(sh):
    depth = sh["w_in"].shape[0]
    parts = {n: sh[n].reshape(depth, rows, D) for n, _, rows in PACK}
    parts["w_in"] = jnp.swapaxes(sh["w_in"], 1, 2)
    conv = lax.bitcast_convert_type(sh["conv_w"].reshape(depth, 3 * 64), BF16).reshape(depth, 1, 384)
    flat = _pack_rows(parts, (depth,), BF16, tail=jnp.pad(conv, ((0, 0), (0, 0), (0, D - 384))))
    return flat.reshape(depth, 2, PACK_ROWS // 2, D)


def _unpack_weights(gathered, l, shard_shapes):
    depth = 1
    flat = gathered[:, l].reshape(4, 1, PACK_ROWS, D)
    full = {}
    for n, off, rows in PACK:
        if n == "w_in":
            full["w_in_t"] = jnp.swapaxes(flat[:, :, off:off + rows, :], 0, 1).reshape(depth, 4 * rows, D)
            continue
        shp = shard_shapes[n][1:]
        piece = flat[:, :, off:off + rows, :].reshape((4, depth) + shp)
        if n in ROW_SHARDED:
            full[n] = jnp.transpose(piece, (1, 0, 2, 3)).reshape(depth, 4 * shp[0], shp[1])
        else:
            full[n] = jnp.transpose(piece, (1, 2, 0, 3)).reshape(depth, shp[0], 4 * shp[1])
    conv = lax.bitcast_convert_type(flat[:, :, CONV_ROW, :384].reshape(4, depth, 192, 2), F32)
    full["conv_w"] = jnp.transpose(conv.reshape(4, depth, 3, 64), (1, 2, 0, 3)).reshape(depth, 3, CVW)
    return full


def _pack_grad_shards(g):
    depth = g["w_in_t"].shape[0]
    parts = {n: _to_pack(g[n], n) for n, _, _ in PACK if n != "w_in"}
    parts["w_in"] = jnp.swapaxes(g["w_in_t"].reshape(depth, 4, -1, D), 0, 1)
    return _pack_rows(parts, (4, depth), BF16)


def _pack_small(arrs, names_shapes, depth):
    flat = jnp.concatenate([arrs[n].reshape(depth, -1) for n, _ in names_shapes], axis=1).reshape(-1)
    rows = -(-flat.shape[0] // 1024) * 8
    return jnp.pad(flat, (0, rows * 128 - flat.shape[0])).reshape(rows, 128)


def _unpack_small(packed, names_shapes, depth):
    per_layer = sum(math.prod(s) for _, s in names_shapes)
    flat = packed.reshape(-1)[:depth * per_layer].reshape(depth, per_layer)
    out, off = {}, 0
    for n, s in names_shapes:
        size = math.prod(s)
        out[n] = flat[:, off:off + size].reshape((depth,) + s)
        off += size
    return out


def kernel(x, positions, mix_pre_g, mix_post_g, ffn_pre_g, ffn_post_g, w_in, q_norm_g, w_uq, kv_norm_g, w_ukv, sg_ln_g, sg_ln_b, w_sp, b_sp, conv_w, out_norm_g, w_out, w_gate, w_up, w_down, loss_target, m_mix_pre_g, m_mix_post_g, m_ffn_pre_g, m_ffn_post_g, m_w_in, m_q_norm_g, m_w_uq, m_kv_norm_g, m_w_ukv, m_sg_ln_g, m_sg_ln_b, m_w_sp, m_b_sp, m_conv_w, m_out_norm_g, m_w_out, m_w_gate, m_w_up, m_w_down, v_mix_pre_g, v_mix_post_g, v_ffn_pre_g, v_ffn_post_g, v_w_in, v_q_norm_g, v_w_uq, v_kv_norm_g, v_w_ukv, v_sg_ln_g, v_sg_ln_b, v_w_sp, v_b_sp, v_conv_w, v_out_norm_g, v_w_out, v_w_gate, v_w_up, v_w_down):
    w = dict(mix_pre_g=mix_pre_g, mix_post_g=mix_post_g, ffn_pre_g=ffn_pre_g, ffn_post_g=ffn_post_g, w_in=w_in,
             q_norm_g=q_norm_g, w_uq=w_uq, kv_norm_g=kv_norm_g, w_ukv=w_ukv, sg_ln_g=sg_ln_g, sg_ln_b=sg_ln_b, w_sp=w_sp,
             b_sp=b_sp, conv_w=conv_w, out_norm_g=out_norm_g, w_out=w_out, w_gate=w_gate, w_up=w_up, w_down=w_down)
    m = dict(mix_pre_g=m_mix_pre_g, mix_post_g=m_mix_post_g, ffn_pre_g=m_ffn_pre_g, ffn_post_g=m_ffn_post_g, w_in=m_w_in,
             q_norm_g=m_q_norm_g, w_uq=m_w_uq, kv_norm_g=m_kv_norm_g, w_ukv=m_w_ukv, sg_ln_g=m_sg_ln_g, sg_ln_b=m_sg_ln_b,
             w_sp=m_w_sp, b_sp=m_b_sp, conv_w=m_conv_w, out_norm_g=m_out_norm_g, w_out=m_w_out, w_gate=m_w_gate,
             w_up=m_w_up, w_down=m_w_down)
    v = dict(mix_pre_g=v_mix_pre_g, mix_post_g=v_mix_post_g, ffn_pre_g=v_ffn_pre_g, ffn_post_g=v_ffn_post_g, w_in=v_w_in,
             q_norm_g=v_q_norm_g, w_uq=v_w_uq, kv_norm_g=v_kv_norm_g, w_ukv=v_w_ukv, sg_ln_g=v_sg_ln_g, sg_ln_b=v_sg_ln_b,
             w_sp=v_w_sp, b_sp=v_b_sp, conv_w=v_conv_w, out_norm_g=v_out_norm_g, w_out=v_w_out, w_gate=v_w_gate,
             w_up=v_w_up, w_down=v_w_down)
    depth = w_in.shape[0]
    c = lax.axis_index("c").astype(jnp.int32).reshape(1)
    chip = (2 * lax.axis_index("x") + lax.axis_index("y")).astype(jnp.int32)

    mine = [_pack_weight_shards(w), *_pad_ffn_shards(jnp.swapaxes(w_gate, 1, 2), jnp.swapaxes(w_up, 1, 2), w_down)]
    bufs = [lax.dynamic_update_slice(g, a[None], (chip,) + (0,) * a.ndim)
            for g, a in zip(_gather_first_layer(mine), mine)]

    small_grads = [None] * depth
    small_pair = []

    def ffn_front(l, g_gu, g_down):
        bigs = [g_gu.reshape(4, 2, HP, D), g_down.reshape(4, 2, HP // 2, D)]
        return [_pair_sum(a, r, c) for a, r in zip(bigs, _swap_halves(bigs))]

    def mixer_front(l, grads):
        small_grads[l] = grads
        bigs = [_pack_grad_shards(grads).reshape(4, 2, PACK_ROWS // 2, D)]
        if l > 0:
            rbigs = _swap_halves(bigs)
        else:
            small = _pack_small({n: jnp.concatenate([g[n] for g in small_grads]) for n, _ in SMALL}, SMALL, depth)
            *rbigs, rsmall = _swap_halves(bigs, [small])
            small_pair.append(_small_sum(jnp.stack([small, rsmall])))
        return [_pair_sum(a, r, c) for a, r in zip(bigs, rbigs)]

    loss, dx, red = _local_step(x[0], positions[0], loss_target[0], w, mine, bufs,
                                {n: w[n].shape for n, _, _ in PACK}, front=(ffn_front, mixer_front))
    loss = lax.psum(loss, ("x", "y", "c"))

    *red[0]["sent_mixer"], rs = _chip_exchange(red[0]["mixer"], small_pair[0])
    own = [[_chip_sum(p, rb, chip.reshape(1))
            for p, rb in zip(r["mixer"] + r["ffn"], list(r["sent_mixer"]) + list(r["sent_ffn"]))] for r in red]
    flat = _send_to_sibling([a for o in own for a in o])
    other = [flat[3 * l:3 * l + 3] for l in range(depth)]
    g_small = _unpack_small(_small_sum(rs), SMALL, depth)
    g_small["conv_w"] = lax.dynamic_slice_in_dim(g_small["conv_w"], chip * 64, 64, axis=2)

    gw, delta, new_m, new_v = dict(g_small), {}, {}, {}

    def adam(n, srcs, turned=False, **where):
        view = (lambda a: jnp.swapaxes(a, 1, 2)) if turned else (lambda a: a)
        outs = _adamw_shard(view(w[n]), view(m[n]), view(v[n]), srcs, c, "adamw_" + n, **where)
        gw[n], delta[n], new_m[n], new_v[n] = [view(o) for o in outs]

    first = c[0] == 0
    packs = [jnp.concatenate([jnp.where(first, o[0], s[0]), jnp.where(first, s[0], o[0])]) for o, s in zip(own, other)]
    for n, off, rows in PACK[1:]:
        pieces = [pk[off:off + rows, :].reshape(w[n].shape[1:]) for pk in packs]
        adam(n, [(pc, pc) for pc in pieces], owner=0)
    rows_in = PACK[0][2]
    turn = lambda a: jnp.swapaxes(a, 1, 2).reshape(depth * rows_in, D)
    back = lambda a: jnp.swapaxes(a.reshape(depth, rows_in, D), 1, 2)
    g_in = jnp.concatenate([pk[:rows_in, :] for pk in packs])
    outs = _adamw_rows(turn(w["w_in"]), g_in, turn(m["w_in"]), turn(v["w_in"]), "adamw_w_in")
    gw["w_in"], delta["w_in"], new_m["w_in"], new_v["w_in"] = [back(a) for a in (g_in, *outs)]
    adam("w_gate", [(o[1], s[1]) for o, s in zip(own, other)], turned=True, owner=0)
    adam("w_up", [(o[1], s[1]) for o, s in zip(own, other)], turned=True, owner=1)
    adam("w_down", [(o[2], s[2]) for o, s in zip(own, other)], split=HP // 2)
    small_local = tuple((n, w[n].shape[1:]) for n, _ in SMALL)
    d_, m_, v_ = _adamw_rows(_pack_small(w, small_local, depth), _pack_small(gw, small_local, depth),
                             _pack_small(m, small_local, depth), _pack_small(v, small_local, depth), "adamw_small")
    delta.update(_unpack_small(d_, small_local, depth))
    new_m.update(_unpack_small(m_, small_local, depth))
    new_v.update(_unpack_small(v_, small_local, depth))

    return (loss, dx[None], *[gw[n] for n in WEIGHTS], *[delta[n] for n in WEIGHTS], *[new_m[n] for n in WEIGHTS],
            *[new_v[n] for n in WEIGHTS])
```
